```python
import math
import jax
import jax.numpy as jnp
from jax import lax
import numpy as np

D_MODEL = 1024
BATCH = 4
SEQ = 4096
DEPTH = 2

EPS = 1e-6
HEAD_DIM = 64
A_Q_HEADS = 8
A_KV_HEADS = 2
A_GROUP = A_Q_HEADS // A_KV_HEADS
WINDOW = 128
N_BUCKETS = 32
MAX_DISTANCE = 128
B_HEADS = 8
B_DK = 64
B_DV = 64
B_CONV = 4
CHUNK = 64
A_Q_W = A_Q_HEADS * HEAD_DIM
A_KV_W = A_KV_HEADS * HEAD_DIM
B_QK_W = B_HEADS * B_DK
B_V_W = B_HEADS * B_DV
B_QKV_W = 2 * B_QK_W + B_V_W
AB_SIZES = (A_Q_W, A_KV_W, A_KV_W, B_QKV_W, B_V_W, B_HEADS, B_HEADS)
AB_IN = sum(AB_SIZES)
AB_OUT = A_Q_W + B_V_W
LRU_WIDTH = D_MODEL
LRU_BLOCKS = 8
LRU_BW = LRU_WIDTH // LRU_BLOCKS
LRU_CONV = 4
LRU_C = 8.0
SC_WIDTH = D_MODEL // 2
SC_CONV = 3
CD_SIZES = (LRU_WIDTH, LRU_WIDTH, SC_WIDTH, SC_WIDTH, SC_WIDTH)
CD_IN = sum(CD_SIZES)
CD_OUT = LRU_WIDTH + SC_WIDTH
D_FF = 2816
N_EXPERTS = 8
TOP_K = 2
D_FF_EXPERT = 3584
N_EVEN = (DEPTH + 1) // 2
N_ODD = DEPTH // 2

kernel_name = 'hybrid_swa_deltanet_rglru_shortconv_moe'


def split_cols(t, sizes):
    return jnp.split(t, [int(s) for s in np.cumsum(sizes)[:-1]], axis=-1)


def rmsnorm(x, w):
    xf = x.astype(jnp.float32)
    y = xf * lax.rsqrt(jnp.mean(xf * xf, axis=-1, keepdims=True) + EPS)
    return (y * w.astype(jnp.float32)).astype(x.dtype)


def l2norm(x):
    xf = x.astype(jnp.float32)
    return xf * lax.rsqrt(jnp.sum(xf * xf, axis=-1, keepdims=True) + EPS)


def causal_dwconv(x, w, b=None):
    K = w.shape[0]
    T = x.shape[1]
    xp = jnp.pad(x, ((0, 0), (K - 1, 0), (0, 0)))
    y = xp[:, 0:T] * w[0]
    for k in range(1, K):
        y = y + xp[:, k:k + T] * w[k]
    if b is not None:
        y = y + b
    return y


def t5_causal_bucket(dist):
    max_exact = N_BUCKETS // 2
    d = np.maximum(dist, 0)
    large = max_exact + (np.log(np.maximum(d, 1) / max_exact) / math.log(MAX_DISTANCE / max_exact)
                         * (N_BUCKETS - max_exact)).astype(np.int32)
    large = np.minimum(large, N_BUCKETS - 1)
    return np.where(d < max_exact, d, large).astype(np.int32)


def swa_sink_attention(q, k, v, sinks, rel_bias):
    Bsz, T = q.shape[0], q.shape[1]
    nb = T // WINDOW
    qb = q.reshape(Bsz, nb, WINDOW, A_KV_HEADS, A_GROUP, HEAD_DIM)

    def band(t):
        tb = t.reshape(Bsz, nb, WINDOW, A_KV_HEADS, HEAD_DIM)
        prev = jnp.pad(tb[:, :-1], ((0, 0), (1, 0), (0, 0), (0, 0), (0, 0)))
        return jnp.concatenate([prev, tb], axis=2)

    kb, vb = band(k), band(v)
    logits = jnp.einsum('bnqkgd,bnskd->bnkgqs', qb, kb,
                        preferred_element_type=jnp.float32) * (HEAD_DIM ** -0.5)
    qi = np.arange(WINDOW)[:, None]
    s = np.arange(2 * WINDOW)[None, :]
    dist = qi + WINDOW - s
    in_window = (dist >= 0) & (dist < WINDOW)
    bias = rel_bias[t5_causal_bucket(dist)].astype(jnp.float32)
    bias = jnp.transpose(bias, (2, 0, 1)).reshape(A_KV_HEADS, A_GROUP, WINDOW, 2 * WINDOW)
    key_valid = (np.arange(nb)[:, None] * WINDOW - WINDOW + s) >= 0
    mask = in_window[None] & key_valid[:, None, :]
    logits = jnp.where(mask[None, :, None, None], logits + bias, -jnp.inf)
    sink = sinks.astype(jnp.float32).reshape(A_KV_HEADS, A_GROUP)[None, None, :, :, None, None]
    m = jnp.maximum(jnp.max(logits, axis=-1, keepdims=True), sink)
    p = jnp.exp(logits - m)
    p = p / (jnp.sum(p, axis=-1, keepdims=True) + jnp.exp(sink - m))
    out = jnp.einsum('bnkgqs,bnskd->bnqkgd', p.astype(v.dtype), vb)
    return out.reshape(Bsz, T, A_Q_W)


def gated_delta_rule(q, k, v, g, beta):
    Bsz, T, H, dk = q.shape
    dv = v.shape[-1]
    n = T // CHUNK

    def chunks(t):
        t = t.astype(jnp.float32).reshape((Bsz, n, CHUNK, H) + t.shape[3:])
        return jnp.moveaxis(t, 3, 1)

    q, k, v, g, beta = chunks(q), chunks(k), chunks(v), chunks(g), chunks(beta)
    q = q * (dk ** -0.5)
    g = jnp.cumsum(g, axis=-1)
    causal = np.tril(np.ones((CHUNK, CHUNK), dtype=bool))
    strict = np.tril(np.ones((CHUNK, CHUNK), dtype=bool), -1)
    decay = jnp.exp(jnp.where(causal, g[..., :, None] - g[..., None, :], -jnp.inf))
    kk = jnp.einsum('bhncd,bhnsd->bhncs', k, k)
    L = jnp.where(strict, beta[..., None] * kk * decay, 0.0)
    eye = jnp.eye(CHUNK, dtype=jnp.float32)
    Tm = lax.linalg.triangular_solve(eye + L, jnp.broadcast_to(eye, L.shape),
                                     left_side=True, lower=True, unit_diagonal=True)
    u = Tm @ (v * beta[..., None])
    w = Tm @ (k * (beta * jnp.exp(g))[..., None])
    qk = jnp.where(causal, jnp.einsum('bhncd,bhnsd->bhncs', q, k) * decay, 0.0)
    g_last = g[..., -1:]
    k_dec = k * jnp.exp(g_last - g)[..., None]
    q_dec = q * jnp.exp(g)[..., None]

    def step(S, xs):
        q_i, k_i, u_i, w_i, qk_i, gl_i = xs
        v_new = u_i - jnp.einsum('bhcd,bhde->bhce', w_i, S)
        o = jnp.einsum('bhcd,bhde->bhce', q_i, S) + jnp.einsum('bhcs,bhse->bhce', qk_i, v_new)
        S = S * jnp.exp(gl_i)[..., None] + jnp.einsum('bhcd,bhce->bhde', k_i, v_new)
        return S, o

    xs = tuple(jnp.moveaxis(t, 2, 0) for t in (q_dec, k_dec, u, w, qk, g_last))
    S0 = jnp.zeros((Bsz, H, dk, dv), jnp.float32)
    _, o = lax.scan(step, S0, xs)
    return jnp.transpose(o, (1, 0, 3, 2, 4)).reshape(Bsz, T, H, dv)


def mixer_ab(h, w_in, sinks, conv_w, a_log, dt_bias, norm_w, w_out, rel_bias):
    Bsz, T, _ = h.shape
    qa, ka, va, qkv_b, gate_b, beta_b, a_b = split_cols(h @ w_in, AB_SIZES)
    attn = swa_sink_attention(qa.reshape(Bsz, T, A_Q_HEADS, HEAD_DIM),
                              ka.reshape(Bsz, T, A_KV_HEADS, HEAD_DIM),
                              va.reshape(Bsz, T, A_KV_HEADS, HEAD_DIM), sinks, rel_bias)
    qkv_b = jax.nn.silu(causal_dwconv(qkv_b, conv_w))
    qb, kb, vb = split_cols(qkv_b, (B_QK_W, B_QK_W, B_V_W))
    qb = l2norm(qb.reshape(Bsz, T, B_HEADS, B_DK))
    kb = l2norm(kb.reshape(Bsz, T, B_HEADS, B_DK))
    vb = vb.reshape(Bsz, T, B_HEADS, B_DV)
    beta = jax.nn.sigmoid(beta_b.astype(jnp.float32))
    g = -jnp.exp(a_log.astype(jnp.float32)) * jax.nn.softplus(a_b.astype(jnp.float32) + dt_bias.astype(jnp.float32))
    o = gated_delta_rule(qb, kb, vb, g, beta)
    o = rmsnorm(o, norm_w) * jax.nn.silu(gate_b.astype(jnp.float32).reshape(Bsz, T, B_HEADS, B_DV))
    o = o.reshape(Bsz, T, B_V_W).astype(h.dtype)
    return jnp.concatenate([attn, o], axis=-1) @ w_out


def lru_combine(left, right):
    a1, b1 = left
    a2, b2 = right
    return a1 * a2, a2 * b1 + b2


def mixer_cd(h, w_in, conv_w, conv_b, gate_a_w, gate_a_b, gate_x_w, gate_x_b, lam, sconv_w, w_out):
    Bsz, T, _ = h.shape
    xc, yc, bd, cd, hd = split_cols(h @ w_in, CD_SIZES)
    xc = causal_dwconv(xc, conv_w, conv_b)
    xblk = xc.reshape(Bsz, T, LRU_BLOCKS, LRU_BW)
    r = jax.nn.sigmoid((jnp.einsum('btni,nij->btnj', xblk, gate_a_w).reshape(Bsz, T, LRU_WIDTH)
                        + gate_a_b).astype(jnp.float32))
    i = jax.nn.sigmoid((jnp.einsum('btni,nij->btnj', xblk, gate_x_w).reshape(Bsz, T, LRU_WIDTH)
                        + gate_x_b).astype(jnp.float32))
    log_a = -LRU_C * r * jax.nn.softplus(-lam.astype(jnp.float32))
    a = jnp.exp(log_a)
    b = jnp.sqrt(-jnp.expm1(2.0 * log_a)) * (i * xc.astype(jnp.float32))
    _, hs = lax.associative_scan(lru_combine, (a, b), axis=1)
    yc_out = hs.astype(h.dtype) * jax.nn.gelu(yc)
    yd_out = bd * causal_dwconv(cd * hd, sconv_w)
    return jnp.concatenate([yc_out, yd_out], axis=-1) @ w_out


def swiglu(h, wg, wu, wd):
    return (jax.nn.silu(h @ wg) * (h @ wu)) @ wd


def moe_swiglu(h, router_w, router_b, wg, wu, wd):
    logits = (h @ router_w).astype(jnp.float32) + router_b.astype(jnp.float32)
    top_val, top_idx = lax.top_k(logits, TOP_K)
    top_w = jax.nn.softmax(top_val, axis=-1)
    gates = jnp.sum(jax.nn.one_hot(top_idx, N_EXPERTS, dtype=jnp.float32) * top_w[..., None], axis=-2)
    gates = gates.astype(h.dtype)
    out = jnp.zeros_like(h)
    for e in range(N_EXPERTS):
        out = out + gates[..., e:e + 1] * swiglu(h, wg[e], wu[e], wd[e])
    return out


def setup_inputs(seed: int = 0) -> dict:
    key = jax.random.key(seed)
    ks = jax.random.split(key, 33)
    D = D_MODEL

    def nrm(k, shape, scale):
        return jax.random.normal(k, shape, jnp.float32) * scale

    def gain(k, shape):
        return 1.0 + nrm(k, shape, 0.02)

    dt = jnp.exp(jax.random.uniform(ks[12], (N_EVEN, B_HEADS), jnp.float32, math.log(1e-3), math.log(1e-1)))
    u = jax.random.uniform(ks[25], (N_ODD, LRU_WIDTH), jnp.float32, 0.9, 0.999)
    p = u ** (1.0 / LRU_C)
    return {
        'x': nrm(ks[0], (BATCH, SEQ, D), 1.0),
        'c': nrm(ks[1], (BATCH, D), 1.0),
        'rel_bias': nrm(ks[2], (N_BUCKETS, A_Q_HEADS), 0.5),
        'ada_w': nrm(ks[3], (DEPTH, D, 6 * D), 0.5 * D ** -0.5),
        'ada_b': nrm(ks[4], (DEPTH, 6 * D), 0.02),
        'norm_mix_w': gain(ks[5], (DEPTH, D)),
        'norm_ffn_w': gain(ks[6], (DEPTH, D)),
        'final_norm_w': gain(ks[7], (D,)),
        'ab_w_in': nrm(ks[8], (N_EVEN, D, AB_IN), D ** -0.5),
        'attn_sinks': nrm(ks[9], (N_EVEN, A_Q_HEADS), 0.5),
        'dn_conv_w': nrm(ks[10], (N_EVEN, B_CONV, B_QKV_W), B_CONV ** -0.5),
        'dn_a_log': jnp.log(jax.random.uniform(ks[11], (N_EVEN, B_HEADS), jnp.float32, 1.0, 16.0)),
        'dn_dt_bias': dt + jnp.log(-jnp.expm1(-dt)),
        'dn_norm_w': gain(ks[13], (N_EVEN, B_DV)),
        'ab_w_out': nrm(ks[14], (N_EVEN, AB_OUT, D), AB_OUT ** -0.5),
        'ffn_w_gate': nrm(ks[15], (N_EVEN, D, D_FF), D ** -0.5),
        'ffn_w_up': nrm(ks[16], (N_EVEN, D, D_FF), D ** -0.5),
        'ffn_w_down': nrm(ks[17], (N_EVEN, D_FF, D), D_FF ** -0.5),
        'cd_w_in': nrm(ks[18], (N_ODD, D, CD_IN), D ** -0.5),
        'lru_conv_w': nrm(ks[19], (N_ODD, LRU_CONV, LRU_WIDTH), LRU_CONV ** -0.5),
        'lru_conv_b': nrm(ks[20], (N_ODD, LRU_WIDTH), 0.01),
        'lru_gate_a_w': nrm(ks[21], (N_ODD, LRU_BLOCKS, LRU_BW, LRU_BW), LRU_BW ** -0.5),
        'lru_gate_a_b': nrm(ks[22], (N_ODD, LRU_WIDTH), 0.01),
        'lru_gate_x_w': nrm(ks[23], (N_ODD, LRU_BLOCKS, LRU_BW, LRU_BW), LRU_BW ** -0.5),
        'lru_gate_x_b': nrm(ks[24], (N_ODD, LRU_WIDTH), 0.01),
        'lru_lambda': jnp.log(p) - jnp.log1p(-p),
        'sconv_w': nrm(ks[26], (N_ODD, SC_CONV, SC_WIDTH), SC_CONV ** -0.5),
        'cd_w_out': nrm(ks[27], (N_ODD, CD_OUT, D), CD_OUT ** -0.5),
        'moe_router_w': nrm(ks[28], (N_ODD, D, N_EXPERTS), D ** -0.5),
        'moe_router_b': nrm(ks[29], (N_ODD, N_EXPERTS), 0.01),
        'moe_w_gate': nrm(ks[30], (N_ODD, N_EXPERTS, D, D_FF_EXPERT), D ** -0.5),
        'moe_w_up': nrm(ks[31], (N_ODD, N_EXPERTS, D, D_FF_EXPERT), D ** -0.5),
        'moe_w_down': nrm(ks[32], (N_ODD, N_EXPERTS, D_FF_EXPERT, D), D_FF_EXPERT ** -0.5),
    }


def reference(x, c, rel_bias, ada_w, ada_b, norm_mix_w, norm_ffn_w, final_norm_w,
              ab_w_in, attn_sinks, dn_conv_w, dn_a_log, dn_dt_bias, dn_norm_w, ab_w_out,
              ffn_w_gate, ffn_w_up, ffn_w_down,
              cd_w_in, lru_conv_w, lru_conv_b, lru_gate_a_w, lru_gate_a_b, lru_gate_x_w, lru_gate_x_b,
              lru_lambda, sconv_w, cd_w_out, moe_router_w, moe_router_b, moe_w_gate, moe_w_up, moe_w_down):
    cond = jax.nn.silu(c)
    for l in range(DEPTH):
        sh1, sc1, g1, sh2, sc2, g2 = jnp.split(cond @ ada_w[l] + ada_b[l], 6, axis=-1)
        hn = rmsnorm(x, norm_mix_w[l]) * (1.0 + sc1[:, None]) + sh1[:, None]
        if l % 2 == 0:
            e = l // 2
            mix = mixer_ab(hn, ab_w_in[e], attn_sinks[e], dn_conv_w[e], dn_a_log[e], dn_dt_bias[e],
                           dn_norm_w[e], ab_w_out[e], rel_bias)
        else:
            o = l // 2
            mix = mixer_cd(hn, cd_w_in[o], lru_conv_w[o], lru_conv_b[o], lru_gate_a_w[o], lru_gate_a_b[o],
                           lru_gate_x_w[o], lru_gate_x_b[o], lru_lambda[o], sconv_w[o], cd_w_out[o])
        x = x + g1[:, None] * mix
        hn = rmsnorm(x, norm_ffn_w[l]) * (1.0 + sc2[:, None]) + sh2[:, None]
        if l % 2 == 0:
            e = l // 2
            ffn = swiglu(hn, ffn_w_gate[e], ffn_w_up[e], ffn_w_down[e])
        else:
            o = l // 2
            ffn = moe_swiglu(hn, moe_router_w[o], moe_router_b[o], moe_w_gate[o], moe_w_up[o], moe_w_down[o])
        x = x + g2[:, None] * ffn
    return rmsnorm(x, final_norm_w)
```

```python
import functools
import math

import numpy as np
import jax
import jax.numpy as jnp
from jax import lax
from jax.experimental import pallas as pl
from jax.experimental.pallas import tpu as pltpu

D_MODEL = 1024
EPS = 1e-6
HEAD_DIM = 64
A_Q_HEADS = 8
A_KV_HEADS = 2
WINDOW = 128
N_BUCKETS = 32
MAX_DISTANCE = 128
B_HEADS = 8
B_CONV = 4
CHUNK = 64
A_Q_W = A_Q_HEADS * HEAD_DIM
A_KV_W = A_KV_HEADS * HEAD_DIM
B_W = B_HEADS * HEAD_DIM
B_QKV_W = 3 * B_W
LRU_WIDTH = D_MODEL
LRU_BLOCKS = 8
LRU_C = 8.0
SC_WIDTH = D_MODEL // 2
D_FF = 2816
N_EXPERTS = 8
D_FF_EXPERT = 3584

LANES = 128
SUBLANES = 8
VMEM_LIMIT_BYTES = 56 * 1024 * 1024
TOKEN_TILE = 512
MOE_TILE = 1024
MOE_SUB = 256
MOE_FF_TILE = 512
ROW_DMA_TILE = 256
NEG_BIG = -1e30

F32 = jnp.float32
BF16 = jnp.bfloat16


def _cparams(*sem):
    return pltpu.CompilerParams(dimension_semantics=tuple(sem), vmem_limit_bytes=VMEM_LIMIT_BYTES)


def _const_spec(shape):
    nd = len(shape)
    return pl.BlockSpec(shape, lambda *_: (0,) * nd)


def _bdot(a, b):
    return jnp.dot(a.astype(BF16), b.astype(BF16), preferred_element_type=F32)


def _bdot_nt(a, b):
    return lax.dot_general(a.astype(BF16), b.astype(BF16), (((1,), (1,)), ((), ())),
                           preferred_element_type=F32)


def _bdot_tn(a, b):
    return lax.dot_general(a.astype(BF16), b.astype(BF16), (((0,), (0,)), ((), ())),
                           preferred_element_type=F32)


def _split(x, n):
    parts = []
    r = x
    for i in range(n):
        p = r.astype(BF16)
        parts.append(p)
        if i + 1 < n:
            r = r - p.astype(F32)
    return parts


def _dot_x(a, b, na=2, nb=2):
    asp = _split(a, na) if na > 1 else [a.astype(BF16)]
    bsp = _split(b, nb) if nb > 1 else [b.astype(BF16)]
    acc = None
    for i, ai in enumerate(asp):
        for j, bj in enumerate(bsp):
            if i + j >= max(na, nb):
                continue
            t = jnp.dot(ai, bj, preferred_element_type=F32)
            acc = t if acc is None else acc + t
    return acc


def _silu(x):
    return x * (1.0 / (1.0 + jnp.exp(-x)))


def _sigmoid(x):
    return 1.0 / (1.0 + jnp.exp(-x))


def _log1p(z):
    u = 1.0 + z
    tiny = u == 1.0
    return jnp.where(tiny, z, jnp.log(u) * (z / jnp.where(tiny, 1.0, u - 1.0)))


def _softplus(x):
    return jnp.maximum(x, 0.0) + _log1p(jnp.exp(-jnp.abs(x)))


def _neg_expm1(y):
    return -jnp.tanh(0.5 * y) * (jnp.exp(y) + 1.0)


def _norm_mod(x, w, sc, sh):
    ms = jnp.mean(x * x, axis=-1, keepdims=True)
    return (x * lax.rsqrt(ms + EPS)) * w * (1.0 + sc) + sh


def _shift_rows(x, k, prev_tail):
    xs = pltpu.roll(x, k, 0)
    head = pltpu.roll(prev_tail, k, 0)
    row = lax.broadcasted_iota(jnp.int32, (SUBLANES, x.shape[1]), 0)
    top = jnp.where(row < k, head, xs[:SUBLANES])
    return jnp.concatenate([top, xs[SUBLANES:]], axis=0)


def _ada_kernel(c_ref, w_ref, b_ref, o_ref):
    c = c_ref[...]
    cond = _silu(c)
    o_ref[0] = _dot_x(cond, w_ref[0], 3, 3) + b_ref[0]


def _ada_mods(c, ada_w, ada_b):
    depth, d, six_d = ada_w.shape
    bsz = c.shape[0]
    rows = max(SUBLANES, bsz)
    c_pad = jnp.zeros((rows, d), F32).at[:bsz].set(c)
    tn = 1536
    out = pl.pallas_call(
        _ada_kernel,
        grid=(depth, six_d // tn),
        in_specs=[pl.BlockSpec((rows, d), lambda l, j: (0, 0)),
                  pl.BlockSpec((1, d, tn), lambda l, j: (l, 0, j)),
                  pl.BlockSpec((1, 1, tn), lambda l, j: (l, 0, j))],
        out_specs=pl.BlockSpec((1, rows, tn), lambda l, j: (l, 0, j)),
        out_shape=jax.ShapeDtypeStruct((depth, rows, six_d), F32),
        compiler_params=_cparams("parallel", "parallel"),
        name="ada_mods",
    )(c_pad, ada_w, ada_b.reshape(depth, 1, six_d))
    return out[:, :bsz].reshape(depth, bsz, 6, 1, d).transpose(0, 2, 1, 3, 4)


def _t5_bucket(dist):
    max_exact = N_BUCKETS // 2
    d = np.maximum(dist, 0)
    large = max_exact + (np.log(np.maximum(d, 1) / max_exact) / math.log(MAX_DISTANCE / max_exact)
                         * (N_BUCKETS - max_exact)).astype(np.int32)
    large = np.minimum(large, N_BUCKETS - 1)
    return np.where(d < max_exact, d, large).astype(np.int32)


def _band_buckets():
    qi = np.arange(WINDOW)[:, None]
    s = np.arange(2 * WINDOW)[None, :]
    dist = qi + WINDOW - s
    in_window = (dist >= 0) & (dist < WINDOW)
    return np.where(in_window, _t5_bucket(dist), -1).astype(np.int32)


def _bias_kernel(rb_ref, bucket_ref, o_ref):
    h = pl.program_id(0)
    bucket = bucket_ref[...]
    acc = jnp.zeros(bucket.shape, F32)
    for b in range(N_BUCKETS):
        acc = jnp.where(bucket == b, rb_ref[b, h], acc)
    o_ref[0] = jnp.where(bucket < 0, NEG_BIG, acc)


def _bias_table(rel_bias):
    bucket = jnp.asarray(_band_buckets())
    out = pl.pallas_call(
        _bias_kernel,
        grid=(A_Q_HEADS,),
        in_specs=[pl.BlockSpec(memory_space=pltpu.SMEM),
                  _const_spec((WINDOW, 2 * WINDOW))],
        out_specs=pl.BlockSpec((1, WINDOW, 2 * WINDOW), lambda h: (h, 0, 0)),
        out_shape=jax.ShapeDtypeStruct((A_Q_HEADS, WINDOW, 2 * WINDOW), F32),
        compiler_params=_cparams("parallel"),
        name="attn_bias_table",
    )(rel_bias, bucket)
    return out.reshape(A_Q_HEADS // 2, 2 * WINDOW, 2 * WINDOW)


_C_QA = 0
_C_KD = _C_QA + A_Q_W
_C_VD = _C_KD + 2 * A_KV_W
_C_QKV = _C_VD + 2 * A_KV_W
_C_GATE = _C_QKV + B_QKV_W
_C_SMALL = _C_GATE + B_W
_AB_COLS = _C_SMALL + LANES


def _ab_in_weight(w_in):
    qa, ka, va, qkv, gate, beta, dec = jnp.split(
        w_in, list(np.cumsum([A_Q_W, A_KV_W, A_KV_W, B_QKV_W, B_W, B_HEADS])), axis=1)

    def dup(t):
        return jnp.concatenate([t[:, :HEAD_DIM]] * 2 + [t[:, HEAD_DIM:]] * 2, axis=1)

    small = jnp.zeros((w_in.shape[0], LANES), w_in.dtype)
    small = small.at[:, :B_HEADS].set(beta).at[:, B_HEADS:2 * B_HEADS].set(dec)
    return jnp.concatenate([qa, dup(ka), dup(va), qkv, gate, small], axis=1).astype(BF16)


def _head_selector():
    e = np.zeros((B_W, LANES), np.float32)
    for h in range(B_HEADS):
        e[h * HEAD_DIM:(h + 1) * HEAD_DIM, h] = 1.0
    return e


def _in0_kernel(x_ref, nw_ref, sc_ref, sh_ref, w_ref, cw_ref, sel_ref, selt_ref, alog_ref, dtb_ref,
                qa_ref, kd_ref, vd_ref, qn_ref, kn_ref, vb_ref, gs_ref, beta_ref, g_ref,
                tail_ref, *, tiles_per_seq):
    i = pl.program_id(0)

    @pl.when(i % tiles_per_seq == 0)
    def _():
        tail_ref[...] = jnp.zeros_like(tail_ref)

    hn = _norm_mod(x_ref[...], nw_ref[...], sc_ref[0], sh_ref[0])
    proj = jnp.dot(hn.astype(BF16), w_ref[...], preferred_element_type=F32)
    qa_ref[...] = proj[:, _C_QA:_C_KD].astype(BF16)
    kd_ref[...] = proj[:, _C_KD:_C_VD].astype(BF16)
    vd_ref[...] = proj[:, _C_VD:_C_QKV].astype(BF16)

    xq = proj[:, _C_QKV:_C_GATE]
    tail = tail_ref[...]
    cw = cw_ref[...]
    y = xq * cw[B_CONV - 1:B_CONV]
    for k in range(1, B_CONV):
        y = y + _shift_rows(xq, k, tail) * cw[B_CONV - 1 - k:B_CONV - k]
    tail_ref[...] = xq[xq.shape[0] - SUBLANES:]
    y = _silu(y)
    q, k_, v = y[:, :B_W], y[:, B_W:2 * B_W], y[:, 2 * B_W:]

    def l2n(t):
        ssq = _dot_x(t * t, sel_ref[...], 2, 1)
        r = lax.rsqrt(ssq + EPS)
        return t * _dot_x(r, selt_ref[...], 2, 1)

    qn_ref[...] = l2n(q) * (HEAD_DIM ** -0.5)
    kn_ref[...] = l2n(k_)
    vb_ref[...] = v
    gs_ref[...] = _silu(proj[:, _C_GATE:_C_SMALL])
    small = proj[:, _C_SMALL:]
    lane = lax.broadcasted_iota(jnp.int32, small.shape, 1)
    beta_ref[...] = jnp.where(lane < B_HEADS, _sigmoid(small), 0.0)
    dec = pltpu.roll(small, LANES - B_HEADS, 1)
    g = -jnp.exp(alog_ref[...]) * _softplus(dec + dtb_ref[...])
    g_ref[...] = jnp.where(lane < B_HEADS, g, 0.0)


def _in_proj0(x2d, nw, sc, sh, w_in, conv_w, a_log, dt_bias, seq_len):
    n, d = x2d.shape
    tm = TOKEN_TILE
    tiles_per_seq = seq_len // tm
    w = _ab_in_weight(w_in)
    sel = jnp.asarray(_head_selector(), BF16)
    selt = jnp.asarray(_head_selector().T.copy(), BF16)
    pad8 = lambda v: jnp.zeros((1, LANES), F32).at[0, :B_HEADS].set(v)
    row = lambda width: pl.BlockSpec((tm, width), lambda i: (i, 0))
    per_b = pl.BlockSpec((1, 1, d), lambda i: (i // tiles_per_seq, 0, 0))
    outs = pl.pallas_call(
        functools.partial(_in0_kernel, tiles_per_seq=tiles_per_seq),
        grid=(n // tm,),
        in_specs=[row(d), _const_spec((1, d)), per_b, per_b,
                  _const_spec((d, _AB_COLS)), _const_spec((B_CONV, B_QKV_W)),
                  _const_spec((B_W, LANES)), _const_spec((LANES, B_W)),
                  _const_spec((1, LANES)), _const_spec((1, LANES))],
        out_specs=[row(A_Q_W), row(2 * A_KV_W), row(2 * A_KV_W),
                   row(B_W), row(B_W), row(B_W), row(B_W), row(LANES), row(LANES)],
        out_shape=[jax.ShapeDtypeStruct((n, A_Q_W), BF16),
                   jax.ShapeDtypeStruct((n, 2 * A_KV_W), BF16),
                   jax.ShapeDtypeStruct((n, 2 * A_KV_W), BF16),
                   jax.ShapeDtypeStruct((n, B_W), F32),
                   jax.ShapeDtypeStruct((n, B_W), F32),
                   jax.ShapeDtypeStruct((n, B_W), F32),
                   jax.ShapeDtypeStruct((n, B_W), F32),
                   jax.ShapeDtypeStruct((n, LANES), F32),
                   jax.ShapeDtypeStruct((n, LANES), F32)],
        scratch_shapes=[pltpu.VMEM((SUBLANES, B_QKV_W), F32)],
        compiler_params=_cparams("arbitrary"),
        name="in_proj0",
    )(x2d, nw.reshape(1, d), sc, sh, w, conv_w, sel, selt, pad8(a_log), pad8(dt_bias))
    return outs


def _attn_kernel(sink_ref, q_ref, kp_ref, kc_ref, vp_ref, vc_ref, bm_ref, o_ref, *, blocks_per_seq):
    i = pl.program_id(0)
    first = (i % blocks_per_seq) == 0
    w = WINDOW
    lane = lax.broadcasted_iota(jnp.int32, (w, LANES), 1)
    low = lane < HEAD_DIM
    col = lax.broadcasted_iota(jnp.int32, (2 * w, 2 * w), 1)
    row = lax.broadcasted_iota(jnp.int32, (2 * w, 1), 0)
    prev_dead = jnp.logical_and(first, col < w)
    q_all = q_ref[...]
    zero = jnp.zeros((), q_all.dtype)
    outs = []
    for j in range(A_Q_HEADS // 2):
        kh = (2 * j) // (A_Q_HEADS // A_KV_HEADS)
        qp = q_all[:, j * LANES:(j + 1) * LANES]
        qs = jnp.concatenate([jnp.where(low, qp, zero), jnp.where(low, zero, qp)], axis=0)
        kd = jnp.concatenate([kp_ref[:, kh * LANES:(kh + 1) * LANES],
                              kc_ref[:, kh * LANES:(kh + 1) * LANES]], axis=0)
        vd = jnp.concatenate([vp_ref[:, kh * LANES:(kh + 1) * LANES],
                              vc_ref[:, kh * LANES:(kh + 1) * LANES]], axis=0)
        s = lax.dot_general(qs, kd, (((1,), (1,)), ((), ())), preferred_element_type=F32)
        s = s * (HEAD_DIM ** -0.5) + bm_ref[j]
        s = jnp.where(prev_dead, NEG_BIG, s)
        sink = jnp.where(row < w, sink_ref[2 * j], sink_ref[2 * j + 1])
        m = jnp.maximum(jnp.max(s, axis=-1, keepdims=True), sink)
        p = jnp.exp(s - m)
        denom = jnp.sum(p, axis=-1, keepdims=True) + jnp.exp(sink - m)
        pv = jnp.dot(p.astype(BF16), vd, preferred_element_type=F32) / denom
        outs.append(jnp.where(low, pv[:w], pv[w:]))
    o_ref[...] = jnp.concatenate(outs, axis=1).astype(o_ref.dtype)


def _attention(qa, kd, vd, bias_tbl, sinks, seq_len):
    n = qa.shape[0]
    w = WINDOW
    nb = seq_len // w
    cur = lambda i: (i, 0)
    prev = lambda i: (jnp.where(i % nb == 0, i, i - 1), 0)
    return pl.pallas_call(
        functools.partial(_attn_kernel, blocks_per_seq=nb),
        grid=(n // w,),
        in_specs=[pl.BlockSpec(memory_space=pltpu.SMEM),
                  pl.BlockSpec((w, A_Q_W), cur),
                  pl.BlockSpec((w, 2 * A_KV_W), prev), pl.BlockSpec((w, 2 * A_KV_W), cur),
                  pl.BlockSpec((w, 2 * A_KV_W), prev), pl.BlockSpec((w, 2 * A_KV_W), cur),
                  _const_spec((A_Q_HEADS // 2, 2 * w, 2 * w))],
        out_specs=pl.BlockSpec((w, A_Q_W), cur),
        out_shape=jax.ShapeDtypeStruct((n, A_Q_W), BF16),
        compiler_params=_cparams("parallel"),
        name="swa_attention",
    )(sinks, qa, kd, kd, vd, vd, bias_tbl)


_DN_PAIRS = B_HEADS // 2
_DN_INV_BLOCK = 16
_DN_INV_SPLIT = 2


def _block_diag(x, low):
    zero = jnp.zeros((), x.dtype)
    return jnp.concatenate([jnp.where(low, x, zero), jnp.where(low, zero, x)], axis=0)


def _dn_chunk(q, k, v, b, g, s_bd, consts):
    (low, i_idx, j_idx, tril, ones_c, mask_bd) = consts
    c = CHUNK
    causal = i_idx >= j_idx
    strict = i_idx > j_idx
    eye = (i_idx == j_idx).astype(F32)
    blk_shift = int(math.log2(_DN_INV_BLOCK))
    same_blk = (i_idx >> blk_shift) == (j_idx >> blk_shift)

    gc = _dot_x(tril, g, 1, 3)
    g_upper = jnp.where(i_idx <= j_idx, g, 0.0)
    gr = _dot_x(ones_c, g_upper, 1, 3)
    decay = jnp.exp(jnp.where(causal, gc - gr, NEG_BIG))
    g_last = gc[c - 1:c, :]

    ks = _block_diag(k, low)
    kk = _bdot_nt(k, ks)
    lmat = jnp.where(strict, b * kk * decay, 0.0)

    def mm(x, y):
        return _dot_x(x, _block_diag(y, low), _DN_INV_SPLIT, _DN_INV_SPLIT)

    l_diag = jnp.where(same_blk, lmat, 0.0)
    l_off = lmat - l_diag
    nil = -l_diag
    d_inv = eye + nil
    pw = nil
    steps = int(math.log2(_DN_INV_BLOCK)) - 1
    for _ in range(steps):
        pw = mm(pw, pw)
        d_inv = mm(d_inv, eye + pw)
    m1 = -mm(d_inv, l_off)
    acc = eye + m1
    pw = m1
    for _ in range(int(math.log2(c // _DN_INV_BLOCK)) - 1):
        pw = mm(pw, pw)
        acc = mm(acc, eye + pw)
    tmat = mm(acc, d_inv)

    u = _bdot(tmat, _block_diag(v * b, low))
    w = _bdot(tmat, _block_diag(k * (b * jnp.exp(gc)), low))
    qk = jnp.where(causal, _bdot_nt(q, ks) * decay, 0.0)
    k_dec = k * jnp.exp(g_last - gc)
    q_dec = q * jnp.exp(gc)

    v_new = u - _bdot(w, s_bd)
    o = _bdot(q_dec, s_bd) + _bdot(qk, _block_diag(v_new, low))
    s_new = s_bd * jnp.exp(g_last) + jnp.where(mask_bd, _bdot_tn(k_dec, v_new), 0.0)
    return o, s_new


def _dn_kernel(qn_ref, kn_ref, vb_ref, gs_ref, beta_ref, g_ref, selt_ref, nw_ref, o_ref,
               s_ref, bexp_ref, gexp_ref, *, groups_per_seq):
    i = pl.program_id(0)

    @pl.when(i % groups_per_seq == 0)
    def _():
        s_ref[...] = jnp.zeros_like(s_ref)

    bexp_ref[...] = _dot_x(beta_ref[...], selt_ref[...], 3, 1)
    gexp_ref[...] = _dot_x(g_ref[...], selt_ref[...], 3, 1)

    c = CHUNK
    lane = lax.broadcasted_iota(jnp.int32, (c, LANES), 1)
    low = lane < HEAD_DIM
    i_idx = lax.broadcasted_iota(jnp.int32, (c, LANES), 0)
    j_idx = lane & (c - 1)
    r2 = lax.broadcasted_iota(jnp.int32, (c, c), 0)
    c2 = lax.broadcasted_iota(jnp.int32, (c, c), 1)
    tril = (r2 >= c2).astype(BF16)
    ones_c = jnp.ones((c, c), BF16)
    rb = lax.broadcasted_iota(jnp.int32, (LANES, LANES), 0)
    cb = lax.broadcasted_iota(jnp.int32, (LANES, LANES), 1)
    mask_bd = (rb < HEAD_DIM) == (cb < HEAD_DIM)
    head_mean = jnp.where(mask_bd, 1.0 / HEAD_DIM, 0.0).astype(BF16)
    consts = (low, i_idx, j_idx, tril, ones_c, mask_bd)
    nw = nw_ref[...]

    def chunk_body(ci, carry):
        r0 = pl.multiple_of(ci * c, c)
        rows = pl.ds(r0, c)
        for p in range(_DN_PAIRS):
            ls = slice(p * LANES, (p + 1) * LANES)
            o, s_new = _dn_chunk(qn_ref[rows, ls], kn_ref[rows, ls], vb_ref[rows, ls],
                                 bexp_ref[rows, ls], gexp_ref[rows, ls], s_ref[p], consts)
            s_ref[p] = s_new
            ms = _dot_x(o * o, head_mean, 2, 1)
            y = (o * lax.rsqrt(ms + EPS)) * nw * gs_ref[rows, ls]
            o_ref[rows, ls] = y.astype(o_ref.dtype)
        return carry

    lax.fori_loop(0, o_ref.shape[0] // c, chunk_body, 0)


def _deltanet(qn, kn, vb, gs, beta, g, norm_w, seq_len):
    n = qn.shape[0]
    tm = TOKEN_TILE
    selt = jnp.asarray(_head_selector().T.copy(), BF16)
    nw2 = jnp.concatenate([norm_w, norm_w]).reshape(1, LANES)
    row = lambda width: pl.BlockSpec((tm, width), lambda i: (i, 0))
    return pl.pallas_call(
        functools.partial(_dn_kernel, groups_per_seq=seq_len // tm),
        grid=(n // tm,),
        in_specs=[row(B_W), row(B_W), row(B_W), row(B_W), row(LANES), row(LANES),
                  _const_spec((LANES, B_W)), _const_spec((1, LANES))],
        out_specs=row(B_W),
        out_shape=jax.ShapeDtypeStruct((n, B_W), BF16),
        scratch_shapes=[pltpu.VMEM((_DN_PAIRS, LANES, LANES), F32),
                        pltpu.VMEM((tm, B_W), F32), pltpu.VMEM((tm, B_W), F32)],
        compiler_params=_cparams("arbitrary"),
        name="gated_deltanet",
    )(qn, kn, vb, gs, beta, g, selt, nw2)


def _resident_spec(shape):
    nd = len(shape)
    return pl.BlockSpec(shape, lambda *_: (0,) * nd, pipeline_mode=pl.Buffered(1))


def _mid0_kernel(attn_ref, dn_ref, x_ref, wo_ref, g1_ref, nw_ref, sc_ref, sh_ref, g2_ref,
                 wg_ref, wu_ref, wd_ref, o_ref):
    mix = (jnp.dot(attn_ref[...], wo_ref[:A_Q_W], preferred_element_type=F32)
           + jnp.dot(dn_ref[...], wo_ref[A_Q_W:], preferred_element_type=F32))
    x1 = x_ref[...] + g1_ref[0] * mix
    hn = _norm_mod(x1, nw_ref[...], sc_ref[0], sh_ref[0]).astype(BF16)
    hg = jnp.dot(hn, wg_ref[...], preferred_element_type=F32)
    hu = jnp.dot(hn, wu_ref[...], preferred_element_type=F32)
    act = (_silu(hg) * hu).astype(BF16)
    o_ref[...] = x1 + g2_ref[0] * jnp.dot(act, wd_ref[...], preferred_element_type=F32)


def _mid0(attn, dn, x2d, w_out, g1, nw, sc, sh, g2, wg, wu, wd, seq_len):
    n, d = x2d.shape
    tm = TOKEN_TILE
    tps = seq_len // tm
    row = lambda width: pl.BlockSpec((tm, width), lambda i: (i, 0))
    per_b = pl.BlockSpec((1, 1, d), lambda i: (i // tps, 0, 0))
    return pl.pallas_call(
        _mid0_kernel,
        grid=(n // tm,),
        in_specs=[row(A_Q_W), row(B_W), row(d), _resident_spec(w_out.shape), per_b,
                  _const_spec((1, d)), per_b, per_b, per_b,
                  _resident_spec(wg.shape), _resident_spec(wu.shape), _resident_spec(wd.shape)],
        out_specs=row(d),
        out_shape=jax.ShapeDtypeStruct((n, d), F32),
        compiler_params=_cparams("parallel"),
        name="out_proj0_swiglu",
    )(attn, dn, x2d, w_out.astype(BF16), g1, nw.reshape(1, d), sc, sh, g2,
      wg.astype(BF16), wu.astype(BF16), wd.astype(BF16))


def _gelu_tanh(x):
    return 0.5 * x * (1.0 + jnp.tanh(math.sqrt(2.0 / math.pi) * (x + 0.044715 * (x * x * x))))


def _linear_scan(a, b):
    n = a.shape[0]
    row = lax.broadcasted_iota(jnp.int32, a.shape, 0)
    s = 1
    while s < n:
        a_sh = pltpu.roll(a, s, 0)
        b_sh = pltpu.roll(b, s, 0)
        valid = row >= s
        b = jnp.where(valid, a * b_sh + b, b)
        a = jnp.where(valid, a * a_sh, a)
        s *= 2
    return a, b


def _mix1_kernel(x_ref, nw_ref, sc_ref, sh_ref, w_ref, cw_ref, cb_ref, ga_ref, gab_ref, gx_ref, gxb_ref,
                 lam_ref, sw_ref, o_ref, tail_c_ref, tail_d_ref, h_ref, *, tiles_per_seq):
    i = pl.program_id(0)

    @pl.when(i % tiles_per_seq == 0)
    def _():
        tail_c_ref[...] = jnp.zeros_like(tail_c_ref)
        tail_d_ref[...] = jnp.zeros_like(tail_d_ref)
        h_ref[...] = jnp.zeros_like(h_ref)

    hn = _norm_mod(x_ref[...], nw_ref[...], sc_ref[0], sh_ref[0]).astype(BF16)
    proj = jnp.dot(hn, w_ref[...], preferred_element_type=F32)
    w_l = LRU_WIDTH
    xc_in = proj[:, :w_l]
    yc = proj[:, w_l:2 * w_l]
    bd = proj[:, 2 * w_l:2 * w_l + SC_WIDTH]
    cd = proj[:, 2 * w_l + SC_WIDTH:2 * w_l + 2 * SC_WIDTH]
    hd = proj[:, 2 * w_l + 2 * SC_WIDTH:]
    tm = xc_in.shape[0]

    kc = cw_ref.shape[0]
    tail = tail_c_ref[...]
    cw = cw_ref[...]
    xc = xc_in * cw[kc - 1:kc] + cb_ref[...]
    for k in range(1, kc):
        xc = xc + _shift_rows(xc_in, k, tail) * cw[kc - 1 - k:kc - k]
    tail_c_ref[...] = xc_in[tm - SUBLANES:]

    xb = xc.astype(BF16)
    gw = ga_ref.shape[1]
    ra, ri = [], []
    for p in range(ga_ref.shape[0]):
        xin = xb[:, p * gw:(p + 1) * gw]
        ra.append(jnp.dot(xin, ga_ref[p], preferred_element_type=F32))
        ri.append(jnp.dot(xin, gx_ref[p], preferred_element_type=F32))
    r = _sigmoid(jnp.concatenate(ra, axis=1) + gab_ref[...])
    ig = _sigmoid(jnp.concatenate(ri, axis=1) + gxb_ref[...])
    log_a = (-LRU_C) * r * _softplus(-lam_ref[...])
    a = jnp.exp(log_a)
    b = jnp.sqrt(_neg_expm1(2.0 * log_a)) * (ig * xc)
    a_cum, h_loc = _linear_scan(a, b)
    h = a_cum * h_ref[0:1, :] + h_loc
    h_ref[...] = jnp.broadcast_to(h[tm - 1:tm, :], h_ref.shape)
    yc_out = h * _gelu_tanh(yc)

    ks = sw_ref.shape[0]
    ch = cd * hd
    tail_d = tail_d_ref[...]
    sw = sw_ref[...]
    conv = ch * sw[ks - 1:ks]
    for k in range(1, ks):
        conv = conv + _shift_rows(ch, k, tail_d) * sw[ks - 1 - k:ks - k]
    tail_d_ref[...] = ch[tm - SUBLANES:]
    o_ref[...] = jnp.concatenate([yc_out, bd * conv], axis=1).astype(o_ref.dtype)


def _pair_block_diag(gw):
    nb, bw, _ = gw.shape
    g2 = gw.reshape(nb // 2, 2, bw, bw)
    z = jnp.zeros((nb // 2, bw, bw), gw.dtype)
    top = jnp.concatenate([g2[:, 0], z], axis=2)
    bot = jnp.concatenate([z, g2[:, 1]], axis=2)
    return jnp.concatenate([top, bot], axis=1).astype(BF16)


def _mix1(x2d, nw, sc, sh, w_in, conv_w, conv_b, ga_w, ga_b, gx_w, gx_b, lam, sconv_w, seq_len):
    n, d = x2d.shape
    tm = TOKEN_TILE
    tps = seq_len // tm
    cd_in = w_in.shape[1]
    cd_out = LRU_WIDTH + SC_WIDTH
    row = lambda width: pl.BlockSpec((tm, width), lambda i: (i, 0))
    per_b = pl.BlockSpec((1, 1, d), lambda i: (i // tps, 0, 0))
    ga = _pair_block_diag(ga_w)
    gx = _pair_block_diag(gx_w)
    vec = lambda v: v.reshape(1, -1)
    return pl.pallas_call(
        functools.partial(_mix1_kernel, tiles_per_seq=tps),
        grid=(n // tm,),
        in_specs=[row(d), _const_spec((1, d)), per_b, per_b, _resident_spec((d, cd_in)),
                  _const_spec(conv_w.shape), _const_spec((1, LRU_WIDTH)),
                  _const_spec(ga.shape), _const_spec((1, LRU_WIDTH)),
                  _const_spec(gx.shape), _const_spec((1, LRU_WIDTH)),
                  _const_spec((1, LRU_WIDTH)), _const_spec(sconv_w.shape)],
        out_specs=row(cd_out),
        out_shape=jax.ShapeDtypeStruct((n, cd_out), BF16),
        scratch_shapes=[pltpu.VMEM((SUBLANES, LRU_WIDTH), F32), pltpu.VMEM((SUBLANES, SC_WIDTH), F32),
                        pltpu.VMEM((SUBLANES, LRU_WIDTH), F32)],
        compiler_params=_cparams("arbitrary"),
        name="rglru_shortconv_mixer",
    )(x2d, vec(nw), sc, sh, w_in.astype(BF16), conv_w, vec(conv_b), ga, vec(ga_b), gx, vec(gx_b),
      vec(lam), sconv_w)


def _route_kernel(cat_ref, x_ref, wo_ref, g1_ref, nw_ref, sc_ref, sh_ref, rw_ref, rb_ref,
                  x3_ref, hn_ref, meta_ref, wt_ref, cnt_ref, carry_ref):
    i = pl.program_id(0)

    @pl.when(i == 0)
    def _():
        carry_ref[...] = jnp.zeros_like(carry_ref)

    x3 = x_ref[...] + g1_ref[0] * jnp.dot(cat_ref[...], wo_ref[...], preferred_element_type=F32)
    x3_ref[...] = x3
    hn = _norm_mod(x3, nw_ref[...], sc_ref[0], sh_ref[0])
    hn_ref[...] = hn
    tm = hn.shape[0]
    lane = lax.broadcasted_iota(jnp.int32, (tm, LANES), 1)
    logits = _dot_x(hn, rw_ref[...], 2, 2) + rb_ref[...]
    lg = jnp.where(lane < N_EXPERTS, logits, NEG_BIG)
    m1 = jnp.max(lg, axis=1, keepdims=True)
    i1 = jnp.min(jnp.where(lg == m1, lane, LANES), axis=1, keepdims=True)
    lg2 = jnp.where(lane == i1, NEG_BIG, lg)
    m2 = jnp.max(lg2, axis=1, keepdims=True)
    i2 = jnp.min(jnp.where(lg2 == m2, lane, LANES), axis=1, keepdims=True)
    e2 = jnp.exp(m2 - m1)
    w1 = 1.0 / (1.0 + e2)
    w2 = e2 / (1.0 + e2)

    hit1 = lane == i1
    hit2 = lane == i2
    sel = jnp.logical_or(hit1, hit2).astype(F32)
    r_i = lax.broadcasted_iota(jnp.int32, (tm, tm), 0)
    c_i = lax.broadcasted_iota(jnp.int32, (tm, tm), 1)
    tril = (r_i >= c_i).astype(BF16)
    incl = jnp.dot(tril, sel.astype(BF16), preferred_element_type=F32)
    carry = carry_ref[0:1, :]
    excl = incl - sel + carry
    r1 = jnp.sum(jnp.where(hit1, excl, 0.0), axis=1, keepdims=True)
    r2 = jnp.sum(jnp.where(hit2, excl, 0.0), axis=1, keepdims=True)
    total = carry + incl[tm - 1:tm, :]
    carry_ref[...] = jnp.broadcast_to(total, carry_ref.shape)
    cnt_ref[...] = jnp.broadcast_to(total, cnt_ref.shape).astype(jnp.int32)

    meta = jnp.where(lane == 0, i1, 0)
    meta = jnp.where(lane == 1, i2, meta)
    meta = jnp.where(lane == 2, r1.astype(jnp.int32), meta)
    meta = jnp.where(lane == 3, r2.astype(jnp.int32), meta)
    meta_ref[...] = meta
    wt_ref[...] = jnp.where(lane == 0, w1, jnp.where(lane == 1, w2, 0.0))


def _route(cat, x2d, w_out, g1, nw, sc, sh, router_w, router_b, seq_len):
    n, d = x2d.shape
    tm = TOKEN_TILE
    tps = seq_len // tm
    row = lambda width: pl.BlockSpec((tm, width), lambda i: (i, 0))
    per_b = pl.BlockSpec((1, 1, d), lambda i: (i // tps, 0, 0))
    rw = jnp.zeros((d, LANES), F32).at[:, :N_EXPERTS].set(router_w)
    rb = jnp.zeros((1, LANES), F32).at[0, :N_EXPERTS].set(router_b)
    return pl.pallas_call(
        _route_kernel,
        grid=(n // tm,),
        in_specs=[row(cat.shape[1]), row(d), _resident_spec(w_out.shape), per_b, _const_spec((1, d)),
                  per_b, per_b, _const_spec((d, LANES)), _const_spec((1, LANES))],
        out_specs=[row(d), row(d), row(LANES), row(LANES), _const_spec((SUBLANES, LANES))],
        out_shape=[jax.ShapeDtypeStruct((n, d), F32), jax.ShapeDtypeStruct((n, d), F32),
                   jax.ShapeDtypeStruct((n, LANES), jnp.int32), jax.ShapeDtypeStruct((n, LANES), F32),
                   jax.ShapeDtypeStruct((SUBLANES, LANES), jnp.int32)],
        scratch_shapes=[pltpu.VMEM((SUBLANES, LANES), F32)],
        compiler_params=_cparams("arbitrary"),
        name="out_proj1_router",
    )(cat, x2d, w_out.astype(BF16), g1, nw.reshape(1, d), sc, sh, rw, rb)


def _dispatch_kernel(d1_ref, d2_ref, hn_ref, xs_ref, sem):
    i = pl.program_id(0)
    tr = ROW_DMA_TILE
    base = i * tr

    def copies(t):
        src = hn_ref.at[pl.ds(base + t, 1)]
        return (pltpu.make_async_copy(src, xs_ref.at[pl.ds(d1_ref[base + t], 1)], sem),
                pltpu.make_async_copy(src, xs_ref.at[pl.ds(d2_ref[base + t], 1)], sem))

    def start(t, c):
        a, b = copies(t)
        a.start()
        b.start()
        return c

    def wait(t, c):
        a, b = copies(t)
        a.wait()
        b.wait()
        return c

    lax.fori_loop(0, tr, start, 0)
    lax.fori_loop(0, tr, wait, 0)


def _dispatch(hn, d1, d2, rows_out):
    n, d = hn.shape
    return pl.pallas_call(
        _dispatch_kernel,
        grid_spec=pltpu.PrefetchScalarGridSpec(
            num_scalar_prefetch=2,
            grid=(n // ROW_DMA_TILE,),
            in_specs=[pl.BlockSpec(memory_space=pl.ANY)],
            out_specs=pl.BlockSpec(memory_space=pl.ANY),
            scratch_shapes=[pltpu.SemaphoreType.DMA(())]),
        out_shape=jax.ShapeDtypeStruct((rows_out, d), hn.dtype),
        compiler_params=_cparams("arbitrary"),
        name="moe_dispatch",
    )(d1, d2, hn)


def _moe_kernel(ti_ref, te_ref, tv_ref, lo_ref, hi_ref, x_ref, wg_ref, wu_ref, wd_ref, o_ref,
                xb_ref, wgb_ref, wub_ref, wdb_ref):
    w = pl.program_id(0)
    f = pl.program_id(1)
    sub = MOE_SUB
    sub_shift = int(math.log2(sub))

    @pl.when(tv_ref[w] == 1)
    def _():
        lo = lo_ref[w]
        hi = hi_ref[w]

        @pl.when(f == 0)
        def _():
            row = lax.broadcasted_iota(jnp.int32, (x_ref.shape[0], 1), 0)
            mine = jnp.logical_and(row >= lo, row < hi)
            xb_ref[...] = jnp.where(mine, x_ref[...], 0.0).astype(BF16)

        wgb_ref[...] = wg_ref[0].astype(BF16)
        wub_ref[...] = wu_ref[0].astype(BF16)
        wdb_ref[...] = wd_ref[0].astype(BF16)

        def sub_block(s, carry):
            rows = pl.ds(pl.multiple_of(s * sub, sub), sub)
            xb = xb_ref[rows, :]
            hg = jnp.dot(xb, wgb_ref[...], preferred_element_type=F32)
            hu = jnp.dot(xb, wub_ref[...], preferred_element_type=F32)
            act = (_silu(hg) * hu).astype(BF16)
            part = jnp.dot(act, wdb_ref[...], preferred_element_type=F32)
            init = jnp.logical_and(f == 0, lo <= s * sub)

            @pl.when(init)
            def _():
                o_ref[rows, :] = part

            @pl.when(jnp.logical_not(init))
            def _():
                o_ref[rows, :] += part

            return carry

        lax.fori_loop(lo >> sub_shift, (hi + sub - 1) >> sub_shift, sub_block, 0)


def _moe_ffn(xs, items, wg, wu, wd):
    rows, d = xs.shape
    tm = MOE_TILE
    tf = MOE_FF_TILE
    nf = wg.shape[2] // tf
    n_items = items[0].shape[0]
    f_idx = lambda f, v: f * v + (nf - 1) * (1 - v)
    return pl.pallas_call(
        _moe_kernel,
        grid_spec=pltpu.PrefetchScalarGridSpec(
            num_scalar_prefetch=5,
            grid=(n_items, nf),
            in_specs=[pl.BlockSpec((tm, d), lambda w, f, ti, te, tv, lo, hi: (ti[w], 0)),
                      pl.BlockSpec((1, d, tf), lambda w, f, ti, te, tv, lo, hi: (te[w], 0, f_idx(f, tv[w]))),
                      pl.BlockSpec((1, d, tf), lambda w, f, ti, te, tv, lo, hi: (te[w], 0, f_idx(f, tv[w]))),
                      pl.BlockSpec((1, tf, d), lambda w, f, ti, te, tv, lo, hi: (te[w], f_idx(f, tv[w]), 0))],
            out_specs=pl.BlockSpec((tm, d), lambda w, f, ti, te, tv, lo, hi: (ti[w], 0)),
            scratch_shapes=[pltpu.VMEM((tm, d), BF16), pltpu.VMEM((d, tf), BF16),
                            pltpu.VMEM((d, tf), BF16), pltpu.VMEM((tf, d), BF16)]),
        out_shape=jax.ShapeDtypeStruct((rows, d), F32),
        compiler_params=_cparams("arbitrary", "arbitrary"),
        name="moe_expert_swiglu",
    )(*items, xs, wg, wu, wd)


def _combine_kernel(d1_ref, d2_ref, ys_ref, x_ref, wt_ref, g2_ref, fw_ref, o_ref, y1_ref, y2_ref, sem):
    i = pl.program_id(0)
    tr = ROW_DMA_TILE
    base = i * tr

    def copies(t):
        return (pltpu.make_async_copy(ys_ref.at[pl.ds(d1_ref[base + t], 1)], y1_ref.at[pl.ds(t, 1)], sem),
                pltpu.make_async_copy(ys_ref.at[pl.ds(d2_ref[base + t], 1)], y2_ref.at[pl.ds(t, 1)], sem))

    def start(t, c):
        a, b = copies(t)
        a.start()
        b.start()
        return c

    def wait(t, c):
        a, b = copies(t)
        a.wait()
        b.wait()
        return c

    lax.fori_loop(0, tr, start, 0)
    lax.fori_loop(0, tr, wait, 0)
    wt = wt_ref[...]
    ffn = wt[:, 0:1] * y1_ref[...] + wt[:, 1:2] * y2_ref[...]
    x4 = x_ref[...] + g2_ref[0] * ffn
    ms = jnp.mean(x4 * x4, axis=-1, keepdims=True)
    o_ref[...] = (x4 * lax.rsqrt(ms + EPS)) * fw_ref[...]


def _combine(ys, d1, d2, x3, wt, g2, final_w, seq_len):
    n, d = x3.shape
    tr = ROW_DMA_TILE
    tps = seq_len // tr
    return pl.pallas_call(
        _combine_kernel,
        grid_spec=pltpu.PrefetchScalarGridSpec(
            num_scalar_prefetch=2,
            grid=(n // tr,),
            in_specs=[pl.BlockSpec(memory_space=pl.ANY),
                      pl.BlockSpec((tr, d), lambda i, a, b: (i, 0)),
                      pl.BlockSpec((tr, LANES), lambda i, a, b: (i, 0)),
                      pl.BlockSpec((1, 1, d), lambda i, a, b: (i // tps, 0, 0)),
                      pl.BlockSpec((1, d), lambda i, a, b: (0, 0))],
            out_specs=pl.BlockSpec((tr, d), lambda i, a, b: (i, 0)),
            scratch_shapes=[pltpu.VMEM((tr, d), F32), pltpu.VMEM((tr, d), F32),
                            pltpu.SemaphoreType.DMA(())]),
        out_shape=jax.ShapeDtypeStruct((n, d), F32),
        compiler_params=_cparams("arbitrary"),
        name="moe_combine_final_norm",
    )(d1, d2, ys, x3, wt, g2, final_w.reshape(1, d))


def _moe_tables(meta, counts):
    n = meta.shape[0]
    tm = MOE_TILE
    n_items = (2 * n) // tm + N_EXPERTS - 1
    cnt = counts[0, :N_EXPERTS]
    end = jnp.cumsum(cnt)
    off = end - cnt
    d1 = jnp.take(off, meta[:, 0]) + meta[:, 2]
    d2 = jnp.take(off, meta[:, 1]) + meta[:, 3]
    first_tile = off // tm
    n_e = jnp.where(cnt > 0, (end - 1) // tm - first_tile + 1, 0)
    item_end = jnp.cumsum(n_e)
    item_start = item_end - n_e
    total = item_end[-1]
    w = jnp.arange(n_items, dtype=jnp.int32)
    w_eff = jnp.minimum(w, total - 1)
    te = jnp.minimum(jnp.sum((item_end[None, :] <= w_eff[:, None]).astype(jnp.int32), axis=1),
                     N_EXPERTS - 1)
    ti = jnp.take(first_tile, te) + (w_eff - jnp.take(item_start, te))
    tv = (w < total).astype(jnp.int32)
    lo = jnp.clip(jnp.take(off, te) - ti * tm, 0, tm) * tv
    hi = jnp.clip(jnp.take(end, te) - ti * tm, 0, tm) * tv
    i32 = lambda t: t.astype(jnp.int32)
    return i32(d1), i32(d2), (i32(ti), i32(te), tv, i32(lo), i32(hi))


def kernel(x, c, rel_bias, ada_w, ada_b, norm_mix_w, norm_ffn_w, final_norm_w, ab_w_in, attn_sinks,
           dn_conv_w, dn_a_log, dn_dt_bias, dn_norm_w, ab_w_out, ffn_w_gate, ffn_w_up, ffn_w_down,
           cd_w_in, lru_conv_w, lru_conv_b, lru_gate_a_w, lru_gate_a_b, lru_gate_x_w, lru_gate_x_b,
           lru_lambda, sconv_w, cd_w_out, moe_router_w, moe_router_b, moe_w_gate, moe_w_up, moe_w_down):
    bsz, seq_len, d = x.shape
    n = bsz * seq_len
    x2d = x.reshape(n, d)
    mods = _ada_mods(c, ada_w, ada_b)

    sh1, sc1, g1, sh2, sc2, g2 = (mods[0, k] for k in range(6))
    qa, kd, vd, qn, kn, vb, gs, beta, g = _in_proj0(
        x2d, norm_mix_w[0], sc1, sh1, ab_w_in[0], dn_conv_w[0], dn_a_log[0], dn_dt_bias[0], seq_len)
    attn = _attention(qa, kd, vd, _bias_table(rel_bias), attn_sinks[0], seq_len)
    dn = _deltanet(qn, kn, vb, gs, beta, g, dn_norm_w[0], seq_len)
    x2 = _mid0(attn, dn, x2d, ab_w_out[0], g1, norm_ffn_w[0], sc2, sh2, g2,
               ffn_w_gate[0], ffn_w_up[0], ffn_w_down[0], seq_len)

    sh1, sc1, g1, sh2, sc2, g2 = (mods[1, k] for k in range(6))
    cat = _mix1(x2, norm_mix_w[1], sc1, sh1, cd_w_in[0], lru_conv_w[0], lru_conv_b[0],
                lru_gate_a_w[0], lru_gate_a_b[0], lru_gate_x_w[0], lru_gate_x_b[0],
                lru_lambda[0], sconv_w[0], seq_len)
    x3, hn4, meta, wt, counts = _route(cat, x2, cd_w_out[0], g1, norm_ffn_w[1], sc2, sh2,
                                       moe_router_w[0], moe_router_b[0], seq_len)
    d1, d2, items = _moe_tables(meta, counts)
    xs = _dispatch(hn4, d1, d2, 2 * n)
    ys = _moe_ffn(xs, items, moe_w_gate[0], moe_w_up[0], moe_w_down[0])
    out = _combine(ys, d1, d2, x3, wt, g2, final_norm_w, seq_len)
    return out.reshape(bsz, seq_len, d)
```

```python
import functools
import math

import numpy as np
import jax
import jax.numpy as jnp
from jax import lax
from jax.experimental import pallas as pl
from jax.experimental.pallas import tpu as pltpu

D_MODEL = 1024
EPS = 1e-6
HEAD_DIM = 64
A_Q_HEADS = 8
A_KV_HEADS = 2
WINDOW = 128
N_BUCKETS = 32
MAX_DISTANCE = 128
B_HEADS = 8
B_CONV = 4
CHUNK = 64
A_Q_W = A_Q_HEADS * HEAD_DIM
A_KV_W = A_KV_HEADS * HEAD_DIM
B_W = B_HEADS * HEAD_DIM
B_QKV_W = 3 * B_W
LRU_WIDTH = D_MODEL
LRU_BLOCKS = 8
LRU_C = 8.0
SC_WIDTH = D_MODEL // 2
D_FF = 2816
N_EXPERTS = 8
D_FF_EXPERT = 3584

LANES = 128
SUBLANES = 8
VMEM_LIMIT_BYTES = 56 * 1024 * 1024
TOKEN_TILE = 512
MOE_TILE = 1024
MOE_SUB = 256
MOE_FF_TILE = 512
ROW_DMA_TILE = 256
NEG_BIG = -1e30

F32 = jnp.float32
BF16 = jnp.bfloat16


def _cparams(*sem):
    return pltpu.CompilerParams(dimension_semantics=tuple(sem), vmem_limit_bytes=VMEM_LIMIT_BYTES)


def _const_spec(shape):
    nd = len(shape)
    return pl.BlockSpec(shape, lambda *_: (0,) * nd)


def _bdot(a, b):
    return jnp.dot(a.astype(BF16), b.astype(BF16), preferred_element_type=F32)


def _bdot_nt(a, b):
    return lax.dot_general(a.astype(BF16), b.astype(BF16), (((1,), (1,)), ((), ())),
                           preferred_element_type=F32)


def _bdot_tn(a, b):
    return lax.dot_general(a.astype(BF16), b.astype(BF16), (((0,), (0,)), ((), ())),
                           preferred_element_type=F32)


def _split(x, n):
    parts = []
    r = x
    for i in range(n):
        p = r.astype(BF16)
        parts.append(p)
        if i + 1 < n:
            r = r - p.astype(F32)
    return parts


def _dot_x(a, b, na=2, nb=2):
    asp = _split(a, na) if na > 1 else [a.astype(BF16)]
    bsp = _split(b, nb) if nb > 1 else [b.astype(BF16)]
    acc = None
    for i, ai in enumerate(asp):
        for j, bj in enumerate(bsp):
            if i + j >= max(na, nb):
                continue
            t = jnp.dot(ai, bj, preferred_element_type=F32)
            acc = t if acc is None else acc + t
    return acc


def _silu(x):
    return x * (1.0 / (1.0 + jnp.exp(-x)))


def _sigmoid(x):
    return 1.0 / (1.0 + jnp.exp(-x))


def _log1p(z):
    u = 1.0 + z
    tiny = u == 1.0
    return jnp.where(tiny, z, jnp.log(u) * (z / jnp.where(tiny, 1.0, u - 1.0)))


def _softplus(x):
    return jnp.maximum(x, 0.0) + _log1p(jnp.exp(-jnp.abs(x)))


def _neg_expm1(y):
    return -jnp.tanh(0.5 * y) * (jnp.exp(y) + 1.0)


def _norm_mod(x, w, sc, sh):
    ms = jnp.mean(x * x, axis=-1, keepdims=True)
    return (x * lax.rsqrt(ms + EPS)) * w * (1.0 + sc) + sh


def _shift_rows(x, k, prev_tail):
    xs = pltpu.roll(x, k, 0)
    head = pltpu.roll(prev_tail, k, 0)
    row = lax.broadcasted_iota(jnp.int32, (SUBLANES, x.shape[1]), 0)
    top = jnp.where(row < k, head, xs[:SUBLANES])
    return jnp.concatenate([top, xs[SUBLANES:]], axis=0)


def _ada_kernel(c_ref, w_ref, b_ref, o_ref):
    c = c_ref[...]
    cond = _silu(c)
    o_ref[0] = _dot_x(cond, w_ref[0], 3, 3) + b_ref[0]


def _ada_mods(c, ada_w, ada_b):
    depth, d, six_d = ada_w.shape
    bsz = c.shape[0]
    rows = max(SUBLANES, bsz)
    c_pad = jnp.zeros((rows, d), F32).at[:bsz].set(c)
    tn = 1536
    out = pl.pallas_call(
        _ada_kernel,
        grid=(depth, six_d // tn),
        in_specs=[pl.BlockSpec((rows, d), lambda l, j: (0, 0)),
                  pl.BlockSpec((1, d, tn), lambda l, j: (l, 0, j)),
                  pl.BlockSpec((1, 1, tn), lambda l, j: (l, 0, j))],
        out_specs=pl.BlockSpec((1, rows, tn), lambda l, j: (l, 0, j)),
        out_shape=jax.ShapeDtypeStruct((depth, rows, six_d), F32),
        compiler_params=_cparams("parallel", "parallel"),
        name="ada_mods",
    )(c_pad, ada_w, ada_b.reshape(depth, 1, six_d))
    return out[:, :bsz].reshape(depth, bsz, 6, 1, d).transpose(0, 2, 1, 3, 4)


def _t5_bucket(dist):
    max_exact = N_BUCKETS // 2
    d = np.maximum(dist, 0)
    large = max_exact + (np.log(np.maximum(d, 1) / max_exact) / math.log(MAX_DISTANCE / max_exact)
                         * (N_BUCKETS - max_exact)).astype(np.int32)
    large = np.minimum(large, N_BUCKETS - 1)
    return np.where(d < max_exact, d, large).astype(np.int32)


def _band_buckets():
    qi = np.arange(WINDOW)[:, None]
    s = np.arange(2 * WINDOW)[None, :]
    dist = qi + WINDOW - s
    in_window = (dist >= 0) & (dist < WINDOW)
    return np.where(in_window, _t5_bucket(dist), -1).astype(np.int32)


def _bias_kernel(rb_ref, bucket_ref, o_ref):
    h = pl.program_id(0)
    bucket = bucket_ref[...]
    acc = jnp.zeros(bucket.shape, F32)
    for b in range(N_BUCKETS):
        acc = jnp.where(bucket == b, rb_ref[b, h], acc)
    o_ref[0] = jnp.where(bucket < 0, NEG_BIG, acc)


def _bias_table(rel_bias):
    bucket = jnp.asarray(_band_buckets())
    out = pl.pallas_call(
        _bias_kernel,
        grid=(A_Q_HEADS,),
        in_specs=[pl.BlockSpec(memory_space=pltpu.SMEM),
                  _const_spec((WINDOW, 2 * WINDOW))],
        out_specs=pl.BlockSpec((1, WINDOW, 2 * WINDOW), lambda h: (h, 0, 0)),
        out_shape=jax.ShapeDtypeStruct((A_Q_HEADS, WINDOW, 2 * WINDOW), F32),
        compiler_params=_cparams("parallel"),
        name="attn_bias_table",
    )(rel_bias, bucket)
    return out.reshape(A_Q_HEADS // 2, 2 * WINDOW, 2 * WINDOW)


_C_QA = 0
_C_KD = _C_QA + A_Q_W
_C_VD = _C_KD + 2 * A_KV_W
_C_QKV = _C_VD + 2 * A_KV_W
_C_GATE = _C_QKV + B_QKV_W
_C_SMALL = _C_GATE + B_W
_AB_COLS = _C_SMALL + LANES


def _ab_in_weight(w_in):
    qa, ka, va, qkv, gate, beta, dec = jnp.split(
        w_in, list(np.cumsum([A_Q_W, A_KV_W, A_KV_W, B_QKV_W, B_W, B_HEADS])), axis=1)

    def dup(t):
        return jnp.concatenate([t[:, :HEAD_DIM]] * 2 + [t[:, HEAD_DIM:]] * 2, axis=1)

    small = jnp.zeros((w_in.shape[0], LANES), w_in.dtype)
    small = small.at[:, :B_HEADS].set(beta).at[:, B_HEADS:2 * B_HEADS].set(dec)
    return jnp.concatenate([qa, dup(ka), dup(va), qkv, gate, small], axis=1).astype(BF16)


def _head_selector():
    e = np.zeros((B_W, LANES), np.float32)
    for h in range(B_HEADS):
        e[h * HEAD_DIM:(h + 1) * HEAD_DIM, h] = 1.0
    return e


def _in0_kernel(x_ref, nw_ref, sc_ref, sh_ref, w_ref, cw_ref, sel_ref, selt_ref, alog_ref, dtb_ref,
                qa_ref, kd_ref, vd_ref, qn_ref, kn_ref, vb_ref, gs_ref, beta_ref, g_ref,
                tail_ref, *, tiles_per_seq):
    i = pl.program_id(0)

    @pl.when(i % tiles_per_seq == 0)
    def _():
        tail_ref[...] = jnp.zeros_like(tail_ref)

    hn = _norm_mod(x_ref[...], nw_ref[...], sc_ref[0], sh_ref[0])
    proj = jnp.dot(hn.astype(BF16), w_ref[...], preferred_element_type=F32)
    qa_ref[...] = proj[:, _C_QA:_C_KD].astype(BF16)
    kd_ref[...] = proj[:, _C_KD:_C_VD].astype(BF16)
    vd_ref[...] = proj[:, _C_VD:_C_QKV].astype(BF16)

    xq = proj[:, _C_QKV:_C_GATE]
    tail = tail_ref[...]
    cw = cw_ref[...]
    y = xq * cw[B_CONV - 1:B_CONV]
    for k in range(1, B_CONV):
        y = y + _shift_rows(xq, k, tail) * cw[B_CONV - 1 - k:B_CONV - k]
    tail_ref[...] = xq[xq.shape[0] - SUBLANES:]
    y = _silu(y)
    q, k_, v = y[:, :B_W], y[:, B_W:2 * B_W], y[:, 2 * B_W:]

    def l2n(t):
        ssq = _dot_x(t * t, sel_ref[...], 2, 1)
        r = lax.rsqrt(ssq + EPS)
        return t * _dot_x(r, selt_ref[...], 2, 1)

    qn_ref[...] = l2n(q) * (HEAD_DIM ** -0.5)
    kn_ref[...] = l2n(k_)
    vb_ref[...] = v
    gs_ref[...] = _silu(proj[:, _C_GATE:_C_SMALL])
    small = proj[:, _C_SMALL:]
    lane = lax.broadcasted_iota(jnp.int32, small.shape, 1)
    beta_ref[...] = jnp.where(lane < B_HEADS, _sigmoid(small), 0.0)
    dec = pltpu.roll(small, LANES - B_HEADS, 1)
    g = -jnp.exp(alog_ref[...]) * _softplus(dec + dtb_ref[...])
    g_ref[...] = jnp.where(lane < B_HEADS, g, 0.0)


def _in_proj0(x2d, nw, sc, sh, w_in, conv_w, a_log, dt_bias, seq_len):
    n, d = x2d.shape
    tm = TOKEN_TILE
    tiles_per_seq = seq_len // tm
    w = _ab_in_weight(w_in)
    sel = jnp.asarray(_head_selector(), BF16)
    selt = jnp.asarray(_head_selector().T.copy(), BF16)
    pad8 = lambda v: jnp.zeros((1, LANES), F32).at[0, :B_HEADS].set(v)
    row = lambda width: pl.BlockSpec((tm, width), lambda i: (i, 0))
    per_b = pl.BlockSpec((1, 1, d), lambda i: (i // tiles_per_seq, 0, 0))
    outs = pl.pallas_call(
        functools.partial(_in0_kernel, tiles_per_seq=tiles_per_seq),
        grid=(n // tm,),
        in_specs=[row(d), _const_spec((1, d)), per_b, per_b,
                  _const_spec((d, _AB_COLS)), _const_spec((B_CONV, B_QKV_W)),
                  _const_spec((B_W, LANES)), _const_spec((LANES, B_W)),
                  _const_spec((1, LANES)), _const_spec((1, LANES))],
        out_specs=[row(A_Q_W), row(2 * A_KV_W), row(2 * A_KV_W),
                   row(B_W), row(B_W), row(B_W), row(B_W), row(LANES), row(LANES)],
        out_shape=[jax.ShapeDtypeStruct((n, A_Q_W), BF16),
                   jax.ShapeDtypeStruct((n, 2 * A_KV_W), BF16),
                   jax.ShapeDtypeStruct((n, 2 * A_KV_W), BF16),
                   jax.ShapeDtypeStruct((n, B_W), F32),
                   jax.ShapeDtypeStruct((n, B_W), F32),
                   jax.ShapeDtypeStruct((n, B_W), F32),
                   jax.ShapeDtypeStruct((n, B_W), F32),
                   jax.ShapeDtypeStruct((n, LANES), F32),
                   jax.ShapeDtypeStruct((n, LANES), F32)],
        scratch_shapes=[pltpu.VMEM((SUBLANES, B_QKV_W), F32)],
        compiler_params=_cparams("arbitrary"),
        name="in_proj0",
    )(x2d, nw.reshape(1, d), sc, sh, w, conv_w, sel, selt, pad8(a_log), pad8(dt_bias))
    return outs


def _attn_kernel(sink_ref, q_ref, kp_ref, kc_ref, vp_ref, vc_ref, bm_ref, o_ref, *, blocks_per_seq):
    i = pl.program_id(0)
    first = (i % blocks_per_seq) == 0
    w = WINDOW
    lane = lax.broadcasted_iota(jnp.int32, (w, LANES), 1)
    low = lane < HEAD_DIM
    col = lax.broadcasted_iota(jnp.int32, (2 * w, 2 * w), 1)
    row = lax.broadcasted_iota(jnp.int32, (2 * w, 1), 0)
    prev_dead = jnp.logical_and(first, col < w)
    q_all = q_ref[...]
    zero = jnp.zeros((), q_all.dtype)
    outs = []
    for j in range(A_Q_HEADS // 2):
        kh = (2 * j) // (A_Q_HEADS // A_KV_HEADS)
        qp = q_all[:, j * LANES:(j + 1) * LANES]
        qs = jnp.concatenate([jnp.where(low, qp, zero), jnp.where(low, zero, qp)], axis=0)
        kd = jnp.concatenate([kp_ref[:, kh * LANES:(kh + 1) * LANES],
                              kc_ref[:, kh * LANES:(kh + 1) * LANES]], axis=0)
        vd = jnp.concatenate([vp_ref[:, kh * LANES:(kh + 1) * LANES],
                              vc_ref[:, kh * LANES:(kh + 1) * LANES]], axis=0)
        s = lax.dot_general(qs, kd, (((1,), (1,)), ((), ())), preferred_element_type=F32)
        s = s * (HEAD_DIM ** -0.5) + bm_ref[j]
        s = jnp.where(prev_dead, NEG_BIG, s)
        sink = jnp.where(row < w, sink_ref[2 * j], sink_ref[2 * j + 1])
        m = jnp.maximum(jnp.max(s, axis=-1, keepdims=True), sink)
        p = jnp.exp(s - m)
        denom = jnp.sum(p, axis=-1, keepdims=True) + jnp.exp(sink - m)
        pv = jnp.dot(p.astype(BF16), vd, preferred_element_type=F32) / denom
        outs.append(jnp.where(low, pv[:w], pv[w:]))
    o_ref[...] = jnp.concatenate(outs, axis=1).astype(o_ref.dtype)


def _attention(qa, kd, vd, bias_tbl, sinks, seq_len):
    n = qa.shape[0]
    w = WINDOW
    nb = seq_len // w
    cur = lambda i: (i, 0)
    prev = lambda i: (jnp.where(i % nb == 0, i, i - 1), 0)
    return pl.pallas_call(
        functools.partial(_attn_kernel, blocks_per_seq=nb),
        grid=(n // w,),
        in_specs=[pl.BlockSpec(memory_space=pltpu.SMEM),
                  pl.BlockSpec((w, A_Q_W), cur),
                  pl.BlockSpec((w, 2 * A_KV_W), prev), pl.BlockSpec((w, 2 * A_KV_W), cur),
                  pl.BlockSpec((w, 2 * A_KV_W), prev), pl.BlockSpec((w, 2 * A_KV_W), cur),
                  _const_spec((A_Q_HEADS // 2, 2 * w, 2 * w))],
        out_specs=pl.BlockSpec((w, A_Q_W), cur),
        out_shape=jax.ShapeDtypeStruct((n, A_Q_W), BF16),
        compiler_params=_cparams("parallel"),
        name="swa_attention",
    )(sinks, qa, kd, kd, vd, vd, bias_tbl)


_DN_PAIRS = B_HEADS // 2
_DN_INV_BLOCK = 16
_DN_INV_SPLIT = 2


def _block_diag(x, low):
    zero = jnp.zeros((), x.dtype)
    return jnp.concatenate([jnp.where(low, x, zero), jnp.where(low, zero, x)], axis=0)


def _dn_chunk(q, k, v, b, g, s_bd, consts):
    (low, i_idx, j_idx, tril, ones_c, mask_bd) = consts
    c = CHUNK
    causal = i_idx >= j_idx
    strict = i_idx > j_idx
    eye = (i_idx == j_idx).astype(F32)
    blk_shift = int(math.log2(_DN_INV_BLOCK))
    same_blk = (i_idx >> blk_shift) == (j_idx >> blk_shift)

    gc = _dot_x(tril, g, 1, 3)
    g_upper = jnp.where(i_idx <= j_idx, g, 0.0)
    gr = _dot_x(ones_c, g_upper, 1, 3)
    decay = jnp.exp(jnp.where(causal, gc - gr, NEG_BIG))
    g_last = gc[c - 1:c, :]

    ks = _block_diag(k, low)
    kk = _bdot_nt(k, ks)
    lmat = jnp.where(strict, b * kk * decay, 0.0)

    def mm(x, y):
        return _dot_x(x, _block_diag(y, low), _DN_INV_SPLIT, _DN_INV_SPLIT)

    l_diag = jnp.where(same_blk, lmat, 0.0)
    l_off = lmat - l_diag
    nil = -l_diag
    d_inv = eye + nil
    pw = nil
    steps = int(math.log2(_DN_INV_BLOCK)) - 1
    for _ in range(steps):
        pw = mm(pw, pw)
        d_inv = mm(d_inv, eye + pw)
    m1 = -mm(d_inv, l_off)
    acc = eye + m1
    pw = m1
    for _ in range(int(math.log2(c // _DN_INV_BLOCK)) - 1):
        pw = mm(pw, pw)
        acc = mm(acc, eye + pw)
    tmat = mm(acc, d_inv)

    u = _bdot(tmat, _block_diag(v * b, low))
    w = _bdot(tmat, _block_diag(k * (b * jnp.exp(gc)), low))
    qk = jnp.where(causal, _bdot_nt(q, ks) * decay, 0.0)
    k_dec = k * jnp.exp(g_last - gc)
    q_dec = q * jnp.exp(gc)

    v_new = u - _bdot(w, s_bd)
    o = _bdot(q_dec, s_bd) + _bdot(qk, _block_diag(v_new, low))
    s_new = s_bd * jnp.exp(g_last) + jnp.where(mask_bd, _bdot_tn(k_dec, v_new), 0.0)
    return o, s_new


def _dn_kernel(qn_ref, kn_ref, vb_ref, gs_ref, beta_ref, g_ref, selt_ref, nw_ref, o_ref,
               s_ref, bexp_ref, gexp_ref, *, groups_per_seq):
    i = pl.program_id(0)

    @pl.when(i % groups_per_seq == 0)
    def _():
        s_ref[...] = jnp.zeros_like(s_ref)

    bexp_ref[...] = _dot_x(beta_ref[...], selt_ref[...], 3, 1)
    gexp_ref[...] = _dot_x(g_ref[...], selt_ref[...], 3, 1)

    c = CHUNK
    lane = lax.broadcasted_iota(jnp.int32, (c, LANES), 1)
    low = lane < HEAD_DIM
    i_idx = lax.broadcasted_iota(jnp.int32, (c, LANES), 0)
    j_idx = lane & (c - 1)
    r2 = lax.broadcasted_iota(jnp.int32, (c, c), 0)
    c2 = lax.broadcasted_iota(jnp.int32, (c, c), 1)
    tril = (r2 >= c2).astype(BF16)
    ones_c = jnp.ones((c, c), BF16)
    rb = lax.broadcasted_iota(jnp.int32, (LANES, LANES), 0)
    cb = lax.broadcasted_iota(jnp.int32, (LANES, LANES), 1)
    mask_bd = (rb < HEAD_DIM) == (cb < HEAD_DIM)
    head_mean = jnp.where(mask_bd, 1.0 / HEAD_DIM, 0.0).astype(BF16)
    consts = (low, i_idx, j_idx, tril, ones_c, mask_bd)
    nw = nw_ref[...]

    def chunk_body(ci, carry):
        r0 = pl.multiple_of(ci * c, c)
        rows = pl.ds(r0, c)
        for p in range(_DN_PAIRS):
            ls = slice(p * LANES, (p + 1) * LANES)
            o, s_new = _dn_chunk(qn_ref[rows, ls], kn_ref[rows, ls], vb_ref[rows, ls],
                                 bexp_ref[rows, ls], gexp_ref[rows, ls], s_ref[p], consts)
            s_ref[p] = s_new
            ms = _dot_x(o * o, head_mean, 2, 1)
            y = (o * lax.rsqrt(ms + EPS)) * nw * gs_ref[rows, ls]
            o_ref[rows, ls] = y.astype(o_ref.dtype)
        return carry

    lax.fori_loop(0, o_ref.shape[0] // c, chunk_body, 0)


def _deltanet(qn, kn, vb, gs, beta, g, norm_w, seq_len):
    n = qn.shape[0]
    tm = TOKEN_TILE
    selt = jnp.asarray(_head_selector().T.copy(), BF16)
    nw2 = jnp.concatenate([norm_w, norm_w]).reshape(1, LANES)
    row = lambda width: pl.BlockSpec((tm, width), lambda i: (i, 0))
    return pl.pallas_call(
        functools.partial(_dn_kernel, groups_per_seq=seq_len // tm),
        grid=(n // tm,),
        in_specs=[row(B_W), row(B_W), row(B_W), row(B_W), row(LANES), row(LANES),
                  _const_spec((LANES, B_W)), _const_spec((1, LANES))],
        out_specs=row(B_W),
        out_shape=jax.ShapeDtypeStruct((n, B_W), BF16),
        scratch_shapes=[pltpu.VMEM((_DN_PAIRS, LANES, LANES), F32),
                        pltpu.VMEM((tm, B_W), F32), pltpu.VMEM((tm, B_W), F32)],
        compiler_params=_cparams("arbitrary"),
        name="gated_deltanet",
    )(qn, kn, vb, gs, beta, g, selt, nw2)


def _resident_spec(shape):
    nd = len(shape)
    return pl.BlockSpec(shape, lambda *_: (0,) * nd, pipeline_mode=pl.Buffered(1))


def _mid0_kernel(attn_ref, dn_ref, x_ref, wo_ref, g1_ref, nw_ref, sc_ref, sh_ref, g2_ref,
                 wg_ref, wu_ref, wd_ref, o_ref):
    mix = (jnp.dot(attn_ref[...], wo_ref[:A_Q_W], preferred_element_type=F32)
           + jnp.dot(dn_ref[...], wo_ref[A_Q_W:], preferred_element_type=F32))
    x1 = x_ref[...] + g1_ref[0] * mix
    hn = _norm_mod(x1, nw_ref[...], sc_ref[0], sh_ref[0]).astype(BF16)
    hg = jnp.dot(hn, wg_ref[...], preferred_element_type=F32)
    hu = jnp.dot(hn, wu_ref[...], preferred_element_type=F32)
    act = (_silu(hg) * hu).astype(BF16)
    o_ref[...] = x1 + g2_ref[0] * jnp.dot(act, wd_ref[...], preferred_element_type=F32)


def _mid0(attn, dn, x2d, w_out, g1, nw, sc, sh, g2, wg, wu, wd, seq_len):
    n, d = x2d.shape
    tm = TOKEN_TILE
    tps = seq_len // tm
    row = lambda width: pl.BlockSpec((tm, width), lambda i: (i, 0))
    per_b = pl.BlockSpec((1, 1, d), lambda i: (i // tps, 0, 0))
    return pl.pallas_call(
        _mid0_kernel,
        grid=(n // tm,),
        in_specs=[row(A_Q_W), row(B_W), row(d), _resident_spec(w_out.shape), per_b,
                  _const_spec((1, d)), per_b, per_b, per_b,
                  _resident_spec(wg.shape), _resident_spec(wu.shape), _resident_spec(wd.shape)],
        out_specs=row(d),
        out_shape=jax.ShapeDtypeStruct((n, d), F32),
        compiler_params=_cparams("parallel"),
        name="out_proj0_swiglu",
    )(attn, dn, x2d, w_out.astype(BF16), g1, nw.reshape(1, d), sc, sh, g2,
      wg.astype(BF16), wu.astype(BF16), wd.astype(BF16))


def _gelu_tanh(x):
    return 0.5 * x * (1.0 + jnp.tanh(math.sqrt(2.0 / math.pi) * (x + 0.044715 * (x * x * x))))


def _linear_scan(a, b):
    n = a.shape[0]
    row = lax.broadcasted_iota(jnp.int32, a.shape, 0)
    s = 1
    while s < n:
        a_sh = pltpu.roll(a, s, 0)
        b_sh = pltpu.roll(b, s, 0)
        valid = row >= s
        b = jnp.where(valid, a * b_sh + b, b)
        a = jnp.where(valid, a * a_sh, a)
        s *= 2
    return a, b


def _mix1_kernel(x_ref, nw_ref, sc_ref, sh_ref, w_ref, cw_ref, cb_ref, ga_ref, gab_ref, gx_ref, gxb_ref,
                 lam_ref, sw_ref, o_ref, tail_c_ref, tail_d_ref, h_ref, *, tiles_per_seq):
    i = pl.program_id(0)

    @pl.when(i % tiles_per_seq == 0)
    def _():
        tail_c_ref[...] = jnp.zeros_like(tail_c_ref)
        tail_d_ref[...] = jnp.zeros_like(tail_d_ref)
        h_ref[...] = jnp.zeros_like(h_ref)

    hn = _norm_mod(x_ref[...], nw_ref[...], sc_ref[0], sh_ref[0]).astype(BF16)
    proj = jnp.dot(hn, w_ref[...], preferred_element_type=F32)
    w_l = LRU_WIDTH
    xc_in = proj[:, :w_l]
    yc = proj[:, w_l:2 * w_l]
    bd = proj[:, 2 * w_l:2 * w_l + SC_WIDTH]
    cd = proj[:, 2 * w_l + SC_WIDTH:2 * w_l + 2 * SC_WIDTH]
    hd = proj[:, 2 * w_l + 2 * SC_WIDTH:]
    tm = xc_in.shape[0]

    kc = cw_ref.shape[0]
    tail = tail_c_ref[...]
    cw = cw_ref[...]
    xc = xc_in * cw[kc - 1:kc] + cb_ref[...]
    for k in range(1, kc):
        xc = xc + _shift_rows(xc_in, k, tail) * cw[kc - 1 - k:kc - k]
    tail_c_ref[...] = xc_in[tm - SUBLANES:]

    xb = xc.astype(BF16)
    gw = ga_ref.shape[1]
    ra, ri = [], []
    for p in range(ga_ref.shape[0]):
        xin = xb[:, p * gw:(p + 1) * gw]
        ra.append(jnp.dot(xin, ga_ref[p], preferred_element_type=F32))
        ri.append(jnp.dot(xin, gx_ref[p], preferred_element_type=F32))
    r = _sigmoid(jnp.concatenate(ra, axis=1) + gab_ref[...])
    ig = _sigmoid(jnp.concatenate(ri, axis=1) + gxb_ref[...])
    log_a = (-LRU_C) * r * _softplus(-lam_ref[...])
    a = jnp.exp(log_a)
    b = jnp.sqrt(_neg_expm1(2.0 * log_a)) * (ig * xc)
    a_cum, h_loc = _linear_scan(a, b)
    h = a_cum * h_ref[0:1, :] + h_loc
    h_ref[...] = jnp.broadcast_to(h[tm - 1:tm, :], h_ref.shape)
    yc_out = h * _gelu_tanh(yc)

    ks = sw_ref.shape[0]
    ch = cd * hd
    tail_d = tail_d_ref[...]
    sw = sw_ref[...]
    conv = ch * sw[ks - 1:ks]
    for k in range(1, ks):
        conv = conv + _shift_rows(ch, k, tail_d) * sw[ks - 1 - k:ks - k]
    tail_d_ref[...] = ch[tm - SUBLANES:]
    o_ref[...] = jnp.concatenate([yc_out, bd * conv], axis=1).astype(o_ref.dtype)


def _pair_block_diag(gw):
    nb, bw, _ = gw.shape
    g2 = gw.reshape(nb // 2, 2, bw, bw)
    z = jnp.zeros((nb // 2, bw, bw), gw.dtype)
    top = jnp.concatenate([g2[:, 0], z], axis=2)
    bot = jnp.concatenate([z, g2[:, 1]], axis=2)
    return jnp.concatenate([top, bot], axis=1).astype(BF16)


def _mix1(x2d, nw, sc, sh, w_in, conv_w, conv_b, ga_w, ga_b, gx_w, gx_b, lam, sconv_w, seq_len):
    n, d = x2d.shape
    tm = TOKEN_TILE
    tps = seq_len // tm
    cd_in = w_in.shape[1]
    cd_out = LRU_WIDTH + SC_WIDTH
    row = lambda width: pl.BlockSpec((tm, width), lambda i: (i, 0))
    per_b = pl.BlockSpec((1, 1, d), lambda i: (i // tps, 0, 0))
    ga = _pair_block_diag(ga_w)
    gx = _pair_block_diag(gx_w)
    vec = lambda v: v.reshape(1, -1)
    return pl.pallas_call(
        functools.partial(_mix1_kernel, tiles_per_seq=tps),
        grid=(n // tm,),
        in_specs=[row(d), _const_spec((1, d)), per_b, per_b, _resident_spec((d, cd_in)),
                  _const_spec(conv_w.shape), _const_spec((1, LRU_WIDTH)),
                  _const_spec(ga.shape), _const_spec((1, LRU_WIDTH)),
                  _const_spec(gx.shape), _const_spec((1, LRU_WIDTH)),
                  _const_spec((1, LRU_WIDTH)), _const_spec(sconv_w.shape)],
        out_specs=row(cd_out),
        out_shape=jax.ShapeDtypeStruct((n, cd_out), BF16),
        scratch_shapes=[pltpu.VMEM((SUBLANES, LRU_WIDTH), F32), pltpu.VMEM((SUBLANES, SC_WIDTH), F32),
                        pltpu.VMEM((SUBLANES, LRU_WIDTH), F32)],
        compiler_params=_cparams("arbitrary"),
        name="rglru_shortconv_mixer",
    )(x2d, vec(nw), sc, sh, w_in.astype(BF16), conv_w, vec(conv_b), ga, vec(ga_b), gx, vec(gx_b),
      vec(lam), sconv_w)


def _route_kernel(cat_ref, x_ref, wo_ref, g1_ref, nw_ref, sc_ref, sh_ref, rw_ref, rb_ref,
                  x3_ref, hn_ref, meta_ref, wt_ref, cnt_ref, carry_ref):
    i = pl.program_id(0)

    @pl.when(i == 0)
    def _():
        carry_ref[...] = jnp.zeros_like(carry_ref)

    x3 = x_ref[...] + g1_ref[0] * jnp.dot(cat_ref[...], wo_ref[...], preferred_element_type=F32)
    x3_ref[...] = x3
    hn = _norm_mod(x3, nw_ref[...], sc_ref[0], sh_ref[0])
    hn_ref[...] = hn
    tm = hn.shape[0]
    lane = lax.broadcasted_iota(jnp.int32, (tm, LANES), 1)
    logits = _dot_x(hn, rw_ref[...], 2, 2) + rb_ref[...]
    lg = jnp.where(lane < N_EXPERTS, logits, NEG_BIG)
    m1 = jnp.max(lg, axis=1, keepdims=True)
    i1 = jnp.min(jnp.where(lg == m1, lane, LANES), axis=1, keepdims=True)
    lg2 = jnp.where(lane == i1, NEG_BIG, lg)
    m2 = jnp.max(lg2, axis=1, keepdims=True)
    i2 = jnp.min(jnp.where(lg2 == m2, lane, LANES), axis=1, keepdims=True)
    e2 = jnp.exp(m2 - m1)
    w1 = 1.0 / (1.0 + e2)
    w2 = e2 / (1.0 + e2)

    hit1 = lane == i1
    hit2 = lane == i2
    sel = jnp.logical_or(hit1, hit2).astype(F32)
    r_i = lax.broadcasted_iota(jnp.int32, (tm, tm), 0)
    c_i = lax.broadcasted_iota(jnp.int32, (tm, tm), 1)
    tril = (r_i >= c_i).astype(BF16)
    incl = jnp.dot(tril, sel.astype(BF16), preferred_element_type=F32)
    carry = carry_ref[0:1, :]
    excl = incl - sel + carry
    r1 = jnp.sum(jnp.where(hit1, excl, 0.0), axis=1, keepdims=True)
    r2 = jnp.sum(jnp.where(hit2, excl, 0.0), axis=1, keepdims=True)
    total = carry + incl[tm - 1:tm, :]
    carry_ref[...] = jnp.broadcast_to(total, carry_ref.shape)
    cnt_ref[...] = jnp.broadcast_to(total, cnt_ref.shape).astype(jnp.int32)

    meta = jnp.where(lane == 0, i1, 0)
    meta = jnp.where(lane == 1, i2, meta)
    meta = jnp.where(lane == 2, r1.astype(jnp.int32), meta)
    meta = jnp.where(lane == 3, r2.astype(jnp.int32), meta)
    meta_ref[...] = meta
    wt_ref[...] = jnp.where(lane == 0, w1, jnp.where(lane == 1, w2, 0.0))


def _route(cat, x2d, w_out, g1, nw, sc, sh, router_w, router_b, seq_len):
    n, d = x2d.shape
    tm = TOKEN_TILE
    tps = seq_len // tm
    row = lambda width: pl.BlockSpec((tm, width), lambda i: (i, 0))
    per_b = pl.BlockSpec((1, 1, d), lambda i: (i // tps, 0, 0))
    rw = jnp.zeros((d, LANES), F32).at[:, :N_EXPERTS].set(router_w)
    rb = jnp.zeros((1, LANES), F32).at[0, :N_EXPERTS].set(router_b)
    return pl.pallas_call(
        _route_kernel,
        grid=(n // tm,),
        in_specs=[row(cat.shape[1]), row(d), _resident_spec(w_out.shape), per_b, _const_spec((1, d)),
                  per_b, per_b, _const_spec((d, LANES)), _const_spec((1, LANES))],
        out_specs=[row(d), row(d), row(LANES), row(LANES), _const_spec((SUBLANES, LANES))],
        out_shape=[jax.ShapeDtypeStruct((n, d), F32), jax.ShapeDtypeStruct((n, d), F32),
                   jax.ShapeDtypeStruct((n, LANES), jnp.int32), jax.ShapeDtypeStruct((n, LANES), F32),
                   jax.ShapeDtypeStruct((SUBLANES, LANES), jnp.int32)],
        scratch_shapes=[pltpu.VMEM((SUBLANES, LANES), F32)],
        compiler_params=_cparams("arbitrary"),
        name="out_proj1_router",
    )(cat, x2d, w_out.astype(BF16), g1, nw.reshape(1, d), sc, sh, rw, rb)


def _dispatch_kernel(d1_ref, d2_ref, hn_ref, xs_ref, sem):
    i = pl.program_id(0)
    tr = ROW_DMA_TILE
    base = i * tr

    def copies(t):
        src = hn_ref.at[pl.ds(t, 1)]
        return (pltpu.make_async_copy(src, xs_ref.at[pl.ds(d1_ref[base + t], 1)], sem),
                pltpu.make_async_copy(src, xs_ref.at[pl.ds(d2_ref[base + t], 1)], sem))

    def start(t, c):
        a, b = copies(t)
        a.start()
        b.start()
        return c

    def wait(t, c):
        a, b = copies(t)
        a.wait()
        b.wait()
        return c

    lax.fori_loop(0, tr, start, 0)
    lax.fori_loop(0, tr, wait, 0)


def _dispatch(hn, d1, d2, rows_out):
    n, d = hn.shape
    return pl.pallas_call(
        _dispatch_kernel,
        grid_spec=pltpu.PrefetchScalarGridSpec(
            num_scalar_prefetch=2,
            grid=(n // ROW_DMA_TILE,),
            in_specs=[pl.BlockSpec((ROW_DMA_TILE, d), lambda i, a, b: (i, 0))],
            out_specs=pl.BlockSpec(memory_space=pl.ANY),
            scratch_shapes=[pltpu.SemaphoreType.DMA(())]),
        out_shape=jax.ShapeDtypeStruct((rows_out, d), hn.dtype),
        compiler_params=_cparams("arbitrary"),
        name="moe_dispatch",
    )(d1, d2, hn)


def _moe_kernel(ti_ref, te_ref, tv_ref, lo_ref, hi_ref, x_ref, wg_ref, wu_ref, wd_ref, o_ref,
                xb_ref, wgb_ref, wub_ref, wdb_ref):
    w = pl.program_id(0)
    f = pl.program_id(1)
    sub = MOE_SUB
    sub_shift = int(math.log2(sub))

    @pl.when(tv_ref[w] == 1)
    def _():
        lo = lo_ref[w]
        hi = hi_ref[w]

        @pl.when(f == 0)
        def _():
            row = lax.broadcasted_iota(jnp.int32, (x_ref.shape[0], 1), 0)
            mine = jnp.logical_and(row >= lo, row < hi)
            xb_ref[...] = jnp.where(mine, x_ref[...], 0.0).astype(BF16)

        wgb_ref[...] = wg_ref[0].astype(BF16)
        wub_ref[...] = wu_ref[0].astype(BF16)
        wdb_ref[...] = wd_ref[0].astype(BF16)

        def sub_block(s, carry):
            rows = pl.ds(pl.multiple_of(s * sub, sub), sub)
            xb = xb_ref[rows, :]
            hg = jnp.dot(xb, wgb_ref[...], preferred_element_type=F32)
            hu = jnp.dot(xb, wub_ref[...], preferred_element_type=F32)
            act = (_silu(hg) * hu).astype(BF16)
            part = jnp.dot(act, wdb_ref[...], preferred_element_type=F32)
            init = jnp.logical_and(f == 0, lo <= s * sub)

            @pl.when(init)
            def _():
                o_ref[rows, :] = part

            @pl.when(jnp.logical_not(init))
            def _():
                o_ref[rows, :] += part

            return carry

        lax.fori_loop(lo >> sub_shift, (hi + sub - 1) >> sub_shift, sub_block, 0)


def _moe_ffn(xs, items, wg, wu, wd):
    rows, d = xs.shape
    tm = MOE_TILE
    tf = MOE_FF_TILE
    nf = wg.shape[2] // tf
    n_items = items[0].shape[0]
    f_idx = lambda f, v: f * v + (nf - 1) * (1 - v)
    return pl.pallas_call(
        _moe_kernel,
        grid_spec=pltpu.PrefetchScalarGridSpec(
            num_scalar_prefetch=5,
            grid=(n_items, nf),
            in_specs=[pl.BlockSpec((tm, d), lambda w, f, ti, te, tv, lo, hi: (ti[w], 0)),
                      pl.BlockSpec((1, d, tf), lambda w, f, ti, te, tv, lo, hi: (te[w], 0, f_idx(f, tv[w]))),
                      pl.BlockSpec((1, d, tf), lambda w, f, ti, te, tv, lo, hi: (te[w], 0, f_idx(f, tv[w]))),
                      pl.BlockSpec((1, tf, d), lambda w, f, ti, te, tv, lo, hi: (te[w], f_idx(f, tv[w]), 0))],
            out_specs=pl.BlockSpec((tm, d), lambda w, f, ti, te, tv, lo, hi: (ti[w], 0)),
            scratch_shapes=[pltpu.VMEM((tm, d), BF16), pltpu.VMEM((d, tf), BF16),
                            pltpu.VMEM((d, tf), BF16), pltpu.VMEM((tf, d), BF16)]),
        out_shape=jax.ShapeDtypeStruct((rows, d), F32),
        compiler_params=_cparams("arbitrary", "arbitrary"),
        name="moe_expert_swiglu",
    )(*items, xs, wg, wu, wd)


def _combine_kernel(d1_ref, d2_ref, ys_ref, x_ref, wt_ref, g2_ref, fw_ref, o_ref, y1_ref, y2_ref, sem):
    i = pl.program_id(0)
    tr = ROW_DMA_TILE
    base = i * tr

    def copies(t):
        return (pltpu.make_async_copy(ys_ref.at[pl.ds(d1_ref[base + t], 1)], y1_ref.at[pl.ds(t, 1)], sem),
                pltpu.make_async_copy(ys_ref.at[pl.ds(d2_ref[base + t], 1)], y2_ref.at[pl.ds(t, 1)], sem))

    def start(t, c):
        a, b = copies(t)
        a.start()
        b.start()
        return c

    def wait(t, c):
        a, b = copies(t)
        a.wait()
        b.wait()
        return c

    lax.fori_loop(0, tr, start, 0)
    lax.fori_loop(0, tr, wait, 0)
    wt = wt_ref[...]
    ffn = wt[:, 0:1] * y1_ref[...] + wt[:, 1:2] * y2_ref[...]
    x4 = x_ref[...] + g2_ref[0] * ffn
    ms = jnp.mean(x4 * x4, axis=-1, keepdims=True)
    o_ref[...] = (x4 * lax.rsqrt(ms + EPS)) * fw_ref[...]


def _combine(ys, d1, d2, x3, wt, g2, final_w, seq_len):
    n, d = x3.shape
    tr = ROW_DMA_TILE
    tps = seq_len // tr
    return pl.pallas_call(
        _combine_kernel,
        grid_spec=pltpu.PrefetchScalarGridSpec(
            num_scalar_prefetch=2,
            grid=(n // tr,),
            in_specs=[pl.BlockSpec(memory_space=pl.ANY),
                      pl.BlockSpec((tr, d), lambda i, a, b: (i, 0)),
                      pl.BlockSpec((tr, LANES), lambda i, a, b: (i, 0)),
                      pl.BlockSpec((1, 1, d), lambda i, a, b: (i // tps, 0, 0)),
                      pl.BlockSpec((1, d), lambda i, a, b: (0, 0))],
            out_specs=pl.BlockSpec((tr, d), lambda i, a, b: (i, 0)),
            scratch_shapes=[pltpu.VMEM((tr, d), F32), pltpu.VMEM((tr, d), F32),
                            pltpu.SemaphoreType.DMA(())]),
        out_shape=jax.ShapeDtypeStruct((n, d), F32),
        compiler_params=_cparams("arbitrary"),
        name="moe_combine_final_norm",
    )(d1, d2, ys, x3, wt, g2, final_w.reshape(1, d))


def _moe_tables(meta, counts):
    n = meta.shape[0]
    tm = MOE_TILE
    n_items = (2 * n) // tm + N_EXPERTS - 1
    cnt = counts[0, :N_EXPERTS]
    end = jnp.cumsum(cnt)
    off = end - cnt
    d1 = jnp.take(off, meta[:, 0]) + meta[:, 2]
    d2 = jnp.take(off, meta[:, 1]) + meta[:, 3]
    first_tile = off // tm
    n_e = jnp.where(cnt > 0, (end - 1) // tm - first_tile + 1, 0)
    item_end = jnp.cumsum(n_e)
    item_start = item_end - n_e
    total = item_end[-1]
    w = jnp.arange(n_items, dtype=jnp.int32)
    w_eff = jnp.minimum(w, total - 1)
    te = jnp.minimum(jnp.sum((item_end[None, :] <= w_eff[:, None]).astype(jnp.int32), axis=1),
                     N_EXPERTS - 1)
    ti = jnp.take(first_tile, te) + (w_eff - jnp.take(item_start, te))
    tv = (w < total).astype(jnp.int32)
    lo = jnp.clip(jnp.take(off, te) - ti * tm, 0, tm) * tv
    hi = jnp.clip(jnp.take(end, te) - ti * tm, 0, tm) * tv
    i32 = lambda t: t.astype(jnp.int32)
    return i32(d1), i32(d2), (i32(ti), i32(te), tv, i32(lo), i32(hi))


def kernel(x, c, rel_bias, ada_w, ada_b, norm_mix_w, norm_ffn_w, final_norm_w, ab_w_in, attn_sinks,
           dn_conv_w, dn_a_log, dn_dt_bias, dn_norm_w, ab_w_out, ffn_w_gate, ffn_w_up, ffn_w_down,
           cd_w_in, lru_conv_w, lru_conv_b, lru_gate_a_w, lru_gate_a_b, lru_gate_x_w, lru_gate_x_b,
           lru_lambda, sconv_w, cd_w_out, moe_router_w, moe_router_b, moe_w_gate, moe_w_up, moe_w_down):
    bsz, seq_len, d = x.shape
    n = bsz * seq_len
    x2d = x.reshape(n, d)
    mods = _ada_mods(c, ada_w, ada_b)

    sh1, sc1, g1, sh2, sc2, g2 = (mods[0, k] for k in range(6))
    qa, kd, vd, qn, kn, vb, gs, beta, g = _in_proj0(
        x2d, norm_mix_w[0], sc1, sh1, ab_w_in[0], dn_conv_w[0], dn_a_log[0], dn_dt_bias[0], seq_len)
    attn = _attention(qa, kd, vd, _bias_table(rel_bias), attn_sinks[0], seq_len)
    dn = _deltanet(qn, kn, vb, gs, beta, g, dn_norm_w[0], seq_len)
    x2 = _mid0(attn, dn, x2d, ab_w_out[0], g1, norm_ffn_w[0], sc2, sh2, g2,
               ffn_w_gate[0], ffn_w_up[0], ffn_w_down[0], seq_len)

    sh1, sc1, g1, sh2, sc2, g2 = (mods[1, k] for k in range(6))
    cat = _mix1(x2, norm_mix_w[1], sc1, sh1, cd_w_in[0], lru_conv_w[0], lru_conv_b[0],
                lru_gate_a_w[0], lru_gate_a_b[0], lru_gate_x_w[0], lru_gate_x_b[0],
                lru_lambda[0], sconv_w[0], seq_len)
    x3, hn4, meta, wt, counts = _route(cat, x2, cd_w_out[0], g1, norm_ffn_w[1], sc2, sh2,
                                       moe_router_w[0], moe_router_b[0], seq_len)
    d1, d2, items = _moe_tables(meta, counts)
    xs = _dispatch(hn4, d1, d2, 2 * n)
    ys = _moe_ffn(xs, items, moe_w_gate[0], moe_w_up[0], moe_w_down[0])
    out = _combine(ys, d1, d2, x3, wt, g2, final_norm_w, seq_len)
    return out.reshape(bsz, seq_len, d)
```

```python
import functools
import math

import numpy as np
import jax
import jax.numpy as jnp
from jax import lax
from jax.experimental import pallas as pl
from jax.experimental.pallas import tpu as pltpu

D_MODEL = 1024
EPS = 1e-6
HEAD_DIM = 64
A_Q_HEADS = 8
A_KV_HEADS = 2
WINDOW = 128
N_BUCKETS = 32
MAX_DISTANCE = 128
B_HEADS = 8
B_CONV = 4
CHUNK = 64
A_Q_W = A_Q_HEADS * HEAD_DIM
A_KV_W = A_KV_HEADS * HEAD_DIM
B_W = B_HEADS * HEAD_DIM
B_QKV_W = 3 * B_W
LRU_WIDTH = D_MODEL
LRU_BLOCKS = 8
LRU_C = 8.0
SC_WIDTH = D_MODEL // 2
D_FF = 2816
N_EXPERTS = 8
D_FF_EXPERT = 3584

LANES = 128
SUBLANES = 8
VMEM_LIMIT_BYTES = 56 * 1024 * 1024
TOKEN_TILE = 512
MOE_TILE = 1024
MOE_SUB = 256
MOE_FF_TILE = 512
ROW_DMA_TILE = 256
NEG_BIG = -1e30

F32 = jnp.float32
BF16 = jnp.bfloat16


def _cparams(*sem):
    return pltpu.CompilerParams(dimension_semantics=tuple(sem), vmem_limit_bytes=VMEM_LIMIT_BYTES)


def _const_spec(shape):
    nd = len(shape)
    return pl.BlockSpec(shape, lambda *_: (0,) * nd)


def _bdot(a, b):
    return jnp.dot(a.astype(BF16), b.astype(BF16), preferred_element_type=F32)


def _bdot_nt(a, b):
    return lax.dot_general(a.astype(BF16), b.astype(BF16), (((1,), (1,)), ((), ())),
                           preferred_element_type=F32)


def _bdot_tn(a, b):
    return lax.dot_general(a.astype(BF16), b.astype(BF16), (((0,), (0,)), ((), ())),
                           preferred_element_type=F32)


def _split(x, n):
    parts = []
    r = x
    for i in range(n):
        p = r.astype(BF16)
        parts.append(p)
        if i + 1 < n:
            r = r - p.astype(F32)
    return parts


def _dot_x(a, b, na=2, nb=2):
    asp = _split(a, na) if na > 1 else [a.astype(BF16)]
    bsp = _split(b, nb) if nb > 1 else [b.astype(BF16)]
    acc = None
    for i, ai in enumerate(asp):
        for j, bj in enumerate(bsp):
            if i + j >= max(na, nb):
                continue
            t = jnp.dot(ai, bj, preferred_element_type=F32)
            acc = t if acc is None else acc + t
    return acc


def _silu(x):
    return x * (1.0 / (1.0 + jnp.exp(-x)))


def _sigmoid(x):
    return 1.0 / (1.0 + jnp.exp(-x))


def _log1p(z):
    u = 1.0 + z
    tiny = u == 1.0
    return jnp.where(tiny, z, jnp.log(u) * (z / jnp.where(tiny, 1.0, u - 1.0)))


def _softplus(x):
    return jnp.maximum(x, 0.0) + _log1p(jnp.exp(-jnp.abs(x)))


def _neg_expm1(y):
    return -jnp.tanh(0.5 * y) * (jnp.exp(y) + 1.0)


def _norm_mod(x, w, sc, sh):
    ms = jnp.mean(x * x, axis=-1, keepdims=True)
    return (x * lax.rsqrt(ms + EPS)) * w * (1.0 + sc) + sh


def _shift_rows(x, k, prev_tail):
    xs = pltpu.roll(x, k, 0)
    head = pltpu.roll(prev_tail, k, 0)
    row = lax.broadcasted_iota(jnp.int32, (SUBLANES, x.shape[1]), 0)
    top = jnp.where(row < k, head, xs[:SUBLANES])
    return jnp.concatenate([top, xs[SUBLANES:]], axis=0)


def _ada_kernel(c_ref, w_ref, b_ref, o_ref):
    c = c_ref[...]
    cond = _silu(c)
    o_ref[0] = _dot_x(cond, w_ref[0], 3, 3) + b_ref[0]


def _ada_mods(c, ada_w, ada_b):
    depth, d, six_d = ada_w.shape
    bsz = c.shape[0]
    rows = max(SUBLANES, bsz)
    c_pad = jnp.zeros((rows, d), F32).at[:bsz].set(c)
    tn = 1536
    out = pl.pallas_call(
        _ada_kernel,
        grid=(depth, six_d // tn),
        in_specs=[pl.BlockSpec((rows, d), lambda l, j: (0, 0)),
                  pl.BlockSpec((1, d, tn), lambda l, j: (l, 0, j)),
                  pl.BlockSpec((1, 1, tn), lambda l, j: (l, 0, j))],
        out_specs=pl.BlockSpec((1, rows, tn), lambda l, j: (l, 0, j)),
        out_shape=jax.ShapeDtypeStruct((depth, rows, six_d), F32),
        compiler_params=_cparams("parallel", "parallel"),
        name="ada_mods",
    )(c_pad, ada_w, ada_b.reshape(depth, 1, six_d))
    return out[:, :bsz].reshape(depth, bsz, 6, 1, d).transpose(0, 2, 1, 3, 4)


def _t5_bucket(dist):
    max_exact = N_BUCKETS // 2
    d = np.maximum(dist, 0)
    large = max_exact + (np.log(np.maximum(d, 1) / max_exact) / math.log(MAX_DISTANCE / max_exact)
                         * (N_BUCKETS - max_exact)).astype(np.int32)
    large = np.minimum(large, N_BUCKETS - 1)
    return np.where(d < max_exact, d, large).astype(np.int32)


def _band_buckets():
    qi = np.arange(WINDOW)[:, None]
    s = np.arange(2 * WINDOW)[None, :]
    dist = qi + WINDOW - s
    in_window = (dist >= 0) & (dist < WINDOW)
    return np.where(in_window, _t5_bucket(dist), -1).astype(np.int32)


def _bias_kernel(rb_ref, bucket_ref, o_ref):
    h = pl.program_id(0)
    bucket = bucket_ref[...]
    acc = jnp.zeros(bucket.shape, F32)
    for b in range(N_BUCKETS):
        acc = jnp.where(bucket == b, rb_ref[b, h], acc)
    o_ref[0] = jnp.where(bucket < 0, NEG_BIG, acc)


def _bias_table(rel_bias):
    bucket = jnp.asarray(_band_buckets())
    out = pl.pallas_call(
        _bias_kernel,
        grid=(A_Q_HEADS,),
        in_specs=[pl.BlockSpec(memory_space=pltpu.SMEM),
                  _const_spec((WINDOW, 2 * WINDOW))],
        out_specs=pl.BlockSpec((1, WINDOW, 2 * WINDOW), lambda h: (h, 0, 0)),
        out_shape=jax.ShapeDtypeStruct((A_Q_HEADS, WINDOW, 2 * WINDOW), F32),
        compiler_params=_cparams("parallel"),
        name="attn_bias_table",
    )(rel_bias, bucket)
    return out.reshape(A_Q_HEADS // 2, 2 * WINDOW, 2 * WINDOW)


_C_QA = 0
_C_KD = _C_QA + A_Q_W
_C_VD = _C_KD + 2 * A_KV_W
_C_QKV = _C_VD + 2 * A_KV_W
_C_GATE = _C_QKV + B_QKV_W
_C_SMALL = _C_GATE + B_W
_AB_COLS = _C_SMALL + LANES


def _ab_in_weight(w_in):
    qa, ka, va, qkv, gate, beta, dec = jnp.split(
        w_in, list(np.cumsum([A_Q_W, A_KV_W, A_KV_W, B_QKV_W, B_W, B_HEADS])), axis=1)

    def dup(t):
        return jnp.concatenate([t[:, :HEAD_DIM]] * 2 + [t[:, HEAD_DIM:]] * 2, axis=1)

    small = jnp.zeros((w_in.shape[0], LANES), w_in.dtype)
    small = small.at[:, :B_HEADS].set(beta).at[:, B_HEADS:2 * B_HEADS].set(dec)
    return jnp.concatenate([qa, dup(ka), dup(va), qkv, gate, small], axis=1).astype(BF16)


def _head_selector():
    e = np.zeros((B_W, LANES), np.float32)
    for h in range(B_HEADS):
        e[h * HEAD_DIM:(h + 1) * HEAD_DIM, h] = 1.0
    return e


def _in0_kernel(x_ref, nw_ref, sc_ref, sh_ref, w_ref, cw_ref, sel_ref, selt_ref, alog_ref, dtb_ref,
                qa_ref, kd_ref, vd_ref, qn_ref, kn_ref, vb_ref, gs_ref, beta_ref, g_ref,
                tail_ref, *, tiles_per_seq):
    i = pl.program_id(0)

    @pl.when(i % tiles_per_seq == 0)
    def _():
        tail_ref[...] = jnp.zeros_like(tail_ref)

    hn = _norm_mod(x_ref[...], nw_ref[...], sc_ref[0], sh_ref[0])
    proj = jnp.dot(hn.astype(BF16), w_ref[...], preferred_element_type=F32)
    qa_ref[...] = proj[:, _C_QA:_C_KD].astype(BF16)
    kd_ref[...] = proj[:, _C_KD:_C_VD].astype(BF16)
    vd_ref[...] = proj[:, _C_VD:_C_QKV].astype(BF16)

    xq = proj[:, _C_QKV:_C_GATE]
    tail = tail_ref[...]
    cw = cw_ref[...]
    y = xq * cw[B_CONV - 1:B_CONV]
    for k in range(1, B_CONV):
        y = y + _shift_rows(xq, k, tail) * cw[B_CONV - 1 - k:B_CONV - k]
    tail_ref[...] = xq[xq.shape[0] - SUBLANES:]
    y = _silu(y)
    q, k_, v = y[:, :B_W], y[:, B_W:2 * B_W], y[:, 2 * B_W:]

    def l2n(t):
        ssq = _dot_x(t * t, sel_ref[...], 2, 1)
        r = lax.rsqrt(ssq + EPS)
        return t * _dot_x(r, selt_ref[...], 2, 1)

    qn_ref[...] = l2n(q) * (HEAD_DIM ** -0.5)
    kn_ref[...] = l2n(k_)
    vb_ref[...] = v
    gs_ref[...] = _silu(proj[:, _C_GATE:_C_SMALL])
    small = proj[:, _C_SMALL:]
    lane = lax.broadcasted_iota(jnp.int32, small.shape, 1)
    beta_ref[...] = jnp.where(lane < B_HEADS, _sigmoid(small), 0.0)
    dec = pltpu.roll(small, LANES - B_HEADS, 1)
    g = -jnp.exp(alog_ref[...]) * _softplus(dec + dtb_ref[...])
    g_ref[...] = jnp.where(lane < B_HEADS, g, 0.0)


def _in_proj0(x2d, nw, sc, sh, w_in, conv_w, a_log, dt_bias, seq_len):
    n, d = x2d.shape
    tm = TOKEN_TILE
    tiles_per_seq = seq_len // tm
    w = _ab_in_weight(w_in)
    sel = jnp.asarray(_head_selector(), BF16)
    selt = jnp.asarray(_head_selector().T.copy(), BF16)
    pad8 = lambda v: jnp.zeros((1, LANES), F32).at[0, :B_HEADS].set(v)
    row = lambda width: pl.BlockSpec((tm, width), lambda i: (i, 0))
    per_b = pl.BlockSpec((1, 1, d), lambda i: (i // tiles_per_seq, 0, 0))
    outs = pl.pallas_call(
        functools.partial(_in0_kernel, tiles_per_seq=tiles_per_seq),
        grid=(n // tm,),
        in_specs=[row(d), _const_spec((1, d)), per_b, per_b,
                  _const_spec((d, _AB_COLS)), _const_spec((B_CONV, B_QKV_W)),
                  _const_spec((B_W, LANES)), _const_spec((LANES, B_W)),
                  _const_spec((1, LANES)), _const_spec((1, LANES))],
        out_specs=[row(A_Q_W), row(2 * A_KV_W), row(2 * A_KV_W),
                   row(B_W), row(B_W), row(B_W), row(B_W), row(LANES), row(LANES)],
        out_shape=[jax.ShapeDtypeStruct((n, A_Q_W), BF16),
                   jax.ShapeDtypeStruct((n, 2 * A_KV_W), BF16),
                   jax.ShapeDtypeStruct((n, 2 * A_KV_W), BF16),
                   jax.ShapeDtypeStruct((n, B_W), F32),
                   jax.ShapeDtypeStruct((n, B_W), F32),
                   jax.ShapeDtypeStruct((n, B_W), F32),
                   jax.ShapeDtypeStruct((n, B_W), F32),
                   jax.ShapeDtypeStruct((n, LANES), F32),
                   jax.ShapeDtypeStruct((n, LANES), F32)],
        scratch_shapes=[pltpu.VMEM((SUBLANES, B_QKV_W), F32)],
        compiler_params=_cparams("arbitrary"),
        name="in_proj0",
    )(x2d, nw.reshape(1, d), sc, sh, w, conv_w, sel, selt, pad8(a_log), pad8(dt_bias))
    return outs


def _attn_kernel(sink_ref, q_ref, kp_ref, kc_ref, vp_ref, vc_ref, bm_ref, o_ref, *, blocks_per_seq):
    i = pl.program_id(0)
    first = (i % blocks_per_seq) == 0
    w = WINDOW
    lane = lax.broadcasted_iota(jnp.int32, (w, LANES), 1)
    low = lane < HEAD_DIM
    col = lax.broadcasted_iota(jnp.int32, (2 * w, 2 * w), 1)
    row = lax.broadcasted_iota(jnp.int32, (2 * w, 1), 0)
    prev_dead = jnp.logical_and(first, col < w)
    q_all = q_ref[...]
    zero = jnp.zeros((), q_all.dtype)
    outs = []
    for j in range(A_Q_HEADS // 2):
        kh = (2 * j) // (A_Q_HEADS // A_KV_HEADS)
        qp = q_all[:, j * LANES:(j + 1) * LANES]
        qs = jnp.concatenate([jnp.where(low, qp, zero), jnp.where(low, zero, qp)], axis=0)
        kd = jnp.concatenate([kp_ref[:, kh * LANES:(kh + 1) * LANES],
                              kc_ref[:, kh * LANES:(kh + 1) * LANES]], axis=0)
        vd = jnp.concatenate([vp_ref[:, kh * LANES:(kh + 1) * LANES],
                              vc_ref[:, kh * LANES:(kh + 1) * LANES]], axis=0)
        s = lax.dot_general(qs, kd, (((1,), (1,)), ((), ())), preferred_element_type=F32)
        s = s * (HEAD_DIM ** -0.5) + bm_ref[j]
        s = jnp.where(prev_dead, NEG_BIG, s)
        sink = jnp.where(row < w, sink_ref[2 * j], sink_ref[2 * j + 1])
        m = jnp.maximum(jnp.max(s, axis=-1, keepdims=True), sink)
        p = jnp.exp(s - m)
        denom = jnp.sum(p, axis=-1, keepdims=True) + jnp.exp(sink - m)
        pv = jnp.dot(p.astype(BF16), vd, preferred_element_type=F32) / denom
        outs.append(jnp.where(low, pv[:w], pv[w:]))
    o_ref[...] = jnp.concatenate(outs, axis=1).astype(o_ref.dtype)


def _attention(qa, kd, vd, bias_tbl, sinks, seq_len):
    n = qa.shape[0]
    w = WINDOW
    nb = seq_len // w
    cur = lambda i: (i, 0)
    prev = lambda i: (jnp.where(i % nb == 0, i, i - 1), 0)
    return pl.pallas_call(
        functools.partial(_attn_kernel, blocks_per_seq=nb),
        grid=(n // w,),
        in_specs=[pl.BlockSpec(memory_space=pltpu.SMEM),
                  pl.BlockSpec((w, A_Q_W), cur),
                  pl.BlockSpec((w, 2 * A_KV_W), prev), pl.BlockSpec((w, 2 * A_KV_W), cur),
                  pl.BlockSpec((w, 2 * A_KV_W), prev), pl.BlockSpec((w, 2 * A_KV_W), cur),
                  _const_spec((A_Q_HEADS // 2, 2 * w, 2 * w))],
        out_specs=pl.BlockSpec((w, A_Q_W), cur),
        out_shape=jax.ShapeDtypeStruct((n, A_Q_W), BF16),
        compiler_params=_cparams("parallel"),
        name="swa_attention",
    )(sinks, qa, kd, kd, vd, vd, bias_tbl)


_DN_PAIRS = B_HEADS // 2
_DN_INV_BLOCK = 16
_DN_GROUP = 2


def _block_diag(x, low):
    zero = jnp.zeros((), x.dtype)
    return jnp.concatenate([jnp.where(low, x, zero), jnp.where(low, zero, x)], axis=0)


def _dn_intra(chunks, data_refs, work_refs, consts):
    qn_ref, kn_ref, vb_ref, bexp_ref, gcexp_ref = data_refs
    u_ref, w_ref, qk_ref, qd_ref, kd_ref, egl_ref = work_refs
    low, i_idx, j_idx, ones3 = consts
    c = CHUNK
    units = [(ci, p) for ci in chunks for p in range(_DN_PAIRS)]
    where = [(slice(ci * c, (ci + 1) * c), slice(p * LANES, (p + 1) * LANES)) for ci, p in units]
    causal = i_idx >= j_idx
    strict = i_idx > j_idx
    on_diag = i_idx == j_idx
    eye = on_diag.astype(F32)
    blk_shift = int(math.log2(_DN_INV_BLOCK))
    same_blk = (i_idx >> blk_shift) == (j_idx >> blk_shift)

    q = [qn_ref[rs, ls] for rs, ls in where]
    k = [kn_ref[rs, ls] for rs, ls in where]
    v = [vb_ref[rs, ls] for rs, ls in where]
    b = [bexp_ref[rs, ls] for rs, ls in where]
    gc = [gcexp_ref[rs, ls] for rs, ls in where]

    gr = [jnp.dot(ones3, jnp.concatenate(_split(jnp.where(on_diag, t, 0.0), 3), axis=0),
                  preferred_element_type=F32) for t in gc]
    ks = [_block_diag(t.astype(BF16), low) for t in k]
    qkk = [lax.dot_general(jnp.concatenate([qt, kt], axis=0).astype(BF16), kst,
                           (((1,), (1,)), ((), ())), preferred_element_type=F32)
           for qt, kt, kst in zip(q, k, ks)]
    decay = [jnp.exp(jnp.where(causal, gct - grt, NEG_BIG)) for gct, grt in zip(gc, gr)]
    lmat = [jnp.where(strict, bt * t[c:] * dt, 0.0) for bt, t, dt in zip(b, qkk, decay)]
    qk = [jnp.where(causal, t[:c] * dt, 0.0) for t, dt in zip(qkk, decay)]

    def mm(xs, ys):
        return [_bdot(x, _block_diag(y.astype(BF16), low)) for x, y in zip(xs, ys)]

    l_diag = [jnp.where(same_blk, t, 0.0) for t in lmat]
    l_off = [t - d for t, d in zip(lmat, l_diag)]
    pw = [-t for t in l_diag]
    d_inv = [eye + t for t in pw]
    for _ in range(blk_shift - 1):
        pw = mm(pw, pw)
        d_inv = mm(d_inv, [eye + t for t in pw])
    pw = [-t for t in mm(d_inv, l_off)]
    acc = [eye + t for t in pw]
    for _ in range(int(math.log2(c // _DN_INV_BLOCK)) - 1):
        pw = mm(pw, pw)
        acc = mm(acc, [eye + t for t in pw])
    tmat = mm(acc, d_inv)

    egc = [jnp.exp(t) for t in gc]
    rhs = [jnp.concatenate([_block_diag((vt * bt).astype(BF16), low),
                            _block_diag((kt * (bt * et)).astype(BF16), low)], axis=1)
           for vt, kt, bt, et in zip(v, k, b, egc)]
    uw = [_bdot(t, r) for t, r in zip(tmat, rhs)]
    for n, (ci, p) in enumerate(units):
        g_last = gc[n][c - 1:c, :]
        u_ref[ci, p] = uw[n][:, :LANES]
        w_ref[ci, p] = uw[n][:, LANES:]
        qk_ref[ci, p] = qk[n]
        qd_ref[ci, p] = q[n] * egc[n]
        kd_ref[ci, p] = k[n] * jnp.exp(g_last - gc[n])
        egl_ref[ci, p] = jnp.broadcast_to(jnp.exp(g_last), (SUBLANES, LANES))


def _dn_scan(ci, work_refs, s_ref, gs_ref, nw, o_ref, consts):
    u_ref, w_ref, qk_ref, qd_ref, kd_ref, egl_ref = work_refs
    low, mask_bd, head_mean2 = consts
    c = CHUNK
    rows = slice(ci * c, (ci + 1) * c)
    pairs = range(_DN_PAIRS)
    s_old = [s_ref[p] for p in pairs]
    wq = [_bdot(jnp.concatenate([w_ref[ci, p], qd_ref[ci, p]], axis=0), s_old[p]) for p in pairs]
    v_new = [u_ref[ci, p] - wq[p][:c] for p in pairs]
    o = [wq[p][c:] + _bdot(qk_ref[ci, p], _block_diag(v_new[p].astype(BF16), low)) for p in pairs]
    kv = [_bdot_tn(kd_ref[ci, p], v_new[p]) for p in pairs]
    for p in pairs:
        s_ref[p] = s_old[p] * egl_ref[ci, p][0:1, :] + jnp.where(mask_bd, kv[p], 0.0)
    ms = [jnp.dot(jnp.concatenate(_split(t * t, 2), axis=1), head_mean2, preferred_element_type=F32)
          for t in o]
    for p in pairs:
        ls = slice(p * LANES, (p + 1) * LANES)
        y = (o[p] * lax.rsqrt(ms[p] + EPS)) * nw * gs_ref[rows, ls]
        o_ref[rows, ls] = y.astype(o_ref.dtype)


def _dn_kernel(qn_ref, kn_ref, vb_ref, gs_ref, beta_ref, g_ref, selt_ref, nw_ref, o_ref,
               s_ref, bexp_ref, gcexp_ref, u_ref, w_ref, qk_ref, qd_ref, kd_ref, egl_ref, *, groups_per_seq):
    i = pl.program_id(0)

    @pl.when(i % groups_per_seq == 0)
    def _():
        s_ref[...] = jnp.zeros_like(s_ref)

    c = CHUNK
    tm = o_ref.shape[0]
    n_chunks = tm // c
    bexp_ref[...] = _dot_x(beta_ref[...], selt_ref[...], 3, 1)
    rt = lax.broadcasted_iota(jnp.int32, (tm, tm), 0)
    ct = lax.broadcasted_iota(jnp.int32, (tm, tm), 1)
    chunk_shift = int(math.log2(c))
    tril_bd = jnp.logical_and(rt >= ct, (rt >> chunk_shift) == (ct >> chunk_shift)).astype(BF16)
    gc8 = _dot_x(tril_bd, g_ref[...], 1, 3)
    gcexp_ref[...] = _dot_x(gc8, selt_ref[...], 3, 1)

    lane = lax.broadcasted_iota(jnp.int32, (c, LANES), 1)
    low = lane < HEAD_DIM
    i_idx = lax.broadcasted_iota(jnp.int32, (c, LANES), 0)
    j_idx = lane & (c - 1)
    ones3 = jnp.ones((c, 3 * c), BF16)
    rb = lax.broadcasted_iota(jnp.int32, (LANES, LANES), 0)
    cb = lax.broadcasted_iota(jnp.int32, (LANES, LANES), 1)
    mask_bd = (rb < HEAD_DIM) == (cb < HEAD_DIM)
    head_mean = jnp.where(mask_bd, 1.0 / HEAD_DIM, 0.0).astype(BF16)
    head_mean2 = jnp.concatenate([head_mean, head_mean], axis=0)
    data_refs = (qn_ref, kn_ref, vb_ref, bexp_ref, gcexp_ref)
    work_refs = (u_ref, w_ref, qk_ref, qd_ref, kd_ref, egl_ref)
    intra_consts = (low, i_idx, j_idx, ones3)
    scan_consts = (low, mask_bd, head_mean2)
    nw = nw_ref[...]

    groups = [list(range(s, s + _DN_GROUP)) for s in range(0, n_chunks, _DN_GROUP)]
    _dn_intra(groups[0], data_refs, work_refs, intra_consts)
    for j, grp in enumerate(groups):
        if j + 1 < len(groups):
            _dn_intra(groups[j + 1], data_refs, work_refs, intra_consts)
        for ci in grp:
            _dn_scan(ci, work_refs, s_ref, gs_ref, nw, o_ref, scan_consts)


def _deltanet(qn, kn, vb, gs, beta, g, norm_w, seq_len):
    n = qn.shape[0]
    tm = TOKEN_TILE
    selt = jnp.asarray(_head_selector().T.copy(), BF16)
    nw2 = jnp.concatenate([norm_w, norm_w]).reshape(1, LANES)
    row = lambda width: pl.BlockSpec((tm, width), lambda i: (i, 0))
    return pl.pallas_call(
        functools.partial(_dn_kernel, groups_per_seq=seq_len // tm),
        grid=(n // tm,),
        in_specs=[row(B_W), row(B_W), row(B_W), row(B_W), row(LANES), row(LANES),
                  _const_spec((LANES, B_W)), _const_spec((1, LANES))],
        out_specs=row(B_W),
        out_shape=jax.ShapeDtypeStruct((n, B_W), BF16),
        scratch_shapes=[pltpu.VMEM((_DN_PAIRS, LANES, LANES), F32),
                        pltpu.VMEM((tm, B_W), F32), pltpu.VMEM((tm, B_W), F32)]
        + [pltpu.VMEM((tm // CHUNK, _DN_PAIRS, CHUNK, LANES), F32)] * 5
        + [pltpu.VMEM((tm // CHUNK, _DN_PAIRS, SUBLANES, LANES), F32)],
        compiler_params=_cparams("arbitrary"),
        name="gated_deltanet",
    )(qn, kn, vb, gs, beta, g, selt, nw2)


def _resident_spec(shape):
    nd = len(shape)
    return pl.BlockSpec(shape, lambda *_: (0,) * nd, pipeline_mode=pl.Buffered(1))


def _mid0_kernel(attn_ref, dn_ref, x_ref, wo_ref, g1_ref, nw_ref, sc_ref, sh_ref, g2_ref,
                 wg_ref, wu_ref, wd_ref, o_ref):
    mix = (jnp.dot(attn_ref[...], wo_ref[:A_Q_W], preferred_element_type=F32)
           + jnp.dot(dn_ref[...], wo_ref[A_Q_W:], preferred_element_type=F32))
    x1 = x_ref[...] + g1_ref[0] * mix
    hn = _norm_mod(x1, nw_ref[...], sc_ref[0], sh_ref[0]).astype(BF16)
    hg = jnp.dot(hn, wg_ref[...], preferred_element_type=F32)
    hu = jnp.dot(hn, wu_ref[...], preferred_element_type=F32)
    act = (_silu(hg) * hu).astype(BF16)
    o_ref[...] = x1 + g2_ref[0] * jnp.dot(act, wd_ref[...], preferred_element_type=F32)


def _mid0(attn, dn, x2d, w_out, g1, nw, sc, sh, g2, wg, wu, wd, seq_len):
    n, d = x2d.shape
    tm = TOKEN_TILE
    tps = seq_len // tm
    row = lambda width: pl.BlockSpec((tm, width), lambda i: (i, 0))
    per_b = pl.BlockSpec((1, 1, d), lambda i: (i // tps, 0, 0))
    return pl.pallas_call(
        _mid0_kernel,
        grid=(n // tm,),
        in_specs=[row(A_Q_W), row(B_W), row(d), _resident_spec(w_out.shape), per_b,
                  _const_spec((1, d)), per_b, per_b, per_b,
                  _resident_spec(wg.shape), _resident_spec(wu.shape), _resident_spec(wd.shape)],
        out_specs=row(d),
        out_shape=jax.ShapeDtypeStruct((n, d), F32),
        compiler_params=_cparams("parallel"),
        name="out_proj0_swiglu",
    )(attn, dn, x2d, w_out.astype(BF16), g1, nw.reshape(1, d), sc, sh, g2,
      wg.astype(BF16), wu.astype(BF16), wd.astype(BF16))


def _gelu_tanh(x):
    return 0.5 * x * (1.0 + jnp.tanh(math.sqrt(2.0 / math.pi) * (x + 0.044715 * (x * x * x))))


def _linear_scan(a, b):
    n = a.shape[0]
    row = lax.broadcasted_iota(jnp.int32, a.shape, 0)
    s = 1
    while s < n:
        a_sh = pltpu.roll(a, s, 0)
        b_sh = pltpu.roll(b, s, 0)
        valid = row >= s
        b = jnp.where(valid, a * b_sh + b, b)
        a = jnp.where(valid, a * a_sh, a)
        s *= 2
    return a, b


def _mix1_kernel(x_ref, nw_ref, sc_ref, sh_ref, w_ref, cw_ref, cb_ref, ga_ref, gab_ref, gx_ref, gxb_ref,
                 lam_ref, sw_ref, o_ref, tail_c_ref, tail_d_ref, h_ref, *, tiles_per_seq):
    i = pl.program_id(0)

    @pl.when(i % tiles_per_seq == 0)
    def _():
        tail_c_ref[...] = jnp.zeros_like(tail_c_ref)
        tail_d_ref[...] = jnp.zeros_like(tail_d_ref)
        h_ref[...] = jnp.zeros_like(h_ref)

    hn = _norm_mod(x_ref[...], nw_ref[...], sc_ref[0], sh_ref[0]).astype(BF16)
    proj = jnp.dot(hn, w_ref[...], preferred_element_type=F32)
    w_l = LRU_WIDTH
    xc_in = proj[:, :w_l]
    yc = proj[:, w_l:2 * w_l]
    bd = proj[:, 2 * w_l:2 * w_l + SC_WIDTH]
    cd = proj[:, 2 * w_l + SC_WIDTH:2 * w_l + 2 * SC_WIDTH]
    hd = proj[:, 2 * w_l + 2 * SC_WIDTH:]
    tm = xc_in.shape[0]

    kc = cw_ref.shape[0]
    tail = tail_c_ref[...]
    cw = cw_ref[...]
    xc = xc_in * cw[kc - 1:kc] + cb_ref[...]
    for k in range(1, kc):
        xc = xc + _shift_rows(xc_in, k, tail) * cw[kc - 1 - k:kc - k]
    tail_c_ref[...] = xc_in[tm - SUBLANES:]

    xb = xc.astype(BF16)
    gw = ga_ref.shape[1]
    ra, ri = [], []
    for p in range(ga_ref.shape[0]):
        xin = xb[:, p * gw:(p + 1) * gw]
        ra.append(jnp.dot(xin, ga_ref[p], preferred_element_type=F32))
        ri.append(jnp.dot(xin, gx_ref[p], preferred_element_type=F32))
    r = _sigmoid(jnp.concatenate(ra, axis=1) + gab_ref[...])
    ig = _sigmoid(jnp.concatenate(ri, axis=1) + gxb_ref[...])
    log_a = (-LRU_C) * r * _softplus(-lam_ref[...])
    a = jnp.exp(log_a)
    b = jnp.sqrt(_neg_expm1(2.0 * log_a)) * (ig * xc)
    a_cum, h_loc = _linear_scan(a, b)
    h = a_cum * h_ref[0:1, :] + h_loc
    h_ref[...] = jnp.broadcast_to(h[tm - 1:tm, :], h_ref.shape)
    yc_out = h * _gelu_tanh(yc)

    ks = sw_ref.shape[0]
    ch = cd * hd
    tail_d = tail_d_ref[...]
    sw = sw_ref[...]
    conv = ch * sw[ks - 1:ks]
    for k in range(1, ks):
        conv = conv + _shift_rows(ch, k, tail_d) * sw[ks - 1 - k:ks - k]
    tail_d_ref[...] = ch[tm - SUBLANES:]
    o_ref[...] = jnp.concatenate([yc_out, bd * conv], axis=1).astype(o_ref.dtype)


def _pair_block_diag(gw):
    nb, bw, _ = gw.shape
    g2 = gw.reshape(nb // 2, 2, bw, bw)
    z = jnp.zeros((nb // 2, bw, bw), gw.dtype)
    top = jnp.concatenate([g2[:, 0], z], axis=2)
    bot = jnp.concatenate([z, g2[:, 1]], axis=2)
    return jnp.concatenate([top, bot], axis=1).astype(BF16)


def _mix1(x2d, nw, sc, sh, w_in, conv_w, conv_b, ga_w, ga_b, gx_w, gx_b, lam, sconv_w, seq_len):
    n, d = x2d.shape
    tm = TOKEN_TILE
    tps = seq_len // tm
    cd_in = w_in.shape[1]
    cd_out = LRU_WIDTH + SC_WIDTH
    row = lambda width: pl.BlockSpec((tm, width), lambda i: (i, 0))
    per_b = pl.BlockSpec((1, 1, d), lambda i: (i // tps, 0, 0))
    ga = _pair_block_diag(ga_w)
    gx = _pair_block_diag(gx_w)
    vec = lambda v: v.reshape(1, -1)
    return pl.pallas_call(
        functools.partial(_mix1_kernel, tiles_per_seq=tps),
        grid=(n // tm,),
        in_specs=[row(d), _const_spec((1, d)), per_b, per_b, _resident_spec((d, cd_in)),
                  _const_spec(conv_w.shape), _const_spec((1, LRU_WIDTH)),
                  _const_spec(ga.shape), _const_spec((1, LRU_WIDTH)),
                  _const_spec(gx.shape), _const_spec((1, LRU_WIDTH)),
                  _const_spec((1, LRU_WIDTH)), _const_spec(sconv_w.shape)],
        out_specs=row(cd_out),
        out_shape=jax.ShapeDtypeStruct((n, cd_out), BF16),
        scratch_shapes=[pltpu.VMEM((SUBLANES, LRU_WIDTH), F32), pltpu.VMEM((SUBLANES, SC_WIDTH), F32),
                        pltpu.VMEM((SUBLANES, LRU_WIDTH), F32)],
        compiler_params=_cparams("arbitrary"),
        name="rglru_shortconv_mixer",
    )(x2d, vec(nw), sc, sh, w_in.astype(BF16), conv_w, vec(conv_b), ga, vec(ga_b), gx, vec(gx_b),
      vec(lam), sconv_w)


def _route_kernel(cat_ref, x_ref, wo_ref, g1_ref, nw_ref, sc_ref, sh_ref, rw_ref, rb_ref,
                  x3_ref, hn_ref, meta_ref, wt_ref, cnt_ref, carry_ref):
    i = pl.program_id(0)

    @pl.when(i == 0)
    def _():
        carry_ref[...] = jnp.zeros_like(carry_ref)

    x3 = x_ref[...] + g1_ref[0] * jnp.dot(cat_ref[...], wo_ref[...], preferred_element_type=F32)
    x3_ref[...] = x3
    hn = _norm_mod(x3, nw_ref[...], sc_ref[0], sh_ref[0])
    hn_ref[...] = hn
    tm = hn.shape[0]
    lane = lax.broadcasted_iota(jnp.int32, (tm, LANES), 1)
    logits = _dot_x(hn, rw_ref[...], 2, 2) + rb_ref[...]
    lg = jnp.where(lane < N_EXPERTS, logits, NEG_BIG)
    m1 = jnp.max(lg, axis=1, keepdims=True)
    i1 = jnp.min(jnp.where(lg == m1, lane, LANES), axis=1, keepdims=True)
    lg2 = jnp.where(lane == i1, NEG_BIG, lg)
    m2 = jnp.max(lg2, axis=1, keepdims=True)
    i2 = jnp.min(jnp.where(lg2 == m2, lane, LANES), axis=1, keepdims=True)
    e2 = jnp.exp(m2 - m1)
    w1 = 1.0 / (1.0 + e2)
    w2 = e2 / (1.0 + e2)

    hit1 = lane == i1
    hit2 = lane == i2
    sel = jnp.logical_or(hit1, hit2).astype(F32)
    r_i = lax.broadcasted_iota(jnp.int32, (tm, tm), 0)
    c_i = lax.broadcasted_iota(jnp.int32, (tm, tm), 1)
    tril = (r_i >= c_i).astype(BF16)
    incl = jnp.dot(tril, sel.astype(BF16), preferred_element_type=F32)
    carry = carry_ref[0:1, :]
    excl = incl - sel + carry
    r1 = jnp.sum(jnp.where(hit1, excl, 0.0), axis=1, keepdims=True)
    r2 = jnp.sum(jnp.where(hit2, excl, 0.0), axis=1, keepdims=True)
    total = carry + incl[tm - 1:tm, :]
    carry_ref[...] = jnp.broadcast_to(total, carry_ref.shape)
    cnt_ref[...] = jnp.broadcast_to(total, cnt_ref.shape).astype(jnp.int32)

    meta = jnp.where(lane == 0, i1, 0)
    meta = jnp.where(lane == 1, i2, meta)
    meta = jnp.where(lane == 2, r1.astype(jnp.int32), meta)
    meta = jnp.where(lane == 3, r2.astype(jnp.int32), meta)
    meta_ref[...] = meta
    wt_ref[...] = jnp.where(lane == 0, w1, jnp.where(lane == 1, w2, 0.0))


def _route(cat, x2d, w_out, g1, nw, sc, sh, router_w, router_b, seq_len):
    n, d = x2d.shape
    tm = TOKEN_TILE
    tps = seq_len // tm
    row = lambda width: pl.BlockSpec((tm, width), lambda i: (i, 0))
    per_b = pl.BlockSpec((1, 1, d), lambda i: (i // tps, 0, 0))
    rw = jnp.zeros((d, LANES), F32).at[:, :N_EXPERTS].set(router_w)
    rb = jnp.zeros((1, LANES), F32).at[0, :N_EXPERTS].set(router_b)
    return pl.pallas_call(
        _route_kernel,
        grid=(n // tm,),
        in_specs=[row(cat.shape[1]), row(d), _resident_spec(w_out.shape), per_b, _const_spec((1, d)),
                  per_b, per_b, _const_spec((d, LANES)), _const_spec((1, LANES))],
        out_specs=[row(d), row(d), row(LANES), row(LANES), _const_spec((SUBLANES, LANES))],
        out_shape=[jax.ShapeDtypeStruct((n, d), F32), jax.ShapeDtypeStruct((n, d), F32),
                   jax.ShapeDtypeStruct((n, LANES), jnp.int32), jax.ShapeDtypeStruct((n, LANES), F32),
                   jax.ShapeDtypeStruct((SUBLANES, LANES), jnp.int32)],
        scratch_shapes=[pltpu.VMEM((SUBLANES, LANES), F32)],
        compiler_params=_cparams("arbitrary"),
        name="out_proj1_router",
    )(cat, x2d, w_out.astype(BF16), g1, nw.reshape(1, d), sc, sh, rw, rb)


def _dispatch_kernel(d1_ref, d2_ref, hn_ref, xs_ref, sem):
    i = pl.program_id(0)
    tr = ROW_DMA_TILE
    base = i * tr

    def copies(t):
        src = hn_ref.at[pl.ds(t, 1)]
        return (pltpu.make_async_copy(src, xs_ref.at[pl.ds(d1_ref[base + t], 1)], sem),
                pltpu.make_async_copy(src, xs_ref.at[pl.ds(d2_ref[base + t], 1)], sem))

    def start(t, c):
        a, b = copies(t)
        a.start()
        b.start()
        return c

    def wait(t, c):
        a, b = copies(t)
        a.wait()
        b.wait()
        return c

    lax.fori_loop(0, tr, start, 0)
    lax.fori_loop(0, tr, wait, 0)


def _dispatch(hn, d1, d2, rows_out):
    n, d = hn.shape
    return pl.pallas_call(
        _dispatch_kernel,
        grid_spec=pltpu.PrefetchScalarGridSpec(
            num_scalar_prefetch=2,
            grid=(n // ROW_DMA_TILE,),
            in_specs=[pl.BlockSpec((ROW_DMA_TILE, d), lambda i, a, b: (i, 0))],
            out_specs=pl.BlockSpec(memory_space=pl.ANY),
            scratch_shapes=[pltpu.SemaphoreType.DMA(())]),
        out_shape=jax.ShapeDtypeStruct((rows_out, d), hn.dtype),
        compiler_params=_cparams("arbitrary"),
        name="moe_dispatch",
    )(d1, d2, hn)


def _moe_kernel(ti_ref, te_ref, tv_ref, lo_ref, hi_ref, x_ref, wg_ref, wu_ref, wd_ref, o_ref,
                xb_ref, wgb_ref, wub_ref, wdb_ref):
    w = pl.program_id(0)
    f = pl.program_id(1)
    tm = x_ref.shape[0]
    sub = MOE_SUB
    sub_shift = int(math.log2(sub))

    def swiglu_part(xb, wg, wu, wd):
        hg = jnp.dot(xb, wg, preferred_element_type=F32)
        hu = jnp.dot(xb, wu, preferred_element_type=F32)
        act = (_silu(hg) * hu).astype(BF16)
        return jnp.dot(act, wd, preferred_element_type=F32)

    @pl.when(tv_ref[w] == 1)
    def _():
        lo = lo_ref[w]
        hi = hi_ref[w]
        whole = jnp.logical_and(lo == 0, hi == tm)

        @pl.when(f == 0)
        def _():
            row = lax.broadcasted_iota(jnp.int32, (tm, 1), 0)
            mine = jnp.logical_and(row >= lo, row < hi)
            xb_ref[...] = jnp.where(mine, x_ref[...], 0.0).astype(BF16)

        @pl.when(whole)
        def _():
            part = swiglu_part(xb_ref[...], wg_ref[0].astype(BF16), wu_ref[0].astype(BF16),
                               wd_ref[0].astype(BF16))

            @pl.when(f == 0)
            def _():
                o_ref[...] = part

            @pl.when(f != 0)
            def _():
                o_ref[...] += part

        @pl.when(jnp.logical_not(whole))
        def _():
            wgb_ref[...] = wg_ref[0].astype(BF16)
            wub_ref[...] = wu_ref[0].astype(BF16)
            wdb_ref[...] = wd_ref[0].astype(BF16)

            def sub_block(s, carry):
                rows = pl.ds(pl.multiple_of(s * sub, sub), sub)
                part = swiglu_part(xb_ref[rows, :], wgb_ref[...], wub_ref[...], wdb_ref[...])
                init = jnp.logical_and(f == 0, lo <= s * sub)

                @pl.when(init)
                def _():
                    o_ref[rows, :] = part

                @pl.when(jnp.logical_not(init))
                def _():
                    o_ref[rows, :] += part

                return carry

            lax.fori_loop(lo >> sub_shift, (hi + sub - 1) >> sub_shift, sub_block, 0)


def _moe_ffn(xs, items, wg, wu, wd):
    rows, d = xs.shape
    tm = MOE_TILE
    tf = MOE_FF_TILE
    nf = wg.shape[2] // tf
    n_items = items[0].shape[0]
    f_idx = lambda f, v: f * v + (nf - 1) * (1 - v)
    return pl.pallas_call(
        _moe_kernel,
        grid_spec=pltpu.PrefetchScalarGridSpec(
            num_scalar_prefetch=5,
            grid=(n_items, nf),
            in_specs=[pl.BlockSpec((tm, d), lambda w, f, ti, te, tv, lo, hi: (ti[w], 0)),
                      pl.BlockSpec((1, d, tf), lambda w, f, ti, te, tv, lo, hi: (te[w], 0, f_idx(f, tv[w]))),
                      pl.BlockSpec((1, d, tf), lambda w, f, ti, te, tv, lo, hi: (te[w], 0, f_idx(f, tv[w]))),
                      pl.BlockSpec((1, tf, d), lambda w, f, ti, te, tv, lo, hi: (te[w], f_idx(f, tv[w]), 0))],
            out_specs=pl.BlockSpec((tm, d), lambda w, f, ti, te, tv, lo, hi: (ti[w], 0)),
            scratch_shapes=[pltpu.VMEM((tm, d), BF16), pltpu.VMEM((d, tf), BF16),
                            pltpu.VMEM((d, tf), BF16), pltpu.VMEM((tf, d), BF16)]),
        out_shape=jax.ShapeDtypeStruct((rows, d), F32),
        compiler_params=_cparams("arbitrary", "arbitrary"),
        name="moe_expert_swiglu",
    )(*items, xs, wg, wu, wd)


def _combine_kernel(d1_ref, d2_ref, ys_ref, x_ref, wt_ref, g2_ref, fw_ref, o_ref, y1_ref, y2_ref, sem):
    i = pl.program_id(0)
    tr = ROW_DMA_TILE
    base = i * tr

    def copies(t):
        return (pltpu.make_async_copy(ys_ref.at[pl.ds(d1_ref[base + t], 1)], y1_ref.at[pl.ds(t, 1)], sem),
                pltpu.make_async_copy(ys_ref.at[pl.ds(d2_ref[base + t], 1)], y2_ref.at[pl.ds(t, 1)], sem))

    def start(t, c):
        a, b = copies(t)
        a.start()
        b.start()
        return c

    def wait(t, c):
        a, b = copies(t)
        a.wait()
        b.wait()
        return c

    lax.fori_loop(0, tr, start, 0)
    lax.fori_loop(0, tr, wait, 0)
    wt = wt_ref[...]
    ffn = wt[:, 0:1] * y1_ref[...] + wt[:, 1:2] * y2_ref[...]
    x4 = x_ref[...] + g2_ref[0] * ffn
    ms = jnp.mean(x4 * x4, axis=-1, keepdims=True)
    o_ref[...] = (x4 * lax.rsqrt(ms + EPS)) * fw_ref[...]


def _combine(ys, d1, d2, x3, wt, g2, final_w, seq_len):
    n, d = x3.shape
    tr = ROW_DMA_TILE
    tps = seq_len // tr
    return pl.pallas_call(
        _combine_kernel,
        grid_spec=pltpu.PrefetchScalarGridSpec(
            num_scalar_prefetch=2,
            grid=(n // tr,),
            in_specs=[pl.BlockSpec(memory_space=pl.ANY),
                      pl.BlockSpec((tr, d), lambda i, a, b: (i, 0)),
                      pl.BlockSpec((tr, LANES), lambda i, a, b: (i, 0)),
                      pl.BlockSpec((1, 1, d), lambda i, a, b: (i // tps, 0, 0)),
                      pl.BlockSpec((1, d), lambda i, a, b: (0, 0))],
            out_specs=pl.BlockSpec((tr, d), lambda i, a, b: (i, 0)),
            scratch_shapes=[pltpu.VMEM((tr, d), F32), pltpu.VMEM((tr, d), F32),
                            pltpu.SemaphoreType.DMA(())]),
        out_shape=jax.ShapeDtypeStruct((n, d), F32),
        compiler_params=_cparams("arbitrary"),
        name="moe_combine_final_norm",
    )(d1, d2, ys, x3, wt, g2, final_w.reshape(1, d))


def _moe_tables(meta, counts):
    n = meta.shape[0]
    tm = MOE_TILE
    n_items = (2 * n) // tm + N_EXPERTS - 1
    cnt = counts[0, :N_EXPERTS]
    end = jnp.cumsum(cnt)
    off = end - cnt
    d1 = jnp.take(off, meta[:, 0]) + meta[:, 2]
    d2 = jnp.take(off, meta[:, 1]) + meta[:, 3]
    first_tile = off // tm
    n_e = jnp.where(cnt > 0, (end - 1) // tm - first_tile + 1, 0)
    item_end = jnp.cumsum(n_e)
    item_start = item_end - n_e
    total = item_end[-1]
    w = jnp.arange(n_items, dtype=jnp.int32)
    w_eff = jnp.minimum(w, total - 1)
    te = jnp.minimum(jnp.sum((item_end[None, :] <= w_eff[:, None]).astype(jnp.int32), axis=1),
                     N_EXPERTS - 1)
    ti = jnp.take(first_tile, te) + (w_eff - jnp.take(item_start, te))
    tv = (w < total).astype(jnp.int32)
    lo = jnp.clip(jnp.take(off, te) - ti * tm, 0, tm) * tv
    hi = jnp.clip(jnp.take(end, te) - ti * tm, 0, tm) * tv
    i32 = lambda t: t.astype(jnp.int32)
    return i32(d1), i32(d2), (i32(ti), i32(te), tv, i32(lo), i32(hi))


def kernel(x, c, rel_bias, ada_w, ada_b, norm_mix_w, norm_ffn_w, final_norm_w, ab_w_in, attn_sinks,
           dn_conv_w, dn_a_log, dn_dt_bias, dn_norm_w, ab_w_out, ffn_w_gate, ffn_w_up, ffn_w_down,
           cd_w_in, lru_conv_w, lru_conv_b, lru_gate_a_w, lru_gate_a_b, lru_gate_x_w, lru_gate_x_b,
           lru_lambda, sconv_w, cd_w_out, moe_router_w, moe_router_b, moe_w_gate, moe_w_up, moe_w_down):
    bsz, seq_len, d = x.shape
    n = bsz * seq_len
    x2d = x.reshape(n, d)
    mods = _ada_mods(c, ada_w, ada_b)

    sh1, sc1, g1, sh2, sc2, g2 = (mods[0, k] for k in range(6))
    qa, kd, vd, qn, kn, vb, gs, beta, g = _in_proj0(
        x2d, norm_mix_w[0], sc1, sh1, ab_w_in[0], dn_conv_w[0], dn_a_log[0], dn_dt_bias[0], seq_len)
    attn = _attention(qa, kd, vd, _bias_table(rel_bias), attn_sinks[0], seq_len)
    dn = _deltanet(qn, kn, vb, gs, beta, g, dn_norm_w[0], seq_len)
    x2 = _mid0(attn, dn, x2d, ab_w_out[0], g1, norm_ffn_w[0], sc2, sh2, g2,
               ffn_w_gate[0], ffn_w_up[0], ffn_w_down[0], seq_len)

    sh1, sc1, g1, sh2, sc2, g2 = (mods[1, k] for k in range(6))
    cat = _mix1(x2, norm_mix_w[1], sc1, sh1, cd_w_in[0], lru_conv_w[0], lru_conv_b[0],
                lru_gate_a_w[0], lru_gate_a_b[0], lru_gate_x_w[0], lru_gate_x_b[0],
                lru_lambda[0], sconv_w[0], seq_len)
    x3, hn4, meta, wt, counts = _route(cat, x2, cd_w_out[0], g1, norm_ffn_w[1], sc2, sh2,
                                       moe_router_w[0], moe_router_b[0], seq_len)
    d1, d2, items = _moe_tables(meta, counts)
    xs = _dispatch(hn4, d1, d2, 2 * n)
    ys = _moe_ffn(xs, items, moe_w_gate[0], moe_w_up[0], moe_w_down[0])
    out = _combine(ys, d1, d2, x3, wt, g2, final_norm_w, seq_len)
    return out.reshape(bsz, seq_len, d)
```

```python
import functools
import math

import numpy as np
import jax
import jax.numpy as jnp
from jax import lax
from jax.experimental import pallas as pl
from jax.experimental.pallas import tpu as pltpu

D_MODEL = 1024
EPS = 1e-6
HEAD_DIM = 64
A_Q_HEADS = 8
A_KV_HEADS = 2
WINDOW = 128
N_BUCKETS = 32
MAX_DISTANCE = 128
B_HEADS = 8
B_CONV = 4
CHUNK = 64
A_Q_W = A_Q_HEADS * HEAD_DIM
A_KV_W = A_KV_HEADS * HEAD_DIM
B_W = B_HEADS * HEAD_DIM
B_QKV_W = 3 * B_W
LRU_WIDTH = D_MODEL
LRU_BLOCKS = 8
LRU_C = 8.0
SC_WIDTH = D_MODEL // 2
D_FF = 2816
N_EXPERTS = 8
D_FF_EXPERT = 3584

LANES = 128
SUBLANES = 8
VMEM_LIMIT_BYTES = 56 * 1024 * 1024
TOKEN_TILE = 512
MOE_TILE = 1024
MOE_SUB = 256
MOE_FF_TILE = 512
ROW_DMA_TILE = 256
ROW_DMA_UNROLL = 8
NEG_BIG = -1e30

F32 = jnp.float32
BF16 = jnp.bfloat16


def _cparams(*sem):
    return pltpu.CompilerParams(dimension_semantics=tuple(sem), vmem_limit_bytes=VMEM_LIMIT_BYTES)


def _const_spec(shape):
    nd = len(shape)
    return pl.BlockSpec(shape, lambda *_: (0,) * nd)


def _bdot(a, b):
    return jnp.dot(a.astype(BF16), b.astype(BF16), preferred_element_type=F32)


def _bdot_nt(a, b):
    return lax.dot_general(a.astype(BF16), b.astype(BF16), (((1,), (1,)), ((), ())),
                           preferred_element_type=F32)


def _bdot_tn(a, b):
    return lax.dot_general(a.astype(BF16), b.astype(BF16), (((0,), (0,)), ((), ())),
                           preferred_element_type=F32)


def _split(x, n):
    parts = []
    r = x
    for i in range(n):
        p = r.astype(BF16)
        parts.append(p)
        if i + 1 < n:
            r = r - p.astype(F32)
    return parts


def _dot_x(a, b, na=2, nb=2):
    asp = _split(a, na) if na > 1 else [a.astype(BF16)]
    bsp = _split(b, nb) if nb > 1 else [b.astype(BF16)]
    acc = None
    for i, ai in enumerate(asp):
        for j, bj in enumerate(bsp):
            if i + j >= max(na, nb):
                continue
            t = jnp.dot(ai, bj, preferred_element_type=F32)
            acc = t if acc is None else acc + t
    return acc


def _silu(x):
    return x * (1.0 / (1.0 + jnp.exp(-x)))


def _sigmoid(x):
    return 1.0 / (1.0 + jnp.exp(-x))


def _log1p(z):
    u = 1.0 + z
    tiny = u == 1.0
    return jnp.where(tiny, z, jnp.log(u) * (z / jnp.where(tiny, 1.0, u - 1.0)))


def _softplus(x):
    return jnp.maximum(x, 0.0) + _log1p(jnp.exp(-jnp.abs(x)))


def _neg_expm1(y):
    return -jnp.tanh(0.5 * y) * (jnp.exp(y) + 1.0)


def _rms_scale(x):
    width = x.shape[1]
    mean_w = jnp.full((width, LANES), 1.0 / width, BF16)
    ms = _dot_x(x * x, mean_w, 2, 1)
    r = lax.rsqrt(ms + EPS)
    return jnp.concatenate([r] * (width // LANES), axis=1)


def _norm_mod(x, w, sc, sh):
    return (x * _rms_scale(x)) * w * (1.0 + sc) + sh


def _shift_rows(x, k, prev_tail):
    n, width = x.shape
    x3 = x.reshape(n // SUBLANES, SUBLANES, width)
    rot = pltpu.roll(x3, k, 1)
    rot_prev = jnp.concatenate([pltpu.roll(prev_tail, k, 0)[None], rot[:-1]], axis=0)
    sub = lax.broadcasted_iota(jnp.int32, x3.shape, 1)
    return jnp.where(sub >= k, rot, rot_prev).reshape(n, width)


def _ada_kernel(c_ref, w_ref, b_ref, o_ref):
    c = c_ref[...]
    cond = _silu(c)
    o_ref[0] = _dot_x(cond, w_ref[0], 3, 3) + b_ref[0]


def _ada_mods(c, ada_w, ada_b):
    depth, d, six_d = ada_w.shape
    bsz = c.shape[0]
    rows = max(SUBLANES, bsz)
    c_pad = jnp.zeros((rows, d), F32).at[:bsz].set(c)
    tn = 1536
    out = pl.pallas_call(
        _ada_kernel,
        grid=(depth, six_d // tn),
        in_specs=[pl.BlockSpec((rows, d), lambda l, j: (0, 0)),
                  pl.BlockSpec((1, d, tn), lambda l, j: (l, 0, j)),
                  pl.BlockSpec((1, 1, tn), lambda l, j: (l, 0, j))],
        out_specs=pl.BlockSpec((1, rows, tn), lambda l, j: (l, 0, j)),
        out_shape=jax.ShapeDtypeStruct((depth, rows, six_d), F32),
        compiler_params=_cparams("parallel", "parallel"),
        name="ada_mods",
    )(c_pad, ada_w, ada_b.reshape(depth, 1, six_d))
    return out[:, :bsz].reshape(depth, bsz, 6, 1, d).transpose(0, 2, 1, 3, 4)


def _t5_bucket(dist):
    max_exact = N_BUCKETS // 2
    d = np.maximum(dist, 0)
    large = max_exact + (np.log(np.maximum(d, 1) / max_exact) / math.log(MAX_DISTANCE / max_exact)
                         * (N_BUCKETS - max_exact)).astype(np.int32)
    large = np.minimum(large, N_BUCKETS - 1)
    return np.where(d < max_exact, d, large).astype(np.int32)


def _band_buckets():
    qi = np.arange(WINDOW)[:, None]
    s = np.arange(2 * WINDOW)[None, :]
    dist = qi + WINDOW - s
    in_window = (dist >= 0) & (dist < WINDOW)
    return np.where(in_window, _t5_bucket(dist), -1).astype(np.int32)


def _bias_kernel(rb_ref, bucket_ref, o_ref):
    h = pl.program_id(0)
    bucket = bucket_ref[...]
    acc = jnp.zeros(bucket.shape, F32)
    for b in range(N_BUCKETS):
        acc = jnp.where(bucket == b, rb_ref[b, h], acc)
    o_ref[0] = jnp.where(bucket < 0, NEG_BIG, acc)


def _bias_table(rel_bias):
    bucket = jnp.asarray(_band_buckets())
    out = pl.pallas_call(
        _bias_kernel,
        grid=(A_Q_HEADS,),
        in_specs=[pl.BlockSpec(memory_space=pltpu.SMEM),
                  _const_spec((WINDOW, 2 * WINDOW))],
        out_specs=pl.BlockSpec((1, WINDOW, 2 * WINDOW), lambda h: (h, 0, 0)),
        out_shape=jax.ShapeDtypeStruct((A_Q_HEADS, WINDOW, 2 * WINDOW), F32),
        compiler_params=_cparams("parallel"),
        name="attn_bias_table",
    )(rel_bias, bucket)
    return out.reshape(A_Q_HEADS // 2, 2 * WINDOW, 2 * WINDOW)


_C_QA = 0
_C_KD = _C_QA + A_Q_W
_C_VD = _C_KD + 2 * A_KV_W
_C_QKV = _C_VD + 2 * A_KV_W
_C_GATE = _C_QKV + B_QKV_W
_C_SMALL = _C_GATE + B_W
_AB_COLS = _C_SMALL + LANES


def _ab_in_weight(w_in):
    qa, ka, va, qkv, gate, beta, dec = jnp.split(
        w_in, list(np.cumsum([A_Q_W, A_KV_W, A_KV_W, B_QKV_W, B_W, B_HEADS])), axis=1)

    def dup(t):
        return jnp.concatenate([t[:, :HEAD_DIM]] * 2 + [t[:, HEAD_DIM:]] * 2, axis=1)

    small = jnp.zeros((w_in.shape[0], LANES), w_in.dtype)
    small = small.at[:, :B_HEADS].set(beta).at[:, B_HEADS:2 * B_HEADS].set(dec)
    return jnp.concatenate([qa, dup(ka), dup(va), qkv, gate, small], axis=1).astype(BF16)


def _head_selector():
    e = np.zeros((B_W, LANES), np.float32)
    for h in range(B_HEADS):
        e[h * HEAD_DIM:(h + 1) * HEAD_DIM, h] = 1.0
    return e


def _in0_kernel(x_ref, nw_ref, sc_ref, sh_ref, w_ref, cw_ref, sel_ref, selt_ref, alog_ref, dtb_ref,
                qa_ref, kd_ref, vd_ref, qn_ref, kn_ref, vb_ref, gs_ref, beta_ref, g_ref,
                tail_ref, *, tiles_per_seq):
    i = pl.program_id(0)

    @pl.when(i % tiles_per_seq == 0)
    def _():
        tail_ref[...] = jnp.zeros_like(tail_ref)

    hn = _norm_mod(x_ref[...], nw_ref[...], sc_ref[0], sh_ref[0])
    proj = jnp.dot(hn.astype(BF16), w_ref[...], preferred_element_type=F32)
    qa_ref[...] = proj[:, _C_QA:_C_KD].astype(BF16)
    kd_ref[...] = proj[:, _C_KD:_C_VD].astype(BF16)
    vd_ref[...] = proj[:, _C_VD:_C_QKV].astype(BF16)

    xq = proj[:, _C_QKV:_C_GATE]
    tail = tail_ref[...]
    cw = cw_ref[...]
    y = xq * cw[B_CONV - 1:B_CONV]
    for k in range(1, B_CONV):
        y = y + _shift_rows(xq, k, tail) * cw[B_CONV - 1 - k:B_CONV - k]
    tail_ref[...] = xq[xq.shape[0] - SUBLANES:]
    y = _silu(y)
    q, k_, v = y[:, :B_W], y[:, B_W:2 * B_W], y[:, 2 * B_W:]

    def l2n(t):
        ssq = _dot_x(t * t, sel_ref[...], 2, 1)
        r = lax.rsqrt(ssq + EPS)
        return t * _dot_x(r, selt_ref[...], 2, 1)

    qn_ref[...] = l2n(q) * (HEAD_DIM ** -0.5)
    kn_ref[...] = l2n(k_)
    vb_ref[...] = v
    gs_ref[...] = _silu(proj[:, _C_GATE:_C_SMALL])
    small = proj[:, _C_SMALL:]
    lane = lax.broadcasted_iota(jnp.int32, small.shape, 1)
    beta_ref[...] = jnp.where(lane < B_HEADS, _sigmoid(small), 0.0)
    dec = pltpu.roll(small, LANES - B_HEADS, 1)
    g = -jnp.exp(alog_ref[...]) * _softplus(dec + dtb_ref[...])
    g_ref[...] = jnp.where(lane < B_HEADS, g, 0.0)


def _in_proj0(x2d, nw, sc, sh, w_in, conv_w, a_log, dt_bias, seq_len):
    n, d = x2d.shape
    tm = TOKEN_TILE
    tiles_per_seq = seq_len // tm
    w = _ab_in_weight(w_in)
    sel = jnp.asarray(_head_selector(), BF16)
    selt = jnp.asarray(_head_selector().T.copy(), BF16)
    pad8 = lambda v: jnp.zeros((1, LANES), F32).at[0, :B_HEADS].set(v)
    row = lambda width: pl.BlockSpec((tm, width), lambda i: (i, 0))
    per_b = pl.BlockSpec((1, 1, d), lambda i: (i // tiles_per_seq, 0, 0))
    outs = pl.pallas_call(
        functools.partial(_in0_kernel, tiles_per_seq=tiles_per_seq),
        grid=(n // tm,),
        in_specs=[row(d), _const_spec((1, d)), per_b, per_b,
                  _const_spec((d, _AB_COLS)), _const_spec((B_CONV, B_QKV_W)),
                  _const_spec((B_W, LANES)), _const_spec((LANES, B_W)),
                  _const_spec((1, LANES)), _const_spec((1, LANES))],
        out_specs=[row(A_Q_W), row(2 * A_KV_W), row(2 * A_KV_W),
                   row(B_W), row(B_W), row(B_W), row(B_W), row(LANES), row(LANES)],
        out_shape=[jax.ShapeDtypeStruct((n, A_Q_W), BF16),
                   jax.ShapeDtypeStruct((n, 2 * A_KV_W), BF16),
                   jax.ShapeDtypeStruct((n, 2 * A_KV_W), BF16),
                   jax.ShapeDtypeStruct((n, B_W), F32),
                   jax.ShapeDtypeStruct((n, B_W), F32),
                   jax.ShapeDtypeStruct((n, B_W), F32),
                   jax.ShapeDtypeStruct((n, B_W), F32),
                   jax.ShapeDtypeStruct((n, LANES), F32),
                   jax.ShapeDtypeStruct((n, LANES), F32)],
        scratch_shapes=[pltpu.VMEM((SUBLANES, B_QKV_W), F32)],
        compiler_params=_cparams("arbitrary"),
        name="in_proj0",
    )(x2d, nw.reshape(1, d), sc, sh, w, conv_w, sel, selt, pad8(a_log), pad8(dt_bias))
    return outs


def _attn_kernel(sink_ref, q_ref, kp_ref, kc_ref, vp_ref, vc_ref, bm_ref, o_ref, *, blocks_per_seq):
    i = pl.program_id(0)
    first = (i % blocks_per_seq) == 0
    w = WINDOW
    lane = lax.broadcasted_iota(jnp.int32, (w, LANES), 1)
    low = lane < HEAD_DIM
    col = lax.broadcasted_iota(jnp.int32, (2 * w, 2 * w), 1)
    row = lax.broadcasted_iota(jnp.int32, (2 * w, 1), 0)
    prev_dead = jnp.logical_and(first, col < w)
    q_all = q_ref[...]
    zero = jnp.zeros((), q_all.dtype)
    outs = []
    for j in range(A_Q_HEADS // 2):
        kh = (2 * j) // (A_Q_HEADS // A_KV_HEADS)
        qp = q_all[:, j * LANES:(j + 1) * LANES]
        qs = jnp.concatenate([jnp.where(low, qp, zero), jnp.where(low, zero, qp)], axis=0)
        kd = jnp.concatenate([kp_ref[:, kh * LANES:(kh + 1) * LANES],
                              kc_ref[:, kh * LANES:(kh + 1) * LANES]], axis=0)
        vd = jnp.concatenate([vp_ref[:, kh * LANES:(kh + 1) * LANES],
                              vc_ref[:, kh * LANES:(kh + 1) * LANES]], axis=0)
        s = lax.dot_general(qs, kd, (((1,), (1,)), ((), ())), preferred_element_type=F32)
        s = s * (HEAD_DIM ** -0.5) + bm_ref[j]
        s = jnp.where(prev_dead, NEG_BIG, s)
        sink = jnp.where(row < w, sink_ref[2 * j], sink_ref[2 * j + 1])
        m = jnp.maximum(jnp.max(s, axis=-1, keepdims=True), sink)
        p = jnp.exp(s - m)
        denom = jnp.sum(p, axis=-1, keepdims=True) + jnp.exp(sink - m)
        pv = jnp.dot(p.astype(BF16), vd, preferred_element_type=F32) / denom
        outs.append(jnp.where(low, pv[:w], pv[w:]))
    o_ref[...] = jnp.concatenate(outs, axis=1).astype(o_ref.dtype)


def _attention(qa, kd, vd, bias_tbl, sinks, seq_len):
    n = qa.shape[0]
    w = WINDOW
    nb = seq_len // w
    cur = lambda i: (i, 0)
    prev = lambda i: (jnp.where(i % nb == 0, i, i - 1), 0)
    return pl.pallas_call(
        functools.partial(_attn_kernel, blocks_per_seq=nb),
        grid=(n // w,),
        in_specs=[pl.BlockSpec(memory_space=pltpu.SMEM),
                  pl.BlockSpec((w, A_Q_W), cur),
                  pl.BlockSpec((w, 2 * A_KV_W), prev), pl.BlockSpec((w, 2 * A_KV_W), cur),
                  pl.BlockSpec((w, 2 * A_KV_W), prev), pl.BlockSpec((w, 2 * A_KV_W), cur),
                  _const_spec((A_Q_HEADS // 2, 2 * w, 2 * w))],
        out_specs=pl.BlockSpec((w, A_Q_W), cur),
        out_shape=jax.ShapeDtypeStruct((n, A_Q_W), BF16),
        compiler_params=_cparams("parallel"),
        name="swa_attention",
    )(sinks, qa, kd, kd, vd, vd, bias_tbl)


_DN_PAIRS = B_HEADS // 2
_DN_INV_BLOCK = 16
_DN_GROUP = 4


def _block_diag(x, low):
    zero = jnp.zeros((), x.dtype)
    return jnp.concatenate([jnp.where(low, x, zero), jnp.where(low, zero, x)], axis=0)


def _dn_intra(chunks, data_refs, work_refs, consts):
    qn_ref, kn_ref, vb_ref, bexp_ref, gcexp_ref = data_refs
    u_ref, w_ref, qk_ref, qd_ref, kd_ref, egl_ref = work_refs
    low, i_idx, j_idx, ones3 = consts
    c = CHUNK
    units = [(ci, p) for ci in chunks for p in range(_DN_PAIRS)]
    where = [(slice(ci * c, (ci + 1) * c), slice(p * LANES, (p + 1) * LANES)) for ci, p in units]
    causal = i_idx >= j_idx
    strict = i_idx > j_idx
    on_diag = i_idx == j_idx
    eye = on_diag.astype(F32)
    blk_shift = int(math.log2(_DN_INV_BLOCK))
    same_blk = (i_idx >> blk_shift) == (j_idx >> blk_shift)

    q = [qn_ref[rs, ls] for rs, ls in where]
    k = [kn_ref[rs, ls] for rs, ls in where]
    v = [vb_ref[rs, ls] for rs, ls in where]
    b = [bexp_ref[rs, ls] for rs, ls in where]
    gc = [gcexp_ref[rs, ls] for rs, ls in where]

    gr = [jnp.dot(ones3, jnp.concatenate(_split(jnp.where(on_diag, t, 0.0), 3), axis=0),
                  preferred_element_type=F32) for t in gc]
    ks = [_block_diag(t.astype(BF16), low) for t in k]
    qkk = [lax.dot_general(jnp.concatenate([qt, kt], axis=0).astype(BF16), kst,
                           (((1,), (1,)), ((), ())), preferred_element_type=F32)
           for qt, kt, kst in zip(q, k, ks)]
    decay = [jnp.exp(jnp.where(causal, gct - grt, NEG_BIG)) for gct, grt in zip(gc, gr)]
    lmat = [jnp.where(strict, bt * t[c:] * dt, 0.0) for bt, t, dt in zip(b, qkk, decay)]
    qk = [jnp.where(causal, t[:c] * dt, 0.0) for t, dt in zip(qkk, decay)]

    def mm(xs, ys):
        return [_bdot(x, _block_diag(y.astype(BF16), low)) for x, y in zip(xs, ys)]

    l_diag = [jnp.where(same_blk, t, 0.0) for t in lmat]
    l_off = [t - d for t, d in zip(lmat, l_diag)]
    pw = [-t for t in l_diag]
    d_inv = [eye + t for t in pw]
    for _ in range(blk_shift - 1):
        pw = mm(pw, pw)
        d_inv = mm(d_inv, [eye + t for t in pw])
    pw = [-t for t in mm(d_inv, l_off)]
    acc = [eye + t for t in pw]
    for _ in range(int(math.log2(c // _DN_INV_BLOCK)) - 1):
        pw = mm(pw, pw)
        acc = mm(acc, [eye + t for t in pw])
    tmat = mm(acc, d_inv)

    egc = [jnp.exp(t) for t in gc]
    rhs = [jnp.concatenate([_block_diag((vt * bt).astype(BF16), low),
                            _block_diag((kt * (bt * et)).astype(BF16), low)], axis=1)
           for vt, kt, bt, et in zip(v, k, b, egc)]
    uw = [_bdot(t, r) for t, r in zip(tmat, rhs)]
    for n, (ci, p) in enumerate(units):
        g_last = gc[n][c - 1:c, :]
        u_ref[ci, p] = uw[n][:, :LANES]
        w_ref[ci, p] = uw[n][:, LANES:]
        qk_ref[ci, p] = qk[n]
        qd_ref[ci, p] = q[n] * egc[n]
        kd_ref[ci, p] = k[n] * jnp.exp(g_last - gc[n])
        egl_ref[ci, p] = jnp.broadcast_to(jnp.exp(g_last), (SUBLANES, LANES))


def _dn_scan(ci, work_refs, s_ref, gs_ref, nw, o_ref, consts):
    u_ref, w_ref, qk_ref, qd_ref, kd_ref, egl_ref = work_refs
    low, mask_bd, head_mean2 = consts
    c = CHUNK
    rows = slice(ci * c, (ci + 1) * c)
    pairs = range(_DN_PAIRS)
    s_old = [s_ref[p] for p in pairs]
    wq = [_bdot(jnp.concatenate([w_ref[ci, p], qd_ref[ci, p]], axis=0), s_old[p]) for p in pairs]
    v_new = [u_ref[ci, p] - wq[p][:c] for p in pairs]
    o = [wq[p][c:] + _bdot(qk_ref[ci, p], _block_diag(v_new[p].astype(BF16), low)) for p in pairs]
    kv = [_bdot_tn(kd_ref[ci, p], v_new[p]) for p in pairs]
    for p in pairs:
        s_ref[p] = s_old[p] * egl_ref[ci, p][0:1, :] + jnp.where(mask_bd, kv[p], 0.0)
    ms = [jnp.dot(jnp.concatenate(_split(t * t, 2), axis=1), head_mean2, preferred_element_type=F32)
          for t in o]
    for p in pairs:
        ls = slice(p * LANES, (p + 1) * LANES)
        y = (o[p] * lax.rsqrt(ms[p] + EPS)) * nw * gs_ref[rows, ls]
        o_ref[rows, ls] = y.astype(o_ref.dtype)


def _dn_kernel(qn_ref, kn_ref, vb_ref, gs_ref, beta_ref, g_ref, selt_ref, nw_ref, o_ref,
               s_ref, bexp_ref, gcexp_ref, u_ref, w_ref, qk_ref, qd_ref, kd_ref, egl_ref, *, groups_per_seq):
    i = pl.program_id(0)

    @pl.when(i % groups_per_seq == 0)
    def _():
        s_ref[...] = jnp.zeros_like(s_ref)

    c = CHUNK
    tm = o_ref.shape[0]
    n_chunks = tm // c
    bexp_ref[...] = _dot_x(beta_ref[...], selt_ref[...], 3, 1)
    rt = lax.broadcasted_iota(jnp.int32, (tm, tm), 0)
    ct = lax.broadcasted_iota(jnp.int32, (tm, tm), 1)
    chunk_shift = int(math.log2(c))
    tril_bd = jnp.logical_and(rt >= ct, (rt >> chunk_shift) == (ct >> chunk_shift)).astype(BF16)
    gc8 = _dot_x(tril_bd, g_ref[...], 1, 3)
    gcexp_ref[...] = _dot_x(gc8, selt_ref[...], 3, 1)

    lane = lax.broadcasted_iota(jnp.int32, (c, LANES), 1)
    low = lane < HEAD_DIM
    i_idx = lax.broadcasted_iota(jnp.int32, (c, LANES), 0)
    j_idx = lane & (c - 1)
    ones3 = jnp.ones((c, 3 * c), BF16)
    rb = lax.broadcasted_iota(jnp.int32, (LANES, LANES), 0)
    cb = lax.broadcasted_iota(jnp.int32, (LANES, LANES), 1)
    mask_bd = (rb < HEAD_DIM) == (cb < HEAD_DIM)
    head_mean = jnp.where(mask_bd, 1.0 / HEAD_DIM, 0.0).astype(BF16)
    head_mean2 = jnp.concatenate([head_mean, head_mean], axis=0)
    data_refs = (qn_ref, kn_ref, vb_ref, bexp_ref, gcexp_ref)
    work_refs = (u_ref, w_ref, qk_ref, qd_ref, kd_ref, egl_ref)
    intra_consts = (low, i_idx, j_idx, ones3)
    scan_consts = (low, mask_bd, head_mean2)
    nw = nw_ref[...]

    groups = [list(range(s, s + _DN_GROUP)) for s in range(0, n_chunks, _DN_GROUP)]
    _dn_intra(groups[0], data_refs, work_refs, intra_consts)
    for j, grp in enumerate(groups):
        if j + 1 < len(groups):
            _dn_intra(groups[j + 1], data_refs, work_refs, intra_consts)
        for ci in grp:
            _dn_scan(ci, work_refs, s_ref, gs_ref, nw, o_ref, scan_consts)


def _deltanet(qn, kn, vb, gs, beta, g, norm_w, seq_len):
    n = qn.shape[0]
    tm = TOKEN_TILE
    selt = jnp.asarray(_head_selector().T.copy(), BF16)
    nw2 = jnp.concatenate([norm_w, norm_w]).reshape(1, LANES)
    row = lambda width: pl.BlockSpec((tm, width), lambda i: (i, 0))
    return pl.pallas_call(
        functools.partial(_dn_kernel, groups_per_seq=seq_len // tm),
        grid=(n // tm,),
        in_specs=[row(B_W), row(B_W), row(B_W), row(B_W), row(LANES), row(LANES),
                  _const_spec((LANES, B_W)), _const_spec((1, LANES))],
        out_specs=row(B_W),
        out_shape=jax.ShapeDtypeStruct((n, B_W), BF16),
        scratch_shapes=[pltpu.VMEM((_DN_PAIRS, LANES, LANES), F32),
                        pltpu.VMEM((tm, B_W), F32), pltpu.VMEM((tm, B_W), F32)]
        + [pltpu.VMEM((tm // CHUNK, _DN_PAIRS, CHUNK, LANES), F32)] * 5
        + [pltpu.VMEM((tm // CHUNK, _DN_PAIRS, SUBLANES, LANES), F32)],
        compiler_params=_cparams("arbitrary"),
        name="gated_deltanet",
    )(qn, kn, vb, gs, beta, g, selt, nw2)


def _resident_spec(shape):
    nd = len(shape)
    return pl.BlockSpec(shape, lambda *_: (0,) * nd, pipeline_mode=pl.Buffered(1))


def _mid0_kernel(attn_ref, dn_ref, x_ref, wo_ref, g1_ref, nw_ref, sc_ref, sh_ref, g2_ref,
                 wg_ref, wu_ref, wd_ref, o_ref):
    mix = (jnp.dot(attn_ref[...], wo_ref[:A_Q_W], preferred_element_type=F32)
           + jnp.dot(dn_ref[...], wo_ref[A_Q_W:], preferred_element_type=F32))
    x1 = x_ref[...] + g1_ref[0] * mix
    hn = _norm_mod(x1, nw_ref[...], sc_ref[0], sh_ref[0]).astype(BF16)
    hg = jnp.dot(hn, wg_ref[...], preferred_element_type=F32)
    hu = jnp.dot(hn, wu_ref[...], preferred_element_type=F32)
    act = (_silu(hg) * hu).astype(BF16)
    o_ref[...] = x1 + g2_ref[0] * jnp.dot(act, wd_ref[...], preferred_element_type=F32)


def _mid0(attn, dn, x2d, w_out, g1, nw, sc, sh, g2, wg, wu, wd, seq_len):
    n, d = x2d.shape
    tm = TOKEN_TILE
    tps = seq_len // tm
    row = lambda width: pl.BlockSpec((tm, width), lambda i: (i, 0))
    per_b = pl.BlockSpec((1, 1, d), lambda i: (i // tps, 0, 0))
    return pl.pallas_call(
        _mid0_kernel,
        grid=(n // tm,),
        in_specs=[row(A_Q_W), row(B_W), row(d), _resident_spec(w_out.shape), per_b,
                  _const_spec((1, d)), per_b, per_b, per_b,
                  _resident_spec(wg.shape), _resident_spec(wu.shape), _resident_spec(wd.shape)],
        out_specs=row(d),
        out_shape=jax.ShapeDtypeStruct((n, d), F32),
        compiler_params=_cparams("parallel"),
        name="out_proj0_swiglu",
    )(attn, dn, x2d, w_out.astype(BF16), g1, nw.reshape(1, d), sc, sh, g2,
      wg.astype(BF16), wu.astype(BF16), wd.astype(BF16))


def _gelu_tanh(x):
    return 0.5 * x * (1.0 + jnp.tanh(math.sqrt(2.0 / math.pi) * (x + 0.044715 * (x * x * x))))


def _linear_scan(a, b, h0):
    n, width = a.shape
    groups = n // SUBLANES
    a = a.reshape(groups, SUBLANES, width)
    b = b.reshape(groups, SUBLANES, width)
    in_group = lax.broadcasted_iota(jnp.int32, a.shape, 1)
    s = 1
    while s < SUBLANES:
        a_sh = pltpu.roll(a, s, 1)
        b_sh = pltpu.roll(b, s, 1)
        valid = in_group >= s
        b = jnp.where(valid, a * b_sh + b, b)
        a = jnp.where(valid, a * a_sh, a)
        s *= 2
    carry = jnp.broadcast_to(h0, (SUBLANES, width))
    out = []
    for g in range(groups):
        hg = a[g] * carry + b[g]
        out.append(hg)
        carry = jnp.broadcast_to(hg[SUBLANES - 1:SUBLANES, :], hg.shape)
    return jnp.concatenate(out, axis=0)


def _mix1_kernel(x_ref, nw_ref, sc_ref, sh_ref, w_ref, cw_ref, cb_ref, ga_ref, gab_ref, gx_ref, gxb_ref,
                 lam_ref, sw_ref, o_ref, tail_c_ref, tail_d_ref, h_ref, *, tiles_per_seq):
    i = pl.program_id(0)

    @pl.when(i % tiles_per_seq == 0)
    def _():
        tail_c_ref[...] = jnp.zeros_like(tail_c_ref)
        tail_d_ref[...] = jnp.zeros_like(tail_d_ref)
        h_ref[...] = jnp.zeros_like(h_ref)

    hn = _norm_mod(x_ref[...], nw_ref[...], sc_ref[0], sh_ref[0]).astype(BF16)
    proj = jnp.dot(hn, w_ref[...], preferred_element_type=F32)
    w_l = LRU_WIDTH
    xc_in = proj[:, :w_l]
    yc = proj[:, w_l:2 * w_l]
    bd = proj[:, 2 * w_l:2 * w_l + SC_WIDTH]
    cd = proj[:, 2 * w_l + SC_WIDTH:2 * w_l + 2 * SC_WIDTH]
    hd = proj[:, 2 * w_l + 2 * SC_WIDTH:]
    tm = xc_in.shape[0]

    kc = cw_ref.shape[0]
    tail = tail_c_ref[...]
    cw = cw_ref[...]
    xc = xc_in * cw[kc - 1:kc] + cb_ref[...]
    for k in range(1, kc):
        xc = xc + _shift_rows(xc_in, k, tail) * cw[kc - 1 - k:kc - k]
    tail_c_ref[...] = xc_in[tm - SUBLANES:]

    xb = xc.astype(BF16)
    gw = ga_ref.shape[1]
    ra, ri = [], []
    for p in range(ga_ref.shape[0]):
        xin = xb[:, p * gw:(p + 1) * gw]
        ra.append(jnp.dot(xin, ga_ref[p], preferred_element_type=F32))
        ri.append(jnp.dot(xin, gx_ref[p], preferred_element_type=F32))
    r = _sigmoid(jnp.concatenate(ra, axis=1) + gab_ref[...])
    ig = _sigmoid(jnp.concatenate(ri, axis=1) + gxb_ref[...])
    log_a = (-LRU_C) * r * _softplus(-lam_ref[...])
    a = jnp.exp(log_a)
    b = jnp.sqrt(_neg_expm1(2.0 * log_a)) * (ig * xc)
    h = _linear_scan(a, b, h_ref[0:1, :])
    h_ref[...] = jnp.broadcast_to(h[tm - 1:tm, :], h_ref.shape)
    yc_out = h * _gelu_tanh(yc)

    ks = sw_ref.shape[0]
    ch = cd * hd
    tail_d = tail_d_ref[...]
    sw = sw_ref[...]
    conv = ch * sw[ks - 1:ks]
    for k in range(1, ks):
        conv = conv + _shift_rows(ch, k, tail_d) * sw[ks - 1 - k:ks - k]
    tail_d_ref[...] = ch[tm - SUBLANES:]
    o_ref[...] = jnp.concatenate([yc_out, bd * conv], axis=1).astype(o_ref.dtype)


def _pair_block_diag(gw):
    nb, bw, _ = gw.shape
    g2 = gw.reshape(nb // 2, 2, bw, bw)
    z = jnp.zeros((nb // 2, bw, bw), gw.dtype)
    top = jnp.concatenate([g2[:, 0], z], axis=2)
    bot = jnp.concatenate([z, g2[:, 1]], axis=2)
    return jnp.concatenate([top, bot], axis=1).astype(BF16)


def _mix1(x2d, nw, sc, sh, w_in, conv_w, conv_b, ga_w, ga_b, gx_w, gx_b, lam, sconv_w, seq_len):
    n, d = x2d.shape
    tm = TOKEN_TILE
    tps = seq_len // tm
    cd_in = w_in.shape[1]
    cd_out = LRU_WIDTH + SC_WIDTH
    row = lambda width: pl.BlockSpec((tm, width), lambda i: (i, 0))
    per_b = pl.BlockSpec((1, 1, d), lambda i: (i // tps, 0, 0))
    ga = _pair_block_diag(ga_w)
    gx = _pair_block_diag(gx_w)
    vec = lambda v: v.reshape(1, -1)
    return pl.pallas_call(
        functools.partial(_mix1_kernel, tiles_per_seq=tps),
        grid=(n // tm,),
        in_specs=[row(d), _const_spec((1, d)), per_b, per_b, _resident_spec((d, cd_in)),
                  _const_spec(conv_w.shape), _const_spec((1, LRU_WIDTH)),
                  _const_spec(ga.shape), _const_spec((1, LRU_WIDTH)),
                  _const_spec(gx.shape), _const_spec((1, LRU_WIDTH)),
                  _const_spec((1, LRU_WIDTH)), _const_spec(sconv_w.shape)],
        out_specs=row(cd_out),
        out_shape=jax.ShapeDtypeStruct((n, cd_out), BF16),
        scratch_shapes=[pltpu.VMEM((SUBLANES, LRU_WIDTH), F32), pltpu.VMEM((SUBLANES, SC_WIDTH), F32),
                        pltpu.VMEM((SUBLANES, LRU_WIDTH), F32)],
        compiler_params=_cparams("arbitrary"),
        name="rglru_shortconv_mixer",
    )(x2d, vec(nw), sc, sh, w_in.astype(BF16), conv_w, vec(conv_b), ga, vec(ga_b), gx, vec(gx_b),
      vec(lam), sconv_w)


def _route_kernel(cat_ref, x_ref, wo_ref, g1_ref, nw_ref, sc_ref, sh_ref, rw_ref, rb_ref,
                  x3_ref, hn_ref, meta_ref, wt_ref, cnt_ref, carry_ref):
    i = pl.program_id(0)

    @pl.when(i == 0)
    def _():
        carry_ref[...] = jnp.zeros_like(carry_ref)

    x3 = x_ref[...] + g1_ref[0] * jnp.dot(cat_ref[...], wo_ref[...], preferred_element_type=F32)
    x3_ref[...] = x3
    hn = _norm_mod(x3, nw_ref[...], sc_ref[0], sh_ref[0])
    hn_ref[...] = hn
    tm = hn.shape[0]
    lane = lax.broadcasted_iota(jnp.int32, (tm, LANES), 1)
    logits = _dot_x(hn, rw_ref[...], 2, 2) + rb_ref[...]
    lg = jnp.where(lane < N_EXPERTS, logits, NEG_BIG)
    m1 = jnp.max(lg, axis=1, keepdims=True)
    i1 = jnp.min(jnp.where(lg == m1, lane, LANES), axis=1, keepdims=True)
    lg2 = jnp.where(lane == i1, NEG_BIG, lg)
    m2 = jnp.max(lg2, axis=1, keepdims=True)
    i2 = jnp.min(jnp.where(lg2 == m2, lane, LANES), axis=1, keepdims=True)
    e2 = jnp.exp(m2 - m1)
    w1 = 1.0 / (1.0 + e2)
    w2 = e2 / (1.0 + e2)

    hit1 = lane == i1
    hit2 = lane == i2
    sel = jnp.logical_or(hit1, hit2).astype(F32)
    r_i = lax.broadcasted_iota(jnp.int32, (tm, tm), 0)
    c_i = lax.broadcasted_iota(jnp.int32, (tm, tm), 1)
    tril = (r_i >= c_i).astype(BF16)
    incl = jnp.dot(tril, sel.astype(BF16), preferred_element_type=F32)
    carry = carry_ref[0:1, :]
    excl = incl - sel + carry
    r1 = jnp.sum(jnp.where(hit1, excl, 0.0), axis=1, keepdims=True)
    r2 = jnp.sum(jnp.where(hit2, excl, 0.0), axis=1, keepdims=True)
    total = carry + incl[tm - 1:tm, :]
    carry_ref[...] = jnp.broadcast_to(total, carry_ref.shape)
    cnt_ref[...] = jnp.broadcast_to(total, cnt_ref.shape).astype(jnp.int32)

    meta = jnp.where(lane == 0, i1, 0)
    meta = jnp.where(lane == 1, i2, meta)
    meta = jnp.where(lane == 2, r1.astype(jnp.int32), meta)
    meta = jnp.where(lane == 3, r2.astype(jnp.int32), meta)
    meta_ref[...] = meta
    wt_ref[...] = jnp.where(lane == 0, w1, jnp.where(lane == 1, w2, 0.0))


def _route(cat, x2d, w_out, g1, nw, sc, sh, router_w, router_b, seq_len):
    n, d = x2d.shape
    tm = TOKEN_TILE
    tps = seq_len // tm
    row = lambda width: pl.BlockSpec((tm, width), lambda i: (i, 0))
    per_b = pl.BlockSpec((1, 1, d), lambda i: (i // tps, 0, 0))
    rw = jnp.zeros((d, LANES), F32).at[:, :N_EXPERTS].set(router_w)
    rb = jnp.zeros((1, LANES), F32).at[0, :N_EXPERTS].set(router_b)
    return pl.pallas_call(
        _route_kernel,
        grid=(n // tm,),
        in_specs=[row(cat.shape[1]), row(d), _resident_spec(w_out.shape), per_b, _const_spec((1, d)),
                  per_b, per_b, _const_spec((d, LANES)), _const_spec((1, LANES))],
        out_specs=[row(d), row(d), row(LANES), row(LANES), _const_spec((SUBLANES, LANES))],
        out_shape=[jax.ShapeDtypeStruct((n, d), F32), jax.ShapeDtypeStruct((n, d), F32),
                   jax.ShapeDtypeStruct((n, LANES), jnp.int32), jax.ShapeDtypeStruct((n, LANES), F32),
                   jax.ShapeDtypeStruct((SUBLANES, LANES), jnp.int32)],
        scratch_shapes=[pltpu.VMEM((SUBLANES, LANES), F32)],
        compiler_params=_cparams("arbitrary"),
        name="out_proj1_router",
    )(cat, x2d, w_out.astype(BF16), g1, nw.reshape(1, d), sc, sh, rw, rb)


def _dispatch_kernel(d1_ref, d2_ref, hn_ref, xs_ref, sem1, sem2):
    i = pl.program_id(0)
    tr = ROW_DMA_TILE
    base = i * tr

    def start(t, c):
        src = hn_ref.at[pl.ds(t, 1)]
        pltpu.make_async_copy(src, xs_ref.at[pl.ds(d1_ref[base + t], 1)], sem1).start()
        pltpu.make_async_copy(src, xs_ref.at[pl.ds(d2_ref[base + t], 1)], sem2).start()
        return c

    lax.fori_loop(0, tr, start, 0, unroll=ROW_DMA_UNROLL)
    pltpu.make_async_copy(hn_ref, xs_ref.at[pl.ds(0, tr)], sem1).wait()
    pltpu.make_async_copy(hn_ref, xs_ref.at[pl.ds(0, tr)], sem2).wait()


def _dispatch(hn, d1, d2, rows_out):
    n, d = hn.shape
    return pl.pallas_call(
        _dispatch_kernel,
        grid_spec=pltpu.PrefetchScalarGridSpec(
            num_scalar_prefetch=2,
            grid=(n // ROW_DMA_TILE,),
            in_specs=[pl.BlockSpec((ROW_DMA_TILE, d), lambda i, a, b: (i, 0))],
            out_specs=pl.BlockSpec(memory_space=pl.ANY),
            scratch_shapes=[pltpu.SemaphoreType.DMA(()), pltpu.SemaphoreType.DMA(())]),
        out_shape=jax.ShapeDtypeStruct((rows_out, d), hn.dtype),
        compiler_params=_cparams("arbitrary"),
        name="moe_dispatch",
    )(d1, d2, hn)


def _moe_kernel(ti_ref, te_ref, tv_ref, lo_ref, hi_ref, x_ref, wg_ref, wu_ref, wd_ref, o_ref, xb_ref):
    w = pl.program_id(0)
    f = pl.program_id(1)
    tm = x_ref.shape[0]
    sub = MOE_SUB
    sub_shift = int(math.log2(sub))

    def swiglu_part(xb, wg, wu, wd):
        hg = jnp.dot(xb, wg, preferred_element_type=F32)
        hu = jnp.dot(xb, wu, preferred_element_type=F32)
        act = (_silu(hg) * hu).astype(BF16)
        return jnp.dot(act, wd, preferred_element_type=F32)

    @pl.when(tv_ref[w] == 1)
    def _():
        lo = lo_ref[w]
        hi = hi_ref[w]
        whole = jnp.logical_and(lo == 0, hi == tm)

        @pl.when(f == 0)
        def _():
            row = lax.broadcasted_iota(jnp.int32, (tm, 1), 0)
            mine = jnp.logical_and(row >= lo, row < hi)
            xb_ref[...] = jnp.where(mine, x_ref[...], 0.0).astype(BF16)

        @pl.when(whole)
        def _():
            part = swiglu_part(xb_ref[...], wg_ref[0].astype(BF16), wu_ref[0].astype(BF16),
                               wd_ref[0].astype(BF16))

            @pl.when(f == 0)
            def _():
                o_ref[...] = part

            @pl.when(f != 0)
            def _():
                o_ref[...] += part

        @pl.when(jnp.logical_not(whole))
        def _():
            def sub_block(s, carry):
                rows = pl.ds(pl.multiple_of(s * sub, sub), sub)
                part = swiglu_part(xb_ref[rows, :], wg_ref[0].astype(BF16), wu_ref[0].astype(BF16),
                                   wd_ref[0].astype(BF16))
                init = jnp.logical_and(f == 0, lo <= s * sub)

                @pl.when(init)
                def _():
                    o_ref[rows, :] = part

                @pl.when(jnp.logical_not(init))
                def _():
                    o_ref[rows, :] += part

                return carry

            lax.fori_loop(lo >> sub_shift, (hi + sub - 1) >> sub_shift, sub_block, 0)


def _moe_ffn(xs, items, wg, wu, wd):
    rows, d = xs.shape
    tm = MOE_TILE
    tf = MOE_FF_TILE
    nf = wg.shape[2] // tf
    n_items = items[0].shape[0]
    f_idx = lambda f, v: f * v + (nf - 1) * (1 - v)
    return pl.pallas_call(
        _moe_kernel,
        grid_spec=pltpu.PrefetchScalarGridSpec(
            num_scalar_prefetch=5,
            grid=(n_items, nf),
            in_specs=[pl.BlockSpec((tm, d), lambda w, f, ti, te, tv, lo, hi: (ti[w], 0)),
                      pl.BlockSpec((1, d, tf), lambda w, f, ti, te, tv, lo, hi: (te[w], 0, f_idx(f, tv[w]))),
                      pl.BlockSpec((1, d, tf), lambda w, f, ti, te, tv, lo, hi: (te[w], 0, f_idx(f, tv[w]))),
                      pl.BlockSpec((1, tf, d), lambda w, f, ti, te, tv, lo, hi: (te[w], f_idx(f, tv[w]), 0))],
            out_specs=pl.BlockSpec((tm, d), lambda w, f, ti, te, tv, lo, hi: (ti[w], 0)),
            scratch_shapes=[pltpu.VMEM((tm, d), BF16)]),
        out_shape=jax.ShapeDtypeStruct((rows, d), F32),
        compiler_params=_cparams("arbitrary", "arbitrary"),
        name="moe_expert_swiglu",
    )(*items, xs, wg, wu, wd)


def _combine_kernel(d1_ref, d2_ref, ys_ref, x_ref, wt_ref, g2_ref, fw_ref, o_ref, y1_ref, y2_ref, sem1, sem2):
    i = pl.program_id(0)
    tr = ROW_DMA_TILE
    base = i * tr

    def start(t, c):
        pltpu.make_async_copy(ys_ref.at[pl.ds(d1_ref[base + t], 1)], y1_ref.at[pl.ds(t, 1)], sem1).start()
        pltpu.make_async_copy(ys_ref.at[pl.ds(d2_ref[base + t], 1)], y2_ref.at[pl.ds(t, 1)], sem2).start()
        return c

    lax.fori_loop(0, tr, start, 0, unroll=ROW_DMA_UNROLL)
    pltpu.make_async_copy(ys_ref.at[pl.ds(0, tr)], y1_ref, sem1).wait()
    pltpu.make_async_copy(ys_ref.at[pl.ds(0, tr)], y2_ref, sem2).wait()
    wt = wt_ref[...]
    ffn = wt[:, 0:1] * y1_ref[...] + wt[:, 1:2] * y2_ref[...]
    x4 = x_ref[...] + g2_ref[0] * ffn
    o_ref[...] = (x4 * _rms_scale(x4)) * fw_ref[...]


def _combine(ys, d1, d2, x3, wt, g2, final_w, seq_len):
    n, d = x3.shape
    tr = ROW_DMA_TILE
    tps = seq_len // tr
    return pl.pallas_call(
        _combine_kernel,
        grid_spec=pltpu.PrefetchScalarGridSpec(
            num_scalar_prefetch=2,
            grid=(n // tr,),
            in_specs=[pl.BlockSpec(memory_space=pl.ANY),
                      pl.BlockSpec((tr, d), lambda i, a, b: (i, 0)),
                      pl.BlockSpec((tr, LANES), lambda i, a, b: (i, 0)),
                      pl.BlockSpec((1, 1, d), lambda i, a, b: (i // tps, 0, 0)),
                      pl.BlockSpec((1, d), lambda i, a, b: (0, 0))],
            out_specs=pl.BlockSpec((tr, d), lambda i, a, b: (i, 0)),
            scratch_shapes=[pltpu.VMEM((tr, d), F32), pltpu.VMEM((tr, d), F32),
                            pltpu.SemaphoreType.DMA(()), pltpu.SemaphoreType.DMA(())]),
        out_shape=jax.ShapeDtypeStruct((n, d), F32),
        compiler_params=_cparams("arbitrary"),
        name="moe_combine_final_norm",
    )(d1, d2, ys, x3, wt, g2, final_w.reshape(1, d))


def _moe_tables(meta, counts):
    n = meta.shape[0]
    tm = MOE_TILE
    n_items = (2 * n) // tm + N_EXPERTS - 1
    cnt = counts[0, :N_EXPERTS]
    end = jnp.cumsum(cnt)
    off = end - cnt
    d1 = jnp.take(off, meta[:, 0]) + meta[:, 2]
    d2 = jnp.take(off, meta[:, 1]) + meta[:, 3]
    first_tile = off // tm
    n_e = jnp.where(cnt > 0, (end - 1) // tm - first_tile + 1, 0)
    item_end = jnp.cumsum(n_e)
    item_start = item_end - n_e
    total = item_end[-1]
    w = jnp.arange(n_items, dtype=jnp.int32)
    w_eff = jnp.minimum(w, total - 1)
    te = jnp.minimum(jnp.sum((item_end[None, :] <= w_eff[:, None]).astype(jnp.int32), axis=1),
                     N_EXPERTS - 1)
    ti = jnp.take(first_tile, te) + (w_eff - jnp.take(item_start, te))
    tv = (w < total).astype(jnp.int32)
    lo = jnp.clip(jnp.take(off, te) - ti * tm, 0, tm) * tv
    hi = jnp.clip(jnp.take(end, te) - ti * tm, 0, tm) * tv
    i32 = lambda t: t.astype(jnp.int32)
    return i32(d1), i32(d2), (i32(ti), i32(te), tv, i32(lo), i32(hi))


def kernel(x, c, rel_bias, ada_w, ada_b, norm_mix_w, norm_ffn_w, final_norm_w, ab_w_in, attn_sinks,
           dn_conv_w, dn_a_log, dn_dt_bias, dn_norm_w, ab_w_out, ffn_w_gate, ffn_w_up, ffn_w_down,
           cd_w_in, lru_conv_w, lru_conv_b, lru_gate_a_w, lru_gate_a_b, lru_gate_x_w, lru_gate_x_b,
           lru_lambda, sconv_w, cd_w_out, moe_router_w, moe_router_b, moe_w_gate, moe_w_up, moe_w_down):
    bsz, seq_len, d = x.shape
    n = bsz * seq_len
    x2d = x.reshape(n, d)
    mods = _ada_mods(c, ada_w, ada_b)

    sh1, sc1, g1, sh2, sc2, g2 = (mods[0, k] for k in range(6))
    qa, kd, vd, qn, kn, vb, gs, beta, g = _in_proj0(
        x2d, norm_mix_w[0], sc1, sh1, ab_w_in[0], dn_conv_w[0], dn_a_log[0], dn_dt_bias[0], seq_len)
    attn = _attention(qa, kd, vd, _bias_table(rel_bias), attn_sinks[0], seq_len)
    dn = _deltanet(qn, kn, vb, gs, beta, g, dn_norm_w[0], seq_len)
    x2 = _mid0(attn, dn, x2d, ab_w_out[0], g1, norm_ffn_w[0], sc2, sh2, g2,
               ffn_w_gate[0], ffn_w_up[0], ffn_w_down[0], seq_len)

    sh1, sc1, g1, sh2, sc2, g2 = (mods[1, k] for k in range(6))
    cat = _mix1(x2, norm_mix_w[1], sc1, sh1, cd_w_in[0], lru_conv_w[0], lru_conv_b[0],
                lru_gate_a_w[0], lru_gate_a_b[0], lru_gate_x_w[0], lru_gate_x_b[0],
                lru_lambda[0], sconv_w[0], seq_len)
    x3, hn4, meta, wt, counts = _route(cat, x2, cd_w_out[0], g1, norm_ffn_w[1], sc2, sh2,
                                       moe_router_w[0], moe_router_b[0], seq_len)
    d1, d2, items = _moe_tables(meta, counts)
    xs = _dispatch(hn4, d1, d2, 2 * n)
    ys = _moe_ffn(xs, items, moe_w_gate[0], moe_w_up[0], moe_w_down[0])
    out = _combine(ys, d1, d2, x3, wt, g2, final_norm_w, seq_len)
    return out.reshape(bsz, seq_len, d)
```

```python
import functools
import math

import numpy as np
import jax
import jax.numpy as jnp
from jax import lax
from jax.experimental import pallas as pl
from jax.experimental.pallas import tpu as pltpu

D_MODEL = 1024
EPS = 1e-6
HEAD_DIM = 64
A_Q_HEADS = 8
A_KV_HEADS = 2
WINDOW = 128
N_BUCKETS = 32
MAX_DISTANCE = 128
B_HEADS = 8
B_CONV = 4
CHUNK = 64
A_Q_W = A_Q_HEADS * HEAD_DIM
A_KV_W = A_KV_HEADS * HEAD_DIM
B_W = B_HEADS * HEAD_DIM
B_QKV_W = 3 * B_W
LRU_WIDTH = D_MODEL
LRU_BLOCKS = 8
LRU_C = 8.0
SC_WIDTH = D_MODEL // 2
D_FF = 2816
N_EXPERTS = 8
D_FF_EXPERT = 3584

LANES = 128
SUBLANES = 8
VMEM_LIMIT_BYTES = 56 * 1024 * 1024
TOKEN_TILE = 512
MOE_TILE = 1024
MOE_SUB = 256
MOE_FF_TILE = 512
ROW_DMA_TILE = 256
ROW_DMA_UNROLL = 8
NEG_BIG = -1e30

F32 = jnp.float32
BF16 = jnp.bfloat16


def _cparams(*sem):
    return pltpu.CompilerParams(dimension_semantics=tuple(sem), vmem_limit_bytes=VMEM_LIMIT_BYTES)


def _const_spec(shape):
    nd = len(shape)
    return pl.BlockSpec(shape, lambda *_: (0,) * nd)


def _bdot(a, b):
    return jnp.dot(a.astype(BF16), b.astype(BF16), preferred_element_type=F32)


def _bdot_nt(a, b):
    return lax.dot_general(a.astype(BF16), b.astype(BF16), (((1,), (1,)), ((), ())),
                           preferred_element_type=F32)


def _bdot_tn(a, b):
    return lax.dot_general(a.astype(BF16), b.astype(BF16), (((0,), (0,)), ((), ())),
                           preferred_element_type=F32)


def _split(x, n):
    parts = []
    r = x
    for i in range(n):
        p = r.astype(BF16)
        parts.append(p)
        if i + 1 < n:
            r = r - p.astype(F32)
    return parts


def _dot_x(a, b, na=2, nb=2):
    asp = _split(a, na) if na > 1 else [a.astype(BF16)]
    bsp = _split(b, nb) if nb > 1 else [b.astype(BF16)]
    acc = None
    for i, ai in enumerate(asp):
        for j, bj in enumerate(bsp):
            if i + j >= max(na, nb):
                continue
            t = jnp.dot(ai, bj, preferred_element_type=F32)
            acc = t if acc is None else acc + t
    return acc


def _silu(x):
    return x * (1.0 / (1.0 + jnp.exp(-x)))


def _sigmoid(x):
    return 1.0 / (1.0 + jnp.exp(-x))


def _log1p(z):
    u = 1.0 + z
    tiny = u == 1.0
    return jnp.where(tiny, z, jnp.log(u) * (z / jnp.where(tiny, 1.0, u - 1.0)))


def _softplus(x):
    return jnp.maximum(x, 0.0) + _log1p(jnp.exp(-jnp.abs(x)))


def _neg_expm1(y):
    return -jnp.tanh(0.5 * y) * (jnp.exp(y) + 1.0)


def _rms_scale(x):
    width = x.shape[1]
    mean_w = jnp.full((width, LANES), 1.0 / width, BF16)
    ms = _dot_x(x * x, mean_w, 2, 1)
    r = lax.rsqrt(ms + EPS)
    return jnp.concatenate([r] * (width // LANES), axis=1)


def _norm_mod(x, w, sc, sh, on_mxu=False):
    if on_mxu:
        scale = _rms_scale(x)
    else:
        scale = lax.rsqrt(jnp.mean(x * x, axis=-1, keepdims=True) + EPS)
    return (x * scale) * w * (1.0 + sc) + sh


def _shift_rows(x, k, prev_tail):
    n, width = x.shape
    x3 = x.reshape(n // SUBLANES, SUBLANES, width)
    rot = pltpu.roll(x3, k, 1)
    rot_prev = jnp.concatenate([pltpu.roll(prev_tail, k, 0)[None], rot[:-1]], axis=0)
    sub = lax.broadcasted_iota(jnp.int32, x3.shape, 1)
    return jnp.where(sub >= k, rot, rot_prev).reshape(n, width)


def _ada_kernel(c_ref, w_ref, b_ref, o_ref):
    c = c_ref[...]
    cond = _silu(c)
    o_ref[0] = _dot_x(cond, w_ref[0], 3, 3) + b_ref[0]


def _ada_mods(c, ada_w, ada_b):
    depth, d, six_d = ada_w.shape
    bsz = c.shape[0]
    rows = max(SUBLANES, bsz)
    c_pad = jnp.zeros((rows, d), F32).at[:bsz].set(c)
    tn = 1536
    out = pl.pallas_call(
        _ada_kernel,
        grid=(depth, six_d // tn),
        in_specs=[pl.BlockSpec((rows, d), lambda l, j: (0, 0)),
                  pl.BlockSpec((1, d, tn), lambda l, j: (l, 0, j)),
                  pl.BlockSpec((1, 1, tn), lambda l, j: (l, 0, j))],
        out_specs=pl.BlockSpec((1, rows, tn), lambda l, j: (l, 0, j)),
        out_shape=jax.ShapeDtypeStruct((depth, rows, six_d), F32),
        compiler_params=_cparams("parallel", "parallel"),
        name="ada_mods",
    )(c_pad, ada_w, ada_b.reshape(depth, 1, six_d))
    return out[:, :bsz].reshape(depth, bsz, 6, 1, d).transpose(0, 2, 1, 3, 4)


def _t5_bucket(dist):
    max_exact = N_BUCKETS // 2
    d = np.maximum(dist, 0)
    large = max_exact + (np.log(np.maximum(d, 1) / max_exact) / math.log(MAX_DISTANCE / max_exact)
                         * (N_BUCKETS - max_exact)).astype(np.int32)
    large = np.minimum(large, N_BUCKETS - 1)
    return np.where(d < max_exact, d, large).astype(np.int32)


def _band_buckets():
    qi = np.arange(WINDOW)[:, None]
    s = np.arange(2 * WINDOW)[None, :]
    dist = qi + WINDOW - s
    in_window = (dist >= 0) & (dist < WINDOW)
    return np.where(in_window, _t5_bucket(dist), -1).astype(np.int32)


def _bias_kernel(rb_ref, bucket_ref, o_ref):
    h = pl.program_id(0)
    bucket = bucket_ref[...]
    acc = jnp.zeros(bucket.shape, F32)
    for b in range(N_BUCKETS):
        acc = jnp.where(bucket == b, rb_ref[b, h], acc)
    o_ref[0] = jnp.where(bucket < 0, NEG_BIG, acc)


def _bias_table(rel_bias):
    bucket = jnp.asarray(_band_buckets())
    out = pl.pallas_call(
        _bias_kernel,
        grid=(A_Q_HEADS,),
        in_specs=[pl.BlockSpec(memory_space=pltpu.SMEM),
                  _const_spec((WINDOW, 2 * WINDOW))],
        out_specs=pl.BlockSpec((1, WINDOW, 2 * WINDOW), lambda h: (h, 0, 0)),
        out_shape=jax.ShapeDtypeStruct((A_Q_HEADS, WINDOW, 2 * WINDOW), F32),
        compiler_params=_cparams("parallel"),
        name="attn_bias_table",
    )(rel_bias, bucket)
    return out.reshape(A_Q_HEADS // 2, 2 * WINDOW, 2 * WINDOW)


_C_QA = 0
_C_KA = _C_QA + A_Q_W
_C_VA = _C_KA + A_KV_W
_C_QKV = _C_VA + A_KV_W
_C_GATE = _C_QKV + B_QKV_W
_C_SMALL = _C_GATE + B_W
_AB_COLS = _C_SMALL + LANES


def _ab_in_weight(w_in):
    return jnp.pad(w_in, ((0, 0), (0, _AB_COLS - w_in.shape[1]))).astype(BF16)


def _dup_heads(t, low):
    swapped = pltpu.roll(t, HEAD_DIM, 1)
    return jnp.concatenate([jnp.where(low, t, swapped), jnp.where(low, swapped, t)], axis=1)


def _chunk_tril(tm):
    r = np.arange(tm)
    return ((r[:, None] >= r[None, :]) & (r[:, None] // CHUNK == r[None, :] // CHUNK)).astype(np.float32)


def _head_selector():
    e = np.zeros((B_W, LANES), np.float32)
    for h in range(B_HEADS):
        e[h * HEAD_DIM:(h + 1) * HEAD_DIM, h] = 1.0
    return e


def _in0_kernel(x_ref, nw_ref, sc_ref, sh_ref, w_ref, cw_ref, sel_ref, selt_ref, tril_ref, alog_ref, dtb_ref,
                qa_ref, kd_ref, vd_ref, qn_ref, kn_ref, vb_ref, gs_ref, bexp_ref, gcexp_ref,
                tail_ref, *, tiles_per_seq):
    i = pl.program_id(0)

    @pl.when(i % tiles_per_seq == 0)
    def _():
        tail_ref[...] = jnp.zeros_like(tail_ref)

    hn = _norm_mod(x_ref[...], nw_ref[...], sc_ref[0], sh_ref[0])
    proj = jnp.dot(hn.astype(BF16), w_ref[...], preferred_element_type=F32)
    low = lax.broadcasted_iota(jnp.int32, (proj.shape[0], LANES), 1) < HEAD_DIM
    qa_ref[...] = proj[:, _C_QA:_C_KA].astype(BF16)
    kd_ref[...] = _dup_heads(proj[:, _C_KA:_C_VA], low).astype(BF16)
    vd_ref[...] = _dup_heads(proj[:, _C_VA:_C_QKV], low).astype(BF16)

    xq = proj[:, _C_QKV:_C_GATE]
    tail = tail_ref[...]
    cw = cw_ref[...]
    y = xq * cw[B_CONV - 1:B_CONV]
    for k in range(1, B_CONV):
        y = y + _shift_rows(xq, k, tail) * cw[B_CONV - 1 - k:B_CONV - k]
    tail_ref[...] = xq[xq.shape[0] - SUBLANES:]
    y = _silu(y)
    q, k_, v = y[:, :B_W], y[:, B_W:2 * B_W], y[:, 2 * B_W:]

    def l2n(t):
        ssq = _dot_x(t * t, sel_ref[...], 2, 1)
        r = lax.rsqrt(ssq + EPS)
        return t * _dot_x(r, selt_ref[...], 2, 1)

    qn_ref[...] = l2n(q) * (HEAD_DIM ** -0.5)
    kn_ref[...] = l2n(k_)
    vb_ref[...] = v
    gs_ref[...] = _silu(proj[:, _C_GATE:_C_SMALL])
    small = proj[:, _C_SMALL:]
    lane = lax.broadcasted_iota(jnp.int32, small.shape, 1)
    beta = jnp.where(lane < B_HEADS, _sigmoid(small), 0.0)
    dec = pltpu.roll(small, LANES - B_HEADS, 1)
    g = jnp.where(lane < B_HEADS, -jnp.exp(alog_ref[...]) * _softplus(dec + dtb_ref[...]), 0.0)
    bexp_ref[...] = _dot_x(beta, selt_ref[...], 2, 1)
    gc = _dot_x(tril_ref[...], g, 1, 3)
    gcexp_ref[...] = _dot_x(gc, selt_ref[...], 3, 1)


def _in_proj0(x2d, nw, sc, sh, w_in, conv_w, a_log, dt_bias, seq_len):
    n, d = x2d.shape
    tm = TOKEN_TILE
    tiles_per_seq = seq_len // tm
    w = _ab_in_weight(w_in)
    sel = jnp.asarray(_head_selector(), BF16)
    selt = jnp.asarray(_head_selector().T.copy(), BF16)
    tril = jnp.asarray(_chunk_tril(tm), BF16)
    pad8 = lambda v: jnp.zeros((1, LANES), F32).at[0, :B_HEADS].set(v)
    row = lambda width: pl.BlockSpec((tm, width), lambda i: (i, 0))
    per_b = pl.BlockSpec((1, 1, d), lambda i: (i // tiles_per_seq, 0, 0))
    outs = pl.pallas_call(
        functools.partial(_in0_kernel, tiles_per_seq=tiles_per_seq),
        grid=(n // tm,),
        in_specs=[row(d), _const_spec((1, d)), per_b, per_b,
                  _resident_spec((d, _AB_COLS)), _const_spec((B_CONV, B_QKV_W)),
                  _const_spec((B_W, LANES)), _const_spec((LANES, B_W)), _const_spec((tm, tm)),
                  _const_spec((1, LANES)), _const_spec((1, LANES))],
        out_specs=[row(A_Q_W), row(2 * A_KV_W), row(2 * A_KV_W)] + [row(B_W)] * 6,
        out_shape=[jax.ShapeDtypeStruct((n, A_Q_W), BF16),
                   jax.ShapeDtypeStruct((n, 2 * A_KV_W), BF16),
                   jax.ShapeDtypeStruct((n, 2 * A_KV_W), BF16)]
        + [jax.ShapeDtypeStruct((n, B_W), F32)] * 6,
        scratch_shapes=[pltpu.VMEM((SUBLANES, B_QKV_W), F32)],
        compiler_params=_cparams("arbitrary"),
        name="in_proj0",
    )(x2d, nw.reshape(1, d), sc, sh, w, conv_w, sel, selt, tril, pad8(a_log), pad8(dt_bias))
    return outs


def _attn_kernel(sink_ref, q_ref, kp_ref, kc_ref, vp_ref, vc_ref, bm_ref, o_ref, *, blocks_per_seq):
    i = pl.program_id(0)
    first = (i % blocks_per_seq) == 0
    w = WINDOW
    lane = lax.broadcasted_iota(jnp.int32, (w, LANES), 1)
    low = lane < HEAD_DIM
    col = lax.broadcasted_iota(jnp.int32, (2 * w, 2 * w), 1)
    row = lax.broadcasted_iota(jnp.int32, (2 * w, 1), 0)
    prev_dead = jnp.logical_and(first, col < w)
    q_all = q_ref[...]
    zero = jnp.zeros((), q_all.dtype)
    outs = []
    for j in range(A_Q_HEADS // 2):
        kh = (2 * j) // (A_Q_HEADS // A_KV_HEADS)
        qp = q_all[:, j * LANES:(j + 1) * LANES]
        qs = jnp.concatenate([jnp.where(low, qp, zero), jnp.where(low, zero, qp)], axis=0)
        kd = jnp.concatenate([kp_ref[:, kh * LANES:(kh + 1) * LANES],
                              kc_ref[:, kh * LANES:(kh + 1) * LANES]], axis=0)
        vd = jnp.concatenate([vp_ref[:, kh * LANES:(kh + 1) * LANES],
                              vc_ref[:, kh * LANES:(kh + 1) * LANES]], axis=0)
        s = lax.dot_general(qs, kd, (((1,), (1,)), ((), ())), preferred_element_type=F32)
        s = s * (HEAD_DIM ** -0.5) + bm_ref[j]
        s = jnp.where(prev_dead, NEG_BIG, s)
        sink = jnp.where(row < w, sink_ref[2 * j], sink_ref[2 * j + 1])
        m = jnp.maximum(jnp.max(s, axis=-1, keepdims=True), sink)
        p = jnp.exp(s - m)
        denom = jnp.sum(p, axis=-1, keepdims=True) + jnp.exp(sink - m)
        pv = jnp.dot(p.astype(BF16), vd, preferred_element_type=F32) / denom
        outs.append(jnp.where(low, pv[:w], pv[w:]))
    o_ref[...] = jnp.concatenate(outs, axis=1).astype(o_ref.dtype)


def _attention(qa, kd, vd, bias_tbl, sinks, seq_len):
    n = qa.shape[0]
    w = WINDOW
    nb = seq_len // w
    cur = lambda i: (i, 0)
    prev = lambda i: (jnp.where(i % nb == 0, i, i - 1), 0)
    return pl.pallas_call(
        functools.partial(_attn_kernel, blocks_per_seq=nb),
        grid=(n // w,),
        in_specs=[pl.BlockSpec(memory_space=pltpu.SMEM),
                  pl.BlockSpec((w, A_Q_W), cur),
                  pl.BlockSpec((w, 2 * A_KV_W), prev), pl.BlockSpec((w, 2 * A_KV_W), cur),
                  pl.BlockSpec((w, 2 * A_KV_W), prev), pl.BlockSpec((w, 2 * A_KV_W), cur),
                  _const_spec((A_Q_HEADS // 2, 2 * w, 2 * w))],
        out_specs=pl.BlockSpec((w, A_Q_W), cur),
        out_shape=jax.ShapeDtypeStruct((n, A_Q_W), BF16),
        compiler_params=_cparams("parallel"),
        name="swa_attention",
    )(sinks, qa, kd, kd, vd, vd, bias_tbl)


_DN_PAIRS = B_HEADS // 2
_DN_INV_BLOCK = 16
_DN_GROUP = 4


def _block_diag(x, low):
    zero = jnp.zeros((), x.dtype)
    return jnp.concatenate([jnp.where(low, x, zero), jnp.where(low, zero, x)], axis=0)


def _dn_intra(chunks, data_refs, work_refs, consts):
    qn_ref, kn_ref, vb_ref, bexp_ref, gcexp_ref = data_refs
    u_ref, w_ref, qk_ref, qd_ref, kd_ref, egl_ref = work_refs
    low, i_idx, j_idx, ones3 = consts
    c = CHUNK
    units = [(ci, p) for ci in chunks for p in range(_DN_PAIRS)]
    where = [(slice(ci * c, (ci + 1) * c), slice(p * LANES, (p + 1) * LANES)) for ci, p in units]
    causal = i_idx >= j_idx
    strict = i_idx > j_idx
    on_diag = i_idx == j_idx
    eye = on_diag.astype(F32)
    blk_shift = int(math.log2(_DN_INV_BLOCK))
    same_blk = (i_idx >> blk_shift) == (j_idx >> blk_shift)

    q = [qn_ref[rs, ls] for rs, ls in where]
    k = [kn_ref[rs, ls] for rs, ls in where]
    v = [vb_ref[rs, ls] for rs, ls in where]
    b = [bexp_ref[rs, ls] for rs, ls in where]
    gc = [gcexp_ref[rs, ls] for rs, ls in where]

    gr = [jnp.dot(ones3, jnp.concatenate(_split(jnp.where(on_diag, t, 0.0), 3), axis=0),
                  preferred_element_type=F32) for t in gc]
    ks = [_block_diag(t.astype(BF16), low) for t in k]
    qkk = [lax.dot_general(jnp.concatenate([qt, kt], axis=0).astype(BF16), kst,
                           (((1,), (1,)), ((), ())), preferred_element_type=F32)
           for qt, kt, kst in zip(q, k, ks)]
    decay = [jnp.exp(jnp.where(causal, gct - grt, NEG_BIG)) for gct, grt in zip(gc, gr)]
    lmat = [jnp.where(strict, bt * t[c:] * dt, 0.0) for bt, t, dt in zip(b, qkk, decay)]
    qk = [jnp.where(causal, t[:c] * dt, 0.0) for t, dt in zip(qkk, decay)]

    def mm(xs, ys):
        return [_bdot(x, _block_diag(y.astype(BF16), low)) for x, y in zip(xs, ys)]

    l_diag = [jnp.where(same_blk, t, 0.0) for t in lmat]
    l_off = [t - d for t, d in zip(lmat, l_diag)]
    pw = [-t for t in l_diag]
    d_inv = [eye + t for t in pw]
    for _ in range(blk_shift - 1):
        pw = mm(pw, pw)
        d_inv = mm(d_inv, [eye + t for t in pw])
    pw = [-t for t in mm(d_inv, l_off)]
    acc = [eye + t for t in pw]
    for _ in range(int(math.log2(c // _DN_INV_BLOCK)) - 1):
        pw = mm(pw, pw)
        acc = mm(acc, [eye + t for t in pw])
    tmat = mm(acc, d_inv)

    egc = [jnp.exp(t) for t in gc]
    rhs = [jnp.concatenate([_block_diag((vt * bt).astype(BF16), low),
                            _block_diag((kt * (bt * et)).astype(BF16), low)], axis=1)
           for vt, kt, bt, et in zip(v, k, b, egc)]
    uw = [_bdot(t, r) for t, r in zip(tmat, rhs)]
    for n, (ci, p) in enumerate(units):
        g_last = gc[n][c - 1:c, :]
        u_ref[ci, p] = uw[n][:, :LANES]
        w_ref[ci, p] = uw[n][:, LANES:]
        qk_ref[ci, p] = qk[n]
        qd_ref[ci, p] = q[n] * egc[n]
        kd_ref[ci, p] = k[n] * jnp.exp(g_last - gc[n])
        egl_ref[ci, p] = jnp.broadcast_to(jnp.exp(g_last), (SUBLANES, LANES))


def _dn_scan(ci, work_refs, s_ref, gs_ref, nw, o_ref, consts):
    u_ref, w_ref, qk_ref, qd_ref, kd_ref, egl_ref = work_refs
    low, mask_bd, head_mean2 = consts
    c = CHUNK
    rows = slice(ci * c, (ci + 1) * c)
    pairs = range(_DN_PAIRS)
    s_old = [s_ref[p] for p in pairs]
    wq = [_bdot(jnp.concatenate([w_ref[ci, p], qd_ref[ci, p]], axis=0), s_old[p]) for p in pairs]
    v_new = [u_ref[ci, p] - wq[p][:c] for p in pairs]
    o = [wq[p][c:] + _bdot(qk_ref[ci, p], _block_diag(v_new[p].astype(BF16), low)) for p in pairs]
    kv = [_bdot_tn(kd_ref[ci, p], v_new[p]) for p in pairs]
    for p in pairs:
        s_ref[p] = s_old[p] * egl_ref[ci, p][0:1, :] + jnp.where(mask_bd, kv[p], 0.0)
    ms = [jnp.dot(jnp.concatenate(_split(t * t, 2), axis=1), head_mean2, preferred_element_type=F32)
          for t in o]
    for p in pairs:
        ls = slice(p * LANES, (p + 1) * LANES)
        y = (o[p] * lax.rsqrt(ms[p] + EPS)) * nw * gs_ref[rows, ls]
        o_ref[rows, ls] = y.astype(o_ref.dtype)


def _dn_kernel(qn_ref, kn_ref, vb_ref, gs_ref, bexp_ref, gcexp_ref, nw_ref, o_ref,
               s_ref, u_ref, w_ref, qk_ref, qd_ref, kd_ref, egl_ref, *, groups_per_seq):
    i = pl.program_id(0)

    @pl.when(i % groups_per_seq == 0)
    def _():
        s_ref[...] = jnp.zeros_like(s_ref)

    c = CHUNK
    tm = o_ref.shape[0]
    n_chunks = tm // c
    lane = lax.broadcasted_iota(jnp.int32, (c, LANES), 1)
    low = lane < HEAD_DIM
    i_idx = lax.broadcasted_iota(jnp.int32, (c, LANES), 0)
    j_idx = lane & (c - 1)
    ones3 = jnp.ones((c, 3 * c), BF16)
    rb = lax.broadcasted_iota(jnp.int32, (LANES, LANES), 0)
    cb = lax.broadcasted_iota(jnp.int32, (LANES, LANES), 1)
    mask_bd = (rb < HEAD_DIM) == (cb < HEAD_DIM)
    head_mean = jnp.where(mask_bd, 1.0 / HEAD_DIM, 0.0).astype(BF16)
    head_mean2 = jnp.concatenate([head_mean, head_mean], axis=0)
    data_refs = (qn_ref, kn_ref, vb_ref, bexp_ref, gcexp_ref)
    work_refs = (u_ref, w_ref, qk_ref, qd_ref, kd_ref, egl_ref)
    intra_consts = (low, i_idx, j_idx, ones3)
    scan_consts = (low, mask_bd, head_mean2)
    nw = nw_ref[...]

    groups = [list(range(s, s + _DN_GROUP)) for s in range(0, n_chunks, _DN_GROUP)]
    _dn_intra(groups[0], data_refs, work_refs, intra_consts)
    for j, grp in enumerate(groups):
        if j + 1 < len(groups):
            _dn_intra(groups[j + 1], data_refs, work_refs, intra_consts)
        for ci in grp:
            _dn_scan(ci, work_refs, s_ref, gs_ref, nw, o_ref, scan_consts)


def _deltanet(qn, kn, vb, gs, bexp, gcexp, norm_w, seq_len):
    n = qn.shape[0]
    tm = TOKEN_TILE
    nw2 = jnp.concatenate([norm_w, norm_w]).reshape(1, LANES)
    row = lambda width: pl.BlockSpec((tm, width), lambda i: (i, 0))
    return pl.pallas_call(
        functools.partial(_dn_kernel, groups_per_seq=seq_len // tm),
        grid=(n // tm,),
        in_specs=[row(B_W)] * 6 + [_const_spec((1, LANES))],
        out_specs=row(B_W),
        out_shape=jax.ShapeDtypeStruct((n, B_W), BF16),
        scratch_shapes=[pltpu.VMEM((_DN_PAIRS, LANES, LANES), F32)]
        + [pltpu.VMEM((tm // CHUNK, _DN_PAIRS, CHUNK, LANES), F32)] * 5
        + [pltpu.VMEM((tm // CHUNK, _DN_PAIRS, SUBLANES, LANES), F32)],
        compiler_params=_cparams("arbitrary"),
        name="gated_deltanet",
    )(qn, kn, vb, gs, bexp, gcexp, nw2)


def _resident_spec(shape):
    nd = len(shape)
    return pl.BlockSpec(shape, lambda *_: (0,) * nd, pipeline_mode=pl.Buffered(1))


def _mid0_kernel(attn_ref, dn_ref, x_ref, wo_ref, g1_ref, nw_ref, sc_ref, sh_ref, g2_ref,
                 wg_ref, wu_ref, wd_ref, o_ref):
    mix = (jnp.dot(attn_ref[...], wo_ref[:A_Q_W], preferred_element_type=F32)
           + jnp.dot(dn_ref[...], wo_ref[A_Q_W:], preferred_element_type=F32))
    x1 = x_ref[...] + g1_ref[0] * mix
    hn = _norm_mod(x1, nw_ref[...], sc_ref[0], sh_ref[0]).astype(BF16)
    hg = jnp.dot(hn, wg_ref[...], preferred_element_type=F32)
    hu = jnp.dot(hn, wu_ref[...], preferred_element_type=F32)
    act = (_silu(hg) * hu).astype(BF16)
    o_ref[...] = x1 + g2_ref[0] * jnp.dot(act, wd_ref[...], preferred_element_type=F32)


def _mid0(attn, dn, x2d, w_out, g1, nw, sc, sh, g2, wg, wu, wd, seq_len):
    n, d = x2d.shape
    tm = TOKEN_TILE
    tps = seq_len // tm
    row = lambda width: pl.BlockSpec((tm, width), lambda i: (i, 0))
    per_b = pl.BlockSpec((1, 1, d), lambda i: (i // tps, 0, 0))
    return pl.pallas_call(
        _mid0_kernel,
        grid=(n // tm,),
        in_specs=[row(A_Q_W), row(B_W), row(d), _resident_spec(w_out.shape), per_b,
                  _const_spec((1, d)), per_b, per_b, per_b,
                  _resident_spec(wg.shape), _resident_spec(wu.shape), _resident_spec(wd.shape)],
        out_specs=row(d),
        out_shape=jax.ShapeDtypeStruct((n, d), F32),
        compiler_params=_cparams("parallel"),
        name="out_proj0_swiglu",
    )(attn, dn, x2d, w_out.astype(BF16), g1, nw.reshape(1, d), sc, sh, g2,
      wg.astype(BF16), wu.astype(BF16), wd.astype(BF16))


def _gelu_tanh(x):
    return 0.5 * x * (1.0 + jnp.tanh(math.sqrt(2.0 / math.pi) * (x + 0.044715 * (x * x * x))))


def _linear_scan(a, b, h0):
    n, width = a.shape
    groups = n // SUBLANES
    a = a.reshape(groups, SUBLANES, width)
    b = b.reshape(groups, SUBLANES, width)
    in_group = lax.broadcasted_iota(jnp.int32, a.shape, 1)
    s = 1
    while s < SUBLANES:
        a_sh = pltpu.roll(a, s, 1)
        b_sh = pltpu.roll(b, s, 1)
        valid = in_group >= s
        b = jnp.where(valid, a * b_sh + b, b)
        a = jnp.where(valid, a * a_sh, a)
        s *= 2
    carry = jnp.broadcast_to(h0, (SUBLANES, width))
    out = []
    for g in range(groups):
        hg = a[g] * carry + b[g]
        out.append(hg)
        carry = jnp.broadcast_to(hg[SUBLANES - 1:SUBLANES, :], hg.shape)
    return jnp.concatenate(out, axis=0)


def _mix1_tile(x, nw, sc, sh, w_ref, cw_ref, cb_ref, ga_ref, gab_ref, gx_ref, gxb_ref, lam_ref, sw_ref,
               tail_c_ref, tail_d_ref, h_ref):
    hn = _norm_mod(x, nw, sc, sh, on_mxu=True).astype(BF16)
    proj = jnp.dot(hn, w_ref[...], preferred_element_type=F32)
    w_l = LRU_WIDTH
    xc_in = proj[:, :w_l]
    yc = proj[:, w_l:2 * w_l]
    bd = proj[:, 2 * w_l:2 * w_l + SC_WIDTH]
    cd = proj[:, 2 * w_l + SC_WIDTH:2 * w_l + 2 * SC_WIDTH]
    hd = proj[:, 2 * w_l + 2 * SC_WIDTH:]
    tm = xc_in.shape[0]

    kc = cw_ref.shape[0]
    tail = tail_c_ref[...]
    cw = cw_ref[...]
    xc = xc_in * cw[kc - 1:kc] + cb_ref[...]
    for k in range(1, kc):
        xc = xc + _shift_rows(xc_in, k, tail) * cw[kc - 1 - k:kc - k]
    tail_c_ref[...] = xc_in[tm - SUBLANES:]

    xb = xc.astype(BF16)
    gw = ga_ref.shape[1]
    ra, ri = [], []
    for p in range(ga_ref.shape[0]):
        xin = xb[:, p * gw:(p + 1) * gw]
        ra.append(jnp.dot(xin, ga_ref[p], preferred_element_type=F32))
        ri.append(jnp.dot(xin, gx_ref[p], preferred_element_type=F32))
    r = _sigmoid(jnp.concatenate(ra, axis=1) + gab_ref[...])
    ig = _sigmoid(jnp.concatenate(ri, axis=1) + gxb_ref[...])
    log_a = (-LRU_C) * r * _softplus(-lam_ref[...])
    a = jnp.exp(log_a)
    b = jnp.sqrt(_neg_expm1(2.0 * log_a)) * (ig * xc)
    h = _linear_scan(a, b, h_ref[0:1, :])
    h_ref[...] = jnp.broadcast_to(h[tm - 1:tm, :], h_ref.shape)
    yc_out = h * _gelu_tanh(yc)

    ks = sw_ref.shape[0]
    ch = cd * hd
    tail_d = tail_d_ref[...]
    sw = sw_ref[...]
    conv = ch * sw[ks - 1:ks]
    for k in range(1, ks):
        conv = conv + _shift_rows(ch, k, tail_d) * sw[ks - 1 - k:ks - k]
    tail_d_ref[...] = ch[tm - SUBLANES:]
    return jnp.concatenate([yc_out, bd * conv], axis=1)


def _pair_block_diag(gw):
    nb, bw, _ = gw.shape
    g2 = gw.reshape(nb // 2, 2, bw, bw)
    z = jnp.zeros((nb // 2, bw, bw), gw.dtype)
    top = jnp.concatenate([g2[:, 0], z], axis=2)
    bot = jnp.concatenate([z, g2[:, 1]], axis=2)
    return jnp.concatenate([top, bot], axis=1).astype(BF16)


def _route_tile(cat, x, wo_ref, g1, nw, sc, sh, rw_ref, rb_ref, carry_ref):
    x3 = x + g1 * jnp.dot(cat, wo_ref[...], preferred_element_type=F32)
    hn = _norm_mod(x3, nw, sc, sh)
    tm = hn.shape[0]
    lane = lax.broadcasted_iota(jnp.int32, (tm, LANES), 1)
    logits = _dot_x(hn, rw_ref[...], 2, 2) + rb_ref[...]
    lg = jnp.where(lane < N_EXPERTS, logits, NEG_BIG)
    m1 = jnp.max(lg, axis=1, keepdims=True)
    i1 = jnp.min(jnp.where(lg == m1, lane, LANES), axis=1, keepdims=True)
    lg2 = jnp.where(lane == i1, NEG_BIG, lg)
    m2 = jnp.max(lg2, axis=1, keepdims=True)
    i2 = jnp.min(jnp.where(lg2 == m2, lane, LANES), axis=1, keepdims=True)
    e2 = jnp.exp(m2 - m1)
    w1 = 1.0 / (1.0 + e2)
    w2 = e2 / (1.0 + e2)

    hit1 = lane == i1
    hit2 = lane == i2
    sel = jnp.logical_or(hit1, hit2).astype(F32)
    r_i = lax.broadcasted_iota(jnp.int32, (tm, tm), 0)
    c_i = lax.broadcasted_iota(jnp.int32, (tm, tm), 1)
    tril = (r_i >= c_i).astype(BF16)
    incl = jnp.dot(tril, sel.astype(BF16), preferred_element_type=F32)
    carry = carry_ref[0:1, :]
    excl = incl - sel + carry
    r1 = jnp.sum(jnp.where(hit1, excl, 0.0), axis=1, keepdims=True)
    r2 = jnp.sum(jnp.where(hit2, excl, 0.0), axis=1, keepdims=True)
    total = carry + incl[tm - 1:tm, :]
    carry_ref[...] = jnp.broadcast_to(total, carry_ref.shape)

    meta = jnp.where(lane == 0, i1, 0)
    meta = jnp.where(lane == 1, i2, meta)
    meta = jnp.where(lane == 2, r1.astype(jnp.int32), meta)
    meta = jnp.where(lane == 3, r2.astype(jnp.int32), meta)
    wt = jnp.where(lane == 0, w1, jnp.where(lane == 1, w2, 0.0))
    return x3, hn, meta, wt, total


def _mix1_kernel(x_ref, nw_ref, sc_ref, sh_ref, w_ref, cw_ref, cb_ref, ga_ref, gab_ref, gx_ref, gxb_ref,
                 lam_ref, sw_ref, o_ref, tail_c_ref, tail_d_ref, h_ref, *, tiles_per_seq):
    i = pl.program_id(0)

    @pl.when(i % tiles_per_seq == 0)
    def _():
        tail_c_ref[...] = jnp.zeros_like(tail_c_ref)
        tail_d_ref[...] = jnp.zeros_like(tail_d_ref)
        h_ref[...] = jnp.zeros_like(h_ref)

    cat = _mix1_tile(x_ref[...], nw_ref[...], sc_ref[0], sh_ref[0], w_ref, cw_ref, cb_ref, ga_ref,
                     gab_ref, gx_ref, gxb_ref, lam_ref, sw_ref, tail_c_ref, tail_d_ref, h_ref)
    o_ref[...] = cat.astype(o_ref.dtype)


def _mix1(x2d, nw, sc, sh, w_in, conv_w, conv_b, ga_w, ga_b, gx_w, gx_b, lam, sconv_w, seq_len):
    n, d = x2d.shape
    tm = TOKEN_TILE
    tps = seq_len // tm
    cd_in = w_in.shape[1]
    cd_out = LRU_WIDTH + SC_WIDTH
    row = lambda width: pl.BlockSpec((tm, width), lambda i: (i, 0))
    per_b = pl.BlockSpec((1, 1, d), lambda i: (i // tps, 0, 0))
    ga = _pair_block_diag(ga_w)
    gx = _pair_block_diag(gx_w)
    vec = lambda v: v.reshape(1, -1)
    return pl.pallas_call(
        functools.partial(_mix1_kernel, tiles_per_seq=tps),
        grid=(n // tm,),
        in_specs=[row(d), _const_spec((1, d)), per_b, per_b, _resident_spec((d, cd_in)),
                  _const_spec(conv_w.shape), _const_spec((1, LRU_WIDTH)),
                  _const_spec(ga.shape), _const_spec((1, LRU_WIDTH)),
                  _const_spec(gx.shape), _const_spec((1, LRU_WIDTH)),
                  _const_spec((1, LRU_WIDTH)), _const_spec(sconv_w.shape)],
        out_specs=row(cd_out),
        out_shape=jax.ShapeDtypeStruct((n, cd_out), BF16),
        scratch_shapes=[pltpu.VMEM((SUBLANES, LRU_WIDTH), F32), pltpu.VMEM((SUBLANES, SC_WIDTH), F32),
                        pltpu.VMEM((SUBLANES, LRU_WIDTH), F32)],
        compiler_params=_cparams("arbitrary"),
        name="rglru_shortconv_mixer",
    )(x2d, vec(nw), sc, sh, w_in.astype(BF16), conv_w, vec(conv_b), ga, vec(ga_b), gx, vec(gx_b),
      vec(lam), sconv_w)


def _route_kernel(cat_ref, x_ref, wo_ref, g1_ref, nw_ref, sc_ref, sh_ref, rw_ref, rb_ref,
                  x3_ref, hn_ref, meta_ref, wt_ref, cnt_ref, carry_ref):
    @pl.when(pl.program_id(0) == 0)
    def _():
        carry_ref[...] = jnp.zeros_like(carry_ref)

    x3, hn, meta, wt, total = _route_tile(cat_ref[...], x_ref[...], wo_ref, g1_ref[0], nw_ref[...],
                                          sc_ref[0], sh_ref[0], rw_ref, rb_ref, carry_ref)
    x3_ref[...] = x3
    hn_ref[...] = hn
    meta_ref[...] = jnp.transpose(meta.astype(F32))[:SUBLANES].astype(jnp.int32)
    wt_ref[...] = wt
    cnt_ref[...] = jnp.broadcast_to(total, cnt_ref.shape).astype(jnp.int32)


def _route(cat, x2d, w_out, g1, nw, sc, sh, router_w, router_b, seq_len):
    n, d = x2d.shape
    tm = TOKEN_TILE
    tps = seq_len // tm
    row = lambda width: pl.BlockSpec((tm, width), lambda i: (i, 0))
    per_b = pl.BlockSpec((1, 1, d), lambda i: (i // tps, 0, 0))
    rw = jnp.zeros((d, LANES), F32).at[:, :N_EXPERTS].set(router_w)
    rb = jnp.zeros((1, LANES), F32).at[0, :N_EXPERTS].set(router_b)
    return pl.pallas_call(
        _route_kernel,
        grid=(n // tm,),
        in_specs=[row(cat.shape[1]), row(d), _resident_spec(w_out.shape), per_b, _const_spec((1, d)),
                  per_b, per_b, _const_spec((d, LANES)), _const_spec((1, LANES))],
        out_specs=[row(d), row(d), pl.BlockSpec((SUBLANES, tm), lambda i: (0, i)), row(LANES),
                   _const_spec((SUBLANES, LANES))],
        out_shape=[jax.ShapeDtypeStruct((n, d), F32), jax.ShapeDtypeStruct((n, d), F32),
                   jax.ShapeDtypeStruct((SUBLANES, n), jnp.int32), jax.ShapeDtypeStruct((n, LANES), F32),
                   jax.ShapeDtypeStruct((SUBLANES, LANES), jnp.int32)],
        scratch_shapes=[pltpu.VMEM((SUBLANES, LANES), F32)],
        compiler_params=_cparams("arbitrary"),
        name="out_proj1_router",
    )(cat, x2d, w_out.astype(BF16), g1, nw.reshape(1, d), sc, sh, rw, rb)


def _dispatch_kernel(d1_ref, d2_ref, hn_ref, xs_ref, sem1, sem2):
    i = pl.program_id(0)
    tr = ROW_DMA_TILE
    base = i * tr

    def start(t, c):
        src = hn_ref.at[pl.ds(t, 1)]
        pltpu.make_async_copy(src, xs_ref.at[pl.ds(d1_ref[base + t], 1)], sem1).start()
        pltpu.make_async_copy(src, xs_ref.at[pl.ds(d2_ref[base + t], 1)], sem2).start()
        return c

    lax.fori_loop(0, tr, start, 0, unroll=ROW_DMA_UNROLL)
    pltpu.make_async_copy(hn_ref, xs_ref.at[pl.ds(0, tr)], sem1).wait()
    pltpu.make_async_copy(hn_ref, xs_ref.at[pl.ds(0, tr)], sem2).wait()


def _dispatch(hn, d1, d2, rows_out):
    n, d = hn.shape
    return pl.pallas_call(
        _dispatch_kernel,
        grid_spec=pltpu.PrefetchScalarGridSpec(
            num_scalar_prefetch=2,
            grid=(n // ROW_DMA_TILE,),
            in_specs=[pl.BlockSpec((ROW_DMA_TILE, d), lambda i, a, b: (i, 0))],
            out_specs=pl.BlockSpec(memory_space=pl.ANY),
            scratch_shapes=[pltpu.SemaphoreType.DMA(()), pltpu.SemaphoreType.DMA(())]),
        out_shape=jax.ShapeDtypeStruct((rows_out, d), hn.dtype),
        compiler_params=_cparams("arbitrary"),
        name="moe_dispatch",
    )(d1, d2, hn)


def _moe_kernel(ti_ref, te_ref, tv_ref, lo_ref, hi_ref, x_ref, wg_ref, wu_ref, wd_ref, o_ref, xb_ref):
    w = pl.program_id(0)
    f = pl.program_id(1)
    tm = x_ref.shape[0]
    sub = MOE_SUB
    sub_shift = int(math.log2(sub))

    def swiglu_part(xb, wg, wu, wd):
        hg = jnp.dot(xb, wg, preferred_element_type=F32)
        hu = jnp.dot(xb, wu, preferred_element_type=F32)
        act = (_silu(hg) * hu).astype(BF16)
        return jnp.dot(act, wd, preferred_element_type=F32)

    @pl.when(tv_ref[w] == 1)
    def _():
        lo = lo_ref[w]
        hi = hi_ref[w]
        whole = jnp.logical_and(lo == 0, hi == tm)

        @pl.when(f == 0)
        def _():
            row = lax.broadcasted_iota(jnp.int32, (tm, 1), 0)
            mine = jnp.logical_and(row >= lo, row < hi)
            xb_ref[...] = jnp.where(mine, x_ref[...], 0.0).astype(BF16)

        @pl.when(whole)
        def _():
            part = swiglu_part(xb_ref[...], wg_ref[0].astype(BF16), wu_ref[0].astype(BF16),
                               wd_ref[0].astype(BF16))

            @pl.when(f == 0)
            def _():
                o_ref[...] = part

            @pl.when(f != 0)
            def _():
                o_ref[...] += part

        @pl.when(jnp.logical_not(whole))
        def _():
            def sub_block(s, carry):
                rows = pl.ds(pl.multiple_of(s * sub, sub), sub)
                part = swiglu_part(xb_ref[rows, :], wg_ref[0].astype(BF16), wu_ref[0].astype(BF16),
                                   wd_ref[0].astype(BF16))
                init = jnp.logical_and(f == 0, lo <= s * sub)

                @pl.when(init)
                def _():
                    o_ref[rows, :] = part

                @pl.when(jnp.logical_not(init))
                def _():
                    o_ref[rows, :] += part

                return carry

            lax.fori_loop(lo >> sub_shift, (hi + sub - 1) >> sub_shift, sub_block, 0)


def _moe_ffn(xs, items, wg, wu, wd):
    rows, d = xs.shape
    tm = MOE_TILE
    tf = MOE_FF_TILE
    nf = wg.shape[2] // tf
    n_items = items[0].shape[0]
    f_idx = lambda f, v: f * v + (nf - 1) * (1 - v)
    return pl.pallas_call(
        _moe_kernel,
        grid_spec=pltpu.PrefetchScalarGridSpec(
            num_scalar_prefetch=5,
            grid=(n_items, nf),
            in_specs=[pl.BlockSpec((tm, d), lambda w, f, ti, te, tv, lo, hi: (ti[w], 0)),
                      pl.BlockSpec((1, d, tf), lambda w, f, ti, te, tv, lo, hi: (te[w], 0, f_idx(f, tv[w]))),
                      pl.BlockSpec((1, d, tf), lambda w, f, ti, te, tv, lo, hi: (te[w], 0, f_idx(f, tv[w]))),
                      pl.BlockSpec((1, tf, d), lambda w, f, ti, te, tv, lo, hi: (te[w], f_idx(f, tv[w]), 0))],
            out_specs=pl.BlockSpec((tm, d), lambda w, f, ti, te, tv, lo, hi: (ti[w], 0)),
            scratch_shapes=[pltpu.VMEM((tm, d), BF16)]),
        out_shape=jax.ShapeDtypeStruct((rows, d), F32),
        compiler_params=_cparams("arbitrary", "arbitrary"),
        name="moe_expert_swiglu",
    )(*items, xs, wg, wu, wd)


def _combine_kernel(d1_ref, d2_ref, ys_ref, x_ref, wt_ref, g2_ref, fw_ref, o_ref, y1_ref, y2_ref, sem1, sem2):
    i = pl.program_id(0)
    tr = ROW_DMA_TILE
    base = i * tr

    def start(t, c):
        pltpu.make_async_copy(ys_ref.at[pl.ds(d1_ref[base + t], 1)], y1_ref.at[pl.ds(t, 1)], sem1).start()
        pltpu.make_async_copy(ys_ref.at[pl.ds(d2_ref[base + t], 1)], y2_ref.at[pl.ds(t, 1)], sem2).start()
        return c

    lax.fori_loop(0, tr, start, 0, unroll=ROW_DMA_UNROLL)
    pltpu.make_async_copy(ys_ref.at[pl.ds(0, tr)], y1_ref, sem1).wait()
    pltpu.make_async_copy(ys_ref.at[pl.ds(0, tr)], y2_ref, sem2).wait()
    wt = wt_ref[...]
    ffn = wt[:, 0:1] * y1_ref[...] + wt[:, 1:2] * y2_ref[...]
    x4 = x_ref[...] + g2_ref[0] * ffn
    o_ref[...] = (x4 * _rms_scale(x4)) * fw_ref[...]


def _combine(ys, d1, d2, x3, wt, g2, final_w, seq_len):
    n, d = x3.shape
    tr = ROW_DMA_TILE
    tps = seq_len // tr
    return pl.pallas_call(
        _combine_kernel,
        grid_spec=pltpu.PrefetchScalarGridSpec(
            num_scalar_prefetch=2,
            grid=(n // tr,),
            in_specs=[pl.BlockSpec(memory_space=pl.ANY),
                      pl.BlockSpec((tr, d), lambda i, a, b: (i, 0)),
                      pl.BlockSpec((tr, LANES), lambda i, a, b: (i, 0)),
                      pl.BlockSpec((1, 1, d), lambda i, a, b: (i // tps, 0, 0)),
                      pl.BlockSpec((1, d), lambda i, a, b: (0, 0))],
            out_specs=pl.BlockSpec((tr, d), lambda i, a, b: (i, 0)),
            scratch_shapes=[pltpu.VMEM((tr, d), F32), pltpu.VMEM((tr, d), F32),
                            pltpu.SemaphoreType.DMA(()), pltpu.SemaphoreType.DMA(())]),
        out_shape=jax.ShapeDtypeStruct((n, d), F32),
        compiler_params=_cparams("arbitrary"),
        name="moe_combine_final_norm",
    )(d1, d2, ys, x3, wt, g2, final_w.reshape(1, d))


def _moe_tables(meta, counts):
    n = meta.shape[1]
    tm = MOE_TILE
    n_items = (2 * n) // tm + N_EXPERTS - 1
    cnt = counts[0, :N_EXPERTS]
    end = jnp.cumsum(cnt)
    off = end - cnt
    d1 = jnp.take(off, meta[0]) + meta[2]
    d2 = jnp.take(off, meta[1]) + meta[3]
    first_tile = off // tm
    n_e = jnp.where(cnt > 0, (end - 1) // tm - first_tile + 1, 0)
    item_end = jnp.cumsum(n_e)
    item_start = item_end - n_e
    total = item_end[-1]
    w = jnp.arange(n_items, dtype=jnp.int32)
    w_eff = jnp.minimum(w, total - 1)
    te = jnp.minimum(jnp.sum((item_end[None, :] <= w_eff[:, None]).astype(jnp.int32), axis=1),
                     N_EXPERTS - 1)
    ti = jnp.take(first_tile, te) + (w_eff - jnp.take(item_start, te))
    tv = (w < total).astype(jnp.int32)
    lo = jnp.clip(jnp.take(off, te) - ti * tm, 0, tm) * tv
    hi = jnp.clip(jnp.take(end, te) - ti * tm, 0, tm) * tv
    i32 = lambda t: t.astype(jnp.int32)
    return i32(d1), i32(d2), (i32(ti), i32(te), tv, i32(lo), i32(hi))


def kernel(x, c, rel_bias, ada_w, ada_b, norm_mix_w, norm_ffn_w, final_norm_w, ab_w_in, attn_sinks,
           dn_conv_w, dn_a_log, dn_dt_bias, dn_norm_w, ab_w_out, ffn_w_gate, ffn_w_up, ffn_w_down,
           cd_w_in, lru_conv_w, lru_conv_b, lru_gate_a_w, lru_gate_a_b, lru_gate_x_w, lru_gate_x_b,
           lru_lambda, sconv_w, cd_w_out, moe_router_w, moe_router_b, moe_w_gate, moe_w_up, moe_w_down):
    bsz, seq_len, d = x.shape
    n = bsz * seq_len
    x2d = x.reshape(n, d)
    mods = _ada_mods(c, ada_w, ada_b)

    sh1, sc1, g1, sh2, sc2, g2 = (mods[0, k] for k in range(6))
    qa, kd, vd, qn, kn, vb, gs, bexp, gcexp = _in_proj0(
        x2d, norm_mix_w[0], sc1, sh1, ab_w_in[0], dn_conv_w[0], dn_a_log[0], dn_dt_bias[0], seq_len)
    attn = _attention(qa, kd, vd, _bias_table(rel_bias), attn_sinks[0], seq_len)
    dn = _deltanet(qn, kn, vb, gs, bexp, gcexp, dn_norm_w[0], seq_len)
    x2 = _mid0(attn, dn, x2d, ab_w_out[0], g1, norm_ffn_w[0], sc2, sh2, g2,
               ffn_w_gate[0], ffn_w_up[0], ffn_w_down[0], seq_len)

    sh1, sc1, g1, sh2, sc2, g2 = (mods[1, k] for k in range(6))
    cat = _mix1(x2, norm_mix_w[1], sc1, sh1, cd_w_in[0], lru_conv_w[0], lru_conv_b[0],
                lru_gate_a_w[0], lru_gate_a_b[0], lru_gate_x_w[0], lru_gate_x_b[0],
                lru_lambda[0], sconv_w[0], seq_len)
    x3, hn4, meta, wt, counts = _route(cat, x2, cd_w_out[0], g1, norm_ffn_w[1], sc2, sh2,
                                       moe_router_w[0], moe_router_b[0], seq_len)
    d1, d2, items = _moe_tables(meta, counts)
    xs = _dispatch(hn4, d1, d2, 2 * n)
    ys = _moe_ffn(xs, items, moe_w_gate[0], moe_w_up[0], moe_w_down[0])
    out = _combine(ys, d1, d2, x3, wt, g2, final_norm_w, seq_len)
    return out.reshape(bsz, seq_len, d)
```

```python
import functools
import math

import numpy as np
import jax
import jax.numpy as jnp
from jax import lax
from jax.experimental import pallas as pl
from jax.experimental.pallas import tpu as pltpu

D_MODEL = 1024
EPS = 1e-6
HEAD_DIM = 64
A_Q_HEADS = 8
A_KV_HEADS = 2
WINDOW = 128
N_BUCKETS = 32
MAX_DISTANCE = 128
B_HEADS = 8
B_CONV = 4
CHUNK = 64
A_Q_W = A_Q_HEADS * HEAD_DIM
A_KV_W = A_KV_HEADS * HEAD_DIM
B_W = B_HEADS * HEAD_DIM
B_QKV_W = 3 * B_W
LRU_WIDTH = D_MODEL
LRU_BLOCKS = 8
LRU_C = 8.0
SC_WIDTH = D_MODEL // 2
D_FF = 2816
N_EXPERTS = 8
D_FF_EXPERT = 3584

LANES = 128
SUBLANES = 8
VMEM_LIMIT_BYTES = 56 * 1024 * 1024
TOKEN_TILE = 512
MOE_TILE = 1024
MOE_SUB = 256
MOE_FF_TILE = 512
NEG_BIG = -1e30

F32 = jnp.float32
BF16 = jnp.bfloat16


def _cparams(*sem):
    return pltpu.CompilerParams(dimension_semantics=tuple(sem), vmem_limit_bytes=VMEM_LIMIT_BYTES)


def _const_spec(shape):
    nd = len(shape)
    return pl.BlockSpec(shape, lambda *_: (0,) * nd)


def _bdot(a, b):
    return jnp.dot(a.astype(BF16), b.astype(BF16), preferred_element_type=F32)


def _bdot_nt(a, b):
    return lax.dot_general(a.astype(BF16), b.astype(BF16), (((1,), (1,)), ((), ())),
                           preferred_element_type=F32)


def _bdot_tn(a, b):
    return lax.dot_general(a.astype(BF16), b.astype(BF16), (((0,), (0,)), ((), ())),
                           preferred_element_type=F32)


def _split(x, n):
    parts = []
    r = x
    for i in range(n):
        p = r.astype(BF16)
        parts.append(p)
        if i + 1 < n:
            r = r - p.astype(F32)
    return parts


def _dot_x(a, b, na=2, nb=2):
    asp = _split(a, na) if na > 1 else [a.astype(BF16)]
    bsp = _split(b, nb) if nb > 1 else [b.astype(BF16)]
    acc = None
    for i, ai in enumerate(asp):
        for j, bj in enumerate(bsp):
            if i + j >= max(na, nb):
                continue
            t = jnp.dot(ai, bj, preferred_element_type=F32)
            acc = t if acc is None else acc + t
    return acc


def _silu(x):
    return x * (1.0 / (1.0 + jnp.exp(-x)))


def _sigmoid(x):
    return 1.0 / (1.0 + jnp.exp(-x))


def _log1p(z):
    u = 1.0 + z
    tiny = u == 1.0
    return jnp.where(tiny, z, jnp.log(u) * (z / jnp.where(tiny, 1.0, u - 1.0)))


def _softplus(x):
    return jnp.maximum(x, 0.0) + _log1p(jnp.exp(-jnp.abs(x)))


def _neg_expm1(y):
    return -jnp.tanh(0.5 * y) * (jnp.exp(y) + 1.0)


def _rms_scale(x):
    width = x.shape[1]
    mean_w = jnp.full((width, LANES), 1.0 / width, BF16)
    ms = _dot_x(x * x, mean_w, 2, 1)
    r = lax.rsqrt(ms + EPS)
    return jnp.concatenate([r] * (width // LANES), axis=1)


def _norm_mod(x, w, sc, sh, on_mxu=False):
    if on_mxu:
        scale = _rms_scale(x)
    else:
        scale = lax.rsqrt(jnp.mean(x * x, axis=-1, keepdims=True) + EPS)
    return (x * scale) * w * (1.0 + sc) + sh


def _shift_rows(x, k, prev_tail):
    n, width = x.shape
    x3 = x.reshape(n // SUBLANES, SUBLANES, width)
    rot = pltpu.roll(x3, k, 1)
    rot_prev = jnp.concatenate([pltpu.roll(prev_tail, k, 0)[None], rot[:-1]], axis=0)
    sub = lax.broadcasted_iota(jnp.int32, x3.shape, 1)
    return jnp.where(sub >= k, rot, rot_prev).reshape(n, width)


def _ada_kernel(c_ref, w_ref, b_ref, o_ref):
    c = c_ref[...]
    cond = _silu(c)
    o_ref[0] = _dot_x(cond, w_ref[0], 3, 3) + b_ref[0]


def _ada_mods(c, ada_w, ada_b):
    depth, d, six_d = ada_w.shape
    bsz = c.shape[0]
    rows = max(SUBLANES, bsz)
    c_pad = jnp.zeros((rows, d), F32).at[:bsz].set(c)
    tn = 1536
    out = pl.pallas_call(
        _ada_kernel,
        grid=(depth, six_d // tn),
        in_specs=[pl.BlockSpec((rows, d), lambda l, j: (0, 0)),
                  pl.BlockSpec((1, d, tn), lambda l, j: (l, 0, j)),
                  pl.BlockSpec((1, 1, tn), lambda l, j: (l, 0, j))],
        out_specs=pl.BlockSpec((1, rows, tn), lambda l, j: (l, 0, j)),
        out_shape=jax.ShapeDtypeStruct((depth, rows, six_d), F32),
        compiler_params=_cparams("parallel", "parallel"),
        name="ada_mods",
    )(c_pad, ada_w, ada_b.reshape(depth, 1, six_d))
    return out[:, :bsz].reshape(depth, bsz, 6, 1, d).transpose(0, 2, 1, 3, 4)


def _t5_bucket(dist):
    max_exact = N_BUCKETS // 2
    d = np.maximum(dist, 0)
    large = max_exact + (np.log(np.maximum(d, 1) / max_exact) / math.log(MAX_DISTANCE / max_exact)
                         * (N_BUCKETS - max_exact)).astype(np.int32)
    large = np.minimum(large, N_BUCKETS - 1)
    return np.where(d < max_exact, d, large).astype(np.int32)


def _band_buckets():
    qi = np.arange(WINDOW)[:, None]
    s = np.arange(2 * WINDOW)[None, :]
    dist = qi + WINDOW - s
    in_window = (dist >= 0) & (dist < WINDOW)
    return np.where(in_window, _t5_bucket(dist), -1).astype(np.int32)


def _bias_kernel(rb_ref, bucket_ref, o_ref):
    h = pl.program_id(0)
    bucket = bucket_ref[...]
    acc = jnp.zeros(bucket.shape, F32)
    for b in range(N_BUCKETS):
        acc = jnp.where(bucket == b, rb_ref[b, h], acc)
    o_ref[0] = jnp.where(bucket < 0, NEG_BIG, acc)


def _bias_table(rel_bias):
    bucket = jnp.asarray(_band_buckets())
    out = pl.pallas_call(
        _bias_kernel,
        grid=(A_Q_HEADS,),
        in_specs=[pl.BlockSpec(memory_space=pltpu.SMEM),
                  _const_spec((WINDOW, 2 * WINDOW))],
        out_specs=pl.BlockSpec((1, WINDOW, 2 * WINDOW), lambda h: (h, 0, 0)),
        out_shape=jax.ShapeDtypeStruct((A_Q_HEADS, WINDOW, 2 * WINDOW), F32),
        compiler_params=_cparams("parallel"),
        name="attn_bias_table",
    )(rel_bias, bucket)
    return out.reshape(A_Q_HEADS // 2, 2 * WINDOW, 2 * WINDOW)


_C_QA = 0
_C_KA = _C_QA + A_Q_W
_C_VA = _C_KA + A_KV_W
_C_QKV = _C_VA + A_KV_W
_C_GATE = _C_QKV + B_QKV_W
_C_SMALL = _C_GATE + B_W
_AB_COLS = _C_SMALL + LANES


def _ab_in_weight(w_in):
    return jnp.pad(w_in, ((0, 0), (0, _AB_COLS - w_in.shape[1]))).astype(BF16)


def _dup_heads(t, low):
    swapped = pltpu.roll(t, HEAD_DIM, 1)
    return jnp.concatenate([jnp.where(low, t, swapped), jnp.where(low, swapped, t)], axis=1)


def _chunk_tril(tm):
    r = np.arange(tm)
    return ((r[:, None] >= r[None, :]) & (r[:, None] // CHUNK == r[None, :] // CHUNK)).astype(np.float32)


def _head_selector():
    e = np.zeros((B_W, LANES), np.float32)
    for h in range(B_HEADS):
        e[h * HEAD_DIM:(h + 1) * HEAD_DIM, h] = 1.0
    return e


def _in0_kernel(x_ref, nw_ref, sc_ref, sh_ref, w_ref, cw_ref, sel_ref, selt_ref, tril_ref, alog_ref, dtb_ref,
                qa_ref, kd_ref, vd_ref, qn_ref, kn_ref, vb_ref, gs_ref, bexp_ref, gcexp_ref,
                tail_ref, *, tiles_per_seq):
    i = pl.program_id(0)

    @pl.when(i % tiles_per_seq == 0)
    def _():
        tail_ref[...] = jnp.zeros_like(tail_ref)

    hn = _norm_mod(x_ref[...], nw_ref[...], sc_ref[0], sh_ref[0])
    proj = jnp.dot(hn.astype(BF16), w_ref[...], preferred_element_type=F32)
    low = lax.broadcasted_iota(jnp.int32, (proj.shape[0], LANES), 1) < HEAD_DIM
    qa_ref[...] = proj[:, _C_QA:_C_KA].astype(BF16)
    kd_ref[...] = _dup_heads(proj[:, _C_KA:_C_VA], low).astype(BF16)
    vd_ref[...] = _dup_heads(proj[:, _C_VA:_C_QKV], low).astype(BF16)

    xq = proj[:, _C_QKV:_C_GATE]
    tail = tail_ref[...]
    cw = cw_ref[...]
    y = xq * cw[B_CONV - 1:B_CONV]
    for k in range(1, B_CONV):
        y = y + _shift_rows(xq, k, tail) * cw[B_CONV - 1 - k:B_CONV - k]
    tail_ref[...] = xq[xq.shape[0] - SUBLANES:]
    y = _silu(y)
    q, k_, v = y[:, :B_W], y[:, B_W:2 * B_W], y[:, 2 * B_W:]

    def l2n(t):
        ssq = _dot_x(t * t, sel_ref[...], 2, 1)
        r = lax.rsqrt(ssq + EPS)
        return t * _dot_x(r, selt_ref[...], 2, 1)

    qn_ref[...] = l2n(q) * (HEAD_DIM ** -0.5)
    kn_ref[...] = l2n(k_)
    vb_ref[...] = v
    gs_ref[...] = _silu(proj[:, _C_GATE:_C_SMALL])
    small = proj[:, _C_SMALL:]
    lane = lax.broadcasted_iota(jnp.int32, small.shape, 1)
    beta = jnp.where(lane < B_HEADS, _sigmoid(small), 0.0)
    dec = pltpu.roll(small, LANES - B_HEADS, 1)
    g = jnp.where(lane < B_HEADS, -jnp.exp(alog_ref[...]) * _softplus(dec + dtb_ref[...]), 0.0)
    bexp_ref[...] = _dot_x(beta, selt_ref[...], 2, 1)
    gc = _dot_x(tril_ref[...], g, 1, 3)
    gcexp_ref[...] = _dot_x(gc, selt_ref[...], 3, 1)


def _in_proj0(x2d, nw, sc, sh, w_in, conv_w, a_log, dt_bias, seq_len):
    n, d = x2d.shape
    tm = TOKEN_TILE
    tiles_per_seq = seq_len // tm
    w = _ab_in_weight(w_in)
    sel = jnp.asarray(_head_selector(), BF16)
    selt = jnp.asarray(_head_selector().T.copy(), BF16)
    tril = jnp.asarray(_chunk_tril(tm), BF16)
    pad8 = lambda v: jnp.zeros((1, LANES), F32).at[0, :B_HEADS].set(v)
    row = lambda width: pl.BlockSpec((tm, width), lambda i: (i, 0))
    per_b = pl.BlockSpec((1, 1, d), lambda i: (i // tiles_per_seq, 0, 0))
    outs = pl.pallas_call(
        functools.partial(_in0_kernel, tiles_per_seq=tiles_per_seq),
        grid=(n // tm,),
        in_specs=[row(d), _const_spec((1, d)), per_b, per_b,
                  _resident_spec((d, _AB_COLS)), _const_spec((B_CONV, B_QKV_W)),
                  _const_spec((B_W, LANES)), _const_spec((LANES, B_W)), _const_spec((tm, tm)),
                  _const_spec((1, LANES)), _const_spec((1, LANES))],
        out_specs=[row(A_Q_W), row(2 * A_KV_W), row(2 * A_KV_W)] + [row(B_W)] * 6,
        out_shape=[jax.ShapeDtypeStruct((n, A_Q_W), BF16),
                   jax.ShapeDtypeStruct((n, 2 * A_KV_W), BF16),
                   jax.ShapeDtypeStruct((n, 2 * A_KV_W), BF16)]
        + [jax.ShapeDtypeStruct((n, B_W), F32)] * 6,
        scratch_shapes=[pltpu.VMEM((SUBLANES, B_QKV_W), F32)],
        compiler_params=_cparams("arbitrary"),
        name="in_proj0",
    )(x2d, nw.reshape(1, d), sc, sh, w, conv_w, sel, selt, tril, pad8(a_log), pad8(dt_bias))
    return outs


def _attn_kernel(sink_ref, q_ref, kp_ref, kc_ref, vp_ref, vc_ref, bm_ref, o_ref, *, blocks_per_seq):
    i = pl.program_id(0)
    first = (i % blocks_per_seq) == 0
    w = WINDOW
    lane = lax.broadcasted_iota(jnp.int32, (w, LANES), 1)
    low = lane < HEAD_DIM
    col = lax.broadcasted_iota(jnp.int32, (2 * w, 2 * w), 1)
    row = lax.broadcasted_iota(jnp.int32, (2 * w, 1), 0)
    prev_dead = jnp.logical_and(first, col < w)
    q_all = q_ref[...]
    zero = jnp.zeros((), q_all.dtype)
    outs = []
    for j in range(A_Q_HEADS // 2):
        kh = (2 * j) // (A_Q_HEADS // A_KV_HEADS)
        qp = q_all[:, j * LANES:(j + 1) * LANES]
        qs = jnp.concatenate([jnp.where(low, qp, zero), jnp.where(low, zero, qp)], axis=0)
        kd = jnp.concatenate([kp_ref[:, kh * LANES:(kh + 1) * LANES],
                              kc_ref[:, kh * LANES:(kh + 1) * LANES]], axis=0)
        vd = jnp.concatenate([vp_ref[:, kh * LANES:(kh + 1) * LANES],
                              vc_ref[:, kh * LANES:(kh + 1) * LANES]], axis=0)
        s = lax.dot_general(qs, kd, (((1,), (1,)), ((), ())), preferred_element_type=F32)
        s = s * (HEAD_DIM ** -0.5) + bm_ref[j]
        s = jnp.where(prev_dead, NEG_BIG, s)
        sink = jnp.where(row < w, sink_ref[2 * j], sink_ref[2 * j + 1])
        m = jnp.maximum(jnp.max(s, axis=-1, keepdims=True), sink)
        p = jnp.exp(s - m)
        denom = jnp.sum(p, axis=-1, keepdims=True) + jnp.exp(sink - m)
        pv = jnp.dot(p.astype(BF16), vd, preferred_element_type=F32) / denom
        outs.append(jnp.where(low, pv[:w], pv[w:]))
    o_ref[...] = jnp.concatenate(outs, axis=1).astype(o_ref.dtype)


def _attention(qa, kd, vd, bias_tbl, sinks, seq_len):
    n = qa.shape[0]
    w = WINDOW
    nb = seq_len // w
    cur = lambda i: (i, 0)
    prev = lambda i: (jnp.where(i % nb == 0, i, i - 1), 0)
    return pl.pallas_call(
        functools.partial(_attn_kernel, blocks_per_seq=nb),
        grid=(n // w,),
        in_specs=[pl.BlockSpec(memory_space=pltpu.SMEM),
                  pl.BlockSpec((w, A_Q_W), cur),
                  pl.BlockSpec((w, 2 * A_KV_W), prev), pl.BlockSpec((w, 2 * A_KV_W), cur),
                  pl.BlockSpec((w, 2 * A_KV_W), prev), pl.BlockSpec((w, 2 * A_KV_W), cur),
                  _const_spec((A_Q_HEADS // 2, 2 * w, 2 * w))],
        out_specs=pl.BlockSpec((w, A_Q_W), cur),
        out_shape=jax.ShapeDtypeStruct((n, A_Q_W), BF16),
        compiler_params=_cparams("parallel"),
        name="swa_attention",
    )(sinks, qa, kd, kd, vd, vd, bias_tbl)


_DN_PAIRS = B_HEADS // 2
_DN_INV_BLOCK = 16
_DN_GROUP = 4


def _block_diag(x, low):
    zero = jnp.zeros((), x.dtype)
    return jnp.concatenate([jnp.where(low, x, zero), jnp.where(low, zero, x)], axis=0)


def _dn_intra(chunks, data_refs, work_refs, consts):
    qn_ref, kn_ref, vb_ref, bexp_ref, gcexp_ref = data_refs
    u_ref, w_ref, qk_ref, qd_ref, kd_ref, egl_ref = work_refs
    low, i_idx, j_idx, ones3 = consts
    c = CHUNK
    units = [(ci, p) for ci in chunks for p in range(_DN_PAIRS)]
    where = [(slice(ci * c, (ci + 1) * c), slice(p * LANES, (p + 1) * LANES)) for ci, p in units]
    causal = i_idx >= j_idx
    strict = i_idx > j_idx
    on_diag = i_idx == j_idx
    eye = on_diag.astype(F32)
    blk_shift = int(math.log2(_DN_INV_BLOCK))
    same_blk = (i_idx >> blk_shift) == (j_idx >> blk_shift)

    q = [qn_ref[rs, ls] for rs, ls in where]
    k = [kn_ref[rs, ls] for rs, ls in where]
    v = [vb_ref[rs, ls] for rs, ls in where]
    b = [bexp_ref[rs, ls] for rs, ls in where]
    gc = [gcexp_ref[rs, ls] for rs, ls in where]

    gr = [jnp.dot(ones3, jnp.concatenate(_split(jnp.where(on_diag, t, 0.0), 3), axis=0),
                  preferred_element_type=F32) for t in gc]
    ks = [_block_diag(t.astype(BF16), low) for t in k]
    qkk = [lax.dot_general(jnp.concatenate([qt, kt], axis=0).astype(BF16), kst,
                           (((1,), (1,)), ((), ())), preferred_element_type=F32)
           for qt, kt, kst in zip(q, k, ks)]
    decay = [jnp.exp(jnp.where(causal, gct - grt, NEG_BIG)) for gct, grt in zip(gc, gr)]
    lmat = [jnp.where(strict, bt * t[c:] * dt, 0.0) for bt, t, dt in zip(b, qkk, decay)]
    qk = [jnp.where(causal, t[:c] * dt, 0.0) for t, dt in zip(qkk, decay)]

    def mm(xs, ys):
        return [_bdot(x, _block_diag(y.astype(BF16), low)) for x, y in zip(xs, ys)]

    l_diag = [jnp.where(same_blk, t, 0.0) for t in lmat]
    l_off = [t - d for t, d in zip(lmat, l_diag)]
    pw = [-t for t in l_diag]
    d_inv = [eye + t for t in pw]
    for _ in range(blk_shift - 1):
        pw = mm(pw, pw)
        d_inv = mm(d_inv, [eye + t for t in pw])
    pw = [-t for t in mm(d_inv, l_off)]
    acc = [eye + t for t in pw]
    for _ in range(int(math.log2(c // _DN_INV_BLOCK)) - 1):
        pw = mm(pw, pw)
        acc = mm(acc, [eye + t for t in pw])
    tmat = mm(acc, d_inv)

    egc = [jnp.exp(t) for t in gc]
    rhs = [jnp.concatenate([_block_diag((vt * bt).astype(BF16), low),
                            _block_diag((kt * (bt * et)).astype(BF16), low)], axis=1)
           for vt, kt, bt, et in zip(v, k, b, egc)]
    uw = [_bdot(t, r) for t, r in zip(tmat, rhs)]
    for n, (ci, p) in enumerate(units):
        g_last = gc[n][c - 1:c, :]
        u_ref[ci, p] = uw[n][:, :LANES]
        w_ref[ci, p] = uw[n][:, LANES:]
        qk_ref[ci, p] = qk[n]
        qd_ref[ci, p] = q[n] * egc[n]
        kd_ref[ci, p] = k[n] * jnp.exp(g_last - gc[n])
        egl_ref[ci, p] = jnp.broadcast_to(jnp.exp(g_last), (SUBLANES, LANES))


def _dn_scan(ci, work_refs, s_ref, gs_ref, nw, o_ref, consts):
    u_ref, w_ref, qk_ref, qd_ref, kd_ref, egl_ref = work_refs
    low, mask_bd, head_mean2 = consts
    c = CHUNK
    rows = slice(ci * c, (ci + 1) * c)
    pairs = range(_DN_PAIRS)
    s_old = [s_ref[p] for p in pairs]
    wq = [_bdot(jnp.concatenate([w_ref[ci, p], qd_ref[ci, p]], axis=0), s_old[p]) for p in pairs]
    v_new = [u_ref[ci, p] - wq[p][:c] for p in pairs]
    o = [wq[p][c:] + _bdot(qk_ref[ci, p], _block_diag(v_new[p].astype(BF16), low)) for p in pairs]
    kv = [_bdot_tn(kd_ref[ci, p], v_new[p]) for p in pairs]
    for p in pairs:
        s_ref[p] = s_old[p] * egl_ref[ci, p][0:1, :] + jnp.where(mask_bd, kv[p], 0.0)
    ms = [jnp.dot(jnp.concatenate(_split(t * t, 2), axis=1), head_mean2, preferred_element_type=F32)
          for t in o]
    for p in pairs:
        ls = slice(p * LANES, (p + 1) * LANES)
        y = (o[p] * lax.rsqrt(ms[p] + EPS)) * nw * gs_ref[rows, ls]
        o_ref[rows, ls] = y.astype(o_ref.dtype)


def _dn_kernel(qn_ref, kn_ref, vb_ref, gs_ref, bexp_ref, gcexp_ref, nw_ref, o_ref,
               s_ref, u_ref, w_ref, qk_ref, qd_ref, kd_ref, egl_ref, *, groups_per_seq):
    i = pl.program_id(0)

    @pl.when(i % groups_per_seq == 0)
    def _():
        s_ref[...] = jnp.zeros_like(s_ref)

    c = CHUNK
    tm = o_ref.shape[0]
    n_chunks = tm // c
    lane = lax.broadcasted_iota(jnp.int32, (c, LANES), 1)
    low = lane < HEAD_DIM
    i_idx = lax.broadcasted_iota(jnp.int32, (c, LANES), 0)
    j_idx = lane & (c - 1)
    ones3 = jnp.ones((c, 3 * c), BF16)
    rb = lax.broadcasted_iota(jnp.int32, (LANES, LANES), 0)
    cb = lax.broadcasted_iota(jnp.int32, (LANES, LANES), 1)
    mask_bd = (rb < HEAD_DIM) == (cb < HEAD_DIM)
    head_mean = jnp.where(mask_bd, 1.0 / HEAD_DIM, 0.0).astype(BF16)
    head_mean2 = jnp.concatenate([head_mean, head_mean], axis=0)
    data_refs = (qn_ref, kn_ref, vb_ref, bexp_ref, gcexp_ref)
    work_refs = (u_ref, w_ref, qk_ref, qd_ref, kd_ref, egl_ref)
    intra_consts = (low, i_idx, j_idx, ones3)
    scan_consts = (low, mask_bd, head_mean2)
    nw = nw_ref[...]

    groups = [list(range(s, s + _DN_GROUP)) for s in range(0, n_chunks, _DN_GROUP)]
    _dn_intra(groups[0], data_refs, work_refs, intra_consts)
    for j, grp in enumerate(groups):
        if j + 1 < len(groups):
            _dn_intra(groups[j + 1], data_refs, work_refs, intra_consts)
        for ci in grp:
            _dn_scan(ci, work_refs, s_ref, gs_ref, nw, o_ref, scan_consts)


def _deltanet(qn, kn, vb, gs, bexp, gcexp, norm_w, seq_len):
    n = qn.shape[0]
    tm = TOKEN_TILE
    nw2 = jnp.concatenate([norm_w, norm_w]).reshape(1, LANES)
    row = lambda width: pl.BlockSpec((tm, width), lambda i: (i, 0))
    return pl.pallas_call(
        functools.partial(_dn_kernel, groups_per_seq=seq_len // tm),
        grid=(n // tm,),
        in_specs=[row(B_W)] * 6 + [_const_spec((1, LANES))],
        out_specs=row(B_W),
        out_shape=jax.ShapeDtypeStruct((n, B_W), BF16),
        scratch_shapes=[pltpu.VMEM((_DN_PAIRS, LANES, LANES), F32)]
        + [pltpu.VMEM((tm // CHUNK, _DN_PAIRS, CHUNK, LANES), F32)] * 5
        + [pltpu.VMEM((tm // CHUNK, _DN_PAIRS, SUBLANES, LANES), F32)],
        compiler_params=_cparams("arbitrary"),
        name="gated_deltanet",
    )(qn, kn, vb, gs, bexp, gcexp, nw2)


def _resident_spec(shape):
    nd = len(shape)
    return pl.BlockSpec(shape, lambda *_: (0,) * nd, pipeline_mode=pl.Buffered(1))


def _mid0_kernel(attn_ref, dn_ref, x_ref, wo_ref, g1_ref, nw_ref, sc_ref, sh_ref, g2_ref,
                 wg_ref, wu_ref, wd_ref, o_ref):
    mix = (jnp.dot(attn_ref[...], wo_ref[:A_Q_W], preferred_element_type=F32)
           + jnp.dot(dn_ref[...], wo_ref[A_Q_W:], preferred_element_type=F32))
    x1 = x_ref[...] + g1_ref[0] * mix
    hn = _norm_mod(x1, nw_ref[...], sc_ref[0], sh_ref[0]).astype(BF16)
    hg = jnp.dot(hn, wg_ref[...], preferred_element_type=F32)
    hu = jnp.dot(hn, wu_ref[...], preferred_element_type=F32)
    act = (_silu(hg) * hu).astype(BF16)
    o_ref[...] = x1 + g2_ref[0] * jnp.dot(act, wd_ref[...], preferred_element_type=F32)


def _mid0(attn, dn, x2d, w_out, g1, nw, sc, sh, g2, wg, wu, wd, seq_len):
    n, d = x2d.shape
    tm = TOKEN_TILE
    tps = seq_len // tm
    row = lambda width: pl.BlockSpec((tm, width), lambda i: (i, 0))
    per_b = pl.BlockSpec((1, 1, d), lambda i: (i // tps, 0, 0))
    return pl.pallas_call(
        _mid0_kernel,
        grid=(n // tm,),
        in_specs=[row(A_Q_W), row(B_W), row(d), _resident_spec(w_out.shape), per_b,
                  _const_spec((1, d)), per_b, per_b, per_b,
                  _resident_spec(wg.shape), _resident_spec(wu.shape), _resident_spec(wd.shape)],
        out_specs=row(d),
        out_shape=jax.ShapeDtypeStruct((n, d), F32),
        compiler_params=_cparams("parallel"),
        name="out_proj0_swiglu",
    )(attn, dn, x2d, w_out.astype(BF16), g1, nw.reshape(1, d), sc, sh, g2,
      wg.astype(BF16), wu.astype(BF16), wd.astype(BF16))


def _gelu_tanh(x):
    return 0.5 * x * (1.0 + jnp.tanh(math.sqrt(2.0 / math.pi) * (x + 0.044715 * (x * x * x))))


def _linear_scan(a, b, h0):
    n, width = a.shape
    groups = n // SUBLANES
    a = a.reshape(groups, SUBLANES, width)
    b = b.reshape(groups, SUBLANES, width)
    in_group = lax.broadcasted_iota(jnp.int32, a.shape, 1)
    s = 1
    while s < SUBLANES:
        a_sh = pltpu.roll(a, s, 1)
        b_sh = pltpu.roll(b, s, 1)
        valid = in_group >= s
        b = jnp.where(valid, a * b_sh + b, b)
        a = jnp.where(valid, a * a_sh, a)
        s *= 2
    carry = jnp.broadcast_to(h0, (SUBLANES, width))
    out = []
    for g in range(groups):
        hg = a[g] * carry + b[g]
        out.append(hg)
        carry = jnp.broadcast_to(hg[SUBLANES - 1:SUBLANES, :], hg.shape)
    return jnp.concatenate(out, axis=0)


def _mix1_tile(x, nw, sc, sh, w_ref, cw_ref, cb_ref, ga_ref, gab_ref, gx_ref, gxb_ref, lam_ref, sw_ref,
               tail_c_ref, tail_d_ref, h_ref):
    hn = _norm_mod(x, nw, sc, sh, on_mxu=True).astype(BF16)
    proj = jnp.dot(hn, w_ref[...], preferred_element_type=F32)
    w_l = LRU_WIDTH
    xc_in = proj[:, :w_l]
    yc = proj[:, w_l:2 * w_l]
    bd = proj[:, 2 * w_l:2 * w_l + SC_WIDTH]
    cd = proj[:, 2 * w_l + SC_WIDTH:2 * w_l + 2 * SC_WIDTH]
    hd = proj[:, 2 * w_l + 2 * SC_WIDTH:]
    tm = xc_in.shape[0]

    kc = cw_ref.shape[0]
    tail = tail_c_ref[...]
    cw = cw_ref[...]
    xc = xc_in * cw[kc - 1:kc] + cb_ref[...]
    for k in range(1, kc):
        xc = xc + _shift_rows(xc_in, k, tail) * cw[kc - 1 - k:kc - k]
    tail_c_ref[...] = xc_in[tm - SUBLANES:]

    xb = xc.astype(BF16)
    gw = ga_ref.shape[1]
    ra, ri = [], []
    for p in range(ga_ref.shape[0]):
        xin = xb[:, p * gw:(p + 1) * gw]
        ra.append(jnp.dot(xin, ga_ref[p], preferred_element_type=F32))
        ri.append(jnp.dot(xin, gx_ref[p], preferred_element_type=F32))
    r = _sigmoid(jnp.concatenate(ra, axis=1) + gab_ref[...])
    ig = _sigmoid(jnp.concatenate(ri, axis=1) + gxb_ref[...])
    log_a = (-LRU_C) * r * _softplus(-lam_ref[...])
    a = jnp.exp(log_a)
    b = jnp.sqrt(_neg_expm1(2.0 * log_a)) * (ig * xc)
    h = _linear_scan(a, b, h_ref[0:1, :])
    h_ref[...] = jnp.broadcast_to(h[tm - 1:tm, :], h_ref.shape)
    yc_out = h * _gelu_tanh(yc)

    ks = sw_ref.shape[0]
    ch = cd * hd
    tail_d = tail_d_ref[...]
    sw = sw_ref[...]
    conv = ch * sw[ks - 1:ks]
    for k in range(1, ks):
        conv = conv + _shift_rows(ch, k, tail_d) * sw[ks - 1 - k:ks - k]
    tail_d_ref[...] = ch[tm - SUBLANES:]
    return jnp.concatenate([yc_out, bd * conv], axis=1)


def _pair_block_diag(gw):
    nb, bw, _ = gw.shape
    g2 = gw.reshape(nb // 2, 2, bw, bw)
    z = jnp.zeros((nb // 2, bw, bw), gw.dtype)
    top = jnp.concatenate([g2[:, 0], z], axis=2)
    bot = jnp.concatenate([z, g2[:, 1]], axis=2)
    return jnp.concatenate([top, bot], axis=1).astype(BF16)


def _route_tile(cat, x, wo_ref, g1, nw, sc, sh, rw_ref, rb_ref, carry_ref):
    x3 = x + g1 * jnp.dot(cat, wo_ref[...], preferred_element_type=F32)
    hn = _norm_mod(x3, nw, sc, sh)
    tm = hn.shape[0]
    lane = lax.broadcasted_iota(jnp.int32, (tm, LANES), 1)
    logits = _dot_x(hn, rw_ref[...], 2, 2) + rb_ref[...]
    lg = jnp.where(lane < N_EXPERTS, logits, NEG_BIG)
    m1 = jnp.max(lg, axis=1, keepdims=True)
    i1 = jnp.min(jnp.where(lg == m1, lane, LANES), axis=1, keepdims=True)
    lg2 = jnp.where(lane == i1, NEG_BIG, lg)
    m2 = jnp.max(lg2, axis=1, keepdims=True)
    i2 = jnp.min(jnp.where(lg2 == m2, lane, LANES), axis=1, keepdims=True)
    e2 = jnp.exp(m2 - m1)
    w1 = 1.0 / (1.0 + e2)
    w2 = e2 / (1.0 + e2)

    hit1 = lane == i1
    hit2 = lane == i2
    sel = jnp.logical_or(hit1, hit2).astype(F32)
    r_i = lax.broadcasted_iota(jnp.int32, (tm, tm), 0)
    c_i = lax.broadcasted_iota(jnp.int32, (tm, tm), 1)
    tril = (r_i >= c_i).astype(BF16)
    incl = jnp.dot(tril, sel.astype(BF16), preferred_element_type=F32)
    carry = carry_ref[0:1, :]
    excl = incl - sel + carry
    r1 = jnp.sum(jnp.where(hit1, excl, 0.0), axis=1, keepdims=True)
    r2 = jnp.sum(jnp.where(hit2, excl, 0.0), axis=1, keepdims=True)
    total = carry + incl[tm - 1:tm, :]
    carry_ref[...] = jnp.broadcast_to(total, carry_ref.shape)

    meta = jnp.where(lane == 0, i1, 0)
    meta = jnp.where(lane == 1, i2, meta)
    meta = jnp.where(lane == 2, r1.astype(jnp.int32), meta)
    meta = jnp.where(lane == 3, r2.astype(jnp.int32), meta)
    wt = jnp.where(lane == 0, w1, jnp.where(lane == 1, w2, 0.0))
    return x3, hn, meta, wt, carry, total


def _mix1_kernel(x_ref, nw_ref, sc_ref, sh_ref, w_ref, cw_ref, cb_ref, ga_ref, gab_ref, gx_ref, gxb_ref,
                 lam_ref, sw_ref, o_ref, tail_c_ref, tail_d_ref, h_ref, *, tiles_per_seq):
    i = pl.program_id(0)

    @pl.when(i % tiles_per_seq == 0)
    def _():
        tail_c_ref[...] = jnp.zeros_like(tail_c_ref)
        tail_d_ref[...] = jnp.zeros_like(tail_d_ref)
        h_ref[...] = jnp.zeros_like(h_ref)

    cat = _mix1_tile(x_ref[...], nw_ref[...], sc_ref[0], sh_ref[0], w_ref, cw_ref, cb_ref, ga_ref,
                     gab_ref, gx_ref, gxb_ref, lam_ref, sw_ref, tail_c_ref, tail_d_ref, h_ref)
    o_ref[...] = cat.astype(o_ref.dtype)


def _mix1(x2d, nw, sc, sh, w_in, conv_w, conv_b, ga_w, ga_b, gx_w, gx_b, lam, sconv_w, seq_len):
    n, d = x2d.shape
    tm = TOKEN_TILE
    tps = seq_len // tm
    cd_in = w_in.shape[1]
    cd_out = LRU_WIDTH + SC_WIDTH
    row = lambda width: pl.BlockSpec((tm, width), lambda i: (i, 0))
    per_b = pl.BlockSpec((1, 1, d), lambda i: (i // tps, 0, 0))
    ga = _pair_block_diag(ga_w)
    gx = _pair_block_diag(gx_w)
    vec = lambda v: v.reshape(1, -1)
    return pl.pallas_call(
        functools.partial(_mix1_kernel, tiles_per_seq=tps),
        grid=(n // tm,),
        in_specs=[row(d), _const_spec((1, d)), per_b, per_b, _resident_spec((d, cd_in)),
                  _const_spec(conv_w.shape), _const_spec((1, LRU_WIDTH)),
                  _const_spec(ga.shape), _const_spec((1, LRU_WIDTH)),
                  _const_spec(gx.shape), _const_spec((1, LRU_WIDTH)),
                  _const_spec((1, LRU_WIDTH)), _const_spec(sconv_w.shape)],
        out_specs=row(cd_out),
        out_shape=jax.ShapeDtypeStruct((n, cd_out), BF16),
        scratch_shapes=[pltpu.VMEM((SUBLANES, LRU_WIDTH), F32), pltpu.VMEM((SUBLANES, SC_WIDTH), F32),
                        pltpu.VMEM((SUBLANES, LRU_WIDTH), F32)],
        compiler_params=_cparams("arbitrary"),
        name="rglru_shortconv_mixer",
    )(x2d, vec(nw), sc, sh, w_in.astype(BF16), conv_w, vec(conv_b), ga, vec(ga_b), gx, vec(gx_b),
      vec(lam), sconv_w)


def _route_kernel(cat_ref, x_ref, wo_ref, g1_ref, nw_ref, sc_ref, sh_ref, rw_ref, rb_ref,
                  x3_ref, hn_ref, metat_ref, meta_ref, wt_ref, base_ref, cnt_ref, carry_ref):
    @pl.when(pl.program_id(0) == 0)
    def _():
        carry_ref[...] = jnp.zeros_like(carry_ref)

    x3, hn, meta, wt, before, total = _route_tile(cat_ref[...], x_ref[...], wo_ref, g1_ref[0], nw_ref[...],
                                                  sc_ref[0], sh_ref[0], rw_ref, rb_ref, carry_ref)
    x3_ref[...] = x3
    hn_ref[...] = hn
    meta_ref[...] = meta
    metat_ref[...] = jnp.transpose(meta.astype(F32))[:SUBLANES].astype(jnp.int32)
    wt_ref[...] = wt
    base_ref[0] = jnp.broadcast_to(before, base_ref.shape[1:]).astype(jnp.int32)
    cnt_ref[...] = jnp.broadcast_to(total, cnt_ref.shape).astype(jnp.int32)


def _route(cat, x2d, w_out, g1, nw, sc, sh, router_w, router_b, seq_len):
    n, d = x2d.shape
    tm = TOKEN_TILE
    tps = seq_len // tm
    row = lambda width: pl.BlockSpec((tm, width), lambda i: (i, 0))
    per_b = pl.BlockSpec((1, 1, d), lambda i: (i // tps, 0, 0))
    rw = jnp.zeros((d, LANES), F32).at[:, :N_EXPERTS].set(router_w)
    rb = jnp.zeros((1, LANES), F32).at[0, :N_EXPERTS].set(router_b)
    return pl.pallas_call(
        _route_kernel,
        grid=(n // tm,),
        in_specs=[row(cat.shape[1]), row(d), _resident_spec(w_out.shape), per_b, _const_spec((1, d)),
                  per_b, per_b, _const_spec((d, LANES)), _const_spec((1, LANES))],
        out_specs=[row(d), row(d), pl.BlockSpec((SUBLANES, tm), lambda i: (0, i)), row(LANES), row(LANES),
                   pl.BlockSpec((1, SUBLANES, LANES), lambda i: (i, 0, 0)), _const_spec((SUBLANES, LANES))],
        out_shape=[jax.ShapeDtypeStruct((n, d), F32), jax.ShapeDtypeStruct((n, d), F32),
                   jax.ShapeDtypeStruct((SUBLANES, n), jnp.int32), jax.ShapeDtypeStruct((n, LANES), jnp.int32),
                   jax.ShapeDtypeStruct((n, LANES), F32),
                   jax.ShapeDtypeStruct((n // tm, SUBLANES, LANES), jnp.int32),
                   jax.ShapeDtypeStruct((SUBLANES, LANES), jnp.int32)],
        scratch_shapes=[pltpu.VMEM((SUBLANES, LANES), F32)],
        compiler_params=_cparams("arbitrary"),
        name="out_proj1_router",
    )(cat, x2d, w_out.astype(BF16), g1, nw.reshape(1, d), sc, sh, rw, rb)


def _local_rows(tr):
    return 2 * tr + N_EXPERTS * SUBLANES


def _xs_rows(n):
    worst = 2 * n + (n // TOKEN_TILE) * N_EXPERTS * (SUBLANES - 1)
    return -(-worst // MOE_TILE) * MOE_TILE


def _local_pos(e_k, r_k, delta_ref, tile):
    shift = jnp.zeros_like(r_k)
    for e in range(N_EXPERTS):
        shift = jnp.where(e_k == e, delta_ref[tile * N_EXPERTS + e], shift)
    return r_k + shift


def _dispatch_kernel(delta_ref, gdst_ref, ng_ref, hn_ref, meta_ref, xs_ref, sbuf_ref, zero_ref, sem,
                     *, n_tokens):
    j = pl.program_id(0)
    tr = hn_ref.shape[0]
    lrows = sbuf_ref.shape[0]
    groups = lrows // SUBLANES

    @pl.when(j == 0)
    def _():
        zero_ref[...] = jnp.zeros_like(zero_ref)
        zr = zero_ref.shape[0]
        tail = [pltpu.make_async_copy(zero_ref, xs_ref.at[pl.ds(2 * n_tokens + k * zr, zr)], sem)
                for k in range((xs_ref.shape[0] - 2 * n_tokens) // zr)]
        for cp in tail:
            cp.start()
        for cp in tail:
            cp.wait()

    meta = meta_ref[...]
    lp1 = _local_pos(meta[0:1], meta[2:3], delta_ref, j)
    lp2 = _local_pos(meta[1:2], meta[3:4], delta_ref, j)
    r_idx = lax.broadcasted_iota(jnp.int32, (lrows, tr), 0)
    onehot = jnp.logical_or(r_idx == lp1, r_idx == lp2).astype(BF16)
    sbuf_ref[...] = jnp.dot(onehot, hn_ref[...].astype(BF16), preferred_element_type=F32)

    def group_copy(g):
        dst = pl.multiple_of(gdst_ref[j * groups + g], SUBLANES)
        src = pl.multiple_of(g * SUBLANES, SUBLANES)
        return pltpu.make_async_copy(sbuf_ref.at[pl.ds(src, SUBLANES)], xs_ref.at[pl.ds(dst, SUBLANES)], sem)

    def start(g, c):
        group_copy(g).start()
        return c

    def wait(g, c):
        group_copy(g).wait()
        return c

    lax.fori_loop(0, ng_ref[j], start, 0)
    lax.fori_loop(0, ng_ref[j], wait, 0)


def _dispatch(hn, meta_t, tables):
    n, d = hn.shape
    tr = TOKEN_TILE
    lrows = _local_rows(tr)
    return pl.pallas_call(
        functools.partial(_dispatch_kernel, n_tokens=n),
        grid_spec=pltpu.PrefetchScalarGridSpec(
            num_scalar_prefetch=3,
            grid=(n // tr,),
            in_specs=[pl.BlockSpec((tr, d), lambda j, *_: (j, 0)),
                      pl.BlockSpec((SUBLANES, tr), lambda j, *_: (0, j))],
            out_specs=pl.BlockSpec(memory_space=pl.ANY),
            scratch_shapes=[pltpu.VMEM((lrows, d), F32), pltpu.VMEM((MOE_SUB, d), F32),
                            pltpu.SemaphoreType.DMA(())]),
        out_shape=jax.ShapeDtypeStruct((_xs_rows(n), d), F32),
        compiler_params=_cparams("arbitrary"),
        name="moe_dispatch",
    )(*tables, hn, meta_t)


_ITEM_PAD, _ITEM_COMPUTE, _ITEM_ZERO = 0, 1, 2


def _moe_kernel(ti_ref, te_ref, tv_ref, lo_ref, hi_ref, first_ref, x_ref, wg_ref, wu_ref, wd_ref, o_ref, xb_ref):
    w = pl.program_id(0)
    f = pl.program_id(1)
    tm = x_ref.shape[0]
    sub = MOE_SUB
    sub_shift = int(math.log2(sub))

    def swiglu_part(xb, wg, wu, wd):
        hg = jnp.dot(xb, wg, preferred_element_type=F32)
        hu = jnp.dot(xb, wu, preferred_element_type=F32)
        act = (_silu(hg) * hu).astype(BF16)
        return jnp.dot(act, wd, preferred_element_type=F32)

    @pl.when(jnp.logical_and(tv_ref[w] == _ITEM_ZERO, f == 0))
    def _():
        o_ref[...] = jnp.zeros_like(o_ref)

    @pl.when(tv_ref[w] == _ITEM_COMPUTE)
    def _():
        lo = lo_ref[w]
        hi = hi_ref[w]
        whole = jnp.logical_and(lo == 0, hi == tm)

        @pl.when(f == 0)
        def _():
            row = lax.broadcasted_iota(jnp.int32, (tm, 1), 0)
            mine = jnp.logical_and(row >= lo, row < hi)
            xb_ref[...] = jnp.where(mine, x_ref[...], 0.0).astype(BF16)

        @pl.when(whole)
        def _():
            part = swiglu_part(xb_ref[...], wg_ref[0].astype(BF16), wu_ref[0].astype(BF16),
                               wd_ref[0].astype(BF16))

            @pl.when(f == 0)
            def _():
                o_ref[...] = part

            @pl.when(f != 0)
            def _():
                o_ref[...] += part

        @pl.when(jnp.logical_not(whole))
        def _():
            @pl.when(jnp.logical_and(f == 0, first_ref[w] == 1))
            def _():
                o_ref[...] = jnp.zeros_like(o_ref)

            def sub_block(s, carry):
                rows = pl.ds(pl.multiple_of(s * sub, sub), sub)
                o_ref[rows, :] += swiglu_part(xb_ref[rows, :], wg_ref[0].astype(BF16),
                                              wu_ref[0].astype(BF16), wd_ref[0].astype(BF16))
                return carry

            lax.fori_loop(lo >> sub_shift, (hi + sub - 1) >> sub_shift, sub_block, 0)


def _moe_ffn(xs, items, wg, wu, wd):
    rows, d = xs.shape
    tm = MOE_TILE
    tf = MOE_FF_TILE
    nf = wg.shape[2] // tf
    n_items = items[0].shape[0]
    def f_idx(f, kind):
        v = (kind == _ITEM_COMPUTE).astype(jnp.int32)
        return f * v + (nf - 1) * (1 - v)

    return pl.pallas_call(
        _moe_kernel,
        grid_spec=pltpu.PrefetchScalarGridSpec(
            num_scalar_prefetch=6,
            grid=(n_items, nf),
            in_specs=[pl.BlockSpec((tm, d), lambda w, f, ti, te, tv, *_: (ti[w], 0)),
                      pl.BlockSpec((1, d, tf), lambda w, f, ti, te, tv, *_: (te[w], 0, f_idx(f, tv[w]))),
                      pl.BlockSpec((1, d, tf), lambda w, f, ti, te, tv, *_: (te[w], 0, f_idx(f, tv[w]))),
                      pl.BlockSpec((1, tf, d), lambda w, f, ti, te, tv, *_: (te[w], f_idx(f, tv[w]), 0))],
            out_specs=pl.BlockSpec((tm, d), lambda w, f, ti, te, tv, *_: (ti[w], 0)),
            scratch_shapes=[pltpu.VMEM((tm, d), BF16)]),
        out_shape=jax.ShapeDtypeStruct((rows, d), F32),
        compiler_params=_cparams("arbitrary", "arbitrary"),
        name="moe_expert_swiglu",
    )(*items, xs, wg, wu, wd)


def _combine_kernel(delta_ref, gdst_ref, ng_ref, ys_ref, x_ref, meta_ref, wt_ref, g2_ref, fw_ref, o_ref,
                    ybuf_ref, sem):
    j = pl.program_id(0)
    tr = x_ref.shape[0]
    lrows = ybuf_ref.shape[0]
    groups = _local_rows(tr) // SUBLANES
    ybuf_ref[2 * tr:, :] = jnp.zeros((lrows - 2 * tr, ybuf_ref.shape[1]), F32)

    def group_copy(g):
        src = pl.multiple_of(gdst_ref[j * groups + g], SUBLANES)
        dst = pl.multiple_of(g * SUBLANES, SUBLANES)
        return pltpu.make_async_copy(ys_ref.at[pl.ds(src, SUBLANES)], ybuf_ref.at[pl.ds(dst, SUBLANES)], sem)

    def start(g, c):
        group_copy(g).start()
        return c

    def wait(g, c):
        group_copy(g).wait()
        return c

    lax.fori_loop(0, ng_ref[j], start, 0)
    lax.fori_loop(0, ng_ref[j], wait, 0)

    meta = meta_ref[...]
    wt = wt_ref[...]
    lp1 = _local_pos(meta[:, 0:1], meta[:, 2:3], delta_ref, j)
    lp2 = _local_pos(meta[:, 1:2], meta[:, 3:4], delta_ref, j)
    l_idx = lax.broadcasted_iota(jnp.int32, (tr, lrows), 1)
    pick = jnp.where(l_idx == lp1, wt[:, 0:1], 0.0) + jnp.where(l_idx == lp2, wt[:, 1:2], 0.0)
    ffn = _dot_x(pick, ybuf_ref[...], 1, 2)
    x4 = x_ref[...] + g2_ref[0] * ffn
    o_ref[...] = (x4 * _rms_scale(x4)) * fw_ref[...]


def _combine(ys, tables, x3, meta, wt, g2, final_w, seq_len):
    n, d = x3.shape
    tr = TOKEN_TILE
    tps = seq_len // tr
    lrows = -(-_local_rows(tr) // LANES) * LANES
    return pl.pallas_call(
        _combine_kernel,
        grid_spec=pltpu.PrefetchScalarGridSpec(
            num_scalar_prefetch=3,
            grid=(n // tr,),
            in_specs=[pl.BlockSpec(memory_space=pl.ANY),
                      pl.BlockSpec((tr, d), lambda j, *_: (j, 0)),
                      pl.BlockSpec((tr, LANES), lambda j, *_: (j, 0)),
                      pl.BlockSpec((tr, LANES), lambda j, *_: (j, 0)),
                      pl.BlockSpec((1, 1, d), lambda j, *_: (j // tps, 0, 0)),
                      pl.BlockSpec((1, d), lambda j, *_: (0, 0))],
            out_specs=pl.BlockSpec((tr, d), lambda j, *_: (j, 0)),
            scratch_shapes=[pltpu.VMEM((lrows, d), F32), pltpu.SemaphoreType.DMA(())]),
        out_shape=jax.ShapeDtypeStruct((n, d), F32),
        compiler_params=_cparams("arbitrary"),
        name="moe_combine_final_norm",
    )(*tables, ys, x3, meta, wt, g2, final_w.reshape(1, d))


def _moe_tables(tile_base, counts, n_tokens):
    i32 = lambda t: t.astype(jnp.int32)
    tr = TOKEN_TILE
    tm = MOE_TILE
    n_tiles = n_tokens // tr
    n_groups_max = _local_rows(tr) // SUBLANES
    before = tile_base[:, 0, :N_EXPERTS]
    total = counts[0, :N_EXPERTS]
    run = jnp.concatenate([before[1:], total[None]], axis=0) - before
    run = (run + SUBLANES - 1) // SUBLANES * SUBLANES
    l_end = jnp.cumsum(run, axis=1)
    l_start = l_end - run
    g_size = jnp.sum(run, axis=0)
    g_end = jnp.cumsum(g_size)
    g_off = g_end - g_size
    g_start = g_off[None, :] + jnp.cumsum(run, axis=0) - run
    delta = l_start - before
    row0 = (jnp.arange(n_groups_max, dtype=jnp.int32) * SUBLANES)[None, :, None]
    in_run = jnp.logical_and(row0 >= l_start[:, None, :], row0 < l_end[:, None, :])
    group_dst = jnp.sum(jnp.where(in_run, g_start[:, None, :] + row0 - l_start[:, None, :], 0), axis=2)
    n_groups = l_end[:, -1] // SUBLANES

    xs_tiles = _xs_rows(n_tokens) // tm
    n_items = xs_tiles + N_EXPERTS - 1
    first_tile = g_off // tm
    per_e = jnp.where(g_size > 0, (g_end - 1) // tm - first_tile + 1, 0)
    item_end = jnp.cumsum(per_e)
    item_start = item_end - per_e
    n_real = item_end[-1]
    used_tiles = (g_end[-1] + tm - 1) // tm
    w = jnp.arange(n_items, dtype=jnp.int32)
    w_real = jnp.minimum(w, n_real - 1)
    te = jnp.minimum(jnp.sum((item_end[None, :] <= w_real[:, None]).astype(jnp.int32), axis=1),
                     N_EXPERTS - 1)
    ti_real = jnp.take(first_tile, te) + (w_real - jnp.take(item_start, te))
    is_real = w < n_real
    is_zero = jnp.logical_and(w >= n_real, w < n_real + (xs_tiles - used_tiles))
    ti = jnp.where(is_real, ti_real, jnp.minimum(used_tiles + (w - n_real), xs_tiles - 1))
    kind = jnp.where(is_real, _ITEM_COMPUTE, jnp.where(is_zero, _ITEM_ZERO, _ITEM_PAD))
    lo = jnp.where(is_real, jnp.clip(jnp.take(g_off, te) - ti * tm, 0, tm), 0)
    hi = jnp.where(is_real, jnp.clip(jnp.take(g_end, te) - ti * tm, 0, tm), 0)
    first = jnp.concatenate([jnp.ones((1,), jnp.int32), i32(ti[1:] != ti[:-1])])
    items = (i32(ti), i32(te), i32(kind), i32(lo), i32(hi), first)
    return (i32(delta).reshape(-1), i32(group_dst).reshape(-1), i32(n_groups)), items


def kernel(x, c, rel_bias, ada_w, ada_b, norm_mix_w, norm_ffn_w, final_norm_w, ab_w_in, attn_sinks,
           dn_conv_w, dn_a_log, dn_dt_bias, dn_norm_w, ab_w_out, ffn_w_gate, ffn_w_up, ffn_w_down,
           cd_w_in, lru_conv_w, lru_conv_b, lru_gate_a_w, lru_gate_a_b, lru_gate_x_w, lru_gate_x_b,
           lru_lambda, sconv_w, cd_w_out, moe_router_w, moe_router_b, moe_w_gate, moe_w_up, moe_w_down):
    bsz, seq_len, d = x.shape
    n = bsz * seq_len
    x2d = x.reshape(n, d)
    mods = _ada_mods(c, ada_w, ada_b)

    sh1, sc1, g1, sh2, sc2, g2 = (mods[0, k] for k in range(6))
    qa, kd, vd, qn, kn, vb, gs, bexp, gcexp = _in_proj0(
        x2d, norm_mix_w[0], sc1, sh1, ab_w_in[0], dn_conv_w[0], dn_a_log[0], dn_dt_bias[0], seq_len)
    attn = _attention(qa, kd, vd, _bias_table(rel_bias), attn_sinks[0], seq_len)
    dn = _deltanet(qn, kn, vb, gs, bexp, gcexp, dn_norm_w[0], seq_len)
    x2 = _mid0(attn, dn, x2d, ab_w_out[0], g1, norm_ffn_w[0], sc2, sh2, g2,
               ffn_w_gate[0], ffn_w_up[0], ffn_w_down[0], seq_len)

    sh1, sc1, g1, sh2, sc2, g2 = (mods[1, k] for k in range(6))
    cat = _mix1(x2, norm_mix_w[1], sc1, sh1, cd_w_in[0], lru_conv_w[0], lru_conv_b[0],
                lru_gate_a_w[0], lru_gate_a_b[0], lru_gate_x_w[0], lru_gate_x_b[0],
                lru_lambda[0], sconv_w[0], seq_len)
    x3, hn4, meta_t, meta, wt, tile_base, counts = _route(
        cat, x2, cd_w_out[0], g1, norm_ffn_w[1], sc2, sh2, moe_router_w[0], moe_router_b[0], seq_len)
    tables, items = _moe_tables(tile_base, counts, n)
    xs = _dispatch(hn4, meta_t, tables)
    ys = _moe_ffn(xs, items, moe_w_gate[0], moe_w_up[0], moe_w_down[0])
    out = _combine(ys, tables, x3, meta, wt, g2, final_norm_w, seq_len)
    return out.reshape(bsz, seq_len, d)
```

```python
import functools
import math

import numpy as np
import jax
import jax.numpy as jnp
from jax import lax
from jax.experimental import pallas as pl
from jax.experimental.pallas import tpu as pltpu

D_MODEL = 1024
EPS = 1e-6
HEAD_DIM = 64
A_Q_HEADS = 8
A_KV_HEADS = 2
WINDOW = 128
N_BUCKETS = 32
MAX_DISTANCE = 128
B_HEADS = 8
B_CONV = 4
CHUNK = 64
A_Q_W = A_Q_HEADS * HEAD_DIM
A_KV_W = A_KV_HEADS * HEAD_DIM
B_W = B_HEADS * HEAD_DIM
B_QKV_W = 3 * B_W
LRU_WIDTH = D_MODEL
LRU_BLOCKS = 8
LRU_C = 8.0
SC_WIDTH = D_MODEL // 2
D_FF = 2816
N_EXPERTS = 8
D_FF_EXPERT = 3584

LANES = 128
SUBLANES = 8
VMEM_LIMIT_BYTES = 56 * 1024 * 1024
TOKEN_TILE = 512
MOE_TILE = 1024
MOE_SUB = 256
MOE_FF_TILE = 512
NEG_BIG = -1e30

F32 = jnp.float32
BF16 = jnp.bfloat16


def _cparams(*sem):
    return pltpu.CompilerParams(dimension_semantics=tuple(sem), vmem_limit_bytes=VMEM_LIMIT_BYTES)


def _const_spec(shape):
    nd = len(shape)
    return pl.BlockSpec(shape, lambda *_: (0,) * nd)


def _bdot(a, b):
    return jnp.dot(a.astype(BF16), b.astype(BF16), preferred_element_type=F32)


def _bdot_nt(a, b):
    return lax.dot_general(a.astype(BF16), b.astype(BF16), (((1,), (1,)), ((), ())),
                           preferred_element_type=F32)


def _bdot_tn(a, b):
    return lax.dot_general(a.astype(BF16), b.astype(BF16), (((0,), (0,)), ((), ())),
                           preferred_element_type=F32)


def _split(x, n):
    parts = []
    r = x
    for i in range(n):
        p = r.astype(BF16)
        parts.append(p)
        if i + 1 < n:
            r = r - p.astype(F32)
    return parts


def _dot_x(a, b, na=2, nb=2):
    asp = _split(a, na) if na > 1 else [a.astype(BF16)]
    bsp = _split(b, nb) if nb > 1 else [b.astype(BF16)]
    acc = None
    for i, ai in enumerate(asp):
        for j, bj in enumerate(bsp):
            if i + j >= max(na, nb):
                continue
            t = jnp.dot(ai, bj, preferred_element_type=F32)
            acc = t if acc is None else acc + t
    return acc


def _silu(x):
    return x * (1.0 / (1.0 + jnp.exp(-x)))


def _sigmoid(x):
    return 1.0 / (1.0 + jnp.exp(-x))


def _log1p(z):
    u = 1.0 + z
    tiny = u == 1.0
    return jnp.where(tiny, z, jnp.log(u) * (z / jnp.where(tiny, 1.0, u - 1.0)))


def _softplus(x):
    return jnp.maximum(x, 0.0) + _log1p(jnp.exp(-jnp.abs(x)))


def _neg_expm1(y):
    return -jnp.tanh(0.5 * y) * (jnp.exp(y) + 1.0)


def _rms_scale(x):
    width = x.shape[1]
    mean_w = jnp.full((width, LANES), 1.0 / width, BF16)
    ms = _dot_x(x * x, mean_w, 2, 1)
    r = lax.rsqrt(ms + EPS)
    return jnp.concatenate([r] * (width // LANES), axis=1)


def _norm_mod(x, w, sc, sh, on_mxu=False):
    if on_mxu:
        scale = _rms_scale(x)
    else:
        scale = lax.rsqrt(jnp.mean(x * x, axis=-1, keepdims=True) + EPS)
    return (x * scale) * w * (1.0 + sc) + sh


def _shift_rows(x, k, prev_tail):
    n, width = x.shape
    x3 = x.reshape(n // SUBLANES, SUBLANES, width)
    rot = pltpu.roll(x3, k, 1)
    rot_prev = jnp.concatenate([pltpu.roll(prev_tail, k, 0)[None], rot[:-1]], axis=0)
    sub = lax.broadcasted_iota(jnp.int32, x3.shape, 1)
    return jnp.where(sub >= k, rot, rot_prev).reshape(n, width)


def _ada_kernel(c_ref, w_ref, b_ref, o_ref):
    c = c_ref[...]
    cond = _silu(c)
    o_ref[0] = _dot_x(cond, w_ref[0], 3, 3) + b_ref[0]


def _ada_mods(c, ada_w, ada_b):
    depth, d, six_d = ada_w.shape
    bsz = c.shape[0]
    rows = max(SUBLANES, bsz)
    c_pad = jnp.zeros((rows, d), F32).at[:bsz].set(c)
    tn = 1536
    out = pl.pallas_call(
        _ada_kernel,
        grid=(depth, six_d // tn),
        in_specs=[pl.BlockSpec((rows, d), lambda l, j: (0, 0)),
                  pl.BlockSpec((1, d, tn), lambda l, j: (l, 0, j)),
                  pl.BlockSpec((1, 1, tn), lambda l, j: (l, 0, j))],
        out_specs=pl.BlockSpec((1, rows, tn), lambda l, j: (l, 0, j)),
        out_shape=jax.ShapeDtypeStruct((depth, rows, six_d), F32),
        compiler_params=_cparams("parallel", "parallel"),
        name="ada_mods",
    )(c_pad, ada_w, ada_b.reshape(depth, 1, six_d))
    return out[:, :bsz].reshape(depth, bsz, 6, 1, d).transpose(0, 2, 1, 3, 4)


def _t5_bucket(dist):
    max_exact = N_BUCKETS // 2
    d = np.maximum(dist, 0)
    large = max_exact + (np.log(np.maximum(d, 1) / max_exact) / math.log(MAX_DISTANCE / max_exact)
                         * (N_BUCKETS - max_exact)).astype(np.int32)
    large = np.minimum(large, N_BUCKETS - 1)
    return np.where(d < max_exact, d, large).astype(np.int32)


def _band_buckets():
    qi = np.arange(WINDOW)[:, None]
    s = np.arange(2 * WINDOW)[None, :]
    dist = qi + WINDOW - s
    in_window = (dist >= 0) & (dist < WINDOW)
    return np.where(in_window, _t5_bucket(dist), -1).astype(np.int32)


def _bias_kernel(rb_ref, bucket_ref, o_ref):
    h = pl.program_id(0)
    bucket = bucket_ref[...]
    acc = jnp.zeros(bucket.shape, F32)
    for b in range(N_BUCKETS):
        acc = jnp.where(bucket == b, rb_ref[b, h], acc)
    o_ref[0] = jnp.where(bucket < 0, NEG_BIG, acc)


def _bias_table(rel_bias):
    bucket = jnp.asarray(_band_buckets())
    out = pl.pallas_call(
        _bias_kernel,
        grid=(A_Q_HEADS,),
        in_specs=[pl.BlockSpec(memory_space=pltpu.SMEM),
                  _const_spec((WINDOW, 2 * WINDOW))],
        out_specs=pl.BlockSpec((1, WINDOW, 2 * WINDOW), lambda h: (h, 0, 0)),
        out_shape=jax.ShapeDtypeStruct((A_Q_HEADS, WINDOW, 2 * WINDOW), F32),
        compiler_params=_cparams("parallel"),
        name="attn_bias_table",
    )(rel_bias, bucket)
    return out.reshape(A_Q_HEADS // 2, 2 * WINDOW, 2 * WINDOW)


_C_QA = 0
_C_KA = _C_QA + A_Q_W
_C_VA = _C_KA + A_KV_W
_C_QKV = _C_VA + A_KV_W
_C_GATE = _C_QKV + B_QKV_W
_C_SMALL = _C_GATE + B_W
_AB_COLS = _C_SMALL + LANES


def _ab_in_weight(w_in):
    return jnp.pad(w_in, ((0, 0), (0, _AB_COLS - w_in.shape[1]))).astype(BF16)


def _dup_heads(t, low):
    swapped = pltpu.roll(t, HEAD_DIM, 1)
    return jnp.concatenate([jnp.where(low, t, swapped), jnp.where(low, swapped, t)], axis=1)


def _chunk_tril(tm):
    r = np.arange(tm)
    return ((r[:, None] >= r[None, :]) & (r[:, None] // CHUNK == r[None, :] // CHUNK)).astype(np.float32)


def _head_selector():
    e = np.zeros((B_W, LANES), np.float32)
    for h in range(B_HEADS):
        e[h * HEAD_DIM:(h + 1) * HEAD_DIM, h] = 1.0
    return e


def _in0_kernel(x_ref, nw_ref, sc_ref, sh_ref, w_ref, cw_ref, sel_ref, selt_ref, tril_ref, alog_ref, dtb_ref,
                qa_ref, kd_ref, vd_ref, qn_ref, kn_ref, vb_ref, gs_ref, bexp_ref, gcexp_ref,
                tail_ref, *, tiles_per_seq):
    i = pl.program_id(0)

    @pl.when(i % tiles_per_seq == 0)
    def _():
        tail_ref[...] = jnp.zeros_like(tail_ref)

    hn = _norm_mod(x_ref[...], nw_ref[...], sc_ref[0], sh_ref[0])
    proj = jnp.dot(hn.astype(BF16), w_ref[...], preferred_element_type=F32)
    low = lax.broadcasted_iota(jnp.int32, (proj.shape[0], LANES), 1) < HEAD_DIM
    qa_ref[...] = proj[:, _C_QA:_C_KA].astype(BF16)
    kd_ref[...] = _dup_heads(proj[:, _C_KA:_C_VA], low).astype(BF16)
    vd_ref[...] = _dup_heads(proj[:, _C_VA:_C_QKV], low).astype(BF16)

    xq = proj[:, _C_QKV:_C_GATE]
    tail = tail_ref[...]
    cw = cw_ref[...]
    y = xq * cw[B_CONV - 1:B_CONV]
    for k in range(1, B_CONV):
        y = y + _shift_rows(xq, k, tail) * cw[B_CONV - 1 - k:B_CONV - k]
    tail_ref[...] = xq[xq.shape[0] - SUBLANES:]
    y = _silu(y)
    q, k_, v = y[:, :B_W], y[:, B_W:2 * B_W], y[:, 2 * B_W:]

    def l2n(t):
        ssq = _dot_x(t * t, sel_ref[...], 2, 1)
        r = lax.rsqrt(ssq + EPS)
        return t * _dot_x(r, selt_ref[...], 2, 1)

    qn_ref[...] = l2n(q) * (HEAD_DIM ** -0.5)
    kn_ref[...] = l2n(k_)
    vb_ref[...] = v
    gs_ref[...] = _silu(proj[:, _C_GATE:_C_SMALL])
    small = proj[:, _C_SMALL:]
    lane = lax.broadcasted_iota(jnp.int32, small.shape, 1)
    beta = jnp.where(lane < B_HEADS, _sigmoid(small), 0.0)
    dec = pltpu.roll(small, LANES - B_HEADS, 1)
    g = jnp.where(lane < B_HEADS, -jnp.exp(alog_ref[...]) * _softplus(dec + dtb_ref[...]), 0.0)
    bexp_ref[...] = _dot_x(beta, selt_ref[...], 2, 1)
    gc = _dot_x(tril_ref[...], g, 1, 3)
    gcexp_ref[...] = _dot_x(gc, selt_ref[...], 3, 1)


def _in_proj0(x2d, nw, sc, sh, w_in, conv_w, a_log, dt_bias, seq_len):
    n, d = x2d.shape
    tm = TOKEN_TILE
    tiles_per_seq = seq_len // tm
    w = _ab_in_weight(w_in)
    sel = jnp.asarray(_head_selector(), BF16)
    selt = jnp.asarray(_head_selector().T.copy(), BF16)
    tril = jnp.asarray(_chunk_tril(tm), BF16)
    pad8 = lambda v: jnp.zeros((1, LANES), F32).at[0, :B_HEADS].set(v)
    row = lambda width: pl.BlockSpec((tm, width), lambda i: (i, 0))
    per_b = pl.BlockSpec((1, 1, d), lambda i: (i // tiles_per_seq, 0, 0))
    outs = pl.pallas_call(
        functools.partial(_in0_kernel, tiles_per_seq=tiles_per_seq),
        grid=(n // tm,),
        in_specs=[row(d), _const_spec((1, d)), per_b, per_b,
                  _resident_spec((d, _AB_COLS)), _const_spec((B_CONV, B_QKV_W)),
                  _const_spec((B_W, LANES)), _const_spec((LANES, B_W)), _const_spec((tm, tm)),
                  _const_spec((1, LANES)), _const_spec((1, LANES))],
        out_specs=[row(A_Q_W), row(2 * A_KV_W), row(2 * A_KV_W)] + [row(B_W)] * 6,
        out_shape=[jax.ShapeDtypeStruct((n, A_Q_W), BF16),
                   jax.ShapeDtypeStruct((n, 2 * A_KV_W), BF16),
                   jax.ShapeDtypeStruct((n, 2 * A_KV_W), BF16)]
        + [jax.ShapeDtypeStruct((n, B_W), F32)] * 6,
        scratch_shapes=[pltpu.VMEM((SUBLANES, B_QKV_W), F32)],
        compiler_params=_cparams("arbitrary"),
        name="in_proj0",
    )(x2d, nw.reshape(1, d), sc, sh, w, conv_w, sel, selt, tril, pad8(a_log), pad8(dt_bias))
    return outs


def _attn_kernel(sink_ref, q_ref, kp_ref, kc_ref, vp_ref, vc_ref, bm_ref, o_ref, *, blocks_per_seq):
    i = pl.program_id(0)
    first = (i % blocks_per_seq) == 0
    w = WINDOW
    lane = lax.broadcasted_iota(jnp.int32, (w, LANES), 1)
    low = lane < HEAD_DIM
    col = lax.broadcasted_iota(jnp.int32, (2 * w, 2 * w), 1)
    row = lax.broadcasted_iota(jnp.int32, (2 * w, 1), 0)
    prev_dead = jnp.logical_and(first, col < w)
    q_all = q_ref[...]
    zero = jnp.zeros((), q_all.dtype)
    pairs = range(A_Q_HEADS // 2)
    kv_of = [(2 * j) // (A_Q_HEADS // A_KV_HEADS) for j in pairs]
    qs = [jnp.concatenate([jnp.where(low, q_all[:, j * LANES:(j + 1) * LANES], zero),
                           jnp.where(low, zero, q_all[:, j * LANES:(j + 1) * LANES])], axis=0)
          for j in pairs]
    kd = [jnp.concatenate([kp_ref[:, kh * LANES:(kh + 1) * LANES],
                           kc_ref[:, kh * LANES:(kh + 1) * LANES]], axis=0) for kh in kv_of]
    vd = [jnp.concatenate([vp_ref[:, kh * LANES:(kh + 1) * LANES],
                           vc_ref[:, kh * LANES:(kh + 1) * LANES]], axis=0) for kh in kv_of]
    s = [lax.dot_general(qs[j], kd[j], (((1,), (1,)), ((), ())), preferred_element_type=F32) for j in pairs]
    s = [jnp.where(prev_dead, NEG_BIG, s[j] * (HEAD_DIM ** -0.5) + bm_ref[j]) for j in pairs]
    sink = [jnp.where(row < w, sink_ref[2 * j], sink_ref[2 * j + 1]) for j in pairs]
    m = [jnp.maximum(jnp.max(s[j], axis=-1, keepdims=True), sink[j]) for j in pairs]
    p = [jnp.exp(s[j] - m[j]) for j in pairs]
    denom = [jnp.sum(p[j], axis=-1, keepdims=True) + jnp.exp(sink[j] - m[j]) for j in pairs]
    pv = [jnp.dot(p[j].astype(BF16), vd[j], preferred_element_type=F32) / denom[j] for j in pairs]
    outs = [jnp.where(low, pv[j][:w], pv[j][w:]) for j in pairs]
    o_ref[...] = jnp.concatenate(outs, axis=1).astype(o_ref.dtype)


def _attention(qa, kd, vd, bias_tbl, sinks, seq_len):
    n = qa.shape[0]
    w = WINDOW
    nb = seq_len // w
    cur = lambda i: (i, 0)
    prev = lambda i: (jnp.where(i % nb == 0, i, i - 1), 0)
    return pl.pallas_call(
        functools.partial(_attn_kernel, blocks_per_seq=nb),
        grid=(n // w,),
        in_specs=[pl.BlockSpec(memory_space=pltpu.SMEM),
                  pl.BlockSpec((w, A_Q_W), cur),
                  pl.BlockSpec((w, 2 * A_KV_W), prev), pl.BlockSpec((w, 2 * A_KV_W), cur),
                  pl.BlockSpec((w, 2 * A_KV_W), prev), pl.BlockSpec((w, 2 * A_KV_W), cur),
                  _const_spec((A_Q_HEADS // 2, 2 * w, 2 * w))],
        out_specs=pl.BlockSpec((w, A_Q_W), cur),
        out_shape=jax.ShapeDtypeStruct((n, A_Q_W), BF16),
        compiler_params=_cparams("parallel"),
        name="swa_attention",
    )(sinks, qa, kd, kd, vd, vd, bias_tbl)


_DN_PAIRS = B_HEADS // 2
_DN_INV_BLOCK = 16
_DN_GROUP = 4


def _block_diag(x, low):
    zero = jnp.zeros((), x.dtype)
    return jnp.concatenate([jnp.where(low, x, zero), jnp.where(low, zero, x)], axis=0)


def _dn_intra(chunks, data_refs, work_refs, consts):
    qn_ref, kn_ref, vb_ref, bexp_ref, gcexp_ref = data_refs
    u_ref, w_ref, qk_ref, qd_ref, kd_ref, egl_ref = work_refs
    low, i_idx, j_idx, ones3 = consts
    c = CHUNK
    units = [(ci, p) for ci in chunks for p in range(_DN_PAIRS)]
    where = [(slice(ci * c, (ci + 1) * c), slice(p * LANES, (p + 1) * LANES)) for ci, p in units]
    causal = i_idx >= j_idx
    strict = i_idx > j_idx
    on_diag = i_idx == j_idx
    eye = on_diag.astype(F32)
    blk_shift = int(math.log2(_DN_INV_BLOCK))
    same_blk = (i_idx >> blk_shift) == (j_idx >> blk_shift)

    q = [qn_ref[rs, ls] for rs, ls in where]
    k = [kn_ref[rs, ls] for rs, ls in where]
    v = [vb_ref[rs, ls] for rs, ls in where]
    b = [bexp_ref[rs, ls] for rs, ls in where]
    gc = [gcexp_ref[rs, ls] for rs, ls in where]

    gr = [jnp.dot(ones3, jnp.concatenate(_split(jnp.where(on_diag, t, 0.0), 3), axis=0),
                  preferred_element_type=F32) for t in gc]
    ks = [_block_diag(t.astype(BF16), low) for t in k]
    qkk = [lax.dot_general(jnp.concatenate([qt, kt], axis=0).astype(BF16), kst,
                           (((1,), (1,)), ((), ())), preferred_element_type=F32)
           for qt, kt, kst in zip(q, k, ks)]
    decay = [jnp.exp(jnp.where(causal, gct - grt, NEG_BIG)) for gct, grt in zip(gc, gr)]
    lmat = [jnp.where(strict, bt * t[c:] * dt, 0.0) for bt, t, dt in zip(b, qkk, decay)]
    qk = [jnp.where(causal, t[:c] * dt, 0.0) for t, dt in zip(qkk, decay)]

    def mm(xs, ys):
        return [_bdot(x, _block_diag(y.astype(BF16), low)) for x, y in zip(xs, ys)]

    l_diag = [jnp.where(same_blk, t, 0.0) for t in lmat]
    l_off = [t - d for t, d in zip(lmat, l_diag)]
    pw = [-t for t in l_diag]
    d_inv = [eye + t for t in pw]
    for _ in range(blk_shift - 1):
        pw = mm(pw, pw)
        d_inv = mm(d_inv, [eye + t for t in pw])
    pw = [-t for t in mm(d_inv, l_off)]
    acc = [eye + t for t in pw]
    for _ in range(int(math.log2(c // _DN_INV_BLOCK)) - 1):
        pw = mm(pw, pw)
        acc = mm(acc, [eye + t for t in pw])
    tmat = mm(acc, d_inv)

    egc = [jnp.exp(t) for t in gc]
    rhs = [jnp.concatenate([_block_diag((vt * bt).astype(BF16), low),
                            _block_diag((kt * (bt * et)).astype(BF16), low)], axis=1)
           for vt, kt, bt, et in zip(v, k, b, egc)]
    uw = [_bdot(t, r) for t, r in zip(tmat, rhs)]
    for n, (ci, p) in enumerate(units):
        g_last = gc[n][c - 1:c, :]
        u_ref[ci, p] = uw[n][:, :LANES]
        w_ref[ci, p] = uw[n][:, LANES:]
        qk_ref[ci, p] = qk[n]
        qd_ref[ci, p] = q[n] * egc[n]
        kd_ref[ci, p] = k[n] * jnp.exp(g_last - gc[n])
        egl_ref[ci, p] = jnp.broadcast_to(jnp.exp(g_last), (SUBLANES, LANES))


def _dn_scan(ci, work_refs, s_ref, gs_ref, nw, o_ref, consts):
    u_ref, w_ref, qk_ref, qd_ref, kd_ref, egl_ref = work_refs
    low, mask_bd, head_mean2 = consts
    c = CHUNK
    rows = slice(ci * c, (ci + 1) * c)
    pairs = range(_DN_PAIRS)
    s_old = [s_ref[p] for p in pairs]
    wq = [_bdot(jnp.concatenate([w_ref[ci, p], qd_ref[ci, p]], axis=0), s_old[p]) for p in pairs]
    v_new = [u_ref[ci, p] - wq[p][:c] for p in pairs]
    o = [wq[p][c:] + _bdot(qk_ref[ci, p], _block_diag(v_new[p].astype(BF16), low)) for p in pairs]
    kv = [_bdot_tn(kd_ref[ci, p], v_new[p]) for p in pairs]
    for p in pairs:
        s_ref[p] = s_old[p] * egl_ref[ci, p][0:1, :] + jnp.where(mask_bd, kv[p], 0.0)
    ms = [jnp.dot(jnp.concatenate(_split(t * t, 2), axis=1), head_mean2, preferred_element_type=F32)
          for t in o]
    for p in pairs:
        ls = slice(p * LANES, (p + 1) * LANES)
        y = (o[p] * lax.rsqrt(ms[p] + EPS)) * nw * gs_ref[rows, ls]
        o_ref[rows, ls] = y.astype(o_ref.dtype)


def _dn_kernel(qn_ref, kn_ref, vb_ref, gs_ref, bexp_ref, gcexp_ref, nw_ref, o_ref,
               s_ref, u_ref, w_ref, qk_ref, qd_ref, kd_ref, egl_ref, *, groups_per_seq):
    i = pl.program_id(0)

    @pl.when(i % groups_per_seq == 0)
    def _():
        s_ref[...] = jnp.zeros_like(s_ref)

    c = CHUNK
    tm = o_ref.shape[0]
    n_chunks = tm // c
    lane = lax.broadcasted_iota(jnp.int32, (c, LANES), 1)
    low = lane < HEAD_DIM
    i_idx = lax.broadcasted_iota(jnp.int32, (c, LANES), 0)
    j_idx = lane & (c - 1)
    ones3 = jnp.ones((c, 3 * c), BF16)
    rb = lax.broadcasted_iota(jnp.int32, (LANES, LANES), 0)
    cb = lax.broadcasted_iota(jnp.int32, (LANES, LANES), 1)
    mask_bd = (rb < HEAD_DIM) == (cb < HEAD_DIM)
    head_mean = jnp.where(mask_bd, 1.0 / HEAD_DIM, 0.0).astype(BF16)
    head_mean2 = jnp.concatenate([head_mean, head_mean], axis=0)
    data_refs = (qn_ref, kn_ref, vb_ref, bexp_ref, gcexp_ref)
    work_refs = (u_ref, w_ref, qk_ref, qd_ref, kd_ref, egl_ref)
    intra_consts = (low, i_idx, j_idx, ones3)
    scan_consts = (low, mask_bd, head_mean2)
    nw = nw_ref[...]

    groups = [list(range(s, s + _DN_GROUP)) for s in range(0, n_chunks, _DN_GROUP)]
    _dn_intra(groups[0], data_refs, work_refs, intra_consts)
    for j, grp in enumerate(groups):
        if j + 1 < len(groups):
            _dn_intra(groups[j + 1], data_refs, work_refs, intra_consts)
        for ci in grp:
            _dn_scan(ci, work_refs, s_ref, gs_ref, nw, o_ref, scan_consts)


def _deltanet(qn, kn, vb, gs, bexp, gcexp, norm_w, seq_len):
    n = qn.shape[0]
    tm = TOKEN_TILE
    nw2 = jnp.concatenate([norm_w, norm_w]).reshape(1, LANES)
    row = lambda width: pl.BlockSpec((tm, width), lambda i: (i, 0))
    return pl.pallas_call(
        functools.partial(_dn_kernel, groups_per_seq=seq_len // tm),
        grid=(n // tm,),
        in_specs=[row(B_W)] * 6 + [_const_spec((1, LANES))],
        out_specs=row(B_W),
        out_shape=jax.ShapeDtypeStruct((n, B_W), BF16),
        scratch_shapes=[pltpu.VMEM((_DN_PAIRS, LANES, LANES), F32)]
        + [pltpu.VMEM((tm // CHUNK, _DN_PAIRS, CHUNK, LANES), F32)] * 5
        + [pltpu.VMEM((tm // CHUNK, _DN_PAIRS, SUBLANES, LANES), F32)],
        compiler_params=_cparams("arbitrary"),
        name="gated_deltanet",
    )(qn, kn, vb, gs, bexp, gcexp, nw2)


def _resident_spec(shape):
    nd = len(shape)
    return pl.BlockSpec(shape, lambda *_: (0,) * nd, pipeline_mode=pl.Buffered(1))


def _mid0_kernel(attn_ref, dn_ref, x_ref, wo_ref, g1_ref, nw_ref, sc_ref, sh_ref, g2_ref,
                 wg_ref, wu_ref, wd_ref, o_ref):
    mix = (jnp.dot(attn_ref[...], wo_ref[:A_Q_W], preferred_element_type=F32)
           + jnp.dot(dn_ref[...], wo_ref[A_Q_W:], preferred_element_type=F32))
    x1 = x_ref[...] + g1_ref[0] * mix
    hn = _norm_mod(x1, nw_ref[...], sc_ref[0], sh_ref[0]).astype(BF16)
    hg = jnp.dot(hn, wg_ref[...], preferred_element_type=F32)
    hu = jnp.dot(hn, wu_ref[...], preferred_element_type=F32)
    act = (_silu(hg) * hu).astype(BF16)
    o_ref[...] = x1 + g2_ref[0] * jnp.dot(act, wd_ref[...], preferred_element_type=F32)


def _mid0(attn, dn, x2d, w_out, g1, nw, sc, sh, g2, wg, wu, wd, seq_len):
    n, d = x2d.shape
    tm = TOKEN_TILE
    tps = seq_len // tm
    row = lambda width: pl.BlockSpec((tm, width), lambda i: (i, 0))
    per_b = pl.BlockSpec((1, 1, d), lambda i: (i // tps, 0, 0))
    return pl.pallas_call(
        _mid0_kernel,
        grid=(n // tm,),
        in_specs=[row(A_Q_W), row(B_W), row(d), _resident_spec(w_out.shape), per_b,
                  _const_spec((1, d)), per_b, per_b, per_b,
                  _resident_spec(wg.shape), _resident_spec(wu.shape), _resident_spec(wd.shape)],
        out_specs=row(d),
        out_shape=jax.ShapeDtypeStruct((n, d), F32),
        compiler_params=_cparams("parallel"),
        name="out_proj0_swiglu",
    )(attn, dn, x2d, w_out.astype(BF16), g1, nw.reshape(1, d), sc, sh, g2,
      wg.astype(BF16), wu.astype(BF16), wd.astype(BF16))


def _gelu_tanh(x):
    return 0.5 * x * (1.0 + jnp.tanh(math.sqrt(2.0 / math.pi) * (x + 0.044715 * (x * x * x))))


def _linear_scan(a, b, h0):
    n, width = a.shape
    groups = n // SUBLANES
    a = a.reshape(groups, SUBLANES, width)
    b = b.reshape(groups, SUBLANES, width)
    in_group = lax.broadcasted_iota(jnp.int32, a.shape, 1)
    s = 1
    while s < SUBLANES:
        a_sh = pltpu.roll(a, s, 1)
        b_sh = pltpu.roll(b, s, 1)
        valid = in_group >= s
        b = jnp.where(valid, a * b_sh + b, b)
        a = jnp.where(valid, a * a_sh, a)
        s *= 2
    carry = jnp.broadcast_to(h0, (SUBLANES, width))
    out = []
    for g in range(groups):
        hg = a[g] * carry + b[g]
        out.append(hg)
        carry = jnp.broadcast_to(hg[SUBLANES - 1:SUBLANES, :], hg.shape)
    return jnp.concatenate(out, axis=0)


def _mix1_tile(x, nw, sc, sh, w_ref, cw_ref, cb_ref, ga_ref, gab_ref, gx_ref, gxb_ref, lam_ref, sw_ref,
               tail_c_ref, tail_d_ref, h_ref):
    hn = _norm_mod(x, nw, sc, sh, on_mxu=True).astype(BF16)
    proj = jnp.dot(hn, w_ref[...], preferred_element_type=F32)
    w_l = LRU_WIDTH
    xc_in = proj[:, :w_l]
    yc = proj[:, w_l:2 * w_l]
    bd = proj[:, 2 * w_l:2 * w_l + SC_WIDTH]
    cd = proj[:, 2 * w_l + SC_WIDTH:2 * w_l + 2 * SC_WIDTH]
    hd = proj[:, 2 * w_l + 2 * SC_WIDTH:]
    tm = xc_in.shape[0]

    kc = cw_ref.shape[0]
    tail = tail_c_ref[...]
    cw = cw_ref[...]
    xc = xc_in * cw[kc - 1:kc] + cb_ref[...]
    for k in range(1, kc):
        xc = xc + _shift_rows(xc_in, k, tail) * cw[kc - 1 - k:kc - k]
    tail_c_ref[...] = xc_in[tm - SUBLANES:]

    xb = xc.astype(BF16)
    gw = ga_ref.shape[1]
    ra, ri = [], []
    for p in range(ga_ref.shape[0]):
        xin = xb[:, p * gw:(p + 1) * gw]
        ra.append(jnp.dot(xin, ga_ref[p], preferred_element_type=F32))
        ri.append(jnp.dot(xin, gx_ref[p], preferred_element_type=F32))
    r = _sigmoid(jnp.concatenate(ra, axis=1) + gab_ref[...])
    ig = _sigmoid(jnp.concatenate(ri, axis=1) + gxb_ref[...])
    log_a = (-LRU_C) * r * _softplus(-lam_ref[...])
    a = jnp.exp(log_a)
    b = jnp.sqrt(_neg_expm1(2.0 * log_a)) * (ig * xc)
    h = _linear_scan(a, b, h_ref[0:1, :])
    h_ref[...] = jnp.broadcast_to(h[tm - 1:tm, :], h_ref.shape)
    yc_out = h * _gelu_tanh(yc)

    ks = sw_ref.shape[0]
    ch = cd * hd
    tail_d = tail_d_ref[...]
    sw = sw_ref[...]
    conv = ch * sw[ks - 1:ks]
    for k in range(1, ks):
        conv = conv + _shift_rows(ch, k, tail_d) * sw[ks - 1 - k:ks - k]
    tail_d_ref[...] = ch[tm - SUBLANES:]
    return jnp.concatenate([yc_out, bd * conv], axis=1)


def _pair_block_diag(gw):
    nb, bw, _ = gw.shape
    g2 = gw.reshape(nb // 2, 2, bw, bw)
    z = jnp.zeros((nb // 2, bw, bw), gw.dtype)
    top = jnp.concatenate([g2[:, 0], z], axis=2)
    bot = jnp.concatenate([z, g2[:, 1]], axis=2)
    return jnp.concatenate([top, bot], axis=1).astype(BF16)


def _route_tile(cat, x, wo_ref, g1, nw, sc, sh, rw_ref, rb_ref, carry_ref):
    x3 = x + g1 * jnp.dot(cat, wo_ref[...], preferred_element_type=F32)
    hn = _norm_mod(x3, nw, sc, sh)
    tm = hn.shape[0]
    lane = lax.broadcasted_iota(jnp.int32, (tm, LANES), 1)
    logits = _dot_x(hn, rw_ref[...], 2, 2) + rb_ref[...]
    lg = jnp.where(lane < N_EXPERTS, logits, NEG_BIG)
    m1 = jnp.max(lg, axis=1, keepdims=True)
    i1 = jnp.min(jnp.where(lg == m1, lane, LANES), axis=1, keepdims=True)
    lg2 = jnp.where(lane == i1, NEG_BIG, lg)
    m2 = jnp.max(lg2, axis=1, keepdims=True)
    i2 = jnp.min(jnp.where(lg2 == m2, lane, LANES), axis=1, keepdims=True)
    e2 = jnp.exp(m2 - m1)
    w1 = 1.0 / (1.0 + e2)
    w2 = e2 / (1.0 + e2)

    hit1 = lane == i1
    hit2 = lane == i2
    sel = jnp.logical_or(hit1, hit2).astype(F32)
    r_i = lax.broadcasted_iota(jnp.int32, (tm, tm), 0)
    c_i = lax.broadcasted_iota(jnp.int32, (tm, tm), 1)
    tril = (r_i >= c_i).astype(BF16)
    incl = jnp.dot(tril, sel.astype(BF16), preferred_element_type=F32)
    carry = carry_ref[0:1, :]
    excl = incl - sel + carry
    r1 = jnp.sum(jnp.where(hit1, excl, 0.0), axis=1, keepdims=True)
    r2 = jnp.sum(jnp.where(hit2, excl, 0.0), axis=1, keepdims=True)
    total = carry + incl[tm - 1:tm, :]
    carry_ref[...] = jnp.broadcast_to(total, carry_ref.shape)

    meta = jnp.where(lane == 0, i1, 0)
    meta = jnp.where(lane == 1, i2, meta)
    meta = jnp.where(lane == 2, r1.astype(jnp.int32), meta)
    meta = jnp.where(lane == 3, r2.astype(jnp.int32), meta)
    wt = jnp.where(lane == 0, w1, jnp.where(lane == 1, w2, 0.0))
    return x3, hn, meta, wt, carry, total


def _mix1_kernel(x_ref, nw_ref, sc_ref, sh_ref, w_ref, cw_ref, cb_ref, ga_ref, gab_ref, gx_ref, gxb_ref,
                 lam_ref, sw_ref, o_ref, tail_c_ref, tail_d_ref, h_ref, *, tiles_per_seq):
    i = pl.program_id(0)

    @pl.when(i % tiles_per_seq == 0)
    def _():
        tail_c_ref[...] = jnp.zeros_like(tail_c_ref)
        tail_d_ref[...] = jnp.zeros_like(tail_d_ref)
        h_ref[...] = jnp.zeros_like(h_ref)

    cat = _mix1_tile(x_ref[...], nw_ref[...], sc_ref[0], sh_ref[0], w_ref, cw_ref, cb_ref, ga_ref,
                     gab_ref, gx_ref, gxb_ref, lam_ref, sw_ref, tail_c_ref, tail_d_ref, h_ref)
    o_ref[...] = cat.astype(o_ref.dtype)


def _mix1(x2d, nw, sc, sh, w_in, conv_w, conv_b, ga_w, ga_b, gx_w, gx_b, lam, sconv_w, seq_len):
    n, d = x2d.shape
    tm = TOKEN_TILE
    tps = seq_len // tm
    cd_in = w_in.shape[1]
    cd_out = LRU_WIDTH + SC_WIDTH
    row = lambda width: pl.BlockSpec((tm, width), lambda i: (i, 0))
    per_b = pl.BlockSpec((1, 1, d), lambda i: (i // tps, 0, 0))
    ga = _pair_block_diag(ga_w)
    gx = _pair_block_diag(gx_w)
    vec = lambda v: v.reshape(1, -1)
    return pl.pallas_call(
        functools.partial(_mix1_kernel, tiles_per_seq=tps),
        grid=(n // tm,),
        in_specs=[row(d), _const_spec((1, d)), per_b, per_b, _resident_spec((d, cd_in)),
                  _const_spec(conv_w.shape), _const_spec((1, LRU_WIDTH)),
                  _const_spec(ga.shape), _const_spec((1, LRU_WIDTH)),
                  _const_spec(gx.shape), _const_spec((1, LRU_WIDTH)),
                  _const_spec((1, LRU_WIDTH)), _const_spec(sconv_w.shape)],
        out_specs=row(cd_out),
        out_shape=jax.ShapeDtypeStruct((n, cd_out), BF16),
        scratch_shapes=[pltpu.VMEM((SUBLANES, LRU_WIDTH), F32), pltpu.VMEM((SUBLANES, SC_WIDTH), F32),
                        pltpu.VMEM((SUBLANES, LRU_WIDTH), F32)],
        compiler_params=_cparams("arbitrary"),
        name="rglru_shortconv_mixer",
    )(x2d, vec(nw), sc, sh, w_in.astype(BF16), conv_w, vec(conv_b), ga, vec(ga_b), gx, vec(gx_b),
      vec(lam), sconv_w)


def _route_kernel(cat_ref, x_ref, wo_ref, g1_ref, nw_ref, sc_ref, sh_ref, rw_ref, rb_ref,
                  x3_ref, hn_ref, metat_ref, meta_ref, wt_ref, base_ref, cnt_ref, carry_ref):
    @pl.when(pl.program_id(0) == 0)
    def _():
        carry_ref[...] = jnp.zeros_like(carry_ref)

    x3, hn, meta, wt, before, total = _route_tile(cat_ref[...], x_ref[...], wo_ref, g1_ref[0], nw_ref[...],
                                                  sc_ref[0], sh_ref[0], rw_ref, rb_ref, carry_ref)
    x3_ref[...] = x3
    hn_ref[...] = hn
    meta_ref[...] = meta
    metat_ref[...] = jnp.transpose(meta.astype(F32))[:SUBLANES].astype(jnp.int32)
    wt_ref[...] = wt
    base_ref[0] = jnp.broadcast_to(before, base_ref.shape[1:]).astype(jnp.int32)
    cnt_ref[...] = jnp.broadcast_to(total, cnt_ref.shape).astype(jnp.int32)


def _route(cat, x2d, w_out, g1, nw, sc, sh, router_w, router_b, seq_len):
    n, d = x2d.shape
    tm = TOKEN_TILE
    tps = seq_len // tm
    row = lambda width: pl.BlockSpec((tm, width), lambda i: (i, 0))
    per_b = pl.BlockSpec((1, 1, d), lambda i: (i // tps, 0, 0))
    rw = jnp.zeros((d, LANES), F32).at[:, :N_EXPERTS].set(router_w)
    rb = jnp.zeros((1, LANES), F32).at[0, :N_EXPERTS].set(router_b)
    return pl.pallas_call(
        _route_kernel,
        grid=(n // tm,),
        in_specs=[row(cat.shape[1]), row(d), _resident_spec(w_out.shape), per_b, _const_spec((1, d)),
                  per_b, per_b, _const_spec((d, LANES)), _const_spec((1, LANES))],
        out_specs=[row(d), row(d), pl.BlockSpec((SUBLANES, tm), lambda i: (0, i)), row(LANES), row(LANES),
                   pl.BlockSpec((1, SUBLANES, LANES), lambda i: (i, 0, 0)), _const_spec((SUBLANES, LANES))],
        out_shape=[jax.ShapeDtypeStruct((n, d), F32), jax.ShapeDtypeStruct((n, d), F32),
                   jax.ShapeDtypeStruct((SUBLANES, n), jnp.int32), jax.ShapeDtypeStruct((n, LANES), jnp.int32),
                   jax.ShapeDtypeStruct((n, LANES), F32),
                   jax.ShapeDtypeStruct((n // tm, SUBLANES, LANES), jnp.int32),
                   jax.ShapeDtypeStruct((SUBLANES, LANES), jnp.int32)],
        scratch_shapes=[pltpu.VMEM((SUBLANES, LANES), F32)],
        compiler_params=_cparams("arbitrary"),
        name="out_proj1_router",
    )(cat, x2d, w_out.astype(BF16), g1, nw.reshape(1, d), sc, sh, rw, rb)


def _local_rows(tr):
    return 2 * tr + N_EXPERTS * SUBLANES


def _xs_rows(n):
    worst = 2 * n + (n // TOKEN_TILE) * N_EXPERTS * (SUBLANES - 1)
    return -(-worst // MOE_TILE) * MOE_TILE


def _local_pos(e_k, r_k, delta_ref, tile):
    shift = jnp.zeros_like(r_k)
    for e in range(N_EXPERTS):
        shift = jnp.where(e_k == e, delta_ref[tile * N_EXPERTS + e], shift)
    return r_k + shift


def _for_each_group(tile, lstart_ref, run_ref, gstart_ref, fn):
    shift = int(math.log2(SUBLANES))
    for e in range(N_EXPERTS):
        k = tile * N_EXPERTS + e
        l_start = lstart_ref[k]
        g_start = gstart_ref[k]

        def body(g, c, l_start=l_start, g_start=g_start):
            off = g * SUBLANES
            fn(pl.multiple_of(l_start + off, SUBLANES), pl.multiple_of(g_start + off, SUBLANES))
            return c

        lax.fori_loop(0, run_ref[k] >> shift, body, 0)


def _dispatch_kernel(delta_ref, lstart_ref, run_ref, gstart_ref, hn_ref, meta_ref, xs_ref, sbuf_ref, zero_ref,
                     sem, *, n_tokens):
    j = pl.program_id(0)
    tr = hn_ref.shape[0]
    lrows = sbuf_ref.shape[0]

    @pl.when(j == 0)
    def _():
        zero_ref[...] = jnp.zeros_like(zero_ref)
        zr = zero_ref.shape[0]
        tail = [pltpu.make_async_copy(zero_ref, xs_ref.at[pl.ds(2 * n_tokens + k * zr, zr)], sem)
                for k in range((xs_ref.shape[0] - 2 * n_tokens) // zr)]
        for cp in tail:
            cp.start()
        for cp in tail:
            cp.wait()

    meta = meta_ref[...]
    lp1 = _local_pos(meta[0:1], meta[2:3], delta_ref, j)
    lp2 = _local_pos(meta[1:2], meta[3:4], delta_ref, j)
    r_idx = lax.broadcasted_iota(jnp.int32, (lrows, tr), 0)
    onehot = jnp.logical_or(r_idx == lp1, r_idx == lp2).astype(BF16)
    sbuf_ref[...] = jnp.dot(onehot, hn_ref[...].astype(BF16), preferred_element_type=F32)

    def group_copy(local_row, xs_row):
        return pltpu.make_async_copy(sbuf_ref.at[pl.ds(local_row, SUBLANES)],
                                     xs_ref.at[pl.ds(xs_row, SUBLANES)], sem)

    _for_each_group(j, lstart_ref, run_ref, gstart_ref, lambda lr, xr: group_copy(lr, xr).start())
    _for_each_group(j, lstart_ref, run_ref, gstart_ref, lambda lr, xr: group_copy(lr, xr).wait())


def _dispatch(hn, meta_t, tables):
    n, d = hn.shape
    tr = TOKEN_TILE
    lrows = _local_rows(tr)
    return pl.pallas_call(
        functools.partial(_dispatch_kernel, n_tokens=n),
        grid_spec=pltpu.PrefetchScalarGridSpec(
            num_scalar_prefetch=4,
            grid=(n // tr,),
            in_specs=[pl.BlockSpec((tr, d), lambda j, *_: (j, 0)),
                      pl.BlockSpec((SUBLANES, tr), lambda j, *_: (0, j))],
            out_specs=pl.BlockSpec(memory_space=pl.ANY),
            scratch_shapes=[pltpu.VMEM((lrows, d), F32), pltpu.VMEM((MOE_SUB, d), F32),
                            pltpu.SemaphoreType.DMA(())]),
        out_shape=jax.ShapeDtypeStruct((_xs_rows(n), d), F32),
        compiler_params=_cparams("arbitrary"),
        name="moe_dispatch",
    )(*tables, hn, meta_t)


_ITEM_PAD, _ITEM_COMPUTE, _ITEM_ZERO = 0, 1, 2


def _moe_kernel(ti_ref, te_ref, tv_ref, lo_ref, hi_ref, first_ref, x_ref, wg_ref, wu_ref, wd_ref, o_ref, xb_ref):
    w = pl.program_id(0)
    f = pl.program_id(1)
    tm = x_ref.shape[0]
    sub = MOE_SUB
    sub_shift = int(math.log2(sub))

    def swiglu_part(xb, wg, wu, wd):
        hg = jnp.dot(xb, wg, preferred_element_type=F32)
        hu = jnp.dot(xb, wu, preferred_element_type=F32)
        act = (_silu(hg) * hu).astype(BF16)
        return jnp.dot(act, wd, preferred_element_type=F32)

    @pl.when(jnp.logical_and(tv_ref[w] == _ITEM_ZERO, f == 0))
    def _():
        o_ref[...] = jnp.zeros_like(o_ref)

    @pl.when(tv_ref[w] == _ITEM_COMPUTE)
    def _():
        lo = lo_ref[w]
        hi = hi_ref[w]
        whole = jnp.logical_and(lo == 0, hi == tm)

        @pl.when(f == 0)
        def _():
            row = lax.broadcasted_iota(jnp.int32, (tm, 1), 0)
            mine = jnp.logical_and(row >= lo, row < hi)
            xb_ref[...] = jnp.where(mine, x_ref[...], 0.0).astype(BF16)

        @pl.when(whole)
        def _():
            part = swiglu_part(xb_ref[...], wg_ref[0].astype(BF16), wu_ref[0].astype(BF16),
                               wd_ref[0].astype(BF16))

            @pl.when(f == 0)
            def _():
                o_ref[...] = part

            @pl.when(f != 0)
            def _():
                o_ref[...] += part

        @pl.when(jnp.logical_not(whole))
        def _():
            @pl.when(jnp.logical_and(f == 0, first_ref[w] == 1))
            def _():
                o_ref[...] = jnp.zeros_like(o_ref)

            def sub_block(s, carry):
                rows = pl.ds(pl.multiple_of(s * sub, sub), sub)
                o_ref[rows, :] += swiglu_part(xb_ref[rows, :], wg_ref[0].astype(BF16),
                                              wu_ref[0].astype(BF16), wd_ref[0].astype(BF16))
                return carry

            lax.fori_loop(lo >> sub_shift, (hi + sub - 1) >> sub_shift, sub_block, 0)


def _moe_ffn(xs, items, wg, wu, wd):
    rows, d = xs.shape
    tm = MOE_TILE
    tf = MOE_FF_TILE
    nf = wg.shape[2] // tf
    n_items = items[0].shape[0]
    def f_idx(f, kind):
        v = (kind == _ITEM_COMPUTE).astype(jnp.int32)
        return f * v + (nf - 1) * (1 - v)

    return pl.pallas_call(
        _moe_kernel,
        grid_spec=pltpu.PrefetchScalarGridSpec(
            num_scalar_prefetch=6,
            grid=(n_items, nf),
            in_specs=[pl.BlockSpec((tm, d), lambda w, f, ti, te, tv, *_: (ti[w], 0)),
                      pl.BlockSpec((1, d, tf), lambda w, f, ti, te, tv, *_: (te[w], 0, f_idx(f, tv[w]))),
                      pl.BlockSpec((1, d, tf), lambda w, f, ti, te, tv, *_: (te[w], 0, f_idx(f, tv[w]))),
                      pl.BlockSpec((1, tf, d), lambda w, f, ti, te, tv, *_: (te[w], f_idx(f, tv[w]), 0))],
            out_specs=pl.BlockSpec((tm, d), lambda w, f, ti, te, tv, *_: (ti[w], 0)),
            scratch_shapes=[pltpu.VMEM((tm, d), BF16)]),
        out_shape=jax.ShapeDtypeStruct((rows, d), F32),
        compiler_params=_cparams("arbitrary", "arbitrary"),
        name="moe_expert_swiglu",
    )(*items, xs, wg, wu, wd)


def _combine_kernel(delta_ref, lstart_ref, run_ref, gstart_ref, ys_ref, x_ref, meta_ref, wt_ref, g2_ref, fw_ref,
                    o_ref, ybuf_ref, sem):
    j = pl.program_id(0)
    n_tiles = pl.num_programs(0)
    tr = x_ref.shape[0]
    lrows = ybuf_ref.shape[1]
    slot = lax.rem(j, 2)

    def group_copy(buf, local_row, xs_row):
        return pltpu.make_async_copy(ys_ref.at[pl.ds(xs_row, SUBLANES)],
                                     ybuf_ref.at[buf, pl.ds(local_row, SUBLANES)], sem.at[buf])

    def fetch(tile, buf):
        ybuf_ref[buf, 2 * tr:, :] = jnp.zeros((lrows - 2 * tr, ybuf_ref.shape[2]), F32)
        _for_each_group(tile, lstart_ref, run_ref, gstart_ref,
                        lambda lr, xr: group_copy(buf, lr, xr).start())

    @pl.when(j == 0)
    def _():
        fetch(0, 0)

    _for_each_group(j, lstart_ref, run_ref, gstart_ref, lambda lr, xr: group_copy(slot, lr, xr).wait())

    @pl.when(j + 1 < n_tiles)
    def _():
        fetch(j + 1, 1 - slot)

    meta = meta_ref[...]
    wt = wt_ref[...]
    lp1 = _local_pos(meta[:, 0:1], meta[:, 2:3], delta_ref, j)
    lp2 = _local_pos(meta[:, 1:2], meta[:, 3:4], delta_ref, j)
    l_idx = lax.broadcasted_iota(jnp.int32, (tr, lrows), 1)
    pick = jnp.where(l_idx == lp1, wt[:, 0:1], 0.0) + jnp.where(l_idx == lp2, wt[:, 1:2], 0.0)
    ffn = _bdot(pick, ybuf_ref[slot])
    x4 = x_ref[...] + g2_ref[0] * ffn
    o_ref[...] = (x4 * _rms_scale(x4)) * fw_ref[...]


def _combine(ys, tables, x3, meta, wt, g2, final_w, seq_len):
    n, d = x3.shape
    tr = TOKEN_TILE
    tps = seq_len // tr
    lrows = -(-_local_rows(tr) // LANES) * LANES
    return pl.pallas_call(
        _combine_kernel,
        grid_spec=pltpu.PrefetchScalarGridSpec(
            num_scalar_prefetch=4,
            grid=(n // tr,),
            in_specs=[pl.BlockSpec(memory_space=pl.ANY),
                      pl.BlockSpec((tr, d), lambda j, *_: (j, 0)),
                      pl.BlockSpec((tr, LANES), lambda j, *_: (j, 0)),
                      pl.BlockSpec((tr, LANES), lambda j, *_: (j, 0)),
                      pl.BlockSpec((1, 1, d), lambda j, *_: (j // tps, 0, 0)),
                      pl.BlockSpec((1, d), lambda j, *_: (0, 0))],
            out_specs=pl.BlockSpec((tr, d), lambda j, *_: (j, 0)),
            scratch_shapes=[pltpu.VMEM((2, lrows, d), F32), pltpu.SemaphoreType.DMA((2,))]),
        out_shape=jax.ShapeDtypeStruct((n, d), F32),
        compiler_params=_cparams("arbitrary"),
        name="moe_combine_final_norm",
    )(*tables, ys, x3, meta, wt, g2, final_w.reshape(1, d))


def _moe_tables(tile_base, counts, n_tokens):
    i32 = lambda t: t.astype(jnp.int32)
    tr = TOKEN_TILE
    tm = MOE_TILE
    before = tile_base[:, 0, :N_EXPERTS]
    total = counts[0, :N_EXPERTS]
    run = jnp.concatenate([before[1:], total[None]], axis=0) - before
    run = (run + SUBLANES - 1) // SUBLANES * SUBLANES
    l_end = jnp.cumsum(run, axis=1)
    l_start = l_end - run
    g_size = jnp.sum(run, axis=0)
    g_end = jnp.cumsum(g_size)
    g_off = g_end - g_size
    g_start = g_off[None, :] + jnp.cumsum(run, axis=0) - run
    delta = l_start - before

    xs_tiles = _xs_rows(n_tokens) // tm
    n_items = xs_tiles + N_EXPERTS - 1
    first_tile = g_off // tm
    per_e = jnp.where(g_size > 0, (g_end - 1) // tm - first_tile + 1, 0)
    item_end = jnp.cumsum(per_e)
    item_start = item_end - per_e
    n_real = item_end[-1]
    used_tiles = (g_end[-1] + tm - 1) // tm
    w = jnp.arange(n_items, dtype=jnp.int32)
    w_real = jnp.minimum(w, n_real - 1)
    te = jnp.minimum(jnp.sum((item_end[None, :] <= w_real[:, None]).astype(jnp.int32), axis=1),
                     N_EXPERTS - 1)
    ti_real = jnp.take(first_tile, te) + (w_real - jnp.take(item_start, te))
    is_real = w < n_real
    is_zero = jnp.logical_and(w >= n_real, w < n_real + (xs_tiles - used_tiles))
    ti = jnp.where(is_real, ti_real, jnp.minimum(used_tiles + (w - n_real), xs_tiles - 1))
    kind = jnp.where(is_real, _ITEM_COMPUTE, jnp.where(is_zero, _ITEM_ZERO, _ITEM_PAD))
    lo = jnp.where(is_real, jnp.clip(jnp.take(g_off, te) - ti * tm, 0, tm), 0)
    hi = jnp.where(is_real, jnp.clip(jnp.take(g_end, te) - ti * tm, 0, tm), 0)
    first = jnp.concatenate([jnp.ones((1,), jnp.int32), i32(ti[1:] != ti[:-1])])
    items = (i32(ti), i32(te), i32(kind), i32(lo), i32(hi), first)
    flat = lambda t: i32(t).reshape(-1)
    return (flat(delta), flat(l_start), flat(run), flat(g_start)), items


def kernel(x, c, rel_bias, ada_w, ada_b, norm_mix_w, norm_ffn_w, final_norm_w, ab_w_in, attn_sinks,
           dn_conv_w, dn_a_log, dn_dt_bias, dn_norm_w, ab_w_out, ffn_w_gate, ffn_w_up, ffn_w_down,
           cd_w_in, lru_conv_w, lru_conv_b, lru_gate_a_w, lru_gate_a_b, lru_gate_x_w, lru_gate_x_b,
           lru_lambda, sconv_w, cd_w_out, moe_router_w, moe_router_b, moe_w_gate, moe_w_up, moe_w_down):
    bsz, seq_len, d = x.shape
    n = bsz * seq_len
    x2d = x.reshape(n, d)
    mods = _ada_mods(c, ada_w, ada_b)

    sh1, sc1, g1, sh2, sc2, g2 = (mods[0, k] for k in range(6))
    qa, kd, vd, qn, kn, vb, gs, bexp, gcexp = _in_proj0(
        x2d, norm_mix_w[0], sc1, sh1, ab_w_in[0], dn_conv_w[0], dn_a_log[0], dn_dt_bias[0], seq_len)
    attn = _attention(qa, kd, vd, _bias_table(rel_bias), attn_sinks[0], seq_len)
    dn = _deltanet(qn, kn, vb, gs, bexp, gcexp, dn_norm_w[0], seq_len)
    x2 = _mid0(attn, dn, x2d, ab_w_out[0], g1, norm_ffn_w[0], sc2, sh2, g2,
               ffn_w_gate[0], ffn_w_up[0], ffn_w_down[0], seq_len)

    sh1, sc1, g1, sh2, sc2, g2 = (mods[1, k] for k in range(6))
    cat = _mix1(x2, norm_mix_w[1], sc1, sh1, cd_w_in[0], lru_conv_w[0], lru_conv_b[0],
                lru_gate_a_w[0], lru_gate_a_b[0], lru_gate_x_w[0], lru_gate_x_b[0],
                lru_lambda[0], sconv_w[0], seq_len)
    x3, hn4, meta_t, meta, wt, tile_base, counts = _route(
        cat, x2, cd_w_out[0], g1, norm_ffn_w[1], sc2, sh2, moe_router_w[0], moe_router_b[0], seq_len)
    tables, items = _moe_tables(tile_base, counts, n)
    xs = _dispatch(hn4, meta_t, tables)
    ys = _moe_ffn(xs, items, moe_w_gate[0], moe_w_up[0], moe_w_down[0])
    out = _combine(ys, tables, x3, meta, wt, g2, final_norm_w, seq_len)
    return out.reshape(bsz, seq_len, d)
```

```python
import functools
import math

import numpy as np
import jax
import jax.numpy as jnp
from jax import lax
from jax.experimental import pallas as pl
from jax.experimental.pallas import tpu as pltpu

D_MODEL = 1024
EPS = 1e-6
HEAD_DIM = 64
A_Q_HEADS = 8
A_KV_HEADS = 2
WINDOW = 128
N_BUCKETS = 32
MAX_DISTANCE = 128
B_HEADS = 8
B_CONV = 4
CHUNK = 64
A_Q_W = A_Q_HEADS * HEAD_DIM
A_KV_W = A_KV_HEADS * HEAD_DIM
B_W = B_HEADS * HEAD_DIM
B_QKV_W = 3 * B_W
LRU_WIDTH = D_MODEL
LRU_BLOCKS = 8
LRU_C = 8.0
SC_WIDTH = D_MODEL // 2
D_FF = 2816
N_EXPERTS = 8
D_FF_EXPERT = 3584

LANES = 128
SUBLANES = 8
VMEM_LIMIT_BYTES = 56 * 1024 * 1024
TOKEN_TILE = 512
MOE_TILE = 1024
MOE_SUB = 256
MOE_FF_TILE = 512
NEG_BIG = -1e30

F32 = jnp.float32
BF16 = jnp.bfloat16


def _cparams(*sem):
    return pltpu.CompilerParams(dimension_semantics=tuple(sem), vmem_limit_bytes=VMEM_LIMIT_BYTES)


def _const_spec(shape):
    nd = len(shape)
    return pl.BlockSpec(shape, lambda *_: (0,) * nd)


def _bdot(a, b):
    return jnp.dot(a.astype(BF16), b.astype(BF16), preferred_element_type=F32)


def _bdot_nt(a, b):
    return lax.dot_general(a.astype(BF16), b.astype(BF16), (((1,), (1,)), ((), ())),
                           preferred_element_type=F32)


def _bdot_tn(a, b):
    return lax.dot_general(a.astype(BF16), b.astype(BF16), (((0,), (0,)), ((), ())),
                           preferred_element_type=F32)


def _split(x, n):
    parts = []
    r = x
    for i in range(n):
        p = r.astype(BF16)
        parts.append(p)
        if i + 1 < n:
            r = r - p.astype(F32)
    return parts


def _dot_x(a, b, na=2, nb=2):
    asp = _split(a, na) if na > 1 else [a.astype(BF16)]
    bsp = _split(b, nb) if nb > 1 else [b.astype(BF16)]
    acc = None
    for i, ai in enumerate(asp):
        for j, bj in enumerate(bsp):
            if i + j >= max(na, nb):
                continue
            t = jnp.dot(ai, bj, preferred_element_type=F32)
            acc = t if acc is None else acc + t
    return acc


def _silu(x):
    return x * (1.0 / (1.0 + jnp.exp(-x)))


def _sigmoid(x):
    return 1.0 / (1.0 + jnp.exp(-x))


def _log1p(z):
    u = 1.0 + z
    tiny = u == 1.0
    return jnp.where(tiny, z, jnp.log(u) * (z / jnp.where(tiny, 1.0, u - 1.0)))


def _softplus(x):
    return jnp.maximum(x, 0.0) + _log1p(jnp.exp(-jnp.abs(x)))


def _neg_expm1(y):
    return -jnp.tanh(0.5 * y) * (jnp.exp(y) + 1.0)


def _rms_scale(x):
    width = x.shape[1]
    mean_w = jnp.full((width, LANES), 1.0 / width, BF16)
    ms = _dot_x(x * x, mean_w, 2, 1)
    r = lax.rsqrt(ms + EPS)
    return jnp.concatenate([r] * (width // LANES), axis=1)


def _norm_mod(x, w, sc, sh, on_mxu=False):
    if on_mxu:
        scale = _rms_scale(x)
    else:
        scale = lax.rsqrt(jnp.mean(x * x, axis=-1, keepdims=True) + EPS)
    return (x * scale) * w * (1.0 + sc) + sh


def _shift_rows(x, k, prev_tail):
    n, width = x.shape
    x3 = x.reshape(n // SUBLANES, SUBLANES, width)
    rot = pltpu.roll(x3, k, 1)
    rot_prev = jnp.concatenate([pltpu.roll(prev_tail, k, 0)[None], rot[:-1]], axis=0)
    sub = lax.broadcasted_iota(jnp.int32, x3.shape, 1)
    return jnp.where(sub >= k, rot, rot_prev).reshape(n, width)


def _ada_kernel(c_ref, w_ref, b_ref, o_ref):
    c = c_ref[...]
    cond = _silu(c)
    o_ref[0] = _dot_x(cond, w_ref[0], 3, 3) + b_ref[0]


def _ada_mods(c, ada_w, ada_b):
    depth, d, six_d = ada_w.shape
    bsz = c.shape[0]
    rows = max(SUBLANES, bsz)
    c_pad = jnp.zeros((rows, d), F32).at[:bsz].set(c)
    tn = 1536
    out = pl.pallas_call(
        _ada_kernel,
        grid=(depth, six_d // tn),
        in_specs=[pl.BlockSpec((rows, d), lambda l, j: (0, 0)),
                  pl.BlockSpec((1, d, tn), lambda l, j: (l, 0, j)),
                  pl.BlockSpec((1, 1, tn), lambda l, j: (l, 0, j))],
        out_specs=pl.BlockSpec((1, rows, tn), lambda l, j: (l, 0, j)),
        out_shape=jax.ShapeDtypeStruct((depth, rows, six_d), F32),
        compiler_params=_cparams("parallel", "parallel"),
        name="ada_mods",
    )(c_pad, ada_w, ada_b.reshape(depth, 1, six_d))
    return out[:, :bsz].reshape(depth, bsz, 6, 1, d).transpose(0, 2, 1, 3, 4)


def _t5_bucket(dist):
    max_exact = N_BUCKETS // 2
    d = np.maximum(dist, 0)
    large = max_exact + (np.log(np.maximum(d, 1) / max_exact) / math.log(MAX_DISTANCE / max_exact)
                         * (N_BUCKETS - max_exact)).astype(np.int32)
    large = np.minimum(large, N_BUCKETS - 1)
    return np.where(d < max_exact, d, large).astype(np.int32)


def _band_buckets():
    qi = np.arange(WINDOW)[:, None]
    s = np.arange(2 * WINDOW)[None, :]
    dist = qi + WINDOW - s
    in_window = (dist >= 0) & (dist < WINDOW)
    return np.where(in_window, _t5_bucket(dist), -1).astype(np.int32)


def _bias_kernel(rb_ref, bucket_ref, o_ref):
    h = pl.program_id(0)
    bucket = bucket_ref[...]
    acc = jnp.zeros(bucket.shape, F32)
    for b in range(N_BUCKETS):
        acc = jnp.where(bucket == b, rb_ref[b, h], acc)
    o_ref[0] = jnp.where(bucket < 0, NEG_BIG, acc)


def _bias_table(rel_bias):
    bucket = jnp.asarray(_band_buckets())
    out = pl.pallas_call(
        _bias_kernel,
        grid=(A_Q_HEADS,),
        in_specs=[pl.BlockSpec(memory_space=pltpu.SMEM),
                  _const_spec((WINDOW, 2 * WINDOW))],
        out_specs=pl.BlockSpec((1, WINDOW, 2 * WINDOW), lambda h: (h, 0, 0)),
        out_shape=jax.ShapeDtypeStruct((A_Q_HEADS, WINDOW, 2 * WINDOW), F32),
        compiler_params=_cparams("parallel"),
        name="attn_bias_table",
    )(rel_bias, bucket)
    return out.reshape(A_Q_HEADS // 2, 2 * WINDOW, 2 * WINDOW)


_C_QA = 0
_C_KA = _C_QA + A_Q_W
_C_VA = _C_KA + A_KV_W
_C_QKV = _C_VA + A_KV_W
_C_GATE = _C_QKV + B_QKV_W
_C_SMALL = _C_GATE + B_W
_AB_COLS = _C_SMALL + LANES


def _ab_in_weight(w_in):
    return jnp.pad(w_in, ((0, 0), (0, _AB_COLS - w_in.shape[1]))).astype(BF16)


def _dup_heads(t, low):
    swapped = pltpu.roll(t, HEAD_DIM, 1)
    return jnp.concatenate([jnp.where(low, t, swapped), jnp.where(low, swapped, t)], axis=1)


def _chunk_tril(tm):
    r = np.arange(tm)
    return ((r[:, None] >= r[None, :]) & (r[:, None] // CHUNK == r[None, :] // CHUNK)).astype(np.float32)


def _head_selector():
    e = np.zeros((B_W, LANES), np.float32)
    for h in range(B_HEADS):
        e[h * HEAD_DIM:(h + 1) * HEAD_DIM, h] = 1.0
    return e


def _in0_kernel(x_ref, nw_ref, sc_ref, sh_ref, w_ref, cw_ref, sel_ref, selt_ref, tril_ref, alog_ref, dtb_ref,
                qa_ref, kd_ref, vd_ref, qn_ref, kn_ref, vb_ref, gs_ref, bexp_ref, gcexp_ref,
                tail_ref, *, tiles_per_seq):
    i = pl.program_id(0)

    @pl.when(i % tiles_per_seq == 0)
    def _():
        tail_ref[...] = jnp.zeros_like(tail_ref)

    hn = _norm_mod(x_ref[...], nw_ref[...], sc_ref[0], sh_ref[0])
    proj = jnp.dot(hn.astype(BF16), w_ref[...], preferred_element_type=F32)
    low = lax.broadcasted_iota(jnp.int32, (proj.shape[0], LANES), 1) < HEAD_DIM
    qa_ref[...] = proj[:, _C_QA:_C_KA].astype(BF16)
    kd_ref[...] = _dup_heads(proj[:, _C_KA:_C_VA], low).astype(BF16)
    vd_ref[...] = _dup_heads(proj[:, _C_VA:_C_QKV], low).astype(BF16)

    xq = proj[:, _C_QKV:_C_GATE]
    tail = tail_ref[...]
    cw = cw_ref[...]
    y = xq * cw[B_CONV - 1:B_CONV]
    for k in range(1, B_CONV):
        y = y + _shift_rows(xq, k, tail) * cw[B_CONV - 1 - k:B_CONV - k]
    tail_ref[...] = xq[xq.shape[0] - SUBLANES:]
    y = _silu(y)
    q, k_, v = y[:, :B_W], y[:, B_W:2 * B_W], y[:, 2 * B_W:]

    def l2n(t):
        ssq = _dot_x(t * t, sel_ref[...], 2, 1)
        r = lax.rsqrt(ssq + EPS)
        return t * _dot_x(r, selt_ref[...], 2, 1)

    qn_ref[...] = l2n(q) * (HEAD_DIM ** -0.5)
    kn_ref[...] = l2n(k_)
    vb_ref[...] = v
    gs_ref[...] = _silu(proj[:, _C_GATE:_C_SMALL])
    small = proj[:, _C_SMALL:]
    lane = lax.broadcasted_iota(jnp.int32, small.shape, 1)
    beta = jnp.where(lane < B_HEADS, _sigmoid(small), 0.0)
    dec = pltpu.roll(small, LANES - B_HEADS, 1)
    g = jnp.where(lane < B_HEADS, -jnp.exp(alog_ref[...]) * _softplus(dec + dtb_ref[...]), 0.0)
    bexp_ref[...] = _dot_x(beta, selt_ref[...], 2, 1)
    gc = _dot_x(tril_ref[...], g, 1, 3)
    gcexp_ref[...] = _dot_x(gc, selt_ref[...], 3, 1)


def _in_proj0(x2d, nw, sc, sh, w_in, conv_w, a_log, dt_bias, seq_len):
    n, d = x2d.shape
    tm = TOKEN_TILE
    tiles_per_seq = seq_len // tm
    w = _ab_in_weight(w_in)
    sel = jnp.asarray(_head_selector(), BF16)
    selt = jnp.asarray(_head_selector().T.copy(), BF16)
    tril = jnp.asarray(_chunk_tril(tm), BF16)
    pad8 = lambda v: jnp.zeros((1, LANES), F32).at[0, :B_HEADS].set(v)
    row = lambda width: pl.BlockSpec((tm, width), lambda i: (i, 0))
    per_b = pl.BlockSpec((1, 1, d), lambda i: (i // tiles_per_seq, 0, 0))
    outs = pl.pallas_call(
        functools.partial(_in0_kernel, tiles_per_seq=tiles_per_seq),
        grid=(n // tm,),
        in_specs=[row(d), _const_spec((1, d)), per_b, per_b,
                  _resident_spec((d, _AB_COLS)), _const_spec((B_CONV, B_QKV_W)),
                  _const_spec((B_W, LANES)), _const_spec((LANES, B_W)), _const_spec((tm, tm)),
                  _const_spec((1, LANES)), _const_spec((1, LANES))],
        out_specs=[row(A_Q_W), row(2 * A_KV_W), row(2 * A_KV_W)] + [row(B_W)] * 6,
        out_shape=[jax.ShapeDtypeStruct((n, A_Q_W), BF16),
                   jax.ShapeDtypeStruct((n, 2 * A_KV_W), BF16),
                   jax.ShapeDtypeStruct((n, 2 * A_KV_W), BF16)]
        + [jax.ShapeDtypeStruct((n, B_W), F32)] * 6,
        scratch_shapes=[pltpu.VMEM((SUBLANES, B_QKV_W), F32)],
        compiler_params=_cparams("arbitrary"),
        name="in_proj0",
    )(x2d, nw.reshape(1, d), sc, sh, w, conv_w, sel, selt, tril, pad8(a_log), pad8(dt_bias))
    return outs


def _attn_kernel(sink_ref, q_ref, kp_ref, kc_ref, vp_ref, vc_ref, bm_ref, o_ref, *, blocks_per_seq):
    i = pl.program_id(0)
    first = (i % blocks_per_seq) == 0
    w = WINDOW
    lane = lax.broadcasted_iota(jnp.int32, (w, LANES), 1)
    low = lane < HEAD_DIM
    col = lax.broadcasted_iota(jnp.int32, (2 * w, 2 * w), 1)
    row = lax.broadcasted_iota(jnp.int32, (2 * w, 1), 0)
    prev_dead = jnp.logical_and(first, col < w)
    q_all = q_ref[...]
    zero = jnp.zeros((), q_all.dtype)
    pairs = range(A_Q_HEADS // 2)
    kv_of = [(2 * j) // (A_Q_HEADS // A_KV_HEADS) for j in pairs]
    qs = [jnp.concatenate([jnp.where(low, q_all[:, j * LANES:(j + 1) * LANES], zero),
                           jnp.where(low, zero, q_all[:, j * LANES:(j + 1) * LANES])], axis=0)
          for j in pairs]
    kd = [jnp.concatenate([kp_ref[:, kh * LANES:(kh + 1) * LANES],
                           kc_ref[:, kh * LANES:(kh + 1) * LANES]], axis=0) for kh in kv_of]
    vd = [jnp.concatenate([vp_ref[:, kh * LANES:(kh + 1) * LANES],
                           vc_ref[:, kh * LANES:(kh + 1) * LANES]], axis=0) for kh in kv_of]
    s = [lax.dot_general(qs[j], kd[j], (((1,), (1,)), ((), ())), preferred_element_type=F32) for j in pairs]
    s = [jnp.where(prev_dead, NEG_BIG, s[j] * (HEAD_DIM ** -0.5) + bm_ref[j]) for j in pairs]
    sink = [jnp.where(row < w, sink_ref[2 * j], sink_ref[2 * j + 1]) for j in pairs]
    m = [jnp.maximum(jnp.max(s[j], axis=-1, keepdims=True), sink[j]) for j in pairs]
    p = [jnp.exp(s[j] - m[j]) for j in pairs]
    denom = [jnp.sum(p[j], axis=-1, keepdims=True) + jnp.exp(sink[j] - m[j]) for j in pairs]
    pv = [jnp.dot(p[j].astype(BF16), vd[j], preferred_element_type=F32) / denom[j] for j in pairs]
    outs = [jnp.where(low, pv[j][:w], pv[j][w:]) for j in pairs]
    o_ref[...] = jnp.concatenate(outs, axis=1).astype(o_ref.dtype)


def _attention(qa, kd, vd, bias_tbl, sinks, seq_len):
    n = qa.shape[0]
    w = WINDOW
    nb = seq_len // w
    cur = lambda i: (i, 0)
    prev = lambda i: (jnp.where(i % nb == 0, i, i - 1), 0)
    return pl.pallas_call(
        functools.partial(_attn_kernel, blocks_per_seq=nb),
        grid=(n // w,),
        in_specs=[pl.BlockSpec(memory_space=pltpu.SMEM),
                  pl.BlockSpec((w, A_Q_W), cur),
                  pl.BlockSpec((w, 2 * A_KV_W), prev), pl.BlockSpec((w, 2 * A_KV_W), cur),
                  pl.BlockSpec((w, 2 * A_KV_W), prev), pl.BlockSpec((w, 2 * A_KV_W), cur),
                  _const_spec((A_Q_HEADS // 2, 2 * w, 2 * w))],
        out_specs=pl.BlockSpec((w, A_Q_W), cur),
        out_shape=jax.ShapeDtypeStruct((n, A_Q_W), BF16),
        compiler_params=_cparams("parallel"),
        name="swa_attention",
    )(sinks, qa, kd, kd, vd, vd, bias_tbl)


_DN_PAIRS = B_HEADS // 2
_DN_INV_BLOCK = 16
_DN_GROUP = 4


def _block_diag(x, low):
    zero = jnp.zeros((), x.dtype)
    return jnp.concatenate([jnp.where(low, x, zero), jnp.where(low, zero, x)], axis=0)


def _dn_intra(chunks, data_refs, work_refs, consts):
    qn_ref, kn_ref, vb_ref, bexp_ref, gcexp_ref = data_refs
    u_ref, w_ref, qk_ref, qd_ref, kd_ref, egl_ref = work_refs
    low, i_idx, j_idx, ones3 = consts
    c = CHUNK
    units = [(ci, p) for ci in chunks for p in range(_DN_PAIRS)]
    where = [(slice(ci * c, (ci + 1) * c), slice(p * LANES, (p + 1) * LANES)) for ci, p in units]
    causal = i_idx >= j_idx
    strict = i_idx > j_idx
    on_diag = i_idx == j_idx
    eye = on_diag.astype(F32)
    blk_shift = int(math.log2(_DN_INV_BLOCK))
    same_blk = (i_idx >> blk_shift) == (j_idx >> blk_shift)

    q = [qn_ref[rs, ls] for rs, ls in where]
    k = [kn_ref[rs, ls] for rs, ls in where]
    v = [vb_ref[rs, ls] for rs, ls in where]
    b = [bexp_ref[rs, ls] for rs, ls in where]
    gc = [gcexp_ref[rs, ls] for rs, ls in where]

    gr = [jnp.dot(ones3, jnp.concatenate(_split(jnp.where(on_diag, t, 0.0), 3), axis=0),
                  preferred_element_type=F32) for t in gc]
    ks = [_block_diag(t.astype(BF16), low) for t in k]
    qkk = [lax.dot_general(jnp.concatenate([qt, kt], axis=0).astype(BF16), kst,
                           (((1,), (1,)), ((), ())), preferred_element_type=F32)
           for qt, kt, kst in zip(q, k, ks)]
    decay = [jnp.exp(jnp.where(causal, gct - grt, NEG_BIG)) for gct, grt in zip(gc, gr)]
    lmat = [jnp.where(strict, bt * t[c:] * dt, 0.0) for bt, t, dt in zip(b, qkk, decay)]
    qk = [jnp.where(causal, t[:c] * dt, 0.0) for t, dt in zip(qkk, decay)]

    def mm(xs, ys):
        return [_bdot(x, _block_diag(y.astype(BF16), low)) for x, y in zip(xs, ys)]

    l_diag = [jnp.where(same_blk, t, 0.0) for t in lmat]
    l_off = [t - d for t, d in zip(lmat, l_diag)]
    pw = [-t for t in l_diag]
    d_inv = [eye + t for t in pw]
    for _ in range(blk_shift - 1):
        pw = mm(pw, pw)
        d_inv = mm(d_inv, [eye + t for t in pw])
    pw = [-t for t in mm(d_inv, l_off)]
    acc = [eye + t for t in pw]
    for _ in range(int(math.log2(c // _DN_INV_BLOCK)) - 1):
        pw = mm(pw, pw)
        acc = mm(acc, [eye + t for t in pw])
    tmat = mm(acc, d_inv)

    egc = [jnp.exp(t) for t in gc]
    rhs = [jnp.concatenate([_block_diag((vt * bt).astype(BF16), low),
                            _block_diag((kt * (bt * et)).astype(BF16), low)], axis=1)
           for vt, kt, bt, et in zip(v, k, b, egc)]
    uw = [_bdot(t, r) for t, r in zip(tmat, rhs)]
    for n, (ci, p) in enumerate(units):
        g_last = gc[n][c - 1:c, :]
        u_ref[ci, p] = uw[n][:, :LANES]
        w_ref[ci, p] = uw[n][:, LANES:]
        qk_ref[ci, p] = qk[n]
        qd_ref[ci, p] = q[n] * egc[n]
        kd_ref[ci, p] = k[n] * jnp.exp(g_last - gc[n])
        egl_ref[ci, p] = jnp.broadcast_to(jnp.exp(g_last), (SUBLANES, LANES))


def _dn_scan(ci, work_refs, s_ref, gs_ref, nw, o_ref, consts):
    u_ref, w_ref, qk_ref, qd_ref, kd_ref, egl_ref = work_refs
    low, mask_bd, head_mean2 = consts
    c = CHUNK
    rows = slice(ci * c, (ci + 1) * c)
    pairs = range(_DN_PAIRS)
    s_old = [s_ref[p] for p in pairs]
    wq = [_bdot(jnp.concatenate([w_ref[ci, p], qd_ref[ci, p]], axis=0), s_old[p]) for p in pairs]
    v_new = [u_ref[ci, p] - wq[p][:c] for p in pairs]
    o = [wq[p][c:] + _bdot(qk_ref[ci, p], _block_diag(v_new[p].astype(BF16), low)) for p in pairs]
    kv = [_bdot_tn(kd_ref[ci, p], v_new[p]) for p in pairs]
    for p in pairs:
        s_ref[p] = s_old[p] * egl_ref[ci, p][0:1, :] + jnp.where(mask_bd, kv[p], 0.0)
    ms = [jnp.dot(jnp.concatenate(_split(t * t, 2), axis=1), head_mean2, preferred_element_type=F32)
          for t in o]
    for p in pairs:
        ls = slice(p * LANES, (p + 1) * LANES)
        y = (o[p] * lax.rsqrt(ms[p] + EPS)) * nw * gs_ref[rows, ls]
        o_ref[rows, ls] = y.astype(o_ref.dtype)


def _dn_kernel(qn_ref, kn_ref, vb_ref, gs_ref, bexp_ref, gcexp_ref, nw_ref, o_ref,
               s_ref, u_ref, w_ref, qk_ref, qd_ref, kd_ref, egl_ref, *, groups_per_seq):
    i = pl.program_id(0)

    @pl.when(i % groups_per_seq == 0)
    def _():
        s_ref[...] = jnp.zeros_like(s_ref)

    c = CHUNK
    tm = o_ref.shape[0]
    n_chunks = tm // c
    lane = lax.broadcasted_iota(jnp.int32, (c, LANES), 1)
    low = lane < HEAD_DIM
    i_idx = lax.broadcasted_iota(jnp.int32, (c, LANES), 0)
    j_idx = lane & (c - 1)
    ones3 = jnp.ones((c, 3 * c), BF16)
    rb = lax.broadcasted_iota(jnp.int32, (LANES, LANES), 0)
    cb = lax.broadcasted_iota(jnp.int32, (LANES, LANES), 1)
    mask_bd = (rb < HEAD_DIM) == (cb < HEAD_DIM)
    head_mean = jnp.where(mask_bd, 1.0 / HEAD_DIM, 0.0).astype(BF16)
    head_mean2 = jnp.concatenate([head_mean, head_mean], axis=0)
    data_refs = (qn_ref, kn_ref, vb_ref, bexp_ref, gcexp_ref)
    work_refs = (u_ref, w_ref, qk_ref, qd_ref, kd_ref, egl_ref)
    intra_consts = (low, i_idx, j_idx, ones3)
    scan_consts = (low, mask_bd, head_mean2)
    nw = nw_ref[...]

    groups = [list(range(s, s + _DN_GROUP)) for s in range(0, n_chunks, _DN_GROUP)]
    _dn_intra(groups[0], data_refs, work_refs, intra_consts)
    for j, grp in enumerate(groups):
        if j + 1 < len(groups):
            _dn_intra(groups[j + 1], data_refs, work_refs, intra_consts)
        for ci in grp:
            _dn_scan(ci, work_refs, s_ref, gs_ref, nw, o_ref, scan_consts)


def _deltanet(qn, kn, vb, gs, bexp, gcexp, norm_w, seq_len):
    n = qn.shape[0]
    tm = TOKEN_TILE
    nw2 = jnp.concatenate([norm_w, norm_w]).reshape(1, LANES)
    row = lambda width: pl.BlockSpec((tm, width), lambda i: (i, 0))
    return pl.pallas_call(
        functools.partial(_dn_kernel, groups_per_seq=seq_len // tm),
        grid=(n // tm,),
        in_specs=[row(B_W)] * 6 + [_const_spec((1, LANES))],
        out_specs=row(B_W),
        out_shape=jax.ShapeDtypeStruct((n, B_W), BF16),
        scratch_shapes=[pltpu.VMEM((_DN_PAIRS, LANES, LANES), F32)]
        + [pltpu.VMEM((tm // CHUNK, _DN_PAIRS, CHUNK, LANES), F32)] * 5
        + [pltpu.VMEM((tm // CHUNK, _DN_PAIRS, SUBLANES, LANES), F32)],
        compiler_params=_cparams("arbitrary"),
        name="gated_deltanet",
    )(qn, kn, vb, gs, bexp, gcexp, nw2)


def _resident_spec(shape):
    nd = len(shape)
    return pl.BlockSpec(shape, lambda *_: (0,) * nd, pipeline_mode=pl.Buffered(1))


def _mid0_kernel(attn_ref, dn_ref, x_ref, wo_ref, g1_ref, nw_ref, sc_ref, sh_ref, g2_ref,
                 wg_ref, wu_ref, wd_ref, o_ref):
    mix = (jnp.dot(attn_ref[...], wo_ref[:A_Q_W], preferred_element_type=F32)
           + jnp.dot(dn_ref[...], wo_ref[A_Q_W:], preferred_element_type=F32))
    x1 = x_ref[...] + g1_ref[0] * mix
    hn = _norm_mod(x1, nw_ref[...], sc_ref[0], sh_ref[0]).astype(BF16)
    hg = jnp.dot(hn, wg_ref[...], preferred_element_type=F32)
    hu = jnp.dot(hn, wu_ref[...], preferred_element_type=F32)
    act = (_silu(hg) * hu).astype(BF16)
    o_ref[...] = x1 + g2_ref[0] * jnp.dot(act, wd_ref[...], preferred_element_type=F32)


def _mid0(attn, dn, x2d, w_out, g1, nw, sc, sh, g2, wg, wu, wd, seq_len):
    n, d = x2d.shape
    tm = TOKEN_TILE
    tps = seq_len // tm
    row = lambda width: pl.BlockSpec((tm, width), lambda i: (i, 0))
    per_b = pl.BlockSpec((1, 1, d), lambda i: (i // tps, 0, 0))
    return pl.pallas_call(
        _mid0_kernel,
        grid=(n // tm,),
        in_specs=[row(A_Q_W), row(B_W), row(d), _resident_spec(w_out.shape), per_b,
                  _const_spec((1, d)), per_b, per_b, per_b,
                  _resident_spec(wg.shape), _resident_spec(wu.shape), _resident_spec(wd.shape)],
        out_specs=row(d),
        out_shape=jax.ShapeDtypeStruct((n, d), F32),
        compiler_params=_cparams("parallel"),
        name="out_proj0_swiglu",
    )(attn, dn, x2d, w_out.astype(BF16), g1, nw.reshape(1, d), sc, sh, g2,
      wg.astype(BF16), wu.astype(BF16), wd.astype(BF16))


def _gelu_tanh(x):
    return 0.5 * x * (1.0 + jnp.tanh(math.sqrt(2.0 / math.pi) * (x + 0.044715 * (x * x * x))))


def _linear_scan(a, b, h0):
    n, width = a.shape
    groups = n // SUBLANES
    a = a.reshape(groups, SUBLANES, width)
    b = b.reshape(groups, SUBLANES, width)
    in_group = lax.broadcasted_iota(jnp.int32, a.shape, 1)
    s = 1
    while s < SUBLANES:
        a_sh = pltpu.roll(a, s, 1)
        b_sh = pltpu.roll(b, s, 1)
        valid = in_group >= s
        b = jnp.where(valid, a * b_sh + b, b)
        a = jnp.where(valid, a * a_sh, a)
        s *= 2
    carry = jnp.broadcast_to(h0, (SUBLANES, width))
    out = []
    for g in range(groups):
        hg = a[g] * carry + b[g]
        out.append(hg)
        carry = jnp.broadcast_to(hg[SUBLANES - 1:SUBLANES, :], hg.shape)
    return jnp.concatenate(out, axis=0)


def _mix1_tile(x, nw, sc, sh, w_ref, cw_ref, cb_ref, ga_ref, gab_ref, gx_ref, gxb_ref, lam_ref, sw_ref,
               tail_c_ref, tail_d_ref, h_ref):
    hn = _norm_mod(x, nw, sc, sh, on_mxu=True).astype(BF16)
    proj = jnp.dot(hn, w_ref[...], preferred_element_type=F32)
    w_l = LRU_WIDTH
    xc_in = proj[:, :w_l]
    yc = proj[:, w_l:2 * w_l]
    bd = proj[:, 2 * w_l:2 * w_l + SC_WIDTH]
    cd = proj[:, 2 * w_l + SC_WIDTH:2 * w_l + 2 * SC_WIDTH]
    hd = proj[:, 2 * w_l + 2 * SC_WIDTH:]
    tm = xc_in.shape[0]

    kc = cw_ref.shape[0]
    tail = tail_c_ref[...]
    cw = cw_ref[...]
    xc = xc_in * cw[kc - 1:kc] + cb_ref[...]
    for k in range(1, kc):
        xc = xc + _shift_rows(xc_in, k, tail) * cw[kc - 1 - k:kc - k]
    tail_c_ref[...] = xc_in[tm - SUBLANES:]

    xb = xc.astype(BF16)
    gw = ga_ref.shape[1]
    ra, ri = [], []
    for p in range(ga_ref.shape[0]):
        xin = xb[:, p * gw:(p + 1) * gw]
        ra.append(jnp.dot(xin, ga_ref[p], preferred_element_type=F32))
        ri.append(jnp.dot(xin, gx_ref[p], preferred_element_type=F32))
    r = _sigmoid(jnp.concatenate(ra, axis=1) + gab_ref[...])
    ig = _sigmoid(jnp.concatenate(ri, axis=1) + gxb_ref[...])
    log_a = (-LRU_C) * r * _softplus(-lam_ref[...])
    a = jnp.exp(log_a)
    b = jnp.sqrt(_neg_expm1(2.0 * log_a)) * (ig * xc)
    h = _linear_scan(a, b, h_ref[0:1, :])
    h_ref[...] = jnp.broadcast_to(h[tm - 1:tm, :], h_ref.shape)
    yc_out = h * _gelu_tanh(yc)

    ks = sw_ref.shape[0]
    ch = cd * hd
    tail_d = tail_d_ref[...]
    sw = sw_ref[...]
    conv = ch * sw[ks - 1:ks]
    for k in range(1, ks):
        conv = conv + _shift_rows(ch, k, tail_d) * sw[ks - 1 - k:ks - k]
    tail_d_ref[...] = ch[tm - SUBLANES:]
    return jnp.concatenate([yc_out, bd * conv], axis=1)


def _pair_block_diag(gw):
    nb, bw, _ = gw.shape
    g2 = gw.reshape(nb // 2, 2, bw, bw)
    z = jnp.zeros((nb // 2, bw, bw), gw.dtype)
    top = jnp.concatenate([g2[:, 0], z], axis=2)
    bot = jnp.concatenate([z, g2[:, 1]], axis=2)
    return jnp.concatenate([top, bot], axis=1).astype(BF16)


def _route_tile(cat, x, wo_ref, g1, nw, sc, sh, rw_ref, rb_ref, carry_ref):
    x3 = x + g1 * jnp.dot(cat, wo_ref[...], preferred_element_type=F32)
    hn = _norm_mod(x3, nw, sc, sh)
    tm = hn.shape[0]
    lane = lax.broadcasted_iota(jnp.int32, (tm, LANES), 1)
    logits = _dot_x(hn, rw_ref[...], 2, 2) + rb_ref[...]
    lg = jnp.where(lane < N_EXPERTS, logits, NEG_BIG)
    m1 = jnp.max(lg, axis=1, keepdims=True)
    i1 = jnp.min(jnp.where(lg == m1, lane, LANES), axis=1, keepdims=True)
    lg2 = jnp.where(lane == i1, NEG_BIG, lg)
    m2 = jnp.max(lg2, axis=1, keepdims=True)
    i2 = jnp.min(jnp.where(lg2 == m2, lane, LANES), axis=1, keepdims=True)
    e2 = jnp.exp(m2 - m1)
    w1 = 1.0 / (1.0 + e2)
    w2 = e2 / (1.0 + e2)

    hit1 = lane == i1
    hit2 = lane == i2
    sel = jnp.logical_or(hit1, hit2).astype(F32)
    r_i = lax.broadcasted_iota(jnp.int32, (tm, tm), 0)
    c_i = lax.broadcasted_iota(jnp.int32, (tm, tm), 1)
    tril = (r_i >= c_i).astype(BF16)
    incl = jnp.dot(tril, sel.astype(BF16), preferred_element_type=F32)
    carry = carry_ref[0:1, :]
    excl = incl - sel + carry
    r1 = jnp.sum(jnp.where(hit1, excl, 0.0), axis=1, keepdims=True)
    r2 = jnp.sum(jnp.where(hit2, excl, 0.0), axis=1, keepdims=True)
    total = carry + incl[tm - 1:tm, :]
    carry_ref[...] = jnp.broadcast_to(total, carry_ref.shape)

    meta = jnp.where(lane == 0, i1, 0)
    meta = jnp.where(lane == 1, i2, meta)
    meta = jnp.where(lane == 2, r1.astype(jnp.int32), meta)
    meta = jnp.where(lane == 3, r2.astype(jnp.int32), meta)
    wt = jnp.where(lane == 0, w1, jnp.where(lane == 1, w2, 0.0))
    return x3, hn, meta, wt, carry, total


def _mix1_kernel(x_ref, nw_ref, sc_ref, sh_ref, w_ref, cw_ref, cb_ref, ga_ref, gab_ref, gx_ref, gxb_ref,
                 lam_ref, sw_ref, o_ref, tail_c_ref, tail_d_ref, h_ref, *, tiles_per_seq):
    i = pl.program_id(0)

    @pl.when(i % tiles_per_seq == 0)
    def _():
        tail_c_ref[...] = jnp.zeros_like(tail_c_ref)
        tail_d_ref[...] = jnp.zeros_like(tail_d_ref)
        h_ref[...] = jnp.zeros_like(h_ref)

    cat = _mix1_tile(x_ref[...], nw_ref[...], sc_ref[0], sh_ref[0], w_ref, cw_ref, cb_ref, ga_ref,
                     gab_ref, gx_ref, gxb_ref, lam_ref, sw_ref, tail_c_ref, tail_d_ref, h_ref)
    o_ref[...] = cat.astype(o_ref.dtype)


def _mix1(x2d, nw, sc, sh, w_in, conv_w, conv_b, ga_w, ga_b, gx_w, gx_b, lam, sconv_w, seq_len):
    n, d = x2d.shape
    tm = TOKEN_TILE
    tps = seq_len // tm
    cd_in = w_in.shape[1]
    cd_out = LRU_WIDTH + SC_WIDTH
    row = lambda width: pl.BlockSpec((tm, width), lambda i: (i, 0))
    per_b = pl.BlockSpec((1, 1, d), lambda i: (i // tps, 0, 0))
    ga = _pair_block_diag(ga_w)
    gx = _pair_block_diag(gx_w)
    vec = lambda v: v.reshape(1, -1)
    return pl.pallas_call(
        functools.partial(_mix1_kernel, tiles_per_seq=tps),
        grid=(n // tm,),
        in_specs=[row(d), _const_spec((1, d)), per_b, per_b, _resident_spec((d, cd_in)),
                  _const_spec(conv_w.shape), _const_spec((1, LRU_WIDTH)),
                  _const_spec(ga.shape), _const_spec((1, LRU_WIDTH)),
                  _const_spec(gx.shape), _const_spec((1, LRU_WIDTH)),
                  _const_spec((1, LRU_WIDTH)), _const_spec(sconv_w.shape)],
        out_specs=row(cd_out),
        out_shape=jax.ShapeDtypeStruct((n, cd_out), BF16),
        scratch_shapes=[pltpu.VMEM((SUBLANES, LRU_WIDTH), F32), pltpu.VMEM((SUBLANES, SC_WIDTH), F32),
                        pltpu.VMEM((SUBLANES, LRU_WIDTH), F32)],
        compiler_params=_cparams("arbitrary"),
        name="rglru_shortconv_mixer",
    )(x2d, vec(nw), sc, sh, w_in.astype(BF16), conv_w, vec(conv_b), ga, vec(ga_b), gx, vec(gx_b),
      vec(lam), sconv_w)


def _route_kernel(cat_ref, x_ref, wo_ref, g1_ref, nw_ref, sc_ref, sh_ref, rw_ref, rb_ref,
                  x3_ref, hn_ref, metat_ref, meta_ref, wt_ref, base_ref, cnt_ref, carry_ref):
    @pl.when(pl.program_id(0) == 0)
    def _():
        carry_ref[...] = jnp.zeros_like(carry_ref)

    x3, hn, meta, wt, before, total = _route_tile(cat_ref[...], x_ref[...], wo_ref, g1_ref[0], nw_ref[...],
                                                  sc_ref[0], sh_ref[0], rw_ref, rb_ref, carry_ref)
    x3_ref[...] = x3
    hn_ref[...] = hn
    meta_ref[...] = meta
    metat_ref[...] = jnp.transpose(meta.astype(F32))[:SUBLANES].astype(jnp.int32)
    wt_ref[...] = wt
    base_ref[0] = jnp.broadcast_to(before, base_ref.shape[1:]).astype(jnp.int32)
    cnt_ref[...] = jnp.broadcast_to(total, cnt_ref.shape).astype(jnp.int32)


def _route(cat, x2d, w_out, g1, nw, sc, sh, router_w, router_b, seq_len):
    n, d = x2d.shape
    tm = TOKEN_TILE
    tps = seq_len // tm
    row = lambda width: pl.BlockSpec((tm, width), lambda i: (i, 0))
    per_b = pl.BlockSpec((1, 1, d), lambda i: (i // tps, 0, 0))
    rw = jnp.zeros((d, LANES), F32).at[:, :N_EXPERTS].set(router_w)
    rb = jnp.zeros((1, LANES), F32).at[0, :N_EXPERTS].set(router_b)
    return pl.pallas_call(
        _route_kernel,
        grid=(n // tm,),
        in_specs=[row(cat.shape[1]), row(d), _resident_spec(w_out.shape), per_b, _const_spec((1, d)),
                  per_b, per_b, _const_spec((d, LANES)), _const_spec((1, LANES))],
        out_specs=[row(d), row(d), pl.BlockSpec((SUBLANES, tm), lambda i: (0, i)), row(LANES), row(LANES),
                   pl.BlockSpec((1, SUBLANES, LANES), lambda i: (i, 0, 0)), _const_spec((SUBLANES, LANES))],
        out_shape=[jax.ShapeDtypeStruct((n, d), F32), jax.ShapeDtypeStruct((n, d), F32),
                   jax.ShapeDtypeStruct((SUBLANES, n), jnp.int32), jax.ShapeDtypeStruct((n, LANES), jnp.int32),
                   jax.ShapeDtypeStruct((n, LANES), F32),
                   jax.ShapeDtypeStruct((n // tm, SUBLANES, LANES), jnp.int32),
                   jax.ShapeDtypeStruct((SUBLANES, LANES), jnp.int32)],
        scratch_shapes=[pltpu.VMEM((SUBLANES, LANES), F32)],
        compiler_params=_cparams("arbitrary"),
        name="out_proj1_router",
    )(cat, x2d, w_out.astype(BF16), g1, nw.reshape(1, d), sc, sh, rw, rb)


def _local_rows(tr):
    return 2 * tr + N_EXPERTS * SUBLANES


def _xs_rows(n):
    worst = 2 * n + (n // TOKEN_TILE) * N_EXPERTS * (SUBLANES - 1)
    return (-(-worst // MOE_TILE) + N_EXPERTS) * MOE_TILE


def _local_pos(e_k, r_k, delta_ref, tile):
    shift = jnp.zeros_like(r_k)
    for e in range(N_EXPERTS):
        shift = jnp.where(e_k == e, delta_ref[tile * N_EXPERTS + e], shift)
    return r_k + shift


def _for_each_group(tile, lstart_ref, run_ref, gstart_ref, fn):
    shift = int(math.log2(SUBLANES))
    for e in range(N_EXPERTS):
        k = tile * N_EXPERTS + e
        l_start = lstart_ref[k]
        g_start = gstart_ref[k]

        def body(g, c, l_start=l_start, g_start=g_start):
            off = g * SUBLANES
            fn(pl.multiple_of(l_start + off, SUBLANES), pl.multiple_of(g_start + off, SUBLANES))
            return c

        lax.fori_loop(0, run_ref[k] >> shift, body, 0)


def _zero_fill_gaps(gap_ref, used_ref, xs_ref, zero_ref, sem):
    zero_ref[...] = jnp.zeros_like(zero_ref)
    zr = zero_ref.shape[0]
    per_tile = MOE_TILE // zr
    shift = int(math.log2(SUBLANES))

    def gap_copy(e, g):
        row = pl.multiple_of(gap_ref[e] + g * SUBLANES, SUBLANES)
        return pltpu.make_async_copy(zero_ref.at[pl.ds(0, SUBLANES)], xs_ref.at[pl.ds(row, SUBLANES)], sem)

    def tile_copy(k):
        row = pl.multiple_of(k * zr, zr)
        return pltpu.make_async_copy(zero_ref, xs_ref.at[pl.ds(row, zr)], sem)

    def both(op):
        for e in range(N_EXPERTS):
            lax.fori_loop(0, gap_ref[N_EXPERTS + e] >> shift, lambda g, c, e=e: (op(gap_copy(e, g)), c)[1], 0)
        lax.fori_loop(used_ref[0] * per_tile, (xs_ref.shape[0] // MOE_TILE) * per_tile,
                      lambda k, c: (op(tile_copy(k)), c)[1], 0)

    both(lambda cp: cp.start())
    both(lambda cp: cp.wait())


def _dispatch_kernel(delta_ref, lstart_ref, run_ref, gstart_ref, gap_ref, used_ref, hn_ref, meta_ref, xs_ref,
                     sbuf_ref, zero_ref, sem):
    j = pl.program_id(0)
    tr = hn_ref.shape[0]
    lrows = sbuf_ref.shape[0]

    @pl.when(j == 0)
    def _():
        _zero_fill_gaps(gap_ref, used_ref, xs_ref, zero_ref, sem)

    meta = meta_ref[...]
    lp1 = _local_pos(meta[0:1], meta[2:3], delta_ref, j)
    lp2 = _local_pos(meta[1:2], meta[3:4], delta_ref, j)
    r_idx = lax.broadcasted_iota(jnp.int32, (lrows, tr), 0)
    onehot = jnp.logical_or(r_idx == lp1, r_idx == lp2).astype(BF16)
    sbuf_ref[...] = jnp.dot(onehot, hn_ref[...].astype(BF16), preferred_element_type=F32)

    def group_copy(local_row, xs_row):
        return pltpu.make_async_copy(sbuf_ref.at[pl.ds(local_row, SUBLANES)],
                                     xs_ref.at[pl.ds(xs_row, SUBLANES)], sem)

    _for_each_group(j, lstart_ref, run_ref, gstart_ref, lambda lr, xr: group_copy(lr, xr).start())
    _for_each_group(j, lstart_ref, run_ref, gstart_ref, lambda lr, xr: group_copy(lr, xr).wait())


def _dispatch(hn, meta_t, tables, gaps, used_tiles):
    n, d = hn.shape
    tr = TOKEN_TILE
    lrows = _local_rows(tr)
    return pl.pallas_call(
        _dispatch_kernel,
        grid_spec=pltpu.PrefetchScalarGridSpec(
            num_scalar_prefetch=6,
            grid=(n // tr,),
            in_specs=[pl.BlockSpec((tr, d), lambda j, *_: (j, 0)),
                      pl.BlockSpec((SUBLANES, tr), lambda j, *_: (0, j))],
            out_specs=pl.BlockSpec(memory_space=pl.ANY),
            scratch_shapes=[pltpu.VMEM((lrows, d), F32), pltpu.VMEM((MOE_SUB, d), F32),
                            pltpu.SemaphoreType.DMA(())]),
        out_shape=jax.ShapeDtypeStruct((_xs_rows(n), d), F32),
        compiler_params=_cparams("arbitrary"),
        name="moe_dispatch",
    )(*tables, gaps, used_tiles, hn, meta_t)


def _moe_kernel(te_ref, hi_ref, x_ref, wg_ref, wu_ref, wd_ref, o_ref, xb_ref):
    w = pl.program_id(0)
    f = pl.program_id(1)
    tm = x_ref.shape[0]
    sub = MOE_SUB
    sub_shift = int(math.log2(sub))
    hi = hi_ref[w]

    def swiglu_part(xb, wg, wu, wd):
        hg = jnp.dot(xb, wg, preferred_element_type=F32)
        hu = jnp.dot(xb, wu, preferred_element_type=F32)
        act = (_silu(hg) * hu).astype(BF16)
        return jnp.dot(act, wd, preferred_element_type=F32)

    @pl.when(jnp.logical_and(f == 0, hi > 0))
    def _():
        row = lax.broadcasted_iota(jnp.int32, (tm, 1), 0)
        xb_ref[...] = jnp.where(row < hi, x_ref[...], 0.0).astype(BF16)

    @pl.when(hi == tm)
    def _():
        part = swiglu_part(xb_ref[...], wg_ref[0].astype(BF16), wu_ref[0].astype(BF16),
                           wd_ref[0].astype(BF16))

        @pl.when(f == 0)
        def _():
            o_ref[...] = part

        @pl.when(f != 0)
        def _():
            o_ref[...] += part

    @pl.when(hi < tm)
    def _():
        @pl.when(f == 0)
        def _():
            o_ref[...] = jnp.zeros_like(o_ref)

        def sub_block(s, carry):
            rows = pl.ds(pl.multiple_of(s * sub, sub), sub)
            o_ref[rows, :] += swiglu_part(xb_ref[rows, :], wg_ref[0].astype(BF16),
                                          wu_ref[0].astype(BF16), wd_ref[0].astype(BF16))
            return carry

        lax.fori_loop(0, (hi + sub - 1) >> sub_shift, sub_block, 0)


def _moe_ffn(xs, tile_expert, tile_rows, wg, wu, wd):
    rows, d = xs.shape
    tm = MOE_TILE
    tf = MOE_FF_TILE
    nf = wg.shape[2] // tf

    def f_idx(f, hi):
        v = (hi > 0).astype(jnp.int32)
        return f * v + (nf - 1) * (1 - v)

    return pl.pallas_call(
        _moe_kernel,
        grid_spec=pltpu.PrefetchScalarGridSpec(
            num_scalar_prefetch=2,
            grid=(rows // tm, nf),
            in_specs=[pl.BlockSpec((tm, d), lambda w, f, te, hi: (w, 0)),
                      pl.BlockSpec((1, d, tf), lambda w, f, te, hi: (te[w], 0, f_idx(f, hi[w]))),
                      pl.BlockSpec((1, d, tf), lambda w, f, te, hi: (te[w], 0, f_idx(f, hi[w]))),
                      pl.BlockSpec((1, tf, d), lambda w, f, te, hi: (te[w], f_idx(f, hi[w]), 0))],
            out_specs=pl.BlockSpec((tm, d), lambda w, f, te, hi: (w, 0)),
            scratch_shapes=[pltpu.VMEM((tm, d), BF16)]),
        out_shape=jax.ShapeDtypeStruct((rows, d), F32),
        compiler_params=_cparams("arbitrary", "arbitrary"),
        name="moe_expert_swiglu",
    )(tile_expert, tile_rows, xs, wg, wu, wd)


def _combine_kernel(delta_ref, lstart_ref, run_ref, gstart_ref, ys_ref, x_ref, meta_ref, wt_ref, g2_ref, fw_ref,
                    o_ref, ybuf_ref, sem):
    j = pl.program_id(0)
    n_tiles = pl.num_programs(0)
    tr = x_ref.shape[0]
    lrows = ybuf_ref.shape[1]
    slot = lax.rem(j, 2)

    def group_copy(buf, local_row, xs_row):
        return pltpu.make_async_copy(ys_ref.at[pl.ds(xs_row, SUBLANES)],
                                     ybuf_ref.at[buf, pl.ds(local_row, SUBLANES)], sem.at[buf])

    def fetch(tile, buf):
        ybuf_ref[buf, 2 * tr:, :] = jnp.zeros((lrows - 2 * tr, ybuf_ref.shape[2]), F32)
        _for_each_group(tile, lstart_ref, run_ref, gstart_ref,
                        lambda lr, xr: group_copy(buf, lr, xr).start())

    @pl.when(j == 0)
    def _():
        fetch(0, 0)

    _for_each_group(j, lstart_ref, run_ref, gstart_ref, lambda lr, xr: group_copy(slot, lr, xr).wait())

    @pl.when(j + 1 < n_tiles)
    def _():
        fetch(j + 1, 1 - slot)

    meta = meta_ref[...]
    wt = wt_ref[...]
    lp1 = _local_pos(meta[:, 0:1], meta[:, 2:3], delta_ref, j)
    lp2 = _local_pos(meta[:, 1:2], meta[:, 3:4], delta_ref, j)
    l_idx = lax.broadcasted_iota(jnp.int32, (tr, lrows), 1)
    pick = jnp.where(l_idx == lp1, wt[:, 0:1], 0.0) + jnp.where(l_idx == lp2, wt[:, 1:2], 0.0)
    ffn = _bdot(pick, ybuf_ref[slot])
    x4 = x_ref[...] + g2_ref[0] * ffn
    o_ref[...] = (x4 * _rms_scale(x4)) * fw_ref[...]


def _combine(ys, tables, x3, meta, wt, g2, final_w, seq_len):
    n, d = x3.shape
    tr = TOKEN_TILE
    tps = seq_len // tr
    lrows = -(-_local_rows(tr) // LANES) * LANES
    return pl.pallas_call(
        _combine_kernel,
        grid_spec=pltpu.PrefetchScalarGridSpec(
            num_scalar_prefetch=4,
            grid=(n // tr,),
            in_specs=[pl.BlockSpec(memory_space=pl.ANY),
                      pl.BlockSpec((tr, d), lambda j, *_: (j, 0)),
                      pl.BlockSpec((tr, LANES), lambda j, *_: (j, 0)),
                      pl.BlockSpec((tr, LANES), lambda j, *_: (j, 0)),
                      pl.BlockSpec((1, 1, d), lambda j, *_: (j // tps, 0, 0)),
                      pl.BlockSpec((1, d), lambda j, *_: (0, 0))],
            out_specs=pl.BlockSpec((tr, d), lambda j, *_: (j, 0)),
            scratch_shapes=[pltpu.VMEM((2, lrows, d), F32), pltpu.SemaphoreType.DMA((2,))]),
        out_shape=jax.ShapeDtypeStruct((n, d), F32),
        compiler_params=_cparams("arbitrary"),
        name="moe_combine_final_norm",
    )(*tables, ys, x3, meta, wt, g2, final_w.reshape(1, d))


def _moe_tables(tile_base, counts, n_tokens):
    i32 = lambda t: t.astype(jnp.int32)
    tm = MOE_TILE
    before = tile_base[:, 0, :N_EXPERTS]
    total = counts[0, :N_EXPERTS]
    run = jnp.concatenate([before[1:], total[None]], axis=0) - before
    run = (run + SUBLANES - 1) // SUBLANES * SUBLANES
    l_end = jnp.cumsum(run, axis=1)
    l_start = l_end - run
    g_size = jnp.sum(run, axis=0)
    g_tiles = (g_size + tm - 1) // tm
    tile_end = jnp.cumsum(g_tiles)
    g_off = (tile_end - g_tiles) * tm
    g_end = g_off + g_size
    g_start = g_off[None, :] + jnp.cumsum(run, axis=0) - run
    delta = l_start - before
    gaps = jnp.concatenate([g_end, tile_end * tm - g_end])
    used_tiles = tile_end[-1:]

    w = jnp.arange(_xs_rows(n_tokens) // tm, dtype=jnp.int32)
    te = jnp.minimum(jnp.sum((tile_end[None, :] <= w[:, None]).astype(jnp.int32), axis=1), N_EXPERTS - 1)
    rows = jnp.where(w < tile_end[-1], jnp.clip(jnp.take(g_end, te) - w * tm, 0, tm), 0)
    flat = lambda t: i32(t).reshape(-1)
    return (flat(delta), flat(l_start), flat(run), flat(g_start)), i32(gaps), i32(used_tiles), i32(te), i32(rows)


def kernel(x, c, rel_bias, ada_w, ada_b, norm_mix_w, norm_ffn_w, final_norm_w, ab_w_in, attn_sinks,
           dn_conv_w, dn_a_log, dn_dt_bias, dn_norm_w, ab_w_out, ffn_w_gate, ffn_w_up, ffn_w_down,
           cd_w_in, lru_conv_w, lru_conv_b, lru_gate_a_w, lru_gate_a_b, lru_gate_x_w, lru_gate_x_b,
           lru_lambda, sconv_w, cd_w_out, moe_router_w, moe_router_b, moe_w_gate, moe_w_up, moe_w_down):
    bsz, seq_len, d = x.shape
    n = bsz * seq_len
    x2d = x.reshape(n, d)
    mods = _ada_mods(c, ada_w, ada_b)

    sh1, sc1, g1, sh2, sc2, g2 = (mods[0, k] for k in range(6))
    qa, kd, vd, qn, kn, vb, gs, bexp, gcexp = _in_proj0(
        x2d, norm_mix_w[0], sc1, sh1, ab_w_in[0], dn_conv_w[0], dn_a_log[0], dn_dt_bias[0], seq_len)
    attn = _attention(qa, kd, vd, _bias_table(rel_bias), attn_sinks[0], seq_len)
    dn = _deltanet(qn, kn, vb, gs, bexp, gcexp, dn_norm_w[0], seq_len)
    x2 = _mid0(attn, dn, x2d, ab_w_out[0], g1, norm_ffn_w[0], sc2, sh2, g2,
               ffn_w_gate[0], ffn_w_up[0], ffn_w_down[0], seq_len)

    sh1, sc1, g1, sh2, sc2, g2 = (mods[1, k] for k in range(6))
    cat = _mix1(x2, norm_mix_w[1], sc1, sh1, cd_w_in[0], lru_conv_w[0], lru_conv_b[0],
                lru_gate_a_w[0], lru_gate_a_b[0], lru_gate_x_w[0], lru_gate_x_b[0],
                lru_lambda[0], sconv_w[0], seq_len)
    x3, hn4, meta_t, meta, wt, tile_base, counts = _route(
        cat, x2, cd_w_out[0], g1, norm_ffn_w[1], sc2, sh2, moe_router_w[0], moe_router_b[0], seq_len)
    tables, gaps, used_tiles, tile_expert, tile_rows = _moe_tables(tile_base, counts, n)
    xs = _dispatch(hn4, meta_t, tables, gaps, used_tiles)
    ys = _moe_ffn(xs, tile_expert, tile_rows, moe_w_gate[0], moe_w_up[0], moe_w_down[0])
    out = _combine(ys, tables, x3, meta, wt, g2, final_norm_w, seq_len)
    return out.reshape(bsz, seq_len, d)
```

```python
import functools
import math

import numpy as np
import jax
import jax.numpy as jnp
from jax import lax
from jax.experimental import pallas as pl
from jax.experimental.pallas import tpu as pltpu

D_MODEL = 1024
EPS = 1e-6
HEAD_DIM = 64
A_Q_HEADS = 8
A_KV_HEADS = 2
WINDOW = 128
N_BUCKETS = 32
MAX_DISTANCE = 128
B_HEADS = 8
B_CONV = 4
CHUNK = 64
A_Q_W = A_Q_HEADS * HEAD_DIM
A_KV_W = A_KV_HEADS * HEAD_DIM
B_W = B_HEADS * HEAD_DIM
B_QKV_W = 3 * B_W
LRU_WIDTH = D_MODEL
LRU_BLOCKS = 8
LRU_C = 8.0
SC_WIDTH = D_MODEL // 2
D_FF = 2816
N_EXPERTS = 8
D_FF_EXPERT = 3584

LANES = 128
SUBLANES = 8
VMEM_LIMIT_BYTES = 56 * 1024 * 1024
TOKEN_TILE = 512
MOE_TILE = 1024
MOE_SUB = 256
MOE_FF_TILE = 512
NEG_BIG = -1e30

F32 = jnp.float32
BF16 = jnp.bfloat16


def _cparams(*sem):
    return pltpu.CompilerParams(dimension_semantics=tuple(sem), vmem_limit_bytes=VMEM_LIMIT_BYTES)


def _const_spec(shape):
    nd = len(shape)
    return pl.BlockSpec(shape, lambda *_: (0,) * nd)


def _bdot(a, b):
    return jnp.dot(a.astype(BF16), b.astype(BF16), preferred_element_type=F32)


def _bdot_nt(a, b):
    return lax.dot_general(a.astype(BF16), b.astype(BF16), (((1,), (1,)), ((), ())),
                           preferred_element_type=F32)


def _bdot_tn(a, b):
    return lax.dot_general(a.astype(BF16), b.astype(BF16), (((0,), (0,)), ((), ())),
                           preferred_element_type=F32)


def _split(x, n):
    parts = []
    r = x
    for i in range(n):
        p = r.astype(BF16)
        parts.append(p)
        if i + 1 < n:
            r = r - p.astype(F32)
    return parts


def _dot_x(a, b, na=2, nb=2):
    asp = _split(a, na) if na > 1 else [a.astype(BF16)]
    bsp = _split(b, nb) if nb > 1 else [b.astype(BF16)]
    acc = None
    for i, ai in enumerate(asp):
        for j, bj in enumerate(bsp):
            if i + j >= max(na, nb):
                continue
            t = jnp.dot(ai, bj, preferred_element_type=F32)
            acc = t if acc is None else acc + t
    return acc


def _silu(x):
    return x * (1.0 / (1.0 + jnp.exp(-x)))


def _sigmoid(x):
    return 1.0 / (1.0 + jnp.exp(-x))


def _log1p(z):
    u = 1.0 + z
    tiny = u == 1.0
    return jnp.where(tiny, z, jnp.log(u) * (z / jnp.where(tiny, 1.0, u - 1.0)))


def _softplus(x):
    return jnp.maximum(x, 0.0) + _log1p(jnp.exp(-jnp.abs(x)))


def _neg_expm1(y):
    return -jnp.tanh(0.5 * y) * (jnp.exp(y) + 1.0)


def _rms_scale(x):
    width = x.shape[1]
    mean_w = jnp.full((width, LANES), 1.0 / width, BF16)
    ms = _dot_x(x * x, mean_w, 2, 1)
    r = lax.rsqrt(ms + EPS)
    return jnp.concatenate([r] * (width // LANES), axis=1)


def _norm_mod(x, w, sc, sh, on_mxu=False):
    if on_mxu:
        scale = _rms_scale(x)
    else:
        scale = lax.rsqrt(jnp.mean(x * x, axis=-1, keepdims=True) + EPS)
    return (x * scale) * w * (1.0 + sc) + sh


def _shift_rows(x, k, prev_tail):
    n, width = x.shape
    x3 = x.reshape(n // SUBLANES, SUBLANES, width)
    rot = pltpu.roll(x3, k, 1)
    rot_prev = jnp.concatenate([pltpu.roll(prev_tail, k, 0)[None], rot[:-1]], axis=0)
    sub = lax.broadcasted_iota(jnp.int32, x3.shape, 1)
    return jnp.where(sub >= k, rot, rot_prev).reshape(n, width)


def _ada_kernel(c_ref, w_ref, b_ref, o_ref):
    c = c_ref[...]
    cond = _silu(c)
    o_ref[0] = _dot_x(cond, w_ref[0], 3, 2) + b_ref[0]


def _ada_mods(c, ada_w, ada_b):
    depth, d, six_d = ada_w.shape
    bsz = c.shape[0]
    rows = max(SUBLANES, bsz)
    c_pad = jnp.zeros((rows, d), F32).at[:bsz].set(c)
    tn = 1536
    out = pl.pallas_call(
        _ada_kernel,
        grid=(depth, six_d // tn),
        in_specs=[pl.BlockSpec((rows, d), lambda l, j: (0, 0)),
                  pl.BlockSpec((1, d, tn), lambda l, j: (l, 0, j)),
                  pl.BlockSpec((1, 1, tn), lambda l, j: (l, 0, j))],
        out_specs=pl.BlockSpec((1, rows, tn), lambda l, j: (l, 0, j)),
        out_shape=jax.ShapeDtypeStruct((depth, rows, six_d), F32),
        compiler_params=_cparams("parallel", "parallel"),
        name="ada_mods",
    )(c_pad, ada_w, ada_b.reshape(depth, 1, six_d))
    return out[:, :bsz].reshape(depth, bsz, 6, 1, d).transpose(0, 2, 1, 3, 4)


def _t5_bucket(dist):
    max_exact = N_BUCKETS // 2
    d = np.maximum(dist, 0)
    large = max_exact + (np.log(np.maximum(d, 1) / max_exact) / math.log(MAX_DISTANCE / max_exact)
                         * (N_BUCKETS - max_exact)).astype(np.int32)
    large = np.minimum(large, N_BUCKETS - 1)
    return np.where(d < max_exact, d, large).astype(np.int32)


def _band_buckets():
    qi = np.arange(WINDOW)[:, None]
    s = np.arange(2 * WINDOW)[None, :]
    dist = qi + WINDOW - s
    in_window = (dist >= 0) & (dist < WINDOW)
    return np.where(in_window, _t5_bucket(dist), -1).astype(np.int32)


def _bias_kernel(rb_ref, bucket_ref, o_ref):
    h = pl.program_id(0)
    bucket = bucket_ref[...]
    acc = jnp.zeros(bucket.shape, F32)
    for b in range(N_BUCKETS):
        acc = jnp.where(bucket == b, rb_ref[b, h], acc)
    o_ref[0] = jnp.where(bucket < 0, NEG_BIG, acc)


def _bias_table(rel_bias):
    bucket = jnp.asarray(_band_buckets())
    out = pl.pallas_call(
        _bias_kernel,
        grid=(A_Q_HEADS,),
        in_specs=[pl.BlockSpec(memory_space=pltpu.SMEM),
                  _const_spec((WINDOW, 2 * WINDOW))],
        out_specs=pl.BlockSpec((1, WINDOW, 2 * WINDOW), lambda h: (h, 0, 0)),
        out_shape=jax.ShapeDtypeStruct((A_Q_HEADS, WINDOW, 2 * WINDOW), F32),
        compiler_params=_cparams("parallel"),
        name="attn_bias_table",
    )(rel_bias, bucket)
    return out.reshape(A_Q_HEADS // 2, 2 * WINDOW, 2 * WINDOW)


_C_QA = 0
_C_KA = _C_QA + A_Q_W
_C_VA = _C_KA + A_KV_W
_C_QKV = _C_VA + A_KV_W
_C_GATE = _C_QKV + B_QKV_W
_C_SMALL = _C_GATE + B_W
_AB_COLS = _C_SMALL + LANES


def _ab_in_weight(w_in):
    return jnp.pad(w_in, ((0, 0), (0, _AB_COLS - w_in.shape[1]))).astype(BF16)


def _dup_heads(t, low):
    swapped = pltpu.roll(t, HEAD_DIM, 1)
    return jnp.concatenate([jnp.where(low, t, swapped), jnp.where(low, swapped, t)], axis=1)


def _chunk_tril(tm):
    r = np.arange(tm)
    return ((r[:, None] >= r[None, :]) & (r[:, None] // CHUNK == r[None, :] // CHUNK)).astype(np.float32)


def _head_selector():
    e = np.zeros((B_W, LANES), np.float32)
    for h in range(B_HEADS):
        e[h * HEAD_DIM:(h + 1) * HEAD_DIM, h] = 1.0
    return e


def _in0_kernel(x_ref, nw_ref, sc_ref, sh_ref, w_ref, cw_ref, sel_ref, selt_ref, tril_ref, alog_ref, dtb_ref,
                qa_ref, kd_ref, vd_ref, qn_ref, kn_ref, vb_ref, gs_ref, bexp_ref, gcexp_ref,
                tail_ref, *, tiles_per_seq):
    i = pl.program_id(0)

    @pl.when(i % tiles_per_seq == 0)
    def _():
        tail_ref[...] = jnp.zeros_like(tail_ref)

    hn = _norm_mod(x_ref[...], nw_ref[...], sc_ref[0], sh_ref[0])
    proj = jnp.dot(hn.astype(BF16), w_ref[...], preferred_element_type=F32)
    tm = proj.shape[0]
    low = lax.broadcasted_iota(jnp.int32, (tm, LANES), 1) < HEAD_DIM

    small = proj[:, _C_SMALL:]
    lane = lax.broadcasted_iota(jnp.int32, small.shape, 1)
    beta = jnp.where(lane < B_HEADS, _sigmoid(small), 0.0)
    dec = pltpu.roll(small, LANES - B_HEADS, 1)
    g = jnp.where(lane < B_HEADS, -jnp.exp(alog_ref[...]) * _softplus(dec + dtb_ref[...]), 0.0)
    bexp_ref[...] = _dot_x(beta, selt_ref[...], 2, 1)
    gc = _dot_x(tril_ref[...], g, 1, 3)
    gcexp_ref[...] = _dot_x(gc, selt_ref[...], 3, 1)

    qa_ref[...] = proj[:, _C_QA:_C_KA].astype(BF16)
    kd_ref[...] = _dup_heads(proj[:, _C_KA:_C_VA], low).astype(BF16)
    vd_ref[...] = _dup_heads(proj[:, _C_VA:_C_QKV], low).astype(BF16)

    def conv_silu(block):
        cols = slice(block * B_W, (block + 1) * B_W)
        xq = proj[:, _C_QKV + block * B_W:_C_QKV + (block + 1) * B_W]
        tail = tail_ref[:, cols]
        cw = cw_ref[:, cols]
        y = xq * cw[B_CONV - 1:B_CONV]
        for k in range(1, B_CONV):
            y = y + _shift_rows(xq, k, tail) * cw[B_CONV - 1 - k:B_CONV - k]
        tail_ref[:, cols] = xq[tm - SUBLANES:]
        return _silu(y)

    def head_scale(t):
        ssq = _dot_x(t * t, sel_ref[...], 2, 1)
        r = lax.rsqrt(ssq + EPS)
        return _dot_x(r, selt_ref[...], 2, 1)

    q = conv_silu(0)
    q_scale = head_scale(q)
    k_ = conv_silu(1)
    k_scale = head_scale(k_)
    vb_ref[...] = conv_silu(2)
    gs_ref[...] = _silu(proj[:, _C_GATE:_C_SMALL])
    qn_ref[...] = q * q_scale * (HEAD_DIM ** -0.5)
    kn_ref[...] = k_ * k_scale


def _in_proj0(x2d, nw, sc, sh, w_in, conv_w, a_log, dt_bias, seq_len):
    n, d = x2d.shape
    tm = TOKEN_TILE
    tiles_per_seq = seq_len // tm
    w = _ab_in_weight(w_in)
    sel = jnp.asarray(_head_selector(), BF16)
    selt = jnp.asarray(_head_selector().T.copy(), BF16)
    tril = jnp.asarray(_chunk_tril(tm), BF16)
    pad8 = lambda v: jnp.zeros((1, LANES), F32).at[0, :B_HEADS].set(v)
    row = lambda width: pl.BlockSpec((tm, width), lambda i: (i, 0))
    per_b = pl.BlockSpec((1, 1, d), lambda i: (i // tiles_per_seq, 0, 0))
    outs = pl.pallas_call(
        functools.partial(_in0_kernel, tiles_per_seq=tiles_per_seq),
        grid=(n // tm,),
        in_specs=[row(d), _const_spec((1, d)), per_b, per_b,
                  _resident_spec((d, _AB_COLS)), _const_spec((B_CONV, B_QKV_W)),
                  _const_spec((B_W, LANES)), _const_spec((LANES, B_W)), _const_spec((tm, tm)),
                  _const_spec((1, LANES)), _const_spec((1, LANES))],
        out_specs=[row(A_Q_W), row(2 * A_KV_W), row(2 * A_KV_W)] + [row(B_W)] * 6,
        out_shape=[jax.ShapeDtypeStruct((n, A_Q_W), BF16),
                   jax.ShapeDtypeStruct((n, 2 * A_KV_W), BF16),
                   jax.ShapeDtypeStruct((n, 2 * A_KV_W), BF16)]
        + [jax.ShapeDtypeStruct((n, B_W), F32)] * 6,
        scratch_shapes=[pltpu.VMEM((SUBLANES, B_QKV_W), F32)],
        compiler_params=_cparams("arbitrary"),
        name="in_proj0",
    )(x2d, nw.reshape(1, d), sc, sh, w, conv_w, sel, selt, tril, pad8(a_log), pad8(dt_bias))
    return outs


_ATTN_BLOCKS = 2


def _attn_kernel(sink_ref, q_ref, kp_ref, kc_ref, vp_ref, vc_ref, bm_ref, o_ref, *, steps_per_seq):
    i = pl.program_id(0)
    first = (i % steps_per_seq) == 0
    w = WINDOW
    lane = lax.broadcasted_iota(jnp.int32, (w, LANES), 1)
    low = lane < HEAD_DIM
    col = lax.broadcasted_iota(jnp.int32, (2 * w, 2 * w), 1)
    row = lax.broadcasted_iota(jnp.int32, (2 * w, 1), 0)
    prev_dead = jnp.logical_and(first, col < w)
    zero = jnp.zeros((), q_ref.dtype)
    pairs = A_Q_HEADS // 2
    units = [(b, j) for b in range(_ATTN_BLOCKS) for j in range(pairs)]

    def keys(p_ref, c_ref, b, kh):
        ls = slice(kh * LANES, (kh + 1) * LANES)
        before = p_ref[:, ls] if b == 0 else c_ref[(b - 1) * w:b * w, ls]
        return jnp.concatenate([before, c_ref[b * w:(b + 1) * w, ls]], axis=0)

    kv_of = lambda j: (2 * j) // (A_Q_HEADS // A_KV_HEADS)
    qp = [q_ref[b * w:(b + 1) * w, j * LANES:(j + 1) * LANES] for b, j in units]
    qs = [jnp.concatenate([jnp.where(low, t, zero), jnp.where(low, zero, t)], axis=0) for t in qp]
    kd = [keys(kp_ref, kc_ref, b, kv_of(j)) for b, j in units]
    vd = [keys(vp_ref, vc_ref, b, kv_of(j)) for b, j in units]
    s = [lax.dot_general(a, k, (((1,), (1,)), ((), ())), preferred_element_type=F32) for a, k in zip(qs, kd)]
    s = [t * (HEAD_DIM ** -0.5) + bm_ref[j] for t, (b, j) in zip(s, units)]
    s = [jnp.where(prev_dead, NEG_BIG, t) if b == 0 else t for t, (b, j) in zip(s, units)]
    sink = [jnp.where(row < w, sink_ref[2 * j], sink_ref[2 * j + 1]) for b, j in units]
    m = [jnp.maximum(jnp.max(t, axis=-1, keepdims=True), sk) for t, sk in zip(s, sink)]
    p = [jnp.exp(t - mt) for t, mt in zip(s, m)]
    denom = [jnp.sum(t, axis=-1, keepdims=True) + jnp.exp(sk - mt) for t, sk, mt in zip(p, sink, m)]
    pv = [jnp.dot(t.astype(BF16), v, preferred_element_type=F32) / dn for t, v, dn in zip(p, vd, denom)]
    outs = [jnp.where(low, t[:w], t[w:]) for t in pv]
    for b in range(_ATTN_BLOCKS):
        o_ref[b * w:(b + 1) * w, :] = jnp.concatenate(outs[b * pairs:(b + 1) * pairs], axis=1).astype(o_ref.dtype)


def _attention(qa, kd, vd, bias_tbl, sinks, seq_len):
    n = qa.shape[0]
    w = WINDOW
    rows = _ATTN_BLOCKS * w
    steps = seq_len // rows
    cur = lambda i: (i, 0)
    prev = lambda i: (jnp.where(i % steps == 0, i * _ATTN_BLOCKS, i * _ATTN_BLOCKS - 1), 0)
    return pl.pallas_call(
        functools.partial(_attn_kernel, steps_per_seq=steps),
        grid=(n // rows,),
        in_specs=[pl.BlockSpec(memory_space=pltpu.SMEM),
                  pl.BlockSpec((rows, A_Q_W), cur),
                  pl.BlockSpec((w, 2 * A_KV_W), prev), pl.BlockSpec((rows, 2 * A_KV_W), cur),
                  pl.BlockSpec((w, 2 * A_KV_W), prev), pl.BlockSpec((rows, 2 * A_KV_W), cur),
                  _const_spec((A_Q_HEADS // 2, 2 * w, 2 * w))],
        out_specs=pl.BlockSpec((rows, A_Q_W), cur),
        out_shape=jax.ShapeDtypeStruct((n, A_Q_W), BF16),
        compiler_params=_cparams("parallel"),
        name="swa_attention",
    )(sinks, qa, kd, kd, vd, vd, bias_tbl)


_DN_PAIRS = B_HEADS // 2
_DN_INV_BLOCK = 16
_DN_GROUP = 4


def _block_diag(x, low):
    zero = jnp.zeros((), x.dtype)
    return jnp.concatenate([jnp.where(low, x, zero), jnp.where(low, zero, x)], axis=0)


def _dn_intra(chunks, data_refs, work_refs, consts):
    qn_ref, kn_ref, vb_ref, bexp_ref, gcexp_ref = data_refs
    u_ref, w_ref, qk_ref, qd_ref, kd_ref, egl_ref = work_refs
    low, i_idx, j_idx, ones3 = consts
    c = CHUNK
    units = [(ci, p) for ci in chunks for p in range(_DN_PAIRS)]
    where = [(slice(ci * c, (ci + 1) * c), slice(p * LANES, (p + 1) * LANES)) for ci, p in units]
    causal = i_idx >= j_idx
    strict = i_idx > j_idx
    on_diag = i_idx == j_idx
    eye = on_diag.astype(F32)
    blk_shift = int(math.log2(_DN_INV_BLOCK))
    same_blk = (i_idx >> blk_shift) == (j_idx >> blk_shift)

    q = [qn_ref[rs, ls] for rs, ls in where]
    k = [kn_ref[rs, ls] for rs, ls in where]
    v = [vb_ref[rs, ls] for rs, ls in where]
    b = [bexp_ref[rs, ls] for rs, ls in where]
    gc = [gcexp_ref[rs, ls] for rs, ls in where]

    gr = [jnp.dot(ones3, jnp.concatenate(_split(jnp.where(on_diag, t, 0.0), 3), axis=0),
                  preferred_element_type=F32) for t in gc]
    ks = [_block_diag(t.astype(BF16), low) for t in k]
    qkk = [lax.dot_general(jnp.concatenate([qt, kt], axis=0).astype(BF16), kst,
                           (((1,), (1,)), ((), ())), preferred_element_type=F32)
           for qt, kt, kst in zip(q, k, ks)]
    decay = [jnp.exp(jnp.where(causal, gct - grt, NEG_BIG)) for gct, grt in zip(gc, gr)]
    lmat = [jnp.where(strict, bt * t[c:] * dt, 0.0) for bt, t, dt in zip(b, qkk, decay)]
    qk = [jnp.where(causal, t[:c] * dt, 0.0) for t, dt in zip(qkk, decay)]

    def mm(xs, ys):
        return [_bdot(x, _block_diag(y.astype(BF16), low)) for x, y in zip(xs, ys)]

    l_diag = [jnp.where(same_blk, t, 0.0) for t in lmat]
    l_off = [t - d for t, d in zip(lmat, l_diag)]
    pw = [-t for t in l_diag]
    d_inv = [eye + t for t in pw]
    for _ in range(blk_shift - 1):
        pw = mm(pw, pw)
        d_inv = mm(d_inv, [eye + t for t in pw])
    pw = [-t for t in mm(d_inv, l_off)]
    acc = [eye + t for t in pw]
    for _ in range(int(math.log2(c // _DN_INV_BLOCK)) - 1):
        pw = mm(pw, pw)
        acc = mm(acc, [eye + t for t in pw])
    tmat = mm(acc, d_inv)

    egc = [jnp.exp(t) for t in gc]
    rhs = [jnp.concatenate([_block_diag((vt * bt).astype(BF16), low),
                            _block_diag((kt * (bt * et)).astype(BF16), low)], axis=1)
           for vt, kt, bt, et in zip(v, k, b, egc)]
    uw = [_bdot(t, r) for t, r in zip(tmat, rhs)]
    for n, (ci, p) in enumerate(units):
        g_last = gc[n][c - 1:c, :]
        u_ref[ci, p] = uw[n][:, :LANES]
        w_ref[ci, p] = uw[n][:, LANES:]
        qk_ref[ci, p] = qk[n]
        qd_ref[ci, p] = q[n] * egc[n]
        kd_ref[ci, p] = k[n] * jnp.exp(g_last - gc[n])
        egl_ref[ci, p] = jnp.broadcast_to(jnp.exp(g_last), (SUBLANES, LANES))


def _dn_scan(ci, work_refs, s_ref, gs_ref, nw, o_ref, consts):
    u_ref, w_ref, qk_ref, qd_ref, kd_ref, egl_ref = work_refs
    low, mask_bd, head_mean2 = consts
    c = CHUNK
    rows = slice(ci * c, (ci + 1) * c)
    pairs = range(_DN_PAIRS)
    s_old = [s_ref[p] for p in pairs]
    wq = [_bdot(jnp.concatenate([w_ref[ci, p], qd_ref[ci, p]], axis=0), s_old[p]) for p in pairs]
    v_new = [u_ref[ci, p] - wq[p][:c] for p in pairs]
    o = [wq[p][c:] + _bdot(qk_ref[ci, p], _block_diag(v_new[p].astype(BF16), low)) for p in pairs]
    kv = [_bdot_tn(kd_ref[ci, p], v_new[p]) for p in pairs]
    for p in pairs:
        s_ref[p] = s_old[p] * egl_ref[ci, p][0:1, :] + jnp.where(mask_bd, kv[p], 0.0)
    ms = [jnp.dot(jnp.concatenate(_split(t * t, 2), axis=1), head_mean2, preferred_element_type=F32)
          for t in o]
    for p in pairs:
        ls = slice(p * LANES, (p + 1) * LANES)
        y = (o[p] * lax.rsqrt(ms[p] + EPS)) * nw * gs_ref[rows, ls]
        o_ref[rows, ls] = y.astype(o_ref.dtype)


def _dn_kernel(qn_ref, kn_ref, vb_ref, gs_ref, bexp_ref, gcexp_ref, nw_ref, o_ref,
               s_ref, u_ref, w_ref, qk_ref, qd_ref, kd_ref, egl_ref, *, groups_per_seq):
    i = pl.program_id(0)

    @pl.when(i % groups_per_seq == 0)
    def _():
        s_ref[...] = jnp.zeros_like(s_ref)

    c = CHUNK
    tm = o_ref.shape[0]
    n_chunks = tm // c
    lane = lax.broadcasted_iota(jnp.int32, (c, LANES), 1)
    low = lane < HEAD_DIM
    i_idx = lax.broadcasted_iota(jnp.int32, (c, LANES), 0)
    j_idx = lane & (c - 1)
    ones3 = jnp.ones((c, 3 * c), BF16)
    rb = lax.broadcasted_iota(jnp.int32, (LANES, LANES), 0)
    cb = lax.broadcasted_iota(jnp.int32, (LANES, LANES), 1)
    mask_bd = (rb < HEAD_DIM) == (cb < HEAD_DIM)
    head_mean = jnp.where(mask_bd, 1.0 / HEAD_DIM, 0.0).astype(BF16)
    head_mean2 = jnp.concatenate([head_mean, head_mean], axis=0)
    data_refs = (qn_ref, kn_ref, vb_ref, bexp_ref, gcexp_ref)
    work_refs = (u_ref, w_ref, qk_ref, qd_ref, kd_ref, egl_ref)
    intra_consts = (low, i_idx, j_idx, ones3)
    scan_consts = (low, mask_bd, head_mean2)
    nw = nw_ref[...]

    groups = [list(range(s, s + _DN_GROUP)) for s in range(0, n_chunks, _DN_GROUP)]
    _dn_intra(groups[0], data_refs, work_refs, intra_consts)
    for j, grp in enumerate(groups):
        if j + 1 < len(groups):
            _dn_intra(groups[j + 1], data_refs, work_refs, intra_consts)
        for ci in grp:
            _dn_scan(ci, work_refs, s_ref, gs_ref, nw, o_ref, scan_consts)


def _deltanet(qn, kn, vb, gs, bexp, gcexp, norm_w, seq_len):
    n = qn.shape[0]
    tm = TOKEN_TILE
    nw2 = jnp.concatenate([norm_w, norm_w]).reshape(1, LANES)
    row = lambda width: pl.BlockSpec((tm, width), lambda i: (i, 0))
    return pl.pallas_call(
        functools.partial(_dn_kernel, groups_per_seq=seq_len // tm),
        grid=(n // tm,),
        in_specs=[row(B_W)] * 6 + [_const_spec((1, LANES))],
        out_specs=row(B_W),
        out_shape=jax.ShapeDtypeStruct((n, B_W), BF16),
        scratch_shapes=[pltpu.VMEM((_DN_PAIRS, LANES, LANES), F32)]
        + [pltpu.VMEM((tm // CHUNK, _DN_PAIRS, CHUNK, LANES), F32)] * 5
        + [pltpu.VMEM((tm // CHUNK, _DN_PAIRS, SUBLANES, LANES), F32)],
        compiler_params=_cparams("arbitrary"),
        name="gated_deltanet",
    )(qn, kn, vb, gs, bexp, gcexp, nw2)


def _resident_spec(shape):
    nd = len(shape)
    return pl.BlockSpec(shape, lambda *_: (0,) * nd, pipeline_mode=pl.Buffered(1))


def _mid0_kernel(attn_ref, dn_ref, x_ref, wo_ref, g1_ref, nw_ref, sc_ref, sh_ref, g2_ref,
                 wg_ref, wu_ref, wd_ref, o_ref):
    mix = (jnp.dot(attn_ref[...], wo_ref[:A_Q_W], preferred_element_type=F32)
           + jnp.dot(dn_ref[...], wo_ref[A_Q_W:], preferred_element_type=F32))
    x1 = x_ref[...] + g1_ref[0] * mix
    hn = _norm_mod(x1, nw_ref[...], sc_ref[0], sh_ref[0]).astype(BF16)
    hg = jnp.dot(hn, wg_ref[...], preferred_element_type=F32)
    hu = jnp.dot(hn, wu_ref[...], preferred_element_type=F32)
    act = (_silu(hg) * hu).astype(BF16)
    o_ref[...] = x1 + g2_ref[0] * jnp.dot(act, wd_ref[...], preferred_element_type=F32)


def _mid0(attn, dn, x2d, w_out, g1, nw, sc, sh, g2, wg, wu, wd, seq_len):
    n, d = x2d.shape
    tm = TOKEN_TILE
    tps = seq_len // tm
    row = lambda width: pl.BlockSpec((tm, width), lambda i: (i, 0))
    per_b = pl.BlockSpec((1, 1, d), lambda i: (i // tps, 0, 0))
    return pl.pallas_call(
        _mid0_kernel,
        grid=(n // tm,),
        in_specs=[row(A_Q_W), row(B_W), row(d), _resident_spec(w_out.shape), per_b,
                  _const_spec((1, d)), per_b, per_b, per_b,
                  _resident_spec(wg.shape), _resident_spec(wu.shape), _resident_spec(wd.shape)],
        out_specs=row(d),
        out_shape=jax.ShapeDtypeStruct((n, d), F32),
        compiler_params=_cparams("parallel"),
        name="out_proj0_swiglu",
    )(attn, dn, x2d, w_out.astype(BF16), g1, nw.reshape(1, d), sc, sh, g2,
      wg.astype(BF16), wu.astype(BF16), wd.astype(BF16))


def _gelu_tanh(x):
    return 0.5 * x * (1.0 + jnp.tanh(math.sqrt(2.0 / math.pi) * (x + 0.044715 * (x * x * x))))


def _linear_scan(a, b, h0):
    n, width = a.shape
    groups = n // SUBLANES
    a = a.reshape(groups, SUBLANES, width)
    b = b.reshape(groups, SUBLANES, width)
    in_group = lax.broadcasted_iota(jnp.int32, a.shape, 1)
    s = 1
    while s < SUBLANES:
        a_sh = pltpu.roll(a, s, 1)
        b_sh = pltpu.roll(b, s, 1)
        valid = in_group >= s
        b = jnp.where(valid, a * b_sh + b, b)
        a = jnp.where(valid, a * a_sh, a)
        s *= 2
    carry = jnp.broadcast_to(h0, (SUBLANES, width))
    out = []
    for g in range(groups):
        hg = a[g] * carry + b[g]
        out.append(hg)
        carry = jnp.broadcast_to(hg[SUBLANES - 1:SUBLANES, :], hg.shape)
    return jnp.concatenate(out, axis=0)


def _mix1_tile(x, nw, sc, sh, w_ref, cw_ref, cb_ref, ga_ref, gab_ref, gx_ref, gxb_ref, lam_ref, sw_ref,
               tail_c_ref, tail_d_ref, h_ref, o_ref):
    hn = _norm_mod(x, nw, sc, sh, on_mxu=True).astype(BF16)
    tm = hn.shape[0]
    gw = ga_ref.shape[1]
    w_l = LRU_WIDTH
    kc = cw_ref.shape[0]
    ks = sw_ref.shape[0]

    def project(c0):
        return jnp.dot(hn, w_ref[:, c0:c0 + gw], preferred_element_type=F32)

    def lru_group(p, xc_in, yc):
        cols = slice(p * gw, (p + 1) * gw)
        cw = cw_ref[:, cols]
        tail = tail_c_ref[:, cols]
        xc = xc_in * cw[kc - 1:kc] + cb_ref[:, cols]
        for k in range(1, kc):
            xc = xc + _shift_rows(xc_in, k, tail) * cw[kc - 1 - k:kc - k]
        tail_c_ref[:, cols] = xc_in[tm - SUBLANES:]
        xin = xc.astype(BF16)
        r = _sigmoid(jnp.dot(xin, ga_ref[p], preferred_element_type=F32) + gab_ref[:, cols])
        ig = _sigmoid(jnp.dot(xin, gx_ref[p], preferred_element_type=F32) + gxb_ref[:, cols])
        log_a = (-LRU_C) * r * _softplus(-lam_ref[:, cols])
        a = jnp.exp(log_a)
        b = jnp.sqrt(_neg_expm1(2.0 * log_a)) * (ig * xc)
        h = _linear_scan(a, b, h_ref[0:1, cols])
        h_ref[:, cols] = jnp.broadcast_to(h[tm - 1:tm, :], (h_ref.shape[0], gw))
        o_ref[:, cols] = (h * _gelu_tanh(yc)).astype(o_ref.dtype)

    def conv_group(q, bd, cd, hd):
        cols = slice(q * gw, (q + 1) * gw)
        ch = cd * hd
        tail = tail_d_ref[:, cols]
        sw = sw_ref[:, cols]
        conv = ch * sw[ks - 1:ks]
        for k in range(1, ks):
            conv = conv + _shift_rows(ch, k, tail) * sw[ks - 1 - k:ks - k]
        tail_d_ref[:, cols] = ch[tm - SUBLANES:]
        o_ref[:, w_l + q * gw:w_l + (q + 1) * gw] = (bd * conv).astype(o_ref.dtype)

    pending = None
    for p in range(w_l // gw):
        xc_in, yc = project(p * gw), project(w_l + p * gw)
        if pending is not None:
            pending()
        pending = functools.partial(lru_group, p, xc_in, yc)
    for q in range(SC_WIDTH // gw):
        c0 = 2 * w_l + q * gw
        bd, cd, hd = project(c0), project(c0 + SC_WIDTH), project(c0 + 2 * SC_WIDTH)
        pending()
        pending = functools.partial(conv_group, q, bd, cd, hd)
    pending()


def _pair_block_diag(gw):
    nb, bw, _ = gw.shape
    g2 = gw.reshape(nb // 2, 2, bw, bw)
    z = jnp.zeros((nb // 2, bw, bw), gw.dtype)
    top = jnp.concatenate([g2[:, 0], z], axis=2)
    bot = jnp.concatenate([z, g2[:, 1]], axis=2)
    return jnp.concatenate([top, bot], axis=1).astype(BF16)


def _route_tile(cat, x, wo_ref, g1, nw, sc, sh, rw_ref, rb_ref, carry_ref):
    x3 = x + g1 * jnp.dot(cat, wo_ref[...], preferred_element_type=F32)
    hn = _norm_mod(x3, nw, sc, sh)
    tm = hn.shape[0]
    lane = lax.broadcasted_iota(jnp.int32, (tm, LANES), 1)
    logits = _dot_x(hn, rw_ref[...], 2, 2) + rb_ref[...]
    lg = jnp.where(lane < N_EXPERTS, logits, NEG_BIG)
    m1 = jnp.max(lg, axis=1, keepdims=True)
    i1 = jnp.min(jnp.where(lg == m1, lane, LANES), axis=1, keepdims=True)
    lg2 = jnp.where(lane == i1, NEG_BIG, lg)
    m2 = jnp.max(lg2, axis=1, keepdims=True)
    i2 = jnp.min(jnp.where(lg2 == m2, lane, LANES), axis=1, keepdims=True)
    e2 = jnp.exp(m2 - m1)
    w1 = 1.0 / (1.0 + e2)
    w2 = e2 / (1.0 + e2)

    hit1 = lane == i1
    hit2 = lane == i2
    sel = jnp.logical_or(hit1, hit2).astype(F32)
    r_i = lax.broadcasted_iota(jnp.int32, (tm, tm), 0)
    c_i = lax.broadcasted_iota(jnp.int32, (tm, tm), 1)
    tril = (r_i >= c_i).astype(BF16)
    incl = jnp.dot(tril, sel.astype(BF16), preferred_element_type=F32)
    carry = carry_ref[0:1, :]
    excl = incl - sel + carry
    r1 = jnp.sum(jnp.where(hit1, excl, 0.0), axis=1, keepdims=True)
    r2 = jnp.sum(jnp.where(hit2, excl, 0.0), axis=1, keepdims=True)
    total = carry + incl[tm - 1:tm, :]
    carry_ref[...] = jnp.broadcast_to(total, carry_ref.shape)

    meta = jnp.where(lane == 0, i1, 0)
    meta = jnp.where(lane == 1, i2, meta)
    meta = jnp.where(lane == 2, r1.astype(jnp.int32), meta)
    meta = jnp.where(lane == 3, r2.astype(jnp.int32), meta)
    wt = jnp.where(lane == 0, w1, jnp.where(lane == 1, w2, 0.0))
    return x3, hn, meta, wt, carry, total


def _mix1_kernel(x_ref, nw_ref, sc_ref, sh_ref, w_ref, cw_ref, cb_ref, ga_ref, gab_ref, gx_ref, gxb_ref,
                 lam_ref, sw_ref, o_ref, tail_c_ref, tail_d_ref, h_ref, *, tiles_per_seq):
    i = pl.program_id(0)

    @pl.when(i % tiles_per_seq == 0)
    def _():
        tail_c_ref[...] = jnp.zeros_like(tail_c_ref)
        tail_d_ref[...] = jnp.zeros_like(tail_d_ref)
        h_ref[...] = jnp.zeros_like(h_ref)

    _mix1_tile(x_ref[...], nw_ref[...], sc_ref[0], sh_ref[0], w_ref, cw_ref, cb_ref, ga_ref,
               gab_ref, gx_ref, gxb_ref, lam_ref, sw_ref, tail_c_ref, tail_d_ref, h_ref, o_ref)


def _mix1(x2d, nw, sc, sh, w_in, conv_w, conv_b, ga_w, ga_b, gx_w, gx_b, lam, sconv_w, seq_len):
    n, d = x2d.shape
    tm = TOKEN_TILE
    tps = seq_len // tm
    cd_in = w_in.shape[1]
    cd_out = LRU_WIDTH + SC_WIDTH
    row = lambda width: pl.BlockSpec((tm, width), lambda i: (i, 0))
    per_b = pl.BlockSpec((1, 1, d), lambda i: (i // tps, 0, 0))
    ga = _pair_block_diag(ga_w)
    gx = _pair_block_diag(gx_w)
    vec = lambda v: v.reshape(1, -1)
    return pl.pallas_call(
        functools.partial(_mix1_kernel, tiles_per_seq=tps),
        grid=(n // tm,),
        in_specs=[row(d), _const_spec((1, d)), per_b, per_b, _resident_spec((d, cd_in)),
                  _const_spec(conv_w.shape), _const_spec((1, LRU_WIDTH)),
                  _const_spec(ga.shape), _const_spec((1, LRU_WIDTH)),
                  _const_spec(gx.shape), _const_spec((1, LRU_WIDTH)),
                  _const_spec((1, LRU_WIDTH)), _const_spec(sconv_w.shape)],
        out_specs=row(cd_out),
        out_shape=jax.ShapeDtypeStruct((n, cd_out), BF16),
        scratch_shapes=[pltpu.VMEM((SUBLANES, LRU_WIDTH), F32), pltpu.VMEM((SUBLANES, SC_WIDTH), F32),
                        pltpu.VMEM((SUBLANES, LRU_WIDTH), F32)],
        compiler_params=_cparams("arbitrary"),
        name="rglru_shortconv_mixer",
    )(x2d, vec(nw), sc, sh, w_in.astype(BF16), conv_w, vec(conv_b), ga, vec(ga_b), gx, vec(gx_b),
      vec(lam), sconv_w)


def _route_kernel(cat_ref, x_ref, wo_ref, g1_ref, nw_ref, sc_ref, sh_ref, rw_ref, rb_ref,
                  x3_ref, hn_ref, metat_ref, meta_ref, wt_ref, base_ref, cnt_ref, carry_ref):
    @pl.when(pl.program_id(0) == 0)
    def _():
        carry_ref[...] = jnp.zeros_like(carry_ref)

    x3, hn, meta, wt, before, total = _route_tile(cat_ref[...], x_ref[...], wo_ref, g1_ref[0], nw_ref[...],
                                                  sc_ref[0], sh_ref[0], rw_ref, rb_ref, carry_ref)
    x3_ref[...] = x3
    hn_ref[...] = hn
    meta_ref[...] = meta
    metat_ref[...] = jnp.transpose(meta.astype(F32))[:SUBLANES].astype(jnp.int32)
    wt_ref[...] = wt
    base_ref[0] = jnp.broadcast_to(before, base_ref.shape[1:]).astype(jnp.int32)
    cnt_ref[...] = jnp.broadcast_to(total, cnt_ref.shape).astype(jnp.int32)


def _route(cat, x2d, w_out, g1, nw, sc, sh, router_w, router_b, seq_len):
    n, d = x2d.shape
    tm = TOKEN_TILE
    tps = seq_len // tm
    row = lambda width: pl.BlockSpec((tm, width), lambda i: (i, 0))
    per_b = pl.BlockSpec((1, 1, d), lambda i: (i // tps, 0, 0))
    rw = jnp.zeros((d, LANES), F32).at[:, :N_EXPERTS].set(router_w)
    rb = jnp.zeros((1, LANES), F32).at[0, :N_EXPERTS].set(router_b)
    return pl.pallas_call(
        _route_kernel,
        grid=(n // tm,),
        in_specs=[row(cat.shape[1]), row(d), _resident_spec(w_out.shape), per_b, _const_spec((1, d)),
                  per_b, per_b, _const_spec((d, LANES)), _const_spec((1, LANES))],
        out_specs=[row(d), row(d), pl.BlockSpec((SUBLANES, tm), lambda i: (0, i)), row(LANES), row(LANES),
                   pl.BlockSpec((1, SUBLANES, LANES), lambda i: (i, 0, 0)), _const_spec((SUBLANES, LANES))],
        out_shape=[jax.ShapeDtypeStruct((n, d), F32), jax.ShapeDtypeStruct((n, d), F32),
                   jax.ShapeDtypeStruct((SUBLANES, n), jnp.int32), jax.ShapeDtypeStruct((n, LANES), jnp.int32),
                   jax.ShapeDtypeStruct((n, LANES), F32),
                   jax.ShapeDtypeStruct((n // tm, SUBLANES, LANES), jnp.int32),
                   jax.ShapeDtypeStruct((SUBLANES, LANES), jnp.int32)],
        scratch_shapes=[pltpu.VMEM((SUBLANES, LANES), F32)],
        compiler_params=_cparams("arbitrary"),
        name="out_proj1_router",
    )(cat, x2d, w_out.astype(BF16), g1, nw.reshape(1, d), sc, sh, rw, rb)


def _local_rows(tr):
    return 2 * tr + N_EXPERTS * SUBLANES


def _xs_rows(n):
    worst = 2 * n + (n // TOKEN_TILE) * N_EXPERTS * (SUBLANES - 1)
    return (-(-worst // MOE_TILE) + N_EXPERTS) * MOE_TILE


def _local_pos(e_k, r_k, delta_ref, tile):
    shift = jnp.zeros_like(r_k)
    for e in range(N_EXPERTS):
        shift = jnp.where(e_k == e, delta_ref[tile * N_EXPERTS + e], shift)
    return r_k + shift


def _for_each_group(tile, lstart_ref, run_ref, gstart_ref, fn):
    shift = int(math.log2(SUBLANES))
    for e in range(N_EXPERTS):
        k = tile * N_EXPERTS + e
        l_start = lstart_ref[k]
        g_start = gstart_ref[k]

        def body(g, c, l_start=l_start, g_start=g_start):
            off = g * SUBLANES
            fn(pl.multiple_of(l_start + off, SUBLANES), pl.multiple_of(g_start + off, SUBLANES))
            return c

        lax.fori_loop(0, run_ref[k] >> shift, body, 0)


def _zero_fill_gaps(gap_ref, used_ref, xs_ref, zero_ref, sem):
    zero_ref[...] = jnp.zeros_like(zero_ref)
    zr = zero_ref.shape[0]
    per_tile = MOE_TILE // zr
    shift = int(math.log2(SUBLANES))

    def gap_copy(e, g):
        row = pl.multiple_of(gap_ref[e] + g * SUBLANES, SUBLANES)
        return pltpu.make_async_copy(zero_ref.at[pl.ds(0, SUBLANES)], xs_ref.at[pl.ds(row, SUBLANES)], sem)

    def tile_copy(k):
        row = pl.multiple_of(k * zr, zr)
        return pltpu.make_async_copy(zero_ref, xs_ref.at[pl.ds(row, zr)], sem)

    def both(op):
        for e in range(N_EXPERTS):
            lax.fori_loop(0, gap_ref[N_EXPERTS + e] >> shift, lambda g, c, e=e: (op(gap_copy(e, g)), c)[1], 0)
        lax.fori_loop(used_ref[0] * per_tile, (xs_ref.shape[0] // MOE_TILE) * per_tile,
                      lambda k, c: (op(tile_copy(k)), c)[1], 0)

    both(lambda cp: cp.start())
    both(lambda cp: cp.wait())


def _dispatch_kernel(delta_ref, lstart_ref, run_ref, gstart_ref, gap_ref, used_ref, hn_ref, meta_ref, xs_ref,
                     sbuf_ref, zero_ref, sem):
    j = pl.program_id(0)
    tr = hn_ref.shape[0]
    lrows = sbuf_ref.shape[0]

    @pl.when(j == 0)
    def _():
        _zero_fill_gaps(gap_ref, used_ref, xs_ref, zero_ref, sem)

    meta = meta_ref[...]
    lp1 = _local_pos(meta[0:1], meta[2:3], delta_ref, j)
    lp2 = _local_pos(meta[1:2], meta[3:4], delta_ref, j)
    r_idx = lax.broadcasted_iota(jnp.int32, (lrows, tr), 0)
    onehot = jnp.logical_or(r_idx == lp1, r_idx == lp2).astype(BF16)
    sbuf_ref[...] = jnp.dot(onehot, hn_ref[...].astype(BF16), preferred_element_type=F32)

    def group_copy(local_row, xs_row):
        return pltpu.make_async_copy(sbuf_ref.at[pl.ds(local_row, SUBLANES)],
                                     xs_ref.at[pl.ds(xs_row, SUBLANES)], sem)

    _for_each_group(j, lstart_ref, run_ref, gstart_ref, lambda lr, xr: group_copy(lr, xr).start())
    _for_each_group(j, lstart_ref, run_ref, gstart_ref, lambda lr, xr: group_copy(lr, xr).wait())


def _dispatch(hn, meta_t, tables, gaps, used_tiles):
    n, d = hn.shape
    tr = TOKEN_TILE
    lrows = _local_rows(tr)
    return pl.pallas_call(
        _dispatch_kernel,
        grid_spec=pltpu.PrefetchScalarGridSpec(
            num_scalar_prefetch=6,
            grid=(n // tr,),
            in_specs=[pl.BlockSpec((tr, d), lambda j, *_: (j, 0)),
                      pl.BlockSpec((SUBLANES, tr), lambda j, *_: (0, j))],
            out_specs=pl.BlockSpec(memory_space=pl.ANY),
            scratch_shapes=[pltpu.VMEM((lrows, d), F32), pltpu.VMEM((MOE_SUB, d), F32),
                            pltpu.SemaphoreType.DMA(())]),
        out_shape=jax.ShapeDtypeStruct((_xs_rows(n), d), F32),
        compiler_params=_cparams("arbitrary"),
        name="moe_dispatch",
    )(*tables, gaps, used_tiles, hn, meta_t)


def _moe_kernel(te_ref, hi_ref, x_ref, wg_ref, wu_ref, wd_ref, o_ref, xb_ref):
    w = pl.program_id(0)
    f = pl.program_id(1)
    tm = x_ref.shape[0]
    sub = MOE_SUB
    sub_shift = int(math.log2(sub))
    hi = hi_ref[w]

    def swiglu_part(xb, wg, wu, wd):
        hg = jnp.dot(xb, wg, preferred_element_type=F32)
        hu = jnp.dot(xb, wu, preferred_element_type=F32)
        act = (_silu(hg) * hu).astype(BF16)
        return jnp.dot(act, wd, preferred_element_type=F32)

    @pl.when(jnp.logical_and(f == 0, hi > 0))
    def _():
        row = lax.broadcasted_iota(jnp.int32, (tm, 1), 0)
        xb_ref[...] = jnp.where(row < hi, x_ref[...], 0.0).astype(BF16)

    @pl.when(hi == tm)
    def _():
        part = swiglu_part(xb_ref[...], wg_ref[0].astype(BF16), wu_ref[0].astype(BF16),
                           wd_ref[0].astype(BF16))

        @pl.when(f == 0)
        def _():
            o_ref[...] = part

        @pl.when(f != 0)
        def _():
            o_ref[...] += part

    @pl.when(hi < tm)
    def _():
        @pl.when(f == 0)
        def _():
            o_ref[...] = jnp.zeros_like(o_ref)

        def sub_block(s, carry):
            rows = pl.ds(pl.multiple_of(s * sub, sub), sub)
            o_ref[rows, :] += swiglu_part(xb_ref[rows, :], wg_ref[0].astype(BF16),
                                          wu_ref[0].astype(BF16), wd_ref[0].astype(BF16))
            return carry

        lax.fori_loop(0, (hi + sub - 1) >> sub_shift, sub_block, 0)


def _moe_ffn(xs, tile_expert, tile_rows, wg, wu, wd):
    rows, d = xs.shape
    tm = MOE_TILE
    tf = MOE_FF_TILE
    nf = wg.shape[2] // tf

    def f_idx(f, hi):
        v = (hi > 0).astype(jnp.int32)
        return f * v + (nf - 1) * (1 - v)

    return pl.pallas_call(
        _moe_kernel,
        grid_spec=pltpu.PrefetchScalarGridSpec(
            num_scalar_prefetch=2,
            grid=(rows // tm, nf),
            in_specs=[pl.BlockSpec((tm, d), lambda w, f, te, hi: (w, 0)),
                      pl.BlockSpec((1, d, tf), lambda w, f, te, hi: (te[w], 0, f_idx(f, hi[w]))),
                      pl.BlockSpec((1, d, tf), lambda w, f, te, hi: (te[w], 0, f_idx(f, hi[w]))),
                      pl.BlockSpec((1, tf, d), lambda w, f, te, hi: (te[w], f_idx(f, hi[w]), 0))],
            out_specs=pl.BlockSpec((tm, d), lambda w, f, te, hi: (w, 0)),
            scratch_shapes=[pltpu.VMEM((tm, d), BF16)]),
        out_shape=jax.ShapeDtypeStruct((rows, d), F32),
        compiler_params=_cparams("arbitrary", "arbitrary"),
        name="moe_expert_swiglu",
    )(tile_expert, tile_rows, xs, wg, wu, wd)


def _combine_kernel(delta_ref, lstart_ref, run_ref, gstart_ref, ys_ref, x_ref, meta_ref, wt_ref, g2_ref, fw_ref,
                    o_ref, ybuf_ref, sem):
    j = pl.program_id(0)
    n_tiles = pl.num_programs(0)
    tr = x_ref.shape[0]
    lrows = ybuf_ref.shape[1]
    slot = lax.rem(j, 2)

    def group_copy(buf, local_row, xs_row):
        return pltpu.make_async_copy(ys_ref.at[pl.ds(xs_row, SUBLANES)],
                                     ybuf_ref.at[buf, pl.ds(local_row, SUBLANES)], sem.at[buf])

    def fetch(tile, buf):
        ybuf_ref[buf, 2 * tr:, :] = jnp.zeros((lrows - 2 * tr, ybuf_ref.shape[2]), F32)
        _for_each_group(tile, lstart_ref, run_ref, gstart_ref,
                        lambda lr, xr: group_copy(buf, lr, xr).start())

    @pl.when(j == 0)
    def _():
        fetch(0, 0)

    _for_each_group(j, lstart_ref, run_ref, gstart_ref, lambda lr, xr: group_copy(slot, lr, xr).wait())

    @pl.when(j + 1 < n_tiles)
    def _():
        fetch(j + 1, 1 - slot)

    meta = meta_ref[...]
    wt = wt_ref[...]
    lp1 = _local_pos(meta[:, 0:1], meta[:, 2:3], delta_ref, j)
    lp2 = _local_pos(meta[:, 1:2], meta[:, 3:4], delta_ref, j)
    l_idx = lax.broadcasted_iota(jnp.int32, (tr, lrows), 1)
    pick = jnp.where(l_idx == lp1, wt[:, 0:1], 0.0) + jnp.where(l_idx == lp2, wt[:, 1:2], 0.0)
    ffn = _bdot(pick, ybuf_ref[slot])
    x4 = x_ref[...] + g2_ref[0] * ffn
    o_ref[...] = (x4 * _rms_scale(x4)) * fw_ref[...]


def _combine(ys, tables, x3, meta, wt, g2, final_w, seq_len):
    n, d = x3.shape
    tr = TOKEN_TILE
    tps = seq_len // tr
    lrows = -(-_local_rows(tr) // LANES) * LANES
    return pl.pallas_call(
        _combine_kernel,
        grid_spec=pltpu.PrefetchScalarGridSpec(
            num_scalar_prefetch=4,
            grid=(n // tr,),
            in_specs=[pl.BlockSpec(memory_space=pl.ANY),
                      pl.BlockSpec((tr, d), lambda j, *_: (j, 0)),
                      pl.BlockSpec((tr, LANES), lambda j, *_: (j, 0)),
                      pl.BlockSpec((tr, LANES), lambda j, *_: (j, 0)),
                      pl.BlockSpec((1, 1, d), lambda j, *_: (j // tps, 0, 0)),
                      pl.BlockSpec((1, d), lambda j, *_: (0, 0))],
            out_specs=pl.BlockSpec((tr, d), lambda j, *_: (j, 0)),
            scratch_shapes=[pltpu.VMEM((2, lrows, d), F32), pltpu.SemaphoreType.DMA((2,))]),
        out_shape=jax.ShapeDtypeStruct((n, d), F32),
        compiler_params=_cparams("arbitrary"),
        name="moe_combine_final_norm",
    )(*tables, ys, x3, meta, wt, g2, final_w.reshape(1, d))


def _moe_tables(tile_base, counts, n_tokens):
    i32 = lambda t: t.astype(jnp.int32)
    tm = MOE_TILE
    before = tile_base[:, 0, :N_EXPERTS]
    total = counts[0, :N_EXPERTS]
    run = jnp.concatenate([before[1:], total[None]], axis=0) - before
    run = (run + SUBLANES - 1) // SUBLANES * SUBLANES
    l_end = jnp.cumsum(run, axis=1)
    l_start = l_end - run
    g_size = jnp.sum(run, axis=0)
    g_tiles = (g_size + tm - 1) // tm
    tile_end = jnp.cumsum(g_tiles)
    g_off = (tile_end - g_tiles) * tm
    g_end = g_off + g_size
    g_start = g_off[None, :] + jnp.cumsum(run, axis=0) - run
    delta = l_start - before
    gaps = jnp.concatenate([g_end, tile_end * tm - g_end])
    used_tiles = tile_end[-1:]

    w = jnp.arange(_xs_rows(n_tokens) // tm, dtype=jnp.int32)
    te = jnp.minimum(jnp.sum((tile_end[None, :] <= w[:, None]).astype(jnp.int32), axis=1), N_EXPERTS - 1)
    rows = jnp.where(w < tile_end[-1], jnp.clip(jnp.take(g_end, te) - w * tm, 0, tm), 0)
    flat = lambda t: i32(t).reshape(-1)
    return (flat(delta), flat(l_start), flat(run), flat(g_start)), i32(gaps), i32(used_tiles), i32(te), i32(rows)


def kernel(x, c, rel_bias, ada_w, ada_b, norm_mix_w, norm_ffn_w, final_norm_w, ab_w_in, attn_sinks,
           dn_conv_w, dn_a_log, dn_dt_bias, dn_norm_w, ab_w_out, ffn_w_gate, ffn_w_up, ffn_w_down,
           cd_w_in, lru_conv_w, lru_conv_b, lru_gate_a_w, lru_gate_a_b, lru_gate_x_w, lru_gate_x_b,
           lru_lambda, sconv_w, cd_w_out, moe_router_w, moe_router_b, moe_w_gate, moe_w_up, moe_w_down):
    bsz, seq_len, d = x.shape
    n = bsz * seq_len
    x2d = x.reshape(n, d)
    mods = _ada_mods(c, ada_w, ada_b)

    sh1, sc1, g1, sh2, sc2, g2 = (mods[0, k] for k in range(6))
    qa, kd, vd, qn, kn, vb, gs, bexp, gcexp = _in_proj0(
        x2d, norm_mix_w[0], sc1, sh1, ab_w_in[0], dn_conv_w[0], dn_a_log[0], dn_dt_bias[0], seq_len)
    attn = _attention(qa, kd, vd, _bias_table(rel_bias), attn_sinks[0], seq_len)
    dn = _deltanet(qn, kn, vb, gs, bexp, gcexp, dn_norm_w[0], seq_len)
    x2 = _mid0(attn, dn, x2d, ab_w_out[0], g1, norm_ffn_w[0], sc2, sh2, g2,
               ffn_w_gate[0], ffn_w_up[0], ffn_w_down[0], seq_len)

    sh1, sc1, g1, sh2, sc2, g2 = (mods[1, k] for k in range(6))
    cat = _mix1(x2, norm_mix_w[1], sc1, sh1, cd_w_in[0], lru_conv_w[0], lru_conv_b[0],
                lru_gate_a_w[0], lru_gate_a_b[0], lru_gate_x_w[0], lru_gate_x_b[0],
                lru_lambda[0], sconv_w[0], seq_len)
    x3, hn4, meta_t, meta, wt, tile_base, counts = _route(
        cat, x2, cd_w_out[0], g1, norm_ffn_w[1], sc2, sh2, moe_router_w[0], moe_router_b[0], seq_len)
    tables, gaps, used_tiles, tile_expert, tile_rows = _moe_tables(tile_base, counts, n)
    xs = _dispatch(hn4, meta_t, tables, gaps, used_tiles)
    ys = _moe_ffn(xs, tile_expert, tile_rows, moe_w_gate[0], moe_w_up[0], moe_w_down[0])
    out = _combine(ys, tables, x3, meta, wt, g2, final_norm_w, seq_len)
    return out.reshape(bsz, seq_len, d)
```

```python
import functools
import math

import numpy as np
import jax
import jax.numpy as jnp
from jax import lax
from jax.experimental import pallas as pl
from jax.experimental.pallas import tpu as pltpu

D_MODEL = 1024
EPS = 1e-6
HEAD_DIM = 64
A_Q_HEADS = 8
A_KV_HEADS = 2
WINDOW = 128
N_BUCKETS = 32
MAX_DISTANCE = 128
B_HEADS = 8
B_CONV = 4
CHUNK = 64
A_Q_W = A_Q_HEADS * HEAD_DIM
A_KV_W = A_KV_HEADS * HEAD_DIM
B_W = B_HEADS * HEAD_DIM
B_QKV_W = 3 * B_W
LRU_WIDTH = D_MODEL
LRU_BLOCKS = 8
LRU_C = 8.0
SC_WIDTH = D_MODEL // 2
D_FF = 2816
N_EXPERTS = 8
D_FF_EXPERT = 3584

LANES = 128
SUBLANES = 8
VMEM_LIMIT_BYTES = 56 * 1024 * 1024
TOKEN_TILE = 512
MOE_TILE = 1024
MOE_SUB = 256
MOE_FF_TILE = 512
NEG_BIG = -1e30

F32 = jnp.float32
BF16 = jnp.bfloat16


def _cparams(*sem):
    return pltpu.CompilerParams(dimension_semantics=tuple(sem), vmem_limit_bytes=VMEM_LIMIT_BYTES)


def _const_spec(shape):
    nd = len(shape)
    return pl.BlockSpec(shape, lambda *_: (0,) * nd)


def _bdot(a, b):
    return jnp.dot(a.astype(BF16), b.astype(BF16), preferred_element_type=F32)


def _bdot_nt(a, b):
    return lax.dot_general(a.astype(BF16), b.astype(BF16), (((1,), (1,)), ((), ())),
                           preferred_element_type=F32)


def _bdot_tn(a, b):
    return lax.dot_general(a.astype(BF16), b.astype(BF16), (((0,), (0,)), ((), ())),
                           preferred_element_type=F32)


def _split(x, n):
    parts = []
    r = x
    for i in range(n):
        p = r.astype(BF16)
        parts.append(p)
        if i + 1 < n:
            r = r - p.astype(F32)
    return parts


def _dot_x(a, b, na=2, nb=2):
    asp = _split(a, na) if na > 1 else [a.astype(BF16)]
    bsp = _split(b, nb) if nb > 1 else [b.astype(BF16)]
    acc = None
    for i, ai in enumerate(asp):
        for j, bj in enumerate(bsp):
            if i + j >= max(na, nb):
                continue
            t = jnp.dot(ai, bj, preferred_element_type=F32)
            acc = t if acc is None else acc + t
    return acc


def _silu(x):
    return x * (1.0 / (1.0 + jnp.exp(-x)))


def _sigmoid(x):
    return 1.0 / (1.0 + jnp.exp(-x))


def _log1p(z):
    u = 1.0 + z
    tiny = u == 1.0
    return jnp.where(tiny, z, jnp.log(u) * (z / jnp.where(tiny, 1.0, u - 1.0)))


def _softplus(x):
    return jnp.maximum(x, 0.0) + _log1p(jnp.exp(-jnp.abs(x)))


def _neg_expm1(y):
    return -jnp.tanh(0.5 * y) * (jnp.exp(y) + 1.0)


def _rms_scale(x):
    width = x.shape[1]
    mean_w = jnp.full((width, LANES), 1.0 / width, BF16)
    ms = _dot_x(x * x, mean_w, 2, 1)
    r = lax.rsqrt(ms + EPS)
    return jnp.concatenate([r] * (width // LANES), axis=1)


def _norm_mod(x, w, sc, sh, on_mxu=False):
    if on_mxu:
        scale = _rms_scale(x)
    else:
        scale = lax.rsqrt(jnp.mean(x * x, axis=-1, keepdims=True) + EPS)
    return (x * scale) * w * (1.0 + sc) + sh


def _shift_rows(x, k, prev_tail):
    n, width = x.shape
    x3 = x.reshape(n // SUBLANES, SUBLANES, width)
    rot = pltpu.roll(x3, k, 1)
    rot_prev = jnp.concatenate([pltpu.roll(prev_tail, k, 0)[None], rot[:-1]], axis=0)
    sub = lax.broadcasted_iota(jnp.int32, x3.shape, 1)
    return jnp.where(sub >= k, rot, rot_prev).reshape(n, width)


def _ada_kernel(c_ref, w_ref, b_ref, o_ref):
    c = c_ref[...]
    cond = _silu(c)
    o_ref[0] = _dot_x(cond, w_ref[0], 3, 2) + b_ref[0]


def _ada_mods(c, ada_w, ada_b):
    depth, d, six_d = ada_w.shape
    bsz = c.shape[0]
    rows = max(SUBLANES, bsz)
    c_pad = jnp.zeros((rows, d), F32).at[:bsz].set(c)
    tn = 1536
    out = pl.pallas_call(
        _ada_kernel,
        grid=(depth, six_d // tn),
        in_specs=[pl.BlockSpec((rows, d), lambda l, j: (0, 0)),
                  pl.BlockSpec((1, d, tn), lambda l, j: (l, 0, j)),
                  pl.BlockSpec((1, 1, tn), lambda l, j: (l, 0, j))],
        out_specs=pl.BlockSpec((1, rows, tn), lambda l, j: (l, 0, j)),
        out_shape=jax.ShapeDtypeStruct((depth, rows, six_d), F32),
        compiler_params=_cparams("parallel", "parallel"),
        name="ada_mods",
    )(c_pad, ada_w, ada_b.reshape(depth, 1, six_d))
    return out[:, :bsz].reshape(depth, bsz, 6, 1, d).transpose(0, 2, 1, 3, 4)


def _t5_bucket(dist):
    max_exact = N_BUCKETS // 2
    d = np.maximum(dist, 0)
    large = max_exact + (np.log(np.maximum(d, 1) / max_exact) / math.log(MAX_DISTANCE / max_exact)
                         * (N_BUCKETS - max_exact)).astype(np.int32)
    large = np.minimum(large, N_BUCKETS - 1)
    return np.where(d < max_exact, d, large).astype(np.int32)


def _band_buckets():
    qi = np.arange(WINDOW)[:, None]
    s = np.arange(2 * WINDOW)[None, :]
    dist = qi + WINDOW - s
    in_window = (dist >= 0) & (dist < WINDOW)
    return np.where(in_window, _t5_bucket(dist), -1).astype(np.int32)


def _bias_kernel(rb_ref, bucket_ref, o_ref):
    h = pl.program_id(0)
    bucket = bucket_ref[...]
    acc = jnp.zeros(bucket.shape, F32)
    for b in range(N_BUCKETS):
        acc = jnp.where(bucket == b, rb_ref[b, h], acc)
    o_ref[0] = jnp.where(bucket < 0, NEG_BIG, acc)


def _bias_table(rel_bias):
    bucket = jnp.asarray(_band_buckets())
    out = pl.pallas_call(
        _bias_kernel,
        grid=(A_Q_HEADS,),
        in_specs=[pl.BlockSpec(memory_space=pltpu.SMEM),
                  _const_spec((WINDOW, 2 * WINDOW))],
        out_specs=pl.BlockSpec((1, WINDOW, 2 * WINDOW), lambda h: (h, 0, 0)),
        out_shape=jax.ShapeDtypeStruct((A_Q_HEADS, WINDOW, 2 * WINDOW), F32),
        compiler_params=_cparams("parallel"),
        name="attn_bias_table",
    )(rel_bias, bucket)
    return out.reshape(A_Q_HEADS // 2, 2 * WINDOW, 2 * WINDOW)


_C_QA = 0
_C_KA = _C_QA + A_Q_W
_C_VA = _C_KA + A_KV_W
_C_QKV = _C_VA + A_KV_W
_C_GATE = _C_QKV + B_QKV_W
_C_SMALL = _C_GATE + B_W
_AB_COLS = _C_SMALL + LANES


def _ab_in_weight(w_in):
    return jnp.pad(w_in, ((0, 0), (0, _AB_COLS - w_in.shape[1]))).astype(BF16)


def _dup_heads(t, low):
    swapped = pltpu.roll(t, HEAD_DIM, 1)
    return jnp.concatenate([jnp.where(low, t, swapped), jnp.where(low, swapped, t)], axis=1)


def _chunk_tril(tm):
    r = np.arange(tm)
    return ((r[:, None] >= r[None, :]) & (r[:, None] // CHUNK == r[None, :] // CHUNK)).astype(np.float32)


def _head_selector():
    e = np.zeros((B_W, LANES), np.float32)
    for h in range(B_HEADS):
        e[h * HEAD_DIM:(h + 1) * HEAD_DIM, h] = 1.0
    return e


def _in0_kernel(x_ref, nw_ref, sc_ref, sh_ref, w_ref, cw_ref, sel_ref, selt_ref, tril_ref, alog_ref, dtb_ref,
                qa_ref, kd_ref, vd_ref, qn_ref, kn_ref, vb_ref, gs_ref, bexp_ref, gcexp_ref,
                tail_ref, *, tiles_per_seq):
    i = pl.program_id(0)

    @pl.when(i % tiles_per_seq == 0)
    def _():
        tail_ref[...] = jnp.zeros_like(tail_ref)

    hn = _norm_mod(x_ref[...], nw_ref[...], sc_ref[0], sh_ref[0])
    proj = jnp.dot(hn.astype(BF16), w_ref[...], preferred_element_type=F32)
    tm = proj.shape[0]
    low = lax.broadcasted_iota(jnp.int32, (tm, LANES), 1) < HEAD_DIM

    small = proj[:, _C_SMALL:]
    lane = lax.broadcasted_iota(jnp.int32, small.shape, 1)
    beta = jnp.where(lane < B_HEADS, _sigmoid(small), 0.0)
    dec = pltpu.roll(small, LANES - B_HEADS, 1)
    g = jnp.where(lane < B_HEADS, -jnp.exp(alog_ref[...]) * _softplus(dec + dtb_ref[...]), 0.0)
    bexp_ref[...] = _dot_x(beta, selt_ref[...], 2, 1)
    gc = _dot_x(tril_ref[...], g, 1, 3)
    gcexp_ref[...] = _dot_x(gc, selt_ref[...], 3, 1)

    qa_ref[...] = proj[:, _C_QA:_C_KA].astype(BF16)
    kd_ref[...] = _dup_heads(proj[:, _C_KA:_C_VA], low).astype(BF16)
    vd_ref[...] = _dup_heads(proj[:, _C_VA:_C_QKV], low).astype(BF16)

    def conv_silu(block):
        cols = slice(block * B_W, (block + 1) * B_W)
        xq = proj[:, _C_QKV + block * B_W:_C_QKV + (block + 1) * B_W]
        tail = tail_ref[:, cols]
        cw = cw_ref[:, cols]
        y = xq * cw[B_CONV - 1:B_CONV]
        for k in range(1, B_CONV):
            y = y + _shift_rows(xq, k, tail) * cw[B_CONV - 1 - k:B_CONV - k]
        tail_ref[:, cols] = xq[tm - SUBLANES:]
        return _silu(y)

    def head_scale(t):
        ssq = _dot_x(t * t, sel_ref[...], 2, 1)
        r = lax.rsqrt(ssq + EPS)
        return _dot_x(r, selt_ref[...], 2, 1)

    q = conv_silu(0)
    q_scale = head_scale(q)
    k_ = conv_silu(1)
    k_scale = head_scale(k_)
    vb_ref[...] = conv_silu(2)
    gs_ref[...] = _silu(proj[:, _C_GATE:_C_SMALL])
    qn_ref[...] = q * q_scale * (HEAD_DIM ** -0.5)
    kn_ref[...] = k_ * k_scale


def _in_proj0(x2d, nw, sc, sh, w_in, conv_w, a_log, dt_bias, seq_len):
    n, d = x2d.shape
    tm = TOKEN_TILE
    tiles_per_seq = seq_len // tm
    w = _ab_in_weight(w_in)
    sel = jnp.asarray(_head_selector(), BF16)
    selt = jnp.asarray(_head_selector().T.copy(), BF16)
    tril = jnp.asarray(_chunk_tril(tm), BF16)
    pad8 = lambda v: jnp.zeros((1, LANES), F32).at[0, :B_HEADS].set(v)
    row = lambda width: pl.BlockSpec((tm, width), lambda i: (i, 0))
    per_b = pl.BlockSpec((1, 1, d), lambda i: (i // tiles_per_seq, 0, 0))
    outs = pl.pallas_call(
        functools.partial(_in0_kernel, tiles_per_seq=tiles_per_seq),
        grid=(n // tm,),
        in_specs=[row(d), _const_spec((1, d)), per_b, per_b,
                  _resident_spec((d, _AB_COLS)), _const_spec((B_CONV, B_QKV_W)),
                  _const_spec((B_W, LANES)), _const_spec((LANES, B_W)), _const_spec((tm, tm)),
                  _const_spec((1, LANES)), _const_spec((1, LANES))],
        out_specs=[row(A_Q_W), row(2 * A_KV_W), row(2 * A_KV_W)] + [row(B_W)] * 6,
        out_shape=[jax.ShapeDtypeStruct((n, A_Q_W), BF16),
                   jax.ShapeDtypeStruct((n, 2 * A_KV_W), BF16),
                   jax.ShapeDtypeStruct((n, 2 * A_KV_W), BF16)]
        + [jax.ShapeDtypeStruct((n, B_W), F32)] * 6,
        scratch_shapes=[pltpu.VMEM((SUBLANES, B_QKV_W), F32)],
        compiler_params=_cparams("arbitrary"),
        name="in_proj0",
    )(x2d, nw.reshape(1, d), sc, sh, w, conv_w, sel, selt, tril, pad8(a_log), pad8(dt_bias))
    return outs


_ATTN_BLOCKS = 2


def _attn_kernel(sink_ref, q_ref, kp_ref, kc_ref, vp_ref, vc_ref, bm_ref, o_ref, *, steps_per_seq):
    i = pl.program_id(0)
    first = (i % steps_per_seq) == 0
    w = WINDOW
    lane = lax.broadcasted_iota(jnp.int32, (w, LANES), 1)
    low = lane < HEAD_DIM
    col = lax.broadcasted_iota(jnp.int32, (2 * w, 2 * w), 1)
    row = lax.broadcasted_iota(jnp.int32, (2 * w, 1), 0)
    prev_dead = jnp.logical_and(first, col < w)
    zero = jnp.zeros((), q_ref.dtype)
    pairs = A_Q_HEADS // 2
    units = [(b, j) for b in range(_ATTN_BLOCKS) for j in range(pairs)]

    def keys(p_ref, c_ref, b, kh):
        ls = slice(kh * LANES, (kh + 1) * LANES)
        before = p_ref[:, ls] if b == 0 else c_ref[(b - 1) * w:b * w, ls]
        return jnp.concatenate([before, c_ref[b * w:(b + 1) * w, ls]], axis=0)

    kv_of = lambda j: (2 * j) // (A_Q_HEADS // A_KV_HEADS)
    qp = [q_ref[b * w:(b + 1) * w, j * LANES:(j + 1) * LANES] for b, j in units]
    qs = [jnp.concatenate([jnp.where(low, t, zero), jnp.where(low, zero, t)], axis=0) for t in qp]
    kd = [keys(kp_ref, kc_ref, b, kv_of(j)) for b, j in units]
    vd = [keys(vp_ref, vc_ref, b, kv_of(j)) for b, j in units]
    s = [lax.dot_general(a, k, (((1,), (1,)), ((), ())), preferred_element_type=F32) for a, k in zip(qs, kd)]
    s = [t * (HEAD_DIM ** -0.5) + bm_ref[j] for t, (b, j) in zip(s, units)]
    s = [jnp.where(prev_dead, NEG_BIG, t) if b == 0 else t for t, (b, j) in zip(s, units)]
    sink = [jnp.where(row < w, sink_ref[2 * j], sink_ref[2 * j + 1]) for b, j in units]
    m = [jnp.maximum(jnp.max(t, axis=-1, keepdims=True), sk) for t, sk in zip(s, sink)]
    p = [jnp.exp(t - mt) for t, mt in zip(s, m)]
    denom = [jnp.sum(t, axis=-1, keepdims=True) + jnp.exp(sk - mt) for t, sk, mt in zip(p, sink, m)]
    pv = [jnp.dot(t.astype(BF16), v, preferred_element_type=F32) / dn for t, v, dn in zip(p, vd, denom)]
    outs = [jnp.where(low, t[:w], t[w:]) for t in pv]
    for b in range(_ATTN_BLOCKS):
        o_ref[b * w:(b + 1) * w, :] = jnp.concatenate(outs[b * pairs:(b + 1) * pairs], axis=1).astype(o_ref.dtype)


def _attention(qa, kd, vd, bias_tbl, sinks, seq_len):
    n = qa.shape[0]
    w = WINDOW
    rows = _ATTN_BLOCKS * w
    steps = seq_len // rows
    cur = lambda i: (i, 0)
    prev = lambda i: (jnp.where(i % steps == 0, i * _ATTN_BLOCKS, i * _ATTN_BLOCKS - 1), 0)
    return pl.pallas_call(
        functools.partial(_attn_kernel, steps_per_seq=steps),
        grid=(n // rows,),
        in_specs=[pl.BlockSpec(memory_space=pltpu.SMEM),
                  pl.BlockSpec((rows, A_Q_W), cur),
                  pl.BlockSpec((w, 2 * A_KV_W), prev), pl.BlockSpec((rows, 2 * A_KV_W), cur),
                  pl.BlockSpec((w, 2 * A_KV_W), prev), pl.BlockSpec((rows, 2 * A_KV_W), cur),
                  _const_spec((A_Q_HEADS // 2, 2 * w, 2 * w))],
        out_specs=pl.BlockSpec((rows, A_Q_W), cur),
        out_shape=jax.ShapeDtypeStruct((n, A_Q_W), BF16),
        compiler_params=_cparams("parallel"),
        name="swa_attention",
    )(sinks, qa, kd, kd, vd, vd, bias_tbl)


_DN_PAIRS = B_HEADS // 2
_DN_INV_BLOCK = 16
_DN_GROUP = 4


def _block_diag(x, low):
    zero = jnp.zeros((), x.dtype)
    return jnp.concatenate([jnp.where(low, x, zero), jnp.where(low, zero, x)], axis=0)


def _dn_intra(chunks, data_refs, work_refs, consts):
    qn_ref, kn_ref, vb_ref, bexp_ref, gcexp_ref = data_refs
    u_ref, w_ref, qk_ref, qd_ref, kd_ref, egl_ref = work_refs
    low, i_idx, j_idx, ones3 = consts
    c = CHUNK
    units = [(ci, p) for ci in chunks for p in range(_DN_PAIRS)]
    where = [(slice(ci * c, (ci + 1) * c), slice(p * LANES, (p + 1) * LANES)) for ci, p in units]
    causal = i_idx >= j_idx
    strict = i_idx > j_idx
    on_diag = i_idx == j_idx
    eye = on_diag.astype(F32)
    blk_shift = int(math.log2(_DN_INV_BLOCK))
    same_blk = (i_idx >> blk_shift) == (j_idx >> blk_shift)

    q = [qn_ref[rs, ls] for rs, ls in where]
    k = [kn_ref[rs, ls] for rs, ls in where]
    v = [vb_ref[rs, ls] for rs, ls in where]
    b = [bexp_ref[rs, ls] for rs, ls in where]
    gc = [gcexp_ref[rs, ls] for rs, ls in where]

    gr = [jnp.dot(ones3, jnp.concatenate(_split(jnp.where(on_diag, t, 0.0), 3), axis=0),
                  preferred_element_type=F32) for t in gc]
    ks = [_block_diag(t.astype(BF16), low) for t in k]
    qkk = [lax.dot_general(jnp.concatenate([qt, kt], axis=0).astype(BF16), kst,
                           (((1,), (1,)), ((), ())), preferred_element_type=F32)
           for qt, kt, kst in zip(q, k, ks)]
    decay = [jnp.exp(jnp.where(causal, gct - grt, NEG_BIG)) for gct, grt in zip(gc, gr)]
    lmat = [jnp.where(strict, bt * t[c:] * dt, 0.0) for bt, t, dt in zip(b, qkk, decay)]
    qk = [jnp.where(causal, t[:c] * dt, 0.0) for t, dt in zip(qkk, decay)]

    def mm(xs, ys):
        return [_bdot(x, _block_diag(y.astype(BF16), low)) for x, y in zip(xs, ys)]

    l_diag = [jnp.where(same_blk, t, 0.0) for t in lmat]
    l_off = [t - d for t, d in zip(lmat, l_diag)]
    pw = [-t for t in l_diag]
    d_inv = [eye + t for t in pw]
    for _ in range(blk_shift - 1):
        pw = mm(pw, pw)
        d_inv = mm(d_inv, [eye + t for t in pw])
    pw = [-t for t in mm(d_inv, l_off)]
    acc = [eye + t for t in pw]
    for _ in range(int(math.log2(c // _DN_INV_BLOCK)) - 1):
        pw = mm(pw, pw)
        acc = mm(acc, [eye + t for t in pw])
    tmat = mm(acc, d_inv)

    egc = [jnp.exp(t) for t in gc]
    rhs = [jnp.concatenate([_block_diag((vt * bt).astype(BF16), low),
                            _block_diag((kt * (bt * et)).astype(BF16), low)], axis=1)
           for vt, kt, bt, et in zip(v, k, b, egc)]
    uw = [_bdot(t, r) for t, r in zip(tmat, rhs)]
    for n, (ci, p) in enumerate(units):
        g_last = gc[n][c - 1:c, :]
        u_ref[ci, p] = uw[n][:, :LANES]
        w_ref[ci, p] = uw[n][:, LANES:]
        qk_ref[ci, p] = qk[n]
        qd_ref[ci, p] = q[n] * egc[n]
        kd_ref[ci, p] = k[n] * jnp.exp(g_last - gc[n])
        egl_ref[ci, p] = jnp.broadcast_to(jnp.exp(g_last), (SUBLANES, LANES))


def _dn_scan(ci, work_refs, s_ref, gs_ref, nw, o_ref, consts):
    u_ref, w_ref, qk_ref, qd_ref, kd_ref, egl_ref = work_refs
    low, mask_bd, head_mean2 = consts
    c = CHUNK
    rows = slice(ci * c, (ci + 1) * c)
    pairs = range(_DN_PAIRS)
    s_old = [s_ref[p] for p in pairs]
    wq = [_bdot(jnp.concatenate([w_ref[ci, p], qd_ref[ci, p]], axis=0), s_old[p]) for p in pairs]
    v_new = [u_ref[ci, p] - wq[p][:c] for p in pairs]
    o = [wq[p][c:] + _bdot(qk_ref[ci, p], _block_diag(v_new[p].astype(BF16), low)) for p in pairs]
    kv = [_bdot_tn(kd_ref[ci, p], v_new[p]) for p in pairs]
    for p in pairs:
        s_ref[p] = s_old[p] * egl_ref[ci, p][0:1, :] + jnp.where(mask_bd, kv[p], 0.0)
    ms = [jnp.dot(jnp.concatenate(_split(t * t, 2), axis=1), head_mean2, preferred_element_type=F32)
          for t in o]
    for p in pairs:
        ls = slice(p * LANES, (p + 1) * LANES)
        y = (o[p] * lax.rsqrt(ms[p] + EPS)) * nw * gs_ref[rows, ls]
        o_ref[rows, ls] = y.astype(o_ref.dtype)


def _dn_kernel(qn_ref, kn_ref, vb_ref, gs_ref, bexp_ref, gcexp_ref, nw_ref, o_ref,
               s_ref, u_ref, w_ref, qk_ref, qd_ref, kd_ref, egl_ref, *, groups_per_seq):
    i = pl.program_id(0)

    @pl.when(i % groups_per_seq == 0)
    def _():
        s_ref[...] = jnp.zeros_like(s_ref)

    c = CHUNK
    tm = o_ref.shape[0]
    n_chunks = tm // c
    lane = lax.broadcasted_iota(jnp.int32, (c, LANES), 1)
    low = lane < HEAD_DIM
    i_idx = lax.broadcasted_iota(jnp.int32, (c, LANES), 0)
    j_idx = lane & (c - 1)
    ones3 = jnp.ones((c, 3 * c), BF16)
    rb = lax.broadcasted_iota(jnp.int32, (LANES, LANES), 0)
    cb = lax.broadcasted_iota(jnp.int32, (LANES, LANES), 1)
    mask_bd = (rb < HEAD_DIM) == (cb < HEAD_DIM)
    head_mean = jnp.where(mask_bd, 1.0 / HEAD_DIM, 0.0).astype(BF16)
    head_mean2 = jnp.concatenate([head_mean, head_mean], axis=0)
    data_refs = (qn_ref, kn_ref, vb_ref, bexp_ref, gcexp_ref)
    work_refs = (u_ref, w_ref, qk_ref, qd_ref, kd_ref, egl_ref)
    intra_consts = (low, i_idx, j_idx, ones3)
    scan_consts = (low, mask_bd, head_mean2)
    nw = nw_ref[...]

    groups = [list(range(s, s + _DN_GROUP)) for s in range(0, n_chunks, _DN_GROUP)]
    _dn_intra(groups[0], data_refs, work_refs, intra_consts)
    for j, grp in enumerate(groups):
        if j + 1 < len(groups):
            _dn_intra(groups[j + 1], data_refs, work_refs, intra_consts)
        for ci in grp:
            _dn_scan(ci, work_refs, s_ref, gs_ref, nw, o_ref, scan_consts)


def _deltanet(qn, kn, vb, gs, bexp, gcexp, norm_w, seq_len):
    n = qn.shape[0]
    tm = TOKEN_TILE
    nw2 = jnp.concatenate([norm_w, norm_w]).reshape(1, LANES)
    row = lambda width: pl.BlockSpec((tm, width), lambda i: (i, 0))
    return pl.pallas_call(
        functools.partial(_dn_kernel, groups_per_seq=seq_len // tm),
        grid=(n // tm,),
        in_specs=[row(B_W)] * 6 + [_const_spec((1, LANES))],
        out_specs=row(B_W),
        out_shape=jax.ShapeDtypeStruct((n, B_W), BF16),
        scratch_shapes=[pltpu.VMEM((_DN_PAIRS, LANES, LANES), F32)]
        + [pltpu.VMEM((tm // CHUNK, _DN_PAIRS, CHUNK, LANES), F32)] * 5
        + [pltpu.VMEM((tm // CHUNK, _DN_PAIRS, SUBLANES, LANES), F32)],
        compiler_params=_cparams("arbitrary"),
        name="gated_deltanet",
    )(qn, kn, vb, gs, bexp, gcexp, nw2)


def _resident_spec(shape):
    nd = len(shape)
    return pl.BlockSpec(shape, lambda *_: (0,) * nd, pipeline_mode=pl.Buffered(1))


def _mid0_kernel(attn_ref, dn_ref, x_ref, wo_ref, g1_ref, nw_ref, sc_ref, sh_ref, g2_ref,
                 wg_ref, wu_ref, wd_ref, o_ref):
    mix = (jnp.dot(attn_ref[...], wo_ref[:A_Q_W], preferred_element_type=F32)
           + jnp.dot(dn_ref[...], wo_ref[A_Q_W:], preferred_element_type=F32))
    x1 = x_ref[...] + g1_ref[0] * mix
    hn = _norm_mod(x1, nw_ref[...], sc_ref[0], sh_ref[0]).astype(BF16)
    hg = jnp.dot(hn, wg_ref[...], preferred_element_type=F32)
    hu = jnp.dot(hn, wu_ref[...], preferred_element_type=F32)
    act = (_silu(hg) * hu).astype(BF16)
    o_ref[...] = x1 + g2_ref[0] * jnp.dot(act, wd_ref[...], preferred_element_type=F32)


def _mid0(attn, dn, x2d, w_out, g1, nw, sc, sh, g2, wg, wu, wd, seq_len):
    n, d = x2d.shape
    tm = TOKEN_TILE
    tps = seq_len // tm
    row = lambda width: pl.BlockSpec((tm, width), lambda i: (i, 0))
    per_b = pl.BlockSpec((1, 1, d), lambda i: (i // tps, 0, 0))
    return pl.pallas_call(
        _mid0_kernel,
        grid=(n // tm,),
        in_specs=[row(A_Q_W), row(B_W), row(d), _resident_spec(w_out.shape), per_b,
                  _const_spec((1, d)), per_b, per_b, per_b,
                  _resident_spec(wg.shape), _resident_spec(wu.shape), _resident_spec(wd.shape)],
        out_specs=row(d),
        out_shape=jax.ShapeDtypeStruct((n, d), F32),
        compiler_params=_cparams("parallel"),
        name="out_proj0_swiglu",
    )(attn, dn, x2d, w_out.astype(BF16), g1, nw.reshape(1, d), sc, sh, g2,
      wg.astype(BF16), wu.astype(BF16), wd.astype(BF16))


def _gelu_tanh(x):
    return 0.5 * x * (1.0 + jnp.tanh(math.sqrt(2.0 / math.pi) * (x + 0.044715 * (x * x * x))))


def _linear_scan(a, b, h0):
    n, width = a.shape
    groups = n // SUBLANES
    a = a.reshape(groups, SUBLANES, width)
    b = b.reshape(groups, SUBLANES, width)
    in_group = lax.broadcasted_iota(jnp.int32, a.shape, 1)
    s = 1
    while s < SUBLANES:
        a_sh = pltpu.roll(a, s, 1)
        b_sh = pltpu.roll(b, s, 1)
        valid = in_group >= s
        b = jnp.where(valid, a * b_sh + b, b)
        a = jnp.where(valid, a * a_sh, a)
        s *= 2
    carry = jnp.broadcast_to(h0, (SUBLANES, width))
    out = []
    for g in range(groups):
        hg = a[g] * carry + b[g]
        out.append(hg)
        carry = jnp.broadcast_to(hg[SUBLANES - 1:SUBLANES, :], hg.shape)
    return jnp.concatenate(out, axis=0)


def _mix1_tile(x, nw, sc, sh, w_ref, cw_ref, cb_ref, ga_ref, gab_ref, gx_ref, gxb_ref, lam_ref, sw_ref,
               tail_c_ref, tail_d_ref, h_ref):
    hn = _norm_mod(x, nw, sc, sh, on_mxu=True).astype(BF16)
    proj = jnp.dot(hn, w_ref[...], preferred_element_type=F32)
    w_l = LRU_WIDTH
    xc_in = proj[:, :w_l]
    yc = proj[:, w_l:2 * w_l]
    bd = proj[:, 2 * w_l:2 * w_l + SC_WIDTH]
    cd = proj[:, 2 * w_l + SC_WIDTH:2 * w_l + 2 * SC_WIDTH]
    hd = proj[:, 2 * w_l + 2 * SC_WIDTH:]
    tm = xc_in.shape[0]

    kc = cw_ref.shape[0]
    tail = tail_c_ref[...]
    cw = cw_ref[...]
    xc = xc_in * cw[kc - 1:kc] + cb_ref[...]
    for k in range(1, kc):
        xc = xc + _shift_rows(xc_in, k, tail) * cw[kc - 1 - k:kc - k]
    tail_c_ref[...] = xc_in[tm - SUBLANES:]

    xb = xc.astype(BF16)
    gw = ga_ref.shape[1]
    ra, ri = [], []
    for p in range(ga_ref.shape[0]):
        xin = xb[:, p * gw:(p + 1) * gw]
        ra.append(jnp.dot(xin, ga_ref[p], preferred_element_type=F32))
        ri.append(jnp.dot(xin, gx_ref[p], preferred_element_type=F32))
    r = _sigmoid(jnp.concatenate(ra, axis=1) + gab_ref[...])
    ig = _sigmoid(jnp.concatenate(ri, axis=1) + gxb_ref[...])
    log_a = (-LRU_C) * r * _softplus(-lam_ref[...])
    a = jnp.exp(log_a)
    b = jnp.sqrt(_neg_expm1(2.0 * log_a)) * (ig * xc)
    h = _linear_scan(a, b, h_ref[0:1, :])
    h_ref[...] = jnp.broadcast_to(h[tm - 1:tm, :], h_ref.shape)
    yc_out = h * _gelu_tanh(yc)

    ks = sw_ref.shape[0]
    ch = cd * hd
    tail_d = tail_d_ref[...]
    sw = sw_ref[...]
    conv = ch * sw[ks - 1:ks]
    for k in range(1, ks):
        conv = conv + _shift_rows(ch, k, tail_d) * sw[ks - 1 - k:ks - k]
    tail_d_ref[...] = ch[tm - SUBLANES:]
    return jnp.concatenate([yc_out, bd * conv], axis=1)


def _pair_block_diag(gw):
    nb, bw, _ = gw.shape
    g2 = gw.reshape(nb // 2, 2, bw, bw)
    z = jnp.zeros((nb // 2, bw, bw), gw.dtype)
    top = jnp.concatenate([g2[:, 0], z], axis=2)
    bot = jnp.concatenate([z, g2[:, 1]], axis=2)
    return jnp.concatenate([top, bot], axis=1).astype(BF16)


def _route_tile(cat, x, wo_ref, g1, nw, sc, sh, rw_ref, rb_ref, carry_ref):
    x3 = x + g1 * jnp.dot(cat, wo_ref[...], preferred_element_type=F32)
    hn = _norm_mod(x3, nw, sc, sh)
    tm = hn.shape[0]
    lane = lax.broadcasted_iota(jnp.int32, (tm, LANES), 1)
    logits = _dot_x(hn, rw_ref[...], 2, 2) + rb_ref[...]
    lg = jnp.where(lane < N_EXPERTS, logits, NEG_BIG)
    m1 = jnp.max(lg, axis=1, keepdims=True)
    i1 = jnp.min(jnp.where(lg == m1, lane, LANES), axis=1, keepdims=True)
    lg2 = jnp.where(lane == i1, NEG_BIG, lg)
    m2 = jnp.max(lg2, axis=1, keepdims=True)
    i2 = jnp.min(jnp.where(lg2 == m2, lane, LANES), axis=1, keepdims=True)
    e2 = jnp.exp(m2 - m1)
    w1 = 1.0 / (1.0 + e2)
    w2 = e2 / (1.0 + e2)

    hit1 = lane == i1
    hit2 = lane == i2
    sel = jnp.logical_or(hit1, hit2).astype(F32)
    r_i = lax.broadcasted_iota(jnp.int32, (tm, tm), 0)
    c_i = lax.broadcasted_iota(jnp.int32, (tm, tm), 1)
    tril = (r_i >= c_i).astype(BF16)
    incl = jnp.dot(tril, sel.astype(BF16), preferred_element_type=F32)
    carry = carry_ref[0:1, :]
    excl = incl - sel + carry
    r1 = jnp.sum(jnp.where(hit1, excl, 0.0), axis=1, keepdims=True)
    r2 = jnp.sum(jnp.where(hit2, excl, 0.0), axis=1, keepdims=True)
    total = carry + incl[tm - 1:tm, :]
    carry_ref[...] = jnp.broadcast_to(total, carry_ref.shape)

    meta = jnp.where(lane == 0, i1, 0)
    meta = jnp.where(lane == 1, i2, meta)
    meta = jnp.where(lane == 2, r1.astype(jnp.int32), meta)
    meta = jnp.where(lane == 3, r2.astype(jnp.int32), meta)
    wt = jnp.where(lane == 0, w1, jnp.where(lane == 1, w2, 0.0))
    return x3, hn, meta, wt, carry, total


def _mix1_kernel(x_ref, nw_ref, sc_ref, sh_ref, w_ref, cw_ref, cb_ref, ga_ref, gab_ref, gx_ref, gxb_ref,
                 lam_ref, sw_ref, o_ref, tail_c_ref, tail_d_ref, h_ref, *, tiles_per_seq):
    i = pl.program_id(0)

    @pl.when(i % tiles_per_seq == 0)
    def _():
        tail_c_ref[...] = jnp.zeros_like(tail_c_ref)
        tail_d_ref[...] = jnp.zeros_like(tail_d_ref)
        h_ref[...] = jnp.zeros_like(h_ref)

    cat = _mix1_tile(x_ref[...], nw_ref[...], sc_ref[0], sh_ref[0], w_ref, cw_ref, cb_ref, ga_ref,
                     gab_ref, gx_ref, gxb_ref, lam_ref, sw_ref, tail_c_ref, tail_d_ref, h_ref)
    o_ref[...] = cat.astype(o_ref.dtype)


def _mix1(x2d, nw, sc, sh, w_in, conv_w, conv_b, ga_w, ga_b, gx_w, gx_b, lam, sconv_w, seq_len):
    n, d = x2d.shape
    tm = TOKEN_TILE
    tps = seq_len // tm
    cd_in = w_in.shape[1]
    cd_out = LRU_WIDTH + SC_WIDTH
    row = lambda width: pl.BlockSpec((tm, width), lambda i: (i, 0))
    per_b = pl.BlockSpec((1, 1, d), lambda i: (i // tps, 0, 0))
    ga = _pair_block_diag(ga_w)
    gx = _pair_block_diag(gx_w)
    vec = lambda v: v.reshape(1, -1)
    return pl.pallas_call(
        functools.partial(_mix1_kernel, tiles_per_seq=tps),
        grid=(n // tm,),
        in_specs=[row(d), _const_spec((1, d)), per_b, per_b, _resident_spec((d, cd_in)),
                  _const_spec(conv_w.shape), _const_spec((1, LRU_WIDTH)),
                  _const_spec(ga.shape), _const_spec((1, LRU_WIDTH)),
                  _const_spec(gx.shape), _const_spec((1, LRU_WIDTH)),
                  _const_spec((1, LRU_WIDTH)), _const_spec(sconv_w.shape)],
        out_specs=row(cd_out),
        out_shape=jax.ShapeDtypeStruct((n, cd_out), BF16),
        scratch_shapes=[pltpu.VMEM((SUBLANES, LRU_WIDTH), F32), pltpu.VMEM((SUBLANES, SC_WIDTH), F32),
                        pltpu.VMEM((SUBLANES, LRU_WIDTH), F32)],
        compiler_params=_cparams("arbitrary"),
        name="rglru_shortconv_mixer",
    )(x2d, vec(nw), sc, sh, w_in.astype(BF16), conv_w, vec(conv_b), ga, vec(ga_b), gx, vec(gx_b),
      vec(lam), sconv_w)


def _route_kernel(cat_ref, x_ref, wo_ref, g1_ref, nw_ref, sc_ref, sh_ref, rw_ref, rb_ref,
                  x3_ref, hn_ref, metat_ref, meta_ref, wt_ref, base_ref, cnt_ref, carry_ref):
    @pl.when(pl.program_id(0) == 0)
    def _():
        carry_ref[...] = jnp.zeros_like(carry_ref)

    x3, hn, meta, wt, before, total = _route_tile(cat_ref[...], x_ref[...], wo_ref, g1_ref[0], nw_ref[...],
                                                  sc_ref[0], sh_ref[0], rw_ref, rb_ref, carry_ref)
    x3_ref[...] = x3
    hn_ref[...] = hn
    meta_ref[...] = meta
    metat_ref[...] = jnp.transpose(meta.astype(F32))[:SUBLANES].astype(jnp.int32)
    wt_ref[...] = wt
    base_ref[0] = jnp.broadcast_to(before, base_ref.shape[1:]).astype(jnp.int32)
    cnt_ref[...] = jnp.broadcast_to(total, cnt_ref.shape).astype(jnp.int32)


def _route(cat, x2d, w_out, g1, nw, sc, sh, router_w, router_b, seq_len):
    n, d = x2d.shape
    tm = TOKEN_TILE
    tps = seq_len // tm
    row = lambda width: pl.BlockSpec((tm, width), lambda i: (i, 0))
    per_b = pl.BlockSpec((1, 1, d), lambda i: (i // tps, 0, 0))
    rw = jnp.zeros((d, LANES), F32).at[:, :N_EXPERTS].set(router_w)
    rb = jnp.zeros((1, LANES), F32).at[0, :N_EXPERTS].set(router_b)
    return pl.pallas_call(
        _route_kernel,
        grid=(n // tm,),
        in_specs=[row(cat.shape[1]), row(d), _resident_spec(w_out.shape), per_b, _const_spec((1, d)),
                  per_b, per_b, _const_spec((d, LANES)), _const_spec((1, LANES))],
        out_specs=[row(d), row(d), pl.BlockSpec((SUBLANES, tm), lambda i: (0, i)), row(LANES), row(LANES),
                   pl.BlockSpec((1, SUBLANES, LANES), lambda i: (i, 0, 0)), _const_spec((SUBLANES, LANES))],
        out_shape=[jax.ShapeDtypeStruct((n, d), F32), jax.ShapeDtypeStruct((n, d), F32),
                   jax.ShapeDtypeStruct((SUBLANES, n), jnp.int32), jax.ShapeDtypeStruct((n, LANES), jnp.int32),
                   jax.ShapeDtypeStruct((n, LANES), F32),
                   jax.ShapeDtypeStruct((n // tm, SUBLANES, LANES), jnp.int32),
                   jax.ShapeDtypeStruct((SUBLANES, LANES), jnp.int32)],
        scratch_shapes=[pltpu.VMEM((SUBLANES, LANES), F32)],
        compiler_params=_cparams("arbitrary"),
        name="out_proj1_router",
    )(cat, x2d, w_out.astype(BF16), g1, nw.reshape(1, d), sc, sh, rw, rb)


def _local_rows(tr):
    return 2 * tr + N_EXPERTS * SUBLANES


def _xs_rows(n):
    worst = 2 * n + (n // TOKEN_TILE) * N_EXPERTS * (SUBLANES - 1)
    return (-(-worst // MOE_TILE) + N_EXPERTS) * MOE_TILE


def _local_pos(e_k, r_k, delta_ref, tile):
    shift = jnp.zeros_like(r_k)
    for e in range(N_EXPERTS):
        shift = jnp.where(e_k == e, delta_ref[tile * N_EXPERTS + e], shift)
    return r_k + shift


def _for_each_group(tile, lstart_ref, run_ref, gstart_ref, fn):
    shift = int(math.log2(SUBLANES))
    for e in range(N_EXPERTS):
        k = tile * N_EXPERTS + e
        l_start = lstart_ref[k]
        g_start = gstart_ref[k]

        def body(g, c, l_start=l_start, g_start=g_start):
            off = g * SUBLANES
            fn(pl.multiple_of(l_start + off, SUBLANES), pl.multiple_of(g_start + off, SUBLANES))
            return c

        lax.fori_loop(0, run_ref[k] >> shift, body, 0)


def _zero_fill_gaps(gap_ref, used_ref, xs_ref, zero_ref, sem):
    zero_ref[...] = jnp.zeros_like(zero_ref)
    zr = zero_ref.shape[0]
    per_tile = MOE_TILE // zr
    shift = int(math.log2(SUBLANES))

    def gap_copy(e, g):
        row = pl.multiple_of(gap_ref[e] + g * SUBLANES, SUBLANES)
        return pltpu.make_async_copy(zero_ref.at[pl.ds(0, SUBLANES)], xs_ref.at[pl.ds(row, SUBLANES)], sem)

    def tile_copy(k):
        row = pl.multiple_of(k * zr, zr)
        return pltpu.make_async_copy(zero_ref, xs_ref.at[pl.ds(row, zr)], sem)

    def both(op):
        for e in range(N_EXPERTS):
            lax.fori_loop(0, gap_ref[N_EXPERTS + e] >> shift, lambda g, c, e=e: (op(gap_copy(e, g)), c)[1], 0)
        lax.fori_loop(used_ref[0] * per_tile, (xs_ref.shape[0] // MOE_TILE) * per_tile,
                      lambda k, c: (op(tile_copy(k)), c)[1], 0)

    both(lambda cp: cp.start())
    both(lambda cp: cp.wait())


def _dispatch_kernel(delta_ref, lstart_ref, run_ref, gstart_ref, gap_ref, used_ref, hn_ref, meta_ref, xs_ref,
                     sbuf_ref, zero_ref, sem):
    j = pl.program_id(0)
    tr = hn_ref.shape[0]
    lrows = sbuf_ref.shape[1]
    slot = lax.rem(j, 2)

    @pl.when(j == 0)
    def _():
        _zero_fill_gaps(gap_ref, used_ref, xs_ref, zero_ref, sem.at[0])

    meta = meta_ref[...]
    lp1 = _local_pos(meta[0:1], meta[2:3], delta_ref, j)
    lp2 = _local_pos(meta[1:2], meta[3:4], delta_ref, j)
    r_idx = lax.broadcasted_iota(jnp.int32, (lrows, tr), 0)
    onehot = jnp.logical_or(r_idx == lp1, r_idx == lp2).astype(BF16)
    sbuf_ref[slot] = jnp.dot(onehot, hn_ref[...].astype(BF16), preferred_element_type=F32)

    def group_copy(buf, local_row, xs_row):
        return pltpu.make_async_copy(sbuf_ref.at[buf, pl.ds(local_row, SUBLANES)],
                                     xs_ref.at[pl.ds(xs_row, SUBLANES)], sem.at[buf])

    def drain(tile, buf):
        _for_each_group(tile, lstart_ref, run_ref, gstart_ref, lambda lr, xr: group_copy(buf, lr, xr).wait())

    _for_each_group(j, lstart_ref, run_ref, gstart_ref, lambda lr, xr: group_copy(slot, lr, xr).start())

    @pl.when(j > 0)
    def _():
        drain(j - 1, 1 - slot)

    @pl.when(j == pl.num_programs(0) - 1)
    def _():
        drain(j, slot)


def _dispatch(hn, meta_t, tables, gaps, used_tiles):
    n, d = hn.shape
    tr = TOKEN_TILE
    lrows = _local_rows(tr)
    return pl.pallas_call(
        _dispatch_kernel,
        grid_spec=pltpu.PrefetchScalarGridSpec(
            num_scalar_prefetch=6,
            grid=(n // tr,),
            in_specs=[pl.BlockSpec((tr, d), lambda j, *_: (j, 0)),
                      pl.BlockSpec((SUBLANES, tr), lambda j, *_: (0, j))],
            out_specs=pl.BlockSpec(memory_space=pl.ANY),
            scratch_shapes=[pltpu.VMEM((2, lrows, d), F32), pltpu.VMEM((MOE_SUB, d), F32),
                            pltpu.SemaphoreType.DMA((2,))]),
        out_shape=jax.ShapeDtypeStruct((_xs_rows(n), d), F32),
        compiler_params=_cparams("arbitrary"),
        name="moe_dispatch",
    )(*tables, gaps, used_tiles, hn, meta_t)


def _moe_kernel(te_ref, hi_ref, x_ref, wg_ref, wu_ref, wd_ref, o_ref, xb_ref):
    w = pl.program_id(0)
    f = pl.program_id(1)
    tm = x_ref.shape[0]
    sub = MOE_SUB
    sub_shift = int(math.log2(sub))
    hi = hi_ref[w]

    def swiglu_part(xb, wg, wu, wd):
        hg = jnp.dot(xb, wg, preferred_element_type=F32)
        hu = jnp.dot(xb, wu, preferred_element_type=F32)
        act = (_silu(hg) * hu).astype(BF16)
        return jnp.dot(act, wd, preferred_element_type=F32)

    @pl.when(jnp.logical_and(f == 0, hi > 0))
    def _():
        row = lax.broadcasted_iota(jnp.int32, (tm, 1), 0)
        xb_ref[...] = jnp.where(row < hi, x_ref[...], 0.0).astype(BF16)

    @pl.when(hi == tm)
    def _():
        part = swiglu_part(xb_ref[...], wg_ref[0].astype(BF16), wu_ref[0].astype(BF16),
                           wd_ref[0].astype(BF16))

        @pl.when(f == 0)
        def _():
            o_ref[...] = part

        @pl.when(f != 0)
        def _():
            o_ref[...] += part

    @pl.when(hi < tm)
    def _():
        @pl.when(f == 0)
        def _():
            o_ref[...] = jnp.zeros_like(o_ref)

        def sub_block(s, carry):
            rows = pl.ds(pl.multiple_of(s * sub, sub), sub)
            o_ref[rows, :] += swiglu_part(xb_ref[rows, :], wg_ref[0].astype(BF16),
                                          wu_ref[0].astype(BF16), wd_ref[0].astype(BF16))
            return carry

        lax.fori_loop(0, (hi + sub - 1) >> sub_shift, sub_block, 0)


def _moe_ffn(xs, tile_expert, tile_rows, wg, wu, wd):
    rows, d = xs.shape
    tm = MOE_TILE
    tf = MOE_FF_TILE
    nf = wg.shape[2] // tf

    def f_idx(f, hi):
        v = (hi > 0).astype(jnp.int32)
        return f * v + (nf - 1) * (1 - v)

    return pl.pallas_call(
        _moe_kernel,
        grid_spec=pltpu.PrefetchScalarGridSpec(
            num_scalar_prefetch=2,
            grid=(rows // tm, nf),
            in_specs=[pl.BlockSpec((tm, d), lambda w, f, te, hi: (w, 0)),
                      pl.BlockSpec((1, d, tf), lambda w, f, te, hi: (te[w], 0, f_idx(f, hi[w]))),
                      pl.BlockSpec((1, d, tf), lambda w, f, te, hi: (te[w], 0, f_idx(f, hi[w]))),
                      pl.BlockSpec((1, tf, d), lambda w, f, te, hi: (te[w], f_idx(f, hi[w]), 0))],
            out_specs=pl.BlockSpec((tm, d), lambda w, f, te, hi: (w, 0)),
            scratch_shapes=[pltpu.VMEM((tm, d), BF16)]),
        out_shape=jax.ShapeDtypeStruct((rows, d), F32),
        compiler_params=_cparams("arbitrary", "arbitrary"),
        name="moe_expert_swiglu",
    )(tile_expert, tile_rows, xs, wg, wu, wd)


def _combine_kernel(delta_ref, lstart_ref, run_ref, gstart_ref, ys_ref, x_ref, meta_ref, wt_ref, g2_ref, fw_ref,
                    o_ref, ybuf_ref, sem):
    j = pl.program_id(0)
    n_tiles = pl.num_programs(0)
    tr = x_ref.shape[0]
    lrows = ybuf_ref.shape[1]
    slot = lax.rem(j, 2)

    def group_copy(buf, local_row, xs_row):
        return pltpu.make_async_copy(ys_ref.at[pl.ds(xs_row, SUBLANES)],
                                     ybuf_ref.at[buf, pl.ds(local_row, SUBLANES)], sem.at[buf])

    def fetch(tile, buf):
        ybuf_ref[buf, 2 * tr:, :] = jnp.zeros((lrows - 2 * tr, ybuf_ref.shape[2]), F32)
        _for_each_group(tile, lstart_ref, run_ref, gstart_ref,
                        lambda lr, xr: group_copy(buf, lr, xr).start())

    @pl.when(j == 0)
    def _():
        fetch(0, 0)

    _for_each_group(j, lstart_ref, run_ref, gstart_ref, lambda lr, xr: group_copy(slot, lr, xr).wait())

    @pl.when(j + 1 < n_tiles)
    def _():
        fetch(j + 1, 1 - slot)

    meta = meta_ref[...]
    wt = wt_ref[...]
    lp1 = _local_pos(meta[:, 0:1], meta[:, 2:3], delta_ref, j)
    lp2 = _local_pos(meta[:, 1:2], meta[:, 3:4], delta_ref, j)
    l_idx = lax.broadcasted_iota(jnp.int32, (tr, lrows), 1)
    pick = jnp.where(l_idx == lp1, wt[:, 0:1], 0.0) + jnp.where(l_idx == lp2, wt[:, 1:2], 0.0)
    ffn = _bdot(pick, ybuf_ref[slot])
    x4 = x_ref[...] + g2_ref[0] * ffn
    o_ref[...] = (x4 * _rms_scale(x4)) * fw_ref[...]


def _combine(ys, tables, x3, meta, wt, g2, final_w, seq_len):
    n, d = x3.shape
    tr = TOKEN_TILE
    tps = seq_len // tr
    lrows = -(-_local_rows(tr) // LANES) * LANES
    return pl.pallas_call(
        _combine_kernel,
        grid_spec=pltpu.PrefetchScalarGridSpec(
            num_scalar_prefetch=4,
            grid=(n // tr,),
            in_specs=[pl.BlockSpec(memory_space=pl.ANY),
                      pl.BlockSpec((tr, d), lambda j, *_: (j, 0)),
                      pl.BlockSpec((tr, LANES), lambda j, *_: (j, 0)),
                      pl.BlockSpec((tr, LANES), lambda j, *_: (j, 0)),
                      pl.BlockSpec((1, 1, d), lambda j, *_: (j // tps, 0, 0)),
                      pl.BlockSpec((1, d), lambda j, *_: (0, 0))],
            out_specs=pl.BlockSpec((tr, d), lambda j, *_: (j, 0)),
            scratch_shapes=[pltpu.VMEM((2, lrows, d), F32), pltpu.SemaphoreType.DMA((2,))]),
        out_shape=jax.ShapeDtypeStruct((n, d), F32),
        compiler_params=_cparams("arbitrary"),
        name="moe_combine_final_norm",
    )(*tables, ys, x3, meta, wt, g2, final_w.reshape(1, d))


def _moe_tables(tile_base, counts, n_tokens):
    i32 = lambda t: t.astype(jnp.int32)
    tm = MOE_TILE
    before = tile_base[:, 0, :N_EXPERTS]
    total = counts[0, :N_EXPERTS]
    run = jnp.concatenate([before[1:], total[None]], axis=0) - before
    run = (run + SUBLANES - 1) // SUBLANES * SUBLANES
    l_end = jnp.cumsum(run, axis=1)
    l_start = l_end - run
    g_size = jnp.sum(run, axis=0)
    g_tiles = (g_size + tm - 1) // tm
    tile_end = jnp.cumsum(g_tiles)
    g_off = (tile_end - g_tiles) * tm
    g_end = g_off + g_size
    g_start = g_off[None, :] + jnp.cumsum(run, axis=0) - run
    delta = l_start - before
    gaps = jnp.concatenate([g_end, tile_end * tm - g_end])
    used_tiles = tile_end[-1:]

    w = jnp.arange(_xs_rows(n_tokens) // tm, dtype=jnp.int32)
    te = jnp.minimum(jnp.sum((tile_end[None, :] <= w[:, None]).astype(jnp.int32), axis=1), N_EXPERTS - 1)
    rows = jnp.where(w < tile_end[-1], jnp.clip(jnp.take(g_end, te) - w * tm, 0, tm), 0)
    flat = lambda t: i32(t).reshape(-1)
    return (flat(delta), flat(l_start), flat(run), flat(g_start)), i32(gaps), i32(used_tiles), i32(te), i32(rows)


def kernel(x, c, rel_bias, ada_w, ada_b, norm_mix_w, norm_ffn_w, final_norm_w, ab_w_in, attn_sinks,
           dn_conv_w, dn_a_log, dn_dt_bias, dn_norm_w, ab_w_out, ffn_w_gate, ffn_w_up, ffn_w_down,
           cd_w_in, lru_conv_w, lru_conv_b, lru_gate_a_w, lru_gate_a_b, lru_gate_x_w, lru_gate_x_b,
           lru_lambda, sconv_w, cd_w_out, moe_router_w, moe_router_b, moe_w_gate, moe_w_up, moe_w_down):
    bsz, seq_len, d = x.shape
    n = bsz * seq_len
    x2d = x.reshape(n, d)
    mods = _ada_mods(c, ada_w, ada_b)

    sh1, sc1, g1, sh2, sc2, g2 = (mods[0, k] for k in range(6))
    qa, kd, vd, qn, kn, vb, gs, bexp, gcexp = _in_proj0(
        x2d, norm_mix_w[0], sc1, sh1, ab_w_in[0], dn_conv_w[0], dn_a_log[0], dn_dt_bias[0], seq_len)
    attn = _attention(qa, kd, vd, _bias_table(rel_bias), attn_sinks[0], seq_len)
    dn = _deltanet(qn, kn, vb, gs, bexp, gcexp, dn_norm_w[0], seq_len)
    x2 = _mid0(attn, dn, x2d, ab_w_out[0], g1, norm_ffn_w[0], sc2, sh2, g2,
               ffn_w_gate[0], ffn_w_up[0], ffn_w_down[0], seq_len)

    sh1, sc1, g1, sh2, sc2, g2 = (mods[1, k] for k in range(6))
    cat = _mix1(x2, norm_mix_w[1], sc1, sh1, cd_w_in[0], lru_conv_w[0], lru_conv_b[0],
                lru_gate_a_w[0], lru_gate_a_b[0], lru_gate_x_w[0], lru_gate_x_b[0],
                lru_lambda[0], sconv_w[0], seq_len)
    x3, hn4, meta_t, meta, wt, tile_base, counts = _route(
        cat, x2, cd_w_out[0], g1, norm_ffn_w[1], sc2, sh2, moe_router_w[0], moe_router_b[0], seq_len)
    tables, gaps, used_tiles, tile_expert, tile_rows = _moe_tables(tile_base, counts, n)
    xs = _dispatch(hn4, meta_t, tables, gaps, used_tiles)
    ys = _moe_ffn(xs, tile_expert, tile_rows, moe_w_gate[0], moe_w_up[0], moe_w_down[0])
    out = _combine(ys, tables, x3, meta, wt, g2, final_norm_w, seq_len)
    return out.reshape(bsz, seq_len, d)
```

```python
import functools
import math

import numpy as np
import jax
import jax.numpy as jnp
from jax import lax
from jax.experimental import pallas as pl
from jax.experimental.pallas import tpu as pltpu

D_MODEL = 1024
EPS = 1e-6
HEAD_DIM = 64
A_Q_HEADS = 8
A_KV_HEADS = 2
WINDOW = 128
N_BUCKETS = 32
MAX_DISTANCE = 128
B_HEADS = 8
B_CONV = 4
CHUNK = 64
A_Q_W = A_Q_HEADS * HEAD_DIM
A_KV_W = A_KV_HEADS * HEAD_DIM
B_W = B_HEADS * HEAD_DIM
B_QKV_W = 3 * B_W
LRU_WIDTH = D_MODEL
LRU_BLOCKS = 8
LRU_C = 8.0
SC_WIDTH = D_MODEL // 2
D_FF = 2816
N_EXPERTS = 8
D_FF_EXPERT = 3584

LANES = 128
SUBLANES = 8
VMEM_LIMIT_BYTES = 56 * 1024 * 1024
TOKEN_TILE = 512
MOE_TILE = 1024
MOE_SUB = 256
MOE_FF_TILE = 512
NEG_BIG = -1e30

F32 = jnp.float32
BF16 = jnp.bfloat16


def _cparams(*sem):
    return pltpu.CompilerParams(dimension_semantics=tuple(sem), vmem_limit_bytes=VMEM_LIMIT_BYTES)


def _const_spec(shape):
    nd = len(shape)
    return pl.BlockSpec(shape, lambda *_: (0,) * nd)


def _bdot(a, b):
    return jnp.dot(a.astype(BF16), b.astype(BF16), preferred_element_type=F32)


def _bdot_nt(a, b):
    return lax.dot_general(a.astype(BF16), b.astype(BF16), (((1,), (1,)), ((), ())),
                           preferred_element_type=F32)


def _bdot_tn(a, b):
    return lax.dot_general(a.astype(BF16), b.astype(BF16), (((0,), (0,)), ((), ())),
                           preferred_element_type=F32)


def _split(x, n):
    parts = []
    r = x
    for i in range(n):
        p = r.astype(BF16)
        parts.append(p)
        if i + 1 < n:
            r = r - p.astype(F32)
    return parts


def _dot_x(a, b, na=2, nb=2):
    asp = _split(a, na) if na > 1 else [a.astype(BF16)]
    bsp = _split(b, nb) if nb > 1 else [b.astype(BF16)]
    acc = None
    for i, ai in enumerate(asp):
        for j, bj in enumerate(bsp):
            if i + j >= max(na, nb):
                continue
            t = jnp.dot(ai, bj, preferred_element_type=F32)
            acc = t if acc is None else acc + t
    return acc


def _dot_terms(a, b_stacked, n):
    return jnp.dot(jnp.concatenate(_split(a, n), axis=1), b_stacked, preferred_element_type=F32)


def _silu(x):
    return x * (1.0 / (1.0 + jnp.exp(-x)))


def _sigmoid(x):
    return 1.0 / (1.0 + jnp.exp(-x))


def _log1p(z):
    u = 1.0 + z
    tiny = u == 1.0
    return jnp.where(tiny, z, jnp.log(u) * (z / jnp.where(tiny, 1.0, u - 1.0)))


def _softplus(x):
    return jnp.maximum(x, 0.0) + _log1p(jnp.exp(-jnp.abs(x)))


def _neg_expm1(y):
    return -jnp.tanh(0.5 * y) * (jnp.exp(y) + 1.0)


def _rms_scale(x):
    width = x.shape[1]
    mean_w = jnp.full((width, LANES), 1.0 / width, BF16)
    ms = _dot_x(x * x, mean_w, 2, 1)
    r = lax.rsqrt(ms + EPS)
    return jnp.concatenate([r] * (width // LANES), axis=1)


def _norm_mod(x, w, sc, sh, on_mxu=False):
    if on_mxu:
        scale = _rms_scale(x)
    else:
        scale = lax.rsqrt(jnp.mean(x * x, axis=-1, keepdims=True) + EPS)
    return (x * scale) * w * (1.0 + sc) + sh


def _shift_rows(x, k, prev_tail):
    n, width = x.shape
    x3 = x.reshape(n // SUBLANES, SUBLANES, width)
    rot = pltpu.roll(x3, k, 1)
    rot_prev = jnp.concatenate([pltpu.roll(prev_tail, k, 0)[None], rot[:-1]], axis=0)
    sub = lax.broadcasted_iota(jnp.int32, x3.shape, 1)
    return jnp.where(sub >= k, rot, rot_prev).reshape(n, width)


def _ada_kernel(c_ref, w_ref, b_ref, o_ref):
    c = c_ref[...]
    cond = _silu(c)
    o_ref[0] = _dot_x(cond, w_ref[0], 3, 2) + b_ref[0]


def _ada_mods(c, ada_w, ada_b):
    depth, d, six_d = ada_w.shape
    bsz = c.shape[0]
    rows = max(SUBLANES, bsz)
    c_pad = jnp.zeros((rows, d), F32).at[:bsz].set(c)
    tn = 1536
    out = pl.pallas_call(
        _ada_kernel,
        grid=(depth, six_d // tn),
        in_specs=[pl.BlockSpec((rows, d), lambda l, j: (0, 0)),
                  pl.BlockSpec((1, d, tn), lambda l, j: (l, 0, j)),
                  pl.BlockSpec((1, 1, tn), lambda l, j: (l, 0, j))],
        out_specs=pl.BlockSpec((1, rows, tn), lambda l, j: (l, 0, j)),
        out_shape=jax.ShapeDtypeStruct((depth, rows, six_d), F32),
        compiler_params=_cparams("parallel", "parallel"),
        name="ada_mods",
    )(c_pad, ada_w, ada_b.reshape(depth, 1, six_d))
    return out[:, :bsz].reshape(depth, bsz, 6, 1, d).transpose(0, 2, 1, 3, 4)


def _t5_bucket(dist):
    max_exact = N_BUCKETS // 2
    d = np.maximum(dist, 0)
    large = max_exact + (np.log(np.maximum(d, 1) / max_exact) / math.log(MAX_DISTANCE / max_exact)
                         * (N_BUCKETS - max_exact)).astype(np.int32)
    large = np.minimum(large, N_BUCKETS - 1)
    return np.where(d < max_exact, d, large).astype(np.int32)


def _band_buckets():
    qi = np.arange(WINDOW)[:, None]
    s = np.arange(2 * WINDOW)[None, :]
    dist = qi + WINDOW - s
    in_window = (dist >= 0) & (dist < WINDOW)
    return np.where(in_window, _t5_bucket(dist), -1).astype(np.int32)


def _bias_kernel(rb_ref, bucket_ref, o_ref):
    h = pl.program_id(0)
    bucket = bucket_ref[...]
    acc = jnp.zeros(bucket.shape, F32)
    for b in range(N_BUCKETS):
        acc = jnp.where(bucket == b, rb_ref[b, h], acc)
    o_ref[0] = jnp.where(bucket < 0, NEG_BIG, acc)


def _bias_table(rel_bias):
    bucket = jnp.asarray(_band_buckets())
    out = pl.pallas_call(
        _bias_kernel,
        grid=(A_Q_HEADS,),
        in_specs=[pl.BlockSpec(memory_space=pltpu.SMEM),
                  _const_spec((WINDOW, 2 * WINDOW))],
        out_specs=pl.BlockSpec((1, WINDOW, 2 * WINDOW), lambda h: (h, 0, 0)),
        out_shape=jax.ShapeDtypeStruct((A_Q_HEADS, WINDOW, 2 * WINDOW), F32),
        compiler_params=_cparams("parallel"),
        name="attn_bias_table",
    )(rel_bias, bucket)
    return out.reshape(A_Q_HEADS // 2, 2 * WINDOW, 2 * WINDOW)


_C_QA = 0
_C_KA = _C_QA + A_Q_W
_C_VA = _C_KA + A_KV_W
_C_QKV = _C_VA + A_KV_W
_C_GATE = _C_QKV + B_QKV_W
_C_SMALL = _C_GATE + B_W
_AB_COLS = _C_SMALL + LANES


def _ab_in_weight(w_in):
    return jnp.pad(w_in, ((0, 0), (0, _AB_COLS - w_in.shape[1]))).astype(BF16)


def _dup_heads(t, low):
    swapped = pltpu.roll(t, HEAD_DIM, 1)
    return jnp.concatenate([jnp.where(low, t, swapped), jnp.where(low, swapped, t)], axis=1)


def _chunk_tril(tm):
    r = np.arange(tm)
    return ((r[:, None] >= r[None, :]) & (r[:, None] // CHUNK == r[None, :] // CHUNK)).astype(np.float32)


def _head_selector():
    e = np.zeros((B_W, LANES), np.float32)
    for h in range(B_HEADS):
        e[h * HEAD_DIM:(h + 1) * HEAD_DIM, h] = 1.0
    return e


def _in0_kernel(x_ref, nw_ref, sc_ref, sh_ref, w_ref, cw_ref, sel_ref, selt2_ref, selt3_ref, tril_ref,
                alog_ref, dtb_ref,
                qa_ref, kd_ref, vd_ref, qn_ref, kn_ref, vb_ref, gs_ref, bexp_ref, gcexp_ref,
                tail_ref, *, tiles_per_seq):
    i = pl.program_id(0)

    @pl.when(i % tiles_per_seq == 0)
    def _():
        tail_ref[...] = jnp.zeros_like(tail_ref)

    hn = _norm_mod(x_ref[...], nw_ref[...], sc_ref[0], sh_ref[0])
    proj = jnp.dot(hn.astype(BF16), w_ref[...], preferred_element_type=F32)
    tm = proj.shape[0]
    low = lax.broadcasted_iota(jnp.int32, (tm, LANES), 1) < HEAD_DIM

    small = proj[:, _C_SMALL:]
    lane = lax.broadcasted_iota(jnp.int32, small.shape, 1)
    beta = jnp.where(lane < B_HEADS, _sigmoid(small), 0.0)
    dec = pltpu.roll(small, LANES - B_HEADS, 1)
    g = jnp.where(lane < B_HEADS, -jnp.exp(alog_ref[...]) * _softplus(dec + dtb_ref[...]), 0.0)
    bexp_ref[...] = _dot_terms(beta, selt2_ref[...], 2)
    gc = _dot_x(tril_ref[...], g, 1, 3)
    gcexp_ref[...] = _dot_terms(gc, selt3_ref[...], 3)

    qa_ref[...] = proj[:, _C_QA:_C_KA].astype(BF16)
    kd_ref[...] = _dup_heads(proj[:, _C_KA:_C_VA], low).astype(BF16)
    vd_ref[...] = _dup_heads(proj[:, _C_VA:_C_QKV], low).astype(BF16)

    def conv_silu(block):
        cols = slice(block * B_W, (block + 1) * B_W)
        xq = proj[:, _C_QKV + block * B_W:_C_QKV + (block + 1) * B_W]
        tail = tail_ref[:, cols]
        cw = cw_ref[:, cols]
        y = xq * cw[B_CONV - 1:B_CONV]
        for k in range(1, B_CONV):
            y = y + _shift_rows(xq, k, tail) * cw[B_CONV - 1 - k:B_CONV - k]
        tail_ref[:, cols] = xq[tm - SUBLANES:]
        return _silu(y)

    q = conv_silu(0)
    k_ = conv_silu(1)
    ssq = _dot_x(jnp.concatenate([q * q, k_ * k_], axis=1), sel_ref[...], 2, 1)
    r = lax.rsqrt(ssq + EPS)
    q_scale = _dot_terms(r[:, :LANES], selt2_ref[...], 2)
    k_scale = _dot_terms(r[:, LANES:], selt2_ref[...], 2)
    vb_ref[...] = conv_silu(2)
    gs_ref[...] = _silu(proj[:, _C_GATE:_C_SMALL])
    qn_ref[...] = q * q_scale * (HEAD_DIM ** -0.5)
    kn_ref[...] = k_ * k_scale


def _in_proj0(x2d, nw, sc, sh, w_in, conv_w, a_log, dt_bias, seq_len):
    n, d = x2d.shape
    tm = TOKEN_TILE
    tiles_per_seq = seq_len // tm
    w = _ab_in_weight(w_in)
    hs = _head_selector()
    zeros = np.zeros_like(hs)
    sel = jnp.asarray(np.block([[hs, zeros], [zeros, hs]]), BF16)
    selt2 = jnp.asarray(np.tile(hs.T, (2, 1)), BF16)
    selt3 = jnp.asarray(np.tile(hs.T, (3, 1)), BF16)
    tril = jnp.asarray(_chunk_tril(tm), BF16)
    pad8 = lambda v: jnp.zeros((1, LANES), F32).at[0, :B_HEADS].set(v)
    row = lambda width: pl.BlockSpec((tm, width), lambda i: (i, 0))
    per_b = pl.BlockSpec((1, 1, d), lambda i: (i // tiles_per_seq, 0, 0))
    outs = pl.pallas_call(
        functools.partial(_in0_kernel, tiles_per_seq=tiles_per_seq),
        grid=(n // tm,),
        in_specs=[row(d), _const_spec((1, d)), per_b, per_b,
                  _resident_spec((d, _AB_COLS)), _const_spec((B_CONV, B_QKV_W)),
                  _const_spec(sel.shape), _const_spec(selt2.shape), _const_spec(selt3.shape),
                  _const_spec((tm, tm)), _const_spec((1, LANES)), _const_spec((1, LANES))],
        out_specs=[row(A_Q_W), row(2 * A_KV_W), row(2 * A_KV_W)] + [row(B_W)] * 6,
        out_shape=[jax.ShapeDtypeStruct((n, A_Q_W), BF16),
                   jax.ShapeDtypeStruct((n, 2 * A_KV_W), BF16),
                   jax.ShapeDtypeStruct((n, 2 * A_KV_W), BF16)]
        + [jax.ShapeDtypeStruct((n, B_W), F32)] * 6,
        scratch_shapes=[pltpu.VMEM((SUBLANES, B_QKV_W), F32)],
        compiler_params=_cparams("arbitrary"),
        name="in_proj0",
    )(x2d, nw.reshape(1, d), sc, sh, w, conv_w, sel, selt2, selt3, tril, pad8(a_log), pad8(dt_bias))
    return outs


_ATTN_BLOCKS = 2


def _attn_kernel(sink_ref, q_ref, kp_ref, kc_ref, vp_ref, vc_ref, bm_ref, o_ref, *, steps_per_seq):
    i = pl.program_id(0)
    first = (i % steps_per_seq) == 0
    w = WINDOW
    lane = lax.broadcasted_iota(jnp.int32, (w, LANES), 1)
    low = lane < HEAD_DIM
    col = lax.broadcasted_iota(jnp.int32, (2 * w, 2 * w), 1)
    row = lax.broadcasted_iota(jnp.int32, (2 * w, 1), 0)
    prev_dead = jnp.logical_and(first, col < w)
    zero = jnp.zeros((), q_ref.dtype)
    pairs = A_Q_HEADS // 2
    units = [(b, j) for b in range(_ATTN_BLOCKS) for j in range(pairs)]

    def keys(p_ref, c_ref, b, kh):
        ls = slice(kh * LANES, (kh + 1) * LANES)
        before = p_ref[:, ls] if b == 0 else c_ref[(b - 1) * w:b * w, ls]
        return jnp.concatenate([before, c_ref[b * w:(b + 1) * w, ls]], axis=0)

    kv_of = lambda j: (2 * j) // (A_Q_HEADS // A_KV_HEADS)
    qp = [q_ref[b * w:(b + 1) * w, j * LANES:(j + 1) * LANES] for b, j in units]
    qs = [jnp.concatenate([jnp.where(low, t, zero), jnp.where(low, zero, t)], axis=0) for t in qp]
    kd = [keys(kp_ref, kc_ref, b, kv_of(j)) for b, j in units]
    vd = [keys(vp_ref, vc_ref, b, kv_of(j)) for b, j in units]
    s = [lax.dot_general(a, k, (((1,), (1,)), ((), ())), preferred_element_type=F32) for a, k in zip(qs, kd)]
    s = [t * (HEAD_DIM ** -0.5) + bm_ref[j] for t, (b, j) in zip(s, units)]
    s = [jnp.where(prev_dead, NEG_BIG, t) if b == 0 else t for t, (b, j) in zip(s, units)]
    sink = [jnp.where(row < w, sink_ref[2 * j], sink_ref[2 * j + 1]) for b, j in units]
    m = [jnp.maximum(jnp.max(t, axis=-1, keepdims=True), sk) for t, sk in zip(s, sink)]
    p = [jnp.exp(t - mt) for t, mt in zip(s, m)]
    denom = [jnp.sum(t, axis=-1, keepdims=True) + jnp.exp(sk - mt) for t, sk, mt in zip(p, sink, m)]
    pv = [jnp.dot(t.astype(BF16), v, preferred_element_type=F32) / dn for t, v, dn in zip(p, vd, denom)]
    outs = [jnp.where(low, t[:w], t[w:]) for t in pv]
    for b in range(_ATTN_BLOCKS):
        o_ref[b * w:(b + 1) * w, :] = jnp.concatenate(outs[b * pairs:(b + 1) * pairs], axis=1).astype(o_ref.dtype)


def _attention(qa, kd, vd, bias_tbl, sinks, seq_len):
    n = qa.shape[0]
    w = WINDOW
    rows = _ATTN_BLOCKS * w
    steps = seq_len // rows
    cur = lambda i: (i, 0)
    prev = lambda i: (jnp.where(i % steps == 0, i * _ATTN_BLOCKS, i * _ATTN_BLOCKS - 1), 0)
    return pl.pallas_call(
        functools.partial(_attn_kernel, steps_per_seq=steps),
        grid=(n // rows,),
        in_specs=[pl.BlockSpec(memory_space=pltpu.SMEM),
                  pl.BlockSpec((rows, A_Q_W), cur),
                  pl.BlockSpec((w, 2 * A_KV_W), prev), pl.BlockSpec((rows, 2 * A_KV_W), cur),
                  pl.BlockSpec((w, 2 * A_KV_W), prev), pl.BlockSpec((rows, 2 * A_KV_W), cur),
                  _const_spec((A_Q_HEADS // 2, 2 * w, 2 * w))],
        out_specs=pl.BlockSpec((rows, A_Q_W), cur),
        out_shape=jax.ShapeDtypeStruct((n, A_Q_W), BF16),
        compiler_params=_cparams("parallel"),
        name="swa_attention",
    )(sinks, qa, kd, kd, vd, vd, bias_tbl)


_DN_PAIRS = B_HEADS // 2
_DN_INV_BLOCK = 16
_DN_GROUP = 4


def _block_diag(x, low):
    zero = jnp.zeros((), x.dtype)
    return jnp.concatenate([jnp.where(low, x, zero), jnp.where(low, zero, x)], axis=0)


def _dn_intra(chunks, data_refs, work_refs, consts):
    qn_ref, kn_ref, vb_ref, bexp_ref, gcexp_ref = data_refs
    u_ref, w_ref, qk_ref, qd_ref, kd_ref, egl_ref = work_refs
    low, i_idx, j_idx, ones3 = consts
    c = CHUNK
    units = [(ci, p) for ci in chunks for p in range(_DN_PAIRS)]
    where = [(slice(ci * c, (ci + 1) * c), slice(p * LANES, (p + 1) * LANES)) for ci, p in units]
    causal = i_idx >= j_idx
    strict = i_idx > j_idx
    on_diag = i_idx == j_idx
    eye = on_diag.astype(F32)
    blk_shift = int(math.log2(_DN_INV_BLOCK))
    same_blk = (i_idx >> blk_shift) == (j_idx >> blk_shift)

    q = [qn_ref[rs, ls] for rs, ls in where]
    k = [kn_ref[rs, ls] for rs, ls in where]
    v = [vb_ref[rs, ls] for rs, ls in where]
    b = [bexp_ref[rs, ls] for rs, ls in where]
    gc = [gcexp_ref[rs, ls] for rs, ls in where]

    gr = [jnp.dot(ones3, jnp.concatenate(_split(jnp.where(on_diag, t, 0.0), 3), axis=0),
                  preferred_element_type=F32) for t in gc]
    ks = [_block_diag(t.astype(BF16), low) for t in k]
    qkk = [lax.dot_general(jnp.concatenate([qt, kt], axis=0).astype(BF16), kst,
                           (((1,), (1,)), ((), ())), preferred_element_type=F32)
           for qt, kt, kst in zip(q, k, ks)]
    decay = [jnp.exp(jnp.where(causal, gct - grt, NEG_BIG)) for gct, grt in zip(gc, gr)]
    lmat = [jnp.where(strict, bt * t[c:] * dt, 0.0) for bt, t, dt in zip(b, qkk, decay)]
    qk = [jnp.where(causal, t[:c] * dt, 0.0) for t, dt in zip(qkk, decay)]

    def mm(xs, ys):
        return [_bdot(x, _block_diag(y.astype(BF16), low)) for x, y in zip(xs, ys)]

    l_diag = [jnp.where(same_blk, t, 0.0) for t in lmat]
    l_off = [t - d for t, d in zip(lmat, l_diag)]
    pw = [-t for t in l_diag]
    d_inv = [eye + t for t in pw]
    for _ in range(blk_shift - 1):
        pw = mm(pw, pw)
        d_inv = mm(d_inv, [eye + t for t in pw])
    pw = [-t for t in mm(d_inv, l_off)]
    acc = [eye + t for t in pw]
    for _ in range(int(math.log2(c // _DN_INV_BLOCK)) - 1):
        pw = mm(pw, pw)
        acc = mm(acc, [eye + t for t in pw])
    tmat = mm(acc, d_inv)

    egc = [jnp.exp(t) for t in gc]
    rhs = [jnp.concatenate([_block_diag((vt * bt).astype(BF16), low),
                            _block_diag((kt * (bt * et)).astype(BF16), low)], axis=1)
           for vt, kt, bt, et in zip(v, k, b, egc)]
    uw = [_bdot(t, r) for t, r in zip(tmat, rhs)]
    for n, (ci, p) in enumerate(units):
        g_last = gc[n][c - 1:c, :]
        u_ref[ci, p] = uw[n][:, :LANES]
        w_ref[ci, p] = uw[n][:, LANES:]
        qk_ref[ci, p] = qk[n]
        qd_ref[ci, p] = q[n] * egc[n]
        kd_ref[ci, p] = k[n] * jnp.exp(g_last - gc[n])
        egl_ref[ci, p] = jnp.broadcast_to(jnp.exp(g_last), (SUBLANES, LANES))


def _dn_scan(ci, work_refs, s_ref, gs_ref, nw, o_ref, consts):
    u_ref, w_ref, qk_ref, qd_ref, kd_ref, egl_ref = work_refs
    low, mask_bd, head_mean2 = consts
    c = CHUNK
    rows = slice(ci * c, (ci + 1) * c)
    pairs = range(_DN_PAIRS)
    s_old = [s_ref[p] for p in pairs]
    wq = [_bdot(jnp.concatenate([w_ref[ci, p], qd_ref[ci, p]], axis=0), s_old[p]) for p in pairs]
    v_new = [u_ref[ci, p] - wq[p][:c] for p in pairs]
    o = [wq[p][c:] + _bdot(qk_ref[ci, p], _block_diag(v_new[p].astype(BF16), low)) for p in pairs]
    kv = [_bdot_tn(kd_ref[ci, p], v_new[p]) for p in pairs]
    for p in pairs:
        s_ref[p] = s_old[p] * egl_ref[ci, p][0:1, :] + jnp.where(mask_bd, kv[p], 0.0)
    ms = [jnp.dot(jnp.concatenate(_split(t * t, 2), axis=1), head_mean2, preferred_element_type=F32)
          for t in o]
    for p in pairs:
        ls = slice(p * LANES, (p + 1) * LANES)
        y = (o[p] * lax.rsqrt(ms[p] + EPS)) * nw * gs_ref[rows, ls]
        o_ref[rows, ls] = y.astype(o_ref.dtype)


def _dn_kernel(qn_ref, kn_ref, vb_ref, gs_ref, bexp_ref, gcexp_ref, nw_ref, o_ref,
               s_ref, u_ref, w_ref, qk_ref, qd_ref, kd_ref, egl_ref, *, groups_per_seq):
    i = pl.program_id(0)

    @pl.when(i % groups_per_seq == 0)
    def _():
        s_ref[...] = jnp.zeros_like(s_ref)

    c = CHUNK
    tm = o_ref.shape[0]
    n_chunks = tm // c
    lane = lax.broadcasted_iota(jnp.int32, (c, LANES), 1)
    low = lane < HEAD_DIM
    i_idx = lax.broadcasted_iota(jnp.int32, (c, LANES), 0)
    j_idx = lane & (c - 1)
    ones3 = jnp.ones((c, 3 * c), BF16)
    rb = lax.broadcasted_iota(jnp.int32, (LANES, LANES), 0)
    cb = lax.broadcasted_iota(jnp.int32, (LANES, LANES), 1)
    mask_bd = (rb < HEAD_DIM) == (cb < HEAD_DIM)
    head_mean = jnp.where(mask_bd, 1.0 / HEAD_DIM, 0.0).astype(BF16)
    head_mean2 = jnp.concatenate([head_mean, head_mean], axis=0)
    data_refs = (qn_ref, kn_ref, vb_ref, bexp_ref, gcexp_ref)
    work_refs = (u_ref, w_ref, qk_ref, qd_ref, kd_ref, egl_ref)
    intra_consts = (low, i_idx, j_idx, ones3)
    scan_consts = (low, mask_bd, head_mean2)
    nw = nw_ref[...]

    groups = [list(range(s, s + _DN_GROUP)) for s in range(0, n_chunks, _DN_GROUP)]
    _dn_intra(groups[0], data_refs, work_refs, intra_consts)
    for j, grp in enumerate(groups):
        if j + 1 < len(groups):
            _dn_intra(groups[j + 1], data_refs, work_refs, intra_consts)
        for ci in grp:
            _dn_scan(ci, work_refs, s_ref, gs_ref, nw, o_ref, scan_consts)


def _deltanet(qn, kn, vb, gs, bexp, gcexp, norm_w, seq_len):
    n = qn.shape[0]
    tm = TOKEN_TILE
    nw2 = jnp.concatenate([norm_w, norm_w]).reshape(1, LANES)
    row = lambda width: pl.BlockSpec((tm, width), lambda i: (i, 0))
    return pl.pallas_call(
        functools.partial(_dn_kernel, groups_per_seq=seq_len // tm),
        grid=(n // tm,),
        in_specs=[row(B_W)] * 6 + [_const_spec((1, LANES))],
        out_specs=row(B_W),
        out_shape=jax.ShapeDtypeStruct((n, B_W), BF16),
        scratch_shapes=[pltpu.VMEM((_DN_PAIRS, LANES, LANES), F32)]
        + [pltpu.VMEM((tm // CHUNK, _DN_PAIRS, CHUNK, LANES), F32)] * 5
        + [pltpu.VMEM((tm // CHUNK, _DN_PAIRS, SUBLANES, LANES), F32)],
        compiler_params=_cparams("arbitrary"),
        name="gated_deltanet",
    )(qn, kn, vb, gs, bexp, gcexp, nw2)


def _resident_spec(shape):
    nd = len(shape)
    return pl.BlockSpec(shape, lambda *_: (0,) * nd, pipeline_mode=pl.Buffered(1))


def _mid0_kernel(attn_ref, dn_ref, x_ref, wo_ref, g1_ref, nw_ref, sc_ref, sh_ref, g2_ref,
                 wg_ref, wu_ref, wd_ref, o_ref):
    mix = (jnp.dot(attn_ref[...], wo_ref[:A_Q_W], preferred_element_type=F32)
           + jnp.dot(dn_ref[...], wo_ref[A_Q_W:], preferred_element_type=F32))
    x1 = x_ref[...] + g1_ref[0] * mix
    hn = _norm_mod(x1, nw_ref[...], sc_ref[0], sh_ref[0]).astype(BF16)
    hg = jnp.dot(hn, wg_ref[...], preferred_element_type=F32)
    hu = jnp.dot(hn, wu_ref[...], preferred_element_type=F32)
    act = (_silu(hg) * hu).astype(BF16)
    o_ref[...] = x1 + g2_ref[0] * jnp.dot(act, wd_ref[...], preferred_element_type=F32)


def _mid0(attn, dn, x2d, w_out, g1, nw, sc, sh, g2, wg, wu, wd, seq_len):
    n, d = x2d.shape
    tm = TOKEN_TILE
    tps = seq_len // tm
    row = lambda width: pl.BlockSpec((tm, width), lambda i: (i, 0))
    per_b = pl.BlockSpec((1, 1, d), lambda i: (i // tps, 0, 0))
    return pl.pallas_call(
        _mid0_kernel,
        grid=(n // tm,),
        in_specs=[row(A_Q_W), row(B_W), row(d), _resident_spec(w_out.shape), per_b,
                  _const_spec((1, d)), per_b, per_b, per_b,
                  _resident_spec(wg.shape), _resident_spec(wu.shape), _resident_spec(wd.shape)],
        out_specs=row(d),
        out_shape=jax.ShapeDtypeStruct((n, d), F32),
        compiler_params=_cparams("parallel"),
        name="out_proj0_swiglu",
    )(attn, dn, x2d, w_out.astype(BF16), g1, nw.reshape(1, d), sc, sh, g2,
      wg.astype(BF16), wu.astype(BF16), wd.astype(BF16))


def _gelu_tanh(x):
    return 0.5 * x * (1.0 + jnp.tanh(math.sqrt(2.0 / math.pi) * (x + 0.044715 * (x * x * x))))


def _linear_scan(a, b, h0):
    n, width = a.shape
    groups = n // SUBLANES
    a = a.reshape(groups, SUBLANES, width)
    b = b.reshape(groups, SUBLANES, width)
    in_group = lax.broadcasted_iota(jnp.int32, a.shape, 1)
    s = 1
    while s < SUBLANES:
        a_sh = pltpu.roll(a, s, 1)
        b_sh = pltpu.roll(b, s, 1)
        valid = in_group >= s
        b = jnp.where(valid, a * b_sh + b, b)
        a = jnp.where(valid, a * a_sh, a)
        s *= 2
    carry = jnp.broadcast_to(h0, (SUBLANES, width))
    out = []
    for g in range(groups):
        hg = a[g] * carry + b[g]
        out.append(hg)
        carry = jnp.broadcast_to(hg[SUBLANES - 1:SUBLANES, :], hg.shape)
    return jnp.concatenate(out, axis=0)


def _mix1_tile(x, nw, sc, sh, w_ref, cw_ref, cb_ref, ga_ref, gab_ref, gx_ref, gxb_ref, lam_ref, sw_ref,
               tail_c_ref, tail_d_ref, h_ref):
    hn = _norm_mod(x, nw, sc, sh, on_mxu=True).astype(BF16)
    proj = jnp.dot(hn, w_ref[...], preferred_element_type=F32)
    w_l = LRU_WIDTH
    xc_in = proj[:, :w_l]
    yc = proj[:, w_l:2 * w_l]
    bd = proj[:, 2 * w_l:2 * w_l + SC_WIDTH]
    cd = proj[:, 2 * w_l + SC_WIDTH:2 * w_l + 2 * SC_WIDTH]
    hd = proj[:, 2 * w_l + 2 * SC_WIDTH:]
    tm = xc_in.shape[0]

    kc = cw_ref.shape[0]
    tail = tail_c_ref[...]
    cw = cw_ref[...]
    xc = xc_in * cw[kc - 1:kc] + cb_ref[...]
    for k in range(1, kc):
        xc = xc + _shift_rows(xc_in, k, tail) * cw[kc - 1 - k:kc - k]
    tail_c_ref[...] = xc_in[tm - SUBLANES:]

    xb = xc.astype(BF16)
    gw = ga_ref.shape[1]
    ra, ri = [], []
    for p in range(ga_ref.shape[0]):
        xin = xb[:, p * gw:(p + 1) * gw]
        ra.append(jnp.dot(xin, ga_ref[p], preferred_element_type=F32))
        ri.append(jnp.dot(xin, gx_ref[p], preferred_element_type=F32))
    r = _sigmoid(jnp.concatenate(ra, axis=1) + gab_ref[...])
    ig = _sigmoid(jnp.concatenate(ri, axis=1) + gxb_ref[...])
    log_a = (-LRU_C) * r * _softplus(-lam_ref[...])
    a = jnp.exp(log_a)
    b = jnp.sqrt(_neg_expm1(2.0 * log_a)) * (ig * xc)
    h = _linear_scan(a, b, h_ref[0:1, :])
    h_ref[...] = jnp.broadcast_to(h[tm - 1:tm, :], h_ref.shape)
    yc_out = h * _gelu_tanh(yc)

    ks = sw_ref.shape[0]
    ch = cd * hd
    tail_d = tail_d_ref[...]
    sw = sw_ref[...]
    conv = ch * sw[ks - 1:ks]
    for k in range(1, ks):
        conv = conv + _shift_rows(ch, k, tail_d) * sw[ks - 1 - k:ks - k]
    tail_d_ref[...] = ch[tm - SUBLANES:]
    return jnp.concatenate([yc_out, bd * conv], axis=1)


def _pair_block_diag(gw):
    nb, bw, _ = gw.shape
    g2 = gw.reshape(nb // 2, 2, bw, bw)
    z = jnp.zeros((nb // 2, bw, bw), gw.dtype)
    top = jnp.concatenate([g2[:, 0], z], axis=2)
    bot = jnp.concatenate([z, g2[:, 1]], axis=2)
    return jnp.concatenate([top, bot], axis=1).astype(BF16)


def _route_tile(cat, x, wo_ref, g1, nw, sc, sh, rw_ref, rb_ref, carry_ref):
    x3 = x + g1 * jnp.dot(cat, wo_ref[...], preferred_element_type=F32)
    hn = _norm_mod(x3, nw, sc, sh)
    tm = hn.shape[0]
    lane = lax.broadcasted_iota(jnp.int32, (tm, LANES), 1)
    h_hi, h_lo = _split(hn, 2)
    both = jnp.dot(h_hi, rw_ref[...], preferred_element_type=F32)
    logits = (both[:, :LANES] + both[:, LANES:]
              + jnp.dot(h_lo, rw_ref[:, :LANES], preferred_element_type=F32) + rb_ref[...])
    lg = jnp.where(lane < N_EXPERTS, logits, NEG_BIG)
    m1 = jnp.max(lg, axis=1, keepdims=True)
    i1 = jnp.min(jnp.where(lg == m1, lane, LANES), axis=1, keepdims=True)
    lg2 = jnp.where(lane == i1, NEG_BIG, lg)
    m2 = jnp.max(lg2, axis=1, keepdims=True)
    i2 = jnp.min(jnp.where(lg2 == m2, lane, LANES), axis=1, keepdims=True)
    e2 = jnp.exp(m2 - m1)
    w1 = 1.0 / (1.0 + e2)
    w2 = e2 / (1.0 + e2)

    hit1 = lane == i1
    hit2 = lane == i2
    sel = jnp.logical_or(hit1, hit2).astype(F32)
    r_i = lax.broadcasted_iota(jnp.int32, (tm, tm), 0)
    c_i = lax.broadcasted_iota(jnp.int32, (tm, tm), 1)
    tril = (r_i >= c_i).astype(BF16)
    incl = jnp.dot(tril, sel.astype(BF16), preferred_element_type=F32)
    carry = carry_ref[0:1, :]
    excl = incl - sel + carry
    r1 = jnp.sum(jnp.where(hit1, excl, 0.0), axis=1, keepdims=True)
    r2 = jnp.sum(jnp.where(hit2, excl, 0.0), axis=1, keepdims=True)
    total = carry + incl[tm - 1:tm, :]
    carry_ref[...] = jnp.broadcast_to(total, carry_ref.shape)

    meta = jnp.where(lane == 0, i1, 0)
    meta = jnp.where(lane == 1, i2, meta)
    meta = jnp.where(lane == 2, r1.astype(jnp.int32), meta)
    meta = jnp.where(lane == 3, r2.astype(jnp.int32), meta)
    wt = jnp.where(lane == 0, w1, jnp.where(lane == 1, w2, 0.0))
    return x3, hn, meta, wt, carry, total


def _mix1_kernel(x_ref, nw_ref, sc_ref, sh_ref, w_ref, cw_ref, cb_ref, ga_ref, gab_ref, gx_ref, gxb_ref,
                 lam_ref, sw_ref, o_ref, tail_c_ref, tail_d_ref, h_ref, *, tiles_per_seq):
    i = pl.program_id(0)

    @pl.when(i % tiles_per_seq == 0)
    def _():
        tail_c_ref[...] = jnp.zeros_like(tail_c_ref)
        tail_d_ref[...] = jnp.zeros_like(tail_d_ref)
        h_ref[...] = jnp.zeros_like(h_ref)

    cat = _mix1_tile(x_ref[...], nw_ref[...], sc_ref[0], sh_ref[0], w_ref, cw_ref, cb_ref, ga_ref,
                     gab_ref, gx_ref, gxb_ref, lam_ref, sw_ref, tail_c_ref, tail_d_ref, h_ref)
    o_ref[...] = cat.astype(o_ref.dtype)


def _mix1(x2d, nw, sc, sh, w_in, conv_w, conv_b, ga_w, ga_b, gx_w, gx_b, lam, sconv_w, seq_len):
    n, d = x2d.shape
    tm = TOKEN_TILE
    tps = seq_len // tm
    cd_in = w_in.shape[1]
    cd_out = LRU_WIDTH + SC_WIDTH
    row = lambda width: pl.BlockSpec((tm, width), lambda i: (i, 0))
    per_b = pl.BlockSpec((1, 1, d), lambda i: (i // tps, 0, 0))
    ga = _pair_block_diag(ga_w)
    gx = _pair_block_diag(gx_w)
    vec = lambda v: v.reshape(1, -1)
    return pl.pallas_call(
        functools.partial(_mix1_kernel, tiles_per_seq=tps),
        grid=(n // tm,),
        in_specs=[row(d), _const_spec((1, d)), per_b, per_b, _resident_spec((d, cd_in)),
                  _const_spec(conv_w.shape), _const_spec((1, LRU_WIDTH)),
                  _const_spec(ga.shape), _const_spec((1, LRU_WIDTH)),
                  _const_spec(gx.shape), _const_spec((1, LRU_WIDTH)),
                  _const_spec((1, LRU_WIDTH)), _const_spec(sconv_w.shape)],
        out_specs=row(cd_out),
        out_shape=jax.ShapeDtypeStruct((n, cd_out), BF16),
        scratch_shapes=[pltpu.VMEM((SUBLANES, LRU_WIDTH), F32), pltpu.VMEM((SUBLANES, SC_WIDTH), F32),
                        pltpu.VMEM((SUBLANES, LRU_WIDTH), F32)],
        compiler_params=_cparams("arbitrary"),
        name="rglru_shortconv_mixer",
    )(x2d, vec(nw), sc, sh, w_in.astype(BF16), conv_w, vec(conv_b), ga, vec(ga_b), gx, vec(gx_b),
      vec(lam), sconv_w)


def _route_kernel(cat_ref, x_ref, wo_ref, g1_ref, nw_ref, sc_ref, sh_ref, rw_ref, rb_ref,
                  x3_ref, hn_ref, metat_ref, meta_ref, wt_ref, base_ref, cnt_ref, carry_ref):
    @pl.when(pl.program_id(0) == 0)
    def _():
        carry_ref[...] = jnp.zeros_like(carry_ref)

    x3, hn, meta, wt, before, total = _route_tile(cat_ref[...], x_ref[...], wo_ref, g1_ref[0], nw_ref[...],
                                                  sc_ref[0], sh_ref[0], rw_ref, rb_ref, carry_ref)
    x3_ref[...] = x3
    hn_ref[...] = hn
    meta_ref[...] = meta
    metat_ref[...] = jnp.transpose(meta.astype(F32))[:SUBLANES].astype(jnp.int32)
    wt_ref[...] = wt
    base_ref[0] = jnp.broadcast_to(before, base_ref.shape[1:]).astype(jnp.int32)
    cnt_ref[...] = jnp.broadcast_to(total, cnt_ref.shape).astype(jnp.int32)


def _route(cat, x2d, w_out, g1, nw, sc, sh, router_w, router_b, seq_len):
    n, d = x2d.shape
    tm = TOKEN_TILE
    tps = seq_len // tm
    row = lambda width: pl.BlockSpec((tm, width), lambda i: (i, 0))
    per_b = pl.BlockSpec((1, 1, d), lambda i: (i // tps, 0, 0))
    rw = jnp.zeros((d, LANES), F32).at[:, :N_EXPERTS].set(router_w)
    rw_hi = rw.astype(BF16)
    rw = jnp.concatenate([rw_hi, (rw - rw_hi.astype(F32)).astype(BF16)], axis=1)
    rb = jnp.zeros((1, LANES), F32).at[0, :N_EXPERTS].set(router_b)
    return pl.pallas_call(
        _route_kernel,
        grid=(n // tm,),
        in_specs=[row(cat.shape[1]), row(d), _resident_spec(w_out.shape), per_b, _const_spec((1, d)),
                  per_b, per_b, _const_spec((d, 2 * LANES)), _const_spec((1, LANES))],
        out_specs=[row(d), row(d), pl.BlockSpec((SUBLANES, tm), lambda i: (0, i)), row(LANES), row(LANES),
                   pl.BlockSpec((1, SUBLANES, LANES), lambda i: (i, 0, 0)), _const_spec((SUBLANES, LANES))],
        out_shape=[jax.ShapeDtypeStruct((n, d), F32), jax.ShapeDtypeStruct((n, d), F32),
                   jax.ShapeDtypeStruct((SUBLANES, n), jnp.int32), jax.ShapeDtypeStruct((n, LANES), jnp.int32),
                   jax.ShapeDtypeStruct((n, LANES), F32),
                   jax.ShapeDtypeStruct((n // tm, SUBLANES, LANES), jnp.int32),
                   jax.ShapeDtypeStruct((SUBLANES, LANES), jnp.int32)],
        scratch_shapes=[pltpu.VMEM((SUBLANES, LANES), F32)],
        compiler_params=_cparams("arbitrary"),
        name="out_proj1_router",
    )(cat, x2d, w_out.astype(BF16), g1, nw.reshape(1, d), sc, sh, rw, rb)


def _local_rows(tr):
    return 2 * tr + N_EXPERTS * SUBLANES


def _xs_rows(n):
    worst = 2 * n + (n // TOKEN_TILE) * N_EXPERTS * (SUBLANES - 1)
    return (-(-worst // MOE_TILE) + N_EXPERTS) * MOE_TILE


def _local_pos(e_k, r_k, delta_ref, tile):
    shift = jnp.zeros_like(r_k)
    for e in range(N_EXPERTS):
        shift = jnp.where(e_k == e, delta_ref[tile * N_EXPERTS + e], shift)
    return r_k + shift


def _for_each_group(tile, lstart_ref, run_ref, gstart_ref, fn):
    shift = int(math.log2(SUBLANES))
    for e in range(N_EXPERTS):
        k = tile * N_EXPERTS + e
        l_start = lstart_ref[k]
        g_start = gstart_ref[k]

        def body(g, c, l_start=l_start, g_start=g_start):
            off = g * SUBLANES
            fn(pl.multiple_of(l_start + off, SUBLANES), pl.multiple_of(g_start + off, SUBLANES))
            return c

        lax.fori_loop(0, run_ref[k] >> shift, body, 0)


def _zero_fill_gaps(gap_ref, used_ref, xs_ref, zero_ref, sem):
    zero_ref[...] = jnp.zeros_like(zero_ref)
    zr = zero_ref.shape[0]
    per_tile = MOE_TILE // zr
    shift = int(math.log2(SUBLANES))

    def gap_copy(e, g):
        row = pl.multiple_of(gap_ref[e] + g * SUBLANES, SUBLANES)
        return pltpu.make_async_copy(zero_ref.at[pl.ds(0, SUBLANES)], xs_ref.at[pl.ds(row, SUBLANES)], sem)

    def tile_copy(k):
        row = pl.multiple_of(k * zr, zr)
        return pltpu.make_async_copy(zero_ref, xs_ref.at[pl.ds(row, zr)], sem)

    def both(op):
        for e in range(N_EXPERTS):
            lax.fori_loop(0, gap_ref[N_EXPERTS + e] >> shift, lambda g, c, e=e: (op(gap_copy(e, g)), c)[1], 0)
        lax.fori_loop(used_ref[0] * per_tile, (xs_ref.shape[0] // MOE_TILE) * per_tile,
                      lambda k, c: (op(tile_copy(k)), c)[1], 0)

    both(lambda cp: cp.start())
    both(lambda cp: cp.wait())


def _dispatch_kernel(delta_ref, lstart_ref, run_ref, gstart_ref, gap_ref, used_ref, hn_ref, meta_ref, xs_ref,
                     sbuf_ref, zero_ref, sem):
    j = pl.program_id(0)
    tr = hn_ref.shape[0]
    lrows = sbuf_ref.shape[1]
    slot = lax.rem(j, 2)

    @pl.when(j == 0)
    def _():
        _zero_fill_gaps(gap_ref, used_ref, xs_ref, zero_ref, sem.at[0])

    meta = meta_ref[...]
    lp1 = _local_pos(meta[0:1], meta[2:3], delta_ref, j)
    lp2 = _local_pos(meta[1:2], meta[3:4], delta_ref, j)
    r_idx = lax.broadcasted_iota(jnp.int32, (lrows, tr), 0)
    onehot = jnp.logical_or(r_idx == lp1, r_idx == lp2).astype(BF16)
    sbuf_ref[slot] = jnp.dot(onehot, hn_ref[...].astype(BF16), preferred_element_type=F32)

    def group_copy(buf, local_row, xs_row):
        return pltpu.make_async_copy(sbuf_ref.at[buf, pl.ds(local_row, SUBLANES)],
                                     xs_ref.at[pl.ds(xs_row, SUBLANES)], sem.at[buf])

    def drain(tile, buf):
        _for_each_group(tile, lstart_ref, run_ref, gstart_ref, lambda lr, xr: group_copy(buf, lr, xr).wait())

    _for_each_group(j, lstart_ref, run_ref, gstart_ref, lambda lr, xr: group_copy(slot, lr, xr).start())

    @pl.when(j > 0)
    def _():
        drain(j - 1, 1 - slot)

    @pl.when(j == pl.num_programs(0) - 1)
    def _():
        drain(j, slot)


def _dispatch(hn, meta_t, tables, gaps, used_tiles):
    n, d = hn.shape
    tr = TOKEN_TILE
    lrows = _local_rows(tr)
    return pl.pallas_call(
        _dispatch_kernel,
        grid_spec=pltpu.PrefetchScalarGridSpec(
            num_scalar_prefetch=6,
            grid=(n // tr,),
            in_specs=[pl.BlockSpec((tr, d), lambda j, *_: (j, 0)),
                      pl.BlockSpec((SUBLANES, tr), lambda j, *_: (0, j))],
            out_specs=pl.BlockSpec(memory_space=pl.ANY),
            scratch_shapes=[pltpu.VMEM((2, lrows, d), F32), pltpu.VMEM((MOE_SUB, d), F32),
                            pltpu.SemaphoreType.DMA((2,))]),
        out_shape=jax.ShapeDtypeStruct((_xs_rows(n), d), F32),
        compiler_params=_cparams("arbitrary"),
        name="moe_dispatch",
    )(*tables, gaps, used_tiles, hn, meta_t)


def _moe_kernel(te_ref, hi_ref, x_ref, wg_ref, wu_ref, wd_ref, o_ref, xb_ref):
    w = pl.program_id(0)
    f = pl.program_id(1)
    tm = x_ref.shape[0]
    sub = MOE_SUB
    sub_shift = int(math.log2(sub))
    hi = hi_ref[w]

    def swiglu_part(xb, wg, wu, wd):
        hg = jnp.dot(xb, wg, preferred_element_type=F32)
        hu = jnp.dot(xb, wu, preferred_element_type=F32)
        act = (_silu(hg) * hu).astype(BF16)
        return jnp.dot(act, wd, preferred_element_type=F32)

    @pl.when(jnp.logical_and(f == 0, hi > 0))
    def _():
        row = lax.broadcasted_iota(jnp.int32, (tm, 1), 0)
        xb_ref[...] = jnp.where(row < hi, x_ref[...], 0.0).astype(BF16)

    @pl.when(hi == tm)
    def _():
        part = swiglu_part(xb_ref[...], wg_ref[0].astype(BF16), wu_ref[0].astype(BF16),
                           wd_ref[0].astype(BF16))

        @pl.when(f == 0)
        def _():
            o_ref[...] = part

        @pl.when(f != 0)
        def _():
            o_ref[...] += part

    @pl.when(hi < tm)
    def _():
        @pl.when(f == 0)
        def _():
            o_ref[...] = jnp.zeros_like(o_ref)

        def sub_block(s, carry):
            rows = pl.ds(pl.multiple_of(s * sub, sub), sub)
            o_ref[rows, :] += swiglu_part(xb_ref[rows, :], wg_ref[0].astype(BF16),
                                          wu_ref[0].astype(BF16), wd_ref[0].astype(BF16))
            return carry

        lax.fori_loop(0, (hi + sub - 1) >> sub_shift, sub_block, 0)


def _moe_ffn(xs, tile_expert, tile_rows, wg, wu, wd):
    rows, d = xs.shape
    tm = MOE_TILE
    tf = MOE_FF_TILE
    nf = wg.shape[2] // tf

    def f_idx(f, hi):
        v = (hi > 0).astype(jnp.int32)
        return f * v + (nf - 1) * (1 - v)

    return pl.pallas_call(
        _moe_kernel,
        grid_spec=pltpu.PrefetchScalarGridSpec(
            num_scalar_prefetch=2,
            grid=(rows // tm, nf),
            in_specs=[pl.BlockSpec((tm, d), lambda w, f, te, hi: (w, 0)),
                      pl.BlockSpec((1, d, tf), lambda w, f, te, hi: (te[w], 0, f_idx(f, hi[w]))),
                      pl.BlockSpec((1, d, tf), lambda w, f, te, hi: (te[w], 0, f_idx(f, hi[w]))),
                      pl.BlockSpec((1, tf, d), lambda w, f, te, hi: (te[w], f_idx(f, hi[w]), 0))],
            out_specs=pl.BlockSpec((tm, d), lambda w, f, te, hi: (w, 0)),
            scratch_shapes=[pltpu.VMEM((tm, d), BF16)]),
        out_shape=jax.ShapeDtypeStruct((rows, d), F32),
        compiler_params=_cparams("arbitrary", "arbitrary"),
        name="moe_expert_swiglu",
    )(tile_expert, tile_rows, xs, wg, wu, wd)


def _combine_kernel(delta_ref, lstart_ref, run_ref, gstart_ref, ys_ref, x_ref, meta_ref, wt_ref, g2_ref, fw_ref,
                    o_ref, ybuf_ref, sem):
    j = pl.program_id(0)
    n_tiles = pl.num_programs(0)
    tr = x_ref.shape[0]
    lrows = ybuf_ref.shape[1]
    slot = lax.rem(j, 2)

    def group_copy(buf, local_row, xs_row):
        return pltpu.make_async_copy(ys_ref.at[pl.ds(xs_row, SUBLANES)],
                                     ybuf_ref.at[buf, pl.ds(local_row, SUBLANES)], sem.at[buf])

    def fetch(tile, buf):
        ybuf_ref[buf, 2 * tr:, :] = jnp.zeros((lrows - 2 * tr, ybuf_ref.shape[2]), F32)
        _for_each_group(tile, lstart_ref, run_ref, gstart_ref,
                        lambda lr, xr: group_copy(buf, lr, xr).start())

    @pl.when(j == 0)
    def _():
        fetch(0, 0)

    _for_each_group(j, lstart_ref, run_ref, gstart_ref, lambda lr, xr: group_copy(slot, lr, xr).wait())

    @pl.when(j + 1 < n_tiles)
    def _():
        fetch(j + 1, 1 - slot)

    meta = meta_ref[...]
    wt = wt_ref[...]
    lp1 = _local_pos(meta[:, 0:1], meta[:, 2:3], delta_ref, j)
    lp2 = _local_pos(meta[:, 1:2], meta[:, 3:4], delta_ref, j)
    l_idx = lax.broadcasted_iota(jnp.int32, (tr, lrows), 1)
    pick = jnp.where(l_idx == lp1, wt[:, 0:1], 0.0) + jnp.where(l_idx == lp2, wt[:, 1:2], 0.0)
    ffn = _bdot(pick, ybuf_ref[slot])
    x4 = x_ref[...] + g2_ref[0] * ffn
    o_ref[...] = (x4 * _rms_scale(x4)) * fw_ref[...]


def _combine(ys, tables, x3, meta, wt, g2, final_w, seq_len):
    n, d = x3.shape
    tr = TOKEN_TILE
    tps = seq_len // tr
    lrows = -(-_local_rows(tr) // LANES) * LANES
    return pl.pallas_call(
        _combine_kernel,
        grid_spec=pltpu.PrefetchScalarGridSpec(
            num_scalar_prefetch=4,
            grid=(n // tr,),
            in_specs=[pl.BlockSpec(memory_space=pl.ANY),
                      pl.BlockSpec((tr, d), lambda j, *_: (j, 0)),
                      pl.BlockSpec((tr, LANES), lambda j, *_: (j, 0)),
                      pl.BlockSpec((tr, LANES), lambda j, *_: (j, 0)),
                      pl.BlockSpec((1, 1, d), lambda j, *_: (j // tps, 0, 0)),
                      pl.BlockSpec((1, d), lambda j, *_: (0, 0))],
            out_specs=pl.BlockSpec((tr, d), lambda j, *_: (j, 0)),
            scratch_shapes=[pltpu.VMEM((2, lrows, d), F32), pltpu.SemaphoreType.DMA((2,))]),
        out_shape=jax.ShapeDtypeStruct((n, d), F32),
        compiler_params=_cparams("arbitrary"),
        name="moe_combine_final_norm",
    )(*tables, ys, x3, meta, wt, g2, final_w.reshape(1, d))


def _moe_tables(tile_base, counts, n_tokens):
    i32 = lambda t: t.astype(jnp.int32)
    tm = MOE_TILE
    before = tile_base[:, 0, :N_EXPERTS]
    total = counts[0, :N_EXPERTS]
    run = jnp.concatenate([before[1:], total[None]], axis=0) - before
    run = (run + SUBLANES - 1) // SUBLANES * SUBLANES
    l_end = jnp.cumsum(run, axis=1)
    l_start = l_end - run
    g_size = jnp.sum(run, axis=0)
    g_tiles = (g_size + tm - 1) // tm
    tile_end = jnp.cumsum(g_tiles)
    g_off = (tile_end - g_tiles) * tm
    g_end = g_off + g_size
    g_start = g_off[None, :] + jnp.cumsum(run, axis=0) - run
    delta = l_start - before
    gaps = jnp.concatenate([g_end, tile_end * tm - g_end])
    used_tiles = tile_end[-1:]

    w = jnp.arange(_xs_rows(n_tokens) // tm, dtype=jnp.int32)
    te = jnp.minimum(jnp.sum((tile_end[None, :] <= w[:, None]).astype(jnp.int32), axis=1), N_EXPERTS - 1)
    rows = jnp.where(w < tile_end[-1], jnp.clip(jnp.take(g_end, te) - w * tm, 0, tm), 0)
    flat = lambda t: i32(t).reshape(-1)
    return (flat(delta), flat(l_start), flat(run), flat(g_start)), i32(gaps), i32(used_tiles), i32(te), i32(rows)


def kernel(x, c, rel_bias, ada_w, ada_b, norm_mix_w, norm_ffn_w, final_norm_w, ab_w_in, attn_sinks,
           dn_conv_w, dn_a_log, dn_dt_bias, dn_norm_w, ab_w_out, ffn_w_gate, ffn_w_up, ffn_w_down,
           cd_w_in, lru_conv_w, lru_conv_b, lru_gate_a_w, lru_gate_a_b, lru_gate_x_w, lru_gate_x_b,
           lru_lambda, sconv_w, cd_w_out, moe_router_w, moe_router_b, moe_w_gate, moe_w_up, moe_w_down):
    bsz, seq_len, d = x.shape
    n = bsz * seq_len
    x2d = x.reshape(n, d)
    mods = _ada_mods(c, ada_w, ada_b)

    sh1, sc1, g1, sh2, sc2, g2 = (mods[0, k] for k in range(6))
    qa, kd, vd, qn, kn, vb, gs, bexp, gcexp = _in_proj0(
        x2d, norm_mix_w[0], sc1, sh1, ab_w_in[0], dn_conv_w[0], dn_a_log[0], dn_dt_bias[0], seq_len)
    attn = _attention(qa, kd, vd, _bias_table(rel_bias), attn_sinks[0], seq_len)
    dn = _deltanet(qn, kn, vb, gs, bexp, gcexp, dn_norm_w[0], seq_len)
    x2 = _mid0(attn, dn, x2d, ab_w_out[0], g1, norm_ffn_w[0], sc2, sh2, g2,
               ffn_w_gate[0], ffn_w_up[0], ffn_w_down[0], seq_len)

    sh1, sc1, g1, sh2, sc2, g2 = (mods[1, k] for k in range(6))
    cat = _mix1(x2, norm_mix_w[1], sc1, sh1, cd_w_in[0], lru_conv_w[0], lru_conv_b[0],
                lru_gate_a_w[0], lru_gate_a_b[0], lru_gate_x_w[0], lru_gate_x_b[0],
                lru_lambda[0], sconv_w[0], seq_len)
    x3, hn4, meta_t, meta, wt, tile_base, counts = _route(
        cat, x2, cd_w_out[0], g1, norm_ffn_w[1], sc2, sh2, moe_router_w[0], moe_router_b[0], seq_len)
    tables, gaps, used_tiles, tile_expert, tile_rows = _moe_tables(tile_base, counts, n)
    xs = _dispatch(hn4, meta_t, tables, gaps, used_tiles)
    ys = _moe_ffn(xs, tile_expert, tile_rows, moe_w_gate[0], moe_w_up[0], moe_w_down[0])
    out = _combine(ys, tables, x3, meta, wt, g2, final_norm_w, seq_len)
    return out.reshape(bsz, seq_len, d)
```

```python
import functools
import math

import numpy as np
import jax
import jax.numpy as jnp
from jax import lax
from jax.experimental import pallas as pl
from jax.experimental.pallas import tpu as pltpu

D_MODEL = 1024
EPS = 1e-6
HEAD_DIM = 64
A_Q_HEADS = 8
A_KV_HEADS = 2
WINDOW = 128
N_BUCKETS = 32
MAX_DISTANCE = 128
B_HEADS = 8
B_CONV = 4
CHUNK = 64
A_Q_W = A_Q_HEADS * HEAD_DIM
A_KV_W = A_KV_HEADS * HEAD_DIM
B_W = B_HEADS * HEAD_DIM
B_QKV_W = 3 * B_W
LRU_WIDTH = D_MODEL
LRU_BLOCKS = 8
LRU_C = 8.0
SC_WIDTH = D_MODEL // 2
D_FF = 2816
N_EXPERTS = 8
D_FF_EXPERT = 3584

LANES = 128
SUBLANES = 8
VMEM_LIMIT_BYTES = 56 * 1024 * 1024
TOKEN_TILE = 512
MOE_TILE = 1024
MOE_SUB = 256
MOE_FF_TILE = 512
NEG_BIG = -1e30

F32 = jnp.float32
BF16 = jnp.bfloat16


def _cparams(*sem):
    return pltpu.CompilerParams(dimension_semantics=tuple(sem), vmem_limit_bytes=VMEM_LIMIT_BYTES)


def _const_spec(shape):
    nd = len(shape)
    return pl.BlockSpec(shape, lambda *_: (0,) * nd)


def _bdot(a, b):
    return jnp.dot(a.astype(BF16), b.astype(BF16), preferred_element_type=F32)


def _bdot_nt(a, b):
    return lax.dot_general(a.astype(BF16), b.astype(BF16), (((1,), (1,)), ((), ())),
                           preferred_element_type=F32)


def _bdot_tn(a, b):
    return lax.dot_general(a.astype(BF16), b.astype(BF16), (((0,), (0,)), ((), ())),
                           preferred_element_type=F32)


def _split(x, n):
    parts = []
    r = x
    for i in range(n):
        p = r.astype(BF16)
        parts.append(p)
        if i + 1 < n:
            r = r - p.astype(F32)
    return parts


def _dot_x(a, b, na=2, nb=2):
    asp = _split(a, na) if na > 1 else [a.astype(BF16)]
    bsp = _split(b, nb) if nb > 1 else [b.astype(BF16)]
    acc = None
    for i, ai in enumerate(asp):
        for j, bj in enumerate(bsp):
            if i + j >= max(na, nb):
                continue
            t = jnp.dot(ai, bj, preferred_element_type=F32)
            acc = t if acc is None else acc + t
    return acc


def _dot_terms(a, b_stacked, n):
    return jnp.dot(jnp.concatenate(_split(a, n), axis=1), b_stacked, preferred_element_type=F32)


def _silu(x):
    return x * (1.0 / (1.0 + jnp.exp(-x)))


def _sigmoid(x):
    return 1.0 / (1.0 + jnp.exp(-x))


def _log1p(z):
    u = 1.0 + z
    tiny = u == 1.0
    return jnp.where(tiny, z, jnp.log(u) * (z / jnp.where(tiny, 1.0, u - 1.0)))


def _softplus(x):
    return jnp.maximum(x, 0.0) + _log1p(jnp.exp(-jnp.abs(x)))


def _rms_scale(x):
    width = x.shape[1]
    mean_w = jnp.full((width, LANES), 1.0 / width, BF16)
    ms = _dot_x(x * x, mean_w, 2, 1)
    r = lax.rsqrt(ms + EPS)
    return jnp.concatenate([r] * (width // LANES), axis=1)


def _norm_mod(x, w, sc, sh, on_mxu=False):
    if on_mxu:
        scale = _rms_scale(x)
    else:
        scale = lax.rsqrt(jnp.mean(x * x, axis=-1, keepdims=True) + EPS)
    return (x * scale) * w * (1.0 + sc) + sh


def _shift_rows(x, k, prev_tail):
    n, width = x.shape
    x3 = x.reshape(n // SUBLANES, SUBLANES, width)
    rot = pltpu.roll(x3, k, 1)
    rot_prev = jnp.concatenate([pltpu.roll(prev_tail, k, 0)[None], rot[:-1]], axis=0)
    sub = lax.broadcasted_iota(jnp.int32, x3.shape, 1)
    return jnp.where(sub >= k, rot, rot_prev).reshape(n, width)


def _ada_kernel(c_ref, w_ref, b_ref, o_ref):
    c = c_ref[...]
    cond = _silu(c)
    o_ref[0] = _dot_x(cond, w_ref[0], 3, 2) + b_ref[0]


def _ada_mods(c, ada_w, ada_b):
    depth, d, six_d = ada_w.shape
    bsz = c.shape[0]
    rows = max(SUBLANES, bsz)
    c_pad = jnp.zeros((rows, d), F32).at[:bsz].set(c)
    tn = 1536
    out = pl.pallas_call(
        _ada_kernel,
        grid=(depth, six_d // tn),
        in_specs=[pl.BlockSpec((rows, d), lambda l, j: (0, 0)),
                  pl.BlockSpec((1, d, tn), lambda l, j: (l, 0, j)),
                  pl.BlockSpec((1, 1, tn), lambda l, j: (l, 0, j))],
        out_specs=pl.BlockSpec((1, rows, tn), lambda l, j: (l, 0, j)),
        out_shape=jax.ShapeDtypeStruct((depth, rows, six_d), F32),
        compiler_params=_cparams("parallel", "parallel"),
        name="ada_mods",
    )(c_pad, ada_w, ada_b.reshape(depth, 1, six_d))
    return out[:, :bsz].reshape(depth, bsz, 6, 1, d).transpose(0, 2, 1, 3, 4)


def _t5_bucket(dist):
    max_exact = N_BUCKETS // 2
    d = np.maximum(dist, 0)
    large = max_exact + (np.log(np.maximum(d, 1) / max_exact) / math.log(MAX_DISTANCE / max_exact)
                         * (N_BUCKETS - max_exact)).astype(np.int32)
    large = np.minimum(large, N_BUCKETS - 1)
    return np.where(d < max_exact, d, large).astype(np.int32)


def _band_buckets():
    qi = np.arange(WINDOW)[:, None]
    s = np.arange(2 * WINDOW)[None, :]
    dist = qi + WINDOW - s
    in_window = (dist >= 0) & (dist < WINDOW)
    return np.where(in_window, _t5_bucket(dist), -1).astype(np.int32)


def _bias_kernel(rb_ref, bucket_ref, o_ref):
    h = pl.program_id(0)
    bucket = bucket_ref[...]
    acc = jnp.zeros(bucket.shape, F32)
    for b in range(N_BUCKETS):
        acc = jnp.where(bucket == b, rb_ref[b, h], acc)
    o_ref[0] = jnp.where(bucket < 0, NEG_BIG, acc)


def _bias_table(rel_bias):
    bucket = jnp.asarray(_band_buckets())
    out = pl.pallas_call(
        _bias_kernel,
        grid=(A_Q_HEADS,),
        in_specs=[pl.BlockSpec(memory_space=pltpu.SMEM),
                  _const_spec((WINDOW, 2 * WINDOW))],
        out_specs=pl.BlockSpec((1, WINDOW, 2 * WINDOW), lambda h: (h, 0, 0)),
        out_shape=jax.ShapeDtypeStruct((A_Q_HEADS, WINDOW, 2 * WINDOW), F32),
        compiler_params=_cparams("parallel"),
        name="attn_bias_table",
    )(rel_bias, bucket)
    return out.reshape(A_Q_HEADS // 2, 2 * WINDOW, 2 * WINDOW)


_C_QA = 0
_C_KA = _C_QA + A_Q_W
_C_VA = _C_KA + A_KV_W
_C_QKV = _C_VA + A_KV_W
_C_GATE = _C_QKV + B_QKV_W
_C_SMALL = _C_GATE + B_W
_AB_COLS = _C_SMALL + LANES


def _ab_in_weight(w_in):
    return jnp.pad(w_in, ((0, 0), (0, _AB_COLS - w_in.shape[1]))).astype(BF16)


def _dup_heads(t, low):
    swapped = pltpu.roll(t, HEAD_DIM, 1)
    return jnp.concatenate([jnp.where(low, t, swapped), jnp.where(low, swapped, t)], axis=1)


def _chunk_tril(tm):
    r = np.arange(tm)
    return ((r[:, None] >= r[None, :]) & (r[:, None] // CHUNK == r[None, :] // CHUNK)).astype(np.float32)


def _head_selector():
    e = np.zeros((B_W, LANES), np.float32)
    for h in range(B_HEADS):
        e[h * HEAD_DIM:(h + 1) * HEAD_DIM, h] = 1.0
    return e


def _in0_kernel(x_ref, nw_ref, sc_ref, sh_ref, w_ref, cw_ref, sel_ref, selt2_ref, selt3_ref, tril_ref,
                alog_ref, dtb_ref,
                qa_ref, kd_ref, vd_ref, qn_ref, kn_ref, vb_ref, gs_ref, bexp_ref, gcexp_ref,
                tail_ref, *, tiles_per_seq):
    i = pl.program_id(0)

    @pl.when(i % tiles_per_seq == 0)
    def _():
        tail_ref[...] = jnp.zeros_like(tail_ref)

    hn = _norm_mod(x_ref[...], nw_ref[...], sc_ref[0], sh_ref[0])
    proj = jnp.dot(hn.astype(BF16), w_ref[...], preferred_element_type=F32)
    tm = proj.shape[0]
    low = lax.broadcasted_iota(jnp.int32, (tm, LANES), 1) < HEAD_DIM

    small = proj[:, _C_SMALL:]
    lane = lax.broadcasted_iota(jnp.int32, small.shape, 1)
    beta = jnp.where(lane < B_HEADS, _sigmoid(small), 0.0)
    dec = pltpu.roll(small, LANES - B_HEADS, 1)
    g = jnp.where(lane < B_HEADS, -jnp.exp(alog_ref[...]) * _softplus(dec + dtb_ref[...]), 0.0)
    bexp_ref[...] = _dot_terms(beta, selt2_ref[...], 2)
    gc = _dot_x(tril_ref[...], g, 1, 3)
    gcexp_ref[...] = _dot_terms(gc, selt3_ref[...], 3)

    qa_ref[...] = proj[:, _C_QA:_C_KA].astype(BF16)
    kd_ref[...] = _dup_heads(proj[:, _C_KA:_C_VA], low).astype(BF16)
    vd_ref[...] = _dup_heads(proj[:, _C_VA:_C_QKV], low).astype(BF16)

    def conv_silu(block):
        cols = slice(block * B_W, (block + 1) * B_W)
        xq = proj[:, _C_QKV + block * B_W:_C_QKV + (block + 1) * B_W]
        tail = tail_ref[:, cols]
        cw = cw_ref[:, cols]
        y = xq * cw[B_CONV - 1:B_CONV]
        for k in range(1, B_CONV):
            y = y + _shift_rows(xq, k, tail) * cw[B_CONV - 1 - k:B_CONV - k]
        tail_ref[:, cols] = xq[tm - SUBLANES:]
        return _silu(y)

    q = conv_silu(0)
    k_ = conv_silu(1)
    ssq = _dot_x(jnp.concatenate([q * q, k_ * k_], axis=1), sel_ref[...], 2, 1)
    r = lax.rsqrt(ssq + EPS)
    q_scale = _dot_terms(r[:, :LANES], selt2_ref[...], 2)
    k_scale = _dot_terms(r[:, LANES:], selt2_ref[...], 2)
    vb_ref[...] = conv_silu(2)
    gs_ref[...] = _silu(proj[:, _C_GATE:_C_SMALL])
    qn_ref[...] = q * q_scale * (HEAD_DIM ** -0.5)
    kn_ref[...] = k_ * k_scale


def _in_proj0(x2d, nw, sc, sh, w_in, conv_w, a_log, dt_bias, seq_len):
    n, d = x2d.shape
    tm = TOKEN_TILE
    tiles_per_seq = seq_len // tm
    w = _ab_in_weight(w_in)
    hs = _head_selector()
    zeros = np.zeros_like(hs)
    sel = jnp.asarray(np.block([[hs, zeros], [zeros, hs]]), BF16)
    selt2 = jnp.asarray(np.tile(hs.T, (2, 1)), BF16)
    selt3 = jnp.asarray(np.tile(hs.T, (3, 1)), BF16)
    tril = jnp.asarray(_chunk_tril(tm), BF16)
    pad8 = lambda v: jnp.zeros((1, LANES), F32).at[0, :B_HEADS].set(v)
    row = lambda width: pl.BlockSpec((tm, width), lambda i: (i, 0))
    per_b = pl.BlockSpec((1, 1, d), lambda i: (i // tiles_per_seq, 0, 0))
    outs = pl.pallas_call(
        functools.partial(_in0_kernel, tiles_per_seq=tiles_per_seq),
        grid=(n // tm,),
        in_specs=[row(d), _const_spec((1, d)), per_b, per_b,
                  _resident_spec((d, _AB_COLS)), _const_spec((B_CONV, B_QKV_W)),
                  _const_spec(sel.shape), _const_spec(selt2.shape), _const_spec(selt3.shape),
                  _const_spec((tm, tm)), _const_spec((1, LANES)), _const_spec((1, LANES))],
        out_specs=[row(A_Q_W), row(2 * A_KV_W), row(2 * A_KV_W)] + [row(B_W)] * 6,
        out_shape=[jax.ShapeDtypeStruct((n, A_Q_W), BF16),
                   jax.ShapeDtypeStruct((n, 2 * A_KV_W), BF16),
                   jax.ShapeDtypeStruct((n, 2 * A_KV_W), BF16)]
        + [jax.ShapeDtypeStruct((n, B_W), F32)] * 6,
        scratch_shapes=[pltpu.VMEM((SUBLANES, B_QKV_W), F32)],
        compiler_params=_cparams("arbitrary"),
        name="in_proj0",
    )(x2d, nw.reshape(1, d), sc, sh, w, conv_w, sel, selt2, selt3, tril, pad8(a_log), pad8(dt_bias))
    return outs


_ATTN_BLOCKS = 2


def _attn_kernel(sink_ref, q_ref, kp_ref, kc_ref, vp_ref, vc_ref, bm_ref, o_ref, *, steps_per_seq):
    i = pl.program_id(0)
    first = (i % steps_per_seq) == 0
    w = WINDOW
    lane = lax.broadcasted_iota(jnp.int32, (w, LANES), 1)
    low = lane < HEAD_DIM
    col = lax.broadcasted_iota(jnp.int32, (2 * w, 2 * w), 1)
    row = lax.broadcasted_iota(jnp.int32, (2 * w, 1), 0)
    prev_dead = jnp.logical_and(first, col < w)
    zero = jnp.zeros((), q_ref.dtype)
    pairs = A_Q_HEADS // 2
    units = [(b, j) for b in range(_ATTN_BLOCKS) for j in range(pairs)]

    def keys(p_ref, c_ref, b, kh):
        ls = slice(kh * LANES, (kh + 1) * LANES)
        before = p_ref[:, ls] if b == 0 else c_ref[(b - 1) * w:b * w, ls]
        return jnp.concatenate([before, c_ref[b * w:(b + 1) * w, ls]], axis=0)

    kv_of = lambda j: (2 * j) // (A_Q_HEADS // A_KV_HEADS)
    qp = [q_ref[b * w:(b + 1) * w, j * LANES:(j + 1) * LANES] for b, j in units]
    qs = [jnp.concatenate([jnp.where(low, t, zero), jnp.where(low, zero, t)], axis=0) for t in qp]
    kd = [keys(kp_ref, kc_ref, b, kv_of(j)) for b, j in units]
    vd = [keys(vp_ref, vc_ref, b, kv_of(j)) for b, j in units]
    s = [lax.dot_general(a, k, (((1,), (1,)), ((), ())), preferred_element_type=F32) for a, k in zip(qs, kd)]
    s = [t * (HEAD_DIM ** -0.5) + bm_ref[j] for t, (b, j) in zip(s, units)]
    s = [jnp.where(prev_dead, NEG_BIG, t) if b == 0 else t for t, (b, j) in zip(s, units)]
    sink = [jnp.where(row < w, sink_ref[2 * j], sink_ref[2 * j + 1]) for b, j in units]
    m = [jnp.maximum(jnp.max(t, axis=-1, keepdims=True), sk) for t, sk in zip(s, sink)]
    p = [jnp.exp(t - mt) for t, mt in zip(s, m)]
    denom = [jnp.sum(t, axis=-1, keepdims=True) + jnp.exp(sk - mt) for t, sk, mt in zip(p, sink, m)]
    pv = [jnp.dot(t.astype(BF16), v, preferred_element_type=F32) / dn for t, v, dn in zip(p, vd, denom)]
    outs = [jnp.where(low, t[:w], t[w:]) for t in pv]
    for b in range(_ATTN_BLOCKS):
        o_ref[b * w:(b + 1) * w, :] = jnp.concatenate(outs[b * pairs:(b + 1) * pairs], axis=1).astype(o_ref.dtype)


def _attention(qa, kd, vd, bias_tbl, sinks, seq_len):
    n = qa.shape[0]
    w = WINDOW
    rows = _ATTN_BLOCKS * w
    steps = seq_len // rows
    cur = lambda i: (i, 0)
    prev = lambda i: (jnp.where(i % steps == 0, i * _ATTN_BLOCKS, i * _ATTN_BLOCKS - 1), 0)
    return pl.pallas_call(
        functools.partial(_attn_kernel, steps_per_seq=steps),
        grid=(n // rows,),
        in_specs=[pl.BlockSpec(memory_space=pltpu.SMEM),
                  pl.BlockSpec((rows, A_Q_W), cur),
                  pl.BlockSpec((w, 2 * A_KV_W), prev), pl.BlockSpec((rows, 2 * A_KV_W), cur),
                  pl.BlockSpec((w, 2 * A_KV_W), prev), pl.BlockSpec((rows, 2 * A_KV_W), cur),
                  _const_spec((A_Q_HEADS // 2, 2 * w, 2 * w))],
        out_specs=pl.BlockSpec((rows, A_Q_W), cur),
        out_shape=jax.ShapeDtypeStruct((n, A_Q_W), BF16),
        compiler_params=_cparams("parallel"),
        name="swa_attention",
    )(sinks, qa, kd, kd, vd, vd, bias_tbl)


_DN_PAIRS = B_HEADS // 2
_DN_INV_BLOCK = 16
_DN_GROUP = 4


def _block_diag(x, low):
    zero = jnp.zeros((), x.dtype)
    return jnp.concatenate([jnp.where(low, x, zero), jnp.where(low, zero, x)], axis=0)


def _dn_intra(chunks, data_refs, work_refs, consts):
    qn_ref, kn_ref, vb_ref, bexp_ref, gcexp_ref = data_refs
    u_ref, w_ref, qk_ref, qd_ref, kd_ref, egl_ref = work_refs
    low, i_idx, j_idx, ones3 = consts
    c = CHUNK
    units = [(ci, p) for ci in chunks for p in range(_DN_PAIRS)]
    where = [(slice(ci * c, (ci + 1) * c), slice(p * LANES, (p + 1) * LANES)) for ci, p in units]
    causal = i_idx >= j_idx
    strict = i_idx > j_idx
    on_diag = i_idx == j_idx
    eye = on_diag.astype(F32)
    blk_shift = int(math.log2(_DN_INV_BLOCK))
    same_blk = (i_idx >> blk_shift) == (j_idx >> blk_shift)

    q = [qn_ref[rs, ls] for rs, ls in where]
    k = [kn_ref[rs, ls] for rs, ls in where]
    v = [vb_ref[rs, ls] for rs, ls in where]
    b = [bexp_ref[rs, ls] for rs, ls in where]
    gc = [gcexp_ref[rs, ls] for rs, ls in where]

    gr = [jnp.dot(ones3, jnp.concatenate(_split(jnp.where(on_diag, t, 0.0), 3), axis=0),
                  preferred_element_type=F32) for t in gc]
    ks = [_block_diag(t.astype(BF16), low) for t in k]
    qkk = [lax.dot_general(jnp.concatenate([qt, kt], axis=0).astype(BF16), kst,
                           (((1,), (1,)), ((), ())), preferred_element_type=F32)
           for qt, kt, kst in zip(q, k, ks)]
    decay = [jnp.exp(jnp.where(causal, gct - grt, NEG_BIG)) for gct, grt in zip(gc, gr)]
    lmat = [jnp.where(strict, bt * t[c:] * dt, 0.0) for bt, t, dt in zip(b, qkk, decay)]
    qk = [jnp.where(causal, t[:c] * dt, 0.0) for t, dt in zip(qkk, decay)]

    def mm(xs, ys):
        return [_bdot(x, _block_diag(y.astype(BF16), low)) for x, y in zip(xs, ys)]

    l_diag = [jnp.where(same_blk, t, 0.0) for t in lmat]
    l_off = [t - d for t, d in zip(lmat, l_diag)]
    pw = [-t for t in l_diag]
    d_inv = [eye + t for t in pw]
    for _ in range(blk_shift - 1):
        pw = mm(pw, pw)
        d_inv = mm(d_inv, [eye + t for t in pw])
    pw = [-t for t in mm(d_inv, l_off)]
    acc = [eye + t for t in pw]
    for _ in range(int(math.log2(c // _DN_INV_BLOCK)) - 1):
        pw = mm(pw, pw)
        acc = mm(acc, [eye + t for t in pw])
    tmat = mm(acc, d_inv)

    egc = [jnp.exp(t) for t in gc]
    rhs = [jnp.concatenate([_block_diag((vt * bt).astype(BF16), low),
                            _block_diag((kt * (bt * et)).astype(BF16), low)], axis=1)
           for vt, kt, bt, et in zip(v, k, b, egc)]
    uw = [_bdot(t, r) for t, r in zip(tmat, rhs)]
    for n, (ci, p) in enumerate(units):
        g_last = gc[n][c - 1:c, :]
        u_ref[ci, p] = uw[n][:, :LANES]
        w_ref[ci, p] = uw[n][:, LANES:]
        qk_ref[ci, p] = qk[n]
        qd_ref[ci, p] = q[n] * egc[n]
        kd_ref[ci, p] = k[n] * jnp.exp(g_last - gc[n])
        egl_ref[ci, p] = jnp.broadcast_to(jnp.exp(g_last), (SUBLANES, LANES))


def _dn_scan(ci, work_refs, s_ref, gs_ref, nw, o_ref, consts):
    u_ref, w_ref, qk_ref, qd_ref, kd_ref, egl_ref = work_refs
    low, mask_bd, head_mean2 = consts
    c = CHUNK
    rows = slice(ci * c, (ci + 1) * c)
    pairs = range(_DN_PAIRS)
    s_old = [s_ref[p] for p in pairs]
    wq = [_bdot(jnp.concatenate([w_ref[ci, p], qd_ref[ci, p]], axis=0), s_old[p]) for p in pairs]
    v_new = [u_ref[ci, p] - wq[p][:c] for p in pairs]
    o = [wq[p][c:] + _bdot(qk_ref[ci, p], _block_diag(v_new[p].astype(BF16), low)) for p in pairs]
    kv = [_bdot_tn(kd_ref[ci, p], v_new[p]) for p in pairs]
    for p in pairs:
        s_ref[p] = s_old[p] * egl_ref[ci, p][0:1, :] + jnp.where(mask_bd, kv[p], 0.0)
    ms = [jnp.dot(jnp.concatenate(_split(t * t, 2), axis=1), head_mean2, preferred_element_type=F32)
          for t in o]
    for p in pairs:
        ls = slice(p * LANES, (p + 1) * LANES)
        y = (o[p] * lax.rsqrt(ms[p] + EPS)) * nw * gs_ref[rows, ls]
        o_ref[rows, ls] = y.astype(o_ref.dtype)


def _dn_kernel(qn_ref, kn_ref, vb_ref, gs_ref, bexp_ref, gcexp_ref, nw_ref, o_ref,
               s_ref, u_ref, w_ref, qk_ref, qd_ref, kd_ref, egl_ref, *, groups_per_seq):
    i = pl.program_id(0)

    @pl.when(i % groups_per_seq == 0)
    def _():
        s_ref[...] = jnp.zeros_like(s_ref)

    c = CHUNK
    tm = o_ref.shape[0]
    n_chunks = tm // c
    lane = lax.broadcasted_iota(jnp.int32, (c, LANES), 1)
    low = lane < HEAD_DIM
    i_idx = lax.broadcasted_iota(jnp.int32, (c, LANES), 0)
    j_idx = lane & (c - 1)
    ones3 = jnp.ones((c, 3 * c), BF16)
    rb = lax.broadcasted_iota(jnp.int32, (LANES, LANES), 0)
    cb = lax.broadcasted_iota(jnp.int32, (LANES, LANES), 1)
    mask_bd = (rb < HEAD_DIM) == (cb < HEAD_DIM)
    head_mean = jnp.where(mask_bd, 1.0 / HEAD_DIM, 0.0).astype(BF16)
    head_mean2 = jnp.concatenate([head_mean, head_mean], axis=0)
    data_refs = (qn_ref, kn_ref, vb_ref, bexp_ref, gcexp_ref)
    work_refs = (u_ref, w_ref, qk_ref, qd_ref, kd_ref, egl_ref)
    intra_consts = (low, i_idx, j_idx, ones3)
    scan_consts = (low, mask_bd, head_mean2)
    nw = nw_ref[...]

    groups = [list(range(s, s + _DN_GROUP)) for s in range(0, n_chunks, _DN_GROUP)]
    _dn_intra(groups[0], data_refs, work_refs, intra_consts)
    for j, grp in enumerate(groups):
        if j + 1 < len(groups):
            _dn_intra(groups[j + 1], data_refs, work_refs, intra_consts)
        for ci in grp:
            _dn_scan(ci, work_refs, s_ref, gs_ref, nw, o_ref, scan_consts)


def _deltanet(qn, kn, vb, gs, bexp, gcexp, norm_w, seq_len):
    n = qn.shape[0]
    tm = TOKEN_TILE
    nw2 = jnp.concatenate([norm_w, norm_w]).reshape(1, LANES)
    row = lambda width: pl.BlockSpec((tm, width), lambda i: (i, 0))
    return pl.pallas_call(
        functools.partial(_dn_kernel, groups_per_seq=seq_len // tm),
        grid=(n // tm,),
        in_specs=[row(B_W)] * 6 + [_const_spec((1, LANES))],
        out_specs=row(B_W),
        out_shape=jax.ShapeDtypeStruct((n, B_W), BF16),
        scratch_shapes=[pltpu.VMEM((_DN_PAIRS, LANES, LANES), F32)]
        + [pltpu.VMEM((tm // CHUNK, _DN_PAIRS, CHUNK, LANES), F32)] * 5
        + [pltpu.VMEM((tm // CHUNK, _DN_PAIRS, SUBLANES, LANES), F32)],
        compiler_params=_cparams("arbitrary"),
        name="gated_deltanet",
    )(qn, kn, vb, gs, bexp, gcexp, nw2)


def _resident_spec(shape):
    nd = len(shape)
    return pl.BlockSpec(shape, lambda *_: (0,) * nd, pipeline_mode=pl.Buffered(1))


def _mid0_kernel(attn_ref, dn_ref, x_ref, wo_ref, g1_ref, nw_ref, sc_ref, sh_ref, g2_ref,
                 wg_ref, wu_ref, wd_ref, o_ref):
    mix = (jnp.dot(attn_ref[...], wo_ref[:A_Q_W], preferred_element_type=F32)
           + jnp.dot(dn_ref[...], wo_ref[A_Q_W:], preferred_element_type=F32))
    x1 = x_ref[...] + g1_ref[0] * mix
    hn = _norm_mod(x1, nw_ref[...], sc_ref[0], sh_ref[0]).astype(BF16)
    hg = jnp.dot(hn, wg_ref[...], preferred_element_type=F32)
    hu = jnp.dot(hn, wu_ref[...], preferred_element_type=F32)
    act = (_silu(hg) * hu).astype(BF16)
    o_ref[...] = x1 + g2_ref[0] * jnp.dot(act, wd_ref[...], preferred_element_type=F32)


def _mid0(attn, dn, x2d, w_out, g1, nw, sc, sh, g2, wg, wu, wd, seq_len):
    n, d = x2d.shape
    tm = TOKEN_TILE
    tps = seq_len // tm
    row = lambda width: pl.BlockSpec((tm, width), lambda i: (i, 0))
    per_b = pl.BlockSpec((1, 1, d), lambda i: (i // tps, 0, 0))
    return pl.pallas_call(
        _mid0_kernel,
        grid=(n // tm,),
        in_specs=[row(A_Q_W), row(B_W), row(d), _resident_spec(w_out.shape), per_b,
                  _const_spec((1, d)), per_b, per_b, per_b,
                  _resident_spec(wg.shape), _resident_spec(wu.shape), _resident_spec(wd.shape)],
        out_specs=row(d),
        out_shape=jax.ShapeDtypeStruct((n, d), F32),
        compiler_params=_cparams("parallel"),
        name="out_proj0_swiglu",
    )(attn, dn, x2d, w_out.astype(BF16), g1, nw.reshape(1, d), sc, sh, g2,
      wg.astype(BF16), wu.astype(BF16), wd.astype(BF16))


def _gelu_tanh(x):
    return 0.5 * x * (1.0 + jnp.tanh(math.sqrt(2.0 / math.pi) * (x + 0.044715 * (x * x * x))))


def _linear_scan(a, b, h0):
    n, width = a.shape
    groups = n // SUBLANES
    a = a.reshape(groups, SUBLANES, width)
    b = b.reshape(groups, SUBLANES, width)
    in_group = lax.broadcasted_iota(jnp.int32, a.shape, 1)
    s = 1
    while s < SUBLANES:
        a_sh = pltpu.roll(a, s, 1)
        b_sh = pltpu.roll(b, s, 1)
        valid = in_group >= s
        b = jnp.where(valid, a * b_sh + b, b)
        a = jnp.where(valid, a * a_sh, a)
        s *= 2
    carry = jnp.broadcast_to(h0, (SUBLANES, width))
    out = []
    for g in range(groups):
        hg = a[g] * carry + b[g]
        out.append(hg)
        carry = jnp.broadcast_to(hg[SUBLANES - 1:SUBLANES, :], hg.shape)
    return jnp.concatenate(out, axis=0)


def _mix1_tile(x, nw, sc, sh, w_ref, cw_ref, cb_ref, ga_ref, gab_ref, gx_ref, gxb_ref, lam_ref, sw_ref,
               tail_c_ref, tail_d_ref, h_ref):
    hn = _norm_mod(x, nw, sc, sh, on_mxu=True).astype(BF16)
    proj = jnp.dot(hn, w_ref[...], preferred_element_type=F32)
    w_l = LRU_WIDTH
    xc_in = proj[:, :w_l]
    yc = proj[:, w_l:2 * w_l]
    bd = proj[:, 2 * w_l:2 * w_l + SC_WIDTH]
    cd = proj[:, 2 * w_l + SC_WIDTH:2 * w_l + 2 * SC_WIDTH]
    hd = proj[:, 2 * w_l + 2 * SC_WIDTH:]
    tm = xc_in.shape[0]

    kc = cw_ref.shape[0]
    tail = tail_c_ref[...]
    cw = cw_ref[...]
    xc = xc_in * cw[kc - 1:kc] + cb_ref[...]
    for k in range(1, kc):
        xc = xc + _shift_rows(xc_in, k, tail) * cw[kc - 1 - k:kc - k]
    tail_c_ref[...] = xc_in[tm - SUBLANES:]

    xb = xc.astype(BF16)
    gw = ga_ref.shape[1]
    ra, ri = [], []
    for p in range(ga_ref.shape[0]):
        xin = xb[:, p * gw:(p + 1) * gw]
        ra.append(jnp.dot(xin, ga_ref[p], preferred_element_type=F32))
        ri.append(jnp.dot(xin, gx_ref[p], preferred_element_type=F32))
    r = _sigmoid(jnp.concatenate(ra, axis=1) + gab_ref[...])
    ig = _sigmoid(jnp.concatenate(ri, axis=1) + gxb_ref[...])
    log_a = (-LRU_C) * r * _softplus(-lam_ref[...])
    a = jnp.exp(log_a)
    b = jnp.sqrt(-jnp.tanh(log_a) * (a * a + 1.0)) * (ig * xc)
    h = _linear_scan(a, b, h_ref[0:1, :])
    h_ref[...] = jnp.broadcast_to(h[tm - 1:tm, :], h_ref.shape)
    yc_out = h * _gelu_tanh(yc)

    ks = sw_ref.shape[0]
    ch = cd * hd
    tail_d = tail_d_ref[...]
    sw = sw_ref[...]
    conv = ch * sw[ks - 1:ks]
    for k in range(1, ks):
        conv = conv + _shift_rows(ch, k, tail_d) * sw[ks - 1 - k:ks - k]
    tail_d_ref[...] = ch[tm - SUBLANES:]
    return jnp.concatenate([yc_out, bd * conv], axis=1)


def _pair_block_diag(gw):
    nb, bw, _ = gw.shape
    g2 = gw.reshape(nb // 2, 2, bw, bw)
    z = jnp.zeros((nb // 2, bw, bw), gw.dtype)
    top = jnp.concatenate([g2[:, 0], z], axis=2)
    bot = jnp.concatenate([z, g2[:, 1]], axis=2)
    return jnp.concatenate([top, bot], axis=1).astype(BF16)


def _route_tile(cat, x, wo_ref, g1, nw, sc, sh, rw_ref, rb_ref, carry_ref):
    x3 = x + g1 * jnp.dot(cat, wo_ref[...], preferred_element_type=F32)
    hn = _norm_mod(x3, nw, sc, sh)
    tm = hn.shape[0]
    lane = lax.broadcasted_iota(jnp.int32, (tm, LANES), 1)
    h_hi, h_lo = _split(hn, 2)
    both = jnp.dot(h_hi, rw_ref[...], preferred_element_type=F32)
    logits = (both[:, :LANES] + both[:, LANES:]
              + jnp.dot(h_lo, rw_ref[:, :LANES], preferred_element_type=F32) + rb_ref[...])
    lg = jnp.where(lane < N_EXPERTS, logits, NEG_BIG)
    m1 = jnp.max(lg, axis=1, keepdims=True)
    i1 = jnp.min(jnp.where(lg == m1, lane, LANES), axis=1, keepdims=True)
    lg2 = jnp.where(lane == i1, NEG_BIG, lg)
    m2 = jnp.max(lg2, axis=1, keepdims=True)
    i2 = jnp.min(jnp.where(lg2 == m2, lane, LANES), axis=1, keepdims=True)
    e2 = jnp.exp(m2 - m1)
    w1 = 1.0 / (1.0 + e2)
    w2 = e2 / (1.0 + e2)

    hit1 = lane == i1
    hit2 = lane == i2
    sel = jnp.logical_or(hit1, hit2).astype(F32)
    r_i = lax.broadcasted_iota(jnp.int32, (tm, tm), 0)
    c_i = lax.broadcasted_iota(jnp.int32, (tm, tm), 1)
    tril = (r_i >= c_i).astype(BF16)
    incl = jnp.dot(tril, sel.astype(BF16), preferred_element_type=F32)
    carry = carry_ref[0:1, :]
    excl = incl - sel + carry
    r1 = jnp.sum(jnp.where(hit1, excl, 0.0), axis=1, keepdims=True)
    r2 = jnp.sum(jnp.where(hit2, excl, 0.0), axis=1, keepdims=True)
    total = carry + incl[tm - 1:tm, :]
    carry_ref[...] = jnp.broadcast_to(total, carry_ref.shape)

    meta = jnp.where(lane == 0, i1, 0)
    meta = jnp.where(lane == 1, i2, meta)
    meta = jnp.where(lane == 2, r1.astype(jnp.int32), meta)
    meta = jnp.where(lane == 3, r2.astype(jnp.int32), meta)
    wt = jnp.where(lane == 0, w1, jnp.where(lane == 1, w2, 0.0))
    return x3, hn, meta, wt, carry, total


def _mix1_kernel(x_ref, nw_ref, sc_ref, sh_ref, w_ref, cw_ref, cb_ref, ga_ref, gab_ref, gx_ref, gxb_ref,
                 lam_ref, sw_ref, o_ref, tail_c_ref, tail_d_ref, h_ref, *, tiles_per_seq):
    i = pl.program_id(0)

    @pl.when(i % tiles_per_seq == 0)
    def _():
        tail_c_ref[...] = jnp.zeros_like(tail_c_ref)
        tail_d_ref[...] = jnp.zeros_like(tail_d_ref)
        h_ref[...] = jnp.zeros_like(h_ref)

    cat = _mix1_tile(x_ref[...], nw_ref[...], sc_ref[0], sh_ref[0], w_ref, cw_ref, cb_ref, ga_ref,
                     gab_ref, gx_ref, gxb_ref, lam_ref, sw_ref, tail_c_ref, tail_d_ref, h_ref)
    o_ref[...] = cat.astype(o_ref.dtype)


def _mix1(x2d, nw, sc, sh, w_in, conv_w, conv_b, ga_w, ga_b, gx_w, gx_b, lam, sconv_w, seq_len):
    n, d = x2d.shape
    tm = TOKEN_TILE
    tps = seq_len // tm
    cd_in = w_in.shape[1]
    cd_out = LRU_WIDTH + SC_WIDTH
    row = lambda width: pl.BlockSpec((tm, width), lambda i: (i, 0))
    per_b = pl.BlockSpec((1, 1, d), lambda i: (i // tps, 0, 0))
    ga = _pair_block_diag(ga_w)
    gx = _pair_block_diag(gx_w)
    vec = lambda v: v.reshape(1, -1)
    return pl.pallas_call(
        functools.partial(_mix1_kernel, tiles_per_seq=tps),
        grid=(n // tm,),
        in_specs=[row(d), _const_spec((1, d)), per_b, per_b, _resident_spec((d, cd_in)),
                  _const_spec(conv_w.shape), _const_spec((1, LRU_WIDTH)),
                  _const_spec(ga.shape), _const_spec((1, LRU_WIDTH)),
                  _const_spec(gx.shape), _const_spec((1, LRU_WIDTH)),
                  _const_spec((1, LRU_WIDTH)), _const_spec(sconv_w.shape)],
        out_specs=row(cd_out),
        out_shape=jax.ShapeDtypeStruct((n, cd_out), BF16),
        scratch_shapes=[pltpu.VMEM((SUBLANES, LRU_WIDTH), F32), pltpu.VMEM((SUBLANES, SC_WIDTH), F32),
                        pltpu.VMEM((SUBLANES, LRU_WIDTH), F32)],
        compiler_params=_cparams("arbitrary"),
        name="rglru_shortconv_mixer",
    )(x2d, vec(nw), sc, sh, w_in.astype(BF16), conv_w, vec(conv_b), ga, vec(ga_b), gx, vec(gx_b),
      vec(lam), sconv_w)


def _route_kernel(cat_ref, x_ref, wo_ref, g1_ref, nw_ref, sc_ref, sh_ref, rw_ref, rb_ref,
                  x3_ref, hn_ref, metat_ref, meta_ref, wt_ref, base_ref, cnt_ref, carry_ref):
    @pl.when(pl.program_id(0) == 0)
    def _():
        carry_ref[...] = jnp.zeros_like(carry_ref)

    x3, hn, meta, wt, before, total = _route_tile(cat_ref[...], x_ref[...], wo_ref, g1_ref[0], nw_ref[...],
                                                  sc_ref[0], sh_ref[0], rw_ref, rb_ref, carry_ref)
    x3_ref[...] = x3
    hn_ref[...] = hn
    meta_ref[...] = meta
    metat_ref[...] = jnp.transpose(meta.astype(F32))[:SUBLANES].astype(jnp.int32)
    wt_ref[...] = wt
    base_ref[0] = jnp.broadcast_to(before, base_ref.shape[1:]).astype(jnp.int32)
    cnt_ref[...] = jnp.broadcast_to(total, cnt_ref.shape).astype(jnp.int32)


def _route(cat, x2d, w_out, g1, nw, sc, sh, router_w, router_b, seq_len):
    n, d = x2d.shape
    tm = TOKEN_TILE
    tps = seq_len // tm
    row = lambda width: pl.BlockSpec((tm, width), lambda i: (i, 0))
    per_b = pl.BlockSpec((1, 1, d), lambda i: (i // tps, 0, 0))
    rw = jnp.zeros((d, LANES), F32).at[:, :N_EXPERTS].set(router_w)
    rw_hi = rw.astype(BF16)
    rw = jnp.concatenate([rw_hi, (rw - rw_hi.astype(F32)).astype(BF16)], axis=1)
    rb = jnp.zeros((1, LANES), F32).at[0, :N_EXPERTS].set(router_b)
    return pl.pallas_call(
        _route_kernel,
        grid=(n // tm,),
        in_specs=[row(cat.shape[1]), row(d), _resident_spec(w_out.shape), per_b, _const_spec((1, d)),
                  per_b, per_b, _const_spec((d, 2 * LANES)), _const_spec((1, LANES))],
        out_specs=[row(d), row(d), pl.BlockSpec((SUBLANES, tm), lambda i: (0, i)), row(LANES), row(LANES),
                   pl.BlockSpec((1, SUBLANES, LANES), lambda i: (i, 0, 0)), _const_spec((SUBLANES, LANES))],
        out_shape=[jax.ShapeDtypeStruct((n, d), F32), jax.ShapeDtypeStruct((n, d), F32),
                   jax.ShapeDtypeStruct((SUBLANES, n), jnp.int32), jax.ShapeDtypeStruct((n, LANES), jnp.int32),
                   jax.ShapeDtypeStruct((n, LANES), F32),
                   jax.ShapeDtypeStruct((n // tm, SUBLANES, LANES), jnp.int32),
                   jax.ShapeDtypeStruct((SUBLANES, LANES), jnp.int32)],
        scratch_shapes=[pltpu.VMEM((SUBLANES, LANES), F32)],
        compiler_params=_cparams("arbitrary"),
        name="out_proj1_router",
    )(cat, x2d, w_out.astype(BF16), g1, nw.reshape(1, d), sc, sh, rw, rb)


def _local_rows(tr):
    return 2 * tr + N_EXPERTS * SUBLANES


def _xs_rows(n):
    worst = 2 * n + (n // TOKEN_TILE) * N_EXPERTS * (SUBLANES - 1)
    return (-(-worst // MOE_TILE) + N_EXPERTS) * MOE_TILE


def _local_pos(e_k, r_k, delta_ref, tile):
    shift = jnp.zeros_like(r_k)
    for e in range(N_EXPERTS):
        shift = jnp.where(e_k == e, delta_ref[tile * N_EXPERTS + e], shift)
    return r_k + shift


_RUN_BLOCK = 64


def _for_each_group(tile, lstart_ref, run_ref, gstart_ref, fn):
    big_shift = int(math.log2(_RUN_BLOCK))
    small_shift = int(math.log2(SUBLANES))
    for e in range(N_EXPERTS):
        k = tile * N_EXPERTS + e
        l_start = lstart_ref[k]
        g_start = gstart_ref[k]
        n_big = run_ref[k] >> big_shift
        tail = n_big << big_shift

        def big(g, c, l_start=l_start, g_start=g_start):
            off = g * _RUN_BLOCK
            fn(pl.multiple_of(l_start + off, SUBLANES), pl.multiple_of(g_start + off, SUBLANES), _RUN_BLOCK)
            return c

        def small(g, c, l_start=l_start, g_start=g_start, tail=tail):
            off = tail + g * SUBLANES
            fn(pl.multiple_of(l_start + off, SUBLANES), pl.multiple_of(g_start + off, SUBLANES), SUBLANES)
            return c

        lax.fori_loop(0, n_big, big, 0)
        lax.fori_loop(0, (run_ref[k] - tail) >> small_shift, small, 0)


def _zero_fill_gaps(gap_ref, used_ref, xs_ref, zero_ref, sem):
    zero_ref[...] = jnp.zeros_like(zero_ref)
    zr = zero_ref.shape[0]
    per_tile = MOE_TILE // zr
    shift = int(math.log2(SUBLANES))

    def gap_copy(e, g):
        row = pl.multiple_of(gap_ref[e] + g * SUBLANES, SUBLANES)
        return pltpu.make_async_copy(zero_ref.at[pl.ds(0, SUBLANES)], xs_ref.at[pl.ds(row, SUBLANES)], sem)

    def tile_copy(k):
        row = pl.multiple_of(k * zr, zr)
        return pltpu.make_async_copy(zero_ref, xs_ref.at[pl.ds(row, zr)], sem)

    def both(op):
        for e in range(N_EXPERTS):
            lax.fori_loop(0, gap_ref[N_EXPERTS + e] >> shift, lambda g, c, e=e: (op(gap_copy(e, g)), c)[1], 0)
        lax.fori_loop(used_ref[0] * per_tile, (xs_ref.shape[0] // MOE_TILE) * per_tile,
                      lambda k, c: (op(tile_copy(k)), c)[1], 0)

    both(lambda cp: cp.start())
    both(lambda cp: cp.wait())


def _dispatch_kernel(delta_ref, lstart_ref, run_ref, gstart_ref, gap_ref, used_ref, hn_ref, meta_ref, xs_ref,
                     sbuf_ref, zero_ref, sem):
    j = pl.program_id(0)
    tr = hn_ref.shape[0]
    lrows = sbuf_ref.shape[1]
    slot = lax.rem(j, 2)

    @pl.when(j == 0)
    def _():
        _zero_fill_gaps(gap_ref, used_ref, xs_ref, zero_ref, sem.at[0])

    meta = meta_ref[...]
    lp1 = _local_pos(meta[0:1], meta[2:3], delta_ref, j)
    lp2 = _local_pos(meta[1:2], meta[3:4], delta_ref, j)
    r_idx = lax.broadcasted_iota(jnp.int32, (lrows, tr), 0)
    onehot = jnp.logical_or(r_idx == lp1, r_idx == lp2).astype(BF16)
    sbuf_ref[slot] = jnp.dot(onehot, hn_ref[...].astype(BF16), preferred_element_type=F32)

    def group_copy(buf, local_row, xs_row, rows):
        return pltpu.make_async_copy(sbuf_ref.at[buf, pl.ds(local_row, rows)],
                                     xs_ref.at[pl.ds(xs_row, rows)], sem.at[buf])

    def drain(tile, buf):
        _for_each_group(tile, lstart_ref, run_ref, gstart_ref,
                        lambda lr, xr, rows: group_copy(buf, lr, xr, rows).wait())

    _for_each_group(j, lstart_ref, run_ref, gstart_ref,
                    lambda lr, xr, rows: group_copy(slot, lr, xr, rows).start())

    @pl.when(j > 0)
    def _():
        drain(j - 1, 1 - slot)

    @pl.when(j == pl.num_programs(0) - 1)
    def _():
        drain(j, slot)


def _dispatch(hn, meta_t, tables, gaps, used_tiles):
    n, d = hn.shape
    tr = TOKEN_TILE
    lrows = _local_rows(tr)
    return pl.pallas_call(
        _dispatch_kernel,
        grid_spec=pltpu.PrefetchScalarGridSpec(
            num_scalar_prefetch=6,
            grid=(n // tr,),
            in_specs=[pl.BlockSpec((tr, d), lambda j, *_: (j, 0)),
                      pl.BlockSpec((SUBLANES, tr), lambda j, *_: (0, j))],
            out_specs=pl.BlockSpec(memory_space=pl.ANY),
            scratch_shapes=[pltpu.VMEM((2, lrows, d), F32), pltpu.VMEM((MOE_SUB, d), F32),
                            pltpu.SemaphoreType.DMA((2,))]),
        out_shape=jax.ShapeDtypeStruct((_xs_rows(n), d), F32),
        compiler_params=_cparams("arbitrary"),
        name="moe_dispatch",
    )(*tables, gaps, used_tiles, hn, meta_t)


def _moe_kernel(te_ref, hi_ref, x_ref, wg_ref, wu_ref, wd_ref, o_ref, xb_ref):
    w = pl.program_id(0)
    f = pl.program_id(1)
    tm = x_ref.shape[0]
    sub = MOE_SUB
    sub_shift = int(math.log2(sub))
    hi = hi_ref[w]

    def swiglu_part(xb, wg, wu, wd):
        hg = jnp.dot(xb, wg, preferred_element_type=F32)
        hu = jnp.dot(xb, wu, preferred_element_type=F32)
        act = (_silu(hg) * hu).astype(BF16)
        return jnp.dot(act, wd, preferred_element_type=F32)

    @pl.when(jnp.logical_and(f == 0, hi > 0))
    def _():
        row = lax.broadcasted_iota(jnp.int32, (tm, 1), 0)
        xb_ref[...] = jnp.where(row < hi, x_ref[...], 0.0).astype(BF16)

    @pl.when(hi == tm)
    def _():
        part = swiglu_part(xb_ref[...], wg_ref[0].astype(BF16), wu_ref[0].astype(BF16),
                           wd_ref[0].astype(BF16))

        @pl.when(f == 0)
        def _():
            o_ref[...] = part

        @pl.when(f != 0)
        def _():
            o_ref[...] += part

    @pl.when(hi < tm)
    def _():
        @pl.when(f == 0)
        def _():
            o_ref[...] = jnp.zeros_like(o_ref)

        def sub_block(s, carry):
            rows = pl.ds(pl.multiple_of(s * sub, sub), sub)
            o_ref[rows, :] += swiglu_part(xb_ref[rows, :], wg_ref[0].astype(BF16),
                                          wu_ref[0].astype(BF16), wd_ref[0].astype(BF16))
            return carry

        lax.fori_loop(0, (hi + sub - 1) >> sub_shift, sub_block, 0)


def _moe_ffn(xs, tile_expert, tile_rows, wg, wu, wd):
    rows, d = xs.shape
    tm = MOE_TILE
    tf = MOE_FF_TILE
    nf = wg.shape[2] // tf

    def f_idx(f, hi):
        v = (hi > 0).astype(jnp.int32)
        return f * v + (nf - 1) * (1 - v)

    return pl.pallas_call(
        _moe_kernel,
        grid_spec=pltpu.PrefetchScalarGridSpec(
            num_scalar_prefetch=2,
            grid=(rows // tm, nf),
            in_specs=[pl.BlockSpec((tm, d), lambda w, f, te, hi: (w, 0)),
                      pl.BlockSpec((1, d, tf), lambda w, f, te, hi: (te[w], 0, f_idx(f, hi[w]))),
                      pl.BlockSpec((1, d, tf), lambda w, f, te, hi: (te[w], 0, f_idx(f, hi[w]))),
                      pl.BlockSpec((1, tf, d), lambda w, f, te, hi: (te[w], f_idx(f, hi[w]), 0))],
            out_specs=pl.BlockSpec((tm, d), lambda w, f, te, hi: (w, 0)),
            scratch_shapes=[pltpu.VMEM((tm, d), BF16)]),
        out_shape=jax.ShapeDtypeStruct((rows, d), F32),
        compiler_params=_cparams("arbitrary", "arbitrary"),
        name="moe_expert_swiglu",
    )(tile_expert, tile_rows, xs, wg, wu, wd)


def _combine_kernel(delta_ref, lstart_ref, run_ref, gstart_ref, ys_ref, x_ref, meta_ref, wt_ref, g2_ref, fw_ref,
                    o_ref, ybuf_ref, sem):
    j = pl.program_id(0)
    n_tiles = pl.num_programs(0)
    tr = x_ref.shape[0]
    lrows = ybuf_ref.shape[1]
    slot = lax.rem(j, 2)

    def group_copy(buf, local_row, xs_row, rows):
        return pltpu.make_async_copy(ys_ref.at[pl.ds(xs_row, rows)],
                                     ybuf_ref.at[buf, pl.ds(local_row, rows)], sem.at[buf])

    def fetch(tile, buf):
        ybuf_ref[buf, 2 * tr:, :] = jnp.zeros((lrows - 2 * tr, ybuf_ref.shape[2]), F32)
        _for_each_group(tile, lstart_ref, run_ref, gstart_ref,
                        lambda lr, xr, rows: group_copy(buf, lr, xr, rows).start())

    @pl.when(j == 0)
    def _():
        fetch(0, 0)

    _for_each_group(j, lstart_ref, run_ref, gstart_ref,
                    lambda lr, xr, rows: group_copy(slot, lr, xr, rows).wait())

    @pl.when(j + 1 < n_tiles)
    def _():
        fetch(j + 1, 1 - slot)

    meta = meta_ref[...]
    wt = wt_ref[...]
    lp1 = _local_pos(meta[:, 0:1], meta[:, 2:3], delta_ref, j)
    lp2 = _local_pos(meta[:, 1:2], meta[:, 3:4], delta_ref, j)
    l_idx = lax.broadcasted_iota(jnp.int32, (tr, lrows), 1)
    pick = jnp.where(l_idx == lp1, wt[:, 0:1], 0.0) + jnp.where(l_idx == lp2, wt[:, 1:2], 0.0)
    ffn = _bdot(pick, ybuf_ref[slot])
    x4 = x_ref[...] + g2_ref[0] * ffn
    o_ref[...] = (x4 * _rms_scale(x4)) * fw_ref[...]


def _combine(ys, tables, x3, meta, wt, g2, final_w, seq_len):
    n, d = x3.shape
    tr = TOKEN_TILE
    tps = seq_len // tr
    lrows = -(-_local_rows(tr) // LANES) * LANES
    return pl.pallas_call(
        _combine_kernel,
        grid_spec=pltpu.PrefetchScalarGridSpec(
            num_scalar_prefetch=4,
            grid=(n // tr,),
            in_specs=[pl.BlockSpec(memory_space=pl.ANY),
                      pl.BlockSpec((tr, d), lambda j, *_: (j, 0)),
                      pl.BlockSpec((tr, LANES), lambda j, *_: (j, 0)),
                      pl.BlockSpec((tr, LANES), lambda j, *_: (j, 0)),
                      pl.BlockSpec((1, 1, d), lambda j, *_: (j // tps, 0, 0)),
                      pl.BlockSpec((1, d), lambda j, *_: (0, 0))],
            out_specs=pl.BlockSpec((tr, d), lambda j, *_: (j, 0)),
            scratch_shapes=[pltpu.VMEM((2, lrows, d), F32), pltpu.SemaphoreType.DMA((2,))]),
        out_shape=jax.ShapeDtypeStruct((n, d), F32),
        compiler_params=_cparams("arbitrary"),
        name="moe_combine_final_norm",
    )(*tables, ys, x3, meta, wt, g2, final_w.reshape(1, d))


def _moe_tables(tile_base, counts, n_tokens):
    i32 = lambda t: t.astype(jnp.int32)
    tm = MOE_TILE
    before = tile_base[:, 0, :N_EXPERTS]
    total = counts[0, :N_EXPERTS]
    run = jnp.concatenate([before[1:], total[None]], axis=0) - before
    run = (run + SUBLANES - 1) // SUBLANES * SUBLANES
    l_end = jnp.cumsum(run, axis=1)
    l_start = l_end - run
    g_size = jnp.sum(run, axis=0)
    g_tiles = (g_size + tm - 1) // tm
    tile_end = jnp.cumsum(g_tiles)
    g_off = (tile_end - g_tiles) * tm
    g_end = g_off + g_size
    g_start = g_off[None, :] + jnp.cumsum(run, axis=0) - run
    delta = l_start - before
    gaps = jnp.concatenate([g_end, tile_end * tm - g_end])
    used_tiles = tile_end[-1:]

    w = jnp.arange(_xs_rows(n_tokens) // tm, dtype=jnp.int32)
    te = jnp.minimum(jnp.sum((tile_end[None, :] <= w[:, None]).astype(jnp.int32), axis=1), N_EXPERTS - 1)
    rows = jnp.where(w < tile_end[-1], jnp.clip(jnp.take(g_end, te) - w * tm, 0, tm), 0)
    flat = lambda t: i32(t).reshape(-1)
    return (flat(delta), flat(l_start), flat(run), flat(g_start)), i32(gaps), i32(used_tiles), i32(te), i32(rows)


def kernel(x, c, rel_bias, ada_w, ada_b, norm_mix_w, norm_ffn_w, final_norm_w, ab_w_in, attn_sinks,
           dn_conv_w, dn_a_log, dn_dt_bias, dn_norm_w, ab_w_out, ffn_w_gate, ffn_w_up, ffn_w_down,
           cd_w_in, lru_conv_w, lru_conv_b, lru_gate_a_w, lru_gate_a_b, lru_gate_x_w, lru_gate_x_b,
           lru_lambda, sconv_w, cd_w_out, moe_router_w, moe_router_b, moe_w_gate, moe_w_up, moe_w_down):
    bsz, seq_len, d = x.shape
    n = bsz * seq_len
    x2d = x.reshape(n, d)
    mods = _ada_mods(c, ada_w, ada_b)

    sh1, sc1, g1, sh2, sc2, g2 = (mods[0, k] for k in range(6))
    qa, kd, vd, qn, kn, vb, gs, bexp, gcexp = _in_proj0(
        x2d, norm_mix_w[0], sc1, sh1, ab_w_in[0], dn_conv_w[0], dn_a_log[0], dn_dt_bias[0], seq_len)
    attn = _attention(qa, kd, vd, _bias_table(rel_bias), attn_sinks[0], seq_len)
    dn = _deltanet(qn, kn, vb, gs, bexp, gcexp, dn_norm_w[0], seq_len)
    x2 = _mid0(attn, dn, x2d, ab_w_out[0], g1, norm_ffn_w[0], sc2, sh2, g2,
               ffn_w_gate[0], ffn_w_up[0], ffn_w_down[0], seq_len)

    sh1, sc1, g1, sh2, sc2, g2 = (mods[1, k] for k in range(6))
    cat = _mix1(x2, norm_mix_w[1], sc1, sh1, cd_w_in[0], lru_conv_w[0], lru_conv_b[0],
                lru_gate_a_w[0], lru_gate_a_b[0], lru_gate_x_w[0], lru_gate_x_b[0],
                lru_lambda[0], sconv_w[0], seq_len)
    x3, hn4, meta_t, meta, wt, tile_base, counts = _route(
        cat, x2, cd_w_out[0], g1, norm_ffn_w[1], sc2, sh2, moe_router_w[0], moe_router_b[0], seq_len)
    tables, gaps, used_tiles, tile_expert, tile_rows = _moe_tables(tile_base, counts, n)
    xs = _dispatch(hn4, meta_t, tables, gaps, used_tiles)
    ys = _moe_ffn(xs, tile_expert, tile_rows, moe_w_gate[0], moe_w_up[0], moe_w_down[0])
    out = _combine(ys, tables, x3, meta, wt, g2, final_norm_w, seq_len)
    return out.reshape(bsz, seq_len, d)
```

```python
import functools
import math

import numpy as np
import jax
import jax.numpy as jnp
from jax import lax
from jax.experimental import pallas as pl
from jax.experimental.pallas import tpu as pltpu

D_MODEL = 1024
EPS = 1e-6
HEAD_DIM = 64
A_Q_HEADS = 8
A_KV_HEADS = 2
WINDOW = 128
N_BUCKETS = 32
MAX_DISTANCE = 128
B_HEADS = 8
B_CONV = 4
CHUNK = 64
A_Q_W = A_Q_HEADS * HEAD_DIM
A_KV_W = A_KV_HEADS * HEAD_DIM
B_W = B_HEADS * HEAD_DIM
B_QKV_W = 3 * B_W
LRU_WIDTH = D_MODEL
LRU_BLOCKS = 8
LRU_C = 8.0
SC_WIDTH = D_MODEL // 2
D_FF = 2816
N_EXPERTS = 8
D_FF_EXPERT = 3584

LANES = 128
SUBLANES = 8
VMEM_LIMIT_BYTES = 56 * 1024 * 1024
TOKEN_TILE = 512
MOE_TILE = 1024
MOE_SUB = 256
MOE_FF_TILE = 512
NEG_BIG = -1e30

F32 = jnp.float32
BF16 = jnp.bfloat16


def _cparams(*sem):
    return pltpu.CompilerParams(dimension_semantics=tuple(sem), vmem_limit_bytes=VMEM_LIMIT_BYTES)


def _const_spec(shape):
    nd = len(shape)
    return pl.BlockSpec(shape, lambda *_: (0,) * nd)


def _bdot(a, b):
    return jnp.dot(a.astype(BF16), b.astype(BF16), preferred_element_type=F32)


def _bdot_nt(a, b):
    return lax.dot_general(a.astype(BF16), b.astype(BF16), (((1,), (1,)), ((), ())),
                           preferred_element_type=F32)


def _bdot_tn(a, b):
    return lax.dot_general(a.astype(BF16), b.astype(BF16), (((0,), (0,)), ((), ())),
                           preferred_element_type=F32)


def _split(x, n):
    parts = []
    r = x
    for i in range(n):
        p = r.astype(BF16)
        parts.append(p)
        if i + 1 < n:
            r = r - p.astype(F32)
    return parts


def _dot_x(a, b, na=2, nb=2):
    asp = _split(a, na) if na > 1 else [a.astype(BF16)]
    bsp = _split(b, nb) if nb > 1 else [b.astype(BF16)]
    acc = None
    for i, ai in enumerate(asp):
        for j, bj in enumerate(bsp):
            if i + j >= max(na, nb):
                continue
            t = jnp.dot(ai, bj, preferred_element_type=F32)
            acc = t if acc is None else acc + t
    return acc


def _dot_terms(a, b_stacked, n):
    return jnp.dot(jnp.concatenate(_split(a, n), axis=1), b_stacked, preferred_element_type=F32)


def _silu(x):
    return x * (1.0 / (1.0 + jnp.exp(-x)))


def _sigmoid(x):
    return 1.0 / (1.0 + jnp.exp(-x))


def _log1p(z):
    u = 1.0 + z
    tiny = u == 1.0
    return jnp.where(tiny, z, jnp.log(u) * (z / jnp.where(tiny, 1.0, u - 1.0)))


def _softplus(x):
    return jnp.maximum(x, 0.0) + _log1p(jnp.exp(-jnp.abs(x)))


def _rms_scale(x):
    width = x.shape[1]
    mean_w = jnp.full((width, LANES), 1.0 / width, BF16)
    ms = _dot_x(x * x, mean_w, 2, 1)
    r = lax.rsqrt(ms + EPS)
    return jnp.concatenate([r] * (width // LANES), axis=1)


def _norm_mod(x, w, sc, sh, on_mxu=False):
    if on_mxu:
        scale = _rms_scale(x)
    else:
        scale = lax.rsqrt(jnp.mean(x * x, axis=-1, keepdims=True) + EPS)
    return (x * scale) * w * (1.0 + sc) + sh


def _shift_rows(x, k, prev_tail):
    n, width = x.shape
    x3 = x.reshape(n // SUBLANES, SUBLANES, width)
    rot = pltpu.roll(x3, k, 1)
    rot_prev = jnp.concatenate([pltpu.roll(prev_tail, k, 0)[None], rot[:-1]], axis=0)
    sub = lax.broadcasted_iota(jnp.int32, x3.shape, 1)
    return jnp.where(sub >= k, rot, rot_prev).reshape(n, width)


def _ada_kernel(c_ref, w_ref, b_ref, o_ref):
    c = c_ref[...]
    cond = _silu(c)
    o_ref[0] = _dot_x(cond, w_ref[0], 3, 2) + b_ref[0]


def _ada_mods(c, ada_w, ada_b):
    depth, d, six_d = ada_w.shape
    bsz = c.shape[0]
    rows = max(SUBLANES, bsz)
    c_pad = jnp.zeros((rows, d), F32).at[:bsz].set(c)
    tn = 1536
    out = pl.pallas_call(
        _ada_kernel,
        grid=(depth, six_d // tn),
        in_specs=[pl.BlockSpec((rows, d), lambda l, j: (0, 0)),
                  pl.BlockSpec((1, d, tn), lambda l, j: (l, 0, j)),
                  pl.BlockSpec((1, 1, tn), lambda l, j: (l, 0, j))],
        out_specs=pl.BlockSpec((1, rows, tn), lambda l, j: (l, 0, j)),
        out_shape=jax.ShapeDtypeStruct((depth, rows, six_d), F32),
        compiler_params=_cparams("parallel", "parallel"),
        name="ada_mods",
    )(c_pad, ada_w, ada_b.reshape(depth, 1, six_d))
    return out[:, :bsz].reshape(depth, bsz, 6, 1, d).transpose(0, 2, 1, 3, 4)


def _t5_bucket(dist):
    max_exact = N_BUCKETS // 2
    d = np.maximum(dist, 0)
    large = max_exact + (np.log(np.maximum(d, 1) / max_exact) / math.log(MAX_DISTANCE / max_exact)
                         * (N_BUCKETS - max_exact)).astype(np.int32)
    large = np.minimum(large, N_BUCKETS - 1)
    return np.where(d < max_exact, d, large).astype(np.int32)


def _band_buckets():
    qi = np.arange(WINDOW)[:, None]
    s = np.arange(2 * WINDOW)[None, :]
    dist = qi + WINDOW - s
    in_window = (dist >= 0) & (dist < WINDOW)
    return np.where(in_window, _t5_bucket(dist), -1).astype(np.int32)


def _bias_kernel(rb_ref, bucket_ref, o_ref):
    h = pl.program_id(0)
    bucket = bucket_ref[...]
    acc = jnp.zeros(bucket.shape, F32)
    for b in range(N_BUCKETS):
        acc = jnp.where(bucket == b, rb_ref[b, h], acc)
    o_ref[0] = jnp.where(bucket < 0, NEG_BIG, acc)


def _bias_table(rel_bias):
    bucket = jnp.asarray(_band_buckets())
    out = pl.pallas_call(
        _bias_kernel,
        grid=(A_Q_HEADS,),
        in_specs=[pl.BlockSpec(memory_space=pltpu.SMEM),
                  _const_spec((WINDOW, 2 * WINDOW))],
        out_specs=pl.BlockSpec((1, WINDOW, 2 * WINDOW), lambda h: (h, 0, 0)),
        out_shape=jax.ShapeDtypeStruct((A_Q_HEADS, WINDOW, 2 * WINDOW), F32),
        compiler_params=_cparams("parallel"),
        name="attn_bias_table",
    )(rel_bias, bucket)
    return out.reshape(A_Q_HEADS // 2, 2 * WINDOW, 2 * WINDOW)


_C_QA = 0
_C_KA = _C_QA + A_Q_W
_C_VA = _C_KA + A_KV_W
_C_QKV = _C_VA + A_KV_W
_C_GATE = _C_QKV + B_QKV_W
_C_SMALL = _C_GATE + B_W
_AB_COLS = _C_SMALL + LANES


def _ab_in_weight(w_in):
    return jnp.pad(w_in, ((0, 0), (0, _AB_COLS - w_in.shape[1]))).astype(BF16)


def _dup_heads(t, low):
    swapped = pltpu.roll(t, HEAD_DIM, 1)
    return jnp.concatenate([jnp.where(low, t, swapped), jnp.where(low, swapped, t)], axis=1)


def _chunk_tril(tm):
    r = np.arange(tm)
    return ((r[:, None] >= r[None, :]) & (r[:, None] // CHUNK == r[None, :] // CHUNK)).astype(np.float32)


def _head_selector():
    e = np.zeros((B_W, LANES), np.float32)
    for h in range(B_HEADS):
        e[h * HEAD_DIM:(h + 1) * HEAD_DIM, h] = 1.0
    return e


def _in0_kernel(x_ref, nw_ref, sc_ref, sh_ref, w_ref, cw_ref, sel_ref, selt2_ref, selt3_ref, tril_ref,
                alog_ref, dtb_ref,
                qa_ref, kd_ref, vd_ref, qn_ref, kn_ref, vb_ref, gs_ref, bexp_ref, gcexp_ref,
                tail_ref, *, tiles_per_seq):
    i = pl.program_id(0)

    @pl.when(i % tiles_per_seq == 0)
    def _():
        tail_ref[...] = jnp.zeros_like(tail_ref)

    hn = _norm_mod(x_ref[...], nw_ref[...], sc_ref[0], sh_ref[0])
    proj = jnp.dot(hn.astype(BF16), w_ref[...], preferred_element_type=F32)
    tm = proj.shape[0]
    low = lax.broadcasted_iota(jnp.int32, (tm, LANES), 1) < HEAD_DIM

    small = proj[:, _C_SMALL:]
    lane = lax.broadcasted_iota(jnp.int32, small.shape, 1)
    beta = jnp.where(lane < B_HEADS, _sigmoid(small), 0.0)
    dec = pltpu.roll(small, LANES - B_HEADS, 1)
    g = jnp.where(lane < B_HEADS, -jnp.exp(alog_ref[...]) * _softplus(dec + dtb_ref[...]), 0.0)
    bexp_ref[...] = _dot_terms(beta, selt2_ref[...], 2)
    gc = _dot_x(tril_ref[...], g, 1, 3)
    gcexp_ref[...] = _dot_terms(gc, selt3_ref[...], 3)

    qa_ref[...] = proj[:, _C_QA:_C_KA].astype(BF16)
    kd_ref[...] = _dup_heads(proj[:, _C_KA:_C_VA], low).astype(BF16)
    vd_ref[...] = _dup_heads(proj[:, _C_VA:_C_QKV], low).astype(BF16)

    def conv_silu(block):
        cols = slice(block * B_W, (block + 1) * B_W)
        xq = proj[:, _C_QKV + block * B_W:_C_QKV + (block + 1) * B_W]
        tail = tail_ref[:, cols]
        cw = cw_ref[:, cols]
        y = xq * cw[B_CONV - 1:B_CONV]
        for k in range(1, B_CONV):
            y = y + _shift_rows(xq, k, tail) * cw[B_CONV - 1 - k:B_CONV - k]
        tail_ref[:, cols] = xq[tm - SUBLANES:]
        return _silu(y)

    q = conv_silu(0)
    k_ = conv_silu(1)
    ssq = _dot_x(jnp.concatenate([q * q, k_ * k_], axis=1), sel_ref[...], 2, 1)
    r = lax.rsqrt(ssq + EPS)
    q_scale = _dot_terms(r[:, :LANES], selt2_ref[...], 2)
    k_scale = _dot_terms(r[:, LANES:], selt2_ref[...], 2)
    vb_ref[...] = conv_silu(2)
    gs_ref[...] = _silu(proj[:, _C_GATE:_C_SMALL])
    qn_ref[...] = q * q_scale * (HEAD_DIM ** -0.5)
    kn_ref[...] = k_ * k_scale


def _in_proj0(x2d, nw, sc, sh, w_in, conv_w, a_log, dt_bias, seq_len):
    n, d = x2d.shape
    tm = TOKEN_TILE
    tiles_per_seq = seq_len // tm
    w = _ab_in_weight(w_in)
    hs = _head_selector()
    zeros = np.zeros_like(hs)
    sel = jnp.asarray(np.block([[hs, zeros], [zeros, hs]]), BF16)
    selt2 = jnp.asarray(np.tile(hs.T, (2, 1)), BF16)
    selt3 = jnp.asarray(np.tile(hs.T, (3, 1)), BF16)
    tril = jnp.asarray(_chunk_tril(tm), BF16)
    pad8 = lambda v: jnp.zeros((1, LANES), F32).at[0, :B_HEADS].set(v)
    row = lambda width: pl.BlockSpec((tm, width), lambda i: (i, 0))
    per_b = pl.BlockSpec((1, 1, d), lambda i: (i // tiles_per_seq, 0, 0))
    outs = pl.pallas_call(
        functools.partial(_in0_kernel, tiles_per_seq=tiles_per_seq),
        grid=(n // tm,),
        in_specs=[row(d), _const_spec((1, d)), per_b, per_b,
                  _resident_spec((d, _AB_COLS)), _const_spec((B_CONV, B_QKV_W)),
                  _const_spec(sel.shape), _const_spec(selt2.shape), _const_spec(selt3.shape),
                  _const_spec((tm, tm)), _const_spec((1, LANES)), _const_spec((1, LANES))],
        out_specs=[row(A_Q_W), row(2 * A_KV_W), row(2 * A_KV_W)] + [row(B_W)] * 6,
        out_shape=[jax.ShapeDtypeStruct((n, A_Q_W), BF16),
                   jax.ShapeDtypeStruct((n, 2 * A_KV_W), BF16),
                   jax.ShapeDtypeStruct((n, 2 * A_KV_W), BF16)]
        + [jax.ShapeDtypeStruct((n, B_W), F32)] * 6,
        scratch_shapes=[pltpu.VMEM((SUBLANES, B_QKV_W), F32)],
        compiler_params=_cparams("arbitrary"),
        name="in_proj0",
    )(x2d, nw.reshape(1, d), sc, sh, w, conv_w, sel, selt2, selt3, tril, pad8(a_log), pad8(dt_bias))
    return outs


_ATTN_BLOCKS = 2


def _attn_kernel(sink_ref, q_ref, kp_ref, kc_ref, vp_ref, vc_ref, bm_ref, o_ref, *, steps_per_seq):
    i = pl.program_id(0)
    first = (i % steps_per_seq) == 0
    w = WINDOW
    lane = lax.broadcasted_iota(jnp.int32, (w, LANES), 1)
    low = lane < HEAD_DIM
    col = lax.broadcasted_iota(jnp.int32, (2 * w, 2 * w), 1)
    row = lax.broadcasted_iota(jnp.int32, (2 * w, 1), 0)
    prev_dead = jnp.logical_and(first, col < w)
    zero = jnp.zeros((), q_ref.dtype)
    pairs = A_Q_HEADS // 2
    units = [(b, j) for b in range(_ATTN_BLOCKS) for j in range(pairs)]

    def keys(p_ref, c_ref, b, kh):
        ls = slice(kh * LANES, (kh + 1) * LANES)
        before = p_ref[:, ls] if b == 0 else c_ref[(b - 1) * w:b * w, ls]
        return jnp.concatenate([before, c_ref[b * w:(b + 1) * w, ls]], axis=0)

    kv_of = lambda j: (2 * j) // (A_Q_HEADS // A_KV_HEADS)
    qp = [q_ref[b * w:(b + 1) * w, j * LANES:(j + 1) * LANES] for b, j in units]
    qs = [jnp.concatenate([jnp.where(low, t, zero), jnp.where(low, zero, t)], axis=0) for t in qp]
    kd = [keys(kp_ref, kc_ref, b, kv_of(j)) for b, j in units]
    vd = [keys(vp_ref, vc_ref, b, kv_of(j)) for b, j in units]
    s = [lax.dot_general(a, k, (((1,), (1,)), ((), ())), preferred_element_type=F32) for a, k in zip(qs, kd)]
    s = [t * (HEAD_DIM ** -0.5) + bm_ref[j] for t, (b, j) in zip(s, units)]
    s = [jnp.where(prev_dead, NEG_BIG, t) if b == 0 else t for t, (b, j) in zip(s, units)]
    sink = [jnp.where(row < w, sink_ref[2 * j], sink_ref[2 * j + 1]) for b, j in units]
    m = [jnp.maximum(jnp.max(t, axis=-1, keepdims=True), sk) for t, sk in zip(s, sink)]
    p = [jnp.exp(t - mt) for t, mt in zip(s, m)]
    denom = [jnp.sum(t, axis=-1, keepdims=True) + jnp.exp(sk - mt) for t, sk, mt in zip(p, sink, m)]
    pv = [jnp.dot(t.astype(BF16), v, preferred_element_type=F32) / dn for t, v, dn in zip(p, vd, denom)]
    outs = [jnp.where(low, t[:w], t[w:]) for t in pv]
    for b in range(_ATTN_BLOCKS):
        o_ref[b * w:(b + 1) * w, :] = jnp.concatenate(outs[b * pairs:(b + 1) * pairs], axis=1).astype(o_ref.dtype)


def _attention(qa, kd, vd, bias_tbl, sinks, seq_len):
    n = qa.shape[0]
    w = WINDOW
    rows = _ATTN_BLOCKS * w
    steps = seq_len // rows
    cur = lambda i: (i, 0)
    prev = lambda i: (jnp.where(i % steps == 0, i * _ATTN_BLOCKS, i * _ATTN_BLOCKS - 1), 0)
    return pl.pallas_call(
        functools.partial(_attn_kernel, steps_per_seq=steps),
        grid=(n // rows,),
        in_specs=[pl.BlockSpec(memory_space=pltpu.SMEM),
                  pl.BlockSpec((rows, A_Q_W), cur),
                  pl.BlockSpec((w, 2 * A_KV_W), prev), pl.BlockSpec((rows, 2 * A_KV_W), cur),
                  pl.BlockSpec((w, 2 * A_KV_W), prev), pl.BlockSpec((rows, 2 * A_KV_W), cur),
                  _const_spec((A_Q_HEADS // 2, 2 * w, 2 * w))],
        out_specs=pl.BlockSpec((rows, A_Q_W), cur),
        out_shape=jax.ShapeDtypeStruct((n, A_Q_W), BF16),
        compiler_params=_cparams("parallel"),
        name="swa_attention",
    )(sinks, qa, kd, kd, vd, vd, bias_tbl)


_DN_PAIRS = B_HEADS // 2
_DN_INV_BLOCK = 16
_DN_GROUP = 4


def _block_diag(x, low):
    zero = jnp.zeros((), x.dtype)
    return jnp.concatenate([jnp.where(low, x, zero), jnp.where(low, zero, x)], axis=0)


def _dn_intra(chunks, data_refs, work_refs, consts):
    qn_ref, kn_ref, vb_ref, bexp_ref, gcexp_ref = data_refs
    u_ref, w_ref, qk_ref, qd_ref, kd_ref, egl_ref = work_refs
    low, i_idx, j_idx, ones3 = consts
    c = CHUNK
    units = [(ci, p) for ci in chunks for p in range(_DN_PAIRS)]
    where = [(slice(ci * c, (ci + 1) * c), slice(p * LANES, (p + 1) * LANES)) for ci, p in units]
    causal = i_idx >= j_idx
    strict = i_idx > j_idx
    on_diag = i_idx == j_idx
    eye = on_diag.astype(F32)
    blk_shift = int(math.log2(_DN_INV_BLOCK))
    same_blk = (i_idx >> blk_shift) == (j_idx >> blk_shift)

    q = [qn_ref[rs, ls] for rs, ls in where]
    k = [kn_ref[rs, ls] for rs, ls in where]
    v = [vb_ref[rs, ls] for rs, ls in where]
    b = [bexp_ref[rs, ls] for rs, ls in where]
    gc = [gcexp_ref[rs, ls] for rs, ls in where]

    gr = [jnp.dot(ones3, jnp.concatenate(_split(jnp.where(on_diag, t, 0.0), 3), axis=0),
                  preferred_element_type=F32) for t in gc]
    ks = [_block_diag(t.astype(BF16), low) for t in k]
    qkk = [lax.dot_general(jnp.concatenate([qt, kt], axis=0).astype(BF16), kst,
                           (((1,), (1,)), ((), ())), preferred_element_type=F32)
           for qt, kt, kst in zip(q, k, ks)]
    decay = [jnp.exp(jnp.where(causal, gct - grt, NEG_BIG)) for gct, grt in zip(gc, gr)]
    lmat = [jnp.where(strict, bt * t[c:] * dt, 0.0) for bt, t, dt in zip(b, qkk, decay)]
    qk = [jnp.where(causal, t[:c] * dt, 0.0) for t, dt in zip(qkk, decay)]

    def mm(xs, ys):
        return [_bdot(x, _block_diag(y.astype(BF16), low)) for x, y in zip(xs, ys)]

    l_diag = [jnp.where(same_blk, t, 0.0) for t in lmat]
    l_off = [t - d for t, d in zip(lmat, l_diag)]
    pw = [-t for t in l_diag]
    d_inv = [eye + t for t in pw]
    for _ in range(blk_shift - 1):
        pw = mm(pw, pw)
        d_inv = mm(d_inv, [eye + t for t in pw])
    pw = [-t for t in mm(d_inv, l_off)]
    acc = [eye + t for t in pw]
    for _ in range(int(math.log2(c // _DN_INV_BLOCK)) - 1):
        pw = mm(pw, pw)
        acc = mm(acc, [eye + t for t in pw])
    tmat = mm(acc, d_inv)

    egc = [jnp.exp(t) for t in gc]
    rhs = [jnp.concatenate([_block_diag((vt * bt).astype(BF16), low),
                            _block_diag((kt * (bt * et)).astype(BF16), low)], axis=1)
           for vt, kt, bt, et in zip(v, k, b, egc)]
    uw = [_bdot(t, r) for t, r in zip(tmat, rhs)]
    for n, (ci, p) in enumerate(units):
        g_last = gc[n][c - 1:c, :]
        u_ref[ci, p] = uw[n][:, :LANES]
        w_ref[ci, p] = uw[n][:, LANES:]
        qk_ref[ci, p] = qk[n]
        qd_ref[ci, p] = q[n] * egc[n]
        kd_ref[ci, p] = k[n] * jnp.exp(g_last - gc[n])
        egl_ref[ci, p] = jnp.broadcast_to(jnp.exp(g_last), (SUBLANES, LANES))


def _dn_scan(ci, work_refs, s_ref, gs_ref, nw, o_ref, consts):
    u_ref, w_ref, qk_ref, qd_ref, kd_ref, egl_ref = work_refs
    low, mask_bd, head_mean2 = consts
    c = CHUNK
    rows = slice(ci * c, (ci + 1) * c)
    pairs = range(_DN_PAIRS)
    s_old = [s_ref[p] for p in pairs]
    wq = [_bdot(jnp.concatenate([w_ref[ci, p], qd_ref[ci, p]], axis=0), s_old[p]) for p in pairs]
    v_new = [u_ref[ci, p] - wq[p][:c] for p in pairs]
    o = [wq[p][c:] + _bdot(qk_ref[ci, p], _block_diag(v_new[p].astype(BF16), low)) for p in pairs]
    kv = [_bdot_tn(kd_ref[ci, p], v_new[p]) for p in pairs]
    for p in pairs:
        s_ref[p] = s_old[p] * egl_ref[ci, p][0:1, :] + jnp.where(mask_bd, kv[p], 0.0)
    ms = [jnp.dot(jnp.concatenate(_split(t * t, 2), axis=1), head_mean2, preferred_element_type=F32)
          for t in o]
    for p in pairs:
        ls = slice(p * LANES, (p + 1) * LANES)
        y = (o[p] * lax.rsqrt(ms[p] + EPS)) * nw * gs_ref[rows, ls]
        o_ref[rows, ls] = y.astype(o_ref.dtype)


def _dn_kernel(qn_ref, kn_ref, vb_ref, gs_ref, bexp_ref, gcexp_ref, nw_ref, o_ref,
               s_ref, u_ref, w_ref, qk_ref, qd_ref, kd_ref, egl_ref, *, groups_per_seq):
    i = pl.program_id(0)

    @pl.when(i % groups_per_seq == 0)
    def _():
        s_ref[...] = jnp.zeros_like(s_ref)

    c = CHUNK
    tm = o_ref.shape[0]
    n_chunks = tm // c
    lane = lax.broadcasted_iota(jnp.int32, (c, LANES), 1)
    low = lane < HEAD_DIM
    i_idx = lax.broadcasted_iota(jnp.int32, (c, LANES), 0)
    j_idx = lane & (c - 1)
    ones3 = jnp.ones((c, 3 * c), BF16)
    rb = lax.broadcasted_iota(jnp.int32, (LANES, LANES), 0)
    cb = lax.broadcasted_iota(jnp.int32, (LANES, LANES), 1)
    mask_bd = (rb < HEAD_DIM) == (cb < HEAD_DIM)
    head_mean = jnp.where(mask_bd, 1.0 / HEAD_DIM, 0.0).astype(BF16)
    head_mean2 = jnp.concatenate([head_mean, head_mean], axis=0)
    data_refs = (qn_ref, kn_ref, vb_ref, bexp_ref, gcexp_ref)
    work_refs = (u_ref, w_ref, qk_ref, qd_ref, kd_ref, egl_ref)
    intra_consts = (low, i_idx, j_idx, ones3)
    scan_consts = (low, mask_bd, head_mean2)
    nw = nw_ref[...]

    groups = [list(range(s, s + _DN_GROUP)) for s in range(0, n_chunks, _DN_GROUP)]
    _dn_intra(groups[0], data_refs, work_refs, intra_consts)
    for j, grp in enumerate(groups):
        if j + 1 < len(groups):
            _dn_intra(groups[j + 1], data_refs, work_refs, intra_consts)
        for ci in grp:
            _dn_scan(ci, work_refs, s_ref, gs_ref, nw, o_ref, scan_consts)


def _deltanet(qn, kn, vb, gs, bexp, gcexp, norm_w, seq_len):
    n = qn.shape[0]
    tm = TOKEN_TILE
    nw2 = jnp.concatenate([norm_w, norm_w]).reshape(1, LANES)
    row = lambda width: pl.BlockSpec((tm, width), lambda i: (i, 0))
    return pl.pallas_call(
        functools.partial(_dn_kernel, groups_per_seq=seq_len // tm),
        grid=(n // tm,),
        in_specs=[row(B_W)] * 6 + [_const_spec((1, LANES))],
        out_specs=row(B_W),
        out_shape=jax.ShapeDtypeStruct((n, B_W), BF16),
        scratch_shapes=[pltpu.VMEM((_DN_PAIRS, LANES, LANES), F32)]
        + [pltpu.VMEM((tm // CHUNK, _DN_PAIRS, CHUNK, LANES), F32)] * 5
        + [pltpu.VMEM((tm // CHUNK, _DN_PAIRS, SUBLANES, LANES), F32)],
        compiler_params=_cparams("arbitrary"),
        name="gated_deltanet",
    )(qn, kn, vb, gs, bexp, gcexp, nw2)


def _resident_spec(shape):
    nd = len(shape)
    return pl.BlockSpec(shape, lambda *_: (0,) * nd, pipeline_mode=pl.Buffered(1))


def _mid0_kernel(attn_ref, dn_ref, x_ref, wo_ref, g1_ref, nw_ref, sc_ref, sh_ref, g2_ref,
                 wg_ref, wu_ref, wd_ref, o_ref):
    mix = (jnp.dot(attn_ref[...], wo_ref[:A_Q_W], preferred_element_type=F32)
           + jnp.dot(dn_ref[...], wo_ref[A_Q_W:], preferred_element_type=F32))
    x1 = x_ref[...] + g1_ref[0] * mix
    hn = _norm_mod(x1, nw_ref[...], sc_ref[0], sh_ref[0]).astype(BF16)
    hg = jnp.dot(hn, wg_ref[...], preferred_element_type=F32)
    hu = jnp.dot(hn, wu_ref[...], preferred_element_type=F32)
    act = (_silu(hg) * hu).astype(BF16)
    o_ref[...] = x1 + g2_ref[0] * jnp.dot(act, wd_ref[...], preferred_element_type=F32)


def _mid0(attn, dn, x2d, w_out, g1, nw, sc, sh, g2, wg, wu, wd, seq_len):
    n, d = x2d.shape
    tm = TOKEN_TILE
    tps = seq_len // tm
    row = lambda width: pl.BlockSpec((tm, width), lambda i: (i, 0))
    per_b = pl.BlockSpec((1, 1, d), lambda i: (i // tps, 0, 0))
    return pl.pallas_call(
        _mid0_kernel,
        grid=(n // tm,),
        in_specs=[row(A_Q_W), row(B_W), row(d), _resident_spec(w_out.shape), per_b,
                  _const_spec((1, d)), per_b, per_b, per_b,
                  _resident_spec(wg.shape), _resident_spec(wu.shape), _resident_spec(wd.shape)],
        out_specs=row(d),
        out_shape=jax.ShapeDtypeStruct((n, d), F32),
        compiler_params=_cparams("parallel"),
        name="out_proj0_swiglu",
    )(attn, dn, x2d, w_out.astype(BF16), g1, nw.reshape(1, d), sc, sh, g2,
      wg.astype(BF16), wu.astype(BF16), wd.astype(BF16))


def _gelu_tanh(x):
    return 0.5 * x * (1.0 + jnp.tanh(math.sqrt(2.0 / math.pi) * (x + 0.044715 * (x * x * x))))


def _linear_scan(a, b, h0):
    n, width = a.shape
    groups = n // SUBLANES
    a = a.reshape(groups, SUBLANES, width)
    b = b.reshape(groups, SUBLANES, width)
    in_group = lax.broadcasted_iota(jnp.int32, a.shape, 1)
    s = 1
    while s < SUBLANES:
        a_sh = pltpu.roll(a, s, 1)
        b_sh = pltpu.roll(b, s, 1)
        valid = in_group >= s
        b = jnp.where(valid, a * b_sh + b, b)
        a = jnp.where(valid, a * a_sh, a)
        s *= 2
    carry = jnp.broadcast_to(h0, (SUBLANES, width))
    out = []
    for g in range(groups):
        hg = a[g] * carry + b[g]
        out.append(hg)
        carry = jnp.broadcast_to(hg[SUBLANES - 1:SUBLANES, :], hg.shape)
    return jnp.concatenate(out, axis=0)


def _mix1_tile(x, nw, sc, sh, w_ref, cw_ref, cb_ref, ga_ref, gab_ref, gx_ref, gxb_ref, lam_ref, sw_ref,
               tail_c_ref, tail_d_ref, h_ref):
    hn = _norm_mod(x, nw, sc, sh, on_mxu=True).astype(BF16)
    proj = jnp.dot(hn, w_ref[...], preferred_element_type=F32)
    w_l = LRU_WIDTH
    xc_in = proj[:, :w_l]
    yc = proj[:, w_l:2 * w_l]
    bd = proj[:, 2 * w_l:2 * w_l + SC_WIDTH]
    cd = proj[:, 2 * w_l + SC_WIDTH:2 * w_l + 2 * SC_WIDTH]
    hd = proj[:, 2 * w_l + 2 * SC_WIDTH:]
    tm = xc_in.shape[0]

    kc = cw_ref.shape[0]
    tail = tail_c_ref[...]
    cw = cw_ref[...]
    xc = xc_in * cw[kc - 1:kc] + cb_ref[...]
    for k in range(1, kc):
        xc = xc + _shift_rows(xc_in, k, tail) * cw[kc - 1 - k:kc - k]
    tail_c_ref[...] = xc_in[tm - SUBLANES:]

    xb = xc.astype(BF16)
    gw = ga_ref.shape[1]
    ra, ri = [], []
    for p in range(ga_ref.shape[0]):
        xin = xb[:, p * gw:(p + 1) * gw]
        ra.append(jnp.dot(xin, ga_ref[p], preferred_element_type=F32))
        ri.append(jnp.dot(xin, gx_ref[p], preferred_element_type=F32))
    r = _sigmoid(jnp.concatenate(ra, axis=1) + gab_ref[...])
    ig = _sigmoid(jnp.concatenate(ri, axis=1) + gxb_ref[...])
    log_a = (-LRU_C) * r * _softplus(-lam_ref[...])
    a = jnp.exp(log_a)
    one_m_a2 = -jnp.tanh(log_a) * (a * a + 1.0)
    root = jnp.where(one_m_a2 > 0.0, one_m_a2 * lax.rsqrt(one_m_a2), 0.0)
    b = root * (ig * xc)
    h = _linear_scan(a, b, h_ref[0:1, :])
    h_ref[...] = jnp.broadcast_to(h[tm - 1:tm, :], h_ref.shape)
    yc_out = h * _gelu_tanh(yc)

    ks = sw_ref.shape[0]
    ch = cd * hd
    tail_d = tail_d_ref[...]
    sw = sw_ref[...]
    conv = ch * sw[ks - 1:ks]
    for k in range(1, ks):
        conv = conv + _shift_rows(ch, k, tail_d) * sw[ks - 1 - k:ks - k]
    tail_d_ref[...] = ch[tm - SUBLANES:]
    return jnp.concatenate([yc_out, bd * conv], axis=1)


def _pair_block_diag(gw):
    nb, bw, _ = gw.shape
    g2 = gw.reshape(nb // 2, 2, bw, bw)
    z = jnp.zeros((nb // 2, bw, bw), gw.dtype)
    top = jnp.concatenate([g2[:, 0], z], axis=2)
    bot = jnp.concatenate([z, g2[:, 1]], axis=2)
    return jnp.concatenate([top, bot], axis=1).astype(BF16)


def _route_tile(cat, x, wo_ref, g1, nw, sc, sh, rw_ref, rb_ref, carry_ref):
    x3 = x + g1 * jnp.dot(cat, wo_ref[...], preferred_element_type=F32)
    hn = _norm_mod(x3, nw, sc, sh)
    tm = hn.shape[0]
    lane = lax.broadcasted_iota(jnp.int32, (tm, LANES), 1)
    h_hi, h_lo = _split(hn, 2)
    both = jnp.dot(h_hi, rw_ref[...], preferred_element_type=F32)
    logits = (both[:, :LANES] + both[:, LANES:]
              + jnp.dot(h_lo, rw_ref[:, :LANES], preferred_element_type=F32) + rb_ref[...])
    lg = jnp.where(lane < N_EXPERTS, logits, NEG_BIG)
    m1 = jnp.max(lg, axis=1, keepdims=True)
    i1 = jnp.min(jnp.where(lg == m1, lane, LANES), axis=1, keepdims=True)
    lg2 = jnp.where(lane == i1, NEG_BIG, lg)
    m2 = jnp.max(lg2, axis=1, keepdims=True)
    i2 = jnp.min(jnp.where(lg2 == m2, lane, LANES), axis=1, keepdims=True)
    e2 = jnp.exp(m2 - m1)
    w1 = 1.0 / (1.0 + e2)
    w2 = e2 / (1.0 + e2)

    hit1 = lane == i1
    hit2 = lane == i2
    sel = jnp.logical_or(hit1, hit2).astype(F32)
    r_i = lax.broadcasted_iota(jnp.int32, (tm, tm), 0)
    c_i = lax.broadcasted_iota(jnp.int32, (tm, tm), 1)
    tril = (r_i >= c_i).astype(BF16)
    incl = jnp.dot(tril, sel.astype(BF16), preferred_element_type=F32)
    carry = carry_ref[0:1, :]
    excl = incl - sel + carry
    r1 = jnp.sum(jnp.where(hit1, excl, 0.0), axis=1, keepdims=True)
    r2 = jnp.sum(jnp.where(hit2, excl, 0.0), axis=1, keepdims=True)
    total = carry + incl[tm - 1:tm, :]
    carry_ref[...] = jnp.broadcast_to(total, carry_ref.shape)

    meta = jnp.where(lane == 0, i1, 0)
    meta = jnp.where(lane == 1, i2, meta)
    meta = jnp.where(lane == 2, r1.astype(jnp.int32), meta)
    meta = jnp.where(lane == 3, r2.astype(jnp.int32), meta)
    wt = jnp.where(lane == 0, w1, jnp.where(lane == 1, w2, 0.0))
    return x3, hn, meta, wt, carry, total


def _mix1_kernel(x_ref, nw_ref, sc_ref, sh_ref, w_ref, cw_ref, cb_ref, ga_ref, gab_ref, gx_ref, gxb_ref,
                 lam_ref, sw_ref, o_ref, tail_c_ref, tail_d_ref, h_ref, *, tiles_per_seq):
    i = pl.program_id(0)

    @pl.when(i % tiles_per_seq == 0)
    def _():
        tail_c_ref[...] = jnp.zeros_like(tail_c_ref)
        tail_d_ref[...] = jnp.zeros_like(tail_d_ref)
        h_ref[...] = jnp.zeros_like(h_ref)

    cat = _mix1_tile(x_ref[...], nw_ref[...], sc_ref[0], sh_ref[0], w_ref, cw_ref, cb_ref, ga_ref,
                     gab_ref, gx_ref, gxb_ref, lam_ref, sw_ref, tail_c_ref, tail_d_ref, h_ref)
    o_ref[...] = cat.astype(o_ref.dtype)


def _mix1(x2d, nw, sc, sh, w_in, conv_w, conv_b, ga_w, ga_b, gx_w, gx_b, lam, sconv_w, seq_len):
    n, d = x2d.shape
    tm = TOKEN_TILE
    tps = seq_len // tm
    cd_in = w_in.shape[1]
    cd_out = LRU_WIDTH + SC_WIDTH
    row = lambda width: pl.BlockSpec((tm, width), lambda i: (i, 0))
    per_b = pl.BlockSpec((1, 1, d), lambda i: (i // tps, 0, 0))
    ga = _pair_block_diag(ga_w)
    gx = _pair_block_diag(gx_w)
    vec = lambda v: v.reshape(1, -1)
    return pl.pallas_call(
        functools.partial(_mix1_kernel, tiles_per_seq=tps),
        grid=(n // tm,),
        in_specs=[row(d), _const_spec((1, d)), per_b, per_b, _resident_spec((d, cd_in)),
                  _const_spec(conv_w.shape), _const_spec((1, LRU_WIDTH)),
                  _const_spec(ga.shape), _const_spec((1, LRU_WIDTH)),
                  _const_spec(gx.shape), _const_spec((1, LRU_WIDTH)),
                  _const_spec((1, LRU_WIDTH)), _const_spec(sconv_w.shape)],
        out_specs=row(cd_out),
        out_shape=jax.ShapeDtypeStruct((n, cd_out), BF16),
        scratch_shapes=[pltpu.VMEM((SUBLANES, LRU_WIDTH), F32), pltpu.VMEM((SUBLANES, SC_WIDTH), F32),
                        pltpu.VMEM((SUBLANES, LRU_WIDTH), F32)],
        compiler_params=_cparams("arbitrary"),
        name="rglru_shortconv_mixer",
    )(x2d, vec(nw), sc, sh, w_in.astype(BF16), conv_w, vec(conv_b), ga, vec(ga_b), gx, vec(gx_b),
      vec(lam), sconv_w)


def _route_kernel(cat_ref, x_ref, wo_ref, g1_ref, nw_ref, sc_ref, sh_ref, rw_ref, rb_ref,
                  x3_ref, hn_ref, metat_ref, meta_ref, wt_ref, base_ref, cnt_ref, carry_ref):
    @pl.when(pl.program_id(0) == 0)
    def _():
        carry_ref[...] = jnp.zeros_like(carry_ref)

    x3, hn, meta, wt, before, total = _route_tile(cat_ref[...], x_ref[...], wo_ref, g1_ref[0], nw_ref[...],
                                                  sc_ref[0], sh_ref[0], rw_ref, rb_ref, carry_ref)
    x3_ref[...] = x3
    hn_ref[...] = hn
    meta_ref[...] = meta
    metat_ref[...] = jnp.transpose(meta.astype(F32))[:SUBLANES].astype(jnp.int32)
    wt_ref[...] = wt
    base_ref[0] = jnp.broadcast_to(before, base_ref.shape[1:]).astype(jnp.int32)
    cnt_ref[...] = jnp.broadcast_to(total, cnt_ref.shape).astype(jnp.int32)


def _route(cat, x2d, w_out, g1, nw, sc, sh, router_w, router_b, seq_len):
    n, d = x2d.shape
    tm = TOKEN_TILE
    tps = seq_len // tm
    row = lambda width: pl.BlockSpec((tm, width), lambda i: (i, 0))
    per_b = pl.BlockSpec((1, 1, d), lambda i: (i // tps, 0, 0))
    rw = jnp.zeros((d, LANES), F32).at[:, :N_EXPERTS].set(router_w)
    rw_hi = rw.astype(BF16)
    rw = jnp.concatenate([rw_hi, (rw - rw_hi.astype(F32)).astype(BF16)], axis=1)
    rb = jnp.zeros((1, LANES), F32).at[0, :N_EXPERTS].set(router_b)
    return pl.pallas_call(
        _route_kernel,
        grid=(n // tm,),
        in_specs=[row(cat.shape[1]), row(d), _resident_spec(w_out.shape), per_b, _const_spec((1, d)),
                  per_b, per_b, _const_spec((d, 2 * LANES)), _const_spec((1, LANES))],
        out_specs=[row(d), row(d), pl.BlockSpec((SUBLANES, tm), lambda i: (0, i)), row(LANES), row(LANES),
                   pl.BlockSpec((1, SUBLANES, LANES), lambda i: (i, 0, 0)), _const_spec((SUBLANES, LANES))],
        out_shape=[jax.ShapeDtypeStruct((n, d), F32), jax.ShapeDtypeStruct((n, d), F32),
                   jax.ShapeDtypeStruct((SUBLANES, n), jnp.int32), jax.ShapeDtypeStruct((n, LANES), jnp.int32),
                   jax.ShapeDtypeStruct((n, LANES), F32),
                   jax.ShapeDtypeStruct((n // tm, SUBLANES, LANES), jnp.int32),
                   jax.ShapeDtypeStruct((SUBLANES, LANES), jnp.int32)],
        scratch_shapes=[pltpu.VMEM((SUBLANES, LANES), F32)],
        compiler_params=_cparams("arbitrary"),
        name="out_proj1_router",
    )(cat, x2d, w_out.astype(BF16), g1, nw.reshape(1, d), sc, sh, rw, rb)


def _local_rows(tr):
    return 2 * tr + N_EXPERTS * SUBLANES


def _xs_rows(n):
    worst = 2 * n + (n // TOKEN_TILE) * N_EXPERTS * (SUBLANES - 1)
    return (-(-worst // MOE_TILE) + N_EXPERTS) * MOE_TILE


def _local_pos(e_k, r_k, delta_ref, tile):
    shift = jnp.zeros_like(r_k)
    for e in range(N_EXPERTS):
        shift = jnp.where(e_k == e, delta_ref[tile * N_EXPERTS + e], shift)
    return r_k + shift


_RUN_BLOCK = 64


def _for_each_group(tile, lstart_ref, run_ref, gstart_ref, fn):
    big_shift = int(math.log2(_RUN_BLOCK))
    small_shift = int(math.log2(SUBLANES))
    for e in range(N_EXPERTS):
        k = tile * N_EXPERTS + e
        l_start = lstart_ref[k]
        g_start = gstart_ref[k]
        n_big = run_ref[k] >> big_shift
        tail = n_big << big_shift

        def big(g, c, l_start=l_start, g_start=g_start):
            off = g * _RUN_BLOCK
            fn(pl.multiple_of(l_start + off, SUBLANES), pl.multiple_of(g_start + off, SUBLANES), _RUN_BLOCK)
            return c

        def small(g, c, l_start=l_start, g_start=g_start, tail=tail):
            off = tail + g * SUBLANES
            fn(pl.multiple_of(l_start + off, SUBLANES), pl.multiple_of(g_start + off, SUBLANES), SUBLANES)
            return c

        lax.fori_loop(0, n_big, big, 0)
        lax.fori_loop(0, (run_ref[k] - tail) >> small_shift, small, 0)


def _zero_fill_gaps(gap_ref, used_ref, xs_ref, zero_ref, sem):
    zero_ref[...] = jnp.zeros_like(zero_ref)
    zr = zero_ref.shape[0]
    per_tile = MOE_TILE // zr
    shift = int(math.log2(SUBLANES))

    def gap_copy(e, g):
        row = pl.multiple_of(gap_ref[e] + g * SUBLANES, SUBLANES)
        return pltpu.make_async_copy(zero_ref.at[pl.ds(0, SUBLANES)], xs_ref.at[pl.ds(row, SUBLANES)], sem)

    def tile_copy(k):
        row = pl.multiple_of(k * zr, zr)
        return pltpu.make_async_copy(zero_ref, xs_ref.at[pl.ds(row, zr)], sem)

    def both(op):
        for e in range(N_EXPERTS):
            lax.fori_loop(0, gap_ref[N_EXPERTS + e] >> shift, lambda g, c, e=e: (op(gap_copy(e, g)), c)[1], 0)
        lax.fori_loop(used_ref[0] * per_tile, (xs_ref.shape[0] // MOE_TILE) * per_tile,
                      lambda k, c: (op(tile_copy(k)), c)[1], 0)

    both(lambda cp: cp.start())
    both(lambda cp: cp.wait())


def _dispatch_kernel(delta_ref, lstart_ref, run_ref, gstart_ref, gap_ref, used_ref, hn_ref, meta_ref, xs_ref,
                     sbuf_ref, zero_ref, sem):
    j = pl.program_id(0)
    tr = hn_ref.shape[0]
    lrows = sbuf_ref.shape[1]
    slot = lax.rem(j, 2)

    @pl.when(j == 0)
    def _():
        _zero_fill_gaps(gap_ref, used_ref, xs_ref, zero_ref, sem.at[0])

    meta = meta_ref[...]
    lp1 = _local_pos(meta[0:1], meta[2:3], delta_ref, j)
    lp2 = _local_pos(meta[1:2], meta[3:4], delta_ref, j)
    r_idx = lax.broadcasted_iota(jnp.int32, (lrows, tr), 0)
    onehot = jnp.logical_or(r_idx == lp1, r_idx == lp2).astype(BF16)
    sbuf_ref[slot] = jnp.dot(onehot, hn_ref[...].astype(BF16), preferred_element_type=F32)

    def group_copy(buf, local_row, xs_row, rows):
        return pltpu.make_async_copy(sbuf_ref.at[buf, pl.ds(local_row, rows)],
                                     xs_ref.at[pl.ds(xs_row, rows)], sem.at[buf])

    def drain(tile, buf):
        _for_each_group(tile, lstart_ref, run_ref, gstart_ref,
                        lambda lr, xr, rows: group_copy(buf, lr, xr, rows).wait())

    _for_each_group(j, lstart_ref, run_ref, gstart_ref,
                    lambda lr, xr, rows: group_copy(slot, lr, xr, rows).start())

    @pl.when(j > 0)
    def _():
        drain(j - 1, 1 - slot)

    @pl.when(j == pl.num_programs(0) - 1)
    def _():
        drain(j, slot)


def _dispatch(hn, meta_t, tables, gaps, used_tiles):
    n, d = hn.shape
    tr = TOKEN_TILE
    lrows = _local_rows(tr)
    return pl.pallas_call(
        _dispatch_kernel,
        grid_spec=pltpu.PrefetchScalarGridSpec(
            num_scalar_prefetch=6,
            grid=(n // tr,),
            in_specs=[pl.BlockSpec((tr, d), lambda j, *_: (j, 0)),
                      pl.BlockSpec((SUBLANES, tr), lambda j, *_: (0, j))],
            out_specs=pl.BlockSpec(memory_space=pl.ANY),
            scratch_shapes=[pltpu.VMEM((2, lrows, d), F32), pltpu.VMEM((MOE_SUB, d), F32),
                            pltpu.SemaphoreType.DMA((2,))]),
        out_shape=jax.ShapeDtypeStruct((_xs_rows(n), d), F32),
        compiler_params=_cparams("arbitrary"),
        name="moe_dispatch",
    )(*tables, gaps, used_tiles, hn, meta_t)


def _moe_kernel(te_ref, hi_ref, x_ref, wg_ref, wu_ref, wd_ref, o_ref, xb_ref):
    w = pl.program_id(0)
    f = pl.program_id(1)
    tm = x_ref.shape[0]
    sub = MOE_SUB
    sub_shift = int(math.log2(sub))
    hi = hi_ref[w]

    def swiglu_part(xb, wg, wu, wd):
        hg = jnp.dot(xb, wg, preferred_element_type=F32)
        hu = jnp.dot(xb, wu, preferred_element_type=F32)
        act = (_silu(hg) * hu).astype(BF16)
        return jnp.dot(act, wd, preferred_element_type=F32)

    @pl.when(jnp.logical_and(f == 0, hi > 0))
    def _():
        row = lax.broadcasted_iota(jnp.int32, (tm, 1), 0)
        xb_ref[...] = jnp.where(row < hi, x_ref[...], 0.0).astype(BF16)

    @pl.when(hi == tm)
    def _():
        part = swiglu_part(xb_ref[...], wg_ref[0].astype(BF16), wu_ref[0].astype(BF16),
                           wd_ref[0].astype(BF16))

        @pl.when(f == 0)
        def _():
            o_ref[...] = part

        @pl.when(f != 0)
        def _():
            o_ref[...] += part

    @pl.when(hi < tm)
    def _():
        @pl.when(f == 0)
        def _():
            o_ref[...] = jnp.zeros_like(o_ref)

        def sub_block(s, carry):
            rows = pl.ds(pl.multiple_of(s * sub, sub), sub)
            o_ref[rows, :] += swiglu_part(xb_ref[rows, :], wg_ref[0].astype(BF16),
                                          wu_ref[0].astype(BF16), wd_ref[0].astype(BF16))
            return carry

        lax.fori_loop(0, (hi + sub - 1) >> sub_shift, sub_block, 0)


def _moe_ffn(xs, tile_expert, tile_rows, wg, wu, wd):
    rows, d = xs.shape
    tm = MOE_TILE
    tf = MOE_FF_TILE
    nf = wg.shape[2] // tf

    def f_idx(f, hi):
        v = (hi > 0).astype(jnp.int32)
        return f * v + (nf - 1) * (1 - v)

    return pl.pallas_call(
        _moe_kernel,
        grid_spec=pltpu.PrefetchScalarGridSpec(
            num_scalar_prefetch=2,
            grid=(rows // tm, nf),
            in_specs=[pl.BlockSpec((tm, d), lambda w, f, te, hi: (jnp.where(hi[w] > 0, w, 0), 0)),
                      pl.BlockSpec((1, d, tf), lambda w, f, te, hi: (te[w], 0, f_idx(f, hi[w]))),
                      pl.BlockSpec((1, d, tf), lambda w, f, te, hi: (te[w], 0, f_idx(f, hi[w]))),
                      pl.BlockSpec((1, tf, d), lambda w, f, te, hi: (te[w], f_idx(f, hi[w]), 0))],
            out_specs=pl.BlockSpec((tm, d), lambda w, f, te, hi: (w, 0)),
            scratch_shapes=[pltpu.VMEM((tm, d), BF16)]),
        out_shape=jax.ShapeDtypeStruct((rows, d), F32),
        compiler_params=_cparams("arbitrary", "arbitrary"),
        name="moe_expert_swiglu",
    )(tile_expert, tile_rows, xs, wg, wu, wd)


def _combine_kernel(delta_ref, lstart_ref, run_ref, gstart_ref, ys_ref, x_ref, meta_ref, wt_ref, g2_ref, fw_ref,
                    o_ref, ybuf_ref, sem):
    j = pl.program_id(0)
    n_tiles = pl.num_programs(0)
    tr = x_ref.shape[0]
    lrows = ybuf_ref.shape[1]
    slot = lax.rem(j, 2)

    def group_copy(buf, local_row, xs_row, rows):
        return pltpu.make_async_copy(ys_ref.at[pl.ds(xs_row, rows)],
                                     ybuf_ref.at[buf, pl.ds(local_row, rows)], sem.at[buf])

    def fetch(tile, buf):
        ybuf_ref[buf, 2 * tr:, :] = jnp.zeros((lrows - 2 * tr, ybuf_ref.shape[2]), F32)
        _for_each_group(tile, lstart_ref, run_ref, gstart_ref,
                        lambda lr, xr, rows: group_copy(buf, lr, xr, rows).start())

    @pl.when(j == 0)
    def _():
        fetch(0, 0)

    _for_each_group(j, lstart_ref, run_ref, gstart_ref,
                    lambda lr, xr, rows: group_copy(slot, lr, xr, rows).wait())

    @pl.when(j + 1 < n_tiles)
    def _():
        fetch(j + 1, 1 - slot)

    meta = meta_ref[...]
    wt = wt_ref[...]
    lp1 = _local_pos(meta[:, 0:1], meta[:, 2:3], delta_ref, j)
    lp2 = _local_pos(meta[:, 1:2], meta[:, 3:4], delta_ref, j)
    l_idx = lax.broadcasted_iota(jnp.int32, (tr, lrows), 1)
    pick = jnp.where(l_idx == lp1, wt[:, 0:1], 0.0) + jnp.where(l_idx == lp2, wt[:, 1:2], 0.0)
    ffn = _bdot(pick, ybuf_ref[slot])
    x4 = x_ref[...] + g2_ref[0] * ffn
    o_ref[...] = (x4 * _rms_scale(x4)) * fw_ref[...]


def _combine(ys, tables, x3, meta, wt, g2, final_w, seq_len):
    n, d = x3.shape
    tr = TOKEN_TILE
    tps = seq_len // tr
    lrows = -(-_local_rows(tr) // LANES) * LANES
    return pl.pallas_call(
        _combine_kernel,
        grid_spec=pltpu.PrefetchScalarGridSpec(
            num_scalar_prefetch=4,
            grid=(n // tr,),
            in_specs=[pl.BlockSpec(memory_space=pl.ANY),
                      pl.BlockSpec((tr, d), lambda j, *_: (j, 0)),
                      pl.BlockSpec((tr, LANES), lambda j, *_: (j, 0)),
                      pl.BlockSpec((tr, LANES), lambda j, *_: (j, 0)),
                      pl.BlockSpec((1, 1, d), lambda j, *_: (j // tps, 0, 0)),
                      pl.BlockSpec((1, d), lambda j, *_: (0, 0))],
            out_specs=pl.BlockSpec((tr, d), lambda j, *_: (j, 0)),
            scratch_shapes=[pltpu.VMEM((2, lrows, d), F32), pltpu.SemaphoreType.DMA((2,))]),
        out_shape=jax.ShapeDtypeStruct((n, d), F32),
        compiler_params=_cparams("arbitrary"),
        name="moe_combine_final_norm",
    )(*tables, ys, x3, meta, wt, g2, final_w.reshape(1, d))


def _moe_tables(tile_base, counts, n_tokens):
    i32 = lambda t: t.astype(jnp.int32)
    tm = MOE_TILE
    before = tile_base[:, 0, :N_EXPERTS]
    total = counts[0, :N_EXPERTS]
    run = jnp.concatenate([before[1:], total[None]], axis=0) - before
    run = (run + SUBLANES - 1) // SUBLANES * SUBLANES
    l_end = jnp.cumsum(run, axis=1)
    l_start = l_end - run
    g_size = jnp.sum(run, axis=0)
    g_tiles = (g_size + tm - 1) // tm
    tile_end = jnp.cumsum(g_tiles)
    g_off = (tile_end - g_tiles) * tm
    g_end = g_off + g_size
    g_start = g_off[None, :] + jnp.cumsum(run, axis=0) - run
    delta = l_start - before
    gaps = jnp.concatenate([g_end, tile_end * tm - g_end])
    used_tiles = tile_end[-1:]

    w = jnp.arange(_xs_rows(n_tokens) // tm, dtype=jnp.int32)
    te = jnp.minimum(jnp.sum((tile_end[None, :] <= w[:, None]).astype(jnp.int32), axis=1), N_EXPERTS - 1)
    rows = jnp.where(w < tile_end[-1], jnp.clip(jnp.take(g_end, te) - w * tm, 0, tm), 0)
    flat = lambda t: i32(t).reshape(-1)
    return (flat(delta), flat(l_start), flat(run), flat(g_start)), i32(gaps), i32(used_tiles), i32(te), i32(rows)


def kernel(x, c, rel_bias, ada_w, ada_b, norm_mix_w, norm_ffn_w, final_norm_w, ab_w_in, attn_sinks,
           dn_conv_w, dn_a_log, dn_dt_bias, dn_norm_w, ab_w_out, ffn_w_gate, ffn_w_up, ffn_w_down,
           cd_w_in, lru_conv_w, lru_conv_b, lru_gate_a_w, lru_gate_a_b, lru_gate_x_w, lru_gate_x_b,
           lru_lambda, sconv_w, cd_w_out, moe_router_w, moe_router_b, moe_w_gate, moe_w_up, moe_w_down):
    bsz, seq_len, d = x.shape
    n = bsz * seq_len
    x2d = x.reshape(n, d)
    mods = _ada_mods(c, ada_w, ada_b)

    sh1, sc1, g1, sh2, sc2, g2 = (mods[0, k] for k in range(6))
    qa, kd, vd, qn, kn, vb, gs, bexp, gcexp = _in_proj0(
        x2d, norm_mix_w[0], sc1, sh1, ab_w_in[0], dn_conv_w[0], dn_a_log[0], dn_dt_bias[0], seq_len)
    attn = _attention(qa, kd, vd, _bias_table(rel_bias), attn_sinks[0], seq_len)
    dn = _deltanet(qn, kn, vb, gs, bexp, gcexp, dn_norm_w[0], seq_len)
    x2 = _mid0(attn, dn, x2d, ab_w_out[0], g1, norm_ffn_w[0], sc2, sh2, g2,
               ffn_w_gate[0], ffn_w_up[0], ffn_w_down[0], seq_len)

    sh1, sc1, g1, sh2, sc2, g2 = (mods[1, k] for k in range(6))
    cat = _mix1(x2, norm_mix_w[1], sc1, sh1, cd_w_in[0], lru_conv_w[0], lru_conv_b[0],
                lru_gate_a_w[0], lru_gate_a_b[0], lru_gate_x_w[0], lru_gate_x_b[0],
                lru_lambda[0], sconv_w[0], seq_len)
    x3, hn4, meta_t, meta, wt, tile_base, counts = _route(
        cat, x2, cd_w_out[0], g1, norm_ffn_w[1], sc2, sh2, moe_router_w[0], moe_router_b[0], seq_len)
    tables, gaps, used_tiles, tile_expert, tile_rows = _moe_tables(tile_base, counts, n)
    xs = _dispatch(hn4, meta_t, tables, gaps, used_tiles)
    ys = _moe_ffn(xs, tile_expert, tile_rows, moe_w_gate[0], moe_w_up[0], moe_w_down[0])
    out = _combine(ys, tables, x3, meta, wt, g2, final_norm_w, seq_len)
    return out.reshape(bsz, seq_len, d)
```

```python
import functools
import math

import numpy as np
import jax
import jax.numpy as jnp
from jax import lax
from jax.experimental import pallas as pl
from jax.experimental.pallas import tpu as pltpu

D_MODEL = 1024
EPS = 1e-6
HEAD_DIM = 64
A_Q_HEADS = 8
A_KV_HEADS = 2
WINDOW = 128
N_BUCKETS = 32
MAX_DISTANCE = 128
B_HEADS = 8
B_CONV = 4
CHUNK = 64
A_Q_W = A_Q_HEADS * HEAD_DIM
A_KV_W = A_KV_HEADS * HEAD_DIM
B_W = B_HEADS * HEAD_DIM
B_QKV_W = 3 * B_W
LRU_WIDTH = D_MODEL
LRU_BLOCKS = 8
LRU_C = 8.0
SC_WIDTH = D_MODEL // 2
D_FF = 2816
N_EXPERTS = 8
D_FF_EXPERT = 3584

LANES = 128
SUBLANES = 8
VMEM_LIMIT_BYTES = 56 * 1024 * 1024
TOKEN_TILE = 512
MOE_TILE = 1024
MOE_SUB = 256
MOE_FF_TILE = 512
NEG_BIG = -1e30

F32 = jnp.float32
BF16 = jnp.bfloat16


def _cparams(*sem):
    return pltpu.CompilerParams(dimension_semantics=tuple(sem), vmem_limit_bytes=VMEM_LIMIT_BYTES)


def _const_spec(shape):
    nd = len(shape)
    return pl.BlockSpec(shape, lambda *_: (0,) * nd)


def _bdot(a, b):
    return jnp.dot(a.astype(BF16), b.astype(BF16), preferred_element_type=F32)


def _bdot_nt(a, b):
    return lax.dot_general(a.astype(BF16), b.astype(BF16), (((1,), (1,)), ((), ())),
                           preferred_element_type=F32)


def _bdot_tn(a, b):
    return lax.dot_general(a.astype(BF16), b.astype(BF16), (((0,), (0,)), ((), ())),
                           preferred_element_type=F32)


def _split(x, n):
    parts = []
    r = x
    for i in range(n):
        p = r.astype(BF16)
        parts.append(p)
        if i + 1 < n:
            r = r - p.astype(F32)
    return parts


def _dot_x(a, b, na=2, nb=2):
    asp = _split(a, na) if na > 1 else [a.astype(BF16)]
    bsp = _split(b, nb) if nb > 1 else [b.astype(BF16)]
    acc = None
    for i, ai in enumerate(asp):
        for j, bj in enumerate(bsp):
            if i + j >= max(na, nb):
                continue
            t = jnp.dot(ai, bj, preferred_element_type=F32)
            acc = t if acc is None else acc + t
    return acc


def _dot_terms(a, b_stacked, n):
    return jnp.dot(jnp.concatenate(_split(a, n), axis=1), b_stacked, preferred_element_type=F32)


def _silu(x):
    return x * (1.0 / (1.0 + jnp.exp(-x)))


def _sigmoid(x):
    return 1.0 / (1.0 + jnp.exp(-x))


def _log1p(z):
    u = 1.0 + z
    tiny = u == 1.0
    return jnp.where(tiny, z, jnp.log(u) * (z / jnp.where(tiny, 1.0, u - 1.0)))


def _softplus(x):
    return jnp.maximum(x, 0.0) + _log1p(jnp.exp(-jnp.abs(x)))


def _rms_scale(x):
    width = x.shape[1]
    mean_w = jnp.full((width, LANES), 1.0 / width, BF16)
    ms = _dot_x(x * x, mean_w, 2, 1)
    r = lax.rsqrt(ms + EPS)
    return jnp.concatenate([r] * (width // LANES), axis=1)


def _norm_mod(x, w, sc, sh, on_mxu=False):
    if on_mxu:
        scale = _rms_scale(x)
    else:
        scale = lax.rsqrt(jnp.mean(x * x, axis=-1, keepdims=True) + EPS)
    return (x * scale) * w * (1.0 + sc) + sh


def _shift_rows(x, k, prev_tail):
    n, width = x.shape
    x3 = x.reshape(n // SUBLANES, SUBLANES, width)
    rot = pltpu.roll(x3, k, 1)
    rot_prev = jnp.concatenate([pltpu.roll(prev_tail, k, 0)[None], rot[:-1]], axis=0)
    sub = lax.broadcasted_iota(jnp.int32, x3.shape, 1)
    return jnp.where(sub >= k, rot, rot_prev).reshape(n, width)


def _ada_kernel(c_ref, w_ref, b_ref, o_ref):
    c = c_ref[...]
    cond = _silu(c)
    o_ref[0] = _dot_x(cond, w_ref[0], 3, 2) + b_ref[0]


def _ada_mods(c, ada_w, ada_b):
    depth, d, six_d = ada_w.shape
    bsz = c.shape[0]
    rows = max(SUBLANES, bsz)
    c_pad = jnp.zeros((rows, d), F32).at[:bsz].set(c)
    tn = 1536
    out = pl.pallas_call(
        _ada_kernel,
        grid=(depth, six_d // tn),
        in_specs=[pl.BlockSpec((rows, d), lambda l, j: (0, 0)),
                  pl.BlockSpec((1, d, tn), lambda l, j: (l, 0, j)),
                  pl.BlockSpec((1, 1, tn), lambda l, j: (l, 0, j))],
        out_specs=pl.BlockSpec((1, rows, tn), lambda l, j: (l, 0, j)),
        out_shape=jax.ShapeDtypeStruct((depth, rows, six_d), F32),
        compiler_params=_cparams("parallel", "parallel"),
        name="ada_mods",
    )(c_pad, ada_w, ada_b.reshape(depth, 1, six_d))
    return out[:, :bsz].reshape(depth, bsz, 6, 1, d).transpose(0, 2, 1, 3, 4)


def _t5_bucket(dist):
    max_exact = N_BUCKETS // 2
    d = np.maximum(dist, 0)
    large = max_exact + (np.log(np.maximum(d, 1) / max_exact) / math.log(MAX_DISTANCE / max_exact)
                         * (N_BUCKETS - max_exact)).astype(np.int32)
    large = np.minimum(large, N_BUCKETS - 1)
    return np.where(d < max_exact, d, large).astype(np.int32)


def _band_buckets():
    qi = np.arange(WINDOW)[:, None]
    s = np.arange(2 * WINDOW)[None, :]
    dist = qi + WINDOW - s
    in_window = (dist >= 0) & (dist < WINDOW)
    return np.where(in_window, _t5_bucket(dist), -1).astype(np.int32)


def _bias_kernel(rb_ref, bucket_ref, o_ref):
    h = pl.program_id(0)
    bucket = bucket_ref[...]
    acc = jnp.zeros(bucket.shape, F32)
    for b in range(N_BUCKETS):
        acc = jnp.where(bucket == b, rb_ref[b, h], acc)
    o_ref[0] = jnp.where(bucket < 0, NEG_BIG, acc)


def _bias_table(rel_bias):
    bucket = jnp.asarray(_band_buckets())
    out = pl.pallas_call(
        _bias_kernel,
        grid=(A_Q_HEADS,),
        in_specs=[pl.BlockSpec(memory_space=pltpu.SMEM),
                  _const_spec((WINDOW, 2 * WINDOW))],
        out_specs=pl.BlockSpec((1, WINDOW, 2 * WINDOW), lambda h: (h, 0, 0)),
        out_shape=jax.ShapeDtypeStruct((A_Q_HEADS, WINDOW, 2 * WINDOW), F32),
        compiler_params=_cparams("parallel"),
        name="attn_bias_table",
    )(rel_bias, bucket)
    return out.reshape(A_Q_HEADS // 2, 2 * WINDOW, 2 * WINDOW)


_C_QA = 0
_C_KA = _C_QA + A_Q_W
_C_VA = _C_KA + A_KV_W
_C_QKV = _C_VA + A_KV_W
_C_GATE = _C_QKV + B_QKV_W
_C_SMALL = _C_GATE + B_W
_AB_COLS = _C_SMALL + LANES


def _ab_in_weight(w_in):
    return jnp.pad(w_in, ((0, 0), (0, _AB_COLS - w_in.shape[1]))).astype(BF16)


def _dup_heads(t, low):
    swapped = pltpu.roll(t, HEAD_DIM, 1)
    return jnp.concatenate([jnp.where(low, t, swapped), jnp.where(low, swapped, t)], axis=1)


def _chunk_tril(tm):
    r = np.arange(tm)
    return ((r[:, None] >= r[None, :]) & (r[:, None] // CHUNK == r[None, :] // CHUNK)).astype(np.float32)


def _head_selector():
    e = np.zeros((B_W, LANES), np.float32)
    for h in range(B_HEADS):
        e[h * HEAD_DIM:(h + 1) * HEAD_DIM, h] = 1.0
    return e


def _in0_kernel(x_ref, nw_ref, sc_ref, sh_ref, w_ref, cw_ref, sel_ref, selt2_ref, selt3_ref, tril_ref,
                alog_ref, dtb_ref,
                qa_ref, kd_ref, vd_ref, qn_ref, kn_ref, vb_ref, gs_ref, bexp_ref, gcexp_ref,
                tail_ref, *, tiles_per_seq):
    i = pl.program_id(0)

    @pl.when(i % tiles_per_seq == 0)
    def _():
        tail_ref[...] = jnp.zeros_like(tail_ref)

    hn = _norm_mod(x_ref[...], nw_ref[...], sc_ref[0], sh_ref[0])
    proj = jnp.dot(hn.astype(BF16), w_ref[...], preferred_element_type=F32)
    tm = proj.shape[0]
    low = lax.broadcasted_iota(jnp.int32, (tm, LANES), 1) < HEAD_DIM

    small = proj[:, _C_SMALL:]
    lane = lax.broadcasted_iota(jnp.int32, small.shape, 1)
    beta = jnp.where(lane < B_HEADS, _sigmoid(small), 0.0)
    dec = pltpu.roll(small, LANES - B_HEADS, 1)
    g = jnp.where(lane < B_HEADS, -jnp.exp(alog_ref[...]) * _softplus(dec + dtb_ref[...]), 0.0)
    bexp_ref[...] = _dot_terms(beta, selt2_ref[...], 2)
    gc = _dot_x(tril_ref[...], g, 1, 3)
    gcexp_ref[...] = _dot_terms(gc, selt3_ref[...], 3)

    qa_ref[...] = proj[:, _C_QA:_C_KA].astype(BF16)
    kd_ref[...] = _dup_heads(proj[:, _C_KA:_C_VA], low).astype(BF16)
    vd_ref[...] = _dup_heads(proj[:, _C_VA:_C_QKV], low).astype(BF16)

    def conv_silu(block):
        cols = slice(block * B_W, (block + 1) * B_W)
        xq = proj[:, _C_QKV + block * B_W:_C_QKV + (block + 1) * B_W]
        tail = tail_ref[:, cols]
        cw = cw_ref[:, cols]
        y = xq * cw[B_CONV - 1:B_CONV]
        for k in range(1, B_CONV):
            y = y + _shift_rows(xq, k, tail) * cw[B_CONV - 1 - k:B_CONV - k]
        tail_ref[:, cols] = xq[tm - SUBLANES:]
        return _silu(y)

    q = conv_silu(0)
    k_ = conv_silu(1)
    ssq = _dot_x(jnp.concatenate([q * q, k_ * k_], axis=1), sel_ref[...], 2, 1)
    r = lax.rsqrt(ssq + EPS)
    q_scale = _dot_terms(r[:, :LANES], selt2_ref[...], 2)
    k_scale = _dot_terms(r[:, LANES:], selt2_ref[...], 2)
    vb_ref[...] = conv_silu(2)
    gs_ref[...] = _silu(proj[:, _C_GATE:_C_SMALL])
    qn_ref[...] = q * q_scale * (HEAD_DIM ** -0.5)
    kn_ref[...] = k_ * k_scale


def _in_proj0(x2d, nw, sc, sh, w_in, conv_w, a_log, dt_bias, seq_len):
    n, d = x2d.shape
    tm = TOKEN_TILE
    tiles_per_seq = seq_len // tm
    w = _ab_in_weight(w_in)
    hs = _head_selector()
    zeros = np.zeros_like(hs)
    sel = jnp.asarray(np.block([[hs, zeros], [zeros, hs]]), BF16)
    selt2 = jnp.asarray(np.tile(hs.T, (2, 1)), BF16)
    selt3 = jnp.asarray(np.tile(hs.T, (3, 1)), BF16)
    tril = jnp.asarray(_chunk_tril(tm), BF16)
    pad8 = lambda v: jnp.zeros((1, LANES), F32).at[0, :B_HEADS].set(v)
    row = lambda width: pl.BlockSpec((tm, width), lambda i: (i, 0))
    per_b = pl.BlockSpec((1, 1, d), lambda i: (i // tiles_per_seq, 0, 0))
    outs = pl.pallas_call(
        functools.partial(_in0_kernel, tiles_per_seq=tiles_per_seq),
        grid=(n // tm,),
        in_specs=[row(d), _const_spec((1, d)), per_b, per_b,
                  _resident_spec((d, _AB_COLS)), _const_spec((B_CONV, B_QKV_W)),
                  _const_spec(sel.shape), _const_spec(selt2.shape), _const_spec(selt3.shape),
                  _const_spec((tm, tm)), _const_spec((1, LANES)), _const_spec((1, LANES))],
        out_specs=[row(A_Q_W), row(2 * A_KV_W), row(2 * A_KV_W)] + [row(B_W)] * 6,
        out_shape=[jax.ShapeDtypeStruct((n, A_Q_W), BF16),
                   jax.ShapeDtypeStruct((n, 2 * A_KV_W), BF16),
                   jax.ShapeDtypeStruct((n, 2 * A_KV_W), BF16)]
        + [jax.ShapeDtypeStruct((n, B_W), F32)] * 6,
        scratch_shapes=[pltpu.VMEM((SUBLANES, B_QKV_W), F32)],
        compiler_params=_cparams("arbitrary"),
        name="in_proj0",
    )(x2d, nw.reshape(1, d), sc, sh, w, conv_w, sel, selt2, selt3, tril, pad8(a_log), pad8(dt_bias))
    return outs


_ATTN_BLOCKS = 2


def _attn_kernel(sink_ref, q_ref, kc_ref, vc_ref, bm_ref, o_ref, kp_ref, vp_ref, *, steps_per_seq):
    i = pl.program_id(0)
    first = (i % steps_per_seq) == 0
    w = WINDOW

    @pl.when(i == 0)
    def _():
        kp_ref[...] = jnp.zeros_like(kp_ref)
        vp_ref[...] = jnp.zeros_like(vp_ref)

    lane = lax.broadcasted_iota(jnp.int32, (w, LANES), 1)
    low = lane < HEAD_DIM
    col = lax.broadcasted_iota(jnp.int32, (2 * w, 2 * w), 1)
    row = lax.broadcasted_iota(jnp.int32, (2 * w, 1), 0)
    prev_dead = jnp.logical_and(first, col < w)
    zero = jnp.zeros((), q_ref.dtype)
    pairs = A_Q_HEADS // 2
    units = [(b, j) for b in range(_ATTN_BLOCKS) for j in range(pairs)]

    def keys(p_ref, c_ref, b, kh):
        ls = slice(kh * LANES, (kh + 1) * LANES)
        before = p_ref[:, ls] if b == 0 else c_ref[(b - 1) * w:b * w, ls]
        return jnp.concatenate([before, c_ref[b * w:(b + 1) * w, ls]], axis=0)

    kv_of = lambda j: (2 * j) // (A_Q_HEADS // A_KV_HEADS)
    qp = [q_ref[b * w:(b + 1) * w, j * LANES:(j + 1) * LANES] for b, j in units]
    qs = [jnp.concatenate([jnp.where(low, t, zero), jnp.where(low, zero, t)], axis=0) for t in qp]
    kd = [keys(kp_ref, kc_ref, b, kv_of(j)) for b, j in units]
    vd = [keys(vp_ref, vc_ref, b, kv_of(j)) for b, j in units]
    s = [lax.dot_general(a, k, (((1,), (1,)), ((), ())), preferred_element_type=F32) for a, k in zip(qs, kd)]
    s = [t * (HEAD_DIM ** -0.5) + bm_ref[j] for t, (b, j) in zip(s, units)]
    s = [jnp.where(prev_dead, NEG_BIG, t) if b == 0 else t for t, (b, j) in zip(s, units)]
    sink = [jnp.where(row < w, sink_ref[2 * j], sink_ref[2 * j + 1]) for b, j in units]
    m = [jnp.maximum(jnp.max(t, axis=-1, keepdims=True), sk) for t, sk in zip(s, sink)]
    p = [jnp.exp(t - mt) for t, mt in zip(s, m)]
    denom = [jnp.sum(t, axis=-1, keepdims=True) + jnp.exp(sk - mt) for t, sk, mt in zip(p, sink, m)]
    pv = [jnp.dot(t.astype(BF16), v, preferred_element_type=F32) / dn for t, v, dn in zip(p, vd, denom)]
    outs = [jnp.where(low, t[:w], t[w:]) for t in pv]
    for b in range(_ATTN_BLOCKS):
        o_ref[b * w:(b + 1) * w, :] = jnp.concatenate(outs[b * pairs:(b + 1) * pairs], axis=1).astype(o_ref.dtype)
    kp_ref[...] = kc_ref[(_ATTN_BLOCKS - 1) * w:, :]
    vp_ref[...] = vc_ref[(_ATTN_BLOCKS - 1) * w:, :]


def _attention(qa, kd, vd, bias_tbl, sinks, seq_len):
    n = qa.shape[0]
    w = WINDOW
    rows = _ATTN_BLOCKS * w
    steps = seq_len // rows
    cur = lambda i: (i, 0)
    return pl.pallas_call(
        functools.partial(_attn_kernel, steps_per_seq=steps),
        grid=(n // rows,),
        in_specs=[pl.BlockSpec(memory_space=pltpu.SMEM),
                  pl.BlockSpec((rows, A_Q_W), cur),
                  pl.BlockSpec((rows, 2 * A_KV_W), cur), pl.BlockSpec((rows, 2 * A_KV_W), cur),
                  _const_spec((A_Q_HEADS // 2, 2 * w, 2 * w))],
        out_specs=pl.BlockSpec((rows, A_Q_W), cur),
        out_shape=jax.ShapeDtypeStruct((n, A_Q_W), BF16),
        scratch_shapes=[pltpu.VMEM((w, 2 * A_KV_W), BF16), pltpu.VMEM((w, 2 * A_KV_W), BF16)],
        compiler_params=_cparams("arbitrary"),
        name="swa_attention",
    )(sinks, qa, kd, vd, bias_tbl)


_DN_PAIRS = B_HEADS // 2
_DN_INV_BLOCK = 16
_DN_GROUP = 4


def _block_diag(x, low):
    zero = jnp.zeros((), x.dtype)
    return jnp.concatenate([jnp.where(low, x, zero), jnp.where(low, zero, x)], axis=0)


def _dn_intra(chunks, data_refs, work_refs, consts):
    qn_ref, kn_ref, vb_ref, bexp_ref, gcexp_ref = data_refs
    u_ref, w_ref, qk_ref, qd_ref, kd_ref, egl_ref = work_refs
    low, i_idx, j_idx, ones3 = consts
    c = CHUNK
    units = [(ci, p) for ci in chunks for p in range(_DN_PAIRS)]
    where = [(slice(ci * c, (ci + 1) * c), slice(p * LANES, (p + 1) * LANES)) for ci, p in units]
    causal = i_idx >= j_idx
    strict = i_idx > j_idx
    on_diag = i_idx == j_idx
    eye = on_diag.astype(F32)
    blk_shift = int(math.log2(_DN_INV_BLOCK))
    same_blk = (i_idx >> blk_shift) == (j_idx >> blk_shift)

    q = [qn_ref[rs, ls] for rs, ls in where]
    k = [kn_ref[rs, ls] for rs, ls in where]
    v = [vb_ref[rs, ls] for rs, ls in where]
    b = [bexp_ref[rs, ls] for rs, ls in where]
    gc = [gcexp_ref[rs, ls] for rs, ls in where]

    gr = [jnp.dot(ones3, jnp.concatenate(_split(jnp.where(on_diag, t, 0.0), 3), axis=0),
                  preferred_element_type=F32) for t in gc]
    ks = [_block_diag(t.astype(BF16), low) for t in k]
    qkk = [lax.dot_general(jnp.concatenate([qt, kt], axis=0).astype(BF16), kst,
                           (((1,), (1,)), ((), ())), preferred_element_type=F32)
           for qt, kt, kst in zip(q, k, ks)]
    decay = [jnp.exp(jnp.where(causal, gct - grt, NEG_BIG)) for gct, grt in zip(gc, gr)]
    lmat = [jnp.where(strict, bt * t[c:] * dt, 0.0) for bt, t, dt in zip(b, qkk, decay)]
    qk = [jnp.where(causal, t[:c] * dt, 0.0) for t, dt in zip(qkk, decay)]

    def mm(xs, ys):
        return [_bdot(x, _block_diag(y.astype(BF16), low)) for x, y in zip(xs, ys)]

    l_diag = [jnp.where(same_blk, t, 0.0) for t in lmat]
    l_off = [t - d for t, d in zip(lmat, l_diag)]
    pw = [-t for t in l_diag]
    d_inv = [eye + t for t in pw]
    for _ in range(blk_shift - 1):
        pw = mm(pw, pw)
        d_inv = mm(d_inv, [eye + t for t in pw])
    pw = [-t for t in mm(d_inv, l_off)]
    acc = [eye + t for t in pw]
    for _ in range(int(math.log2(c // _DN_INV_BLOCK)) - 1):
        pw = mm(pw, pw)
        acc = mm(acc, [eye + t for t in pw])
    tmat = mm(acc, d_inv)

    egc = [jnp.exp(t) for t in gc]
    rhs = [jnp.concatenate([_block_diag((vt * bt).astype(BF16), low),
                            _block_diag((kt * (bt * et)).astype(BF16), low)], axis=1)
           for vt, kt, bt, et in zip(v, k, b, egc)]
    uw = [_bdot(t, r) for t, r in zip(tmat, rhs)]
    for n, (ci, p) in enumerate(units):
        g_last = gc[n][c - 1:c, :]
        u_ref[ci, p] = uw[n][:, :LANES]
        w_ref[ci, p] = uw[n][:, LANES:]
        qk_ref[ci, p] = qk[n]
        qd_ref[ci, p] = q[n] * egc[n]
        kd_ref[ci, p] = k[n] * jnp.exp(g_last - gc[n])
        egl_ref[ci, p] = jnp.broadcast_to(jnp.exp(g_last), (SUBLANES, LANES))


def _dn_scan(ci, work_refs, s_ref, gs_ref, nw, o_ref, consts):
    u_ref, w_ref, qk_ref, qd_ref, kd_ref, egl_ref = work_refs
    low, mask_bd, head_mean2 = consts
    c = CHUNK
    rows = slice(ci * c, (ci + 1) * c)
    pairs = range(_DN_PAIRS)
    s_old = [s_ref[p] for p in pairs]
    wq = [_bdot(jnp.concatenate([w_ref[ci, p], qd_ref[ci, p]], axis=0), s_old[p]) for p in pairs]
    v_new = [u_ref[ci, p] - wq[p][:c] for p in pairs]
    o = [wq[p][c:] + _bdot(qk_ref[ci, p], _block_diag(v_new[p].astype(BF16), low)) for p in pairs]
    kv = [_bdot_tn(kd_ref[ci, p], v_new[p]) for p in pairs]
    for p in pairs:
        s_ref[p] = s_old[p] * egl_ref[ci, p][0:1, :] + jnp.where(mask_bd, kv[p], 0.0)
    ms = [jnp.dot(jnp.concatenate(_split(t * t, 2), axis=1), head_mean2, preferred_element_type=F32)
          for t in o]
    for p in pairs:
        ls = slice(p * LANES, (p + 1) * LANES)
        y = (o[p] * lax.rsqrt(ms[p] + EPS)) * nw * gs_ref[rows, ls]
        o_ref[rows, ls] = y.astype(o_ref.dtype)


def _dn_kernel(qn_ref, kn_ref, vb_ref, gs_ref, bexp_ref, gcexp_ref, nw_ref, o_ref,
               s_ref, u_ref, w_ref, qk_ref, qd_ref, kd_ref, egl_ref, *, groups_per_seq):
    i = pl.program_id(0)

    @pl.when(i % groups_per_seq == 0)
    def _():
        s_ref[...] = jnp.zeros_like(s_ref)

    c = CHUNK
    tm = o_ref.shape[0]
    n_chunks = tm // c
    lane = lax.broadcasted_iota(jnp.int32, (c, LANES), 1)
    low = lane < HEAD_DIM
    i_idx = lax.broadcasted_iota(jnp.int32, (c, LANES), 0)
    j_idx = lane & (c - 1)
    ones3 = jnp.ones((c, 3 * c), BF16)
    rb = lax.broadcasted_iota(jnp.int32, (LANES, LANES), 0)
    cb = lax.broadcasted_iota(jnp.int32, (LANES, LANES), 1)
    mask_bd = (rb < HEAD_DIM) == (cb < HEAD_DIM)
    head_mean = jnp.where(mask_bd, 1.0 / HEAD_DIM, 0.0).astype(BF16)
    head_mean2 = jnp.concatenate([head_mean, head_mean], axis=0)
    data_refs = (qn_ref, kn_ref, vb_ref, bexp_ref, gcexp_ref)
    work_refs = (u_ref, w_ref, qk_ref, qd_ref, kd_ref, egl_ref)
    intra_consts = (low, i_idx, j_idx, ones3)
    scan_consts = (low, mask_bd, head_mean2)
    nw = nw_ref[...]

    groups = [list(range(s, s + _DN_GROUP)) for s in range(0, n_chunks, _DN_GROUP)]
    _dn_intra(groups[0], data_refs, work_refs, intra_consts)
    for j, grp in enumerate(groups):
        if j + 1 < len(groups):
            _dn_intra(groups[j + 1], data_refs, work_refs, intra_consts)
        for ci in grp:
            _dn_scan(ci, work_refs, s_ref, gs_ref, nw, o_ref, scan_consts)


def _deltanet(qn, kn, vb, gs, bexp, gcexp, norm_w, seq_len):
    n = qn.shape[0]
    tm = TOKEN_TILE
    nw2 = jnp.concatenate([norm_w, norm_w]).reshape(1, LANES)
    row = lambda width: pl.BlockSpec((tm, width), lambda i: (i, 0))
    return pl.pallas_call(
        functools.partial(_dn_kernel, groups_per_seq=seq_len // tm),
        grid=(n // tm,),
        in_specs=[row(B_W)] * 6 + [_const_spec((1, LANES))],
        out_specs=row(B_W),
        out_shape=jax.ShapeDtypeStruct((n, B_W), BF16),
        scratch_shapes=[pltpu.VMEM((_DN_PAIRS, LANES, LANES), F32)]
        + [pltpu.VMEM((tm // CHUNK, _DN_PAIRS, CHUNK, LANES), F32)] * 5
        + [pltpu.VMEM((tm // CHUNK, _DN_PAIRS, SUBLANES, LANES), F32)],
        compiler_params=_cparams("arbitrary"),
        name="gated_deltanet",
    )(qn, kn, vb, gs, bexp, gcexp, nw2)


def _resident_spec(shape):
    nd = len(shape)
    return pl.BlockSpec(shape, lambda *_: (0,) * nd, pipeline_mode=pl.Buffered(1))


def _mid0_kernel(attn_ref, dn_ref, x_ref, wo_ref, g1_ref, nw_ref, sc_ref, sh_ref, g2_ref,
                 wg_ref, wu_ref, wd_ref, o_ref):
    mix = (jnp.dot(attn_ref[...], wo_ref[:A_Q_W], preferred_element_type=F32)
           + jnp.dot(dn_ref[...], wo_ref[A_Q_W:], preferred_element_type=F32))
    x1 = x_ref[...] + g1_ref[0] * mix
    hn = _norm_mod(x1, nw_ref[...], sc_ref[0], sh_ref[0]).astype(BF16)
    hg = jnp.dot(hn, wg_ref[...], preferred_element_type=F32)
    hu = jnp.dot(hn, wu_ref[...], preferred_element_type=F32)
    act = (_silu(hg) * hu).astype(BF16)
    o_ref[...] = x1 + g2_ref[0] * jnp.dot(act, wd_ref[...], preferred_element_type=F32)


def _mid0(attn, dn, x2d, w_out, g1, nw, sc, sh, g2, wg, wu, wd, seq_len):
    n, d = x2d.shape
    tm = TOKEN_TILE
    tps = seq_len // tm
    row = lambda width: pl.BlockSpec((tm, width), lambda i: (i, 0))
    per_b = pl.BlockSpec((1, 1, d), lambda i: (i // tps, 0, 0))
    return pl.pallas_call(
        _mid0_kernel,
        grid=(n // tm,),
        in_specs=[row(A_Q_W), row(B_W), row(d), _resident_spec(w_out.shape), per_b,
                  _const_spec((1, d)), per_b, per_b, per_b,
                  _resident_spec(wg.shape), _resident_spec(wu.shape), _resident_spec(wd.shape)],
        out_specs=row(d),
        out_shape=jax.ShapeDtypeStruct((n, d), F32),
        compiler_params=_cparams("parallel"),
        name="out_proj0_swiglu",
    )(attn, dn, x2d, w_out.astype(BF16), g1, nw.reshape(1, d), sc, sh, g2,
      wg.astype(BF16), wu.astype(BF16), wd.astype(BF16))


def _gelu_tanh(x):
    return 0.5 * x * (1.0 + jnp.tanh(math.sqrt(2.0 / math.pi) * (x + 0.044715 * (x * x * x))))


def _linear_scan(a, b, h0):
    n, width = a.shape
    groups = n // SUBLANES
    a = a.reshape(groups, SUBLANES, width)
    b = b.reshape(groups, SUBLANES, width)
    in_group = lax.broadcasted_iota(jnp.int32, a.shape, 1)
    s = 1
    while s < SUBLANES:
        a_sh = pltpu.roll(a, s, 1)
        b_sh = pltpu.roll(b, s, 1)
        valid = in_group >= s
        b = jnp.where(valid, a * b_sh + b, b)
        a = jnp.where(valid, a * a_sh, a)
        s *= 2
    carry = jnp.broadcast_to(h0, (SUBLANES, width))
    out = []
    for g in range(groups):
        hg = a[g] * carry + b[g]
        out.append(hg)
        carry = jnp.broadcast_to(hg[SUBLANES - 1:SUBLANES, :], hg.shape)
    return jnp.concatenate(out, axis=0)


def _mix1_tile(x, nw, sc, sh, w_ref, cw_ref, cb_ref, ga_ref, gab_ref, gx_ref, gxb_ref, lam_ref, sw_ref,
               tail_c_ref, tail_d_ref, h_ref):
    hn = _norm_mod(x, nw, sc, sh, on_mxu=True).astype(BF16)
    proj = jnp.dot(hn, w_ref[...], preferred_element_type=F32)
    w_l = LRU_WIDTH
    xc_in = proj[:, :w_l]
    yc = proj[:, w_l:2 * w_l]
    bd = proj[:, 2 * w_l:2 * w_l + SC_WIDTH]
    cd = proj[:, 2 * w_l + SC_WIDTH:2 * w_l + 2 * SC_WIDTH]
    hd = proj[:, 2 * w_l + 2 * SC_WIDTH:]
    tm = xc_in.shape[0]

    kc = cw_ref.shape[0]
    tail = tail_c_ref[...]
    cw = cw_ref[...]
    xc = xc_in * cw[kc - 1:kc] + cb_ref[...]
    for k in range(1, kc):
        xc = xc + _shift_rows(xc_in, k, tail) * cw[kc - 1 - k:kc - k]
    tail_c_ref[...] = xc_in[tm - SUBLANES:]

    xb = xc.astype(BF16)
    gw = ga_ref.shape[1]
    ra, ri = [], []
    for p in range(ga_ref.shape[0]):
        xin = xb[:, p * gw:(p + 1) * gw]
        ra.append(jnp.dot(xin, ga_ref[p], preferred_element_type=F32))
        ri.append(jnp.dot(xin, gx_ref[p], preferred_element_type=F32))
    r = _sigmoid(jnp.concatenate(ra, axis=1) + gab_ref[...])
    ig = _sigmoid(jnp.concatenate(ri, axis=1) + gxb_ref[...])
    log_a = (-LRU_C) * r * _softplus(-lam_ref[...])
    a = jnp.exp(log_a)
    one_m_a2 = -jnp.tanh(log_a) * (a * a + 1.0)
    root = jnp.where(one_m_a2 > 0.0, one_m_a2 * lax.rsqrt(one_m_a2), 0.0)
    b = root * (ig * xc)
    h = _linear_scan(a, b, h_ref[0:1, :])
    h_ref[...] = jnp.broadcast_to(h[tm - 1:tm, :], h_ref.shape)
    yc_out = h * _gelu_tanh(yc)

    ks = sw_ref.shape[0]
    ch = cd * hd
    tail_d = tail_d_ref[...]
    sw = sw_ref[...]
    conv = ch * sw[ks - 1:ks]
    for k in range(1, ks):
        conv = conv + _shift_rows(ch, k, tail_d) * sw[ks - 1 - k:ks - k]
    tail_d_ref[...] = ch[tm - SUBLANES:]
    return jnp.concatenate([yc_out, bd * conv], axis=1)


def _pair_block_diag(gw):
    nb, bw, _ = gw.shape
    g2 = gw.reshape(nb // 2, 2, bw, bw)
    z = jnp.zeros((nb // 2, bw, bw), gw.dtype)
    top = jnp.concatenate([g2[:, 0], z], axis=2)
    bot = jnp.concatenate([z, g2[:, 1]], axis=2)
    return jnp.concatenate([top, bot], axis=1).astype(BF16)


def _route_tile(cat, x, wo_ref, g1, nw, sc, sh, rw_ref, rb_ref, carry_ref):
    x3 = x + g1 * jnp.dot(cat, wo_ref[...], preferred_element_type=F32)
    hn = _norm_mod(x3, nw, sc, sh)
    tm = hn.shape[0]
    lane = lax.broadcasted_iota(jnp.int32, (tm, LANES), 1)
    h_hi, h_lo = _split(hn, 2)
    both = jnp.dot(h_hi, rw_ref[...], preferred_element_type=F32)
    logits = (both[:, :LANES] + both[:, LANES:]
              + jnp.dot(h_lo, rw_ref[:, :LANES], preferred_element_type=F32) + rb_ref[...])
    lg = jnp.where(lane < N_EXPERTS, logits, NEG_BIG)
    m1 = jnp.max(lg, axis=1, keepdims=True)
    i1 = jnp.min(jnp.where(lg == m1, lane, LANES), axis=1, keepdims=True)
    lg2 = jnp.where(lane == i1, NEG_BIG, lg)
    m2 = jnp.max(lg2, axis=1, keepdims=True)
    i2 = jnp.min(jnp.where(lg2 == m2, lane, LANES), axis=1, keepdims=True)
    e2 = jnp.exp(m2 - m1)
    w1 = 1.0 / (1.0 + e2)
    w2 = e2 / (1.0 + e2)

    hit1 = lane == i1
    hit2 = lane == i2
    sel = jnp.logical_or(hit1, hit2).astype(F32)
    r_i = lax.broadcasted_iota(jnp.int32, (tm, tm), 0)
    c_i = lax.broadcasted_iota(jnp.int32, (tm, tm), 1)
    tril = (r_i >= c_i).astype(BF16)
    incl = jnp.dot(tril, sel.astype(BF16), preferred_element_type=F32)
    carry = carry_ref[0:1, :]
    excl = incl - sel + carry
    r1 = jnp.sum(jnp.where(hit1, excl, 0.0), axis=1, keepdims=True)
    r2 = jnp.sum(jnp.where(hit2, excl, 0.0), axis=1, keepdims=True)
    total = carry + incl[tm - 1:tm, :]
    carry_ref[...] = jnp.broadcast_to(total, carry_ref.shape)

    meta = jnp.where(lane == 0, i1, 0)
    meta = jnp.where(lane == 1, i2, meta)
    meta = jnp.where(lane == 2, r1.astype(jnp.int32), meta)
    meta = jnp.where(lane == 3, r2.astype(jnp.int32), meta)
    wt = jnp.where(lane == 0, w1, jnp.where(lane == 1, w2, 0.0))
    return x3, hn, meta, wt, carry, total


def _mix1_kernel(x_ref, nw_ref, sc_ref, sh_ref, w_ref, cw_ref, cb_ref, ga_ref, gab_ref, gx_ref, gxb_ref,
                 lam_ref, sw_ref, o_ref, tail_c_ref, tail_d_ref, h_ref, *, tiles_per_seq):
    i = pl.program_id(0)

    @pl.when(i % tiles_per_seq == 0)
    def _():
        tail_c_ref[...] = jnp.zeros_like(tail_c_ref)
        tail_d_ref[...] = jnp.zeros_like(tail_d_ref)
        h_ref[...] = jnp.zeros_like(h_ref)

    cat = _mix1_tile(x_ref[...], nw_ref[...], sc_ref[0], sh_ref[0], w_ref, cw_ref, cb_ref, ga_ref,
                     gab_ref, gx_ref, gxb_ref, lam_ref, sw_ref, tail_c_ref, tail_d_ref, h_ref)
    o_ref[...] = cat.astype(o_ref.dtype)


def _mix1(x2d, nw, sc, sh, w_in, conv_w, conv_b, ga_w, ga_b, gx_w, gx_b, lam, sconv_w, seq_len):
    n, d = x2d.shape
    tm = TOKEN_TILE
    tps = seq_len // tm
    cd_in = w_in.shape[1]
    cd_out = LRU_WIDTH + SC_WIDTH
    row = lambda width: pl.BlockSpec((tm, width), lambda i: (i, 0))
    per_b = pl.BlockSpec((1, 1, d), lambda i: (i // tps, 0, 0))
    ga = _pair_block_diag(ga_w)
    gx = _pair_block_diag(gx_w)
    vec = lambda v: v.reshape(1, -1)
    return pl.pallas_call(
        functools.partial(_mix1_kernel, tiles_per_seq=tps),
        grid=(n // tm,),
        in_specs=[row(d), _const_spec((1, d)), per_b, per_b, _resident_spec((d, cd_in)),
                  _const_spec(conv_w.shape), _const_spec((1, LRU_WIDTH)),
                  _const_spec(ga.shape), _const_spec((1, LRU_WIDTH)),
                  _const_spec(gx.shape), _const_spec((1, LRU_WIDTH)),
                  _const_spec((1, LRU_WIDTH)), _const_spec(sconv_w.shape)],
        out_specs=row(cd_out),
        out_shape=jax.ShapeDtypeStruct((n, cd_out), BF16),
        scratch_shapes=[pltpu.VMEM((SUBLANES, LRU_WIDTH), F32), pltpu.VMEM((SUBLANES, SC_WIDTH), F32),
                        pltpu.VMEM((SUBLANES, LRU_WIDTH), F32)],
        compiler_params=_cparams("arbitrary"),
        name="rglru_shortconv_mixer",
    )(x2d, vec(nw), sc, sh, w_in.astype(BF16), conv_w, vec(conv_b), ga, vec(ga_b), gx, vec(gx_b),
      vec(lam), sconv_w)


def _route_kernel(cat_ref, x_ref, wo_ref, g1_ref, nw_ref, sc_ref, sh_ref, rw_ref, rb_ref,
                  x3_ref, hn_ref, metat_ref, meta_ref, wt_ref, base_ref, cnt_ref, carry_ref):
    @pl.when(pl.program_id(0) == 0)
    def _():
        carry_ref[...] = jnp.zeros_like(carry_ref)

    x3, hn, meta, wt, before, total = _route_tile(cat_ref[...], x_ref[...], wo_ref, g1_ref[0], nw_ref[...],
                                                  sc_ref[0], sh_ref[0], rw_ref, rb_ref, carry_ref)
    x3_ref[...] = x3
    hn_ref[...] = hn
    meta_ref[...] = meta
    metat_ref[...] = jnp.transpose(meta.astype(F32))[:SUBLANES].astype(jnp.int32)
    wt_ref[...] = wt
    base_ref[0] = jnp.broadcast_to(before, base_ref.shape[1:]).astype(jnp.int32)
    cnt_ref[...] = jnp.broadcast_to(total, cnt_ref.shape).astype(jnp.int32)


def _route(cat, x2d, w_out, g1, nw, sc, sh, router_w, router_b, seq_len):
    n, d = x2d.shape
    tm = TOKEN_TILE
    tps = seq_len // tm
    row = lambda width: pl.BlockSpec((tm, width), lambda i: (i, 0))
    per_b = pl.BlockSpec((1, 1, d), lambda i: (i // tps, 0, 0))
    rw = jnp.zeros((d, LANES), F32).at[:, :N_EXPERTS].set(router_w)
    rw_hi = rw.astype(BF16)
    rw = jnp.concatenate([rw_hi, (rw - rw_hi.astype(F32)).astype(BF16)], axis=1)
    rb = jnp.zeros((1, LANES), F32).at[0, :N_EXPERTS].set(router_b)
    return pl.pallas_call(
        _route_kernel,
        grid=(n // tm,),
        in_specs=[row(cat.shape[1]), row(d), _resident_spec(w_out.shape), per_b, _const_spec((1, d)),
                  per_b, per_b, _const_spec((d, 2 * LANES)), _const_spec((1, LANES))],
        out_specs=[row(d), row(d), pl.BlockSpec((SUBLANES, tm), lambda i: (0, i)), row(LANES), row(LANES),
                   pl.BlockSpec((1, SUBLANES, LANES), lambda i: (i, 0, 0)), _const_spec((SUBLANES, LANES))],
        out_shape=[jax.ShapeDtypeStruct((n, d), F32), jax.ShapeDtypeStruct((n, d), F32),
                   jax.ShapeDtypeStruct((SUBLANES, n), jnp.int32), jax.ShapeDtypeStruct((n, LANES), jnp.int32),
                   jax.ShapeDtypeStruct((n, LANES), F32),
                   jax.ShapeDtypeStruct((n // tm, SUBLANES, LANES), jnp.int32),
                   jax.ShapeDtypeStruct((SUBLANES, LANES), jnp.int32)],
        scratch_shapes=[pltpu.VMEM((SUBLANES, LANES), F32)],
        compiler_params=_cparams("arbitrary"),
        name="out_proj1_router",
    )(cat, x2d, w_out.astype(BF16), g1, nw.reshape(1, d), sc, sh, rw, rb)


def _local_rows(tr):
    return 2 * tr + N_EXPERTS * SUBLANES


def _xs_rows(n):
    worst = 2 * n + (n // TOKEN_TILE) * N_EXPERTS * (SUBLANES - 1)
    return (-(-worst // MOE_TILE) + N_EXPERTS) * MOE_TILE


def _local_pos(e_k, r_k, delta_ref, tile):
    shift = jnp.zeros_like(r_k)
    for e in range(N_EXPERTS):
        shift = jnp.where(e_k == e, delta_ref[tile * N_EXPERTS + e], shift)
    return r_k + shift


_RUN_BLOCK = 64


def _for_each_group(tile, lstart_ref, run_ref, gstart_ref, fn):
    big_shift = int(math.log2(_RUN_BLOCK))
    small_shift = int(math.log2(SUBLANES))
    for e in range(N_EXPERTS):
        k = tile * N_EXPERTS + e
        l_start = lstart_ref[k]
        g_start = gstart_ref[k]
        n_big = run_ref[k] >> big_shift
        tail = n_big << big_shift

        def big(g, c, l_start=l_start, g_start=g_start):
            off = g * _RUN_BLOCK
            fn(pl.multiple_of(l_start + off, SUBLANES), pl.multiple_of(g_start + off, SUBLANES), _RUN_BLOCK)
            return c

        def small(g, c, l_start=l_start, g_start=g_start, tail=tail):
            off = tail + g * SUBLANES
            fn(pl.multiple_of(l_start + off, SUBLANES), pl.multiple_of(g_start + off, SUBLANES), SUBLANES)
            return c

        lax.fori_loop(0, n_big, big, 0)
        lax.fori_loop(0, (run_ref[k] - tail) >> small_shift, small, 0)


def _zero_fill_gaps(gap_ref, used_ref, xs_ref, zero_ref, sem):
    zero_ref[...] = jnp.zeros_like(zero_ref)
    zr = zero_ref.shape[0]
    per_tile = MOE_TILE // zr
    shift = int(math.log2(SUBLANES))

    def gap_copy(e, g):
        row = pl.multiple_of(gap_ref[e] + g * SUBLANES, SUBLANES)
        return pltpu.make_async_copy(zero_ref.at[pl.ds(0, SUBLANES)], xs_ref.at[pl.ds(row, SUBLANES)], sem)

    def tile_copy(k):
        row = pl.multiple_of(k * zr, zr)
        return pltpu.make_async_copy(zero_ref, xs_ref.at[pl.ds(row, zr)], sem)

    def both(op):
        for e in range(N_EXPERTS):
            lax.fori_loop(0, gap_ref[N_EXPERTS + e] >> shift, lambda g, c, e=e: (op(gap_copy(e, g)), c)[1], 0)
        lax.fori_loop(used_ref[0] * per_tile, (xs_ref.shape[0] // MOE_TILE) * per_tile,
                      lambda k, c: (op(tile_copy(k)), c)[1], 0)

    both(lambda cp: cp.start())
    both(lambda cp: cp.wait())


def _dispatch_kernel(delta_ref, lstart_ref, run_ref, gstart_ref, gap_ref, used_ref, hn_ref, meta_ref, xs_ref,
                     sbuf_ref, zero_ref, sem):
    j = pl.program_id(0)
    tr = hn_ref.shape[0]
    lrows = sbuf_ref.shape[1]
    slot = lax.rem(j, 2)

    @pl.when(j == 0)
    def _():
        _zero_fill_gaps(gap_ref, used_ref, xs_ref, zero_ref, sem.at[0])

    meta = meta_ref[...]
    lp1 = _local_pos(meta[0:1], meta[2:3], delta_ref, j)
    lp2 = _local_pos(meta[1:2], meta[3:4], delta_ref, j)
    r_idx = lax.broadcasted_iota(jnp.int32, (lrows, tr), 0)
    onehot = jnp.logical_or(r_idx == lp1, r_idx == lp2).astype(BF16)
    sbuf_ref[slot] = jnp.dot(onehot, hn_ref[...].astype(BF16), preferred_element_type=F32)

    def group_copy(buf, local_row, xs_row, rows):
        return pltpu.make_async_copy(sbuf_ref.at[buf, pl.ds(local_row, rows)],
                                     xs_ref.at[pl.ds(xs_row, rows)], sem.at[buf])

    def drain(tile, buf):
        _for_each_group(tile, lstart_ref, run_ref, gstart_ref,
                        lambda lr, xr, rows: group_copy(buf, lr, xr, rows).wait())

    _for_each_group(j, lstart_ref, run_ref, gstart_ref,
                    lambda lr, xr, rows: group_copy(slot, lr, xr, rows).start())

    @pl.when(j > 0)
    def _():
        drain(j - 1, 1 - slot)

    @pl.when(j == pl.num_programs(0) - 1)
    def _():
        drain(j, slot)


def _dispatch(hn, meta_t, tables, gaps, used_tiles):
    n, d = hn.shape
    tr = TOKEN_TILE
    lrows = _local_rows(tr)
    return pl.pallas_call(
        _dispatch_kernel,
        grid_spec=pltpu.PrefetchScalarGridSpec(
            num_scalar_prefetch=6,
            grid=(n // tr,),
            in_specs=[pl.BlockSpec((tr, d), lambda j, *_: (j, 0)),
                      pl.BlockSpec((SUBLANES, tr), lambda j, *_: (0, j))],
            out_specs=pl.BlockSpec(memory_space=pl.ANY),
            scratch_shapes=[pltpu.VMEM((2, lrows, d), F32), pltpu.VMEM((MOE_SUB, d), F32),
                            pltpu.SemaphoreType.DMA((2,))]),
        out_shape=jax.ShapeDtypeStruct((_xs_rows(n), d), F32),
        compiler_params=_cparams("arbitrary"),
        name="moe_dispatch",
    )(*tables, gaps, used_tiles, hn, meta_t)


def _moe_kernel(te_ref, hi_ref, x_ref, wg_ref, wu_ref, wd_ref, o_ref, xb_ref):
    w = pl.program_id(0)
    f = pl.program_id(1)
    tm = x_ref.shape[0]
    sub = MOE_SUB
    sub_shift = int(math.log2(sub))
    hi = hi_ref[w]

    def swiglu_part(xb, wg, wu, wd):
        hg = jnp.dot(xb, wg, preferred_element_type=F32)
        hu = jnp.dot(xb, wu, preferred_element_type=F32)
        act = (_silu(hg) * hu).astype(BF16)
        return jnp.dot(act, wd, preferred_element_type=F32)

    @pl.when(jnp.logical_and(f == 0, hi > 0))
    def _():
        row = lax.broadcasted_iota(jnp.int32, (tm, 1), 0)
        xb_ref[...] = jnp.where(row < hi, x_ref[...], 0.0).astype(BF16)

    @pl.when(hi == tm)
    def _():
        part = swiglu_part(xb_ref[...], wg_ref[0].astype(BF16), wu_ref[0].astype(BF16),
                           wd_ref[0].astype(BF16))

        @pl.when(f == 0)
        def _():
            o_ref[...] = part

        @pl.when(f != 0)
        def _():
            o_ref[...] += part

    @pl.when(hi < tm)
    def _():
        @pl.when(f == 0)
        def _():
            o_ref[...] = jnp.zeros_like(o_ref)

        def sub_block(s, carry):
            rows = pl.ds(pl.multiple_of(s * sub, sub), sub)
            o_ref[rows, :] += swiglu_part(xb_ref[rows, :], wg_ref[0].astype(BF16),
                                          wu_ref[0].astype(BF16), wd_ref[0].astype(BF16))
            return carry

        lax.fori_loop(0, (hi + sub - 1) >> sub_shift, sub_block, 0)


def _moe_ffn(xs, tile_expert, tile_rows, wg, wu, wd):
    rows, d = xs.shape
    tm = MOE_TILE
    tf = MOE_FF_TILE
    nf = wg.shape[2] // tf

    def f_idx(f, hi):
        v = (hi > 0).astype(jnp.int32)
        return f * v + (nf - 1) * (1 - v)

    return pl.pallas_call(
        _moe_kernel,
        grid_spec=pltpu.PrefetchScalarGridSpec(
            num_scalar_prefetch=2,
            grid=(rows // tm, nf),
            in_specs=[pl.BlockSpec((tm, d), lambda w, f, te, hi: (jnp.where(hi[w] > 0, w, 0), 0)),
                      pl.BlockSpec((1, d, tf), lambda w, f, te, hi: (te[w], 0, f_idx(f, hi[w]))),
                      pl.BlockSpec((1, d, tf), lambda w, f, te, hi: (te[w], 0, f_idx(f, hi[w]))),
                      pl.BlockSpec((1, tf, d), lambda w, f, te, hi: (te[w], f_idx(f, hi[w]), 0))],
            out_specs=pl.BlockSpec((tm, d), lambda w, f, te, hi: (w, 0)),
            scratch_shapes=[pltpu.VMEM((tm, d), BF16)]),
        out_shape=jax.ShapeDtypeStruct((rows, d), F32),
        compiler_params=_cparams("arbitrary", "arbitrary"),
        name="moe_expert_swiglu",
    )(tile_expert, tile_rows, xs, wg, wu, wd)


def _combine_kernel(delta_ref, lstart_ref, run_ref, gstart_ref, ys_ref, x_ref, meta_ref, wt_ref, g2_ref, fw_ref,
                    o_ref, ybuf_ref, sem):
    j = pl.program_id(0)
    n_tiles = pl.num_programs(0)
    tr = x_ref.shape[0]
    lrows = ybuf_ref.shape[1]
    slot = lax.rem(j, 2)

    def group_copy(buf, local_row, xs_row, rows):
        return pltpu.make_async_copy(ys_ref.at[pl.ds(xs_row, rows)],
                                     ybuf_ref.at[buf, pl.ds(local_row, rows)], sem.at[buf])

    def fetch(tile, buf):
        ybuf_ref[buf, 2 * tr:, :] = jnp.zeros((lrows - 2 * tr, ybuf_ref.shape[2]), F32)
        _for_each_group(tile, lstart_ref, run_ref, gstart_ref,
                        lambda lr, xr, rows: group_copy(buf, lr, xr, rows).start())

    @pl.when(j == 0)
    def _():
        fetch(0, 0)

    _for_each_group(j, lstart_ref, run_ref, gstart_ref,
                    lambda lr, xr, rows: group_copy(slot, lr, xr, rows).wait())

    @pl.when(j + 1 < n_tiles)
    def _():
        fetch(j + 1, 1 - slot)

    meta = meta_ref[...]
    wt = wt_ref[...]
    lp1 = _local_pos(meta[:, 0:1], meta[:, 2:3], delta_ref, j)
    lp2 = _local_pos(meta[:, 1:2], meta[:, 3:4], delta_ref, j)
    l_idx = lax.broadcasted_iota(jnp.int32, (tr, lrows), 1)
    pick = jnp.where(l_idx == lp1, wt[:, 0:1], 0.0) + jnp.where(l_idx == lp2, wt[:, 1:2], 0.0)
    ffn = _bdot(pick, ybuf_ref[slot])
    x4 = x_ref[...] + g2_ref[0] * ffn
    o_ref[...] = (x4 * _rms_scale(x4)) * fw_ref[...]


def _combine(ys, tables, x3, meta, wt, g2, final_w, seq_len):
    n, d = x3.shape
    tr = TOKEN_TILE
    tps = seq_len // tr
    lrows = -(-_local_rows(tr) // LANES) * LANES
    return pl.pallas_call(
        _combine_kernel,
        grid_spec=pltpu.PrefetchScalarGridSpec(
            num_scalar_prefetch=4,
            grid=(n // tr,),
            in_specs=[pl.BlockSpec(memory_space=pl.ANY),
                      pl.BlockSpec((tr, d), lambda j, *_: (j, 0)),
                      pl.BlockSpec((tr, LANES), lambda j, *_: (j, 0)),
                      pl.BlockSpec((tr, LANES), lambda j, *_: (j, 0)),
                      pl.BlockSpec((1, 1, d), lambda j, *_: (j // tps, 0, 0)),
                      pl.BlockSpec((1, d), lambda j, *_: (0, 0))],
            out_specs=pl.BlockSpec((tr, d), lambda j, *_: (j, 0)),
            scratch_shapes=[pltpu.VMEM((2, lrows, d), F32), pltpu.SemaphoreType.DMA((2,))]),
        out_shape=jax.ShapeDtypeStruct((n, d), F32),
        compiler_params=_cparams("arbitrary"),
        name="moe_combine_final_norm",
    )(*tables, ys, x3, meta, wt, g2, final_w.reshape(1, d))


def _moe_tables(tile_base, counts, n_tokens):
    i32 = lambda t: t.astype(jnp.int32)
    tm = MOE_TILE
    before = tile_base[:, 0, :N_EXPERTS]
    total = counts[0, :N_EXPERTS]
    run = jnp.concatenate([before[1:], total[None]], axis=0) - before
    run = (run + SUBLANES - 1) // SUBLANES * SUBLANES
    l_end = jnp.cumsum(run, axis=1)
    l_start = l_end - run
    g_size = jnp.sum(run, axis=0)
    g_tiles = (g_size + tm - 1) // tm
    tile_end = jnp.cumsum(g_tiles)
    g_off = (tile_end - g_tiles) * tm
    g_end = g_off + g_size
    g_start = g_off[None, :] + jnp.cumsum(run, axis=0) - run
    delta = l_start - before
    gaps = jnp.concatenate([g_end, tile_end * tm - g_end])
    used_tiles = tile_end[-1:]

    w = jnp.arange(_xs_rows(n_tokens) // tm, dtype=jnp.int32)
    te = jnp.minimum(jnp.sum((tile_end[None, :] <= w[:, None]).astype(jnp.int32), axis=1), N_EXPERTS - 1)
    rows = jnp.where(w < tile_end[-1], jnp.clip(jnp.take(g_end, te) - w * tm, 0, tm), 0)
    flat = lambda t: i32(t).reshape(-1)
    return (flat(delta), flat(l_start), flat(run), flat(g_start)), i32(gaps), i32(used_tiles), i32(te), i32(rows)


def kernel(x, c, rel_bias, ada_w, ada_b, norm_mix_w, norm_ffn_w, final_norm_w, ab_w_in, attn_sinks,
           dn_conv_w, dn_a_log, dn_dt_bias, dn_norm_w, ab_w_out, ffn_w_gate, ffn_w_up, ffn_w_down,
           cd_w_in, lru_conv_w, lru_conv_b, lru_gate_a_w, lru_gate_a_b, lru_gate_x_w, lru_gate_x_b,
           lru_lambda, sconv_w, cd_w_out, moe_router_w, moe_router_b, moe_w_gate, moe_w_up, moe_w_down):
    bsz, seq_len, d = x.shape
    n = bsz * seq_len
    x2d = x.reshape(n, d)
    mods = _ada_mods(c, ada_w, ada_b)

    sh1, sc1, g1, sh2, sc2, g2 = (mods[0, k] for k in range(6))
    qa, kd, vd, qn, kn, vb, gs, bexp, gcexp = _in_proj0(
        x2d, norm_mix_w[0], sc1, sh1, ab_w_in[0], dn_conv_w[0], dn_a_log[0], dn_dt_bias[0], seq_len)
    attn = _attention(qa, kd, vd, _bias_table(rel_bias), attn_sinks[0], seq_len)
    dn = _deltanet(qn, kn, vb, gs, bexp, gcexp, dn_norm_w[0], seq_len)
    x2 = _mid0(attn, dn, x2d, ab_w_out[0], g1, norm_ffn_w[0], sc2, sh2, g2,
               ffn_w_gate[0], ffn_w_up[0], ffn_w_down[0], seq_len)

    sh1, sc1, g1, sh2, sc2, g2 = (mods[1, k] for k in range(6))
    cat = _mix1(x2, norm_mix_w[1], sc1, sh1, cd_w_in[0], lru_conv_w[0], lru_conv_b[0],
                lru_gate_a_w[0], lru_gate_a_b[0], lru_gate_x_w[0], lru_gate_x_b[0],
                lru_lambda[0], sconv_w[0], seq_len)
    x3, hn4, meta_t, meta, wt, tile_base, counts = _route(
        cat, x2, cd_w_out[0], g1, norm_ffn_w[1], sc2, sh2, moe_router_w[0], moe_router_b[0], seq_len)
    tables, gaps, used_tiles, tile_expert, tile_rows = _moe_tables(tile_base, counts, n)
    xs = _dispatch(hn4, meta_t, tables, gaps, used_tiles)
    ys = _moe_ffn(xs, tile_expert, tile_rows, moe_w_gate[0], moe_w_up[0], moe_w_down[0])
    out = _combine(ys, tables, x3, meta, wt, g2, final_norm_w, seq_len)
    return out.reshape(bsz, seq_len, d)
```

```python
import functools
import math

import numpy as np
import jax
import jax.numpy as jnp
from jax import lax
from jax.experimental import pallas as pl
from jax.experimental.pallas import tpu as pltpu

D_MODEL = 1024
EPS = 1e-6
HEAD_DIM = 64
A_Q_HEADS = 8
A_KV_HEADS = 2
WINDOW = 128
N_BUCKETS = 32
MAX_DISTANCE = 128
B_HEADS = 8
B_CONV = 4
CHUNK = 64
A_Q_W = A_Q_HEADS * HEAD_DIM
A_KV_W = A_KV_HEADS * HEAD_DIM
B_W = B_HEADS * HEAD_DIM
B_QKV_W = 3 * B_W
LRU_WIDTH = D_MODEL
LRU_BLOCKS = 8
LRU_C = 8.0
SC_WIDTH = D_MODEL // 2
D_FF = 2816
N_EXPERTS = 8
D_FF_EXPERT = 3584

LANES = 128
SUBLANES = 8
VMEM_LIMIT_BYTES = 56 * 1024 * 1024
TOKEN_TILE = 512
MOE_TILE = 1024
MOE_SUB = 256
MOE_FF_TILE = 512
NEG_BIG = -1e30

F32 = jnp.float32
BF16 = jnp.bfloat16


def _cparams(*sem):
    return pltpu.CompilerParams(dimension_semantics=tuple(sem), vmem_limit_bytes=VMEM_LIMIT_BYTES)


def _const_spec(shape):
    nd = len(shape)
    return pl.BlockSpec(shape, lambda *_: (0,) * nd)


def _bdot(a, b):
    return jnp.dot(a.astype(BF16), b.astype(BF16), preferred_element_type=F32)


def _bdot_tn(a, b):
    return lax.dot_general(a.astype(BF16), b.astype(BF16), (((0,), (0,)), ((), ())),
                           preferred_element_type=F32)


def _split(x, n):
    parts = []
    r = x
    for i in range(n):
        p = r.astype(BF16)
        parts.append(p)
        if i + 1 < n:
            r = r - p.astype(F32)
    return parts


def _dot_x(a, b, na=2, nb=2):
    asp = _split(a, na) if na > 1 else [a.astype(BF16)]
    bsp = _split(b, nb) if nb > 1 else [b.astype(BF16)]
    acc = None
    for i, ai in enumerate(asp):
        for j, bj in enumerate(bsp):
            if i + j >= max(na, nb):
                continue
            t = jnp.dot(ai, bj, preferred_element_type=F32)
            acc = t if acc is None else acc + t
    return acc


def _dot_terms(a, b_stacked, n):
    return jnp.dot(jnp.concatenate(_split(a, n), axis=1), b_stacked, preferred_element_type=F32)


def _silu(x):
    return x * (1.0 / (1.0 + jnp.exp(-x)))


def _sigmoid(x):
    return 1.0 / (1.0 + jnp.exp(-x))


def _log1p(z):
    u = 1.0 + z
    tiny = u == 1.0
    return jnp.where(tiny, z, jnp.log(u) * (z / jnp.where(tiny, 1.0, u - 1.0)))


def _softplus(x):
    return jnp.maximum(x, 0.0) + _log1p(jnp.exp(-jnp.abs(x)))


def _rms_scale(x):
    width = x.shape[1]
    mean_w = jnp.full((width, LANES), 1.0 / width, BF16)
    ms = _dot_x(x * x, mean_w, 2, 1)
    r = lax.rsqrt(ms + EPS)
    return jnp.concatenate([r] * (width // LANES), axis=1)


def _norm_mod(x, w, sc, sh, on_mxu=False):
    if on_mxu:
        scale = _rms_scale(x)
    else:
        scale = lax.rsqrt(jnp.mean(x * x, axis=-1, keepdims=True) + EPS)
    return (x * scale) * w * (1.0 + sc) + sh


def _shift_rows(x, k, prev_tail):
    n, width = x.shape
    x3 = x.reshape(n // SUBLANES, SUBLANES, width)
    rot = pltpu.roll(x3, k, 1)
    rot_prev = jnp.concatenate([pltpu.roll(prev_tail, k, 0)[None], rot[:-1]], axis=0)
    sub = lax.broadcasted_iota(jnp.int32, x3.shape, 1)
    return jnp.where(sub >= k, rot, rot_prev).reshape(n, width)


def _ada_kernel(c_ref, w_ref, b_ref, o_ref):
    c = c_ref[...]
    cond = _silu(c)
    o_ref[0] = _dot_x(cond, w_ref[0], 3, 2) + b_ref[0]


def _ada_mods(c, ada_w, ada_b):
    depth, d, six_d = ada_w.shape
    bsz = c.shape[0]
    rows = max(SUBLANES, bsz)
    c_pad = jnp.zeros((rows, d), F32).at[:bsz].set(c)
    tn = 1536
    out = pl.pallas_call(
        _ada_kernel,
        grid=(depth, six_d // tn),
        in_specs=[pl.BlockSpec((rows, d), lambda l, j: (0, 0)),
                  pl.BlockSpec((1, d, tn), lambda l, j: (l, 0, j)),
                  pl.BlockSpec((1, 1, tn), lambda l, j: (l, 0, j))],
        out_specs=pl.BlockSpec((1, rows, tn), lambda l, j: (l, 0, j)),
        out_shape=jax.ShapeDtypeStruct((depth, rows, six_d), F32),
        compiler_params=_cparams("parallel", "parallel"),
        name="ada_mods",
    )(c_pad, ada_w, ada_b.reshape(depth, 1, six_d))
    return out[:, :bsz].reshape(depth, bsz, 6, 1, d).transpose(0, 2, 1, 3, 4)


def _t5_bucket(dist):
    max_exact = N_BUCKETS // 2
    d = np.maximum(dist, 0)
    large = max_exact + (np.log(np.maximum(d, 1) / max_exact) / math.log(MAX_DISTANCE / max_exact)
                         * (N_BUCKETS - max_exact)).astype(np.int32)
    large = np.minimum(large, N_BUCKETS - 1)
    return np.where(d < max_exact, d, large).astype(np.int32)


def _band_buckets():
    qi = np.arange(WINDOW)[:, None]
    s = np.arange(2 * WINDOW)[None, :]
    dist = qi + WINDOW - s
    in_window = (dist >= 0) & (dist < WINDOW)
    return np.where(in_window, _t5_bucket(dist), -1).astype(np.int32)


def _bias_kernel(rb_ref, bucket_ref, o_ref):
    h = pl.program_id(0)
    bucket = bucket_ref[...]
    acc = jnp.zeros(bucket.shape, F32)
    for b in range(N_BUCKETS):
        acc = jnp.where(bucket == b, rb_ref[b, h], acc)
    o_ref[0] = jnp.where(bucket < 0, NEG_BIG, acc)


def _bias_table(rel_bias):
    bucket = jnp.asarray(_band_buckets())
    out = pl.pallas_call(
        _bias_kernel,
        grid=(A_Q_HEADS,),
        in_specs=[pl.BlockSpec(memory_space=pltpu.SMEM),
                  _const_spec((WINDOW, 2 * WINDOW))],
        out_specs=pl.BlockSpec((1, WINDOW, 2 * WINDOW), lambda h: (h, 0, 0)),
        out_shape=jax.ShapeDtypeStruct((A_Q_HEADS, WINDOW, 2 * WINDOW), F32),
        compiler_params=_cparams("parallel"),
        name="attn_bias_table",
    )(rel_bias, bucket)
    return out.reshape(A_Q_HEADS // 2, 2 * WINDOW, 2 * WINDOW)


_C_QA = 0
_C_KA = _C_QA + A_Q_W
_C_VA = _C_KA + A_KV_W
_C_QKV = _C_VA + A_KV_W
_C_GATE = _C_QKV + B_QKV_W
_C_SMALL = _C_GATE + B_W
_AB_COLS = _C_SMALL + LANES


def _ab_in_weight(w_in):
    return jnp.pad(w_in, ((0, 0), (0, _AB_COLS - w_in.shape[1]))).astype(BF16)


def _dup_heads(t, low):
    swapped = pltpu.roll(t, HEAD_DIM, 1)
    return jnp.concatenate([jnp.where(low, t, swapped), jnp.where(low, swapped, t)], axis=1)


def _chunk_tril(tm):
    r = np.arange(tm)
    return ((r[:, None] >= r[None, :]) & (r[:, None] // CHUNK == r[None, :] // CHUNK)).astype(np.float32)


def _head_selector():
    e = np.zeros((B_W, LANES), np.float32)
    for h in range(B_HEADS):
        e[h * HEAD_DIM:(h + 1) * HEAD_DIM, h] = 1.0
    return e


def _in0_kernel(x_ref, nw_ref, sc_ref, sh_ref, w_ref, cw_ref, sel_ref, selt2_ref, selt3_ref, tril_ref,
                alog_ref, dtb_ref,
                qa_ref, kd_ref, vd_ref, qn_ref, kn_ref, vb_ref, gs_ref, bexp_ref, gcexp_ref,
                tail_ref, *, tiles_per_seq):
    i = pl.program_id(0)

    @pl.when(i % tiles_per_seq == 0)
    def _():
        tail_ref[...] = jnp.zeros_like(tail_ref)

    hn = _norm_mod(x_ref[...], nw_ref[...], sc_ref[0], sh_ref[0])
    proj = jnp.dot(hn.astype(BF16), w_ref[...], preferred_element_type=F32)
    tm = proj.shape[0]
    low = lax.broadcasted_iota(jnp.int32, (tm, LANES), 1) < HEAD_DIM

    small = proj[:, _C_SMALL:]
    lane = lax.broadcasted_iota(jnp.int32, small.shape, 1)
    beta = jnp.where(lane < B_HEADS, _sigmoid(small), 0.0)
    dec = pltpu.roll(small, LANES - B_HEADS, 1)
    g = jnp.where(lane < B_HEADS, -jnp.exp(alog_ref[...]) * _softplus(dec + dtb_ref[...]), 0.0)
    bexp_ref[...] = _dot_terms(beta, selt2_ref[...], 2)
    gc = _dot_x(tril_ref[...], g, 1, 3)
    gcexp_ref[...] = _dot_terms(gc, selt3_ref[...], 3)

    qa_ref[...] = proj[:, _C_QA:_C_KA].astype(BF16)
    kd_ref[...] = _dup_heads(proj[:, _C_KA:_C_VA], low).astype(BF16)
    vd_ref[...] = _dup_heads(proj[:, _C_VA:_C_QKV], low).astype(BF16)

    def conv_silu(block):
        cols = slice(block * B_W, (block + 1) * B_W)
        xq = proj[:, _C_QKV + block * B_W:_C_QKV + (block + 1) * B_W]
        tail = tail_ref[:, cols]
        cw = cw_ref[:, cols]
        y = xq * cw[B_CONV - 1:B_CONV]
        for k in range(1, B_CONV):
            y = y + _shift_rows(xq, k, tail) * cw[B_CONV - 1 - k:B_CONV - k]
        tail_ref[:, cols] = xq[tm - SUBLANES:]
        return _silu(y)

    q = conv_silu(0)
    k_ = conv_silu(1)
    ssq = _dot_x(jnp.concatenate([q * q, k_ * k_], axis=1), sel_ref[...], 2, 1)
    r = lax.rsqrt(ssq + EPS)
    q_scale = _dot_terms(r[:, :LANES], selt2_ref[...], 2)
    k_scale = _dot_terms(r[:, LANES:], selt2_ref[...], 2)
    vb_ref[...] = conv_silu(2)
    gs_ref[...] = _silu(proj[:, _C_GATE:_C_SMALL])
    qn_ref[...] = q * q_scale * (HEAD_DIM ** -0.5)
    kn_ref[...] = k_ * k_scale


def _in_proj0(x2d, nw, sc, sh, w_in, conv_w, a_log, dt_bias, seq_len):
    n, d = x2d.shape
    tm = TOKEN_TILE
    tiles_per_seq = seq_len // tm
    w = _ab_in_weight(w_in)
    hs = _head_selector()
    zeros = np.zeros_like(hs)
    sel = jnp.asarray(np.block([[hs, zeros], [zeros, hs]]), BF16)
    selt2 = jnp.asarray(np.tile(hs.T, (2, 1)), BF16)
    selt3 = jnp.asarray(np.tile(hs.T, (3, 1)), BF16)
    tril = jnp.asarray(_chunk_tril(tm), BF16)
    pad8 = lambda v: jnp.zeros((1, LANES), F32).at[0, :B_HEADS].set(v)
    row = lambda width: pl.BlockSpec((tm, width), lambda i: (i, 0))
    per_b = pl.BlockSpec((1, 1, d), lambda i: (i // tiles_per_seq, 0, 0))
    outs = pl.pallas_call(
        functools.partial(_in0_kernel, tiles_per_seq=tiles_per_seq),
        grid=(n // tm,),
        in_specs=[row(d), _const_spec((1, d)), per_b, per_b,
                  _resident_spec((d, _AB_COLS)), _const_spec((B_CONV, B_QKV_W)),
                  _const_spec(sel.shape), _const_spec(selt2.shape), _const_spec(selt3.shape),
                  _const_spec((tm, tm)), _const_spec((1, LANES)), _const_spec((1, LANES))],
        out_specs=[row(A_Q_W), row(2 * A_KV_W), row(2 * A_KV_W)] + [row(B_W)] * 6,
        out_shape=[jax.ShapeDtypeStruct((n, A_Q_W), BF16),
                   jax.ShapeDtypeStruct((n, 2 * A_KV_W), BF16),
                   jax.ShapeDtypeStruct((n, 2 * A_KV_W), BF16)]
        + [jax.ShapeDtypeStruct((n, B_W), F32)] * 6,
        scratch_shapes=[pltpu.VMEM((SUBLANES, B_QKV_W), F32)],
        compiler_params=_cparams("arbitrary"),
        name="in_proj0",
    )(x2d, nw.reshape(1, d), sc, sh, w, conv_w, sel, selt2, selt3, tril, pad8(a_log), pad8(dt_bias))
    return outs


_ATTN_BLOCKS = 2


def _attn_kernel(sink_ref, q_ref, kc_ref, vc_ref, bm_ref, o_ref, kp_ref, vp_ref, *, steps_per_seq):
    i = pl.program_id(0)
    first = (i % steps_per_seq) == 0
    w = WINDOW

    @pl.when(i == 0)
    def _():
        kp_ref[...] = jnp.zeros_like(kp_ref)
        vp_ref[...] = jnp.zeros_like(vp_ref)

    lane = lax.broadcasted_iota(jnp.int32, (w, LANES), 1)
    low = lane < HEAD_DIM
    col = lax.broadcasted_iota(jnp.int32, (2 * w, 2 * w), 1)
    row = lax.broadcasted_iota(jnp.int32, (2 * w, 1), 0)
    prev_dead = jnp.logical_and(first, col < w)
    zero = jnp.zeros((), q_ref.dtype)
    pairs = A_Q_HEADS // 2
    units = [(b, j) for b in range(_ATTN_BLOCKS) for j in range(pairs)]

    def keys(p_ref, c_ref, b, kh):
        ls = slice(kh * LANES, (kh + 1) * LANES)
        before = p_ref[:, ls] if b == 0 else c_ref[(b - 1) * w:b * w, ls]
        return jnp.concatenate([before, c_ref[b * w:(b + 1) * w, ls]], axis=0)

    kv_of = lambda j: (2 * j) // (A_Q_HEADS // A_KV_HEADS)
    qp = [q_ref[b * w:(b + 1) * w, j * LANES:(j + 1) * LANES] for b, j in units]
    qs = [jnp.concatenate([jnp.where(low, t, zero), jnp.where(low, zero, t)], axis=0) for t in qp]
    kd = [keys(kp_ref, kc_ref, b, kv_of(j)) for b, j in units]
    vd = [keys(vp_ref, vc_ref, b, kv_of(j)) for b, j in units]
    s = [lax.dot_general(a, k, (((1,), (1,)), ((), ())), preferred_element_type=F32) for a, k in zip(qs, kd)]
    s = [t * (HEAD_DIM ** -0.5) + bm_ref[j] for t, (b, j) in zip(s, units)]
    s = [jnp.where(prev_dead, NEG_BIG, t) if b == 0 else t for t, (b, j) in zip(s, units)]
    sink = [jnp.where(row < w, sink_ref[2 * j], sink_ref[2 * j + 1]) for b, j in units]
    m = [jnp.maximum(jnp.max(t, axis=-1, keepdims=True), sk) for t, sk in zip(s, sink)]
    p = [jnp.exp(t - mt) for t, mt in zip(s, m)]
    denom = [jnp.sum(t, axis=-1, keepdims=True) + jnp.exp(sk - mt) for t, sk, mt in zip(p, sink, m)]
    pv = [jnp.dot(t.astype(BF16), v, preferred_element_type=F32) / dn for t, v, dn in zip(p, vd, denom)]
    outs = [jnp.where(low, t[:w], t[w:]) for t in pv]
    for b in range(_ATTN_BLOCKS):
        o_ref[b * w:(b + 1) * w, :] = jnp.concatenate(outs[b * pairs:(b + 1) * pairs], axis=1).astype(o_ref.dtype)
    kp_ref[...] = kc_ref[(_ATTN_BLOCKS - 1) * w:, :]
    vp_ref[...] = vc_ref[(_ATTN_BLOCKS - 1) * w:, :]


def _attention(qa, kd, vd, bias_tbl, sinks, seq_len):
    n = qa.shape[0]
    w = WINDOW
    rows = _ATTN_BLOCKS * w
    steps = seq_len // rows
    cur = lambda i: (i, 0)
    return pl.pallas_call(
        functools.partial(_attn_kernel, steps_per_seq=steps),
        grid=(n // rows,),
        in_specs=[pl.BlockSpec(memory_space=pltpu.SMEM),
                  pl.BlockSpec((rows, A_Q_W), cur),
                  pl.BlockSpec((rows, 2 * A_KV_W), cur), pl.BlockSpec((rows, 2 * A_KV_W), cur),
                  _const_spec((A_Q_HEADS // 2, 2 * w, 2 * w))],
        out_specs=pl.BlockSpec((rows, A_Q_W), cur),
        out_shape=jax.ShapeDtypeStruct((n, A_Q_W), BF16),
        scratch_shapes=[pltpu.VMEM((w, 2 * A_KV_W), BF16), pltpu.VMEM((w, 2 * A_KV_W), BF16)],
        compiler_params=_cparams("arbitrary"),
        name="swa_attention",
    )(sinks, qa, kd, vd, bias_tbl)


_DN_PAIRS = B_HEADS // 2
_DN_INV_BLOCK = 16
_DN_GROUP = 4


def _block_diag(x, low):
    zero = jnp.zeros((), x.dtype)
    return jnp.concatenate([jnp.where(low, x, zero), jnp.where(low, zero, x)], axis=0)


def _dn_intra(chunks, data_refs, work_refs, consts):
    qn_ref, kn_ref, vb_ref, bexp_ref, gcexp_ref = data_refs
    u_ref, w_ref, qk_ref, qd_ref, kd_ref, egl_ref = work_refs
    low, i_idx, j_idx, ones3 = consts
    c = CHUNK
    units = [(ci, p) for ci in chunks for p in range(_DN_PAIRS)]
    where = [(slice(ci * c, (ci + 1) * c), slice(p * LANES, (p + 1) * LANES)) for ci, p in units]
    causal = i_idx >= j_idx
    strict = i_idx > j_idx
    on_diag = i_idx == j_idx
    eye = on_diag.astype(F32)
    blk_shift = int(math.log2(_DN_INV_BLOCK))
    same_blk = (i_idx >> blk_shift) == (j_idx >> blk_shift)

    q = [qn_ref[rs, ls] for rs, ls in where]
    k = [kn_ref[rs, ls] for rs, ls in where]
    v = [vb_ref[rs, ls] for rs, ls in where]
    b = [bexp_ref[rs, ls] for rs, ls in where]
    gc = [gcexp_ref[rs, ls] for rs, ls in where]

    gr = [jnp.dot(ones3, jnp.concatenate(_split(jnp.where(on_diag, t, 0.0), 3), axis=0),
                  preferred_element_type=F32) for t in gc]
    ks = [_block_diag(t.astype(BF16), low) for t in k]
    qkk = [lax.dot_general(jnp.concatenate([qt, kt], axis=0).astype(BF16), kst,
                           (((1,), (1,)), ((), ())), preferred_element_type=F32)
           for qt, kt, kst in zip(q, k, ks)]
    decay = [jnp.exp(jnp.where(causal, gct - grt, NEG_BIG)) for gct, grt in zip(gc, gr)]
    lmat = [jnp.where(strict, bt * t[c:] * dt, 0.0) for bt, t, dt in zip(b, qkk, decay)]
    qk = [jnp.where(causal, t[:c] * dt, 0.0) for t, dt in zip(qkk, decay)]

    def mm(xs, ys):
        return [_bdot(x, _block_diag(y.astype(BF16), low)) for x, y in zip(xs, ys)]

    l_diag = [jnp.where(same_blk, t, 0.0) for t in lmat]
    l_off = [t - d for t, d in zip(lmat, l_diag)]
    pw = [-t for t in l_diag]
    d_inv = [eye + t for t in pw]
    for _ in range(blk_shift - 1):
        pw = mm(pw, pw)
        d_inv = mm(d_inv, [eye + t for t in pw])
    pw = [-t for t in mm(d_inv, l_off)]
    acc = [eye + t for t in pw]
    for _ in range(int(math.log2(c // _DN_INV_BLOCK)) - 1):
        pw = mm(pw, pw)
        acc = mm(acc, [eye + t for t in pw])
    tmat = mm(acc, d_inv)

    egc = [jnp.exp(t) for t in gc]
    rhs = [jnp.concatenate([_block_diag((vt * bt).astype(BF16), low),
                            _block_diag((kt * (bt * et)).astype(BF16), low)], axis=1)
           for vt, kt, bt, et in zip(v, k, b, egc)]
    uw = [_bdot(t, r) for t, r in zip(tmat, rhs)]
    for n, (ci, p) in enumerate(units):
        g_last = gc[n][c - 1:c, :]
        u_ref[ci, p] = uw[n][:, :LANES]
        w_ref[ci, p] = uw[n][:, LANES:]
        qk_ref[ci, p] = qk[n]
        qd_ref[ci, p] = q[n] * egc[n]
        kd_ref[ci, p] = k[n] * jnp.exp(g_last - gc[n])
        egl_ref[ci, p] = jnp.broadcast_to(jnp.exp(g_last), (SUBLANES, LANES))


def _dn_scan(ci, work_refs, s_ref, gs_ref, nw, o_ref, consts):
    u_ref, w_ref, qk_ref, qd_ref, kd_ref, egl_ref = work_refs
    low, mask_bd, head_mean2 = consts
    c = CHUNK
    rows = slice(ci * c, (ci + 1) * c)
    pairs = range(_DN_PAIRS)
    s_old = [s_ref[p] for p in pairs]
    wq = [_bdot(jnp.concatenate([w_ref[ci, p], qd_ref[ci, p]], axis=0), s_old[p]) for p in pairs]
    v_new = [u_ref[ci, p] - wq[p][:c] for p in pairs]
    o = [wq[p][c:] + _bdot(qk_ref[ci, p], _block_diag(v_new[p].astype(BF16), low)) for p in pairs]
    kv = [_bdot_tn(kd_ref[ci, p], v_new[p]) for p in pairs]
    for p in pairs:
        s_ref[p] = s_old[p] * egl_ref[ci, p][0:1, :] + jnp.where(mask_bd, kv[p], 0.0)
    ms = [jnp.dot(jnp.concatenate(_split(t * t, 2), axis=1), head_mean2, preferred_element_type=F32)
          for t in o]
    for p in pairs:
        ls = slice(p * LANES, (p + 1) * LANES)
        y = (o[p] * lax.rsqrt(ms[p] + EPS)) * nw * gs_ref[rows, ls]
        o_ref[rows, ls] = y.astype(o_ref.dtype)


def _dn_kernel(qn_ref, kn_ref, vb_ref, gs_ref, bexp_ref, gcexp_ref, nw_ref, o_ref,
               s_ref, u_ref, w_ref, qk_ref, qd_ref, kd_ref, egl_ref, *, groups_per_seq):
    i = pl.program_id(0)

    @pl.when(i % groups_per_seq == 0)
    def _():
        s_ref[...] = jnp.zeros_like(s_ref)

    c = CHUNK
    tm = o_ref.shape[0]
    n_chunks = tm // c
    lane = lax.broadcasted_iota(jnp.int32, (c, LANES), 1)
    low = lane < HEAD_DIM
    i_idx = lax.broadcasted_iota(jnp.int32, (c, LANES), 0)
    j_idx = lane & (c - 1)
    ones3 = jnp.ones((c, 3 * c), BF16)
    rb = lax.broadcasted_iota(jnp.int32, (LANES, LANES), 0)
    cb = lax.broadcasted_iota(jnp.int32, (LANES, LANES), 1)
    mask_bd = (rb < HEAD_DIM) == (cb < HEAD_DIM)
    head_mean = jnp.where(mask_bd, 1.0 / HEAD_DIM, 0.0).astype(BF16)
    head_mean2 = jnp.concatenate([head_mean, head_mean], axis=0)
    data_refs = (qn_ref, kn_ref, vb_ref, bexp_ref, gcexp_ref)
    work_refs = (u_ref, w_ref, qk_ref, qd_ref, kd_ref, egl_ref)
    intra_consts = (low, i_idx, j_idx, ones3)
    scan_consts = (low, mask_bd, head_mean2)
    nw = nw_ref[...]

    groups = [list(range(s, s + _DN_GROUP)) for s in range(0, n_chunks, _DN_GROUP)]
    _dn_intra(groups[0], data_refs, work_refs, intra_consts)
    for j, grp in enumerate(groups):
        if j + 1 < len(groups):
            _dn_intra(groups[j + 1], data_refs, work_refs, intra_consts)
        for ci in grp:
            _dn_scan(ci, work_refs, s_ref, gs_ref, nw, o_ref, scan_consts)


def _deltanet(qn, kn, vb, gs, bexp, gcexp, norm_w, seq_len):
    n = qn.shape[0]
    tm = TOKEN_TILE
    nw2 = jnp.concatenate([norm_w, norm_w]).reshape(1, LANES)
    row = lambda width: pl.BlockSpec((tm, width), lambda i: (i, 0))
    return pl.pallas_call(
        functools.partial(_dn_kernel, groups_per_seq=seq_len // tm),
        grid=(n // tm,),
        in_specs=[row(B_W)] * 6 + [_const_spec((1, LANES))],
        out_specs=row(B_W),
        out_shape=jax.ShapeDtypeStruct((n, B_W), BF16),
        scratch_shapes=[pltpu.VMEM((_DN_PAIRS, LANES, LANES), F32)]
        + [pltpu.VMEM((tm // CHUNK, _DN_PAIRS, CHUNK, LANES), F32)] * 5
        + [pltpu.VMEM((tm // CHUNK, _DN_PAIRS, SUBLANES, LANES), F32)],
        compiler_params=_cparams("arbitrary"),
        name="gated_deltanet",
    )(qn, kn, vb, gs, bexp, gcexp, nw2)


def _resident_spec(shape):
    nd = len(shape)
    return pl.BlockSpec(shape, lambda *_: (0,) * nd, pipeline_mode=pl.Buffered(1))


def _mid0_kernel(attn_ref, dn_ref, x_ref, wo_ref, g1_ref, nw_ref, sc_ref, sh_ref, g2_ref,
                 wg_ref, wu_ref, wd_ref, o_ref):
    mix = (jnp.dot(attn_ref[...], wo_ref[:A_Q_W], preferred_element_type=F32)
           + jnp.dot(dn_ref[...], wo_ref[A_Q_W:], preferred_element_type=F32))
    x1 = x_ref[...] + g1_ref[0] * mix
    hn = _norm_mod(x1, nw_ref[...], sc_ref[0], sh_ref[0]).astype(BF16)
    hg = jnp.dot(hn, wg_ref[...], preferred_element_type=F32)
    hu = jnp.dot(hn, wu_ref[...], preferred_element_type=F32)
    act = (_silu(hg) * hu).astype(BF16)
    o_ref[...] = x1 + g2_ref[0] * jnp.dot(act, wd_ref[...], preferred_element_type=F32)


def _mid0(attn, dn, x2d, w_out, g1, nw, sc, sh, g2, wg, wu, wd, seq_len):
    n, d = x2d.shape
    tm = TOKEN_TILE
    tps = seq_len // tm
    row = lambda width: pl.BlockSpec((tm, width), lambda i: (i, 0))
    per_b = pl.BlockSpec((1, 1, d), lambda i: (i // tps, 0, 0))
    return pl.pallas_call(
        _mid0_kernel,
        grid=(n // tm,),
        in_specs=[row(A_Q_W), row(B_W), row(d), _resident_spec(w_out.shape), per_b,
                  _const_spec((1, d)), per_b, per_b, per_b,
                  _resident_spec(wg.shape), _resident_spec(wu.shape), _resident_spec(wd.shape)],
        out_specs=row(d),
        out_shape=jax.ShapeDtypeStruct((n, d), F32),
        compiler_params=_cparams("parallel"),
        name="out_proj0_swiglu",
    )(attn, dn, x2d, w_out.astype(BF16), g1, nw.reshape(1, d), sc, sh, g2,
      wg.astype(BF16), wu.astype(BF16), wd.astype(BF16))


def _gelu_tanh(x):
    return 0.5 * x * (1.0 + jnp.tanh(math.sqrt(2.0 / math.pi) * (x + 0.044715 * (x * x * x))))


def _linear_scan(a, b, h0):
    n, width = a.shape
    groups = n // SUBLANES
    a = a.reshape(groups, SUBLANES, width)
    b = b.reshape(groups, SUBLANES, width)
    in_group = lax.broadcasted_iota(jnp.int32, a.shape, 1)
    s = 1
    while s < SUBLANES:
        a_sh = pltpu.roll(a, s, 1)
        b_sh = pltpu.roll(b, s, 1)
        valid = in_group >= s
        b = jnp.where(valid, a * b_sh + b, b)
        a = jnp.where(valid, a * a_sh, a)
        s *= 2
    carry = jnp.broadcast_to(h0, (SUBLANES, width))
    out = []
    for g in range(groups):
        hg = a[g] * carry + b[g]
        out.append(hg)
        carry = jnp.broadcast_to(hg[SUBLANES - 1:SUBLANES, :], hg.shape)
    return jnp.concatenate(out, axis=0)


def _mix1_tile(x, nw, sc, sh, w_ref, cw_ref, cb_ref, ga_ref, gab_ref, gx_ref, gxb_ref, lam_ref, sw_ref,
               tail_c_ref, tail_d_ref, h_ref):
    hn = _norm_mod(x, nw, sc, sh, on_mxu=True).astype(BF16)
    proj = jnp.dot(hn, w_ref[...], preferred_element_type=F32)
    w_l = LRU_WIDTH
    xc_in = proj[:, :w_l]
    yc = proj[:, w_l:2 * w_l]
    bd = proj[:, 2 * w_l:2 * w_l + SC_WIDTH]
    cd = proj[:, 2 * w_l + SC_WIDTH:2 * w_l + 2 * SC_WIDTH]
    hd = proj[:, 2 * w_l + 2 * SC_WIDTH:]
    tm = xc_in.shape[0]

    kc = cw_ref.shape[0]
    tail = tail_c_ref[...]
    cw = cw_ref[...]
    xc = xc_in * cw[kc - 1:kc] + cb_ref[...]
    for k in range(1, kc):
        xc = xc + _shift_rows(xc_in, k, tail) * cw[kc - 1 - k:kc - k]
    tail_c_ref[...] = xc_in[tm - SUBLANES:]

    xb = xc.astype(BF16)
    gw = ga_ref.shape[1]
    ra, ri = [], []
    for p in range(ga_ref.shape[0]):
        xin = xb[:, p * gw:(p + 1) * gw]
        ra.append(jnp.dot(xin, ga_ref[p], preferred_element_type=F32))
        ri.append(jnp.dot(xin, gx_ref[p], preferred_element_type=F32))
    r = _sigmoid(jnp.concatenate(ra, axis=1) + gab_ref[...])
    ig = _sigmoid(jnp.concatenate(ri, axis=1) + gxb_ref[...])
    log_a = (-LRU_C) * r * _softplus(-lam_ref[...])
    a = jnp.exp(log_a)
    one_m_a2 = -jnp.tanh(log_a) * (a * a + 1.0)
    root = jnp.where(one_m_a2 > 0.0, one_m_a2 * lax.rsqrt(one_m_a2), 0.0)
    b = root * (ig * xc)
    h = _linear_scan(a, b, h_ref[0:1, :])
    h_ref[...] = jnp.broadcast_to(h[tm - 1:tm, :], h_ref.shape)
    yc_out = h * _gelu_tanh(yc)

    ks = sw_ref.shape[0]
    ch = cd * hd
    tail_d = tail_d_ref[...]
    sw = sw_ref[...]
    conv = ch * sw[ks - 1:ks]
    for k in range(1, ks):
        conv = conv + _shift_rows(ch, k, tail_d) * sw[ks - 1 - k:ks - k]
    tail_d_ref[...] = ch[tm - SUBLANES:]
    return jnp.concatenate([yc_out, bd * conv], axis=1)


def _pair_block_diag(gw):
    nb, bw, _ = gw.shape
    g2 = gw.reshape(nb // 2, 2, bw, bw)
    z = jnp.zeros((nb // 2, bw, bw), gw.dtype)
    top = jnp.concatenate([g2[:, 0], z], axis=2)
    bot = jnp.concatenate([z, g2[:, 1]], axis=2)
    return jnp.concatenate([top, bot], axis=1).astype(BF16)


def _route_tile(cat, x, wo_ref, g1, nw, sc, sh, rw_ref, rb_ref, carry_ref):
    x3 = x + g1 * jnp.dot(cat, wo_ref[...], preferred_element_type=F32)
    hn = _norm_mod(x3, nw, sc, sh)
    tm = hn.shape[0]
    lane = lax.broadcasted_iota(jnp.int32, (tm, LANES), 1)
    h_hi, h_lo = _split(hn, 2)
    both = jnp.dot(h_hi, rw_ref[...], preferred_element_type=F32)
    logits = (both[:, :LANES] + both[:, LANES:]
              + jnp.dot(h_lo, rw_ref[:, :LANES], preferred_element_type=F32) + rb_ref[...])
    lg = jnp.where(lane < N_EXPERTS, logits, NEG_BIG)
    m1 = jnp.max(lg, axis=1, keepdims=True)
    i1 = jnp.min(jnp.where(lg == m1, lane, LANES), axis=1, keepdims=True)
    lg2 = jnp.where(lane == i1, NEG_BIG, lg)
    m2 = jnp.max(lg2, axis=1, keepdims=True)
    i2 = jnp.min(jnp.where(lg2 == m2, lane, LANES), axis=1, keepdims=True)
    e2 = jnp.exp(m2 - m1)
    w1 = 1.0 / (1.0 + e2)
    w2 = e2 / (1.0 + e2)

    hit1 = lane == i1
    hit2 = lane == i2
    sel = jnp.logical_or(hit1, hit2).astype(F32)
    r_i = lax.broadcasted_iota(jnp.int32, (tm, tm), 0)
    c_i = lax.broadcasted_iota(jnp.int32, (tm, tm), 1)
    tril = (r_i >= c_i).astype(BF16)
    incl = jnp.dot(tril, sel.astype(BF16), preferred_element_type=F32)
    carry = carry_ref[0:1, :]
    excl = incl - sel + carry
    r1 = jnp.sum(jnp.where(hit1, excl, 0.0), axis=1, keepdims=True)
    r2 = jnp.sum(jnp.where(hit2, excl, 0.0), axis=1, keepdims=True)
    total = carry + incl[tm - 1:tm, :]
    carry_ref[...] = jnp.broadcast_to(total, carry_ref.shape)

    meta = jnp.where(lane == 0, i1, 0)
    meta = jnp.where(lane == 1, i2, meta)
    meta = jnp.where(lane == 2, r1.astype(jnp.int32), meta)
    meta = jnp.where(lane == 3, r2.astype(jnp.int32), meta)
    wt = jnp.where(lane == 0, w1, jnp.where(lane == 1, w2, 0.0))
    return x3, hn, meta, wt, carry, total


def _mix1_kernel(x_ref, nw_ref, sc_ref, sh_ref, w_ref, cw_ref, cb_ref, ga_ref, gab_ref, gx_ref, gxb_ref,
                 lam_ref, sw_ref, o_ref, tail_c_ref, tail_d_ref, h_ref, *, tiles_per_seq):
    i = pl.program_id(0)

    @pl.when(i % tiles_per_seq == 0)
    def _():
        tail_c_ref[...] = jnp.zeros_like(tail_c_ref)
        tail_d_ref[...] = jnp.zeros_like(tail_d_ref)
        h_ref[...] = jnp.zeros_like(h_ref)

    cat = _mix1_tile(x_ref[...], nw_ref[...], sc_ref[0], sh_ref[0], w_ref, cw_ref, cb_ref, ga_ref,
                     gab_ref, gx_ref, gxb_ref, lam_ref, sw_ref, tail_c_ref, tail_d_ref, h_ref)
    o_ref[...] = cat.astype(o_ref.dtype)


def _mix1(x2d, nw, sc, sh, w_in, conv_w, conv_b, ga_w, ga_b, gx_w, gx_b, lam, sconv_w, seq_len):
    n, d = x2d.shape
    tm = TOKEN_TILE
    tps = seq_len // tm
    cd_in = w_in.shape[1]
    cd_out = LRU_WIDTH + SC_WIDTH
    row = lambda width: pl.BlockSpec((tm, width), lambda i: (i, 0))
    per_b = pl.BlockSpec((1, 1, d), lambda i: (i // tps, 0, 0))
    ga = _pair_block_diag(ga_w)
    gx = _pair_block_diag(gx_w)
    vec = lambda v: v.reshape(1, -1)
    return pl.pallas_call(
        functools.partial(_mix1_kernel, tiles_per_seq=tps),
        grid=(n // tm,),
        in_specs=[row(d), _const_spec((1, d)), per_b, per_b, _resident_spec((d, cd_in)),
                  _const_spec(conv_w.shape), _const_spec((1, LRU_WIDTH)),
                  _const_spec(ga.shape), _const_spec((1, LRU_WIDTH)),
                  _const_spec(gx.shape), _const_spec((1, LRU_WIDTH)),
                  _const_spec((1, LRU_WIDTH)), _const_spec(sconv_w.shape)],
        out_specs=row(cd_out),
        out_shape=jax.ShapeDtypeStruct((n, cd_out), BF16),
        scratch_shapes=[pltpu.VMEM((SUBLANES, LRU_WIDTH), F32), pltpu.VMEM((SUBLANES, SC_WIDTH), F32),
                        pltpu.VMEM((SUBLANES, LRU_WIDTH), F32)],
        compiler_params=_cparams("arbitrary"),
        name="rglru_shortconv_mixer",
    )(x2d, vec(nw), sc, sh, w_in.astype(BF16), conv_w, vec(conv_b), ga, vec(ga_b), gx, vec(gx_b),
      vec(lam), sconv_w)


def _route_kernel(cat_ref, x_ref, wo_ref, g1_ref, nw_ref, sc_ref, sh_ref, rw_ref, rb_ref,
                  x3_ref, hn_ref, metat_ref, meta_ref, wt_ref, base_ref, cnt_ref, carry_ref):
    @pl.when(pl.program_id(0) == 0)
    def _():
        carry_ref[...] = jnp.zeros_like(carry_ref)

    x3, hn, meta, wt, before, total = _route_tile(cat_ref[...], x_ref[...], wo_ref, g1_ref[0], nw_ref[...],
                                                  sc_ref[0], sh_ref[0], rw_ref, rb_ref, carry_ref)
    x3_ref[...] = x3
    hn_ref[...] = hn
    meta_ref[...] = meta
    metat_ref[...] = jnp.transpose(meta.astype(F32))[:SUBLANES].astype(jnp.int32)
    wt_ref[...] = wt
    base_ref[0] = jnp.broadcast_to(before, base_ref.shape[1:]).astype(jnp.int32)
    cnt_ref[...] = jnp.broadcast_to(total, cnt_ref.shape).astype(jnp.int32)


def _route(cat, x2d, w_out, g1, nw, sc, sh, router_w, router_b, seq_len):
    n, d = x2d.shape
    tm = TOKEN_TILE
    tps = seq_len // tm
    row = lambda width: pl.BlockSpec((tm, width), lambda i: (i, 0))
    per_b = pl.BlockSpec((1, 1, d), lambda i: (i // tps, 0, 0))
    rw = jnp.zeros((d, LANES), F32).at[:, :N_EXPERTS].set(router_w)
    rw_hi = rw.astype(BF16)
    rw = jnp.concatenate([rw_hi, (rw - rw_hi.astype(F32)).astype(BF16)], axis=1)
    rb = jnp.zeros((1, LANES), F32).at[0, :N_EXPERTS].set(router_b)
    return pl.pallas_call(
        _route_kernel,
        grid=(n // tm,),
        in_specs=[row(cat.shape[1]), row(d), _resident_spec(w_out.shape), per_b, _const_spec((1, d)),
                  per_b, per_b, _const_spec((d, 2 * LANES)), _const_spec((1, LANES))],
        out_specs=[row(d), row(d), pl.BlockSpec((SUBLANES, tm), lambda i: (0, i)), row(LANES), row(LANES),
                   pl.BlockSpec((1, SUBLANES, LANES), lambda i: (i, 0, 0)), _const_spec((SUBLANES, LANES))],
        out_shape=[jax.ShapeDtypeStruct((n, d), F32), jax.ShapeDtypeStruct((n, d), F32),
                   jax.ShapeDtypeStruct((SUBLANES, n), jnp.int32), jax.ShapeDtypeStruct((n, LANES), jnp.int32),
                   jax.ShapeDtypeStruct((n, LANES), F32),
                   jax.ShapeDtypeStruct((n // tm, SUBLANES, LANES), jnp.int32),
                   jax.ShapeDtypeStruct((SUBLANES, LANES), jnp.int32)],
        scratch_shapes=[pltpu.VMEM((SUBLANES, LANES), F32)],
        compiler_params=_cparams("arbitrary"),
        name="out_proj1_router",
    )(cat, x2d, w_out.astype(BF16), g1, nw.reshape(1, d), sc, sh, rw, rb)


def _local_rows(tr):
    return 2 * tr + N_EXPERTS * SUBLANES


def _xs_rows(n):
    worst = 2 * n + (n // TOKEN_TILE) * N_EXPERTS * (SUBLANES - 1)
    return (-(-worst // MOE_TILE) + N_EXPERTS) * MOE_TILE


def _local_pos(e_k, r_k, delta_ref, tile):
    shift = jnp.zeros_like(r_k)
    for e in range(N_EXPERTS):
        shift = jnp.where(e_k == e, delta_ref[tile * N_EXPERTS + e], shift)
    return r_k + shift


_RUN_PIECES = (64, 16, SUBLANES)


def _for_each_group(tile, lstart_ref, run_ref, gstart_ref, fn):
    for e in range(N_EXPERTS):
        k = tile * N_EXPERTS + e
        l_start = lstart_ref[k]
        g_start = gstart_ref[k]
        done = 0
        for rows in _RUN_PIECES:
            shift = int(math.log2(rows))
            count = (run_ref[k] - done) >> shift

            def piece(g, c, rows=rows, done=done, l_start=l_start, g_start=g_start):
                off = done + g * rows
                fn(pl.multiple_of(l_start + off, SUBLANES), pl.multiple_of(g_start + off, SUBLANES), rows)
                return c

            lax.fori_loop(0, count, piece, 0)
            done = done + (count << shift)


def _zero_fill_gaps(gap_ref, used_ref, xs_ref, zero_ref, sem):
    zero_ref[...] = jnp.zeros_like(zero_ref)
    zr = zero_ref.shape[0]
    per_tile = MOE_TILE // zr
    shift = int(math.log2(SUBLANES))

    def gap_copy(e, g):
        row = pl.multiple_of(gap_ref[e] + g * SUBLANES, SUBLANES)
        return pltpu.make_async_copy(zero_ref.at[pl.ds(0, SUBLANES)], xs_ref.at[pl.ds(row, SUBLANES)], sem)

    def tile_copy(k):
        row = pl.multiple_of(k * zr, zr)
        return pltpu.make_async_copy(zero_ref, xs_ref.at[pl.ds(row, zr)], sem)

    def both(op):
        for e in range(N_EXPERTS):
            lax.fori_loop(0, gap_ref[N_EXPERTS + e] >> shift, lambda g, c, e=e: (op(gap_copy(e, g)), c)[1], 0)
        lax.fori_loop(used_ref[0] * per_tile, (xs_ref.shape[0] // MOE_TILE) * per_tile,
                      lambda k, c: (op(tile_copy(k)), c)[1], 0)

    both(lambda cp: cp.start())
    both(lambda cp: cp.wait())


def _dispatch_kernel(delta_ref, lstart_ref, run_ref, gstart_ref, gap_ref, used_ref, hn_ref, meta_ref, xs_ref,
                     sbuf_ref, zero_ref, sem):
    j = pl.program_id(0)
    tr = hn_ref.shape[0]
    lrows = sbuf_ref.shape[1]
    slot = lax.rem(j, 2)

    @pl.when(j == 0)
    def _():
        _zero_fill_gaps(gap_ref, used_ref, xs_ref, zero_ref, sem.at[0])

    meta = meta_ref[...]
    lp1 = _local_pos(meta[0:1], meta[2:3], delta_ref, j)
    lp2 = _local_pos(meta[1:2], meta[3:4], delta_ref, j)
    r_idx = lax.broadcasted_iota(jnp.int32, (lrows, tr), 0)
    onehot = jnp.logical_or(r_idx == lp1, r_idx == lp2).astype(BF16)
    sbuf_ref[slot] = jnp.dot(onehot, hn_ref[...].astype(BF16), preferred_element_type=F32)

    def group_copy(buf, local_row, xs_row, rows):
        return pltpu.make_async_copy(sbuf_ref.at[buf, pl.ds(local_row, rows)],
                                     xs_ref.at[pl.ds(xs_row, rows)], sem.at[buf])

    def drain(tile, buf):
        _for_each_group(tile, lstart_ref, run_ref, gstart_ref,
                        lambda lr, xr, rows: group_copy(buf, lr, xr, rows).wait())

    _for_each_group(j, lstart_ref, run_ref, gstart_ref,
                    lambda lr, xr, rows: group_copy(slot, lr, xr, rows).start())

    @pl.when(j > 0)
    def _():
        drain(j - 1, 1 - slot)

    @pl.when(j == pl.num_programs(0) - 1)
    def _():
        drain(j, slot)


def _dispatch(hn, meta_t, tables, gaps, used_tiles):
    n, d = hn.shape
    tr = TOKEN_TILE
    lrows = _local_rows(tr)
    return pl.pallas_call(
        _dispatch_kernel,
        grid_spec=pltpu.PrefetchScalarGridSpec(
            num_scalar_prefetch=6,
            grid=(n // tr,),
            in_specs=[pl.BlockSpec((tr, d), lambda j, *_: (j, 0)),
                      pl.BlockSpec((SUBLANES, tr), lambda j, *_: (0, j))],
            out_specs=pl.BlockSpec(memory_space=pl.ANY),
            scratch_shapes=[pltpu.VMEM((2, lrows, d), F32), pltpu.VMEM((MOE_SUB, d), F32),
                            pltpu.SemaphoreType.DMA((2,))]),
        out_shape=jax.ShapeDtypeStruct((_xs_rows(n), d), F32),
        compiler_params=_cparams("arbitrary"),
        name="moe_dispatch",
    )(*tables, gaps, used_tiles, hn, meta_t)


def _moe_kernel(te_ref, hi_ref, x_ref, wg_ref, wu_ref, wd_ref, o_ref, xb_ref):
    w = pl.program_id(0)
    f = pl.program_id(1)
    tm = x_ref.shape[0]
    sub = MOE_SUB
    sub_shift = int(math.log2(sub))
    hi = hi_ref[w]

    def swiglu_part(xb, wg, wu, wd):
        hg = jnp.dot(xb, wg, preferred_element_type=F32)
        hu = jnp.dot(xb, wu, preferred_element_type=F32)
        act = (_silu(hg) * hu).astype(BF16)
        return jnp.dot(act, wd, preferred_element_type=F32)

    @pl.when(jnp.logical_and(f == 0, hi > 0))
    def _():
        row = lax.broadcasted_iota(jnp.int32, (tm, 1), 0)
        xb_ref[...] = jnp.where(row < hi, x_ref[...], 0.0).astype(BF16)

    @pl.when(hi == tm)
    def _():
        part = swiglu_part(xb_ref[...], wg_ref[0].astype(BF16), wu_ref[0].astype(BF16),
                           wd_ref[0].astype(BF16))

        @pl.when(f == 0)
        def _():
            o_ref[...] = part

        @pl.when(f != 0)
        def _():
            o_ref[...] += part

    @pl.when(hi < tm)
    def _():
        @pl.when(f == 0)
        def _():
            o_ref[...] = jnp.zeros_like(o_ref)

        def sub_block(s, carry):
            rows = pl.ds(pl.multiple_of(s * sub, sub), sub)
            o_ref[rows, :] += swiglu_part(xb_ref[rows, :], wg_ref[0].astype(BF16),
                                          wu_ref[0].astype(BF16), wd_ref[0].astype(BF16))
            return carry

        lax.fori_loop(0, (hi + sub - 1) >> sub_shift, sub_block, 0)


def _moe_ffn(xs, tile_expert, tile_rows, wg, wu, wd):
    rows, d = xs.shape
    tm = MOE_TILE
    tf = MOE_FF_TILE
    nf = wg.shape[2] // tf

    def f_idx(f, hi):
        v = (hi > 0).astype(jnp.int32)
        return f * v + (nf - 1) * (1 - v)

    return pl.pallas_call(
        _moe_kernel,
        grid_spec=pltpu.PrefetchScalarGridSpec(
            num_scalar_prefetch=2,
            grid=(rows // tm, nf),
            in_specs=[pl.BlockSpec((tm, d), lambda w, f, te, hi: (jnp.where(hi[w] > 0, w, 0), 0)),
                      pl.BlockSpec((1, d, tf), lambda w, f, te, hi: (te[w], 0, f_idx(f, hi[w]))),
                      pl.BlockSpec((1, d, tf), lambda w, f, te, hi: (te[w], 0, f_idx(f, hi[w]))),
                      pl.BlockSpec((1, tf, d), lambda w, f, te, hi: (te[w], f_idx(f, hi[w]), 0))],
            out_specs=pl.BlockSpec((tm, d), lambda w, f, te, hi: (w, 0)),
            scratch_shapes=[pltpu.VMEM((tm, d), BF16)]),
        out_shape=jax.ShapeDtypeStruct((rows, d), F32),
        compiler_params=_cparams("arbitrary", "arbitrary"),
        name="moe_expert_swiglu",
    )(tile_expert, tile_rows, xs, wg, wu, wd)


def _combine_kernel(delta_ref, lstart_ref, run_ref, gstart_ref, ys_ref, x_ref, meta_ref, wt_ref, g2_ref, fw_ref,
                    o_ref, ybuf_ref, sem):
    j = pl.program_id(0)
    n_tiles = pl.num_programs(0)
    tr = x_ref.shape[0]
    lrows = ybuf_ref.shape[1]
    slot = lax.rem(j, 2)

    def group_copy(buf, local_row, xs_row, rows):
        return pltpu.make_async_copy(ys_ref.at[pl.ds(xs_row, rows)],
                                     ybuf_ref.at[buf, pl.ds(local_row, rows)], sem.at[buf])

    def fetch(tile, buf):
        ybuf_ref[buf, 2 * tr:, :] = jnp.zeros((lrows - 2 * tr, ybuf_ref.shape[2]), F32)
        _for_each_group(tile, lstart_ref, run_ref, gstart_ref,
                        lambda lr, xr, rows: group_copy(buf, lr, xr, rows).start())

    @pl.when(j == 0)
    def _():
        fetch(0, 0)

    _for_each_group(j, lstart_ref, run_ref, gstart_ref,
                    lambda lr, xr, rows: group_copy(slot, lr, xr, rows).wait())

    @pl.when(j + 1 < n_tiles)
    def _():
        fetch(j + 1, 1 - slot)

    meta = meta_ref[...]
    wt = wt_ref[...]
    lp1 = _local_pos(meta[:, 0:1], meta[:, 2:3], delta_ref, j)
    lp2 = _local_pos(meta[:, 1:2], meta[:, 3:4], delta_ref, j)
    l_idx = lax.broadcasted_iota(jnp.int32, (tr, lrows), 1)
    pick = jnp.where(l_idx == lp1, wt[:, 0:1], 0.0) + jnp.where(l_idx == lp2, wt[:, 1:2], 0.0)
    ffn = _bdot(pick, ybuf_ref[slot])
    x4 = x_ref[...] + g2_ref[0] * ffn
    o_ref[...] = (x4 * _rms_scale(x4)) * fw_ref[...]


def _combine(ys, tables, x3, meta, wt, g2, final_w, seq_len):
    n, d = x3.shape
    tr = TOKEN_TILE
    tps = seq_len // tr
    lrows = -(-_local_rows(tr) // LANES) * LANES
    return pl.pallas_call(
        _combine_kernel,
        grid_spec=pltpu.PrefetchScalarGridSpec(
            num_scalar_prefetch=4,
            grid=(n // tr,),
            in_specs=[pl.BlockSpec(memory_space=pl.ANY),
                      pl.BlockSpec((tr, d), lambda j, *_: (j, 0)),
                      pl.BlockSpec((tr, LANES), lambda j, *_: (j, 0)),
                      pl.BlockSpec((tr, LANES), lambda j, *_: (j, 0)),
                      pl.BlockSpec((1, 1, d), lambda j, *_: (j // tps, 0, 0)),
                      pl.BlockSpec((1, d), lambda j, *_: (0, 0))],
            out_specs=pl.BlockSpec((tr, d), lambda j, *_: (j, 0)),
            scratch_shapes=[pltpu.VMEM((2, lrows, d), F32), pltpu.SemaphoreType.DMA((2,))]),
        out_shape=jax.ShapeDtypeStruct((n, d), F32),
        compiler_params=_cparams("arbitrary"),
        name="moe_combine_final_norm",
    )(*tables, ys, x3, meta, wt, g2, final_w.reshape(1, d))


def _moe_tables(tile_base, counts, n_tokens):
    i32 = lambda t: t.astype(jnp.int32)
    tm = MOE_TILE
    before = tile_base[:, 0, :N_EXPERTS]
    total = counts[0, :N_EXPERTS]
    run = jnp.concatenate([before[1:], total[None]], axis=0) - before
    run = (run + SUBLANES - 1) // SUBLANES * SUBLANES
    l_end = jnp.cumsum(run, axis=1)
    l_start = l_end - run
    g_size = jnp.sum(run, axis=0)
    g_tiles = (g_size + tm - 1) // tm
    tile_end = jnp.cumsum(g_tiles)
    g_off = (tile_end - g_tiles) * tm
    g_end = g_off + g_size
    g_start = g_off[None, :] + jnp.cumsum(run, axis=0) - run
    delta = l_start - before
    gaps = jnp.concatenate([g_end, tile_end * tm - g_end])
    used_tiles = tile_end[-1:]

    w = jnp.arange(_xs_rows(n_tokens) // tm, dtype=jnp.int32)
    te = jnp.minimum(jnp.sum((tile_end[None, :] <= w[:, None]).astype(jnp.int32), axis=1), N_EXPERTS - 1)
    rows = jnp.where(w < tile_end[-1], jnp.clip(jnp.take(g_end, te) - w * tm, 0, tm), 0)
    flat = lambda t: i32(t).reshape(-1)
    return (flat(delta), flat(l_start), flat(run), flat(g_start)), i32(gaps), i32(used_tiles), i32(te), i32(rows)


def kernel(x, c, rel_bias, ada_w, ada_b, norm_mix_w, norm_ffn_w, final_norm_w, ab_w_in, attn_sinks,
           dn_conv_w, dn_a_log, dn_dt_bias, dn_norm_w, ab_w_out, ffn_w_gate, ffn_w_up, ffn_w_down,
           cd_w_in, lru_conv_w, lru_conv_b, lru_gate_a_w, lru_gate_a_b, lru_gate_x_w, lru_gate_x_b,
           lru_lambda, sconv_w, cd_w_out, moe_router_w, moe_router_b, moe_w_gate, moe_w_up, moe_w_down):
    bsz, seq_len, d = x.shape
    n = bsz * seq_len
    x2d = x.reshape(n, d)
    mods = _ada_mods(c, ada_w, ada_b)

    sh1, sc1, g1, sh2, sc2, g2 = (mods[0, k] for k in range(6))
    qa, kd, vd, qn, kn, vb, gs, bexp, gcexp = _in_proj0(
        x2d, norm_mix_w[0], sc1, sh1, ab_w_in[0], dn_conv_w[0], dn_a_log[0], dn_dt_bias[0], seq_len)
    attn = _attention(qa, kd, vd, _bias_table(rel_bias), attn_sinks[0], seq_len)
    dn = _deltanet(qn, kn, vb, gs, bexp, gcexp, dn_norm_w[0], seq_len)
    x2 = _mid0(attn, dn, x2d, ab_w_out[0], g1, norm_ffn_w[0], sc2, sh2, g2,
               ffn_w_gate[0], ffn_w_up[0], ffn_w_down[0], seq_len)

    sh1, sc1, g1, sh2, sc2, g2 = (mods[1, k] for k in range(6))
    cat = _mix1(x2, norm_mix_w[1], sc1, sh1, cd_w_in[0], lru_conv_w[0], lru_conv_b[0],
                lru_gate_a_w[0], lru_gate_a_b[0], lru_gate_x_w[0], lru_gate_x_b[0],
                lru_lambda[0], sconv_w[0], seq_len)
    x3, hn4, meta_t, meta, wt, tile_base, counts = _route(
        cat, x2, cd_w_out[0], g1, norm_ffn_w[1], sc2, sh2, moe_router_w[0], moe_router_b[0], seq_len)
    tables, gaps, used_tiles, tile_expert, tile_rows = _moe_tables(tile_base, counts, n)
    xs = _dispatch(hn4, meta_t, tables, gaps, used_tiles)
    ys = _moe_ffn(xs, tile_expert, tile_rows, moe_w_gate[0], moe_w_up[0], moe_w_down[0])
    out = _combine(ys, tables, x3, meta, wt, g2, final_norm_w, seq_len)
    return out.reshape(bsz, seq_len, d)
```

```python
import functools
import math

import numpy as np
import jax
import jax.numpy as jnp
from jax import lax
from jax.experimental import pallas as pl
from jax.experimental.pallas import tpu as pltpu

D_MODEL = 1024
EPS = 1e-6
HEAD_DIM = 64
A_Q_HEADS = 8
A_KV_HEADS = 2
WINDOW = 128
N_BUCKETS = 32
MAX_DISTANCE = 128
B_HEADS = 8
B_CONV = 4
CHUNK = 64
A_Q_W = A_Q_HEADS * HEAD_DIM
A_KV_W = A_KV_HEADS * HEAD_DIM
B_W = B_HEADS * HEAD_DIM
B_QKV_W = 3 * B_W
LRU_WIDTH = D_MODEL
LRU_BLOCKS = 8
LRU_C = 8.0
SC_WIDTH = D_MODEL // 2
D_FF = 2816
N_EXPERTS = 8
D_FF_EXPERT = 3584

LANES = 128
SUBLANES = 8
VMEM_LIMIT_BYTES = 56 * 1024 * 1024
TOKEN_TILE = 512
MOE_TILE = 1024
MOE_SUB = 256
MOE_FF_TILE = 512
NEG_BIG = -1e30

F32 = jnp.float32
BF16 = jnp.bfloat16


def _cparams(*sem):
    return pltpu.CompilerParams(dimension_semantics=tuple(sem), vmem_limit_bytes=VMEM_LIMIT_BYTES)


def _const_spec(shape):
    nd = len(shape)
    return pl.BlockSpec(shape, lambda *_: (0,) * nd)


def _bdot(a, b):
    return jnp.dot(a.astype(BF16), b.astype(BF16), preferred_element_type=F32)


def _bdot_tn(a, b):
    return lax.dot_general(a.astype(BF16), b.astype(BF16), (((0,), (0,)), ((), ())),
                           preferred_element_type=F32)


def _split(x, n):
    parts = []
    r = x
    for i in range(n):
        p = r.astype(BF16)
        parts.append(p)
        if i + 1 < n:
            r = r - p.astype(F32)
    return parts


def _dot_x(a, b, na=2, nb=2):
    asp = _split(a, na) if na > 1 else [a.astype(BF16)]
    bsp = _split(b, nb) if nb > 1 else [b.astype(BF16)]
    acc = None
    for i, ai in enumerate(asp):
        for j, bj in enumerate(bsp):
            if i + j >= max(na, nb):
                continue
            t = jnp.dot(ai, bj, preferred_element_type=F32)
            acc = t if acc is None else acc + t
    return acc


def _dot_terms(a, b_stacked, n):
    return jnp.dot(jnp.concatenate(_split(a, n), axis=1), b_stacked, preferred_element_type=F32)


def _silu(x):
    return x * (1.0 / (1.0 + jnp.exp(-x)))


def _sigmoid(x):
    return 1.0 / (1.0 + jnp.exp(-x))


def _log1p(z):
    u = 1.0 + z
    tiny = u == 1.0
    return jnp.where(tiny, z, jnp.log(u) * (z / jnp.where(tiny, 1.0, u - 1.0)))


def _softplus(x):
    return jnp.maximum(x, 0.0) + _log1p(jnp.exp(-jnp.abs(x)))


def _rms_scale(x):
    width = x.shape[1]
    mean_w = jnp.full((width, LANES), 1.0 / width, BF16)
    ms = _dot_x(x * x, mean_w, 2, 1)
    r = lax.rsqrt(ms + EPS)
    return jnp.concatenate([r] * (width // LANES), axis=1)


def _norm_mod(x, w, sc, sh, on_mxu=False):
    if on_mxu:
        scale = _rms_scale(x)
    else:
        scale = lax.rsqrt(jnp.mean(x * x, axis=-1, keepdims=True) + EPS)
    return (x * scale) * w * (1.0 + sc) + sh


def _shift_rows(x, k, prev_tail):
    n, width = x.shape
    x3 = x.reshape(n // SUBLANES, SUBLANES, width)
    rot = pltpu.roll(x3, k, 1)
    rot_prev = jnp.concatenate([pltpu.roll(prev_tail, k, 0)[None], rot[:-1]], axis=0)
    sub = lax.broadcasted_iota(jnp.int32, x3.shape, 1)
    return jnp.where(sub >= k, rot, rot_prev).reshape(n, width)


def _ada_kernel(c_ref, w_ref, b_ref, o_ref):
    c = c_ref[...]
    cond = _silu(c)
    o_ref[0] = _dot_x(cond, w_ref[0], 3, 2) + b_ref[0]


def _ada_mods(c, ada_w, ada_b):
    depth, d, six_d = ada_w.shape
    bsz = c.shape[0]
    rows = max(SUBLANES, bsz)
    c_pad = jnp.zeros((rows, d), F32).at[:bsz].set(c)
    tn = 1536
    out = pl.pallas_call(
        _ada_kernel,
        grid=(depth, six_d // tn),
        in_specs=[pl.BlockSpec((rows, d), lambda l, j: (0, 0)),
                  pl.BlockSpec((1, d, tn), lambda l, j: (l, 0, j)),
                  pl.BlockSpec((1, 1, tn), lambda l, j: (l, 0, j))],
        out_specs=pl.BlockSpec((1, rows, tn), lambda l, j: (l, 0, j)),
        out_shape=jax.ShapeDtypeStruct((depth, rows, six_d), F32),
        compiler_params=_cparams("parallel", "parallel"),
        name="ada_mods",
    )(c_pad, ada_w, ada_b.reshape(depth, 1, six_d))
    return out[:, :bsz].reshape(depth, bsz, 6, 1, d).transpose(0, 2, 1, 3, 4)


def _t5_bucket(dist):
    max_exact = N_BUCKETS // 2
    d = np.maximum(dist, 0)
    large = max_exact + (np.log(np.maximum(d, 1) / max_exact) / math.log(MAX_DISTANCE / max_exact)
                         * (N_BUCKETS - max_exact)).astype(np.int32)
    large = np.minimum(large, N_BUCKETS - 1)
    return np.where(d < max_exact, d, large).astype(np.int32)


def _band_buckets():
    qi = np.arange(WINDOW)[:, None]
    s = np.arange(2 * WINDOW)[None, :]
    dist = qi + WINDOW - s
    in_window = (dist >= 0) & (dist < WINDOW)
    return np.where(in_window, _t5_bucket(dist), -1).astype(np.int32)


def _bias_kernel(rb_ref, bucket_ref, o_ref):
    h = pl.program_id(0)
    bucket = bucket_ref[...]
    acc = jnp.zeros(bucket.shape, F32)
    for b in range(N_BUCKETS):
        acc = jnp.where(bucket == b, rb_ref[b, h], acc)
    o_ref[0] = jnp.where(bucket < 0, NEG_BIG, acc)


def _bias_table(rel_bias):
    bucket = jnp.asarray(_band_buckets())
    out = pl.pallas_call(
        _bias_kernel,
        grid=(A_Q_HEADS,),
        in_specs=[pl.BlockSpec(memory_space=pltpu.SMEM),
                  _const_spec((WINDOW, 2 * WINDOW))],
        out_specs=pl.BlockSpec((1, WINDOW, 2 * WINDOW), lambda h: (h, 0, 0)),
        out_shape=jax.ShapeDtypeStruct((A_Q_HEADS, WINDOW, 2 * WINDOW), F32),
        compiler_params=_cparams("parallel"),
        name="attn_bias_table",
    )(rel_bias, bucket)
    return out.reshape(A_Q_HEADS // 2, 2 * WINDOW, 2 * WINDOW)


_C_QA = 0
_C_KA = _C_QA + A_Q_W
_C_VA = _C_KA + A_KV_W
_C_QKV = _C_VA + A_KV_W
_C_GATE = _C_QKV + B_QKV_W
_C_SMALL = _C_GATE + B_W
_AB_COLS = _C_SMALL + LANES


def _ab_in_weight(w_in):
    return jnp.pad(w_in, ((0, 0), (0, _AB_COLS - w_in.shape[1]))).astype(BF16)


def _dup_heads(t, low):
    swapped = pltpu.roll(t, HEAD_DIM, 1)
    return jnp.concatenate([jnp.where(low, t, swapped), jnp.where(low, swapped, t)], axis=1)


def _chunk_tril(tm):
    r = np.arange(tm)
    return ((r[:, None] >= r[None, :]) & (r[:, None] // CHUNK == r[None, :] // CHUNK)).astype(np.float32)


def _head_selector():
    e = np.zeros((B_W, LANES), np.float32)
    for h in range(B_HEADS):
        e[h * HEAD_DIM:(h + 1) * HEAD_DIM, h] = 1.0
    return e


def _in0_kernel(x_ref, nw_ref, sc_ref, sh_ref, w_ref, cw_ref, sel_ref, selt2_ref, selt3_ref, tril_ref,
                alog_ref, dtb_ref,
                qa_ref, kd_ref, vd_ref, qn_ref, kn_ref, vb_ref, gs_ref, bexp_ref, gcexp_ref,
                tail_ref, *, tiles_per_seq):
    i = pl.program_id(0)

    @pl.when(i % tiles_per_seq == 0)
    def _():
        tail_ref[...] = jnp.zeros_like(tail_ref)

    hn = _norm_mod(x_ref[...], nw_ref[...], sc_ref[0], sh_ref[0])
    proj = jnp.dot(hn.astype(BF16), w_ref[...], preferred_element_type=F32)
    tm = proj.shape[0]
    low = lax.broadcasted_iota(jnp.int32, (tm, LANES), 1) < HEAD_DIM

    small = proj[:, _C_SMALL:]
    lane = lax.broadcasted_iota(jnp.int32, small.shape, 1)
    beta = jnp.where(lane < B_HEADS, _sigmoid(small), 0.0)
    dec = pltpu.roll(small, LANES - B_HEADS, 1)
    g = jnp.where(lane < B_HEADS, -jnp.exp(alog_ref[...]) * _softplus(dec + dtb_ref[...]), 0.0)
    bexp_ref[...] = _dot_terms(beta, selt2_ref[...], 2)
    gc = _dot_x(tril_ref[...], g, 1, 3)
    gcexp_ref[...] = _dot_terms(gc, selt3_ref[...], 3)

    qa_ref[...] = proj[:, _C_QA:_C_KA].astype(BF16)
    kd_ref[...] = _dup_heads(proj[:, _C_KA:_C_VA], low).astype(BF16)
    vd_ref[...] = _dup_heads(proj[:, _C_VA:_C_QKV], low).astype(BF16)

    def conv_silu(block):
        cols = slice(block * B_W, (block + 1) * B_W)
        xq = proj[:, _C_QKV + block * B_W:_C_QKV + (block + 1) * B_W]
        tail = tail_ref[:, cols]
        cw = cw_ref[:, cols]
        y = xq * cw[B_CONV - 1:B_CONV]
        for k in range(1, B_CONV):
            y = y + _shift_rows(xq, k, tail) * cw[B_CONV - 1 - k:B_CONV - k]
        tail_ref[:, cols] = xq[tm - SUBLANES:]
        return _silu(y)

    q = conv_silu(0)
    k_ = conv_silu(1)
    ssq = _dot_x(jnp.concatenate([q * q, k_ * k_], axis=1), sel_ref[...], 2, 1)
    r = lax.rsqrt(ssq + EPS)
    q_scale = _dot_terms(r[:, :LANES], selt2_ref[...], 2)
    k_scale = _dot_terms(r[:, LANES:], selt2_ref[...], 2)
    vb_ref[...] = conv_silu(2)
    gs_ref[...] = _silu(proj[:, _C_GATE:_C_SMALL])
    qn_ref[...] = q * q_scale * (HEAD_DIM ** -0.5)
    kn_ref[...] = k_ * k_scale


def _in_proj0(x2d, nw, sc, sh, w_in, conv_w, a_log, dt_bias, seq_len):
    n, d = x2d.shape
    tm = TOKEN_TILE
    tiles_per_seq = seq_len // tm
    w = _ab_in_weight(w_in)
    hs = _head_selector()
    zeros = np.zeros_like(hs)
    sel = jnp.asarray(np.block([[hs, zeros], [zeros, hs]]), BF16)
    selt2 = jnp.asarray(np.tile(hs.T, (2, 1)), BF16)
    selt3 = jnp.asarray(np.tile(hs.T, (3, 1)), BF16)
    tril = jnp.asarray(_chunk_tril(tm), BF16)
    pad8 = lambda v: jnp.zeros((1, LANES), F32).at[0, :B_HEADS].set(v)
    row = lambda width: pl.BlockSpec((tm, width), lambda i: (i, 0))
    per_b = pl.BlockSpec((1, 1, d), lambda i: (i // tiles_per_seq, 0, 0))
    outs = pl.pallas_call(
        functools.partial(_in0_kernel, tiles_per_seq=tiles_per_seq),
        grid=(n // tm,),
        in_specs=[row(d), _const_spec((1, d)), per_b, per_b,
                  _resident_spec((d, _AB_COLS)), _const_spec((B_CONV, B_QKV_W)),
                  _const_spec(sel.shape), _const_spec(selt2.shape), _const_spec(selt3.shape),
                  _const_spec((tm, tm)), _const_spec((1, LANES)), _const_spec((1, LANES))],
        out_specs=[row(A_Q_W), row(2 * A_KV_W), row(2 * A_KV_W)] + [row(B_W)] * 6,
        out_shape=[jax.ShapeDtypeStruct((n, A_Q_W), BF16),
                   jax.ShapeDtypeStruct((n, 2 * A_KV_W), BF16),
                   jax.ShapeDtypeStruct((n, 2 * A_KV_W), BF16)]
        + [jax.ShapeDtypeStruct((n, B_W), F32)] * 6,
        scratch_shapes=[pltpu.VMEM((SUBLANES, B_QKV_W), F32)],
        compiler_params=_cparams("arbitrary"),
        name="in_proj0",
    )(x2d, nw.reshape(1, d), sc, sh, w, conv_w, sel, selt2, selt3, tril, pad8(a_log), pad8(dt_bias))
    return outs


_ATTN_BLOCKS = 2


def _attn_kernel(sink_ref, q_ref, kc_ref, vc_ref, bm_ref, o_ref, kp_ref, vp_ref, *, steps_per_seq):
    i = pl.program_id(0)
    first = (i % steps_per_seq) == 0
    w = WINDOW

    @pl.when(i == 0)
    def _():
        kp_ref[...] = jnp.zeros_like(kp_ref)
        vp_ref[...] = jnp.zeros_like(vp_ref)

    lane = lax.broadcasted_iota(jnp.int32, (w, LANES), 1)
    low = lane < HEAD_DIM
    col = lax.broadcasted_iota(jnp.int32, (2 * w, 2 * w), 1)
    row = lax.broadcasted_iota(jnp.int32, (2 * w, 1), 0)
    prev_dead = jnp.logical_and(first, col < w)
    zero = jnp.zeros((), q_ref.dtype)
    pairs = A_Q_HEADS // 2
    units = [(b, j) for b in range(_ATTN_BLOCKS) for j in range(pairs)]

    def keys(p_ref, c_ref, b, kh):
        ls = slice(kh * LANES, (kh + 1) * LANES)
        before = p_ref[:, ls] if b == 0 else c_ref[(b - 1) * w:b * w, ls]
        return jnp.concatenate([before, c_ref[b * w:(b + 1) * w, ls]], axis=0)

    kv_of = lambda j: (2 * j) // (A_Q_HEADS // A_KV_HEADS)
    qp = [q_ref[b * w:(b + 1) * w, j * LANES:(j + 1) * LANES] for b, j in units]
    qs = [jnp.concatenate([jnp.where(low, t, zero), jnp.where(low, zero, t)], axis=0) for t in qp]
    kd = [keys(kp_ref, kc_ref, b, kv_of(j)) for b, j in units]
    vd = [keys(vp_ref, vc_ref, b, kv_of(j)) for b, j in units]
    s = [lax.dot_general(a, k, (((1,), (1,)), ((), ())), preferred_element_type=F32) for a, k in zip(qs, kd)]
    s = [t * (HEAD_DIM ** -0.5) + bm_ref[j] for t, (b, j) in zip(s, units)]
    s = [jnp.where(prev_dead, NEG_BIG, t) if b == 0 else t for t, (b, j) in zip(s, units)]
    sink = [jnp.where(row < w, sink_ref[2 * j], sink_ref[2 * j + 1]) for b, j in units]
    m = [jnp.maximum(jnp.max(t, axis=-1, keepdims=True), sk) for t, sk in zip(s, sink)]
    p = [jnp.exp(t - mt) for t, mt in zip(s, m)]
    denom = [jnp.sum(t, axis=-1, keepdims=True) + jnp.exp(sk - mt) for t, sk, mt in zip(p, sink, m)]
    pv = [jnp.dot(t.astype(BF16), v, preferred_element_type=F32) / dn for t, v, dn in zip(p, vd, denom)]
    outs = [jnp.where(low, t[:w], t[w:]) for t in pv]
    for b in range(_ATTN_BLOCKS):
        o_ref[b * w:(b + 1) * w, :] = jnp.concatenate(outs[b * pairs:(b + 1) * pairs], axis=1).astype(o_ref.dtype)
    kp_ref[...] = kc_ref[(_ATTN_BLOCKS - 1) * w:, :]
    vp_ref[...] = vc_ref[(_ATTN_BLOCKS - 1) * w:, :]


def _attention(qa, kd, vd, bias_tbl, sinks, seq_len):
    n = qa.shape[0]
    w = WINDOW
    rows = _ATTN_BLOCKS * w
    steps = seq_len // rows
    cur = lambda i: (i, 0)
    return pl.pallas_call(
        functools.partial(_attn_kernel, steps_per_seq=steps),
        grid=(n // rows,),
        in_specs=[pl.BlockSpec(memory_space=pltpu.SMEM),
                  pl.BlockSpec((rows, A_Q_W), cur),
                  pl.BlockSpec((rows, 2 * A_KV_W), cur), pl.BlockSpec((rows, 2 * A_KV_W), cur),
                  _const_spec((A_Q_HEADS // 2, 2 * w, 2 * w))],
        out_specs=pl.BlockSpec((rows, A_Q_W), cur),
        out_shape=jax.ShapeDtypeStruct((n, A_Q_W), BF16),
        scratch_shapes=[pltpu.VMEM((w, 2 * A_KV_W), BF16), pltpu.VMEM((w, 2 * A_KV_W), BF16)],
        compiler_params=_cparams("arbitrary"),
        name="swa_attention",
    )(sinks, qa, kd, vd, bias_tbl)


_DN_PAIRS = B_HEADS // 2
_DN_INV_BLOCK = 16
_DN_GROUP = 4


def _block_diag(x, low):
    zero = jnp.zeros((), x.dtype)
    return jnp.concatenate([jnp.where(low, x, zero), jnp.where(low, zero, x)], axis=0)


def _dn_intra(chunks, data_refs, work_refs, consts):
    qn_ref, kn_ref, vb_ref, bexp_ref, gcexp_ref = data_refs
    u_ref, w_ref, qk_ref, qd_ref, kd_ref, egl_ref = work_refs
    low, i_idx, j_idx, ones3 = consts
    c = CHUNK
    units = [(ci, p) for ci in chunks for p in range(_DN_PAIRS)]
    where = [(slice(ci * c, (ci + 1) * c), slice(p * LANES, (p + 1) * LANES)) for ci, p in units]
    causal = i_idx >= j_idx
    strict = i_idx > j_idx
    on_diag = i_idx == j_idx
    eye = on_diag.astype(F32)
    blk_shift = int(math.log2(_DN_INV_BLOCK))
    same_blk = (i_idx >> blk_shift) == (j_idx >> blk_shift)

    q = [qn_ref[rs, ls] for rs, ls in where]
    k = [kn_ref[rs, ls] for rs, ls in where]
    v = [vb_ref[rs, ls] for rs, ls in where]
    b = [bexp_ref[rs, ls] for rs, ls in where]
    gc = [gcexp_ref[rs, ls] for rs, ls in where]

    gr = [jnp.dot(ones3, jnp.concatenate(_split(jnp.where(on_diag, t, 0.0), 3), axis=0),
                  preferred_element_type=F32) for t in gc]
    ks = [_block_diag(t.astype(BF16), low) for t in k]
    qkk = [lax.dot_general(jnp.concatenate([qt, kt], axis=0).astype(BF16), kst,
                           (((1,), (1,)), ((), ())), preferred_element_type=F32)
           for qt, kt, kst in zip(q, k, ks)]
    decay = [jnp.exp(jnp.where(causal, gct - grt, NEG_BIG)) for gct, grt in zip(gc, gr)]
    lmat = [jnp.where(strict, bt * t[c:] * dt, 0.0) for bt, t, dt in zip(b, qkk, decay)]
    qk = [jnp.where(causal, t[:c] * dt, 0.0) for t, dt in zip(qkk, decay)]

    def mm(xs, ys):
        return [_bdot(x, _block_diag(y.astype(BF16), low)) for x, y in zip(xs, ys)]

    l_diag = [jnp.where(same_blk, t, 0.0) for t in lmat]
    l_off = [t - d for t, d in zip(lmat, l_diag)]
    pw = [-t for t in l_diag]
    d_inv = [eye + t for t in pw]
    for _ in range(blk_shift - 1):
        pw = mm(pw, pw)
        d_inv = mm(d_inv, [eye + t for t in pw])
    pw = [-t for t in mm(d_inv, l_off)]
    acc = [eye + t for t in pw]
    for _ in range(int(math.log2(c // _DN_INV_BLOCK)) - 1):
        pw = mm(pw, pw)
        acc = mm(acc, [eye + t for t in pw])
    tmat = mm(acc, d_inv)

    egc = [jnp.exp(t) for t in gc]
    rhs = [jnp.concatenate([_block_diag((vt * bt).astype(BF16), low),
                            _block_diag((kt * (bt * et)).astype(BF16), low)], axis=1)
           for vt, kt, bt, et in zip(v, k, b, egc)]
    uw = [_bdot(t, r) for t, r in zip(tmat, rhs)]
    for n, (ci, p) in enumerate(units):
        g_last = gc[n][c - 1:c, :]
        u_ref[ci, p] = uw[n][:, :LANES]
        w_ref[ci, p] = uw[n][:, LANES:]
        qk_ref[ci, p] = qk[n]
        qd_ref[ci, p] = q[n] * egc[n]
        kd_ref[ci, p] = k[n] * jnp.exp(g_last - gc[n])
        egl_ref[ci, p] = jnp.broadcast_to(jnp.exp(g_last), (SUBLANES, LANES))


def _dn_scan(ci, work_refs, s_ref, gs_ref, nw, o_ref, consts):
    u_ref, w_ref, qk_ref, qd_ref, kd_ref, egl_ref = work_refs
    low, mask_bd, head_mean2 = consts
    c = CHUNK
    rows = slice(ci * c, (ci + 1) * c)
    pairs = range(_DN_PAIRS)
    s_old = [s_ref[p] for p in pairs]
    wq = [_bdot(jnp.concatenate([w_ref[ci, p], qd_ref[ci, p]], axis=0), s_old[p]) for p in pairs]
    v_new = [u_ref[ci, p] - wq[p][:c] for p in pairs]
    o = [wq[p][c:] + _bdot(qk_ref[ci, p], _block_diag(v_new[p].astype(BF16), low)) for p in pairs]
    kv = [_bdot_tn(kd_ref[ci, p], v_new[p]) for p in pairs]
    for p in pairs:
        s_ref[p] = s_old[p] * egl_ref[ci, p][0:1, :] + jnp.where(mask_bd, kv[p], 0.0)
    ms = [jnp.dot(jnp.concatenate(_split(t * t, 2), axis=1), head_mean2, preferred_element_type=F32)
          for t in o]
    for p in pairs:
        ls = slice(p * LANES, (p + 1) * LANES)
        y = (o[p] * lax.rsqrt(ms[p] + EPS)) * nw * gs_ref[rows, ls]
        o_ref[rows, ls] = y.astype(o_ref.dtype)


def _dn_kernel(qn_ref, kn_ref, vb_ref, gs_ref, bexp_ref, gcexp_ref, nw_ref, o_ref,
               s_ref, u_ref, w_ref, qk_ref, qd_ref, kd_ref, egl_ref, *, groups_per_seq):
    i = pl.program_id(0)

    @pl.when(i % groups_per_seq == 0)
    def _():
        s_ref[...] = jnp.zeros_like(s_ref)

    c = CHUNK
    tm = o_ref.shape[0]
    n_chunks = tm // c
    lane = lax.broadcasted_iota(jnp.int32, (c, LANES), 1)
    low = lane < HEAD_DIM
    i_idx = lax.broadcasted_iota(jnp.int32, (c, LANES), 0)
    j_idx = lane & (c - 1)
    ones3 = jnp.ones((c, 3 * c), BF16)
    rb = lax.broadcasted_iota(jnp.int32, (LANES, LANES), 0)
    cb = lax.broadcasted_iota(jnp.int32, (LANES, LANES), 1)
    mask_bd = (rb < HEAD_DIM) == (cb < HEAD_DIM)
    head_mean = jnp.where(mask_bd, 1.0 / HEAD_DIM, 0.0).astype(BF16)
    head_mean2 = jnp.concatenate([head_mean, head_mean], axis=0)
    data_refs = (qn_ref, kn_ref, vb_ref, bexp_ref, gcexp_ref)
    work_refs = (u_ref, w_ref, qk_ref, qd_ref, kd_ref, egl_ref)
    intra_consts = (low, i_idx, j_idx, ones3)
    scan_consts = (low, mask_bd, head_mean2)
    nw = nw_ref[...]

    groups = [list(range(s, s + _DN_GROUP)) for s in range(0, n_chunks, _DN_GROUP)]
    _dn_intra(groups[0], data_refs, work_refs, intra_consts)
    for j, grp in enumerate(groups):
        if j + 1 < len(groups):
            _dn_intra(groups[j + 1], data_refs, work_refs, intra_consts)
        for ci in grp:
            _dn_scan(ci, work_refs, s_ref, gs_ref, nw, o_ref, scan_consts)


def _deltanet(qn, kn, vb, gs, bexp, gcexp, norm_w, seq_len):
    n = qn.shape[0]
    tm = TOKEN_TILE
    nw2 = jnp.concatenate([norm_w, norm_w]).reshape(1, LANES)
    row = lambda width: pl.BlockSpec((tm, width), lambda i: (i, 0))
    return pl.pallas_call(
        functools.partial(_dn_kernel, groups_per_seq=seq_len // tm),
        grid=(n // tm,),
        in_specs=[row(B_W)] * 6 + [_const_spec((1, LANES))],
        out_specs=row(B_W),
        out_shape=jax.ShapeDtypeStruct((n, B_W), BF16),
        scratch_shapes=[pltpu.VMEM((_DN_PAIRS, LANES, LANES), F32)]
        + [pltpu.VMEM((tm // CHUNK, _DN_PAIRS, CHUNK, LANES), F32)] * 5
        + [pltpu.VMEM((tm // CHUNK, _DN_PAIRS, SUBLANES, LANES), F32)],
        compiler_params=_cparams("arbitrary"),
        name="gated_deltanet",
    )(qn, kn, vb, gs, bexp, gcexp, nw2)


def _resident_spec(shape):
    nd = len(shape)
    return pl.BlockSpec(shape, lambda *_: (0,) * nd, pipeline_mode=pl.Buffered(1))


def _mid0_kernel(attn_ref, dn_ref, x_ref, wo_ref, g1_ref, nw_ref, sc_ref, sh_ref, g2_ref,
                 wg_ref, wu_ref, wd_ref, o_ref):
    mix = (jnp.dot(attn_ref[...], wo_ref[:A_Q_W], preferred_element_type=F32)
           + jnp.dot(dn_ref[...], wo_ref[A_Q_W:], preferred_element_type=F32))
    x1 = x_ref[...] + g1_ref[0] * mix
    hn = _norm_mod(x1, nw_ref[...], sc_ref[0], sh_ref[0]).astype(BF16)
    hg = jnp.dot(hn, wg_ref[...], preferred_element_type=F32)
    hu = jnp.dot(hn, wu_ref[...], preferred_element_type=F32)
    act = (_silu(hg) * hu).astype(BF16)
    o_ref[...] = x1 + g2_ref[0] * jnp.dot(act, wd_ref[...], preferred_element_type=F32)


def _mid0(attn, dn, x2d, w_out, g1, nw, sc, sh, g2, wg, wu, wd, seq_len):
    n, d = x2d.shape
    tm = TOKEN_TILE
    tps = seq_len // tm
    row = lambda width: pl.BlockSpec((tm, width), lambda i: (i, 0))
    per_b = pl.BlockSpec((1, 1, d), lambda i: (i // tps, 0, 0))
    return pl.pallas_call(
        _mid0_kernel,
        grid=(n // tm,),
        in_specs=[row(A_Q_W), row(B_W), row(d), _resident_spec(w_out.shape), per_b,
                  _const_spec((1, d)), per_b, per_b, per_b,
                  _resident_spec(wg.shape), _resident_spec(wu.shape), _resident_spec(wd.shape)],
        out_specs=row(d),
        out_shape=jax.ShapeDtypeStruct((n, d), F32),
        compiler_params=_cparams("parallel"),
        name="out_proj0_swiglu",
    )(attn, dn, x2d, w_out.astype(BF16), g1, nw.reshape(1, d), sc, sh, g2,
      wg.astype(BF16), wu.astype(BF16), wd.astype(BF16))


def _gelu_tanh(x):
    return 0.5 * x * (1.0 + jnp.tanh(math.sqrt(2.0 / math.pi) * (x + 0.044715 * (x * x * x))))


def _linear_scan(a, b, h0):
    n, width = a.shape
    groups = n // SUBLANES
    a = a.reshape(groups, SUBLANES, width)
    b = b.reshape(groups, SUBLANES, width)
    in_group = lax.broadcasted_iota(jnp.int32, a.shape, 1)
    s = 1
    while s < SUBLANES:
        a_sh = pltpu.roll(a, s, 1)
        b_sh = pltpu.roll(b, s, 1)
        valid = in_group >= s
        b = jnp.where(valid, a * b_sh + b, b)
        a = jnp.where(valid, a * a_sh, a)
        s *= 2
    carry = jnp.broadcast_to(h0, (SUBLANES, width))
    out = []
    for g in range(groups):
        hg = a[g] * carry + b[g]
        out.append(hg)
        carry = jnp.broadcast_to(hg[SUBLANES - 1:SUBLANES, :], hg.shape)
    return jnp.concatenate(out, axis=0)


def _mix1_tile(x, nw, sc, sh, w_ref, cw_ref, cb_ref, ga_ref, gab_ref, gx_ref, gxb_ref, lam_ref, sw_ref,
               tail_c_ref, tail_d_ref, h_ref):
    hn = _norm_mod(x, nw, sc, sh, on_mxu=True).astype(BF16)
    proj = jnp.dot(hn, w_ref[...], preferred_element_type=F32)
    w_l = LRU_WIDTH
    xc_in = proj[:, :w_l]
    yc = proj[:, w_l:2 * w_l]
    bd = proj[:, 2 * w_l:2 * w_l + SC_WIDTH]
    cd = proj[:, 2 * w_l + SC_WIDTH:2 * w_l + 2 * SC_WIDTH]
    hd = proj[:, 2 * w_l + 2 * SC_WIDTH:]
    tm = xc_in.shape[0]

    kc = cw_ref.shape[0]
    tail = tail_c_ref[...]
    cw = cw_ref[...]
    xc = xc_in * cw[kc - 1:kc] + cb_ref[...]
    for k in range(1, kc):
        xc = xc + _shift_rows(xc_in, k, tail) * cw[kc - 1 - k:kc - k]
    tail_c_ref[...] = xc_in[tm - SUBLANES:]

    xb = xc.astype(BF16)
    gw = ga_ref.shape[1]
    ra, ri = [], []
    for p in range(ga_ref.shape[0]):
        xin = xb[:, p * gw:(p + 1) * gw]
        ra.append(jnp.dot(xin, ga_ref[p], preferred_element_type=F32))
        ri.append(jnp.dot(xin, gx_ref[p], preferred_element_type=F32))
    r = _sigmoid(jnp.concatenate(ra, axis=1) + gab_ref[...])
    ig = _sigmoid(jnp.concatenate(ri, axis=1) + gxb_ref[...])
    log_a = (-LRU_C) * r * _softplus(-lam_ref[...])
    a = jnp.exp(log_a)
    one_m_a2 = -jnp.tanh(log_a) * (a * a + 1.0)
    root = jnp.where(one_m_a2 > 0.0, one_m_a2 * lax.rsqrt(one_m_a2), 0.0)
    b = root * (ig * xc)
    h = _linear_scan(a, b, h_ref[0:1, :])
    h_ref[...] = jnp.broadcast_to(h[tm - 1:tm, :], h_ref.shape)
    yc_out = h * _gelu_tanh(yc)

    ks = sw_ref.shape[0]
    ch = cd * hd
    tail_d = tail_d_ref[...]
    sw = sw_ref[...]
    conv = ch * sw[ks - 1:ks]
    for k in range(1, ks):
        conv = conv + _shift_rows(ch, k, tail_d) * sw[ks - 1 - k:ks - k]
    tail_d_ref[...] = ch[tm - SUBLANES:]
    return jnp.concatenate([yc_out, bd * conv], axis=1)


def _pair_block_diag(gw):
    nb, bw, _ = gw.shape
    g2 = gw.reshape(nb // 2, 2, bw, bw)
    z = jnp.zeros((nb // 2, bw, bw), gw.dtype)
    top = jnp.concatenate([g2[:, 0], z], axis=2)
    bot = jnp.concatenate([z, g2[:, 1]], axis=2)
    return jnp.concatenate([top, bot], axis=1).astype(BF16)


def _route_tile(cat, x, wo_ref, g1, nw, sc, sh, rw_ref, rb_ref, carry_ref):
    x3 = x + g1 * jnp.dot(cat, wo_ref[...], preferred_element_type=F32)
    hn = _norm_mod(x3, nw, sc, sh)
    tm = hn.shape[0]
    lane = lax.broadcasted_iota(jnp.int32, (tm, LANES), 1)
    h_hi, h_lo = _split(hn, 2)
    both = jnp.dot(h_hi, rw_ref[...], preferred_element_type=F32)
    logits = (both[:, :LANES] + both[:, LANES:]
              + jnp.dot(h_lo, rw_ref[:, :LANES], preferred_element_type=F32) + rb_ref[...])
    lg = jnp.where(lane < N_EXPERTS, logits, NEG_BIG)
    m1 = jnp.max(lg, axis=1, keepdims=True)
    i1 = jnp.min(jnp.where(lg == m1, lane, LANES), axis=1, keepdims=True)
    lg2 = jnp.where(lane == i1, NEG_BIG, lg)
    m2 = jnp.max(lg2, axis=1, keepdims=True)
    i2 = jnp.min(jnp.where(lg2 == m2, lane, LANES), axis=1, keepdims=True)
    e2 = jnp.exp(m2 - m1)
    w1 = 1.0 / (1.0 + e2)
    w2 = e2 / (1.0 + e2)

    hit1 = lane == i1
    hit2 = lane == i2
    sel = jnp.logical_or(hit1, hit2).astype(F32)
    r_i = lax.broadcasted_iota(jnp.int32, (tm, tm), 0)
    c_i = lax.broadcasted_iota(jnp.int32, (tm, tm), 1)
    tril = (r_i >= c_i).astype(BF16)
    incl = jnp.dot(tril, sel.astype(BF16), preferred_element_type=F32)
    carry = carry_ref[0:1, :]
    excl = incl - sel + carry
    r1 = jnp.sum(jnp.where(hit1, excl, 0.0), axis=1, keepdims=True)
    r2 = jnp.sum(jnp.where(hit2, excl, 0.0), axis=1, keepdims=True)
    total = carry + incl[tm - 1:tm, :]
    carry_ref[...] = jnp.broadcast_to(total, carry_ref.shape)

    meta = jnp.where(lane == 0, i1, 0)
    meta = jnp.where(lane == 1, i2, meta)
    meta = jnp.where(lane == 2, r1.astype(jnp.int32), meta)
    meta = jnp.where(lane == 3, r2.astype(jnp.int32), meta)
    wt = jnp.where(lane == 0, w1, jnp.where(lane == 1, w2, 0.0))
    return x3, hn, meta, wt, carry, total


def _mix1_kernel(x_ref, nw_ref, sc_ref, sh_ref, w_ref, cw_ref, cb_ref, ga_ref, gab_ref, gx_ref, gxb_ref,
                 lam_ref, sw_ref, o_ref, tail_c_ref, tail_d_ref, h_ref, *, tiles_per_seq):
    i = pl.program_id(0)

    @pl.when(i % tiles_per_seq == 0)
    def _():
        tail_c_ref[...] = jnp.zeros_like(tail_c_ref)
        tail_d_ref[...] = jnp.zeros_like(tail_d_ref)
        h_ref[...] = jnp.zeros_like(h_ref)

    cat = _mix1_tile(x_ref[...], nw_ref[...], sc_ref[0], sh_ref[0], w_ref, cw_ref, cb_ref, ga_ref,
                     gab_ref, gx_ref, gxb_ref, lam_ref, sw_ref, tail_c_ref, tail_d_ref, h_ref)
    o_ref[...] = cat.astype(o_ref.dtype)


def _mix1(x2d, nw, sc, sh, w_in, conv_w, conv_b, ga_w, ga_b, gx_w, gx_b, lam, sconv_w, seq_len):
    n, d = x2d.shape
    tm = TOKEN_TILE
    tps = seq_len // tm
    cd_in = w_in.shape[1]
    cd_out = LRU_WIDTH + SC_WIDTH
    row = lambda width: pl.BlockSpec((tm, width), lambda i: (i, 0))
    per_b = pl.BlockSpec((1, 1, d), lambda i: (i // tps, 0, 0))
    ga = _pair_block_diag(ga_w)
    gx = _pair_block_diag(gx_w)
    vec = lambda v: v.reshape(1, -1)
    return pl.pallas_call(
        functools.partial(_mix1_kernel, tiles_per_seq=tps),
        grid=(n // tm,),
        in_specs=[row(d), _const_spec((1, d)), per_b, per_b, _resident_spec((d, cd_in)),
                  _const_spec(conv_w.shape), _const_spec((1, LRU_WIDTH)),
                  _const_spec(ga.shape), _const_spec((1, LRU_WIDTH)),
                  _const_spec(gx.shape), _const_spec((1, LRU_WIDTH)),
                  _const_spec((1, LRU_WIDTH)), _const_spec(sconv_w.shape)],
        out_specs=row(cd_out),
        out_shape=jax.ShapeDtypeStruct((n, cd_out), BF16),
        scratch_shapes=[pltpu.VMEM((SUBLANES, LRU_WIDTH), F32), pltpu.VMEM((SUBLANES, SC_WIDTH), F32),
                        pltpu.VMEM((SUBLANES, LRU_WIDTH), F32)],
        compiler_params=_cparams("arbitrary"),
        name="rglru_shortconv_mixer",
    )(x2d, vec(nw), sc, sh, w_in.astype(BF16), conv_w, vec(conv_b), ga, vec(ga_b), gx, vec(gx_b),
      vec(lam), sconv_w)


def _route_kernel(cat_ref, x_ref, wo_ref, g1_ref, nw_ref, sc_ref, sh_ref, rw_ref, rb_ref,
                  x3_ref, hn_ref, metat_ref, meta_ref, wt_ref, base_ref, cnt_ref, carry_ref):
    @pl.when(pl.program_id(0) == 0)
    def _():
        carry_ref[...] = jnp.zeros_like(carry_ref)

    x3, hn, meta, wt, before, total = _route_tile(cat_ref[...], x_ref[...], wo_ref, g1_ref[0], nw_ref[...],
                                                  sc_ref[0], sh_ref[0], rw_ref, rb_ref, carry_ref)
    x3_ref[...] = x3
    hn_ref[...] = hn
    meta_ref[...] = meta
    metat_ref[...] = jnp.transpose(meta.astype(F32))[:SUBLANES].astype(jnp.int32)
    wt_ref[...] = wt
    base_ref[0] = jnp.broadcast_to(before, base_ref.shape[1:]).astype(jnp.int32)
    cnt_ref[...] = jnp.broadcast_to(total, cnt_ref.shape).astype(jnp.int32)


def _route(cat, x2d, w_out, g1, nw, sc, sh, router_w, router_b, seq_len):
    n, d = x2d.shape
    tm = TOKEN_TILE
    tps = seq_len // tm
    row = lambda width: pl.BlockSpec((tm, width), lambda i: (i, 0))
    per_b = pl.BlockSpec((1, 1, d), lambda i: (i // tps, 0, 0))
    rw = jnp.zeros((d, LANES), F32).at[:, :N_EXPERTS].set(router_w)
    rw_hi = rw.astype(BF16)
    rw = jnp.concatenate([rw_hi, (rw - rw_hi.astype(F32)).astype(BF16)], axis=1)
    rb = jnp.zeros((1, LANES), F32).at[0, :N_EXPERTS].set(router_b)
    return pl.pallas_call(
        _route_kernel,
        grid=(n // tm,),
        in_specs=[row(cat.shape[1]), row(d), _resident_spec(w_out.shape), per_b, _const_spec((1, d)),
                  per_b, per_b, _const_spec((d, 2 * LANES)), _const_spec((1, LANES))],
        out_specs=[row(d), row(d), pl.BlockSpec((SUBLANES, tm), lambda i: (0, i)), row(LANES), row(LANES),
                   pl.BlockSpec((1, SUBLANES, LANES), lambda i: (i, 0, 0)), _const_spec((SUBLANES, LANES))],
        out_shape=[jax.ShapeDtypeStruct((n, d), F32), jax.ShapeDtypeStruct((n, d), F32),
                   jax.ShapeDtypeStruct((SUBLANES, n), jnp.int32), jax.ShapeDtypeStruct((n, LANES), jnp.int32),
                   jax.ShapeDtypeStruct((n, LANES), F32),
                   jax.ShapeDtypeStruct((n // tm, SUBLANES, LANES), jnp.int32),
                   jax.ShapeDtypeStruct((SUBLANES, LANES), jnp.int32)],
        scratch_shapes=[pltpu.VMEM((SUBLANES, LANES), F32)],
        compiler_params=_cparams("arbitrary"),
        name="out_proj1_router",
    )(cat, x2d, w_out.astype(BF16), g1, nw.reshape(1, d), sc, sh, rw, rb)


def _local_rows(tr):
    return 2 * tr + N_EXPERTS * SUBLANES


def _xs_rows(n):
    worst = 2 * n + (n // TOKEN_TILE) * N_EXPERTS * (SUBLANES - 1)
    return (-(-worst // MOE_TILE) + N_EXPERTS) * MOE_TILE


def _local_pos(e_k, r_k, delta_ref, tile):
    shift = jnp.zeros_like(r_k)
    for e in range(N_EXPERTS):
        shift = jnp.where(e_k == e, delta_ref[tile * N_EXPERTS + e], shift)
    return r_k + shift


_RUN_PIECES = (64, 16, SUBLANES)


def _for_each_group(tile, lstart_ref, run_ref, gstart_ref, fn):
    for e in range(N_EXPERTS):
        k = tile * N_EXPERTS + e
        l_start = lstart_ref[k]
        g_start = gstart_ref[k]
        done = 0
        for rows in _RUN_PIECES:
            shift = int(math.log2(rows))
            count = (run_ref[k] - done) >> shift

            def piece(g, c, rows=rows, done=done, l_start=l_start, g_start=g_start, prio=e % 2):
                off = done + g * rows
                fn(pl.multiple_of(l_start + off, SUBLANES), pl.multiple_of(g_start + off, SUBLANES), rows, prio)
                return c

            lax.fori_loop(0, count, piece, 0)
            done = done + (count << shift)


def _zero_fill_gaps(gap_ref, used_ref, xs_ref, zero_ref, sem):
    zero_ref[...] = jnp.zeros_like(zero_ref)
    zr = zero_ref.shape[0]
    per_tile = MOE_TILE // zr
    shift = int(math.log2(SUBLANES))

    def gap_copy(e, g):
        row = pl.multiple_of(gap_ref[e] + g * SUBLANES, SUBLANES)
        return pltpu.make_async_copy(zero_ref.at[pl.ds(0, SUBLANES)], xs_ref.at[pl.ds(row, SUBLANES)], sem)

    def tile_copy(k):
        row = pl.multiple_of(k * zr, zr)
        return pltpu.make_async_copy(zero_ref, xs_ref.at[pl.ds(row, zr)], sem)

    def both(op):
        for e in range(N_EXPERTS):
            lax.fori_loop(0, gap_ref[N_EXPERTS + e] >> shift, lambda g, c, e=e: (op(gap_copy(e, g)), c)[1], 0)
        lax.fori_loop(used_ref[0] * per_tile, (xs_ref.shape[0] // MOE_TILE) * per_tile,
                      lambda k, c: (op(tile_copy(k)), c)[1], 0)

    both(lambda cp: cp.start())
    both(lambda cp: cp.wait())


def _dispatch_kernel(delta_ref, lstart_ref, run_ref, gstart_ref, gap_ref, used_ref, hn_ref, meta_ref, xs_ref,
                     sbuf_ref, zero_ref, sem):
    j = pl.program_id(0)
    tr = hn_ref.shape[0]
    lrows = sbuf_ref.shape[1]
    slot = lax.rem(j, 2)

    @pl.when(j == 0)
    def _():
        _zero_fill_gaps(gap_ref, used_ref, xs_ref, zero_ref, sem.at[0])

    meta = meta_ref[...]
    lp1 = _local_pos(meta[0:1], meta[2:3], delta_ref, j)
    lp2 = _local_pos(meta[1:2], meta[3:4], delta_ref, j)
    r_idx = lax.broadcasted_iota(jnp.int32, (lrows, tr), 0)
    onehot = jnp.logical_or(r_idx == lp1, r_idx == lp2).astype(BF16)
    sbuf_ref[slot] = jnp.dot(onehot, hn_ref[...].astype(BF16), preferred_element_type=F32)

    def group_copy(buf, local_row, xs_row, rows):
        return pltpu.make_async_copy(sbuf_ref.at[buf, pl.ds(local_row, rows)],
                                     xs_ref.at[pl.ds(xs_row, rows)], sem.at[buf])

    def drain(tile, buf):
        _for_each_group(tile, lstart_ref, run_ref, gstart_ref,
                        lambda lr, xr, rows, prio: group_copy(buf, lr, xr, rows).wait())

    _for_each_group(j, lstart_ref, run_ref, gstart_ref,
                    lambda lr, xr, rows, prio: group_copy(slot, lr, xr, rows).start(priority=prio))

    @pl.when(j > 0)
    def _():
        drain(j - 1, 1 - slot)

    @pl.when(j == pl.num_programs(0) - 1)
    def _():
        drain(j, slot)


def _dispatch(hn, meta_t, tables, gaps, used_tiles):
    n, d = hn.shape
    tr = TOKEN_TILE
    lrows = _local_rows(tr)
    return pl.pallas_call(
        _dispatch_kernel,
        grid_spec=pltpu.PrefetchScalarGridSpec(
            num_scalar_prefetch=6,
            grid=(n // tr,),
            in_specs=[pl.BlockSpec((tr, d), lambda j, *_: (j, 0)),
                      pl.BlockSpec((SUBLANES, tr), lambda j, *_: (0, j))],
            out_specs=pl.BlockSpec(memory_space=pl.ANY),
            scratch_shapes=[pltpu.VMEM((2, lrows, d), F32), pltpu.VMEM((MOE_SUB, d), F32),
                            pltpu.SemaphoreType.DMA((2,))]),
        out_shape=jax.ShapeDtypeStruct((_xs_rows(n), d), F32),
        compiler_params=_cparams("arbitrary"),
        name="moe_dispatch",
    )(*tables, gaps, used_tiles, hn, meta_t)


def _moe_kernel(te_ref, hi_ref, x_ref, wg_ref, wu_ref, wd_ref, o_ref, xb_ref):
    w = pl.program_id(0)
    f = pl.program_id(1)
    tm = x_ref.shape[0]
    sub = MOE_SUB
    sub_shift = int(math.log2(sub))
    hi = hi_ref[w]

    def swiglu_part(xb, wg, wu, wd):
        hg = jnp.dot(xb, wg, preferred_element_type=F32)
        hu = jnp.dot(xb, wu, preferred_element_type=F32)
        act = (_silu(hg) * hu).astype(BF16)
        return jnp.dot(act, wd, preferred_element_type=F32)

    @pl.when(jnp.logical_and(f == 0, hi > 0))
    def _():
        row = lax.broadcasted_iota(jnp.int32, (tm, 1), 0)
        xb_ref[...] = jnp.where(row < hi, x_ref[...], 0.0).astype(BF16)

    @pl.when(hi == tm)
    def _():
        part = swiglu_part(xb_ref[...], wg_ref[0].astype(BF16), wu_ref[0].astype(BF16),
                           wd_ref[0].astype(BF16))

        @pl.when(f == 0)
        def _():
            o_ref[...] = part

        @pl.when(f != 0)
        def _():
            o_ref[...] += part

    @pl.when(hi < tm)
    def _():
        @pl.when(f == 0)
        def _():
            o_ref[...] = jnp.zeros_like(o_ref)

        def sub_block(s, carry):
            rows = pl.ds(pl.multiple_of(s * sub, sub), sub)
            o_ref[rows, :] += swiglu_part(xb_ref[rows, :], wg_ref[0].astype(BF16),
                                          wu_ref[0].astype(BF16), wd_ref[0].astype(BF16))
            return carry

        lax.fori_loop(0, (hi + sub - 1) >> sub_shift, sub_block, 0)


def _moe_ffn(xs, tile_expert, tile_rows, wg, wu, wd):
    rows, d = xs.shape
    tm = MOE_TILE
    tf = MOE_FF_TILE
    nf = wg.shape[2] // tf

    def f_idx(f, hi):
        v = (hi > 0).astype(jnp.int32)
        return f * v + (nf - 1) * (1 - v)

    return pl.pallas_call(
        _moe_kernel,
        grid_spec=pltpu.PrefetchScalarGridSpec(
            num_scalar_prefetch=2,
            grid=(rows // tm, nf),
            in_specs=[pl.BlockSpec((tm, d), lambda w, f, te, hi: (jnp.where(hi[w] > 0, w, 0), 0)),
                      pl.BlockSpec((1, d, tf), lambda w, f, te, hi: (te[w], 0, f_idx(f, hi[w]))),
                      pl.BlockSpec((1, d, tf), lambda w, f, te, hi: (te[w], 0, f_idx(f, hi[w]))),
                      pl.BlockSpec((1, tf, d), lambda w, f, te, hi: (te[w], f_idx(f, hi[w]), 0))],
            out_specs=pl.BlockSpec((tm, d), lambda w, f, te, hi: (w, 0)),
            scratch_shapes=[pltpu.VMEM((tm, d), BF16)]),
        out_shape=jax.ShapeDtypeStruct((rows, d), F32),
        compiler_params=_cparams("arbitrary", "arbitrary"),
        name="moe_expert_swiglu",
    )(tile_expert, tile_rows, xs, wg, wu, wd)


def _combine_kernel(delta_ref, lstart_ref, run_ref, gstart_ref, ys_ref, x_ref, meta_ref, wt_ref, g2_ref, fw_ref,
                    o_ref, ybuf_ref, sem):
    j = pl.program_id(0)
    n_tiles = pl.num_programs(0)
    tr = x_ref.shape[0]
    lrows = ybuf_ref.shape[1]
    slot = lax.rem(j, 2)

    def group_copy(buf, local_row, xs_row, rows):
        return pltpu.make_async_copy(ys_ref.at[pl.ds(xs_row, rows)],
                                     ybuf_ref.at[buf, pl.ds(local_row, rows)], sem.at[buf])

    def fetch(tile, buf):
        ybuf_ref[buf, 2 * tr:, :] = jnp.zeros((lrows - 2 * tr, ybuf_ref.shape[2]), F32)
        _for_each_group(tile, lstart_ref, run_ref, gstart_ref,
                        lambda lr, xr, rows, prio: group_copy(buf, lr, xr, rows).start(priority=prio))

    @pl.when(j == 0)
    def _():
        fetch(0, 0)

    _for_each_group(j, lstart_ref, run_ref, gstart_ref,
                    lambda lr, xr, rows, prio: group_copy(slot, lr, xr, rows).wait())

    @pl.when(j + 1 < n_tiles)
    def _():
        fetch(j + 1, 1 - slot)

    meta = meta_ref[...]
    wt = wt_ref[...]
    lp1 = _local_pos(meta[:, 0:1], meta[:, 2:3], delta_ref, j)
    lp2 = _local_pos(meta[:, 1:2], meta[:, 3:4], delta_ref, j)
    l_idx = lax.broadcasted_iota(jnp.int32, (tr, lrows), 1)
    pick = jnp.where(l_idx == lp1, wt[:, 0:1], 0.0) + jnp.where(l_idx == lp2, wt[:, 1:2], 0.0)
    ffn = _bdot(pick, ybuf_ref[slot])
    x4 = x_ref[...] + g2_ref[0] * ffn
    o_ref[...] = (x4 * _rms_scale(x4)) * fw_ref[...]


def _combine(ys, tables, x3, meta, wt, g2, final_w, seq_len):
    n, d = x3.shape
    tr = TOKEN_TILE
    tps = seq_len // tr
    lrows = -(-_local_rows(tr) // LANES) * LANES
    return pl.pallas_call(
        _combine_kernel,
        grid_spec=pltpu.PrefetchScalarGridSpec(
            num_scalar_prefetch=4,
            grid=(n // tr,),
            in_specs=[pl.BlockSpec(memory_space=pl.ANY),
                      pl.BlockSpec((tr, d), lambda j, *_: (j, 0)),
                      pl.BlockSpec((tr, LANES), lambda j, *_: (j, 0)),
                      pl.BlockSpec((tr, LANES), lambda j, *_: (j, 0)),
                      pl.BlockSpec((1, 1, d), lambda j, *_: (j // tps, 0, 0)),
                      pl.BlockSpec((1, d), lambda j, *_: (0, 0))],
            out_specs=pl.BlockSpec((tr, d), lambda j, *_: (j, 0)),
            scratch_shapes=[pltpu.VMEM((2, lrows, d), F32), pltpu.SemaphoreType.DMA((2,))]),
        out_shape=jax.ShapeDtypeStruct((n, d), F32),
        compiler_params=_cparams("arbitrary"),
        name="moe_combine_final_norm",
    )(*tables, ys, x3, meta, wt, g2, final_w.reshape(1, d))


def _moe_tables(tile_base, counts, n_tokens):
    i32 = lambda t: t.astype(jnp.int32)
    tm = MOE_TILE
    before = tile_base[:, 0, :N_EXPERTS]
    total = counts[0, :N_EXPERTS]
    run = jnp.concatenate([before[1:], total[None]], axis=0) - before
    run = (run + SUBLANES - 1) // SUBLANES * SUBLANES
    l_end = jnp.cumsum(run, axis=1)
    l_start = l_end - run
    g_size = jnp.sum(run, axis=0)
    g_tiles = (g_size + tm - 1) // tm
    tile_end = jnp.cumsum(g_tiles)
    g_off = (tile_end - g_tiles) * tm
    g_end = g_off + g_size
    g_start = g_off[None, :] + jnp.cumsum(run, axis=0) - run
    delta = l_start - before
    gaps = jnp.concatenate([g_end, tile_end * tm - g_end])
    used_tiles = tile_end[-1:]

    w = jnp.arange(_xs_rows(n_tokens) // tm, dtype=jnp.int32)
    te = jnp.minimum(jnp.sum((tile_end[None, :] <= w[:, None]).astype(jnp.int32), axis=1), N_EXPERTS - 1)
    rows = jnp.where(w < tile_end[-1], jnp.clip(jnp.take(g_end, te) - w * tm, 0, tm), 0)
    flat = lambda t: i32(t).reshape(-1)
    return (flat(delta), flat(l_start), flat(run), flat(g_start)), i32(gaps), i32(used_tiles), i32(te), i32(rows)


def kernel(x, c, rel_bias, ada_w, ada_b, norm_mix_w, norm_ffn_w, final_norm_w, ab_w_in, attn_sinks,
           dn_conv_w, dn_a_log, dn_dt_bias, dn_norm_w, ab_w_out, ffn_w_gate, ffn_w_up, ffn_w_down,
           cd_w_in, lru_conv_w, lru_conv_b, lru_gate_a_w, lru_gate_a_b, lru_gate_x_w, lru_gate_x_b,
           lru_lambda, sconv_w, cd_w_out, moe_router_w, moe_router_b, moe_w_gate, moe_w_up, moe_w_down):
    bsz, seq_len, d = x.shape
    n = bsz * seq_len
    x2d = x.reshape(n, d)
    mods = _ada_mods(c, ada_w, ada_b)

    sh1, sc1, g1, sh2, sc2, g2 = (mods[0, k] for k in range(6))
    qa, kd, vd, qn, kn, vb, gs, bexp, gcexp = _in_proj0(
        x2d, norm_mix_w[0], sc1, sh1, ab_w_in[0], dn_conv_w[0], dn_a_log[0], dn_dt_bias[0], seq_len)
    attn = _attention(qa, kd, vd, _bias_table(rel_bias), attn_sinks[0], seq_len)
    dn = _deltanet(qn, kn, vb, gs, bexp, gcexp, dn_norm_w[0], seq_len)
    x2 = _mid0(attn, dn, x2d, ab_w_out[0], g1, norm_ffn_w[0], sc2, sh2, g2,
               ffn_w_gate[0], ffn_w_up[0], ffn_w_down[0], seq_len)

    sh1, sc1, g1, sh2, sc2, g2 = (mods[1, k] for k in range(6))
    cat = _mix1(x2, norm_mix_w[1], sc1, sh1, cd_w_in[0], lru_conv_w[0], lru_conv_b[0],
                lru_gate_a_w[0], lru_gate_a_b[0], lru_gate_x_w[0], lru_gate_x_b[0],
                lru_lambda[0], sconv_w[0], seq_len)
    x3, hn4, meta_t, meta, wt, tile_base, counts = _route(
        cat, x2, cd_w_out[0], g1, norm_ffn_w[1], sc2, sh2, moe_router_w[0], moe_router_b[0], seq_len)
    tables, gaps, used_tiles, tile_expert, tile_rows = _moe_tables(tile_base, counts, n)
    xs = _dispatch(hn4, meta_t, tables, gaps, used_tiles)
    ys = _moe_ffn(xs, tile_expert, tile_rows, moe_w_gate[0], moe_w_up[0], moe_w_down[0])
    out = _combine(ys, tables, x3, meta, wt, g2, final_norm_w, seq_len)
    return out.reshape(bsz, seq_len, d)
```
